```python
import jax
import jax.numpy as jnp
from jax import lax
import numpy as np

D_MODEL = 2048
BATCH = 8
SEQ = 8192
DEPTH = 2

GRID_W = 64
CTX_LEN = 256
N_EVEN = (DEPTH + 1) // 2
N_ODD = DEPTH // 2
EPS = 1e-6
D_FF = 5632
FFN_STEP = 0.5
LRU_WIDTH = D_MODEL // 2
LRU_BLOCKS = 8
LRU_BLOCK = LRU_WIDTH // LRU_BLOCKS
LRU_C = 8.0
CONV_W = 4
CONV_LEFT = 2
RET_HEADS = 4
RET_DK = 256
RET_DV = 256
RET_QK = RET_HEADS * RET_DK
RET_V = RET_HEADS * RET_DV
RET_CHUNK = 128
RET_THETA = 10000.0
EV_IN = 2 * LRU_WIDTH + 2 * RET_QK + 2 * RET_V
EV_MIX = LRU_WIDTH + RET_V
POOL_WINDOWS = (2, 4, 8, 16)
POOL_GROUP = 128
POOL_WIDTH = POOL_GROUP * len(POOL_WINDOWS)
ATT_HEADS = 12
KV_HEADS = 4
GROUP = ATT_HEADS // KV_HEADS
HEAD_DIM = 128
ATT_QW = ATT_HEADS * HEAD_DIM
ATT_KVW = KV_HEADS * HEAD_DIM
Q_BLOCK = 128
ROPE_THETA = 10000.0
OD_IN = POOL_WIDTH + ATT_QW + 2 * ATT_KVW
OD_MIX = POOL_WIDTH + ATT_QW

kernel_name = "hybrid_lru_retention_pool_gqa_diffusion_block"


def rmsnorm(x, g):
    xf = x.astype(jnp.float32)
    y = xf * lax.rsqrt(jnp.mean(xf * xf, axis=-1, keepdims=True) + EPS)
    return (y * g.astype(jnp.float32)).astype(x.dtype)


def modulate(x, g, shift, scale):
    return rmsnorm(x, g) * (1 + scale) + shift


def residual(x, y, g_post, gate, w):
    return x + w * gate * rmsnorm(y, g_post)


def swiglu(h, w_gate, w_up, w_down):
    return (jax.nn.silu(h @ w_gate) * (h @ w_up)) @ w_down


def split_heads(t, h, d):
    return t.reshape(t.shape[0], t.shape[1], h, d)


def apply_rotary(x, cos, sin):
    x1, x2 = jnp.split(x, 2, axis=-1)
    c = cos[:, None, :]
    s = sin[:, None, :]
    return jnp.concatenate([x1 * c - x2 * s, x1 * s + x2 * c], axis=-1).astype(x.dtype)


def dwconv(x, w, b):
    L = x.shape[1]
    xp = jnp.pad(x, ((0, 0), (CONV_LEFT, CONV_W - 1 - CONV_LEFT), (0, 0)))
    y = xp[:, 0:L] * w[0]
    for k in range(1, CONV_W):
        y = y + xp[:, k:k + L] * w[k]
    return y + b


def blockdiag(x, w, b):
    B_, L, _ = x.shape
    xb = x.reshape(B_, L, LRU_BLOCKS, LRU_BLOCK)
    return jnp.einsum('blnc,ncd->blnd', xb, w).reshape(B_, L, LRU_WIDTH) + b


def linear_scan(a, b, h0):
    def comb(l, r):
        return r[0] * l[0], r[0] * l[1] + r[1]
    a_cum, b_cum = lax.associative_scan(comb, (a, b), axis=1)
    if h0 is None:
        return b_cum
    return b_cum + a_cum * h0[:, None, :]


def rglru_coeffs(u, wa, ba, wx, bx, lam):
    r = jax.nn.sigmoid(blockdiag(u, wa, ba))
    i = jax.nn.sigmoid(blockdiag(u, wx, bx))
    log_a = -LRU_C * r * jax.nn.softplus(-lam.astype(jnp.float32))
    a = jnp.exp(log_a)
    bterm = jnp.sqrt(-jnp.expm1(2.0 * log_a)) * (i * u)
    return a, bterm


def rglru_bidir(ul, uc, wa, ba, wx, bx, lam):
    outs_l, outs_c = [], []
    for d in range(2):
        al, bl = rglru_coeffs(ul, wa[d], ba[d], wx[d], bx[d], lam[d])
        ac, bc = rglru_coeffs(uc, wa[d], ba[d], wx[d], bx[d], lam[d])
        if d == 1:
            al, bl, ac, bc = [jnp.flip(t, axis=1) for t in (al, bl, ac, bc)]
        h_c = linear_scan(ac, bc, None)
        h_l = linear_scan(al, bl, h_c[:, -1])
        if d == 1:
            h_c = jnp.flip(h_c, axis=1)
            h_l = jnp.flip(h_l, axis=1)
        outs_l.append(h_l)
        outs_c.append(h_c)
    return outs_l[0] + outs_l[1], outs_c[0] + outs_c[1]


def retention_scan(q, k, v, log_g, s0):
    B_, H, L, _ = q.shape
    C = RET_CHUNK
    n = L // C
    idx = jnp.arange(C, dtype=jnp.float32)
    diff = idx[:, None] - idx[None, :]
    lg = log_g[:, None, None]
    intra = jnp.where(diff >= 0, jnp.exp(lg * jnp.maximum(diff, 0.0)), 0.0)
    q_dec = jnp.exp(log_g[:, None] * (idx + 1.0))
    k_dec = jnp.exp(log_g[:, None] * (C - 1.0 - idx))
    s_dec = jnp.exp(log_g * C)

    def to_chunks(t):
        return t.reshape(B_, H, n, C, t.shape[-1]).transpose(2, 0, 1, 3, 4)

    def step(s, qkv):
        qc, kc, vc = qkv
        scores = jnp.einsum('bhid,bhjd->bhij', qc, kc) * intra
        o = (jnp.einsum('bhij,bhje->bhie', scores, vc)
             + jnp.einsum('bhid,bhde->bhie', qc * q_dec[..., None], s))
        s_new = s * s_dec[:, None, None] + jnp.einsum('bhjd,bhje->bhde', kc * k_dec[..., None], vc)
        return s_new, o

    s_fin, o = lax.scan(step, s0, (to_chunks(q), to_chunks(k), to_chunks(v)))
    o = o.transpose(1, 2, 0, 3, 4).reshape(B_, H, L, o.shape[-1])
    return o, s_fin


def retention_bidir(ql, kl, vl, qc, kc, vc, log_g):
    B_, H = ql.shape[0], ql.shape[1]
    s0 = jnp.zeros((B_, H, RET_DK, RET_DV), jnp.float32)
    outs_l, outs_c = [], []
    for d in range(2):
        seqs = (ql, kl, vl, qc, kc, vc)
        if d == 1:
            seqs = tuple(jnp.flip(t, axis=2) for t in seqs)
        o_c, s_c = retention_scan(seqs[3], seqs[4], seqs[5], log_g[d], s0)
        o_l, _ = retention_scan(seqs[0], seqs[1], seqs[2], log_g[d], s_c)
        if d == 1:
            o_c = jnp.flip(o_c, axis=2)
            o_l = jnp.flip(o_l, axis=2)
        outs_l.append(o_l)
        outs_c.append(o_c)
    return outs_l[0] + outs_l[1], outs_c[0] + outs_c[1]


def head_groupnorm(o, g):
    mu = jnp.mean(o, axis=-1, keepdims=True)
    var = jnp.mean(jnp.square(o - mu), axis=-1, keepdims=True)
    y = (o - mu) * lax.rsqrt(var + EPS)
    B_, H, L, dv = y.shape
    return y.transpose(0, 2, 1, 3).reshape(B_, L, H * dv) * g


def to_bhld(t):
    return t.astype(jnp.float32).transpose(0, 2, 1, 3)


def even_mixer(hl, hc, w_in, w_out, conv_w, conv_b, wa, ba, wx, bx, lam, decay_logit, gn_g, cos, sin):
    splits = [LRU_WIDTH, 2 * LRU_WIDTH, 2 * LRU_WIDTH + RET_QK, 2 * LRU_WIDTH + 2 * RET_QK,
              2 * LRU_WIDTH + 2 * RET_QK + RET_V]
    gl, rl, ql, kl, vl, ol = jnp.split(hl @ w_in, splits, axis=-1)
    gc, rc, qc, kc, vc, oc = jnp.split(hc @ w_in, splits, axis=-1)
    ul = dwconv(rl, conv_w, conv_b).astype(jnp.float32)
    uc = dwconv(rc, conv_w, conv_b).astype(jnp.float32)
    hl_lru, hc_lru = rglru_bidir(ul, uc, wa, ba, wx, bx, lam)
    lru_l = jax.nn.gelu(gl.astype(jnp.float32)) * hl_lru
    lru_c = jax.nn.gelu(gc.astype(jnp.float32)) * hc_lru
    k_scale = RET_DK ** -0.5
    ql = to_bhld(apply_rotary(split_heads(ql, RET_HEADS, RET_DK), cos, sin))
    kl = to_bhld(apply_rotary(split_heads(kl, RET_HEADS, RET_DK), cos, sin)) * k_scale
    vl = to_bhld(split_heads(vl, RET_HEADS, RET_DV))
    qc = to_bhld(split_heads(qc, RET_HEADS, RET_DK))
    kc = to_bhld(split_heads(kc, RET_HEADS, RET_DK)) * k_scale
    vc = to_bhld(split_heads(vc, RET_HEADS, RET_DV))
    log_g = -jax.nn.softplus(-decay_logit.astype(jnp.float32))
    rl_out, rc_out = retention_bidir(ql, kl, vl, qc, kc, vc, log_g)
    ret_l = head_groupnorm(rl_out, gn_g) * jax.nn.silu(ol.astype(jnp.float32))
    ret_c = head_groupnorm(rc_out, gn_g) * jax.nn.silu(oc.astype(jnp.float32))
    yl = jnp.concatenate([lru_l, ret_l], axis=-1).astype(hl.dtype) @ w_out
    yc = jnp.concatenate([lru_c, ret_c], axis=-1).astype(hc.dtype) @ w_out
    return yl, yc


def multiscale_pool(x, pool_w, pool_scale):
    B_, L, _ = x.shape
    xf = x.astype(jnp.float32)
    cs = jnp.concatenate([jnp.zeros((B_, 1, POOL_WIDTH), jnp.float32), jnp.cumsum(xf, axis=1)], axis=1)
    t = jnp.arange(L)
    outs = []
    for gi, w in enumerate(POOL_WINDOWS):
        lo = jnp.clip(t - w // 2, 0, L)
        hi = jnp.clip(t + w // 2, 0, L)
        sl = slice(gi * POOL_GROUP, (gi + 1) * POOL_GROUP)
        csg = cs[..., sl]
        cnt = (hi - lo).astype(jnp.float32)[None, :, None]
        mean = (csg[:, hi] - csg[:, lo]) / cnt
        outs.append(jnp.einsum('blc,cd->bld', mean - xf[..., sl], pool_w[gi].astype(jnp.float32)))
    return (jnp.concatenate(outs, axis=-1) * pool_scale).astype(x.dtype)


def attend(q, k, v):
    B_, Lq = q.shape[0], q.shape[1]
    nb = Lq // Q_BLOCK
    qb = q.reshape(B_, nb, Q_BLOCK, KV_HEADS, GROUP, HEAD_DIM).transpose(1, 0, 3, 4, 2, 5)
    kt = k.transpose(0, 2, 1, 3)
    vt = v.transpose(0, 2, 1, 3)
    scale = HEAD_DIM ** -0.5

    def blk(qi):
        s = jnp.einsum('bkgqd,bksd->bkgqs', qi, kt, preferred_element_type=jnp.float32) * scale
        p = jax.nn.softmax(s, axis=-1)
        return jnp.einsum('bkgqs,bksd->bkgqd', p.astype(vt.dtype), vt)

    o = lax.map(blk, qb)
    return o.transpose(1, 0, 4, 2, 3, 5).reshape(B_, Lq, ATT_QW)


def odd_mixer(hl, hc, w_in, w_out, pool_w, pool_scale, q_g, k_g, cos, sin, with_ctx):
    splits = [POOL_WIDTH, POOL_WIDTH + ATT_QW, POOL_WIDTH + ATT_QW + ATT_KVW]
    pool_l, ql, kl, vl = jnp.split(hl @ w_in, splits, axis=-1)
    ql = apply_rotary(rmsnorm(split_heads(ql, ATT_HEADS, HEAD_DIM), q_g), cos, sin)
    kl = apply_rotary(rmsnorm(split_heads(kl, KV_HEADS, HEAD_DIM), k_g), cos, sin)
    vl = split_heads(vl, KV_HEADS, HEAD_DIM)
    if with_ctx:
        pool_c, qc, kc, vc = jnp.split(hc @ w_in, splits, axis=-1)
    else:
        kc, vc = jnp.split(hc @ w_in[:, POOL_WIDTH + ATT_QW:], [ATT_KVW], axis=-1)
    kc = rmsnorm(split_heads(kc, KV_HEADS, HEAD_DIM), k_g)
    vc = split_heads(vc, KV_HEADS, HEAD_DIM)
    att_l = attend(ql, jnp.concatenate([kc, kl], axis=1), jnp.concatenate([vc, vl], axis=1))
    yl = jnp.concatenate([multiscale_pool(pool_l, pool_w, pool_scale), att_l.astype(hl.dtype)], axis=-1) @ w_out
    if not with_ctx:
        return yl, None
    qc = rmsnorm(split_heads(qc, ATT_HEADS, HEAD_DIM), q_g)
    att_c = attend(qc, kc, vc)
    yc = jnp.concatenate([multiscale_pool(pool_c, pool_w, pool_scale), att_c.astype(hc.dtype)], axis=-1) @ w_out
    return yl, yc


def _fwd_setup_inputs(seed: int = 0) -> dict:
    key = jax.random.key(seed)
    ks = list(jax.random.split(key, 28))

    def nrm(i, shape, s):
        return jax.random.normal(ks[i], shape, jnp.float32) * s

    D = D_MODEL
    u = jax.random.uniform(ks[19], (N_EVEN, 2, LRU_WIDTH), jnp.float32, 0.9, 0.999)
    a = u ** (1.0 / LRU_C)
    lam = jnp.log(a) - jnp.log1p(-a)
    gam = 1.0 - 2.0 ** (-5.0 - jnp.arange(RET_HEADS, dtype=jnp.float32))
    decay_base = jnp.log(gam) - jnp.log1p(-gam)
    return {
        "x": nrm(0, (BATCH, SEQ, D), 1.0),
        "c": nrm(1, (BATCH, D), 1.0),
        "ctx": nrm(2, (BATCH, CTX_LEN, D), 1.0),
        "c_ctx": nrm(3, (D,), 1.0),
        "mod_w": nrm(4, (DEPTH, D, 9 * D), 0.5 * D ** -0.5),
        "mod_b": nrm(5, (DEPTH, 9 * D), 0.02),
        "norm_pre": 1.0 + nrm(6, (DEPTH, 3, D), 0.02),
        "norm_post": 1.0 + nrm(7, (DEPTH, 3, D), 0.02),
        "ffn_gate": nrm(8, (DEPTH, 2, D, D_FF), D ** -0.5),
        "ffn_up": nrm(9, (DEPTH, 2, D, D_FF), D ** -0.5),
        "ffn_down": nrm(10, (DEPTH, 2, D_FF, D), D_FF ** -0.5),
        "ev_w_in": nrm(11, (N_EVEN, D, EV_IN), D ** -0.5),
        "ev_w_out": nrm(12, (N_EVEN, EV_MIX, D), EV_MIX ** -0.5),
        "lru_conv_w": nrm(13, (N_EVEN, CONV_W, LRU_WIDTH), CONV_W ** -0.5),
        "lru_conv_b": nrm(14, (N_EVEN, LRU_WIDTH), 0.01),
        "lru_wa": nrm(15, (N_EVEN, 2, LRU_BLOCKS, LRU_BLOCK, LRU_BLOCK), LRU_BLOCK ** -0.5),
        "lru_ba": nrm(16, (N_EVEN, 2, LRU_WIDTH), 0.01),
        "lru_wx": nrm(17, (N_EVEN, 2, LRU_BLOCKS, LRU_BLOCK, LRU_BLOCK), LRU_BLOCK ** -0.5),
        "lru_bx": nrm(18, (N_EVEN, 2, LRU_WIDTH), 0.01),
        "lru_lambda": lam,
        "ret_decay_logit": decay_base + nrm(20, (N_EVEN, 2, RET_HEADS), 0.05),
        "ret_gn": 1.0 + nrm(21, (N_EVEN, RET_V), 0.02),
        "od_w_in": nrm(22, (N_ODD, D, OD_IN), D ** -0.5),
        "od_w_out": nrm(23, (N_ODD, OD_MIX, D), OD_MIX ** -0.5),
        "pool_w": nrm(24, (N_ODD, len(POOL_WINDOWS), POOL_GROUP, POOL_GROUP), POOL_GROUP ** -0.5),
        "pool_scale": 1.0 + nrm(25, (N_ODD, POOL_WIDTH), 0.1),
        "q_norm": 1.0 + nrm(26, (N_ODD, HEAD_DIM), 0.02),
        "k_norm": 1.0 + nrm(27, (N_ODD, HEAD_DIM), 0.02),
    }


def _fwd_reference(x, c, ctx, c_ctx, mod_w, mod_b, norm_pre, norm_post, ffn_gate, ffn_up, ffn_down,
              ev_w_in, ev_w_out, lru_conv_w, lru_conv_b, lru_wa, lru_ba, lru_wx, lru_bx, lru_lambda,
              ret_decay_logit, ret_gn, od_w_in, od_w_out, pool_w, pool_scale, q_norm, k_norm):
    D = D_MODEL
    S = x.shape[1]
    rows = S // GRID_W
    row = jnp.repeat(jnp.arange(rows, dtype=jnp.float32), GRID_W)
    col = jnp.tile(jnp.arange(GRID_W, dtype=jnp.float32), rows)
    n_ax = HEAD_DIM // 4
    f_ax = ROPE_THETA ** (-jnp.arange(n_ax, dtype=jnp.float32) / n_ax)
    ang2 = jnp.concatenate([row[:, None] * f_ax, col[:, None] * f_ax], axis=-1)
    cos2, sin2 = jnp.cos(ang2), jnp.sin(ang2)
    n_r = RET_DK // 2
    f_r = RET_THETA ** (-jnp.arange(n_r, dtype=jnp.float32) / n_r)
    ang1 = jnp.arange(S, dtype=jnp.float32)[:, None] * f_r
    cos1, sin1 = jnp.cos(ang1), jnp.sin(ang1)

    sc = jax.nn.silu(c)
    scc = jax.nn.silu(c_ctx)
    xl, xc = x, ctx
    for li in range(DEPTH):
        last = li == DEPTH - 1
        n_ctx_sub = 2 if last else 3
        mod_l = (sc @ mod_w[li] + mod_b[li]).reshape(-1, 3, 3, 1, D)
        mod_c = (scc @ mod_w[li][:, :n_ctx_sub * 3 * D] + mod_b[li, :n_ctx_sub * 3 * D]).reshape(n_ctx_sub, 3, D)

        hl = modulate(xl, norm_pre[li, 0], mod_l[:, 0, 0], mod_l[:, 0, 1])
        hc = modulate(xc, norm_pre[li, 0], mod_c[0, 0], mod_c[0, 1])
        xl = residual(xl, swiglu(hl, ffn_gate[li, 0], ffn_up[li, 0], ffn_down[li, 0]), norm_post[li, 0], mod_l[:, 0, 2], FFN_STEP)
        xc = residual(xc, swiglu(hc, ffn_gate[li, 0], ffn_up[li, 0], ffn_down[li, 0]), norm_post[li, 0], mod_c[0, 2], FFN_STEP)

        hl = modulate(xl, norm_pre[li, 1], mod_l[:, 1, 0], mod_l[:, 1, 1])
        hc = modulate(xc, norm_pre[li, 1], mod_c[1, 0], mod_c[1, 1])
        if li % 2 == 0:
            e = li // 2
            yl, yc = even_mixer(hl, hc, ev_w_in[e], ev_w_out[e], lru_conv_w[e], lru_conv_b[e],
                                lru_wa[e], lru_ba[e], lru_wx[e], lru_bx[e], lru_lambda[e],
                                ret_decay_logit[e], ret_gn[e], cos1, sin1)
        else:
            o = li // 2
            yl, yc = odd_mixer(hl, hc, od_w_in[o], od_w_out[o], pool_w[o], pool_scale[o],
                               q_norm[o], k_norm[o], cos2, sin2, not last)
        xl = residual(xl, yl, norm_post[li, 1], mod_l[:, 1, 2], 1.0)
        if not last:
            xc = residual(xc, yc, norm_post[li, 1], mod_c[1, 2], 1.0)

        hl = modulate(xl, norm_pre[li, 2], mod_l[:, 2, 0], mod_l[:, 2, 1])
        xl = residual(xl, swiglu(hl, ffn_gate[li, 1], ffn_up[li, 1], ffn_down[li, 1]), norm_post[li, 2], mod_l[:, 2, 2], FFN_STEP)
        if not last:
            hc = modulate(xc, norm_pre[li, 2], mod_c[2, 0], mod_c[2, 1])
            xc = residual(xc, swiglu(hc, ffn_gate[li, 1], ffn_up[li, 1], ffn_down[li, 1]), norm_post[li, 2], mod_c[2, 2], FFN_STEP)
    return xl


import jax as _jax
import jax.numpy as _jnp

TWIN_FORMAT = 'train_step'
FWD_PARAMS = ['x', 'c', 'ctx', 'c_ctx', 'mod_w', 'mod_b', 'norm_pre', 'norm_post', 'ffn_gate', 'ffn_up', 'ffn_down', 'ev_w_in', 'ev_w_out', 'lru_conv_w', 'lru_conv_b', 'lru_wa', 'lru_ba', 'lru_wx', 'lru_bx', 'lru_lambda', 'ret_decay_logit', 'ret_gn', 'od_w_in', 'od_w_out', 'pool_w', 'pool_scale', 'q_norm', 'k_norm']
TWIN_WEIGHTS = ['c_ctx', 'mod_w', 'mod_b', 'norm_pre', 'norm_post', 'ffn_gate', 'ffn_up', 'ffn_down', 'ev_w_in', 'ev_w_out', 'lru_conv_w', 'lru_conv_b', 'lru_wa', 'lru_ba', 'lru_wx', 'lru_bx', 'lru_lambda', 'ret_decay_logit', 'ret_gn', 'od_w_in', 'od_w_out', 'pool_w', 'pool_scale', 'q_norm', 'k_norm']
TWIN_DIFF_INPUT = 'x'
TWIN_INPUTS = ['x', 'c', 'ctx', 'c_ctx', 'mod_w', 'mod_b', 'norm_pre', 'norm_post', 'ffn_gate', 'ffn_up', 'ffn_down', 'ev_w_in', 'ev_w_out', 'lru_conv_w', 'lru_conv_b', 'lru_wa', 'lru_ba', 'lru_wx', 'lru_bx', 'lru_lambda', 'ret_decay_logit', 'ret_gn', 'od_w_in', 'od_w_out', 'pool_w', 'pool_scale', 'q_norm', 'k_norm', 'loss_target', 'm_c_ctx', 'm_mod_w', 'm_mod_b', 'm_norm_pre', 'm_norm_post', 'm_ffn_gate', 'm_ffn_up', 'm_ffn_down', 'm_ev_w_in', 'm_ev_w_out', 'm_lru_conv_w', 'm_lru_conv_b', 'm_lru_wa', 'm_lru_ba', 'm_lru_wx', 'm_lru_bx', 'm_lru_lambda', 'm_ret_decay_logit', 'm_ret_gn', 'm_od_w_in', 'm_od_w_out', 'm_pool_w', 'm_pool_scale', 'm_q_norm', 'm_k_norm', 'v_c_ctx', 'v_mod_w', 'v_mod_b', 'v_norm_pre', 'v_norm_post', 'v_ffn_gate', 'v_ffn_up', 'v_ffn_down', 'v_ev_w_in', 'v_ev_w_out', 'v_lru_conv_w', 'v_lru_conv_b', 'v_lru_wa', 'v_lru_ba', 'v_lru_wx', 'v_lru_bx', 'v_lru_lambda', 'v_ret_decay_logit', 'v_ret_gn', 'v_od_w_in', 'v_od_w_out', 'v_pool_w', 'v_pool_scale', 'v_q_norm', 'v_k_norm']
TWIN_OUTPUTS = ['loss', 'grad_x', 'grad_c_ctx', 'grad_mod_w', 'grad_mod_b', 'grad_norm_pre', 'grad_norm_post', 'grad_ffn_gate', 'grad_ffn_up', 'grad_ffn_down', 'grad_ev_w_in', 'grad_ev_w_out', 'grad_lru_conv_w', 'grad_lru_conv_b', 'grad_lru_wa', 'grad_lru_ba', 'grad_lru_wx', 'grad_lru_bx', 'grad_lru_lambda', 'grad_ret_decay_logit', 'grad_ret_gn', 'grad_od_w_in', 'grad_od_w_out', 'grad_pool_w', 'grad_pool_scale', 'grad_q_norm', 'grad_k_norm', 'delta_c_ctx', 'delta_mod_w', 'delta_mod_b', 'delta_norm_pre', 'delta_norm_post', 'delta_ffn_gate', 'delta_ffn_up', 'delta_ffn_down', 'delta_ev_w_in', 'delta_ev_w_out', 'delta_lru_conv_w', 'delta_lru_conv_b', 'delta_lru_wa', 'delta_lru_ba', 'delta_lru_wx', 'delta_lru_bx', 'delta_lru_lambda', 'delta_ret_decay_logit', 'delta_ret_gn', 'delta_od_w_in', 'delta_od_w_out', 'delta_pool_w', 'delta_pool_scale', 'delta_q_norm', 'delta_k_norm', 'new_m_c_ctx', 'new_m_mod_w', 'new_m_mod_b', 'new_m_norm_pre', 'new_m_norm_post', 'new_m_ffn_gate', 'new_m_ffn_up', 'new_m_ffn_down', 'new_m_ev_w_in', 'new_m_ev_w_out', 'new_m_lru_conv_w', 'new_m_lru_conv_b', 'new_m_lru_wa', 'new_m_lru_ba', 'new_m_lru_wx', 'new_m_lru_bx', 'new_m_lru_lambda', 'new_m_ret_decay_logit', 'new_m_ret_gn', 'new_m_od_w_in', 'new_m_od_w_out', 'new_m_pool_w', 'new_m_pool_scale', 'new_m_q_norm', 'new_m_k_norm', 'new_v_c_ctx', 'new_v_mod_w', 'new_v_mod_b', 'new_v_norm_pre', 'new_v_norm_post', 'new_v_ffn_gate', 'new_v_ffn_up', 'new_v_ffn_down', 'new_v_ev_w_in', 'new_v_ev_w_out', 'new_v_lru_conv_w', 'new_v_lru_conv_b', 'new_v_lru_wa', 'new_v_lru_ba', 'new_v_lru_wx', 'new_v_lru_bx', 'new_v_lru_lambda', 'new_v_ret_decay_logit', 'new_v_ret_gn', 'new_v_od_w_in', 'new_v_od_w_out', 'new_v_pool_w', 'new_v_pool_scale', 'new_v_q_norm', 'new_v_k_norm']
TWIN_LEAF_KINDS = {'loss': 'loss', 'grad_x': 'grad_x', 'grad_c_ctx': 'grad_w', 'grad_mod_w': 'grad_w', 'grad_mod_b': 'grad_w', 'grad_norm_pre': 'grad_w', 'grad_norm_post': 'grad_w', 'grad_ffn_gate': 'grad_w', 'grad_ffn_up': 'grad_w', 'grad_ffn_down': 'grad_w', 'grad_ev_w_in': 'grad_w', 'grad_ev_w_out': 'grad_w', 'grad_lru_conv_w': 'grad_w', 'grad_lru_conv_b': 'grad_w', 'grad_lru_wa': 'grad_w', 'grad_lru_ba': 'grad_w', 'grad_lru_wx': 'grad_w', 'grad_lru_bx': 'grad_w', 'grad_lru_lambda': 'grad_w', 'grad_ret_decay_logit': 'grad_w', 'grad_ret_gn': 'grad_w', 'grad_od_w_in': 'grad_w', 'grad_od_w_out': 'grad_w', 'grad_pool_w': 'grad_w', 'grad_pool_scale': 'grad_w', 'grad_q_norm': 'grad_w', 'grad_k_norm': 'grad_w', 'delta_c_ctx': 'delta_w', 'delta_mod_w': 'delta_w', 'delta_mod_b': 'delta_w', 'delta_norm_pre': 'delta_w', 'delta_norm_post': 'delta_w', 'delta_ffn_gate': 'delta_w', 'delta_ffn_up': 'delta_w', 'delta_ffn_down': 'delta_w', 'delta_ev_w_in': 'delta_w', 'delta_ev_w_out': 'delta_w', 'delta_lru_conv_w': 'delta_w', 'delta_lru_conv_b': 'delta_w', 'delta_lru_wa': 'delta_w', 'delta_lru_ba': 'delta_w', 'delta_lru_wx': 'delta_w', 'delta_lru_bx': 'delta_w', 'delta_lru_lambda': 'delta_w', 'delta_ret_decay_logit': 'delta_w', 'delta_ret_gn': 'delta_w', 'delta_od_w_in': 'delta_w', 'delta_od_w_out': 'delta_w', 'delta_pool_w': 'delta_w', 'delta_pool_scale': 'delta_w', 'delta_q_norm': 'delta_w', 'delta_k_norm': 'delta_w', 'new_m_c_ctx': 'new_m', 'new_m_mod_w': 'new_m', 'new_m_mod_b': 'new_m', 'new_m_norm_pre': 'new_m', 'new_m_norm_post': 'new_m', 'new_m_ffn_gate': 'new_m', 'new_m_ffn_up': 'new_m', 'new_m_ffn_down': 'new_m', 'new_m_ev_w_in': 'new_m', 'new_m_ev_w_out': 'new_m', 'new_m_lru_conv_w': 'new_m', 'new_m_lru_conv_b': 'new_m', 'new_m_lru_wa': 'new_m', 'new_m_lru_ba': 'new_m', 'new_m_lru_wx': 'new_m', 'new_m_lru_bx': 'new_m', 'new_m_lru_lambda': 'new_m', 'new_m_ret_decay_logit': 'new_m', 'new_m_ret_gn': 'new_m', 'new_m_od_w_in': 'new_m', 'new_m_od_w_out': 'new_m', 'new_m_pool_w': 'new_m', 'new_m_pool_scale': 'new_m', 'new_m_q_norm': 'new_m', 'new_m_k_norm': 'new_m', 'new_v_c_ctx': 'new_v', 'new_v_mod_w': 'new_v', 'new_v_mod_b': 'new_v', 'new_v_norm_pre': 'new_v', 'new_v_norm_post': 'new_v', 'new_v_ffn_gate': 'new_v', 'new_v_ffn_up': 'new_v', 'new_v_ffn_down': 'new_v', 'new_v_ev_w_in': 'new_v', 'new_v_ev_w_out': 'new_v', 'new_v_lru_conv_w': 'new_v', 'new_v_lru_conv_b': 'new_v', 'new_v_lru_wa': 'new_v', 'new_v_lru_ba': 'new_v', 'new_v_lru_wx': 'new_v', 'new_v_lru_bx': 'new_v', 'new_v_lru_lambda': 'new_v', 'new_v_ret_decay_logit': 'new_v', 'new_v_ret_gn': 'new_v', 'new_v_od_w_in': 'new_v', 'new_v_od_w_out': 'new_v', 'new_v_pool_w': 'new_v', 'new_v_pool_scale': 'new_v', 'new_v_q_norm': 'new_v', 'new_v_k_norm': 'new_v'}


def _forward(args):
    return _fwd_reference(*[args[k] for k in FWD_PARAMS])


def _output_shape():
    def fwd():
        inp = _fwd_setup_inputs(0)
        return _fwd_reference(*[inp[k] for k in FWD_PARAMS])
    out = _jax.eval_shape(fwd)
    return out.shape, out.dtype

N_MICROBATCH = 1
ADAM_LR = 0.001
ADAM_B1 = 0.9
ADAM_B2 = 0.999
ADAM_EPS = 1e-08
ADAM_WD = 0.01
ADAM_STEP = 10
PER_EXAMPLE_BATCH_AXIS = {'x': 0, 'c': 0, 'ctx': 0, 'loss_target': 0}
SHARED_INPUTS = []
_WEIGHT_DTYPES = {'c_ctx': _jnp.float32, 'mod_w': _jnp.float32, 'mod_b': _jnp.float32, 'norm_pre': _jnp.float32, 'norm_post': _jnp.float32, 'ffn_gate': _jnp.float32, 'ffn_up': _jnp.float32, 'ffn_down': _jnp.float32, 'ev_w_in': _jnp.float32, 'ev_w_out': _jnp.float32, 'lru_conv_w': _jnp.float32, 'lru_conv_b': _jnp.float32, 'lru_wa': _jnp.float32, 'lru_ba': _jnp.float32, 'lru_wx': _jnp.float32, 'lru_bx': _jnp.float32, 'lru_lambda': _jnp.float32, 'ret_decay_logit': _jnp.float32, 'ret_gn': _jnp.float32, 'od_w_in': _jnp.float32, 'od_w_out': _jnp.float32, 'pool_w': _jnp.float32, 'pool_scale': _jnp.float32, 'q_norm': _jnp.float32, 'k_norm': _jnp.float32}
MOMENT_SCALE = {'c_ctx': 5.823146e-02, 'mod_w': 8.155558e-01, 'mod_b': 1.672927e+00, 'norm_pre': 6.829502e-02, 'norm_post': 2.003170e+00, 'ffn_gate': 2.218141e-02, 'ffn_up': 2.734773e-02, 'ffn_down': 4.532654e-02, 'ev_w_in': 1.935302e-01, 'ev_w_out': 3.545279e-01, 'lru_conv_w': 5.236172e-01, 'lru_conv_b': 1.639706e+00, 'lru_wa': 2.276786e-02, 'lru_ba': 3.458205e-02, 'lru_wx': 5.005542e-02, 'lru_bx': 9.600940e-02, 'lru_lambda': 9.115402e-02, 'ret_decay_logit': 2.167010e-01, 'ret_gn': 3.062200e-02, 'od_w_in': 3.141442e-01, 'od_w_out': 3.969568e-01, 'pool_w': 1.505293e-01, 'pool_scale': 1.703165e-01, 'q_norm': 3.995712e-02, 'k_norm': 3.942975e-02}


def _to_microbatches(a, axis):
    t = _jnp.moveaxis(a, axis, 0)
    t = t.reshape((N_MICROBATCH, t.shape[0] // N_MICROBATCH) + t.shape[1:])
    return _jnp.moveaxis(t, 1, axis + 1)


def setup_inputs(seed: int = 0) -> dict:
    inp = _fwd_setup_inputs(seed)
    key = _jax.random.fold_in(_jax.random.key(seed), 7919)
    shape, _ = _output_shape()
    out = dict(inp)
    out["loss_target"] = _jax.random.normal(_jax.random.fold_in(key, 0), shape, _jnp.float32)
    for i, name in enumerate(TWIN_WEIGHTS):
        w = inp[name].astype(_jnp.float32)
        if MOMENT_SCALE is None:
            s = _jnp.sqrt(_jnp.mean(_jnp.square(w)) + 1e-30)
        else:
            s = MOMENT_SCALE[name]
        km, kv = _jax.random.split(_jax.random.fold_in(key, i + 1))
        out[name] = w
        out["m_" + name] = s * _jax.random.normal(km, w.shape, _jnp.float32)
        out["v_" + name] = (s * s) * _jax.random.uniform(kv, w.shape, _jnp.float32, 0.5, 1.5)
    if N_MICROBATCH > 1:
        for name, axis in PER_EXAMPLE_BATCH_AXIS.items():
            out[name] = _to_microbatches(out[name], axis)
    return {'x': out['x'], 'c': out['c'], 'ctx': out['ctx'], 'c_ctx': out['c_ctx'], 'mod_w': out['mod_w'], 'mod_b': out['mod_b'], 'norm_pre': out['norm_pre'], 'norm_post': out['norm_post'], 'ffn_gate': out['ffn_gate'], 'ffn_up': out['ffn_up'], 'ffn_down': out['ffn_down'], 'ev_w_in': out['ev_w_in'], 'ev_w_out': out['ev_w_out'], 'lru_conv_w': out['lru_conv_w'], 'lru_conv_b': out['lru_conv_b'], 'lru_wa': out['lru_wa'], 'lru_ba': out['lru_ba'], 'lru_wx': out['lru_wx'], 'lru_bx': out['lru_bx'], 'lru_lambda': out['lru_lambda'], 'ret_decay_logit': out['ret_decay_logit'], 'ret_gn': out['ret_gn'], 'od_w_in': out['od_w_in'], 'od_w_out': out['od_w_out'], 'pool_w': out['pool_w'], 'pool_scale': out['pool_scale'], 'q_norm': out['q_norm'], 'k_norm': out['k_norm'], 'loss_target': out['loss_target'], 'm_c_ctx': out['m_c_ctx'], 'm_mod_w': out['m_mod_w'], 'm_mod_b': out['m_mod_b'], 'm_norm_pre': out['m_norm_pre'], 'm_norm_post': out['m_norm_post'], 'm_ffn_gate': out['m_ffn_gate'], 'm_ffn_up': out['m_ffn_up'], 'm_ffn_down': out['m_ffn_down'], 'm_ev_w_in': out['m_ev_w_in'], 'm_ev_w_out': out['m_ev_w_out'], 'm_lru_conv_w': out['m_lru_conv_w'], 'm_lru_conv_b': out['m_lru_conv_b'], 'm_lru_wa': out['m_lru_wa'], 'm_lru_ba': out['m_lru_ba'], 'm_lru_wx': out['m_lru_wx'], 'm_lru_bx': out['m_lru_bx'], 'm_lru_lambda': out['m_lru_lambda'], 'm_ret_decay_logit': out['m_ret_decay_logit'], 'm_ret_gn': out['m_ret_gn'], 'm_od_w_in': out['m_od_w_in'], 'm_od_w_out': out['m_od_w_out'], 'm_pool_w': out['m_pool_w'], 'm_pool_scale': out['m_pool_scale'], 'm_q_norm': out['m_q_norm'], 'm_k_norm': out['m_k_norm'], 'v_c_ctx': out['v_c_ctx'], 'v_mod_w': out['v_mod_w'], 'v_mod_b': out['v_mod_b'], 'v_norm_pre': out['v_norm_pre'], 'v_norm_post': out['v_norm_post'], 'v_ffn_gate': out['v_ffn_gate'], 'v_ffn_up': out['v_ffn_up'], 'v_ffn_down': out['v_ffn_down'], 'v_ev_w_in': out['v_ev_w_in'], 'v_ev_w_out': out['v_ev_w_out'], 'v_lru_conv_w': out['v_lru_conv_w'], 'v_lru_conv_b': out['v_lru_conv_b'], 'v_lru_wa': out['v_lru_wa'], 'v_lru_ba': out['v_lru_ba'], 'v_lru_wx': out['v_lru_wx'], 'v_lru_bx': out['v_lru_bx'], 'v_lru_lambda': out['v_lru_lambda'], 'v_ret_decay_logit': out['v_ret_decay_logit'], 'v_ret_gn': out['v_ret_gn'], 'v_od_w_in': out['v_od_w_in'], 'v_od_w_out': out['v_od_w_out'], 'v_pool_w': out['v_pool_w'], 'v_pool_scale': out['v_pool_scale'], 'v_q_norm': out['v_q_norm'], 'v_k_norm': out['v_k_norm']}


def _loss(weights, diff, rest, loss_target):
    with _jax.named_scope("forward"):
        args = {**rest, TWIN_DIFF_INPUT: diff, **{k: w.astype(_WEIGHT_DTYPES[k]) for k, w in weights.items()}}
        y = _forward(args)
    with _jax.named_scope("loss_head"):
        err = _jnp.square(y.astype(_jnp.float32) - loss_target)
        return 0.5 * _jnp.sum(_jnp.mean(err, axis=-1)) if err.ndim else 0.5 * err


def _adamw(w, g, m, v):
    m = ADAM_B1 * m + (1.0 - ADAM_B1) * g
    v = ADAM_B2 * v + (1.0 - ADAM_B2) * _jnp.square(g)
    m_hat = m / (1.0 - ADAM_B1 ** ADAM_STEP)
    v_hat = v / (1.0 - ADAM_B2 ** ADAM_STEP)
    delta = -ADAM_LR * (m_hat / (_jnp.sqrt(v_hat) + ADAM_EPS) + ADAM_WD * w)
    return delta, m, v


def reference(x, c, ctx, c_ctx, mod_w, mod_b, norm_pre, norm_post, ffn_gate, ffn_up, ffn_down, ev_w_in, ev_w_out, lru_conv_w, lru_conv_b, lru_wa, lru_ba, lru_wx, lru_bx, lru_lambda, ret_decay_logit, ret_gn, od_w_in, od_w_out, pool_w, pool_scale, q_norm, k_norm, loss_target, m_c_ctx, m_mod_w, m_mod_b, m_norm_pre, m_norm_post, m_ffn_gate, m_ffn_up, m_ffn_down, m_ev_w_in, m_ev_w_out, m_lru_conv_w, m_lru_conv_b, m_lru_wa, m_lru_ba, m_lru_wx, m_lru_bx, m_lru_lambda, m_ret_decay_logit, m_ret_gn, m_od_w_in, m_od_w_out, m_pool_w, m_pool_scale, m_q_norm, m_k_norm, v_c_ctx, v_mod_w, v_mod_b, v_norm_pre, v_norm_post, v_ffn_gate, v_ffn_up, v_ffn_down, v_ev_w_in, v_ev_w_out, v_lru_conv_w, v_lru_conv_b, v_lru_wa, v_lru_ba, v_lru_wx, v_lru_bx, v_lru_lambda, v_ret_decay_logit, v_ret_gn, v_od_w_in, v_od_w_out, v_pool_w, v_pool_scale, v_q_norm, v_k_norm):
    given = dict(x=x, c=c, ctx=ctx, c_ctx=c_ctx, mod_w=mod_w, mod_b=mod_b, norm_pre=norm_pre, norm_post=norm_post, ffn_gate=ffn_gate, ffn_up=ffn_up, ffn_down=ffn_down, ev_w_in=ev_w_in, ev_w_out=ev_w_out, lru_conv_w=lru_conv_w, lru_conv_b=lru_conv_b, lru_wa=lru_wa, lru_ba=lru_ba, lru_wx=lru_wx, lru_bx=lru_bx, lru_lambda=lru_lambda, ret_decay_logit=ret_decay_logit, ret_gn=ret_gn, od_w_in=od_w_in, od_w_out=od_w_out, pool_w=pool_w, pool_scale=pool_scale, q_norm=q_norm, k_norm=k_norm, loss_target=loss_target, m_c_ctx=m_c_ctx, m_mod_w=m_mod_w, m_mod_b=m_mod_b, m_norm_pre=m_norm_pre, m_norm_post=m_norm_post, m_ffn_gate=m_ffn_gate, m_ffn_up=m_ffn_up, m_ffn_down=m_ffn_down, m_ev_w_in=m_ev_w_in, m_ev_w_out=m_ev_w_out, m_lru_conv_w=m_lru_conv_w, m_lru_conv_b=m_lru_conv_b, m_lru_wa=m_lru_wa, m_lru_ba=m_lru_ba, m_lru_wx=m_lru_wx, m_lru_bx=m_lru_bx, m_lru_lambda=m_lru_lambda, m_ret_decay_logit=m_ret_decay_logit, m_ret_gn=m_ret_gn, m_od_w_in=m_od_w_in, m_od_w_out=m_od_w_out, m_pool_w=m_pool_w, m_pool_scale=m_pool_scale, m_q_norm=m_q_norm, m_k_norm=m_k_norm, v_c_ctx=v_c_ctx, v_mod_w=v_mod_w, v_mod_b=v_mod_b, v_norm_pre=v_norm_pre, v_norm_post=v_norm_post, v_ffn_gate=v_ffn_gate, v_ffn_up=v_ffn_up, v_ffn_down=v_ffn_down, v_ev_w_in=v_ev_w_in, v_ev_w_out=v_ev_w_out, v_lru_conv_w=v_lru_conv_w, v_lru_conv_b=v_lru_conv_b, v_lru_wa=v_lru_wa, v_lru_ba=v_lru_ba, v_lru_wx=v_lru_wx, v_lru_bx=v_lru_bx, v_lru_lambda=v_lru_lambda, v_ret_decay_logit=v_ret_decay_logit, v_ret_gn=v_ret_gn, v_od_w_in=v_od_w_in, v_od_w_out=v_od_w_out, v_pool_w=v_pool_w, v_pool_scale=v_pool_scale, v_q_norm=v_q_norm, v_k_norm=v_k_norm)
    weights = {n: given[n] for n in TWIN_WEIGHTS}
    shared = {n: given[n] for n in SHARED_INPUTS}
    per_example = {n: given[n] for n in ['x', 'c', 'ctx']}
    grad_fn = _jax.value_and_grad(_loss, argnums=(0, 1))

    def one_microbatch(ex, loss_target):
        ex = dict(ex)
        diff = ex.pop(TWIN_DIFF_INPUT)
        return grad_fn(weights, diff, {**shared, **ex}, loss_target)

    if N_MICROBATCH == 1:
        loss, (grad_w, grad_x) = one_microbatch(per_example, given["loss_target"])
    else:
        def body(carry, xs):
            loss_sum, grad_sum = carry
            l_k, (gw_k, gx_k) = one_microbatch(xs[0], xs[1])
            with _jax.named_scope("update"):
                return (loss_sum + l_k, _jax.tree.map(_jnp.add, grad_sum, gw_k)), gx_k

        init = (_jnp.zeros((), _jnp.float32), _jax.tree.map(_jnp.zeros_like, weights))
        (loss, grad_w), grad_x = _jax.lax.scan(body, init, (per_example, given["loss_target"]))
    with _jax.named_scope("update"):
        delta_w, new_m, new_v = {}, {}, {}
        for n in TWIN_WEIGHTS:
            delta_w[n], new_m[n], new_v[n] = _adamw(weights[n], grad_w[n], given["m_" + n], given["v_" + n])
    return (loss, grad_x, *[grad_w[n] for n in TWIN_WEIGHTS], *[delta_w[n] for n in TWIN_WEIGHTS],
            *[new_m[n] for n in TWIN_WEIGHTS], *[new_v[n] for n in TWIN_WEIGHTS])
```

```python
import functools
import math

import jax
import jax.numpy as jnp
from jax import lax
from jax.experimental import pallas as pl
from jax.experimental.pallas import tpu as pltpu

f32 = jnp.float32
bf16 = jnp.bfloat16
MESH = pl.DeviceIdType.MESH

EPS = 1e-6
FFN_STEP = 0.5
LRU_C = 8.0
CONV_W = 4
CONV_LEFT = 2
RET_DK = 256
RET_DV = 256
RET_CHUNK = 128
RET_THETA = 10000.0
POOL_WINDOWS = (2, 4, 8, 16)
POOL_GROUP = 128
HEAD_DIM = 128
ROPE_THETA = 10000.0
GRID_W = 64
ADAM_LR = 0.001
ADAM_B1 = 0.9
ADAM_B2 = 0.999
ADAM_EPS = 1e-08
ADAM_WD = 0.01
ADAM_STEP = 10

N_CHIPS = 4
N_DEV = 8
HALO = 8
VMEM_LIMIT = 56 * 1024 * 1024


def _params(sem=None):
    return pltpu.CompilerParams(dimension_semantics=sem, vmem_limit_bytes=VMEM_LIMIT)


def _pick(n, prefs):
    for p in prefs:
        if n % p == 0:
            return p
    return n


def _sds(shape, dtype):
    return jax.ShapeDtypeStruct(tuple(shape), dtype)


_MM_KINDS = {
    "v1": ((1, 0), "out[:, g] = A @ W[g]"),
    "v2": ((1, 0), "out = sum_g A[:, g] @ W[g]"),
    "v3": ((1, 1), "out = sum_g A[:, g] @ W[g]^T"),
    "v4": ((1, 1), "out[:, g] = A @ W[g]^T"),
    "v5": ((0, 0), "out[g] = A^T @ C[:, g]"),
    "v6": ((0, 0), "out[g] = A[:, g]^T @ C"),
}


def matmul(kind, a, b, *, widx=(), out_dtype=f32, init=None, dst=None, name):
    nw = len(widx)
    cdims = _MM_KINDS[kind][0]
    if kind in ("v1", "v2", "v3", "v4"):
        G = b.shape[0]
        d1, d2 = b.shape[-2:]
        M = a.shape[0]
    else:
        G = dst.shape[0]
        d1, d2 = dst.shape[-2:]
        M = a.shape[0]
    tm_p, tn_p, tk_p = (768, 512, 256, 128), (1408, 1536, 1024, 768, 512, 256, 128), (2048, 1408, 1536, 1024, 768, 512, 256, 128)
    wnone = (None,) * (1 + nw)

    if kind == "v1":
        K, Ns = d1, d2
        tm, tn, tk = _pick(M, tm_p), _pick(Ns, tn_p), _pick(K, tk_p)
        nI, nJ, nR = M // tm, Ns // tn, K // tk
        grid = (G, nI, nJ, nR)
        a_spec = pl.BlockSpec((tm, tk), lambda g, i, j, r: (i, r))
        b_spec = pl.BlockSpec(wnone + (tk, tn), lambda g, i, j, r: (g,) + widx + (r, j))
        o_spec = pl.BlockSpec((tm, tn), lambda g, i, j, r: (i, g * nJ + j))
        out_shape = _sds((M, G * Ns), out_dtype)
        acc_shape = (tm, tn)
    elif kind == "v2":
        Ks, N = d1, d2
        tm, tn, tk = _pick(M, tm_p), _pick(N, tn_p), _pick(Ks, tk_p)
        nI, nJ, nRk = M // tm, N // tn, Ks // tk
        nR = G * nRk
        grid = (1, nI, nJ, nR)
        a_spec = pl.BlockSpec((tm, tk), lambda g, i, j, r: (i, r))
        b_spec = pl.BlockSpec(wnone + (tk, tn), lambda g, i, j, r: (r // nRk,) + widx + (r % nRk, j))
        o_spec = pl.BlockSpec((tm, tn), lambda g, i, j, r: (i, j))
        out_shape = _sds((M, N), out_dtype)
        acc_shape = (tm, tn)
    elif kind == "v3":
        K, Ns = d1, d2
        tm, tn, tk = _pick(M, tm_p), _pick(K, tn_p), _pick(Ns, tk_p)
        nI, nJ, nRk = M // tm, K // tn, Ns // tk
        nR = G * nRk
        grid = (1, nI, nJ, nR)
        a_spec = pl.BlockSpec((tm, tk), lambda g, i, j, r: (i, r))
        b_spec = pl.BlockSpec(wnone + (tn, tk), lambda g, i, j, r: (r // nRk,) + widx + (j, r % nRk))
        o_spec = pl.BlockSpec((tm, tn), lambda g, i, j, r: (i, j))
        out_shape = _sds((M, K), out_dtype)
        acc_shape = (tm, tn)
    elif kind == "v4":
        Ks, N = d1, d2
        tm, tn, tk = _pick(M, tm_p), _pick(Ks, tn_p), _pick(N, tk_p)
        nI, nJ, nR = M // tm, Ks // tn, N // tk
        grid = (G, nI, nJ, nR)
        a_spec = pl.BlockSpec((tm, tk), lambda g, i, j, r: (i, r))
        b_spec = pl.BlockSpec(wnone + (tn, tk), lambda g, i, j, r: (g,) + widx + (j, r))
        o_spec = pl.BlockSpec((tm, tn), lambda g, i, j, r: (i, g * nJ + j))
        out_shape = _sds((M, G * Ks), out_dtype)
        acc_shape = (tm, tn)
    elif kind == "v5":
        K, Ns = d1, d2
        tm, tn, tk = _pick(K, tm_p), _pick(Ns, tn_p), _pick(M, (768, 512, 256, 128))
        nI, nJ, nR = K // tm, Ns // tn, M // tk
        grid = (G, nI, nJ, nR)
        a_spec = pl.BlockSpec((tk, tm), lambda g, i, j, r: (r, i))
        b_spec = pl.BlockSpec((tk, tn), lambda g, i, j, r: (r, g * nJ + j))
        o_spec = pl.BlockSpec(wnone + (tm, tn), lambda g, i, j, r: (g,) + widx + (i, j))
        acc_shape = (tm, tn)
    else:
        Ks, N = d1, d2
        tm, tn, tk = _pick(Ks, tm_p), _pick(N, tn_p), _pick(M, (768, 512, 256, 128))
        nI, nJ, nR = Ks // tm, N // tn, M // tk
        grid = (G, nI, nJ, nR)
        a_spec = pl.BlockSpec((tk, tm), lambda g, i, j, r: (r, g * nI + i))
        b_spec = pl.BlockSpec((tk, tn), lambda g, i, j, r: (r, j))
        o_spec = pl.BlockSpec(wnone + (tm, tn), lambda g, i, j, r: (g,) + widx + (i, j))
        acc_shape = (tm, tn)

    has_init = init is not None
    has_dst = dst is not None

    def body(*refs):
        a_ref, b_ref = refs[0], refs[1]
        pos = 2
        init_ref = None
        if has_init:
            init_ref = refs[pos]
            pos += 1
        if has_dst:
            pos += 1
        o_ref, acc_ref = refs[pos], refs[pos + 1]
        r = pl.program_id(3)

        @pl.when(r == 0)
        def _():
            if has_init:
                acc_ref[...] = init_ref[...]
            else:
                acc_ref[...] = jnp.zeros(acc_shape, f32)

        acc_ref[...] += lax.dot_general(a_ref[...], b_ref[...], ((cdims[:1], cdims[1:]), ((), ())),
                                        preferred_element_type=f32)

        @pl.when(r == nR - 1)
        def _():
            o_ref[...] = acc_ref[...].astype(o_ref.dtype)

    in_specs = [a_spec, b_spec]
    args = [a, b]
    aliases = {}
    if has_init:
        in_specs.append(pl.BlockSpec((tm, tn), lambda g, i, j, r: (i, j)))
        args.append(init)
    if has_dst:
        in_specs.append(pl.BlockSpec(memory_space=pl.ANY))
        args.append(dst)
        aliases = {len(args) - 1: 0}
        out_shape = _sds(dst.shape, dst.dtype)
    return pl.pallas_call(
        body, name=name, grid=grid, in_specs=in_specs, out_specs=o_spec, out_shape=out_shape,
        scratch_shapes=[pltpu.VMEM(acc_shape, f32)], input_output_aliases=aliases,
        compiler_params=_params(("arbitrary", "arbitrary", "arbitrary", "arbitrary")),
    )(*args)


class RowCfg:
    def __init__(self, TR, nT, cT):
        self.TR, self.nT, self.cT = TR, nT, cT


def _row_spec(cfg, spec, off):
    kind = spec[0]
    TR = cfg.TR
    hb = TR // HALO
    nH = cfg.nT * hb
    if kind == "row":
        _, arr, w, cb = spec
        return pl.BlockSpec((TR, w), lambda i: (i + off, cb))
    if kind == "prev":
        _, arr, w, cb = spec
        return pl.BlockSpec((HALO, w), lambda i: (jnp.maximum((i + off) * hb - 1, 0), cb))
    if kind == "next":
        _, arr, w, cb = spec
        return pl.BlockSpec((HALO, w), lambda i: (jnp.minimum((i + off + 1) * hb, nH - 1), cb))
    if kind == "full":
        arr = spec[1]
        nd = arr.ndim
        return pl.BlockSpec(arr.shape, lambda i: (0,) * nd)
    if kind == "grp":
        arr = spec[1]
        cT = cfg.cT
        return pl.BlockSpec((None, 1, arr.shape[-1]), lambda i: (((i + off) >= cT).astype(jnp.int32), 0, 0))
    if kind == "drow":
        _, arr, w, cb = spec
        return pl.BlockSpec((arr.shape[0], TR, w), lambda i: (0, i + off, cb))
    raise ValueError(kind)


def rowcall(cfg, fn, name, ins, outs, *, off=0, n=None, scratch=()):
    n = cfg.nT - off if n is None else n
    in_specs = [_row_spec(cfg, s, off) for s in ins]
    out_specs = [_row_spec(cfg, (s[0], s[1]) + tuple(s[2:]), off) for s in outs]
    out_shape = [s[1] for s in outs]

    def body(*refs):
        fn(pl.program_id(0) + off, *refs)

    res = pl.pallas_call(
        body, name=name, grid=(n,), in_specs=in_specs, out_specs=out_specs, out_shape=out_shape,
        scratch_shapes=list(scratch), compiler_params=_params(("arbitrary",)),
    )(*[s[1] for s in ins])
    return res


def _acc(ref, val, first):
    @pl.when(first)
    def _():
        ref[...] = val

    @pl.when(jnp.logical_not(first))
    def _():
        ref[...] += val


def _rms(x):
    return x * lax.rsqrt(jnp.mean(x * x, axis=-1, keepdims=True) + EPS)


def _pre_fn(x, g, shift, scale):
    return (_rms(x) * g) * (1.0 + scale) + shift


def pre_fwd(cfg, x, g, shift, scale, name):
    D = x.shape[1]

    def fn(i, x_ref, g_ref, sh_ref, sc_ref, h_ref):
        h_ref[...] = _pre_fn(x_ref[...], g_ref[...], sh_ref[...], sc_ref[...]).astype(bf16)

    return rowcall(cfg, fn, name, [("row", x, D, 0), ("full", g), ("grp", shift), ("grp", scale)],
                   [("row", _sds(x.shape, bf16), D, 0)])[0]


def pre_bwd(cfg, x, g, shift, scale, dh, dx_in, name):
    D = x.shape[1]
    cT = cfg.cT

    def fn(i, x_ref, g_ref, sh_ref, sc_ref, dh_ref, dxin_ref, dx_ref, dg_ref, dsh_ref, dsc_ref):
        _, vjp = jax.vjp(_pre_fn, x_ref[...], g_ref[...], sh_ref[...], sc_ref[...])
        dx, dg, dsh, dsc = vjp(dh_ref[...])
        dx_ref[...] = dxin_ref[...] + dx
        _acc(dg_ref, dg, i == 0)
        first = jnp.logical_or(i == 0, i == cT)
        _acc(dsh_ref, dsh, first)
        _acc(dsc_ref, dsc, first)

    return rowcall(cfg, fn, name,
                   [("row", x, D, 0), ("full", g), ("grp", shift), ("grp", scale), ("row", dh, D, 0), ("row", dx_in, D, 0)],
                   [("row", _sds(x.shape, f32), D, 0), ("full", _sds((1, D), f32)),
                    ("grp", _sds((2, 1, D), f32)), ("grp", _sds((2, 1, D), f32))])


def _post_fn(w, y, g, gate):
    return (w * gate) * (_rms(y) * g)


def post_fwd(cfg, x, y, g, gate, w, name):
    D = x.shape[1]

    def fn(i, x_ref, y_ref, g_ref, gt_ref, o_ref):
        o_ref[...] = x_ref[...] + _post_fn(w, y_ref[...], g_ref[...], gt_ref[...])

    return rowcall(cfg, fn, name, [("row", x, D, 0), ("row", y, D, 0), ("full", g), ("grp", gate)],
                   [("row", _sds(x.shape, f32), D, 0)])[0]


def post_bwd(cfg, dx, y, g, gate, w, name):
    D = dx.shape[1]
    cT = cfg.cT

    def fn(i, dx_ref, y_ref, g_ref, gt_ref, dy_ref, dg_ref, dgt_ref):
        _, vjp = jax.vjp(functools.partial(_post_fn, w), y_ref[...], g_ref[...], gt_ref[...])
        dy, dg, dgt = vjp(dx_ref[...])
        dy_ref[...] = dy.astype(bf16)
        _acc(dg_ref, dg, i == 0)
        _acc(dgt_ref, dgt, jnp.logical_or(i == 0, i == cT))

    return rowcall(cfg, fn, name, [("row", dx, D, 0), ("row", y, D, 0), ("full", g), ("grp", gate)],
                   [("row", _sds(dx.shape, bf16), D, 0), ("full", _sds((1, D), f32)), ("grp", _sds((2, 1, D), f32))])


def _swiglu_fn(a, b):
    return jax.nn.silu(a) * b


def swiglu_fwd(cfg, a, b, name):
    F = a.shape[1]
    tf = _pick(F, (1408, 1024, 512, 256, 128))
    TR = cfg.TR

    def body(a_ref, b_ref, u_ref):
        u_ref[...] = _swiglu_fn(a_ref[...], b_ref[...]).astype(bf16)

    spec = pl.BlockSpec((TR, tf), lambda i, j: (i, j))
    return pl.pallas_call(body, name=name, grid=(cfg.nT, F // tf), in_specs=[spec, spec], out_specs=spec,
                          out_shape=_sds(a.shape, bf16), compiler_params=_params(("arbitrary", "arbitrary")))(a, b)


def swiglu_bwd(cfg, a, b, du, name):
    F = a.shape[1]
    tf = _pick(F, (1408, 1024, 512, 256, 128))
    TR = cfg.TR

    def body(a_ref, b_ref, du_ref, da_ref, db_ref):
        _, vjp = jax.vjp(_swiglu_fn, a_ref[...], b_ref[...])
        da, db = vjp(du_ref[...])
        da_ref[...] = da.astype(bf16)
        db_ref[...] = db.astype(bf16)

    spec = pl.BlockSpec((TR, tf), lambda i, j: (i, j))
    return pl.pallas_call(body, name=name, grid=(cfg.nT, F // tf), in_specs=[spec, spec, spec], out_specs=[spec, spec],
                          out_shape=[_sds(a.shape, bf16), _sds(a.shape, bf16)],
                          compiler_params=_params(("arbitrary", "arbitrary")))(a, b, du)


def ffn_fwd(cfg, x, p, tag):
    h = pre_fwd(cfg, x, p["g_pre"], p["shift"], p["scale"], tag + "_pre")
    a = matmul("v1", h, p["wg"], widx=p["widx"], name=tag + "_gate")
    b = matmul("v1", h, p["wu"], widx=p["widx"], name=tag + "_up")
    u = swiglu_fwd(cfg, a, b, tag + "_act")
    y = matmul("v2", u, p["wd"], widx=p["widx"], name=tag + "_down")
    xo = post_fwd(cfg, x, y, p["g_post"], p["gate"], FFN_STEP, tag + "_post")
    return xo, (x, h, a, b, u, y)


def ffn_bwd(cfg, dX, saved, p, gbuf, tag):
    x, h, a, b, u, y = saved
    dy, dg_post, dgate = post_bwd(cfg, dX, y, p["g_post"], p["gate"], FFN_STEP, tag + "_postb")
    du = matmul("v4", dy, p["wd"], widx=p["widx"], name=tag + "_du")
    gbuf["wd"] = matmul("v6", u, dy, dst=gbuf["wd"], widx=p["widx"], name=tag + "_gwd")
    da, db = swiglu_bwd(cfg, a, b, du, tag + "_actb")
    gbuf["wg"] = matmul("v5", h, da, dst=gbuf["wg"], widx=p["widx"], name=tag + "_gwg")
    gbuf["wu"] = matmul("v5", h, db, dst=gbuf["wu"], widx=p["widx"], name=tag + "_gwu")
    dh = matmul("v3", da, p["wg"], widx=p["widx"], name=tag + "_dh1")
    dh = matmul("v3", db, p["wu"], widx=p["widx"], init=dh, name=tag + "_dh2")
    dX, dg_pre, dshift, dscale = pre_bwd(cfg, x, p["g_pre"], p["shift"], p["scale"], dh, dX, tag + "_preb")
    return dX, dict(g_pre=dg_pre, g_post=dg_post, shift=dshift, scale=dscale, gate=dgate), gbuf


def _seg_flags(cfg, i):
    start = jnp.logical_or(i == 0, i == cfg.cT)
    end = jnp.logical_or(i == cfg.cT - 1, i == cfg.nT - 1)
    return start, end


def _fill_halo(buf, cur, prev, nxt, start, end, TR):
    buf[pl.ds(0, HALO), :] = jnp.where(start, 0.0, prev)
    buf[pl.ds(HALO, TR), :] = cur
    buf[pl.ds(HALO + TR, HALO), :] = jnp.where(end, 0.0, nxt)


def conv_fwd(cfg, z, cw, cb, W, name):
    TR = cfg.TR

    def fn(i, r_ref, rp_ref, rn_ref, cw_ref, cb_ref, u_ref, buf):
        start, end = _seg_flags(cfg, i)
        _fill_halo(buf, r_ref[...], rp_ref[...], rn_ref[...], start, end, TR)
        u = jnp.broadcast_to(cb_ref[...], (TR, W))
        for k in range(CONV_W):
            u = u + buf[pl.ds(HALO + k - CONV_LEFT, TR), :] * cw_ref[pl.ds(k, 1), :]
        u_ref[...] = u

    return rowcall(cfg, fn, name, [("row", z, W, 1), ("prev", z, W, 1), ("next", z, W, 1), ("full", cw), ("full", cb)],
                   [("row", _sds((z.shape[0], W), f32), W, 0)], scratch=[pltpu.VMEM((TR + 2 * HALO, W), f32)])[0]


def conv_bwd(cfg, z, du, cw, W, name):
    TR = cfg.TR

    def fn(i, r_ref, rp_ref, rn_ref, du_ref, dup_ref, dun_ref, cw_ref, dr_ref, dcw_ref, dcb_ref, rbuf, dbuf):
        start, end = _seg_flags(cfg, i)
        _fill_halo(rbuf, r_ref[...], rp_ref[...], rn_ref[...], start, end, TR)
        _fill_halo(dbuf, du_ref[...], dup_ref[...], dun_ref[...], start, end, TR)
        du = du_ref[...]

        @pl.when(i == 0)
        def _():
            dcw_ref[...] = jnp.zeros(dcw_ref.shape, f32)
            dcb_ref[...] = jnp.zeros(dcb_ref.shape, f32)

        dr = jnp.zeros((TR, W), f32)
        for k in range(CONV_W):
            dr = dr + dbuf[pl.ds(HALO - (k - CONV_LEFT), TR), :] * cw_ref[pl.ds(k, 1), :]
            dcw_ref[pl.ds(k, 1), :] += jnp.sum(du * rbuf[pl.ds(HALO + k - CONV_LEFT, TR), :], axis=0, keepdims=True)
        dcb_ref[...] += jnp.sum(du, axis=0, keepdims=True)
        dr_ref[...] = dr

    T = z.shape[0]
    return rowcall(cfg, fn, name,
                   [("row", z, W, 1), ("prev", z, W, 1), ("next", z, W, 1), ("row", du, W, 0), ("prev", du, W, 0),
                    ("next", du, W, 0), ("full", cw)],
                   [("row", _sds((T, W), f32), W, 0), ("full", _sds((CONV_W, W), f32)), ("full", _sds((1, W), f32))],
                   scratch=[pltpu.VMEM((TR + 2 * HALO, W), f32), pltpu.VMEM((TR + 2 * HALO, W), f32)])


def _softplus(x):
    return jnp.maximum(x, 0.0) + jnp.log1p(jnp.exp(-jnp.abs(x)))


def _neg_expm1(x):
    series = -x * (1.0 + x * (0.5 + x * (1.0 / 6.0 + x * (1.0 / 24.0 + x * (1.0 / 120.0 + x * (1.0 / 720.0))))))
    return jnp.where(x > -0.1, series, 1.0 - jnp.exp(x))


def _lru_coef(u, pa, px, lam):
    r = jax.nn.sigmoid(pa)
    i = jax.nn.sigmoid(px)
    log_a = -LRU_C * r * _softplus(-lam)
    a = jnp.exp(log_a)
    b = jnp.sqrt(_neg_expm1(2.0 * log_a)) * (i * u)
    return a, b


def _blockdiag(u_bf, w_ref, d, nblk, blk):
    return jnp.concatenate(
        [jnp.dot(u_bf[:, n * blk:(n + 1) * blk], w_ref[d, n].astype(bf16), preferred_element_type=f32)
         for n in range(nblk)], axis=1)


def lru_coef_fwd(cfg, u, wa, ba, wx, bx, lam, name):
    T, W = u.shape
    nblk, blk = wa.shape[1], wa.shape[2]

    def fn(i, u_ref, wa_ref, ba_ref, wx_ref, bx_ref, lam_ref, a_ref, b_ref):
        uv = u_ref[...]
        u_bf = uv.astype(bf16)
        for d in range(2):
            pa = _blockdiag(u_bf, wa_ref, d, nblk, blk) + ba_ref[d]
            px = _blockdiag(u_bf, wx_ref, d, nblk, blk) + bx_ref[d]
            a, b = _lru_coef(uv, pa, px, lam_ref[d])
            a_ref[d] = a
            b_ref[d] = b

    return rowcall(cfg, fn, name, [("row", u, W, 0), ("full", wa), ("full", ba), ("full", wx), ("full", bx), ("full", lam)],
                   [("drow", _sds((2, T, W), f32), W, 0), ("drow", _sds((2, T, W), f32), W, 0)])


def lru_coef_bwd(cfg, u, da, db, wa, ba, wx, bx, lam, name):
    T, W = u.shape
    nblk, blk = wa.shape[1], wa.shape[2]

    def fn(i, u_ref, da_ref, db_ref, wa_ref, ba_ref, wx_ref, bx_ref, lam_ref,
           du_ref, dwa_ref, dba_ref, dwx_ref, dbx_ref, dlam_ref):
        @pl.when(i == 0)
        def _():
            for r in (dwa_ref, dba_ref, dwx_ref, dbx_ref, dlam_ref):
                r[...] = jnp.zeros(r.shape, f32)

        uv = u_ref[...]
        u_bf = uv.astype(bf16)
        du = jnp.zeros(uv.shape, f32)
        for d in range(2):
            pa = _blockdiag(u_bf, wa_ref, d, nblk, blk) + ba_ref[d]
            px = _blockdiag(u_bf, wx_ref, d, nblk, blk) + bx_ref[d]
            _, vjp = jax.vjp(_lru_coef, uv, pa, px, lam_ref[d])
            du_e, dpa, dpx, dlam = vjp((da_ref[d], db_ref[d]))
            du = du + du_e
            dba_ref[d] += jnp.sum(dpa, axis=0, keepdims=True)
            dbx_ref[d] += jnp.sum(dpx, axis=0, keepdims=True)
            dlam_ref[d] += dlam
            parts = []
            for n in range(nblk):
                sl = slice(n * blk, (n + 1) * blk)
                ga, gx = dpa[:, sl].astype(bf16), dpx[:, sl].astype(bf16)
                ub = u_bf[:, sl]
                dwa_ref[d, n] += lax.dot_general(ub, ga, (((0,), (0,)), ((), ())), preferred_element_type=f32)
                dwx_ref[d, n] += lax.dot_general(ub, gx, (((0,), (0,)), ((), ())), preferred_element_type=f32)
                parts.append(
                    lax.dot_general(ga, wa_ref[d, n].astype(bf16), (((1,), (1,)), ((), ())), preferred_element_type=f32)
                    + lax.dot_general(gx, wx_ref[d, n].astype(bf16), (((1,), (1,)), ((), ())), preferred_element_type=f32))
            du = du + jnp.concatenate(parts, axis=1)
        du_ref[...] = du

    return rowcall(cfg, fn, name,
                   [("row", u, W, 0), ("drow", da, W, 0), ("drow", db, W, 0), ("full", wa), ("full", ba), ("full", wx),
                    ("full", bx), ("full", lam)],
                   [("row", _sds((T, W), f32), W, 0), ("full", _sds(wa.shape, f32)), ("full", _sds(ba.shape, f32)),
                    ("full", _sds(wx.shape, f32)), ("full", _sds(bx.shape, f32)), ("full", _sds(lam.shape, f32))])


def _dir_tile(cfg, d, j):
    rev = jnp.where(j < cfg.cT, cfg.cT - 1 - j, cfg.nT - 1 - (j - cfg.cT))
    return jnp.where(d == 0, j, rev)


def lru_scan(cfg, a, b, name):
    _, T, W = a.shape
    TR, nT = cfg.TR, cfg.nT

    def body(a_ref, b_ref, h_ref, hp_ref, st):
        d, j = pl.program_id(0), pl.program_id(1)

        @pl.when(j == 0)
        def _():
            st[...] = jnp.zeros(st.shape, f32)

        def step(t, h):
            idx = t + d * (TR - 1 - 2 * t)
            hn = a_ref[pl.ds(idx, 1), :] * h + b_ref[pl.ds(idx, 1), :]
            hp_ref[pl.ds(idx, 1), :] = h
            h_ref[pl.ds(idx, 1), :] = hn
            return hn

        st[...] = lax.fori_loop(0, TR, step, st[...])

    spec = pl.BlockSpec((None, TR, W), lambda d, j: (d, _dir_tile(cfg, d, j), 0))
    return pl.pallas_call(body, name=name, grid=(2, nT), in_specs=[spec, spec], out_specs=[spec, spec],
                          out_shape=[_sds(a.shape, f32), _sds(a.shape, f32)], scratch_shapes=[pltpu.VMEM((1, W), f32)],
                          compiler_params=_params(("arbitrary", "arbitrary")))(a, b)


def lru_scan_bwd(cfg, a, hp, dh, name):
    _, T, W = a.shape
    TR, nT = cfg.TR, cfg.nT

    def body(a_ref, hp_ref, dh_ref, da_ref, db_ref, st):
        d, j = pl.program_id(0), pl.program_id(1)

        @pl.when(j == 0)
        def _():
            st[...] = jnp.zeros(st.shape, f32)

        def step(t, c):
            p = TR - 1 - t
            idx = p + d * (TR - 1 - 2 * p)
            g = dh_ref[pl.ds(idx, 1), :] + c
            db_ref[pl.ds(idx, 1), :] = g
            da_ref[pl.ds(idx, 1), :] = g * hp_ref[pl.ds(idx, 1), :]
            return a_ref[pl.ds(idx, 1), :] * g

        st[...] = lax.fori_loop(0, TR, step, st[...])

    spec = pl.BlockSpec((None, TR, W), lambda d, j: (d, _dir_tile(cfg, d, nT - 1 - j), 0))
    dspec = pl.BlockSpec((TR, W), lambda d, j: (_dir_tile(cfg, d, nT - 1 - j), 0))
    return pl.pallas_call(body, name=name, grid=(2, nT), in_specs=[spec, spec, dspec], out_specs=[spec, spec],
                          out_shape=[_sds(a.shape, f32), _sds(a.shape, f32)], scratch_shapes=[pltpu.VMEM((1, W), f32)],
                          compiler_params=_params(("arbitrary", "arbitrary")))(a, hp, dh)


def _lru_out_fn(gl, h0, h1):
    return jax.nn.gelu(gl) * (h0 + h1)


def lru_out_fwd(cfg, z, h, W, name):
    def fn(i, g_ref, h_ref, o_ref):
        o_ref[...] = _lru_out_fn(g_ref[...], h_ref[0], h_ref[1]).astype(bf16)

    return rowcall(cfg, fn, name, [("row", z, W, 0), ("drow", h, W, 0)], [("row", _sds((z.shape[0], W), bf16), W, 0)])[0]


def lru_out_bwd(cfg, z, h, dmix, W, name):
    def fn(i, g_ref, h_ref, d_ref, dg_ref, dh_ref):
        _, vjp = jax.vjp(_lru_out_fn, g_ref[...], h_ref[0], h_ref[1])
        dg, dh0, _ = vjp(d_ref[...])
        dg_ref[...] = dg
        dh_ref[...] = dh0

    T = z.shape[0]
    return rowcall(cfg, fn, name, [("row", z, W, 0), ("drow", h, W, 0), ("row", dmix, W, 0)],
                   [("row", _sds((T, W), f32), W, 0), ("row", _sds((T, W), f32), W, 0)])


def _rot_half(x, cos, sin):
    n = x.shape[1] // 2
    x1, x2 = x[:, :n], x[:, n:]
    return jnp.concatenate([x1 * cos - x2 * sin, x1 * sin + x2 * cos], axis=1)


def _dotf(a, b, ca, cb):
    return lax.dot_general(a, b, (((ca,), (cb,)), ((), ())), preferred_element_type=f32)


def _ret_chunk(d, q, k, v, s, logit, cos, sin):
    C = q.shape[0]
    lg = -_softplus(-logit)
    qr = _rot_half(q, cos, sin)
    kr = _rot_half(k, cos, sin) * (RET_DK ** -0.5)
    ii = lax.broadcasted_iota(jnp.int32, (C, C), 0)
    jj = lax.broadcasted_iota(jnp.int32, (C, C), 1)
    diff = ((ii - jj) if d == 0 else (jj - ii)).astype(f32)
    intra = jnp.where(diff >= 0, jnp.exp(lg * jnp.maximum(diff, 0.0)), 0.0)
    pos = lax.broadcasted_iota(jnp.int32, (C, 1), 0).astype(f32)
    if d == 0:
        q_dec, k_dec = jnp.exp(lg * (pos + 1.0)), jnp.exp(lg * (C - 1.0 - pos))
    else:
        q_dec, k_dec = jnp.exp(lg * (C - pos)), jnp.exp(lg * pos)
    s_dec = jnp.exp(lg * C)
    scores = _dotf(qr, kr, 1, 1) * intra
    o = _dotf(scores, v, 1, 0) + _dotf(qr * q_dec, s, 1, 0)
    s_new = s * s_dec + _dotf(kr * k_dec, v, 0, 0)
    return o, s_new


def _chunk_cfg(cfg):
    f = cfg.TR // RET_CHUNK
    return RowCfg(RET_CHUNK, cfg.nT * f, cfg.cT * f)


def ret_fwd(cfg, z, logit, cos, sin, H, qcol, name):
    T = z.shape[0]
    cc = _chunk_cfg(cfg)
    C, nC = RET_CHUNK, cc.nT

    def body(q_ref, k_ref, v_ref, lg_ref, cos_ref, sin_ref, o_ref, s_ref, st):
        d, j = pl.program_id(0), pl.program_id(2)

        @pl.when(j == 0)
        def _():
            st[...] = jnp.zeros(st.shape, f32)

        s_ref[...] = st[...]
        for dd in range(2):
            @pl.when(d == dd)
            def _():
                o, s_new = _ret_chunk(dd, q_ref[...], k_ref[...], v_ref[...], st[...], lg_ref[...], cos_ref[...], sin_ref[...])
                o_ref[...] = o
                st[...] = s_new

    tile = lambda d, j: _dir_tile(cc, d, j)
    zq = pl.BlockSpec((C, RET_DK), lambda d, h, j: (tile(d, j), qcol + h))
    zk = pl.BlockSpec((C, RET_DK), lambda d, h, j: (tile(d, j), qcol + H + h))
    zv = pl.BlockSpec((C, RET_DV), lambda d, h, j: (tile(d, j), qcol + 2 * H + h))
    lgs = pl.BlockSpec((None, None, 1, 1), lambda d, h, j: (d, h, 0, 0))
    cs = pl.BlockSpec((C, RET_DK // 2), lambda d, h, j: (tile(d, j), 0))
    o_spec = pl.BlockSpec((None, C, RET_DV), lambda d, h, j: (d, tile(d, j), h))
    s_spec = pl.BlockSpec((None, None, None, RET_DK, RET_DV), lambda d, h, j: (d, h, tile(d, j), 0, 0))
    return pl.pallas_call(
        body, name=name, grid=(2, H, nC), in_specs=[zq, zk, zv, lgs, cs, cs], out_specs=[o_spec, s_spec],
        out_shape=[_sds((2, T, H * RET_DV), f32), _sds((2, H, nC, RET_DK, RET_DV), f32)],
        scratch_shapes=[pltpu.VMEM((RET_DK, RET_DV), f32)],
        compiler_params=_params(("arbitrary", "arbitrary", "arbitrary")))(z, z, z, logit, cos, sin)


def ret_bwd(cfg, z, states, do, logit, cos, sin, H, qcol, name):
    T = z.shape[0]
    cc = _chunk_cfg(cfg)
    C, nC = RET_CHUNK, cc.nT

    def body(q_ref, k_ref, v_ref, s_ref, do_ref, lg_ref, cos_ref, sin_ref, dq_ref, dk_ref, dv_ref, dlg_ref, st):
        d, j = pl.program_id(0), pl.program_id(2)

        @pl.when(j == 0)
        def _():
            st[...] = jnp.zeros(st.shape, f32)
            dlg_ref[...] = jnp.zeros(dlg_ref.shape, f32)

        for dd in range(2):
            @pl.when(d == dd)
            def _():
                fn = lambda q, k, v, s, lg: _ret_chunk(dd, q, k, v, s, lg, cos_ref[...], sin_ref[...])
                _, vjp = jax.vjp(fn, q_ref[...], k_ref[...], v_ref[...], s_ref[...], lg_ref[...])
                dq, dk, dv, ds, dlg = vjp((do_ref[...], st[...]))
                dq_ref[...] = dq
                dk_ref[...] = dk
                dv_ref[...] = dv
                st[...] = ds
                dlg_ref[...] += dlg

    tile = lambda d, j: _dir_tile(cc, d, nC - 1 - j)
    zq = pl.BlockSpec((C, RET_DK), lambda d, h, j: (tile(d, j), qcol + h))
    zk = pl.BlockSpec((C, RET_DK), lambda d, h, j: (tile(d, j), qcol + H + h))
    zv = pl.BlockSpec((C, RET_DV), lambda d, h, j: (tile(d, j), qcol + 2 * H + h))
    s_spec = pl.BlockSpec((None, None, None, RET_DK, RET_DV), lambda d, h, j: (d, h, tile(d, j), 0, 0))
    do_spec = pl.BlockSpec((C, RET_DV), lambda d, h, j: (tile(d, j), h))
    lgs = pl.BlockSpec((None, None, 1, 1), lambda d, h, j: (d, h, 0, 0))
    cs = pl.BlockSpec((C, RET_DK // 2), lambda d, h, j: (tile(d, j), 0))
    g_spec = pl.BlockSpec((None, C, RET_DK), lambda d, h, j: (d, tile(d, j), h))
    gshape = _sds((2, T, H * RET_DK), f32)
    return pl.pallas_call(
        body, name=name, grid=(2, H, nC), in_specs=[zq, zk, zv, s_spec, do_spec, lgs, cs, cs],
        out_specs=[g_spec, g_spec, g_spec, lgs], out_shape=[gshape, gshape, gshape, _sds((2, H, 1, 1), f32)],
        scratch_shapes=[pltpu.VMEM((RET_DK, RET_DV), f32)],
        compiler_params=_params(("arbitrary", "arbitrary", "arbitrary")))(z, z, z, states, do, logit, cos, sin)


def _ret_norm_fn(H, o0, o1, ol, gn):
    o = o0 + o1
    parts = []
    for h in range(H):
        x = o[:, h * RET_DV:(h + 1) * RET_DV]
        mu = jnp.mean(x, axis=-1, keepdims=True)
        var = jnp.mean(jnp.square(x - mu), axis=-1, keepdims=True)
        parts.append((x - mu) * lax.rsqrt(var + EPS))
    return (jnp.concatenate(parts, axis=1) * gn) * jax.nn.silu(ol)


def ret_norm_fwd(cfg, o, z, gn, H, olcol, name):
    RV = H * RET_DV

    def fn(i, o_ref, ol_ref, gn_ref, r_ref):
        r_ref[...] = _ret_norm_fn(H, o_ref[0], o_ref[1], ol_ref[...], gn_ref[...]).astype(bf16)

    return rowcall(cfg, fn, name, [("drow", o, RV, 0), ("row", z, RV, olcol), ("full", gn)],
                   [("row", _sds((z.shape[0], RV), bf16), RV, 0)])[0]


def ret_norm_bwd(cfg, o, z, gn, dmix, H, olcol, dcol, name):
    RV = H * RET_DV
    T = z.shape[0]

    def fn(i, o_ref, ol_ref, gn_ref, d_ref, do_ref, dol_ref, dgn_ref):
        _, vjp = jax.vjp(functools.partial(_ret_norm_fn, H), o_ref[0], o_ref[1], ol_ref[...], gn_ref[...])
        do, _, dol, dgn = vjp(d_ref[...])
        do_ref[...] = do
        dol_ref[...] = dol
        _acc(dgn_ref, dgn, i == 0)

    return rowcall(cfg, fn, name, [("drow", o, RV, 0), ("row", z, RV, olcol), ("full", gn), ("row", dmix, RV, dcol)],
                   [("row", _sds((T, RV), f32), RV, 0), ("row", _sds((T, RV), f32), RV, 0), ("full", _sds((1, RV), f32))])


def _pool_geom(cfg, i, w, L):
    t = (i - cfg.cT) * cfg.TR + lax.broadcasted_iota(jnp.int32, (cfg.TR, 1), 0)
    lo = jnp.clip(t - w // 2, 0, L)
    hi = jnp.clip(t + w // 2, 0, L)
    return (hi - lo).astype(f32)


def _pool_centred(cfg, i, buf, gi, w, L):
    TR, G = cfg.TR, POOL_GROUP
    cols = pl.ds(gi * G, G)
    tot = buf[pl.ds(HALO - w // 2, TR), cols]
    for s in range(-w // 2 + 1, w // 2):
        tot = tot + buf[pl.ds(HALO + s, TR), cols]
    cnt = _pool_geom(cfg, i, w, L)
    return tot / cnt - buf[pl.ds(HALO, TR), cols], cnt


def pool_fwd(cfg, z, pw, ps, name):
    T = z.shape[0]
    TR, cT = cfg.TR, cfg.cT
    P = POOL_GROUP * len(POOL_WINDOWS)
    L = T - cT * TR

    def fn(i, x_ref, xp_ref, xn_ref, pw_ref, ps_ref, o_ref, buf):
        @pl.when(i < cT)
        def _():
            o_ref[...] = jnp.zeros(o_ref.shape, bf16)

        @pl.when(i >= cT)
        def _():
            start, end = _seg_flags(cfg, i)
            _fill_halo(buf, x_ref[...], xp_ref[...], xn_ref[...], start, end, TR)
            outs = []
            for gi, w in enumerate(POOL_WINDOWS):
                m, _ = _pool_centred(cfg, i, buf, gi, w, L)
                outs.append(jnp.dot(m.astype(bf16), pw_ref[gi].astype(bf16), preferred_element_type=f32))
            o_ref[...] = (jnp.concatenate(outs, axis=1) * ps_ref[...]).astype(bf16)

    return rowcall(cfg, fn, name, [("row", z, P, 0), ("prev", z, P, 0), ("next", z, P, 0), ("full", pw), ("full", ps)],
                   [("row", _sds((T, P), bf16), P, 0)], scratch=[pltpu.VMEM((TR + 2 * HALO, P), f32)])[0]


def pool_bwd_a(cfg, z, dmix, pw, ps, name):
    T = z.shape[0]
    TR, cT = cfg.TR, cfg.cT
    G = POOL_GROUP
    P = G * len(POOL_WINDOWS)
    L = T - cT * TR

    def fn(i, x_ref, xp_ref, xn_ref, d_ref, pw_ref, ps_ref, dm_ref, dmn_ref, dpw_ref, dps_ref, buf):
        @pl.when(i == 0)
        def _():
            dpw_ref[...] = jnp.zeros(dpw_ref.shape, f32)
            dps_ref[...] = jnp.zeros(dps_ref.shape, f32)

        @pl.when(i < cT)
        def _():
            dm_ref[...] = jnp.zeros(dm_ref.shape, f32)
            dmn_ref[...] = jnp.zeros(dmn_ref.shape, f32)

        @pl.when(i >= cT)
        def _():
            start, end = _seg_flags(cfg, i)
            _fill_halo(buf, x_ref[...], xp_ref[...], xn_ref[...], start, end, TR)
            dout = d_ref[...]
            dpre = dout * ps_ref[...]
            pres, dms, dmns = [], [], []
            for gi, w in enumerate(POOL_WINDOWS):
                m, cnt = _pool_centred(cfg, i, buf, gi, w, L)
                m_bf = m.astype(bf16)
                w_bf = pw_ref[gi].astype(bf16)
                pres.append(jnp.dot(m_bf, w_bf, preferred_element_type=f32))
                g_bf = dpre[:, gi * G:(gi + 1) * G].astype(bf16)
                dpw_ref[gi] += _dotf(m_bf, g_bf, 0, 0)
                dm = _dotf(g_bf, w_bf, 1, 1)
                dms.append(dm)
                dmns.append(dm / cnt)
            dps_ref[...] += jnp.sum(dout * jnp.concatenate(pres, axis=1), axis=0, keepdims=True)
            dm_ref[...] = jnp.concatenate(dms, axis=1)
            dmn_ref[...] = jnp.concatenate(dmns, axis=1)

    return rowcall(cfg, fn, name,
                   [("row", z, P, 0), ("prev", z, P, 0), ("next", z, P, 0), ("row", dmix, P, 0), ("full", pw), ("full", ps)],
                   [("row", _sds((T, P), f32), P, 0), ("row", _sds((T, P), f32), P, 0), ("full", _sds(pw.shape, f32)),
                    ("full", _sds((1, P), f32))], scratch=[pltpu.VMEM((TR + 2 * HALO, P), f32)])


def pool_bwd_b(cfg, dm, dmn, name):
    T, P = dm.shape
    TR, cT = cfg.TR, cfg.cT
    G = POOL_GROUP

    def fn(i, dm_ref, c_ref, p_ref, n_ref, dx_ref, buf):
        start, end = _seg_flags(cfg, i)
        _fill_halo(buf, c_ref[...], p_ref[...], n_ref[...], start, end, TR)
        outs = []
        for gi, w in enumerate(POOL_WINDOWS):
            cols = pl.ds(gi * G, G)
            tot = buf[pl.ds(HALO + w // 2, TR), cols]
            for s in range(-w // 2 + 1, w // 2):
                tot = tot + buf[pl.ds(HALO + s, TR), cols]
            outs.append(tot)
        dx_ref[...] = jnp.concatenate(outs, axis=1) - dm_ref[...]

    return rowcall(cfg, fn, name, [("row", dm, P, 0), ("row", dmn, P, 0), ("prev", dmn, P, 0), ("next", dmn, P, 0)],
                   [("row", _sds((T, P), f32), P, 0)], scratch=[pltpu.VMEM((TR + 2 * HALO, P), f32)])[0]


def _swap_halves(x):
    return pltpu.roll(x, HEAD_DIM // 2, 1)


def _headnorm(x, g):
    return _rms(x) * g


def att_prep(cfg, z, qg, kg, cosf, sinf, nq, name):
    T = z.shape[0]
    U = z.shape[1] // (nq + 3)
    nh = U // HEAD_DIM

    def fn(i, *refs):
        q_refs = refs[:nq]
        k_ref, v_ref, qg_ref, kg_ref, cos_ref, sin_ref, qn_ref, kn_ref, vb_ref = refs[nq:]
        cosv, sinv = cos_ref[...], sin_ref[...]

        def heads(x, g):
            outs = []
            for h in range(nh):
                y = _headnorm(x[:, h * HEAD_DIM:(h + 1) * HEAD_DIM], g)
                outs.append(y * cosv + _swap_halves(y) * sinv)
            return jnp.concatenate(outs, axis=1)

        qn_ref[...] = jnp.concatenate([heads(r[...], qg_ref[...]) for r in q_refs], axis=1).astype(bf16)
        kn_ref[...] = heads(k_ref[...], kg_ref[...]).astype(bf16)
        vb_ref[...] = v_ref[...].astype(bf16)

    ins = [("row", z, U, 1 + n) for n in range(nq)] + [("row", z, U, nq + 1), ("row", z, U, nq + 2), ("full", qg),
                                                       ("full", kg), ("row", cosf, HEAD_DIM, 0), ("row", sinf, HEAD_DIM, 0)]
    return rowcall(cfg, fn, name, ins, [("row", _sds((T, nq * U), bf16), nq * U, 0), ("row", _sds((T, U), bf16), U, 0),
                                        ("row", _sds((T, U), bf16), U, 0)])


def att_prep_bwd(cfg, z, qg, kg, cosf, sinf, dqn, dkn, dvb, dxpool, nq, name):
    T = z.shape[0]
    U = z.shape[1] // (nq + 3)
    nh = U // HEAD_DIM
    cT = cfg.cT

    def fn(i, *refs):
        q_refs = refs[:nq]
        (k_ref, qg_ref, kg_ref, cos_ref, sin_ref, dqn_ref, dkn_ref, dvb_ref, dxp_ref, dz_ref, dqg_ref, dkg_ref) = refs[nq:]
        cosv, sinv = cos_ref[...], sin_ref[...]

        @pl.when(i == 0)
        def _():
            dqg_ref[...] = jnp.zeros(dqg_ref.shape, f32)
            dkg_ref[...] = jnp.zeros(dkg_ref.shape, f32)

        def heads_bwd(x, g, dy, dg_ref):
            outs = []
            for h in range(nh):
                sl = slice(h * HEAD_DIM, (h + 1) * HEAD_DIM)
                d = dy[:, sl]
                dn = d * cosv + _swap_halves(d * sinv)
                _, vjp = jax.vjp(_headnorm, x[:, sl], g)
                dx, dg = vjp(dn)
                dg_ref[...] += dg
                outs.append(dx)
            return jnp.concatenate(outs, axis=1)

        dk = heads_bwd(k_ref[...], kg_ref[...], dkn_ref[...], dkg_ref)
        tail = [dk.astype(bf16), dvb_ref[...].astype(bf16)]

        @pl.when(i < cT)
        def _():
            zeros = jnp.zeros((cfg.TR, (nq + 1) * U), bf16)
            dz_ref[...] = jnp.concatenate([zeros] + tail, axis=1)

        @pl.when(i >= cT)
        def _():
            dq = [heads_bwd(r[...], qg_ref[...], dqn_ref[:, n * U:(n + 1) * U], dqg_ref) for n, r in enumerate(q_refs)]
            dz_ref[...] = jnp.concatenate([dxp_ref[...].astype(bf16)] + [t.astype(bf16) for t in dq] + tail, axis=1)

    ins = ([("row", z, U, 1 + n) for n in range(nq)] +
           [("row", z, U, nq + 1), ("full", qg), ("full", kg), ("row", cosf, HEAD_DIM, 0), ("row", sinf, HEAD_DIM, 0),
            ("row", dqn, nq * U, 0), ("row", dkn, U, 0), ("row", dvb, U, 0), ("row", dxpool, U, 0)])
    W = (nq + 3) * U
    return rowcall(cfg, fn, name, ins, [("row", _sds((T, W), bf16), W, 0), ("full", _sds((1, HEAD_DIM), f32)),
                                        ("full", _sds((1, HEAD_DIM), f32))])


def _stack_heads(x, n):
    return jnp.concatenate([x[:, h * HEAD_DIM:(h + 1) * HEAD_DIM] for h in range(n)], axis=0)


def _unstack_heads(x, n):
    rows = x.shape[0] // n
    return jnp.concatenate([x[h * rows:(h + 1) * rows] for h in range(n)], axis=1)


def _att_tiles(cfg, T):
    tq = cfg.TR
    tk = _pick(T, (768, 512, 256, 128))
    return tq, tk, (T - cfg.cT * cfg.TR) // tq, T // tk


def att_fwd(cfg, qn, kn, vb, nq, name):
    T, U = kn.shape
    KV = U // HEAD_DIM
    tq, tk, nQ, nK = _att_tiles(cfg, T)
    scale = HEAD_DIM ** -0.5
    R = nq * tq

    def body(q_ref, k_ref, v_ref, o_ref, lse_ref, m_sc, l_sc, acc):
        ik = pl.program_id(2)

        @pl.when(ik == 0)
        def _():
            m_sc[...] = jnp.full(m_sc.shape, -jnp.inf, f32)
            l_sc[...] = jnp.zeros(l_sc.shape, f32)
            acc[...] = jnp.zeros(acc.shape, f32)

        q3 = _stack_heads(q_ref[...], nq)
        s = _dotf(q3, k_ref[...], 1, 1) * scale
        m_new = jnp.maximum(m_sc[...], jnp.max(s, axis=-1, keepdims=True))
        alpha = jnp.exp(m_sc[...] - m_new)
        p = jnp.exp(s - m_new)
        l_sc[...] = alpha * l_sc[...] + jnp.sum(p, axis=-1, keepdims=True)
        acc[...] = alpha * acc[...] + jnp.dot(p.astype(bf16), v_ref[...], preferred_element_type=f32)
        m_sc[...] = m_new

        @pl.when(ik == nK - 1)
        def _():
            o_ref[...] = _unstack_heads(acc[...] / l_sc[...], nq)
            lse_ref[...] = m_sc[...] + jnp.log(l_sc[...])

    W = nq * HEAD_DIM
    q_spec = pl.BlockSpec((tq, W), lambda h, i, k: (i + cfg.cT, h))
    kv_spec = pl.BlockSpec((tk, HEAD_DIM), lambda h, i, k: (k, h))
    lse_spec = pl.BlockSpec((None, None, R, 1), lambda h, i, k: (h, i, 0, 0))
    return pl.pallas_call(
        body, name=name, grid=(KV, nQ, nK), in_specs=[q_spec, kv_spec, kv_spec], out_specs=[q_spec, lse_spec],
        out_shape=[_sds((T, nq * U), f32), _sds((KV, nQ, R, 1), f32)],
        scratch_shapes=[pltpu.VMEM((R, 1), f32), pltpu.VMEM((R, 1), f32), pltpu.VMEM((R, HEAD_DIM), f32)],
        compiler_params=_params(("arbitrary", "arbitrary", "arbitrary")))(qn, kn, vb)


def _att_probs(q3, k, lse, scale):
    return jnp.exp(_dotf(q3, k, 1, 1) * scale - lse)


def att_bwd_dq(cfg, qn, kn, vb, o, lse, do, nq, name):
    T, U = kn.shape
    KV = U // HEAD_DIM
    tq, tk, nQ, nK = _att_tiles(cfg, T)
    scale = HEAD_DIM ** -0.5
    R = nq * tq
    W = nq * HEAD_DIM

    def body(q_ref, k_ref, v_ref, o_ref, lse_ref, do_ref, dq_ref, acc, dl):
        ik = pl.program_id(2)
        do3 = _stack_heads(do_ref[...], nq)

        @pl.when(ik == 0)
        def _():
            acc[...] = jnp.zeros(acc.shape, f32)
            dl[...] = jnp.sum(do3 * _stack_heads(o_ref[...], nq), axis=-1, keepdims=True)

        k = k_ref[...]
        p = _att_probs(_stack_heads(q_ref[...], nq), k, lse_ref[...], scale)
        dp = _dotf(do3.astype(bf16), v_ref[...], 1, 1)
        ds = (p * (dp - dl[...]) * scale).astype(bf16)
        acc[...] += jnp.dot(ds, k, preferred_element_type=f32)

        @pl.when(ik == nK - 1)
        def _():
            dq_ref[...] = _unstack_heads(acc[...], nq)

    q_spec = pl.BlockSpec((tq, W), lambda h, i, k: (i + cfg.cT, h))
    kv_spec = pl.BlockSpec((tk, HEAD_DIM), lambda h, i, k: (k, h))
    lse_spec = pl.BlockSpec((None, None, R, 1), lambda h, i, k: (h, i, 0, 0))
    return pl.pallas_call(
        body, name=name, grid=(KV, nQ, nK), in_specs=[q_spec, kv_spec, kv_spec, q_spec, lse_spec, q_spec], out_specs=q_spec,
        out_shape=_sds((T, nq * U), f32), scratch_shapes=[pltpu.VMEM((R, HEAD_DIM), f32), pltpu.VMEM((R, 1), f32)],
        compiler_params=_params(("arbitrary", "arbitrary", "arbitrary")))(qn, kn, vb, o, lse, do)


def att_bwd_dkv(cfg, qn, kn, vb, o, lse, do, nq, name):
    T, U = kn.shape
    KV = U // HEAD_DIM
    tq, tk, nQ, nK = _att_tiles(cfg, T)
    scale = HEAD_DIM ** -0.5
    R = nq * tq
    W = nq * HEAD_DIM

    def body(q_ref, k_ref, v_ref, o_ref, lse_ref, do_ref, dk_ref, dv_ref, dk_acc, dv_acc):
        iq = pl.program_id(2)

        @pl.when(iq == 0)
        def _():
            dk_acc[...] = jnp.zeros(dk_acc.shape, f32)
            dv_acc[...] = jnp.zeros(dv_acc.shape, f32)

        q3 = _stack_heads(q_ref[...], nq)
        do3 = _stack_heads(do_ref[...], nq)
        dl = jnp.sum(do3 * _stack_heads(o_ref[...], nq), axis=-1, keepdims=True)
        p = _att_probs(q3, k_ref[...], lse_ref[...], scale)
        do_bf = do3.astype(bf16)
        dv_acc[...] += _dotf(p.astype(bf16), do_bf, 0, 0)
        dp = _dotf(do_bf, v_ref[...], 1, 1)
        ds = (p * (dp - dl) * scale).astype(bf16)
        dk_acc[...] += _dotf(ds, q3, 0, 0)

        @pl.when(iq == nQ - 1)
        def _():
            dk_ref[...] = dk_acc[...]
            dv_ref[...] = dv_acc[...]

    q_spec = pl.BlockSpec((tq, W), lambda h, k, i: (i + cfg.cT, h))
    kv_spec = pl.BlockSpec((tk, HEAD_DIM), lambda h, k, i: (k, h))
    lse_spec = pl.BlockSpec((None, None, R, 1), lambda h, k, i: (h, i, 0, 0))
    return pl.pallas_call(
        body, name=name, grid=(KV, nK, nQ), in_specs=[q_spec, kv_spec, kv_spec, q_spec, lse_spec, q_spec],
        out_specs=[kv_spec, kv_spec], out_shape=[_sds((T, U), f32), _sds((T, U), f32)],
        scratch_shapes=[pltpu.VMEM((tk, HEAD_DIM), f32), pltpu.VMEM((tk, HEAD_DIM), f32)],
        compiler_params=_params(("arbitrary", "arbitrary", "arbitrary")))(qn, kn, vb, o, lse, do)


def od_mix(cfg, pooled, o, name):
    T, P = pooled.shape
    QW = o.shape[1]
    cT = cfg.cT

    def fn(i, p_ref, o_ref, m_ref):
        @pl.when(i < cT)
        def _():
            m_ref[...] = jnp.zeros(m_ref.shape, bf16)

        @pl.when(i >= cT)
        def _():
            m_ref[...] = jnp.concatenate([p_ref[...], o_ref[...].astype(bf16)], axis=1)

    return rowcall(cfg, fn, name, [("row", pooled, P, 0), ("row", o, QW, 0)], [("row", _sds((T, P + QW), bf16), P + QW, 0)])[0]


def ev_mix(cfg, lru, ret, name):
    T, W = lru.shape
    RV = ret.shape[1]

    def fn(i, a_ref, b_ref, m_ref):
        m_ref[...] = jnp.concatenate([a_ref[...], b_ref[...]], axis=1)

    return rowcall(cfg, fn, name, [("row", lru, W, 0), ("row", ret, RV, 0)], [("row", _sds((T, W + RV), bf16), W + RV, 0)])[0]


def ev_dz_pack(cfg, dgl, dr, dq, dk, dv, dol, name):
    T, W = dgl.shape
    RV = dol.shape[1]
    width = 2 * W + 4 * RV

    def fn(i, g_ref, r_ref, q_ref, k_ref, v_ref, o_ref, dz_ref):
        parts = [g_ref[...], r_ref[...], q_ref[0] + q_ref[1], k_ref[0] + k_ref[1], v_ref[0] + v_ref[1], o_ref[...]]
        dz_ref[...] = jnp.concatenate([p.astype(bf16) for p in parts], axis=1)

    return rowcall(cfg, fn, name, [("row", dgl, W, 0), ("row", dr, W, 0), ("drow", dq, RV, 0), ("drow", dk, RV, 0),
                                   ("drow", dv, RV, 0), ("row", dol, RV, 0)], [("row", _sds((T, width), bf16), width, 0)])[0]


def loss_fwd_bwd(cfg, xf, target, name):
    T, D = xf.shape
    TR, cT = cfg.TR, cfg.cT

    def body(x_ref, t_ref, sq_ref, dx_ref):
        i = pl.program_id(0)

        @pl.when(i == 0)
        def _():
            sq_ref[...] = jnp.zeros(sq_ref.shape, f32)

        @pl.when(i < cT)
        def _():
            dx_ref[...] = jnp.zeros(dx_ref.shape, f32)

        @pl.when(i >= cT)
        def _():
            diff = x_ref[...] - t_ref[...]
            sq_ref[...] += jnp.sum(diff * diff, axis=0, keepdims=True)
            dx_ref[...] = diff / D

    row = pl.BlockSpec((TR, D), lambda i: (i, 0))
    trow = pl.BlockSpec((TR, D), lambda i: (jnp.maximum(i - cT, 0), 0))
    return pl.pallas_call(body, name=name, grid=(cfg.nT,), in_specs=[row, trow],
                          out_specs=[pl.BlockSpec((1, D), lambda i: (0, 0)), row],
                          out_shape=[_sds((1, D), f32), _sds((T, D), f32)], compiler_params=_params(("arbitrary",)))(xf, target)


MOD_ROWS = 16


def mod_fwd(s16, mod_w, name):
    nL, D, C4 = mod_w.shape
    tc = _pick(C4, (512, 256, 128))

    def body(s_ref, w_ref, o_ref):
        o_ref[...] = jnp.dot(s_ref[...], w_ref[...], precision=lax.Precision.HIGHEST, preferred_element_type=f32)

    return pl.pallas_call(
        body, name=name, grid=(nL, C4 // tc),
        in_specs=[pl.BlockSpec((MOD_ROWS, D), lambda l, j: (0, 0)), pl.BlockSpec((None, D, tc), lambda l, j: (l, 0, j))],
        out_specs=pl.BlockSpec((None, MOD_ROWS, tc), lambda l, j: (l, 0, j)), out_shape=_sds((nL, MOD_ROWS, C4), f32),
        compiler_params=_params(("arbitrary", "arbitrary")))(s16, mod_w)


def mod_bwd(s16, dm16, mod_w, name):
    nL, D, C4 = mod_w.shape
    tc = _pick(C4, (512, 256, 128))
    half = MOD_ROWS // 2

    def body(s_ref, d_ref, w_ref, g_ref, dc_ref):
        first = jnp.logical_and(pl.program_id(0) == 0, pl.program_id(1) == 0)
        g_ref[...] = lax.dot_general(s_ref[...], d_ref[...], (((0,), (0,)), ((), ())), precision=lax.Precision.HIGHEST,
                                     preferred_element_type=f32)
        part = lax.dot_general(d_ref[...], w_ref[...], (((1,), (1,)), ((), ())), precision=lax.Precision.HIGHEST,
                               preferred_element_type=f32)
        _acc(dc_ref, jnp.sum(part[half:], axis=0, keepdims=True), first)

    return pl.pallas_call(
        body, name=name, grid=(nL, C4 // tc),
        in_specs=[pl.BlockSpec((MOD_ROWS, D), lambda l, j: (0, 0)), pl.BlockSpec((None, MOD_ROWS, tc), lambda l, j: (l, 0, j)),
                  pl.BlockSpec((None, D, tc), lambda l, j: (l, 0, j))],
        out_specs=[pl.BlockSpec((None, D, tc), lambda l, j: (l, 0, j)), pl.BlockSpec((1, D), lambda l, j: (0, 0))],
        out_shape=[_sds((nL, D, C4), f32), _sds((1, D), f32)],
        compiler_params=_params(("arbitrary", "arbitrary")))(s16, dm16, mod_w)


def _as2d(a):
    return a.reshape(-1, a.shape[-1])


def _tiles2d(shape):
    R, C = shape
    return _pick(R, (256, 128, 64, 32, 16, 8)), _pick(C, (1536, 1408, 1024, 768, 512, 256, 128))


def cast_bf16(a, name):
    a2 = _as2d(a)
    tr, tc = _tiles2d(a2.shape)

    def body(a_ref, o_ref):
        o_ref[...] = a_ref[...].astype(bf16)

    spec = pl.BlockSpec((tr, tc), lambda i, j: (i, j))
    out = pl.pallas_call(body, name=name, grid=(a2.shape[0] // tr, a2.shape[1] // tc), in_specs=[spec], out_specs=spec,
                         out_shape=_sds(a2.shape, bf16), compiler_params=_params(("arbitrary", "arbitrary")))(a2)
    return out.reshape(a.shape)


def sum_leading(a, name):
    n = a.shape[0]
    a3 = a.reshape(n, -1, a.shape[-1])
    tr, tc = _tiles2d(a3.shape[1:])

    def body(a_ref, o_ref):
        tot = a_ref[0].astype(f32)
        for k in range(1, n):
            tot = tot + a_ref[k].astype(f32)
        o_ref[...] = tot

    out = pl.pallas_call(body, name=name, grid=(a3.shape[1] // tr, a3.shape[2] // tc),
                         in_specs=[pl.BlockSpec((n, tr, tc), lambda i, j: (0, i, j))],
                         out_specs=pl.BlockSpec((tr, tc), lambda i, j: (i, j)), out_shape=_sds(a3.shape[1:], f32),
                         compiler_params=_params(("arbitrary", "arbitrary")))(a3)
    return out.reshape(a.shape[1:])


def adamw(w, m, v, g_parts, name):
    w2, m2, v2 = _as2d(w), _as2d(m), _as2d(v)
    parts = [_as2d(p) for p in g_parts]
    tr, tc = _tiles2d(w2.shape)
    npart = len(parts)

    def body(*refs):
        w_ref, m_ref, v_ref = refs[:3]
        p_refs = refs[3:3 + npart]
        g_ref, d_ref, nm_ref, nv_ref = refs[3 + npart:]
        g = p_refs[0][...]
        for p in p_refs[1:]:
            g = g + p[...]
        mn = ADAM_B1 * m_ref[...] + (1.0 - ADAM_B1) * g
        vn = ADAM_B2 * v_ref[...] + (1.0 - ADAM_B2) * jnp.square(g)
        m_hat = mn / (1.0 - ADAM_B1 ** ADAM_STEP)
        v_hat = vn / (1.0 - ADAM_B2 ** ADAM_STEP)
        g_ref[...] = g
        d_ref[...] = -ADAM_LR * (m_hat / (jnp.sqrt(v_hat) + ADAM_EPS) + ADAM_WD * w_ref[...])
        nm_ref[...] = mn
        nv_ref[...] = vn

    spec = pl.BlockSpec((tr, tc), lambda i, j: (i, j))
    outs = pl.pallas_call(body, name=name, grid=(w2.shape[0] // tr, w2.shape[1] // tc), in_specs=[spec] * (3 + npart),
                          out_specs=[spec] * 4, out_shape=[_sds(w2.shape, f32)] * 4,
                          compiler_params=_params(("arbitrary", "arbitrary")))(w2, m2, v2, *parts)
    return [o.reshape(w.shape) for o in outs]


def _coords():
    return lax.axis_index("x"), lax.axis_index("y"), lax.axis_index("c")


def _flip(v, bit):
    return 1 - v if bit else v


def all_gather_small(a, name):
    R, C = a.shape

    def body(a_ref, out_ref, send_sems, recv_sems, local_sem):
        x, y, c = _coords()
        me = 4 * x + 2 * y + c
        mine = pltpu.make_async_copy(a_ref, out_ref.at[me], local_sem)
        mine.start()
        copies = []
        for k in range(1, N_DEV):
            kx, ky, kc = (k >> 2) & 1, (k >> 1) & 1, k & 1
            peer = (_flip(x, kx), _flip(y, ky), _flip(c, kc))
            cp = pltpu.make_async_remote_copy(src_ref=a_ref, dst_ref=out_ref.at[me], send_sem=send_sems.at[k - 1],
                                              recv_sem=recv_sems.at[k - 1], device_id=peer, device_id_type=MESH)
            cp.start()
            copies.append((cp, 4 * peer[0] + 2 * peer[1] + peer[2], peer))
        for k, (cp, pidx, peer) in enumerate(copies):
            pltpu.make_async_remote_copy(src_ref=a_ref, dst_ref=out_ref.at[pidx], send_sem=send_sems.at[k],
                                         recv_sem=recv_sems.at[k], device_id=peer, device_id_type=MESH).wait_recv()
        for cp, _, _ in copies:
            cp.wait_send()
        mine.wait()

    return pl.pallas_call(
        body, name=name, out_shape=_sds((N_DEV, R, C), f32),
        in_specs=[pl.BlockSpec(memory_space=pltpu.VMEM)], out_specs=pl.BlockSpec(memory_space=pltpu.VMEM),
        scratch_shapes=[pltpu.SemaphoreType.DMA((N_DEV - 1,)), pltpu.SemaphoreType.DMA((N_DEV - 1,)), pltpu.SemaphoreType.DMA],
        compiler_params=pltpu.CompilerParams(vmem_limit_bytes=VMEM_LIMIT))(a)


def _chip_peers(x, y, c):
    out = []
    for k in range(1, N_CHIPS):
        kx, ky = (k >> 1) & 1, k & 1
        px, py = _flip(x, kx), _flip(y, ky)
        out.append((k, (px, py, c), 2 * px + py))
    return out


def gather_weights(shards, name):
    n = len(shards)

    def body(*refs):
        in_refs, out_refs = refs[:n], refs[n:2 * n]
        send_sems, recv_sems, local_sems = refs[2 * n:]
        x, y, c = _coords()
        s = 2 * x + y
        local = []
        for w in range(n):
            cp = pltpu.make_async_copy(in_refs[w], out_refs[w].at[s], local_sems.at[w])
            cp.start()
            local.append(cp)
        sends = []
        for w in range(n):
            for k, peer, pidx in _chip_peers(x, y, c):
                j = w * (N_CHIPS - 1) + k - 1
                cp = pltpu.make_async_remote_copy(src_ref=in_refs[w], dst_ref=out_refs[w].at[s], send_sem=send_sems.at[j],
                                                  recv_sem=recv_sems.at[j], device_id=peer, device_id_type=MESH)
                cp.start()
                sends.append((cp, w, j, peer, pidx))
        for cp, w, j, peer, pidx in sends:
            pltpu.make_async_remote_copy(src_ref=in_refs[w], dst_ref=out_refs[w].at[pidx], send_sem=send_sems.at[j],
                                         recv_sem=recv_sems.at[j], device_id=peer, device_id_type=MESH).wait_recv()
        for cp, *_ in sends:
            cp.wait_send()
        for cp in local:
            cp.wait()

    hbm = pl.BlockSpec(memory_space=pl.ANY)
    nsem = n * (N_CHIPS - 1)
    return pl.pallas_call(
        body, name=name, out_shape=[_sds((N_CHIPS,) + a.shape, a.dtype) for a in shards],
        in_specs=[hbm] * n, out_specs=[hbm] * n,
        scratch_shapes=[pltpu.SemaphoreType.DMA((nsem,)), pltpu.SemaphoreType.DMA((nsem,)), pltpu.SemaphoreType.DMA((n,))],
        )(*shards)


def scatter_grads(grads, name):
    n = len(grads)

    def body(*refs):
        in_refs, out_refs = refs[:n], refs[n:2 * n]
        send_sems, recv_sems, local_sems = refs[2 * n:]
        x, y, c = _coords()
        s = 2 * x + y
        local = []
        for w in range(n):
            cp = pltpu.make_async_copy(in_refs[w].at[s], out_refs[w].at[N_CHIPS - 1], local_sems.at[w])
            cp.start()
            local.append(cp)
        sends = []
        for w in range(n):
            for k, peer, pidx in _chip_peers(x, y, c):
                j = w * (N_CHIPS - 1) + k - 1
                cp = pltpu.make_async_remote_copy(src_ref=in_refs[w].at[pidx], dst_ref=out_refs[w].at[k - 1],
                                                  send_sem=send_sems.at[j], recv_sem=recv_sems.at[j], device_id=peer,
                                                  device_id_type=MESH)
                cp.start()
                sends.append(cp)
        for cp in sends:
            cp.wait_recv()
        for cp in sends:
            cp.wait_send()
        for cp in local:
            cp.wait()

    hbm = pl.BlockSpec(memory_space=pl.ANY)
    nsem = n * (N_CHIPS - 1)
    return pl.pallas_call(
        body, name=name, out_shape=[_sds(a.shape, a.dtype) for a in grads], in_specs=[hbm] * n, out_specs=[hbm] * n,
        scratch_shapes=[pltpu.SemaphoreType.DMA((nsem,)), pltpu.SemaphoreType.DMA((nsem,)), pltpu.SemaphoreType.DMA((n,))],
        )(*grads)


def swap_with_sibling(parts, name):
    n = len(parts)

    def body(*refs):
        in_refs, out_refs = refs[:n], refs[n:2 * n]
        send_sems, recv_sems = refs[2 * n:]
        x, y, c = _coords()
        sends = []
        for w in range(n):
            cp = pltpu.make_async_remote_copy(src_ref=in_refs[w], dst_ref=out_refs[w], send_sem=send_sems.at[w],
                                              recv_sem=recv_sems.at[w], device_id=(x, y, 1 - c), device_id_type=MESH)
            cp.start()
            sends.append(cp)
        for cp in sends:
            cp.wait_recv()
        for cp in sends:
            cp.wait_send()

    hbm = pl.BlockSpec(memory_space=pl.ANY)
    return pl.pallas_call(
        body, name=name, out_shape=[_sds(a.shape, a.dtype) for a in parts], in_specs=[hbm] * n, out_specs=[hbm] * n,
        scratch_shapes=[pltpu.SemaphoreType.DMA((n,)), pltpu.SemaphoreType.DMA((n,))],
        )(*parts)


def even_fwd(cfg, x, p, tag):
    W, H = p["W"], p["H"]
    h = pre_fwd(cfg, x, p["g_pre"], p["shift"], p["scale"], tag + "_pre")
    z = matmul("v1", h, p["w_in"], widx=(0,), name=tag + "_in")
    u = conv_fwd(cfg, z, p["conv_w"], p["conv_b"], W, tag + "_conv")
    a, b = lru_coef_fwd(cfg, u, p["wa"], p["ba"], p["wx"], p["bx"], p["lam"], tag + "_coef")
    hh, hp = lru_scan(cfg, a, b, tag + "_scan")
    lru = lru_out_fwd(cfg, z, hh, W, tag + "_lruout")
    qcol = 2 * W // RET_DK
    o, st = ret_fwd(cfg, z, p["logit"], p["cos1"], p["sin1"], H, qcol, tag + "_ret")
    olcol = (2 * W + 3 * H * RET_DK) // (H * RET_DV)
    ret = ret_norm_fwd(cfg, o, z, p["gn"], H, olcol, tag + "_retnorm")
    mix = ev_mix(cfg, lru, ret, tag + "_mix")
    y = matmul("v2", mix, p["w_out"], widx=(0,), name=tag + "_out")
    xo = post_fwd(cfg, x, y, p["g_post"], p["gate"], 1.0, tag + "_post")
    return xo, (x, h, z, u, a, hh, hp, o, st, mix, y, olcol, qcol)


def even_bwd(cfg, dX, saved, p, gbuf, tag):
    x, h, z, u, a, hh, hp, o, st, mix, y, olcol, qcol = saved
    W, H = p["W"], p["H"]
    dy, dg_post, dgate = post_bwd(cfg, dX, y, p["g_post"], p["gate"], 1.0, tag + "_postb")
    dmix = matmul("v4", dy, p["w_out"], widx=(0,), name=tag + "_dmix")
    gbuf["w_out"] = matmul("v6", mix, dy, dst=gbuf["w_out"], widx=(0,), name=tag + "_gwout")
    dgl, dhs = lru_out_bwd(cfg, z, hh, dmix, W, tag + "_lruoutb")
    da, db = lru_scan_bwd(cfg, a, hp, dhs, tag + "_scanb")
    du, dwa, dba, dwx, dbx, dlam = lru_coef_bwd(cfg, u, da, db, p["wa"], p["ba"], p["wx"], p["bx"], p["lam"], tag + "_coefb")
    dr, dcw, dcb = conv_bwd(cfg, z, du, p["conv_w"], W, tag + "_convb")
    do, dol, dgn = ret_norm_bwd(cfg, o, z, p["gn"], dmix, H, olcol, W // (H * RET_DV), tag + "_retnormb")
    dq, dk, dv, dlg = ret_bwd(cfg, z, st, do, p["logit"], p["cos1"], p["sin1"], H, qcol, tag + "_retb")
    dz = ev_dz_pack(cfg, dgl, dr, dq, dk, dv, dol, tag + "_dz")
    gbuf["w_in"] = matmul("v5", h, dz, dst=gbuf["w_in"], widx=(0,), name=tag + "_gwin")
    dh = matmul("v3", dz, p["w_in"], widx=(0,), name=tag + "_dh")
    dX, dg_pre, dshift, dscale = pre_bwd(cfg, x, p["g_pre"], p["shift"], p["scale"], dh, dX, tag + "_preb")
    pg = dict(g_pre=dg_pre, g_post=dg_post, shift=dshift, scale=dscale, gate=dgate, conv_w=dcw, conv_b=dcb, wa=dwa,
              ba=dba, wx=dwx, bx=dbx, lam=dlam, logit=dlg, gn=dgn)
    return dX, pg, gbuf


def odd_fwd(cfg, x, p, tag):
    nq = p["nq"]
    h = pre_fwd(cfg, x, p["g_pre"], p["shift"], p["scale"], tag + "_pre")
    z = matmul("v1", h, p["w_in"], widx=(0,), name=tag + "_in")
    pooled = pool_fwd(cfg, z, p["pool_w"], p["pool_scale"], tag + "_pool")
    qn, kn, vb = att_prep(cfg, z, p["qg"], p["kg"], p["cosf"], p["sinf"], nq, tag + "_prep")
    o, lse = att_fwd(cfg, qn, kn, vb, nq, tag + "_att")
    mix = od_mix(cfg, pooled, o, tag + "_mix")
    y = matmul("v2", mix, p["w_out"], widx=(0,), name=tag + "_out")
    xo = post_fwd(cfg, x, y, p["g_post"], p["gate"], 1.0, tag + "_post")
    return xo, (x, h, z, qn, kn, vb, o, lse, mix, y)


def odd_bwd(cfg, dX, saved, p, gbuf, tag):
    x, h, z, qn, kn, vb, o, lse, mix, y = saved
    nq = p["nq"]
    U = kn.shape[1]
    dy, dg_post, dgate = post_bwd(cfg, dX, y, p["g_post"], p["gate"], 1.0, tag + "_postb")
    dmix = matmul("v4", dy, p["w_out"], widx=(0,), name=tag + "_dmix")
    gbuf["w_out"] = matmul("v6", mix, dy, dst=gbuf["w_out"], widx=(0,), name=tag + "_gwout")
    dm, dmn, dpw, dps = pool_bwd_a(cfg, z, dmix, p["pool_w"], p["pool_scale"], tag + "_poolb")
    dxp = pool_bwd_b(cfg, dm, dmn, tag + "_poolb2")
    do = dmix[:, U:]
    dqn = att_bwd_dq(cfg, qn, kn, vb, o, lse, do, nq, tag + "_attdq")
    dkn, dvb = att_bwd_dkv(cfg, qn, kn, vb, o, lse, do, nq, tag + "_attdkv")
    dz, dqg, dkg = att_prep_bwd(cfg, z, p["qg"], p["kg"], p["cosf"], p["sinf"], dqn, dkn, dvb, dxp, nq, tag + "_prepb")
    gbuf["w_in"] = matmul("v5", h, dz, dst=gbuf["w_in"], widx=(0,), name=tag + "_gwin")
    dh = matmul("v3", dz, p["w_in"], widx=(0,), name=tag + "_dh")
    dX, dg_pre, dshift, dscale = pre_bwd(cfg, x, p["g_pre"], p["shift"], p["scale"], dh, dX, tag + "_preb")
    pg = dict(g_pre=dg_pre, g_post=dg_post, shift=dshift, scale=dscale, gate=dgate, pool_w=dpw, pool_scale=dps, qg=dqg, kg=dkg)
    return dX, pg, gbuf


WEIGHT_NAMES = ("c_ctx", "mod_w", "mod_b", "norm_pre", "norm_post", "ffn_gate", "ffn_up", "ffn_down", "ev_w_in", "ev_w_out",
                "lru_conv_w", "lru_conv_b", "lru_wa", "lru_ba", "lru_wx", "lru_bx", "lru_lambda", "ret_decay_logit", "ret_gn",
                "od_w_in", "od_w_out", "pool_w", "pool_scale", "q_norm", "k_norm")
BIG = ("ffn_gate", "ffn_up", "ffn_down", "ev_w_in", "ev_w_out", "od_w_in", "od_w_out")
SMALL_SHARDED = ("norm_pre", "norm_post", "lru_conv_w", "lru_ba", "lru_bx", "lru_lambda", "pool_scale")
SMALL_REPL = ("mod_b", "lru_conv_b", "lru_wa", "lru_wx", "ret_decay_logit", "ret_gn", "pool_w", "q_norm", "k_norm")
LANES = 128


def _pack(arrs):
    flat = jnp.concatenate([a.reshape(-1) for a in arrs])
    n = flat.shape[0]
    pad = (-n) % (8 * LANES)
    return jnp.pad(flat, (0, pad)).reshape(-1, LANES), [a.size for a in arrs]


def _unpack(packed, shapes, lead=()):
    flat = packed.reshape(lead + (-1,))
    out, pos = [], 0
    for shp in shapes:
        n = math.prod(shp)
        out.append(flat[..., pos:pos + n].reshape(lead + tuple(shp)))
        pos += n
    return out


def _unshard(g):
    return jnp.moveaxis(g, 0, -2).reshape(g.shape[1:-1] + (g.shape[0] * g.shape[-1],))


def _rope_tables(S, Lc):
    n_r = RET_DK // 2
    f_r = RET_THETA ** (-jnp.arange(n_r, dtype=f32) / n_r)
    ang1 = jnp.arange(S, dtype=f32)[:, None] * f_r
    rows = S // GRID_W
    row = jnp.repeat(jnp.arange(rows, dtype=f32), GRID_W)
    col = jnp.tile(jnp.arange(GRID_W, dtype=f32), rows)
    n_ax = HEAD_DIM // 4
    f_ax = ROPE_THETA ** (-jnp.arange(n_ax, dtype=f32) / n_ax)
    ang2 = jnp.concatenate([row[:, None] * f_ax, col[:, None] * f_ax], axis=-1)
    cos2, sin2 = jnp.cos(ang2), jnp.sin(ang2)
    ones = lambda n: jnp.ones((Lc, n), f32)
    zeros = lambda n: jnp.zeros((Lc, n), f32)
    cos1 = jnp.concatenate([ones(n_r), jnp.cos(ang1)])
    sin1 = jnp.concatenate([zeros(n_r), jnp.sin(ang1)])
    cosf = jnp.concatenate([ones(HEAD_DIM), jnp.concatenate([cos2, cos2], axis=1)])
    sinf = jnp.concatenate([zeros(HEAD_DIM), jnp.concatenate([-sin2, sin2], axis=1)])
    return cos1, sin1, cosf, sinf


def kernel(x, c, ctx, c_ctx, mod_w, mod_b, norm_pre, norm_post, ffn_gate, ffn_up, ffn_down, ev_w_in, ev_w_out, lru_conv_w, lru_conv_b, lru_wa, lru_ba, lru_wx, lru_bx, lru_lambda, ret_decay_logit, ret_gn, od_w_in, od_w_out, pool_w, pool_scale, q_norm, k_norm, loss_target, m_c_ctx, m_mod_w, m_mod_b, m_norm_pre, m_norm_post, m_ffn_gate, m_ffn_up, m_ffn_down, m_ev_w_in, m_ev_w_out, m_lru_conv_w, m_lru_conv_b, m_lru_wa, m_lru_ba, m_lru_wx, m_lru_bx, m_lru_lambda, m_ret_decay_logit, m_ret_gn, m_od_w_in, m_od_w_out, m_pool_w, m_pool_scale, m_q_norm, m_k_norm, v_c_ctx, v_mod_w, v_mod_b, v_norm_pre, v_norm_post, v_ffn_gate, v_ffn_up, v_ffn_down, v_ev_w_in, v_ev_w_out, v_lru_conv_w, v_lru_conv_b, v_lru_wa, v_lru_ba, v_lru_wx, v_lru_bx, v_lru_lambda, v_ret_decay_logit, v_ret_gn, v_od_w_in, v_od_w_out, v_pool_w, v_pool_scale, v_q_norm, v_k_norm):
    wts = dict(c_ctx=c_ctx, mod_w=mod_w, mod_b=mod_b, norm_pre=norm_pre, norm_post=norm_post, ffn_gate=ffn_gate, ffn_up=ffn_up,
               ffn_down=ffn_down, ev_w_in=ev_w_in, ev_w_out=ev_w_out, lru_conv_w=lru_conv_w, lru_conv_b=lru_conv_b,
               lru_wa=lru_wa, lru_ba=lru_ba, lru_wx=lru_wx, lru_bx=lru_bx, lru_lambda=lru_lambda,
               ret_decay_logit=ret_decay_logit, ret_gn=ret_gn, od_w_in=od_w_in, od_w_out=od_w_out, pool_w=pool_w,
               pool_scale=pool_scale, q_norm=q_norm, k_norm=k_norm)
    mom_m = dict(zip(WEIGHT_NAMES, (m_c_ctx, m_mod_w, m_mod_b, m_norm_pre, m_norm_post, m_ffn_gate, m_ffn_up, m_ffn_down,
                                    m_ev_w_in, m_ev_w_out, m_lru_conv_w, m_lru_conv_b, m_lru_wa, m_lru_ba, m_lru_wx, m_lru_bx,
                                    m_lru_lambda, m_ret_decay_logit, m_ret_gn, m_od_w_in, m_od_w_out, m_pool_w, m_pool_scale,
                                    m_q_norm, m_k_norm)))
    mom_v = dict(zip(WEIGHT_NAMES, (v_c_ctx, v_mod_w, v_mod_b, v_norm_pre, v_norm_post, v_ffn_gate, v_ffn_up, v_ffn_down,
                                    v_ev_w_in, v_ev_w_out, v_lru_conv_w, v_lru_conv_b, v_lru_wa, v_lru_ba, v_lru_wx, v_lru_bx,
                                    v_lru_lambda, v_ret_decay_logit, v_ret_gn, v_od_w_in, v_od_w_out, v_pool_w, v_pool_scale,
                                    v_q_norm, v_k_norm)))

    _, S, D = x.shape
    Lc = ctx.shape[1]
    T = Lc + S
    TR = 256 if (Lc % 256 == 0 and S % 256 == 0) else 128
    assert Lc % TR == 0 and S % TR == 0 and TR % RET_CHUNK == 0
    cfg = RowCfg(TR, T // TR, Lc // TR)
    W = lru_conv_b.shape[-1]
    H = ret_decay_logit.shape[-1]
    U = POOL_GROUP * len(POOL_WINDOWS)
    nq = (N_CHIPS * od_w_in.shape[-1]) // U - 3
    assert W % (H * RET_DV) == 0 and (2 * W) % RET_DK == 0
    nL = mod_w.shape[0]
    C4 = mod_w.shape[-1]
    assert nL == 2, "two layers: an even mixer then an odd one"

    xi, yi, ci = _coords()
    chip = 2 * xi + yi
    me = 4 * xi + 2 * yi + ci

    sc = jax.nn.silu(c)
    small_in, _ = _pack([sc] + [wts[n] for n in SMALL_SHARDED])
    g1 = all_gather_small(small_in, "gather_small_fwd")
    parts = _unpack(g1, [sc.shape] + [wts[n].shape for n in SMALL_SHARDED], lead=(N_DEV,))
    sc_all = parts[0][:, 0]
    full = {n: _unshard(parts[1 + i][0::2]) for i, n in enumerate(SMALL_SHARDED)}
    for n in SMALL_REPL + ("c_ctx",):
        full[n] = wts[n]

    scc = jax.nn.silu(c_ctx)[None]
    pad_rows = MOD_ROWS - N_DEV - 1
    s16 = jnp.concatenate([sc_all, scc, jnp.zeros((pad_rows, D), f32)])
    modp = mod_fwd(s16, mod_w, "mod_fwd")
    g2 = all_gather_small(modp.reshape(-1, LANES), "gather_mod")
    mod_all = g2.reshape(N_DEV, nL, MOD_ROWS, C4)[0::2]
    mod_all = jnp.moveaxis(mod_all, 0, 2).reshape(nL, MOD_ROWS, N_CHIPS * C4) + mod_b[:, None, :]
    mod_l = lax.dynamic_index_in_dim(mod_all, me, axis=1, keepdims=False).reshape(nL, 3, 3, D)
    mod_c = mod_all[:, N_DEV].reshape(nL, 3, 3, D)

    def mod_of(li, s, kind, ctx_live=True):
        cpart = mod_c[li, s, kind] if ctx_live else jnp.zeros((D,), f32)
        return jnp.stack([cpart, mod_l[li, s, kind]])[:, None, :]

    packed = {n: cast_bf16(wts[n], "cast_" + n) for n in BIG}
    gathered = dict(zip(BIG, gather_weights([packed[n] for n in BIG], "gather_weights")))

    cos1, sin1, cosf, sinf = _rope_tables(S, Lc)

    def sub_params(li, s, ctx_live=True, gate_ctx_live=True):
        return dict(g_pre=full["norm_pre"][li, s][None], g_post=full["norm_post"][li, s][None],
                    shift=mod_of(li, s, 0, ctx_live), scale=mod_of(li, s, 1, ctx_live),
                    gate=mod_of(li, s, 2, ctx_live and gate_ctx_live))

    def ffn_params(li, k, s, ctx_live=True):
        p = sub_params(li, s, ctx_live)
        p.update(wg=gathered["ffn_gate"], wu=gathered["ffn_up"], wd=gathered["ffn_down"], widx=(li, k))
        return p

    p00, p02, p10, p12 = ffn_params(0, 0, 0), ffn_params(0, 1, 2), ffn_params(1, 0, 0), ffn_params(1, 1, 2, ctx_live=False)
    p01 = sub_params(0, 1)
    p01.update(W=W, H=H, w_in=gathered["ev_w_in"], w_out=gathered["ev_w_out"], conv_w=full["lru_conv_w"][0],
               conv_b=full["lru_conv_b"], wa=full["lru_wa"][0], ba=full["lru_ba"][0][:, None, :], wx=full["lru_wx"][0],
               bx=full["lru_bx"][0][:, None, :], lam=full["lru_lambda"][0][:, None, :],
               logit=full["ret_decay_logit"][0][:, :, None, None], gn=full["ret_gn"], cos1=cos1, sin1=sin1)
    p11 = sub_params(1, 1, gate_ctx_live=False)
    p11.update(nq=nq, w_in=gathered["od_w_in"], w_out=gathered["od_w_out"], pool_w=full["pool_w"][0],
               pool_scale=full["pool_scale"], qg=full["q_norm"], kg=full["k_norm"], cosf=cosf, sinf=sinf)

    X0 = jnp.concatenate([ctx[0], x[0]], axis=0)
    X1, s00 = ffn_fwd(cfg, X0, p00, "l0f0")
    X2, s01 = even_fwd(cfg, X1, p01, "l0mix")
    X3, s02 = ffn_fwd(cfg, X2, p02, "l0f1")
    X4, s10 = ffn_fwd(cfg, X3, p10, "l1f0")
    X5, s11 = odd_fwd(cfg, X4, p11, "l1mix")
    X6, s12 = ffn_fwd(cfg, X5, p12, "l1f1")
    sq, dX = loss_fwd_bwd(cfg, X6, loss_target[0], "loss")
    loss = lax.psum(0.5 * jnp.sum(sq) / D, ("x", "y", "c"))

    gb = {n: jnp.zeros(gathered[n].shape, bf16) for n in BIG}
    gf = dict(wg=gb["ffn_gate"], wu=gb["ffn_up"], wd=gb["ffn_down"])
    dX, g12, gf = ffn_bwd(cfg, dX, s12, p12, gf, "l1f1")
    godd = dict(w_in=gb["od_w_in"], w_out=gb["od_w_out"])
    dX, g11, godd = odd_bwd(cfg, dX, s11, p11, godd, "l1mix")
    dX, g10, gf = ffn_bwd(cfg, dX, s10, p10, gf, "l1f0")
    dX, g02, gf = ffn_bwd(cfg, dX, s02, p02, gf, "l0f1")
    gev = dict(w_in=gb["ev_w_in"], w_out=gb["ev_w_out"])
    dX, g01, gev = even_bwd(cfg, dX, s01, p01, gev, "l0mix")
    dX, g00, gf = ffn_bwd(cfg, dX, s00, p00, gf, "l0f0")
    grad_x = dX[Lc:][None]
    big_grads = dict(ffn_gate=gf["wg"], ffn_up=gf["wu"], ffn_down=gf["wd"], ev_w_in=gev["w_in"], ev_w_out=gev["w_out"],
                     od_w_in=godd["w_in"], od_w_out=godd["w_out"])

    subs = [[g00, g01, g02], [g10, g11, g12]]
    zero_d = jnp.zeros((D,), f32)

    def dmod(group, live):
        rows = []
        for li in range(nL):
            for s in range(3):
                for kind, key in enumerate(("shift", "scale", "gate")):
                    rows.append(subs[li][s][key][group, 0] if live(li, s, kind) else zero_d)
        return jnp.stack(rows).reshape(nL, 9 * D)

    dmod_l = dmod(1, lambda li, s, kind: True)
    dmod_c = dmod(0, lambda li, s, kind: not (li == 1 and (s == 2 or (s == 1 and kind == 2))))

    dm_in, _ = _pack([dmod_l, dmod_c])
    g3 = all_gather_small(dm_in, "gather_dmod")
    dl_all, dc_all = _unpack(g3, [dmod_l.shape, dmod_c.shape], lead=(N_DEV,))
    dm16 = jnp.moveaxis(jnp.concatenate([dl_all, dc_all], axis=0), 0, 1)
    dm16 = lax.dynamic_slice_in_dim(dm16, chip * C4, C4, axis=2)
    s16b = jnp.concatenate([sc_all, jnp.broadcast_to(scc, (N_DEV, D))])
    g_mod_w, dscc_part = mod_bwd(s16b, dm16, mod_w, "mod_bwd")

    norm_pre_g = jnp.stack([jnp.concatenate([subs[li][s]["g_pre"] for s in range(3)]) for li in range(nL)])
    norm_post_g = jnp.stack([jnp.concatenate([subs[li][s]["g_post"] for s in range(3)]) for li in range(nL)])
    small_g = dict(norm_pre=norm_pre_g, norm_post=norm_post_g, lru_conv_w=g01["conv_w"][None], lru_ba=g01["ba"][:, 0][None],
                   lru_bx=g01["bx"][:, 0][None], lru_lambda=g01["lam"][:, 0][None], pool_scale=g11["pool_scale"],
                   mod_b=dmod_l + dmod_c, lru_conv_b=g01["conv_b"], lru_wa=g01["wa"][None], lru_wx=g01["wx"][None],
                   ret_decay_logit=g01["logit"][:, :, 0, 0][None], ret_gn=g01["gn"], pool_w=g11["pool_w"][None],
                   q_norm=g11["qg"], k_norm=g11["kg"])
    names = SMALL_SHARDED + SMALL_REPL
    sg_in, _ = _pack([small_g[n] for n in names] + [dscc_part])
    g4 = all_gather_small(sg_in, "gather_small_grads")
    tot = sum_leading(g4, "sum_small_grads")
    tot_parts = _unpack(tot, [small_g[n].shape for n in names])
    dscc_all = _unpack(g4, [small_g[n].shape for n in names] + [dscc_part.shape], lead=(N_DEV,))[-1]
    dscc = dscc_all[0, 0] + dscc_all[2, 0] + dscc_all[4, 0] + dscc_all[6, 0]
    _, silu_vjp = jax.vjp(jax.nn.silu, c_ctx)
    grads = {"c_ctx": silu_vjp(dscc)[0]}
    for n, g in zip(names, tot_parts):
        if n in SMALL_SHARDED:
            k = wts[n].shape[-1]
            g = lax.dynamic_slice_in_dim(g, chip * k, k, axis=g.ndim - 1)
        grads[n] = g.reshape(wts[n].shape)

    recv = scatter_grads([big_grads[n] for n in BIG], "scatter_grads")
    partial = [sum_leading(r, "sum_" + n) for n, r in zip(BIG, recv)]
    other = swap_with_sibling(partial, "swap_partials")

    delta, new_m, new_v = {}, {}, {}
    for n, pa, pb in zip(BIG, partial, other):
        grads[n], delta[n], new_m[n], new_v[n] = adamw(wts[n], mom_m[n], mom_v[n], [pa, pb], "adamw_" + n)
    grads["mod_w"], delta["mod_w"], new_m["mod_w"], new_v["mod_w"] = adamw(mod_w, m_mod_w, v_mod_w, [g_mod_w], "adamw_mod_w")
    snames = [n for n in WEIGHT_NAMES if n not in BIG and n != "mod_w"]
    pk = lambda d: _pack([d[n] for n in snames])[0]
    sres = adamw(pk(wts), pk(mom_m), pk(mom_v), [pk(grads)], "adamw_small")
    for res, dst in zip(sres[1:], (delta, new_m, new_v)):
        for n, a in zip(snames, _unpack(res, [wts[n].shape for n in snames])):
            dst[n] = a

    return (loss, grad_x, *[grads[n] for n in WEIGHT_NAMES], *[delta[n] for n in WEIGHT_NAMES],
            *[new_m[n] for n in WEIGHT_NAMES], *[new_v[n] for n in WEIGHT_NAMES])
```

```python
import functools
import math

import jax
import jax.numpy as jnp
from jax import lax
from jax.experimental import pallas as pl
from jax.experimental.pallas import tpu as pltpu

f32 = jnp.float32
bf16 = jnp.bfloat16
MESH = pl.DeviceIdType.MESH

EPS = 1e-6
FFN_STEP = 0.5
LRU_C = 8.0
CONV_W = 4
CONV_LEFT = 2
RET_DK = 256
RET_DV = 256
RET_CHUNK = 128
RET_THETA = 10000.0
POOL_WINDOWS = (2, 4, 8, 16)
POOL_GROUP = 128
HEAD_DIM = 128
ROPE_THETA = 10000.0
GRID_W = 64
ADAM_LR = 0.001
ADAM_B1 = 0.9
ADAM_B2 = 0.999
ADAM_EPS = 1e-08
ADAM_WD = 0.01
ADAM_STEP = 10

N_CHIPS = 4
N_DEV = 8
HALO = 8
VMEM_LIMIT = 56 * 1024 * 1024


def _params(sem=None):
    return pltpu.CompilerParams(dimension_semantics=sem, vmem_limit_bytes=VMEM_LIMIT)


def _pick(n, prefs):
    for p in prefs:
        if n % p == 0:
            return p
    return n


def _sds(shape, dtype):
    return jax.ShapeDtypeStruct(tuple(shape), dtype)


_MM_KINDS = {
    "v1": ((1, 0), "out[:, g] = A @ W[g]"),
    "v2": ((1, 0), "out = sum_g A[:, g] @ W[g]"),
    "v3": ((1, 1), "out = sum_g A[:, g] @ W[g]^T"),
    "v4": ((1, 1), "out[:, g] = A @ W[g]^T"),
    "v5": ((0, 0), "out[g] = A^T @ C[:, g]"),
    "v6": ((0, 0), "out[g] = A[:, g]^T @ C"),
}


def matmul(kind, a, b, *, widx=(), out_dtype=f32, init=None, gshape=None, comm=None, name):
    nw = len(widx)
    cdims = _MM_KINDS[kind][0]
    if kind in ("v1", "v2", "v3", "v4"):
        G = b.shape[0]
        d1, d2 = b.shape[-2:]
        M = a.shape[0]
    else:
        G, d1, d2 = gshape
        M = a.shape[0]
    tm_p, tn_p, tk_p = (768, 512, 256, 128), (1408, 1536, 1024, 768, 512, 256, 128), (2048, 1408, 1536, 1024, 768, 512, 256, 128)
    wnone = (None,) * (1 + nw)

    if kind == "v1":
        K, Ns = d1, d2
        tm, tn, tk = _pick(M, tm_p), _pick(Ns, tn_p), _pick(K, tk_p)
        nI, nJ, nR = M // tm, Ns // tn, K // tk
        grid = (G, nI, nJ, nR)
        a_spec = pl.BlockSpec((tm, tk), lambda g, i, j, r: (i, r))
        b_spec = pl.BlockSpec(wnone + (tk, tn), lambda g, i, j, r: (g,) + widx + (r, j))
        o_spec = pl.BlockSpec((tm, tn), lambda g, i, j, r: (i, g * nJ + j))
        out_shape = _sds((M, G * Ns), out_dtype)
        acc_shape = (tm, tn)
    elif kind == "v2":
        Ks, N = d1, d2
        tm, tn, tk = _pick(M, tm_p), _pick(N, tn_p), _pick(Ks, tk_p)
        nI, nJ, nRk = M // tm, N // tn, Ks // tk
        nR = G * nRk
        grid = (1, nI, nJ, nR)
        a_spec = pl.BlockSpec((tm, tk), lambda g, i, j, r: (i, r))
        b_spec = pl.BlockSpec(wnone + (tk, tn), lambda g, i, j, r: (r // nRk,) + widx + (r % nRk, j))
        o_spec = pl.BlockSpec((tm, tn), lambda g, i, j, r: (i, j))
        out_shape = _sds((M, N), out_dtype)
        acc_shape = (tm, tn)
    elif kind == "v3":
        K, Ns = d1, d2
        tm, tn, tk = _pick(M, tm_p), _pick(K, tn_p), _pick(Ns, tk_p)
        nI, nJ, nRk = M // tm, K // tn, Ns // tk
        nR = G * nRk
        grid = (1, nI, nJ, nR)
        a_spec = pl.BlockSpec((tm, tk), lambda g, i, j, r: (i, r))
        b_spec = pl.BlockSpec(wnone + (tn, tk), lambda g, i, j, r: (r // nRk,) + widx + (j, r % nRk))
        o_spec = pl.BlockSpec((tm, tn), lambda g, i, j, r: (i, j))
        out_shape = _sds((M, K), out_dtype)
        acc_shape = (tm, tn)
    elif kind == "v4":
        Ks, N = d1, d2
        tm, tn, tk = _pick(M, tm_p), _pick(Ks, tn_p), _pick(N, tk_p)
        nI, nJ, nR = M // tm, Ks // tn, N // tk
        grid = (G, nI, nJ, nR)
        a_spec = pl.BlockSpec((tm, tk), lambda g, i, j, r: (i, r))
        b_spec = pl.BlockSpec(wnone + (tn, tk), lambda g, i, j, r: (g,) + widx + (j, r))
        o_spec = pl.BlockSpec((tm, tn), lambda g, i, j, r: (i, g * nJ + j))
        out_shape = _sds((M, G * Ks), out_dtype)
        acc_shape = (tm, tn)
    elif kind == "v5":
        K, Ns = d1, d2
        tm, tn, tk = _pick(K, (2048,) + tm_p), _pick(Ns, tn_p), _pick(M, (768, 512, 256, 128))
        nI, nJ, nR = K // tm, Ns // tn, M // tk
        grid = (G, nI, nJ, nR)
        a_spec = pl.BlockSpec((tk, tm), lambda g, i, j, r: (r, i))
        b_spec = pl.BlockSpec((tk, tn), lambda g, i, j, r: (r, g * nJ + j))
        o_spec = pl.BlockSpec((None, tm, tn), lambda g, i, j, r: (g, i, j))
        out_shape = _sds(gshape, out_dtype)
        acc_shape = (tm, tn)
    else:
        Ks, N = d1, d2
        tm, tn, tk = _pick(Ks, (1408,) + tm_p), _pick(N, (2048,) + tn_p), _pick(M, (768, 512, 256, 128))
        nI, nJ, nR = Ks // tm, N // tn, M // tk
        grid = (G, nI, nJ, nR)
        a_spec = pl.BlockSpec((tk, tm), lambda g, i, j, r: (r, g * nI + i))
        b_spec = pl.BlockSpec((tk, tn), lambda g, i, j, r: (r, j))
        o_spec = pl.BlockSpec((None, tm, tn), lambda g, i, j, r: (g, i, j))
        out_shape = _sds(gshape, out_dtype)
        acc_shape = (tm, tn)

    has_init = init is not None
    ncomm = len(comm.srcs) if comm is not None else 0
    last = tuple(n - 1 for n in grid)

    def body(*refs):
        a_ref, b_ref = refs[0], refs[1]
        pos = 2
        init_ref = None
        if has_init:
            init_ref = refs[pos]
            pos += 1
        cin = refs[pos:pos + ncomm]
        pos += ncomm
        o_ref = refs[pos]
        cout = refs[pos + 1:pos + 1 + ncomm]
        acc_ref = refs[pos + 1 + ncomm]
        sems = refs[pos + 2 + ncomm:]
        r = pl.program_id(3)
        ids = [pl.program_id(n) for n in range(4)]

        if ncomm:
            @pl.when(functools.reduce(jnp.logical_and, [i == 0 for i in ids]))
            def _():
                _comm_start(comm, cin, cout, *sems)

        @pl.when(r == 0)
        def _():
            if has_init:
                acc_ref[...] = init_ref[...]
            else:
                acc_ref[...] = jnp.zeros(acc_shape, f32)

        acc_ref[...] += lax.dot_general(a_ref[...], b_ref[...], ((cdims[:1], cdims[1:]), ((), ())),
                                        preferred_element_type=f32)

        @pl.when(r == nR - 1)
        def _():
            o_ref[...] = acc_ref[...].astype(o_ref.dtype)

        if ncomm:
            @pl.when(functools.reduce(jnp.logical_and, [i == n for i, n in zip(ids, last)]))
            def _():
                _comm_wait(comm, cin, cout, *sems)

    in_specs = [a_spec, b_spec]
    args = [a, b]
    if has_init:
        in_specs.append(pl.BlockSpec((tm, tn), lambda g, i, j, r: (i, j)))
        args.append(init)
    out_specs, out_shapes, scratch = [o_spec], [out_shape], [pltpu.VMEM(acc_shape, f32)]
    if ncomm:
        hbm = pl.BlockSpec(memory_space=pl.ANY)
        in_specs += [hbm] * ncomm
        args += [src for src, _ in comm.srcs]
        out_specs += [hbm] * ncomm
        out_shapes += comm.out_shapes()
        scratch += _comm_sems(ncomm)
    res = pl.pallas_call(
        body, name=name, grid=grid, in_specs=in_specs, out_specs=out_specs, out_shape=out_shapes,
        scratch_shapes=scratch, compiler_params=_params(("arbitrary", "arbitrary", "arbitrary", "arbitrary")),
    )(*args)
    return (res[0], list(res[1:])) if ncomm else res[0]


class Comm:
    def __init__(self, mode, srcs):
        self.mode, self.srcs = mode, srcs

    def piece(self, n):
        arr, idx = self.srcs[n]
        shp = arr.shape[len(idx):]
        return shp if self.mode == "gather" else shp[1:]

    def out_shapes(self):
        return [_sds((N_CHIPS,) + tuple(self.piece(n)), self.srcs[n][0].dtype) for n in range(len(self.srcs))]


def _comm_sems(n):
    nsem = n * (N_CHIPS - 1)
    return [pltpu.SemaphoreType.DMA((nsem,)), pltpu.SemaphoreType.DMA((nsem,)), pltpu.SemaphoreType.DMA((n,))]


def _coords():
    return lax.axis_index("x"), lax.axis_index("y"), lax.axis_index("c")


def _flip(v, bit):
    return 1 - v if bit else v


def _chip_peers(x, y, c):
    out = []
    for k in range(1, N_CHIPS):
        kx, ky = (k >> 1) & 1, k & 1
        px, py = _flip(x, kx), _flip(y, ky)
        out.append((k, (px, py, c), 2 * px + py))
    return out


def _comm_copies(comm, in_refs, out_refs, send_sems, recv_sems, local_sems, with_recvs):
    x, y, c = _coords()
    s = 2 * x + y
    local, sends, recvs = [], [], []
    for w, (_, idx) in enumerate(comm.srcs):
        src = in_refs[w].at[idx] if idx else in_refs[w]
        out = out_refs[w]
        if comm.mode == "gather":
            local.append(pltpu.make_async_copy(src, out.at[s], local_sems.at[w]))
        else:
            local.append(pltpu.make_async_copy(src.at[s], out.at[N_CHIPS - 1], local_sems.at[w]))
        for k, peer, pidx in _chip_peers(x, y, c):
            j = w * (N_CHIPS - 1) + k - 1
            if comm.mode == "gather":
                out_src, out_dst, in_dst = src, out.at[s], out.at[pidx]
            else:
                out_src, out_dst, in_dst = src.at[pidx], out.at[k - 1], out.at[k - 1]
            sends.append(pltpu.make_async_remote_copy(src_ref=out_src, dst_ref=out_dst, send_sem=send_sems.at[j],
                                                      recv_sem=recv_sems.at[j], device_id=peer, device_id_type=MESH))
            if with_recvs:
                recvs.append(pltpu.make_async_remote_copy(src_ref=out_src, dst_ref=in_dst, send_sem=send_sems.at[j],
                                                          recv_sem=recv_sems.at[j], device_id=peer, device_id_type=MESH))
    return local, sends, recvs


def _comm_start(comm, in_refs, out_refs, send_sems, recv_sems, local_sems):
    local, sends, _ = _comm_copies(comm, in_refs, out_refs, send_sems, recv_sems, local_sems, False)
    for cp in local + sends:
        cp.start()


def _comm_wait(comm, in_refs, out_refs, send_sems, recv_sems, local_sems):
    local, sends, recvs = _comm_copies(comm, in_refs, out_refs, send_sems, recv_sems, local_sems, True)
    for cp in recvs:
        cp.wait_recv()
    for cp in sends:
        cp.wait_send()
    for cp in local:
        cp.wait()


def exchange(comm, name):
    n = len(comm.srcs)

    def body(*refs):
        in_refs, out_refs, sems = refs[:n], refs[n:2 * n], refs[2 * n:]
        _comm_start(comm, in_refs, out_refs, *sems)
        _comm_wait(comm, in_refs, out_refs, *sems)

    hbm = pl.BlockSpec(memory_space=pl.ANY)
    return pl.pallas_call(body, name=name, out_shape=comm.out_shapes(), in_specs=[hbm] * n, out_specs=[hbm] * n,
                          scratch_shapes=_comm_sems(n))(*[src for src, _ in comm.srcs])


class RowCfg:
    def __init__(self, TR, nT, cT):
        self.TR, self.nT, self.cT = TR, nT, cT


def _row_spec(cfg, spec, off):
    kind = spec[0]
    TR = cfg.TR
    hb = TR // HALO
    nH = cfg.nT * hb
    if kind == "row":
        _, arr, w, cb = spec
        return pl.BlockSpec((TR, w), lambda i: (i + off, cb))
    if kind == "prev":
        _, arr, w, cb = spec
        return pl.BlockSpec((HALO, w), lambda i: (jnp.maximum((i + off) * hb - 1, 0), cb))
    if kind == "next":
        _, arr, w, cb = spec
        return pl.BlockSpec((HALO, w), lambda i: (jnp.minimum((i + off + 1) * hb, nH - 1), cb))
    if kind == "full":
        arr = spec[1]
        nd = arr.ndim
        return pl.BlockSpec(arr.shape, lambda i: (0,) * nd)
    if kind == "grp":
        arr = spec[1]
        cT = cfg.cT
        return pl.BlockSpec((None, 1, arr.shape[-1]), lambda i: (((i + off) >= cT).astype(jnp.int32), 0, 0))
    if kind == "drow":
        _, arr, w, cb = spec
        return pl.BlockSpec((arr.shape[0], TR, w), lambda i: (0, i + off, cb))
    raise ValueError(kind)


def rowcall(cfg, fn, name, ins, outs, *, off=0, n=None, scratch=()):
    n = cfg.nT - off if n is None else n
    in_specs = [_row_spec(cfg, s, off) for s in ins]
    out_specs = [_row_spec(cfg, (s[0], s[1]) + tuple(s[2:]), off) for s in outs]
    out_shape = [s[1] for s in outs]

    def body(*refs):
        fn(pl.program_id(0) + off, *refs)

    res = pl.pallas_call(
        body, name=name, grid=(n,), in_specs=in_specs, out_specs=out_specs, out_shape=out_shape,
        scratch_shapes=list(scratch), compiler_params=_params(("arbitrary",)),
    )(*[s[1] for s in ins])
    return res


def _acc(ref, val, first):
    @pl.when(first)
    def _():
        ref[...] = val

    @pl.when(jnp.logical_not(first))
    def _():
        ref[...] += val


def _rms(x):
    return x * lax.rsqrt(jnp.mean(x * x, axis=-1, keepdims=True) + EPS)


def _pre_fn(x, g, shift, scale):
    return (_rms(x) * g) * (1.0 + scale) + shift


def pre_fwd(cfg, x, g, shift, scale, name):
    D = x.shape[1]

    def fn(i, x_ref, g_ref, sh_ref, sc_ref, h_ref):
        h_ref[...] = _pre_fn(x_ref[...], g_ref[...], sh_ref[...], sc_ref[...]).astype(bf16)

    return rowcall(cfg, fn, name, [("row", x, D, 0), ("full", g), ("grp", shift), ("grp", scale)],
                   [("row", _sds(x.shape, bf16), D, 0)])[0]


def pre_bwd(cfg, x, g, shift, scale, dh, dx_in, name):
    D = x.shape[1]
    cT = cfg.cT

    def fn(i, x_ref, g_ref, sh_ref, sc_ref, dh_ref, dxin_ref, dx_ref, dg_ref, dsh_ref, dsc_ref):
        _, vjp = jax.vjp(_pre_fn, x_ref[...], g_ref[...], sh_ref[...], sc_ref[...])
        dx, dg, dsh, dsc = vjp(dh_ref[...])
        dx_ref[...] = dxin_ref[...] + dx
        _acc(dg_ref, dg, i == 0)
        first = jnp.logical_or(i == 0, i == cT)
        _acc(dsh_ref, dsh, first)
        _acc(dsc_ref, dsc, first)

    return rowcall(cfg, fn, name,
                   [("row", x, D, 0), ("full", g), ("grp", shift), ("grp", scale), ("row", dh, D, 0), ("row", dx_in, D, 0)],
                   [("row", _sds(x.shape, f32), D, 0), ("full", _sds((1, D), f32)),
                    ("grp", _sds((2, 1, D), f32)), ("grp", _sds((2, 1, D), f32))])


def _post_fn(w, y, g, gate):
    return (w * gate) * (_rms(y) * g)


def post_fwd(cfg, x, y, g, gate, w, name):
    D = x.shape[1]

    def fn(i, x_ref, y_ref, g_ref, gt_ref, o_ref):
        o_ref[...] = x_ref[...] + _post_fn(w, y_ref[...], g_ref[...], gt_ref[...])

    return rowcall(cfg, fn, name, [("row", x, D, 0), ("row", y, D, 0), ("full", g), ("grp", gate)],
                   [("row", _sds(x.shape, f32), D, 0)])[0]


def post_bwd(cfg, dx, y, g, gate, w, name):
    D = dx.shape[1]
    cT = cfg.cT

    def fn(i, dx_ref, y_ref, g_ref, gt_ref, dy_ref, dg_ref, dgt_ref):
        _, vjp = jax.vjp(functools.partial(_post_fn, w), y_ref[...], g_ref[...], gt_ref[...])
        dy, dg, dgt = vjp(dx_ref[...])
        dy_ref[...] = dy.astype(bf16)
        _acc(dg_ref, dg, i == 0)
        _acc(dgt_ref, dgt, jnp.logical_or(i == 0, i == cT))

    return rowcall(cfg, fn, name, [("row", dx, D, 0), ("row", y, D, 0), ("full", g), ("grp", gate)],
                   [("row", _sds(dx.shape, bf16), D, 0), ("full", _sds((1, D), f32)), ("grp", _sds((2, 1, D), f32))])


def _swiglu_fn(a, b):
    return jax.nn.silu(a) * b


def swiglu_fwd(cfg, a, b, name):
    F = a.shape[1]
    tf = _pick(F, (1408, 1024, 512, 256, 128))
    TR = cfg.TR

    def body(a_ref, b_ref, u_ref):
        u_ref[...] = _swiglu_fn(a_ref[...], b_ref[...]).astype(bf16)

    spec = pl.BlockSpec((TR, tf), lambda i, j: (i, j))
    return pl.pallas_call(body, name=name, grid=(cfg.nT, F // tf), in_specs=[spec, spec], out_specs=spec,
                          out_shape=_sds(a.shape, bf16), compiler_params=_params(("arbitrary", "arbitrary")))(a, b)


def swiglu_bwd(cfg, a, b, du, name):
    F = a.shape[1]
    tf = _pick(F, (1408, 1024, 512, 256, 128))
    TR = cfg.TR

    def body(a_ref, b_ref, du_ref, da_ref, db_ref):
        _, vjp = jax.vjp(_swiglu_fn, a_ref[...], b_ref[...])
        da, db = vjp(du_ref[...])
        da_ref[...] = da.astype(bf16)
        db_ref[...] = db.astype(bf16)

    spec = pl.BlockSpec((TR, tf), lambda i, j: (i, j))
    return pl.pallas_call(body, name=name, grid=(cfg.nT, F // tf), in_specs=[spec, spec, spec], out_specs=[spec, spec],
                          out_shape=[_sds(a.shape, bf16), _sds(a.shape, bf16)],
                          compiler_params=_params(("arbitrary", "arbitrary")))(a, b, du)


def _hosted(host, role, got, fn):
    comm = host.get(role) if host else None
    if comm is None:
        return fn(None)
    out, res = fn(comm)
    got[role] = res
    return out


def ffn_fwd(cfg, x, p, tag, host=None):
    got = {}
    h = pre_fwd(cfg, x, p["g_pre"], p["shift"], p["scale"], tag + "_pre")
    a = _hosted(host, "gate", got, lambda cm: matmul("v1", h, p["wg"], comm=cm, name=tag + "_gate"))
    b = _hosted(host, "up", got, lambda cm: matmul("v1", h, p["wu"], comm=cm, name=tag + "_up"))
    u = swiglu_fwd(cfg, a, b, tag + "_act")
    y = _hosted(host, "down", got, lambda cm: matmul("v2", u, p["wd"], comm=cm, name=tag + "_down"))
    xo = post_fwd(cfg, x, y, p["g_post"], p["gate"], FFN_STEP, tag + "_post")
    return xo, (x, h, a, b, u, y), got


def ffn_bwd(cfg, dX, saved, p, tag, host=None):
    x, h, a, b, u, y = saved
    got = {}
    dy, dg_post, dgate = post_bwd(cfg, dX, y, p["g_post"], p["gate"], FFN_STEP, tag + "_postb")
    du = _hosted(host, "du", got, lambda cm: matmul("v4", dy, p["wd"], comm=cm, name=tag + "_du"))
    gwd = _hosted(host, "gwd", got, lambda cm: matmul("v6", u, dy, gshape=p["wd"].shape, out_dtype=bf16, comm=cm,
                                                      name=tag + "_gwd"))
    da, db = swiglu_bwd(cfg, a, b, du, tag + "_actb")
    gwg = _hosted(host, "gwg", got, lambda cm: matmul("v5", h, da, gshape=p["wg"].shape, out_dtype=bf16, comm=cm,
                                                      name=tag + "_gwg"))
    gwu = matmul("v5", h, db, gshape=p["wu"].shape, out_dtype=bf16, name=tag + "_gwu")
    dh = _hosted(host, "dh1", got, lambda cm: matmul("v3", da, p["wg"], comm=cm, name=tag + "_dh1"))
    dh = _hosted(host, "dh2", got, lambda cm: matmul("v3", db, p["wu"], init=dh, comm=cm, name=tag + "_dh2"))
    dX, dg_pre, dshift, dscale = pre_bwd(cfg, x, p["g_pre"], p["shift"], p["scale"], dh, dX, tag + "_preb")
    small = dict(g_pre=dg_pre, g_post=dg_post, shift=dshift, scale=dscale, gate=dgate)
    return dX, small, dict(wg=gwg, wu=gwu, wd=gwd), got


def _seg_flags(cfg, i):
    start = jnp.logical_or(i == 0, i == cfg.cT)
    end = jnp.logical_or(i == cfg.cT - 1, i == cfg.nT - 1)
    return start, end


def _fill_halo(buf, cur, prev, nxt, start, end, TR):
    buf[pl.ds(0, HALO), :] = jnp.where(start, 0.0, prev)
    buf[pl.ds(HALO, TR), :] = cur
    buf[pl.ds(HALO + TR, HALO), :] = jnp.where(end, 0.0, nxt)


def conv_fwd(cfg, z, cw, cb, W, name):
    TR = cfg.TR

    def fn(i, r_ref, rp_ref, rn_ref, cw_ref, cb_ref, u_ref, buf):
        start, end = _seg_flags(cfg, i)
        _fill_halo(buf, r_ref[...], rp_ref[...], rn_ref[...], start, end, TR)
        u = jnp.broadcast_to(cb_ref[...], (TR, W))
        for k in range(CONV_W):
            u = u + buf[pl.ds(HALO + k - CONV_LEFT, TR), :] * cw_ref[pl.ds(k, 1), :]
        u_ref[...] = u

    return rowcall(cfg, fn, name, [("row", z, W, 1), ("prev", z, W, 1), ("next", z, W, 1), ("full", cw), ("full", cb)],
                   [("row", _sds((z.shape[0], W), f32), W, 0)], scratch=[pltpu.VMEM((TR + 2 * HALO, W), f32)])[0]


def conv_bwd(cfg, z, du, cw, W, name):
    TR = cfg.TR

    def fn(i, r_ref, rp_ref, rn_ref, du_ref, dup_ref, dun_ref, cw_ref, dr_ref, dcw_ref, dcb_ref, rbuf, dbuf):
        start, end = _seg_flags(cfg, i)
        _fill_halo(rbuf, r_ref[...], rp_ref[...], rn_ref[...], start, end, TR)
        _fill_halo(dbuf, du_ref[...], dup_ref[...], dun_ref[...], start, end, TR)
        du = du_ref[...]

        @pl.when(i == 0)
        def _():
            dcw_ref[...] = jnp.zeros(dcw_ref.shape, f32)
            dcb_ref[...] = jnp.zeros(dcb_ref.shape, f32)

        dr = jnp.zeros((TR, W), f32)
        for k in range(CONV_W):
            dr = dr + dbuf[pl.ds(HALO - (k - CONV_LEFT), TR), :] * cw_ref[pl.ds(k, 1), :]
            dcw_ref[pl.ds(k, 1), :] += jnp.sum(du * rbuf[pl.ds(HALO + k - CONV_LEFT, TR), :], axis=0, keepdims=True)
        dcb_ref[...] += jnp.sum(du, axis=0, keepdims=True)
        dr_ref[...] = dr

    T = z.shape[0]
    return rowcall(cfg, fn, name,
                   [("row", z, W, 1), ("prev", z, W, 1), ("next", z, W, 1), ("row", du, W, 0), ("prev", du, W, 0),
                    ("next", du, W, 0), ("full", cw)],
                   [("row", _sds((T, W), f32), W, 0), ("full", _sds((CONV_W, W), f32)), ("full", _sds((1, W), f32))],
                   scratch=[pltpu.VMEM((TR + 2 * HALO, W), f32), pltpu.VMEM((TR + 2 * HALO, W), f32)])


def _softplus(x):
    return jnp.maximum(x, 0.0) + jnp.log1p(jnp.exp(-jnp.abs(x)))


def _neg_expm1(x):
    series = -x * (1.0 + x * (0.5 + x * (1.0 / 6.0 + x * (1.0 / 24.0 + x * (1.0 / 120.0 + x * (1.0 / 720.0))))))
    return jnp.where(x > -0.1, series, 1.0 - jnp.exp(x))


def _lru_coef(u, pa, px, lam):
    r = jax.nn.sigmoid(pa)
    i = jax.nn.sigmoid(px)
    log_a = -LRU_C * r * _softplus(-lam)
    a = jnp.exp(log_a)
    b = jnp.sqrt(_neg_expm1(2.0 * log_a)) * (i * u)
    return a, b


def _blockdiag(u_bf, w_ref, d, nblk, blk):
    return jnp.concatenate(
        [jnp.dot(u_bf[:, n * blk:(n + 1) * blk], w_ref[d, n].astype(bf16), preferred_element_type=f32)
         for n in range(nblk)], axis=1)


def lru_coef_fwd(cfg, u, wa, ba, wx, bx, lam, name):
    T, W = u.shape
    nblk, blk = wa.shape[1], wa.shape[2]

    def fn(i, u_ref, wa_ref, ba_ref, wx_ref, bx_ref, lam_ref, a_ref, b_ref):
        uv = u_ref[...]
        u_bf = uv.astype(bf16)
        for d in range(2):
            pa = _blockdiag(u_bf, wa_ref, d, nblk, blk) + ba_ref[d]
            px = _blockdiag(u_bf, wx_ref, d, nblk, blk) + bx_ref[d]
            a, b = _lru_coef(uv, pa, px, lam_ref[d])
            a_ref[d] = a
            b_ref[d] = b

    return rowcall(cfg, fn, name, [("row", u, W, 0), ("full", wa), ("full", ba), ("full", wx), ("full", bx), ("full", lam)],
                   [("drow", _sds((2, T, W), f32), W, 0), ("drow", _sds((2, T, W), f32), W, 0)])


def lru_coef_bwd(cfg, u, da, db, wa, ba, wx, bx, lam, name):
    T, W = u.shape
    nblk, blk = wa.shape[1], wa.shape[2]

    def fn(i, u_ref, da_ref, db_ref, wa_ref, ba_ref, wx_ref, bx_ref, lam_ref,
           du_ref, dwa_ref, dba_ref, dwx_ref, dbx_ref, dlam_ref):
        @pl.when(i == 0)
        def _():
            for r in (dwa_ref, dba_ref, dwx_ref, dbx_ref, dlam_ref):
                r[...] = jnp.zeros(r.shape, f32)

        uv = u_ref[...]
        u_bf = uv.astype(bf16)
        du = jnp.zeros(uv.shape, f32)
        for d in range(2):
            pa = _blockdiag(u_bf, wa_ref, d, nblk, blk) + ba_ref[d]
            px = _blockdiag(u_bf, wx_ref, d, nblk, blk) + bx_ref[d]
            _, vjp = jax.vjp(_lru_coef, uv, pa, px, lam_ref[d])
            du_e, dpa, dpx, dlam = vjp((da_ref[d], db_ref[d]))
            du = du + du_e
            dba_ref[d] += jnp.sum(dpa, axis=0, keepdims=True)
            dbx_ref[d] += jnp.sum(dpx, axis=0, keepdims=True)
            dlam_ref[d] += dlam
            parts = []
            for n in range(nblk):
                sl = slice(n * blk, (n + 1) * blk)
                ga, gx = dpa[:, sl].astype(bf16), dpx[:, sl].astype(bf16)
                ub = u_bf[:, sl]
                dwa_ref[d, n] += lax.dot_general(ub, ga, (((0,), (0,)), ((), ())), preferred_element_type=f32)
                dwx_ref[d, n] += lax.dot_general(ub, gx, (((0,), (0,)), ((), ())), preferred_element_type=f32)
                parts.append(
                    lax.dot_general(ga, wa_ref[d, n].astype(bf16), (((1,), (1,)), ((), ())), preferred_element_type=f32)
                    + lax.dot_general(gx, wx_ref[d, n].astype(bf16), (((1,), (1,)), ((), ())), preferred_element_type=f32))
            du = du + jnp.concatenate(parts, axis=1)
        du_ref[...] = du

    return rowcall(cfg, fn, name,
                   [("row", u, W, 0), ("drow", da, W, 0), ("drow", db, W, 0), ("full", wa), ("full", ba), ("full", wx),
                    ("full", bx), ("full", lam)],
                   [("row", _sds((T, W), f32), W, 0), ("full", _sds(wa.shape, f32)), ("full", _sds(ba.shape, f32)),
                    ("full", _sds(wx.shape, f32)), ("full", _sds(bx.shape, f32)), ("full", _sds(lam.shape, f32))])


def _dir_tile(cfg, d, j):
    rev = jnp.where(j < cfg.cT, cfg.cT - 1 - j, cfg.nT - 1 - (j - cfg.cT))
    return jnp.where(d == 0, j, rev)


def lru_scan(cfg, a, b, name):
    _, T, W = a.shape
    TR, nT = cfg.TR, cfg.nT

    def body(a_ref, b_ref, h_ref, hp_ref, st):
        d, j = pl.program_id(0), pl.program_id(1)

        @pl.when(j == 0)
        def _():
            st[...] = jnp.zeros(st.shape, f32)

        def step(t, h):
            idx = t + d * (TR - 1 - 2 * t)
            hn = a_ref[pl.ds(idx, 1), :] * h + b_ref[pl.ds(idx, 1), :]
            hp_ref[pl.ds(idx, 1), :] = h
            h_ref[pl.ds(idx, 1), :] = hn
            return hn

        st[...] = lax.fori_loop(0, TR, step, st[...])

    spec = pl.BlockSpec((None, TR, W), lambda d, j: (d, _dir_tile(cfg, d, j), 0))
    return pl.pallas_call(body, name=name, grid=(2, nT), in_specs=[spec, spec], out_specs=[spec, spec],
                          out_shape=[_sds(a.shape, f32), _sds(a.shape, f32)], scratch_shapes=[pltpu.VMEM((1, W), f32)],
                          compiler_params=_params(("arbitrary", "arbitrary")))(a, b)


def lru_scan_bwd(cfg, a, hp, dh, name):
    _, T, W = a.shape
    TR, nT = cfg.TR, cfg.nT

    def body(a_ref, hp_ref, dh_ref, da_ref, db_ref, st):
        d, j = pl.program_id(0), pl.program_id(1)

        @pl.when(j == 0)
        def _():
            st[...] = jnp.zeros(st.shape, f32)

        def step(t, c):
            p = TR - 1 - t
            idx = p + d * (TR - 1 - 2 * p)
            g = dh_ref[pl.ds(idx, 1), :] + c
            db_ref[pl.ds(idx, 1), :] = g
            da_ref[pl.ds(idx, 1), :] = g * hp_ref[pl.ds(idx, 1), :]
            return a_ref[pl.ds(idx, 1), :] * g

        st[...] = lax.fori_loop(0, TR, step, st[...])

    spec = pl.BlockSpec((None, TR, W), lambda d, j: (d, _dir_tile(cfg, d, nT - 1 - j), 0))
    dspec = pl.BlockSpec((TR, W), lambda d, j: (_dir_tile(cfg, d, nT - 1 - j), 0))
    return pl.pallas_call(body, name=name, grid=(2, nT), in_specs=[spec, spec, dspec], out_specs=[spec, spec],
                          out_shape=[_sds(a.shape, f32), _sds(a.shape, f32)], scratch_shapes=[pltpu.VMEM((1, W), f32)],
                          compiler_params=_params(("arbitrary", "arbitrary")))(a, hp, dh)


def _lru_out_fn(gl, h0, h1):
    return jax.nn.gelu(gl) * (h0 + h1)


def lru_out_fwd(cfg, z, h, W, name):
    def fn(i, g_ref, h_ref, o_ref):
        o_ref[...] = _lru_out_fn(g_ref[...], h_ref[0], h_ref[1]).astype(bf16)

    return rowcall(cfg, fn, name, [("row", z, W, 0), ("drow", h, W, 0)], [("row", _sds((z.shape[0], W), bf16), W, 0)])[0]


def lru_out_bwd(cfg, z, h, dmix, W, name):
    def fn(i, g_ref, h_ref, d_ref, dg_ref, dh_ref):
        _, vjp = jax.vjp(_lru_out_fn, g_ref[...], h_ref[0], h_ref[1])
        dg, dh0, _ = vjp(d_ref[...])
        dg_ref[...] = dg
        dh_ref[...] = dh0

    T = z.shape[0]
    return rowcall(cfg, fn, name, [("row", z, W, 0), ("drow", h, W, 0), ("row", dmix, W, 0)],
                   [("row", _sds((T, W), f32), W, 0), ("row", _sds((T, W), f32), W, 0)])


def _rot_half(x, cos, sin):
    n = x.shape[1] // 2
    x1, x2 = x[:, :n], x[:, n:]
    return jnp.concatenate([x1 * cos - x2 * sin, x1 * sin + x2 * cos], axis=1)


def _dotf(a, b, ca, cb):
    return lax.dot_general(a, b, (((ca,), (cb,)), ((), ())), preferred_element_type=f32)


def _ret_chunk(d, q, k, v, s, logit, cos, sin):
    C = q.shape[0]
    lg = -_softplus(-logit)
    qr = _rot_half(q, cos, sin)
    kr = _rot_half(k, cos, sin) * (RET_DK ** -0.5)
    ii = lax.broadcasted_iota(jnp.int32, (C, C), 0)
    jj = lax.broadcasted_iota(jnp.int32, (C, C), 1)
    diff = ((ii - jj) if d == 0 else (jj - ii)).astype(f32)
    intra = jnp.where(diff >= 0, jnp.exp(lg * jnp.maximum(diff, 0.0)), 0.0)
    pos = lax.broadcasted_iota(jnp.int32, (C, 1), 0).astype(f32)
    if d == 0:
        q_dec, k_dec = jnp.exp(lg * (pos + 1.0)), jnp.exp(lg * (C - 1.0 - pos))
    else:
        q_dec, k_dec = jnp.exp(lg * (C - pos)), jnp.exp(lg * pos)
    s_dec = jnp.exp(lg * C)
    scores = _dotf(qr, kr, 1, 1) * intra
    o = _dotf(scores, v, 1, 0) + _dotf(qr * q_dec, s, 1, 0)
    s_new = s * s_dec + _dotf(kr * k_dec, v, 0, 0)
    return o, s_new


def _chunk_cfg(cfg):
    f = cfg.TR // RET_CHUNK
    return RowCfg(RET_CHUNK, cfg.nT * f, cfg.cT * f)


def ret_fwd(cfg, z, logit, cos, sin, H, qcol, name):
    T = z.shape[0]
    cc = _chunk_cfg(cfg)
    C, nC = RET_CHUNK, cc.nT

    def body(q_ref, k_ref, v_ref, lg_ref, cos_ref, sin_ref, o_ref, s_ref, st):
        d, j = pl.program_id(0), pl.program_id(2)

        @pl.when(j == 0)
        def _():
            st[...] = jnp.zeros(st.shape, f32)

        s_ref[...] = st[...]
        for dd in range(2):
            @pl.when(d == dd)
            def _():
                o, s_new = _ret_chunk(dd, q_ref[...], k_ref[...], v_ref[...], st[...], lg_ref[...], cos_ref[...], sin_ref[...])
                o_ref[...] = o
                st[...] = s_new

    tile = lambda d, j: _dir_tile(cc, d, j)
    zq = pl.BlockSpec((C, RET_DK), lambda d, h, j: (tile(d, j), qcol + h))
    zk = pl.BlockSpec((C, RET_DK), lambda d, h, j: (tile(d, j), qcol + H + h))
    zv = pl.BlockSpec((C, RET_DV), lambda d, h, j: (tile(d, j), qcol + 2 * H + h))
    lgs = pl.BlockSpec((None, None, 1, 1), lambda d, h, j: (d, h, 0, 0))
    cs = pl.BlockSpec((C, RET_DK // 2), lambda d, h, j: (tile(d, j), 0))
    o_spec = pl.BlockSpec((None, C, RET_DV), lambda d, h, j: (d, tile(d, j), h))
    s_spec = pl.BlockSpec((None, None, None, RET_DK, RET_DV), lambda d, h, j: (d, h, tile(d, j), 0, 0))
    return pl.pallas_call(
        body, name=name, grid=(2, H, nC), in_specs=[zq, zk, zv, lgs, cs, cs], out_specs=[o_spec, s_spec],
        out_shape=[_sds((2, T, H * RET_DV), f32), _sds((2, H, nC, RET_DK, RET_DV), f32)],
        scratch_shapes=[pltpu.VMEM((RET_DK, RET_DV), f32)],
        compiler_params=_params(("arbitrary", "arbitrary", "arbitrary")))(z, z, z, logit, cos, sin)


def ret_bwd(cfg, z, states, do, logit, cos, sin, H, qcol, name):
    T = z.shape[0]
    cc = _chunk_cfg(cfg)
    C, nC = RET_CHUNK, cc.nT

    def body(q_ref, k_ref, v_ref, s_ref, do_ref, lg_ref, cos_ref, sin_ref, dq_ref, dk_ref, dv_ref, dlg_ref, st):
        d, j = pl.program_id(0), pl.program_id(2)

        @pl.when(j == 0)
        def _():
            st[...] = jnp.zeros(st.shape, f32)
            dlg_ref[...] = jnp.zeros(dlg_ref.shape, f32)

        for dd in range(2):
            @pl.when(d == dd)
            def _():
                fn = lambda q, k, v, s, lg: _ret_chunk(dd, q, k, v, s, lg, cos_ref[...], sin_ref[...])
                _, vjp = jax.vjp(fn, q_ref[...], k_ref[...], v_ref[...], s_ref[...], lg_ref[...])
                dq, dk, dv, ds, dlg = vjp((do_ref[...], st[...]))
                dq_ref[...] = dq
                dk_ref[...] = dk
                dv_ref[...] = dv
                st[...] = ds
                dlg_ref[...] += dlg

    tile = lambda d, j: _dir_tile(cc, d, nC - 1 - j)
    zq = pl.BlockSpec((C, RET_DK), lambda d, h, j: (tile(d, j), qcol + h))
    zk = pl.BlockSpec((C, RET_DK), lambda d, h, j: (tile(d, j), qcol + H + h))
    zv = pl.BlockSpec((C, RET_DV), lambda d, h, j: (tile(d, j), qcol + 2 * H + h))
    s_spec = pl.BlockSpec((None, None, None, RET_DK, RET_DV), lambda d, h, j: (d, h, tile(d, j), 0, 0))
    do_spec = pl.BlockSpec((C, RET_DV), lambda d, h, j: (tile(d, j), h))
    lgs = pl.BlockSpec((None, None, 1, 1), lambda d, h, j: (d, h, 0, 0))
    cs = pl.BlockSpec((C, RET_DK // 2), lambda d, h, j: (tile(d, j), 0))
    g_spec = pl.BlockSpec((None, C, RET_DK), lambda d, h, j: (d, tile(d, j), h))
    gshape = _sds((2, T, H * RET_DK), f32)
    return pl.pallas_call(
        body, name=name, grid=(2, H, nC), in_specs=[zq, zk, zv, s_spec, do_spec, lgs, cs, cs],
        out_specs=[g_spec, g_spec, g_spec, lgs], out_shape=[gshape, gshape, gshape, _sds((2, H, 1, 1), f32)],
        scratch_shapes=[pltpu.VMEM((RET_DK, RET_DV), f32)],
        compiler_params=_params(("arbitrary", "arbitrary", "arbitrary")))(z, z, z, states, do, logit, cos, sin)


def _ret_norm_fn(H, o0, o1, ol, gn):
    o = o0 + o1
    parts = []
    for h in range(H):
        x = o[:, h * RET_DV:(h + 1) * RET_DV]
        mu = jnp.mean(x, axis=-1, keepdims=True)
        var = jnp.mean(jnp.square(x - mu), axis=-1, keepdims=True)
        parts.append((x - mu) * lax.rsqrt(var + EPS))
    return (jnp.concatenate(parts, axis=1) * gn) * jax.nn.silu(ol)


def ret_norm_fwd(cfg, o, z, gn, H, olcol, name):
    RV = H * RET_DV

    def fn(i, o_ref, ol_ref, gn_ref, r_ref):
        r_ref[...] = _ret_norm_fn(H, o_ref[0], o_ref[1], ol_ref[...], gn_ref[...]).astype(bf16)

    return rowcall(cfg, fn, name, [("drow", o, RV, 0), ("row", z, RV, olcol), ("full", gn)],
                   [("row", _sds((z.shape[0], RV), bf16), RV, 0)])[0]


def ret_norm_bwd(cfg, o, z, gn, dmix, H, olcol, dcol, name):
    RV = H * RET_DV
    T = z.shape[0]

    def fn(i, o_ref, ol_ref, gn_ref, d_ref, do_ref, dol_ref, dgn_ref):
        _, vjp = jax.vjp(functools.partial(_ret_norm_fn, H), o_ref[0], o_ref[1], ol_ref[...], gn_ref[...])
        do, _, dol, dgn = vjp(d_ref[...])
        do_ref[...] = do
        dol_ref[...] = dol
        _acc(dgn_ref, dgn, i == 0)

    return rowcall(cfg, fn, name, [("drow", o, RV, 0), ("row", z, RV, olcol), ("full", gn), ("row", dmix, RV, dcol)],
                   [("row", _sds((T, RV), f32), RV, 0), ("row", _sds((T, RV), f32), RV, 0), ("full", _sds((1, RV), f32))])


def _pool_geom(cfg, i, w, L):
    t = (i - cfg.cT) * cfg.TR + lax.broadcasted_iota(jnp.int32, (cfg.TR, 1), 0)
    lo = jnp.clip(t - w // 2, 0, L)
    hi = jnp.clip(t + w // 2, 0, L)
    return (hi - lo).astype(f32)


def _pool_centred(cfg, i, buf, gi, w, L):
    TR, G = cfg.TR, POOL_GROUP
    cols = pl.ds(gi * G, G)
    tot = buf[pl.ds(HALO - w // 2, TR), cols]
    for s in range(-w // 2 + 1, w // 2):
        tot = tot + buf[pl.ds(HALO + s, TR), cols]
    cnt = _pool_geom(cfg, i, w, L)
    return tot / cnt - buf[pl.ds(HALO, TR), cols], cnt


def pool_fwd(cfg, z, pw, ps, name):
    T = z.shape[0]
    TR, cT = cfg.TR, cfg.cT
    P = POOL_GROUP * len(POOL_WINDOWS)
    L = T - cT * TR

    def fn(i, x_ref, xp_ref, xn_ref, pw_ref, ps_ref, o_ref, buf):
        @pl.when(i < cT)
        def _():
            o_ref[...] = jnp.zeros(o_ref.shape, bf16)

        @pl.when(i >= cT)
        def _():
            start, end = _seg_flags(cfg, i)
            _fill_halo(buf, x_ref[...], xp_ref[...], xn_ref[...], start, end, TR)
            outs = []
            for gi, w in enumerate(POOL_WINDOWS):
                m, _ = _pool_centred(cfg, i, buf, gi, w, L)
                outs.append(jnp.dot(m.astype(bf16), pw_ref[gi].astype(bf16), preferred_element_type=f32))
            o_ref[...] = (jnp.concatenate(outs, axis=1) * ps_ref[...]).astype(bf16)

    return rowcall(cfg, fn, name, [("row", z, P, 0), ("prev", z, P, 0), ("next", z, P, 0), ("full", pw), ("full", ps)],
                   [("row", _sds((T, P), bf16), P, 0)], scratch=[pltpu.VMEM((TR + 2 * HALO, P), f32)])[0]


def pool_bwd_a(cfg, z, dmix, pw, ps, name):
    T = z.shape[0]
    TR, cT = cfg.TR, cfg.cT
    G = POOL_GROUP
    P = G * len(POOL_WINDOWS)
    L = T - cT * TR

    def fn(i, x_ref, xp_ref, xn_ref, d_ref, pw_ref, ps_ref, dm_ref, dmn_ref, dpw_ref, dps_ref, buf):
        @pl.when(i == 0)
        def _():
            dpw_ref[...] = jnp.zeros(dpw_ref.shape, f32)
            dps_ref[...] = jnp.zeros(dps_ref.shape, f32)

        @pl.when(i < cT)
        def _():
            dm_ref[...] = jnp.zeros(dm_ref.shape, f32)
            dmn_ref[...] = jnp.zeros(dmn_ref.shape, f32)

        @pl.when(i >= cT)
        def _():
            start, end = _seg_flags(cfg, i)
            _fill_halo(buf, x_ref[...], xp_ref[...], xn_ref[...], start, end, TR)
            dout = d_ref[...]
            dpre = dout * ps_ref[...]
            pres, dms, dmns = [], [], []
            for gi, w in enumerate(POOL_WINDOWS):
                m, cnt = _pool_centred(cfg, i, buf, gi, w, L)
                m_bf = m.astype(bf16)
                w_bf = pw_ref[gi].astype(bf16)
                pres.append(jnp.dot(m_bf, w_bf, preferred_element_type=f32))
                g_bf = dpre[:, gi * G:(gi + 1) * G].astype(bf16)
                dpw_ref[gi] += _dotf(m_bf, g_bf, 0, 0)
                dm = _dotf(g_bf, w_bf, 1, 1)
                dms.append(dm)
                dmns.append(dm / cnt)
            dps_ref[...] += jnp.sum(dout * jnp.concatenate(pres, axis=1), axis=0, keepdims=True)
            dm_ref[...] = jnp.concatenate(dms, axis=1)
            dmn_ref[...] = jnp.concatenate(dmns, axis=1)

    return rowcall(cfg, fn, name,
                   [("row", z, P, 0), ("prev", z, P, 0), ("next", z, P, 0), ("row", dmix, P, 0), ("full", pw), ("full", ps)],
                   [("row", _sds((T, P), f32), P, 0), ("row", _sds((T, P), f32), P, 0), ("full", _sds(pw.shape, f32)),
                    ("full", _sds((1, P), f32))], scratch=[pltpu.VMEM((TR + 2 * HALO, P), f32)])


def pool_bwd_b(cfg, dm, dmn, name):
    T, P = dm.shape
    TR, cT = cfg.TR, cfg.cT
    G = POOL_GROUP

    def fn(i, dm_ref, c_ref, p_ref, n_ref, dx_ref, buf):
        start, end = _seg_flags(cfg, i)
        _fill_halo(buf, c_ref[...], p_ref[...], n_ref[...], start, end, TR)
        outs = []
        for gi, w in enumerate(POOL_WINDOWS):
            cols = pl.ds(gi * G, G)
            tot = buf[pl.ds(HALO + w // 2, TR), cols]
            for s in range(-w // 2 + 1, w // 2):
                tot = tot + buf[pl.ds(HALO + s, TR), cols]
            outs.append(tot)
        dx_ref[...] = jnp.concatenate(outs, axis=1) - dm_ref[...]

    return rowcall(cfg, fn, name, [("row", dm, P, 0), ("row", dmn, P, 0), ("prev", dmn, P, 0), ("next", dmn, P, 0)],
                   [("row", _sds((T, P), f32), P, 0)], scratch=[pltpu.VMEM((TR + 2 * HALO, P), f32)])[0]


def _swap_halves(x):
    return pltpu.roll(x, HEAD_DIM // 2, 1)


def _headnorm(x, g):
    return _rms(x) * g


def att_prep(cfg, z, qg, kg, cosf, sinf, nq, name):
    T = z.shape[0]
    U = z.shape[1] // (nq + 3)
    nh = U // HEAD_DIM

    def fn(i, *refs):
        q_refs = refs[:nq]
        k_ref, v_ref, qg_ref, kg_ref, cos_ref, sin_ref, qn_ref, kn_ref, vb_ref = refs[nq:]
        cosv, sinv = cos_ref[...], sin_ref[...]

        def heads(x, g):
            outs = []
            for h in range(nh):
                y = _headnorm(x[:, h * HEAD_DIM:(h + 1) * HEAD_DIM], g)
                outs.append(y * cosv + _swap_halves(y) * sinv)
            return jnp.concatenate(outs, axis=1)

        qn_ref[...] = jnp.concatenate([heads(r[...], qg_ref[...]) for r in q_refs], axis=1).astype(bf16)
        kn_ref[...] = heads(k_ref[...], kg_ref[...]).astype(bf16)
        vb_ref[...] = v_ref[...].astype(bf16)

    ins = [("row", z, U, 1 + n) for n in range(nq)] + [("row", z, U, nq + 1), ("row", z, U, nq + 2), ("full", qg),
                                                       ("full", kg), ("row", cosf, HEAD_DIM, 0), ("row", sinf, HEAD_DIM, 0)]
    return rowcall(cfg, fn, name, ins, [("row", _sds((T, nq * U), bf16), nq * U, 0), ("row", _sds((T, U), bf16), U, 0),
                                        ("row", _sds((T, U), bf16), U, 0)])


def att_prep_bwd(cfg, z, qg, kg, cosf, sinf, dqn, dkn, dvb, dxpool, nq, name):
    T = z.shape[0]
    U = z.shape[1] // (nq + 3)
    nh = U // HEAD_DIM
    cT = cfg.cT

    def fn(i, *refs):
        q_refs = refs[:nq]
        (k_ref, qg_ref, kg_ref, cos_ref, sin_ref, dqn_ref, dkn_ref, dvb_ref, dxp_ref, dz_ref, dqg_ref, dkg_ref) = refs[nq:]
        cosv, sinv = cos_ref[...], sin_ref[...]

        @pl.when(i == 0)
        def _():
            dqg_ref[...] = jnp.zeros(dqg_ref.shape, f32)
            dkg_ref[...] = jnp.zeros(dkg_ref.shape, f32)

        def heads_bwd(x, g, dy, dg_ref):
            outs = []
            for h in range(nh):
                sl = slice(h * HEAD_DIM, (h + 1) * HEAD_DIM)
                d = dy[:, sl]
                dn = d * cosv + _swap_halves(d * sinv)
                _, vjp = jax.vjp(_headnorm, x[:, sl], g)
                dx, dg = vjp(dn)
                dg_ref[...] += dg
                outs.append(dx)
            return jnp.concatenate(outs, axis=1)

        dk = heads_bwd(k_ref[...], kg_ref[...], dkn_ref[...], dkg_ref)
        tail = [dk.astype(bf16), dvb_ref[...].astype(bf16)]

        @pl.when(i < cT)
        def _():
            zeros = jnp.zeros((cfg.TR, (nq + 1) * U), bf16)
            dz_ref[...] = jnp.concatenate([zeros] + tail, axis=1)

        @pl.when(i >= cT)
        def _():
            dq = [heads_bwd(r[...], qg_ref[...], dqn_ref[:, n * U:(n + 1) * U], dqg_ref) for n, r in enumerate(q_refs)]
            dz_ref[...] = jnp.concatenate([dxp_ref[...].astype(bf16)] + [t.astype(bf16) for t in dq] + tail, axis=1)

    ins = ([("row", z, U, 1 + n) for n in range(nq)] +
           [("row", z, U, nq + 1), ("full", qg), ("full", kg), ("row", cosf, HEAD_DIM, 0), ("row", sinf, HEAD_DIM, 0),
            ("row", dqn, nq * U, 0), ("row", dkn, U, 0), ("row", dvb, U, 0), ("row", dxpool, U, 0)])
    W = (nq + 3) * U
    return rowcall(cfg, fn, name, ins, [("row", _sds((T, W), bf16), W, 0), ("full", _sds((1, HEAD_DIM), f32)),
                                        ("full", _sds((1, HEAD_DIM), f32))])


def _stack_heads(x, n):
    return jnp.concatenate([x[:, h * HEAD_DIM:(h + 1) * HEAD_DIM] for h in range(n)], axis=0)


def _unstack_heads(x, n):
    rows = x.shape[0] // n
    return jnp.concatenate([x[h * rows:(h + 1) * rows] for h in range(n)], axis=1)


def _att_tiles(cfg, T):
    tq = cfg.TR
    tk = _pick(T, (768, 512, 256, 128))
    return tq, tk, (T - cfg.cT * cfg.TR) // tq, T // tk


LOG2E = 1.4426950408889634


def att_fwd(cfg, qn, kn, vb, nq, name):
    T, U = kn.shape
    KV = U // HEAD_DIM
    tq, tk, nQ, nK = _att_tiles(cfg, T)
    scale = HEAD_DIM ** -0.5
    c2 = scale * LOG2E
    R = nq * tq

    def body(q_ref, k_ref, v_ref, o_ref, lse_ref, m_sc, l_sc, acc):
        ik = pl.program_id(2)

        @pl.when(ik == 0)
        def _():
            m_sc[...] = jnp.full(m_sc.shape, -jnp.inf, f32)
            l_sc[...] = jnp.zeros(l_sc.shape, f32)
            acc[...] = jnp.zeros(acc.shape, f32)

        k, v = k_ref[...], v_ref[...]
        for h in range(nq):
            rows = pl.ds(h * tq, tq)
            s = _dotf(q_ref[:, h * HEAD_DIM:(h + 1) * HEAD_DIM], k, 1, 1)
            m_old = m_sc[rows, :]
            m_new = jnp.maximum(m_old, jnp.max(s, axis=-1, keepdims=True))
            alpha = jnp.exp2((m_old - m_new) * c2)
            p = jnp.exp2((s - m_new) * c2)
            l_sc[rows, :] = alpha * l_sc[rows, :] + jnp.sum(p, axis=-1, keepdims=True)
            acc[rows, :] = alpha * acc[rows, :] + jnp.dot(p.astype(bf16), v, preferred_element_type=f32)
            m_sc[rows, :] = m_new

        @pl.when(ik == nK - 1)
        def _():
            o_ref[...] = _unstack_heads(acc[...] / l_sc[...], nq)
            lse_ref[...] = m_sc[...] * scale + jnp.log(l_sc[...])

    W = nq * HEAD_DIM
    q_spec = pl.BlockSpec((tq, W), lambda h, i, k: (i + cfg.cT, h))
    kv_spec = pl.BlockSpec((tk, HEAD_DIM), lambda h, i, k: (k, h))
    lse_spec = pl.BlockSpec((None, None, R, 1), lambda h, i, k: (h, i, 0, 0))
    return pl.pallas_call(
        body, name=name, grid=(KV, nQ, nK), in_specs=[q_spec, kv_spec, kv_spec], out_specs=[q_spec, lse_spec],
        out_shape=[_sds((T, nq * U), f32), _sds((KV, nQ, R, 1), f32)],
        scratch_shapes=[pltpu.VMEM((R, 1), f32), pltpu.VMEM((R, 1), f32), pltpu.VMEM((R, HEAD_DIM), f32)],
        compiler_params=_params(("arbitrary", "arbitrary", "arbitrary")))(qn, kn, vb)


def att_bwd_dq(cfg, qn, kn, vb, o, lse, do, nq, name):
    T, U = kn.shape
    KV = U // HEAD_DIM
    tq, tk, nQ, nK = _att_tiles(cfg, T)
    scale = HEAD_DIM ** -0.5
    c2 = scale * LOG2E
    R = nq * tq
    W = nq * HEAD_DIM

    def body(q_ref, k_ref, v_ref, o_ref, lse_ref, do_ref, dq_ref, acc, dl):
        ik = pl.program_id(2)

        @pl.when(ik == 0)
        def _():
            acc[...] = jnp.zeros(acc.shape, f32)
            dl[...] = jnp.sum(_stack_heads(do_ref[...] * o_ref[...], nq), axis=-1, keepdims=True)

        k, v = k_ref[...], v_ref[...]
        for h in range(nq):
            rows = pl.ds(h * tq, tq)
            cols = slice(h * HEAD_DIM, (h + 1) * HEAD_DIM)
            s = _dotf(q_ref[:, cols], k, 1, 1)
            p = jnp.exp2(s * c2 - lse_ref[rows, :] * LOG2E)
            dp = _dotf(do_ref[:, cols].astype(bf16), v, 1, 1)
            ds = (p * (dp - dl[rows, :]) * scale).astype(bf16)
            acc[rows, :] += jnp.dot(ds, k, preferred_element_type=f32)

        @pl.when(ik == nK - 1)
        def _():
            dq_ref[...] = _unstack_heads(acc[...], nq)

    q_spec = pl.BlockSpec((tq, W), lambda h, i, k: (i + cfg.cT, h))
    kv_spec = pl.BlockSpec((tk, HEAD_DIM), lambda h, i, k: (k, h))
    lse_spec = pl.BlockSpec((None, None, R, 1), lambda h, i, k: (h, i, 0, 0))
    return pl.pallas_call(
        body, name=name, grid=(KV, nQ, nK), in_specs=[q_spec, kv_spec, kv_spec, q_spec, lse_spec, q_spec], out_specs=q_spec,
        out_shape=_sds((T, nq * U), f32), scratch_shapes=[pltpu.VMEM((R, HEAD_DIM), f32), pltpu.VMEM((R, 1), f32)],
        compiler_params=_params(("arbitrary", "arbitrary", "arbitrary")))(qn, kn, vb, o, lse, do)


def att_bwd_dkv(cfg, qn, kn, vb, o, lse, do, nq, name):
    T, U = kn.shape
    KV = U // HEAD_DIM
    tq, tk, nQ, nK = _att_tiles(cfg, T)
    scale = HEAD_DIM ** -0.5
    c2 = scale * LOG2E
    R = nq * tq
    W = nq * HEAD_DIM

    def body(q_ref, k_ref, v_ref, o_ref, lse_ref, do_ref, dk_ref, dv_ref, dk_acc, dv_acc):
        iq = pl.program_id(2)

        @pl.when(iq == 0)
        def _():
            dk_acc[...] = jnp.zeros(dk_acc.shape, f32)
            dv_acc[...] = jnp.zeros(dv_acc.shape, f32)

        k, v = k_ref[...], v_ref[...]
        for h in range(nq):
            rows = pl.ds(h * tq, tq)
            cols = slice(h * HEAD_DIM, (h + 1) * HEAD_DIM)
            qh = q_ref[:, cols]
            doh = do_ref[:, cols]
            dl = jnp.sum(doh * o_ref[:, cols], axis=-1, keepdims=True)
            p = jnp.exp2(_dotf(qh, k, 1, 1) * c2 - lse_ref[rows, :] * LOG2E)
            do_bf = doh.astype(bf16)
            dv_acc[...] += _dotf(p.astype(bf16), do_bf, 0, 0)
            dp = _dotf(do_bf, v, 1, 1)
            ds = (p * (dp - dl) * scale).astype(bf16)
            dk_acc[...] += _dotf(ds, qh, 0, 0)

        @pl.when(iq == nQ - 1)
        def _():
            dk_ref[...] = dk_acc[...]
            dv_ref[...] = dv_acc[...]

    q_spec = pl.BlockSpec((tq, W), lambda h, k, i: (i + cfg.cT, h))
    kv_spec = pl.BlockSpec((tk, HEAD_DIM), lambda h, k, i: (k, h))
    lse_spec = pl.BlockSpec((None, None, R, 1), lambda h, k, i: (h, i, 0, 0))
    return pl.pallas_call(
        body, name=name, grid=(KV, nK, nQ), in_specs=[q_spec, kv_spec, kv_spec, q_spec, lse_spec, q_spec],
        out_specs=[kv_spec, kv_spec], out_shape=[_sds((T, U), f32), _sds((T, U), f32)],
        scratch_shapes=[pltpu.VMEM((tk, HEAD_DIM), f32), pltpu.VMEM((tk, HEAD_DIM), f32)],
        compiler_params=_params(("arbitrary", "arbitrary", "arbitrary")))(qn, kn, vb, o, lse, do)


def od_mix(cfg, pooled, o, name):
    T, P = pooled.shape
    QW = o.shape[1]
    cT = cfg.cT

    def fn(i, p_ref, o_ref, m_ref):
        @pl.when(i < cT)
        def _():
            m_ref[...] = jnp.zeros(m_ref.shape, bf16)

        @pl.when(i >= cT)
        def _():
            m_ref[...] = jnp.concatenate([p_ref[...], o_ref[...].astype(bf16)], axis=1)

    return rowcall(cfg, fn, name, [("row", pooled, P, 0), ("row", o, QW, 0)], [("row", _sds((T, P + QW), bf16), P + QW, 0)])[0]


def ev_mix(cfg, lru, ret, name):
    T, W = lru.shape
    RV = ret.shape[1]

    def fn(i, a_ref, b_ref, m_ref):
        m_ref[...] = jnp.concatenate([a_ref[...], b_ref[...]], axis=1)

    return rowcall(cfg, fn, name, [("row", lru, W, 0), ("row", ret, RV, 0)], [("row", _sds((T, W + RV), bf16), W + RV, 0)])[0]


def ev_dz_pack(cfg, dgl, dr, dq, dk, dv, dol, name):
    T, W = dgl.shape
    RV = dol.shape[1]
    width = 2 * W + 4 * RV

    def fn(i, g_ref, r_ref, q_ref, k_ref, v_ref, o_ref, dz_ref):
        parts = [g_ref[...], r_ref[...], q_ref[0] + q_ref[1], k_ref[0] + k_ref[1], v_ref[0] + v_ref[1], o_ref[...]]
        dz_ref[...] = jnp.concatenate([p.astype(bf16) for p in parts], axis=1)

    return rowcall(cfg, fn, name, [("row", dgl, W, 0), ("row", dr, W, 0), ("drow", dq, RV, 0), ("drow", dk, RV, 0),
                                   ("drow", dv, RV, 0), ("row", dol, RV, 0)], [("row", _sds((T, width), bf16), width, 0)])[0]


def loss_fwd_bwd(cfg, xf, target, name):
    T, D = xf.shape
    TR, cT = cfg.TR, cfg.cT

    def body(x_ref, t_ref, sq_ref, dx_ref):
        i = pl.program_id(0)

        @pl.when(i == 0)
        def _():
            sq_ref[...] = jnp.zeros(sq_ref.shape, f32)

        @pl.when(i < cT)
        def _():
            dx_ref[...] = jnp.zeros(dx_ref.shape, f32)

        @pl.when(i >= cT)
        def _():
            diff = x_ref[...] - t_ref[...]
            sq_ref[...] += jnp.sum(diff * diff, axis=0, keepdims=True)
            dx_ref[...] = diff / D

    row = pl.BlockSpec((TR, D), lambda i: (i, 0))
    trow = pl.BlockSpec((TR, D), lambda i: (jnp.maximum(i - cT, 0), 0))
    return pl.pallas_call(body, name=name, grid=(cfg.nT,), in_specs=[row, trow],
                          out_specs=[pl.BlockSpec((1, D), lambda i: (0, 0)), row],
                          out_shape=[_sds((1, D), f32), _sds((T, D), f32)], compiler_params=_params(("arbitrary",)))(xf, target)


MOD_ROWS = 16


def mod_fwd(s16, mod_w, name):
    nL, D, C4 = mod_w.shape
    tc = _pick(C4, (512, 256, 128))

    def body(s_ref, w_ref, o_ref):
        o_ref[...] = jnp.dot(s_ref[...], w_ref[...], precision=lax.Precision.HIGHEST, preferred_element_type=f32)

    return pl.pallas_call(
        body, name=name, grid=(nL, C4 // tc),
        in_specs=[pl.BlockSpec((MOD_ROWS, D), lambda l, j: (0, 0)), pl.BlockSpec((None, D, tc), lambda l, j: (l, 0, j))],
        out_specs=pl.BlockSpec((None, MOD_ROWS, tc), lambda l, j: (l, 0, j)), out_shape=_sds((nL, MOD_ROWS, C4), f32),
        compiler_params=_params(("arbitrary", "arbitrary")))(s16, mod_w)


def mod_bwd(s16, dm16, mod_w, name):
    nL, D, C4 = mod_w.shape
    tc = _pick(C4, (512, 256, 128))
    half = MOD_ROWS // 2

    def body(s_ref, d_ref, w_ref, g_ref, dc_ref):
        first = jnp.logical_and(pl.program_id(0) == 0, pl.program_id(1) == 0)
        g_ref[...] = lax.dot_general(s_ref[...], d_ref[...], (((0,), (0,)), ((), ())), precision=lax.Precision.HIGHEST,
                                     preferred_element_type=f32)
        part = lax.dot_general(d_ref[...], w_ref[...], (((1,), (1,)), ((), ())), precision=lax.Precision.HIGHEST,
                               preferred_element_type=f32)
        _acc(dc_ref, jnp.sum(part[half:], axis=0, keepdims=True), first)

    return pl.pallas_call(
        body, name=name, grid=(nL, C4 // tc),
        in_specs=[pl.BlockSpec((MOD_ROWS, D), lambda l, j: (0, 0)), pl.BlockSpec((None, MOD_ROWS, tc), lambda l, j: (l, 0, j)),
                  pl.BlockSpec((None, D, tc), lambda l, j: (l, 0, j))],
        out_specs=[pl.BlockSpec((None, D, tc), lambda l, j: (l, 0, j)), pl.BlockSpec((1, D), lambda l, j: (0, 0))],
        out_shape=[_sds((nL, D, C4), f32), _sds((1, D), f32)],
        compiler_params=_params(("arbitrary", "arbitrary")))(s16, dm16, mod_w)


def _as2d(a):
    return a.reshape(-1, a.shape[-1])


ELEMENTWISE_VMEM = 24 * 1024 * 1024


def _tiles2d(shape, n_arrays):
    R, C = shape
    tc = _pick(C, (1536, 1408, 1024, 768, 512, 256, 128))
    fits = [t for t in (512, 256, 128, 64, 32, 16, 8) if R % t == 0 and t * tc * 4 * 2 * n_arrays <= ELEMENTWISE_VMEM]
    return (fits[0] if fits else R), tc


def cast_bf16(a, name):
    a2 = _as2d(a)
    tr, tc = _tiles2d(a2.shape, 2)

    def body(a_ref, o_ref):
        o_ref[...] = a_ref[...].astype(bf16)

    spec = pl.BlockSpec((tr, tc), lambda i, j: (i, j))
    out = pl.pallas_call(body, name=name, grid=(a2.shape[0] // tr, a2.shape[1] // tc), in_specs=[spec], out_specs=spec,
                         out_shape=_sds(a2.shape, bf16), compiler_params=_params(("arbitrary", "arbitrary")))(a2)
    return out.reshape(a.shape)


def sum_leading(a, name, *, into=None, full_shape=None, widx=()):
    n = a.shape[0]
    a3 = a.reshape(n, -1, a.shape[-1])
    tr, tc = _tiles2d(a3.shape[1:], n + 1)

    def body(a_ref, *rest):
        o_ref = rest[-1]
        tot = a_ref[0].astype(f32)
        for k in range(1, n):
            tot = tot + a_ref[k].astype(f32)
        o_ref[...] = tot

    grid = (a3.shape[1] // tr, a3.shape[2] // tc)
    in_specs = [pl.BlockSpec((n, tr, tc), lambda i, j: (0, i, j))]
    args = [a3]
    if not widx:
        out = pl.pallas_call(body, name=name, grid=grid, in_specs=in_specs,
                             out_specs=pl.BlockSpec((tr, tc), lambda i, j: (i, j)), out_shape=_sds(a3.shape[1:], f32),
                             compiler_params=_params(("arbitrary", "arbitrary")))(*args)
        return out.reshape(a.shape[1:])
    lead = tuple(full_shape[:len(widx)])
    flat = lead + tuple(a3.shape[1:])
    aliases = {}
    if into is not None:
        in_specs.append(pl.BlockSpec(memory_space=pl.ANY))
        args.append(into.reshape(flat))
        aliases = {1: 0}
    out = pl.pallas_call(body, name=name, grid=grid, in_specs=in_specs,
                         out_specs=pl.BlockSpec((None,) * len(widx) + (tr, tc), lambda i, j: tuple(widx) + (i, j)),
                         out_shape=_sds(flat, f32), input_output_aliases=aliases,
                         compiler_params=_params(("arbitrary", "arbitrary")))(*args)
    return out.reshape(full_shape)


def adamw(w, m, v, g_parts, name):
    w2, m2, v2 = _as2d(w), _as2d(m), _as2d(v)
    parts = [_as2d(p) for p in g_parts]
    tr, tc = _tiles2d(w2.shape, 7 + len(parts))
    npart = len(parts)

    def body(*refs):
        w_ref, m_ref, v_ref = refs[:3]
        p_refs = refs[3:3 + npart]
        g_ref, d_ref, nm_ref, nv_ref = refs[3 + npart:]
        g = p_refs[0][...]
        for p in p_refs[1:]:
            g = g + p[...]
        mn = ADAM_B1 * m_ref[...] + (1.0 - ADAM_B1) * g
        vn = ADAM_B2 * v_ref[...] + (1.0 - ADAM_B2) * jnp.square(g)
        m_hat = mn / (1.0 - ADAM_B1 ** ADAM_STEP)
        v_hat = vn / (1.0 - ADAM_B2 ** ADAM_STEP)
        g_ref[...] = g
        d_ref[...] = -ADAM_LR * (m_hat / (jnp.sqrt(v_hat) + ADAM_EPS) + ADAM_WD * w_ref[...])
        nm_ref[...] = mn
        nv_ref[...] = vn

    spec = pl.BlockSpec((tr, tc), lambda i, j: (i, j))
    outs = pl.pallas_call(body, name=name, grid=(w2.shape[0] // tr, w2.shape[1] // tc), in_specs=[spec] * (3 + npart),
                          out_specs=[spec] * 4, out_shape=[_sds(w2.shape, f32)] * 4,
                          compiler_params=_params(("arbitrary", "arbitrary")))(w2, m2, v2, *parts)
    return [o.reshape(w.shape) for o in outs]


def all_gather_small(a, name):
    R, C = a.shape

    def body(a_ref, out_ref, send_sems, recv_sems, local_sem):
        x, y, c = _coords()
        me = 4 * x + 2 * y + c
        mine = pltpu.make_async_copy(a_ref, out_ref.at[me], local_sem)
        mine.start()
        copies = []
        for k in range(1, N_DEV):
            kx, ky, kc = (k >> 2) & 1, (k >> 1) & 1, k & 1
            peer = (_flip(x, kx), _flip(y, ky), _flip(c, kc))
            cp = pltpu.make_async_remote_copy(src_ref=a_ref, dst_ref=out_ref.at[me], send_sem=send_sems.at[k - 1],
                                              recv_sem=recv_sems.at[k - 1], device_id=peer, device_id_type=MESH)
            cp.start()
            copies.append((cp, 4 * peer[0] + 2 * peer[1] + peer[2], peer))
        for k, (cp, pidx, peer) in enumerate(copies):
            pltpu.make_async_remote_copy(src_ref=a_ref, dst_ref=out_ref.at[pidx], send_sem=send_sems.at[k],
                                         recv_sem=recv_sems.at[k], device_id=peer, device_id_type=MESH).wait_recv()
        for cp, _, _ in copies:
            cp.wait_send()
        mine.wait()

    return pl.pallas_call(
        body, name=name, out_shape=_sds((N_DEV, R, C), f32),
        in_specs=[pl.BlockSpec(memory_space=pltpu.VMEM)], out_specs=pl.BlockSpec(memory_space=pltpu.VMEM),
        scratch_shapes=[pltpu.SemaphoreType.DMA((N_DEV - 1,)), pltpu.SemaphoreType.DMA((N_DEV - 1,)), pltpu.SemaphoreType.DMA],
        compiler_params=pltpu.CompilerParams(vmem_limit_bytes=VMEM_LIMIT))(a)


def swap_with_sibling(parts, name):
    n = len(parts)

    def body(*refs):
        in_refs, out_refs = refs[:n], refs[n:2 * n]
        send_sems, recv_sems = refs[2 * n:]
        x, y, c = _coords()
        sends = []
        for w in range(n):
            cp = pltpu.make_async_remote_copy(src_ref=in_refs[w], dst_ref=out_refs[w], send_sem=send_sems.at[w],
                                              recv_sem=recv_sems.at[w], device_id=(x, y, 1 - c), device_id_type=MESH)
            cp.start()
            sends.append(cp)
        for cp in sends:
            cp.wait_recv()
        for cp in sends:
            cp.wait_send()

    hbm = pl.BlockSpec(memory_space=pl.ANY)
    return pl.pallas_call(
        body, name=name, out_shape=[_sds(a.shape, a.dtype) for a in parts], in_specs=[hbm] * n, out_specs=[hbm] * n,
        scratch_shapes=[pltpu.SemaphoreType.DMA((n,)), pltpu.SemaphoreType.DMA((n,))],
        )(*parts)


def even_fwd(cfg, x, p, tag, host=None):
    W, H = p["W"], p["H"]
    got = {}
    h = pre_fwd(cfg, x, p["g_pre"], p["shift"], p["scale"], tag + "_pre")
    z = _hosted(host, "in", got, lambda cm: matmul("v1", h, p["w_in"], comm=cm, name=tag + "_in"))
    u = conv_fwd(cfg, z, p["conv_w"], p["conv_b"], W, tag + "_conv")
    a, b = lru_coef_fwd(cfg, u, p["wa"], p["ba"], p["wx"], p["bx"], p["lam"], tag + "_coef")
    hh, hp = lru_scan(cfg, a, b, tag + "_scan")
    lru = lru_out_fwd(cfg, z, hh, W, tag + "_lruout")
    qcol = 2 * W // RET_DK
    o, st = ret_fwd(cfg, z, p["logit"], p["cos1"], p["sin1"], H, qcol, tag + "_ret")
    olcol = (2 * W + 3 * H * RET_DK) // (H * RET_DV)
    ret = ret_norm_fwd(cfg, o, z, p["gn"], H, olcol, tag + "_retnorm")
    mix = ev_mix(cfg, lru, ret, tag + "_mix")
    y = _hosted(host, "out", got, lambda cm: matmul("v2", mix, p["w_out"], comm=cm, name=tag + "_out"))
    xo = post_fwd(cfg, x, y, p["g_post"], p["gate"], 1.0, tag + "_post")
    return xo, (x, h, z, u, a, hh, hp, o, st, mix, y, olcol, qcol), got


def even_bwd(cfg, dX, saved, p, tag):
    x, h, z, u, a, hh, hp, o, st, mix, y, olcol, qcol = saved
    W, H = p["W"], p["H"]
    dy, dg_post, dgate = post_bwd(cfg, dX, y, p["g_post"], p["gate"], 1.0, tag + "_postb")
    dmix = matmul("v4", dy, p["w_out"], name=tag + "_dmix")
    g_out = matmul("v6", mix, dy, gshape=p["w_out"].shape, out_dtype=bf16, name=tag + "_gwout")
    dgl, dhs = lru_out_bwd(cfg, z, hh, dmix, W, tag + "_lruoutb")
    da, db = lru_scan_bwd(cfg, a, hp, dhs, tag + "_scanb")
    du, dwa, dba, dwx, dbx, dlam = lru_coef_bwd(cfg, u, da, db, p["wa"], p["ba"], p["wx"], p["bx"], p["lam"], tag + "_coefb")
    dr, dcw, dcb = conv_bwd(cfg, z, du, p["conv_w"], W, tag + "_convb")
    do, dol, dgn = ret_norm_bwd(cfg, o, z, p["gn"], dmix, H, olcol, W // (H * RET_DV), tag + "_retnormb")
    dq, dk, dv, dlg = ret_bwd(cfg, z, st, do, p["logit"], p["cos1"], p["sin1"], H, qcol, tag + "_retb")
    dz = ev_dz_pack(cfg, dgl, dr, dq, dk, dv, dol, tag + "_dz")
    g_in = matmul("v5", h, dz, gshape=p["w_in"].shape, out_dtype=bf16, name=tag + "_gwin")
    dh = matmul("v3", dz, p["w_in"], name=tag + "_dh")
    dX, dg_pre, dshift, dscale = pre_bwd(cfg, x, p["g_pre"], p["shift"], p["scale"], dh, dX, tag + "_preb")
    pg = dict(g_pre=dg_pre, g_post=dg_post, shift=dshift, scale=dscale, gate=dgate, conv_w=dcw, conv_b=dcb, wa=dwa,
              ba=dba, wx=dwx, bx=dbx, lam=dlam, logit=dlg, gn=dgn)
    return dX, pg, dict(w_in=g_in, w_out=g_out)


def odd_fwd(cfg, x, p, tag):
    nq = p["nq"]
    h = pre_fwd(cfg, x, p["g_pre"], p["shift"], p["scale"], tag + "_pre")
    z = matmul("v1", h, p["w_in"], name=tag + "_in")
    pooled = pool_fwd(cfg, z, p["pool_w"], p["pool_scale"], tag + "_pool")
    qn, kn, vb = att_prep(cfg, z, p["qg"], p["kg"], p["cosf"], p["sinf"], nq, tag + "_prep")
    o, lse = att_fwd(cfg, qn, kn, vb, nq, tag + "_att")
    mix = od_mix(cfg, pooled, o, tag + "_mix")
    y = matmul("v2", mix, p["w_out"], name=tag + "_out")
    xo = post_fwd(cfg, x, y, p["g_post"], p["gate"], 1.0, tag + "_post")
    return xo, (x, h, z, qn, kn, vb, o, lse, mix, y)


def odd_bwd(cfg, dX, saved, p, tag):
    x, h, z, qn, kn, vb, o, lse, mix, y = saved
    nq = p["nq"]
    U = kn.shape[1]
    dy, dg_post, dgate = post_bwd(cfg, dX, y, p["g_post"], p["gate"], 1.0, tag + "_postb")
    dmix = matmul("v4", dy, p["w_out"], name=tag + "_dmix")
    g_out = matmul("v6", mix, dy, gshape=p["w_out"].shape, out_dtype=bf16, name=tag + "_gwout")
    dm, dmn, dpw, dps = pool_bwd_a(cfg, z, dmix, p["pool_w"], p["pool_scale"], tag + "_poolb")
    dxp = pool_bwd_b(cfg, dm, dmn, tag + "_poolb2")
    do = dmix[:, U:]
    dqn = att_bwd_dq(cfg, qn, kn, vb, o, lse, do, nq, tag + "_attdq")
    dkn, dvb = att_bwd_dkv(cfg, qn, kn, vb, o, lse, do, nq, tag + "_attdkv")
    dz, dqg, dkg = att_prep_bwd(cfg, z, p["qg"], p["kg"], p["cosf"], p["sinf"], dqn, dkn, dvb, dxp, nq, tag + "_prepb")
    g_in = matmul("v5", h, dz, gshape=p["w_in"].shape, out_dtype=bf16, name=tag + "_gwin")
    dh = matmul("v3", dz, p["w_in"], name=tag + "_dh")
    dX, dg_pre, dshift, dscale = pre_bwd(cfg, x, p["g_pre"], p["shift"], p["scale"], dh, dX, tag + "_preb")
    pg = dict(g_pre=dg_pre, g_post=dg_post, shift=dshift, scale=dscale, gate=dgate, pool_w=dpw, pool_scale=dps, qg=dqg, kg=dkg)
    return dX, pg, dict(w_in=g_in, w_out=g_out)


WEIGHT_NAMES = ("c_ctx", "mod_w", "mod_b", "norm_pre", "norm_post", "ffn_gate", "ffn_up", "ffn_down", "ev_w_in", "ev_w_out",
                "lru_conv_w", "lru_conv_b", "lru_wa", "lru_ba", "lru_wx", "lru_bx", "lru_lambda", "ret_decay_logit", "ret_gn",
                "od_w_in", "od_w_out", "pool_w", "pool_scale", "q_norm", "k_norm")
BIG = ("ffn_gate", "ffn_up", "ffn_down", "ev_w_in", "ev_w_out", "od_w_in", "od_w_out")
SMALL_SHARDED = ("norm_pre", "norm_post", "lru_conv_w", "lru_ba", "lru_bx", "lru_lambda", "pool_scale")
SMALL_REPL = ("mod_b", "lru_conv_b", "lru_wa", "lru_wx", "ret_decay_logit", "ret_gn", "pool_w", "q_norm", "k_norm")
LANES = 128


PACK_ROWS = 512


def _rows_of(n):
    return -(-n // LANES)


def _pack(arrs):
    rows = []
    for a in arrs:
        flat = a.reshape(-1)
        rows.append(jnp.pad(flat, (0, _rows_of(flat.shape[0]) * LANES - flat.shape[0])).reshape(-1, LANES))
    total = sum(r.shape[0] for r in rows)
    rows.append(jnp.zeros(((-total) % PACK_ROWS, LANES), f32))
    return jnp.concatenate(rows), None


def _unpack(packed, shapes, lead=()):
    out, pos = [], 0
    for shp in shapes:
        n = math.prod(shp)
        r = _rows_of(n)
        piece = packed[..., pos:pos + r, :].reshape(lead + (r * LANES,))
        out.append(piece[..., :n].reshape(lead + tuple(shp)))
        pos += r
    return out


def _unshard(g):
    return jnp.moveaxis(g, 0, -2).reshape(g.shape[1:-1] + (g.shape[0] * g.shape[-1],))


def _rope_tables(S, Lc):
    n_r = RET_DK // 2
    f_r = RET_THETA ** (-jnp.arange(n_r, dtype=f32) / n_r)
    ang1 = jnp.arange(S, dtype=f32)[:, None] * f_r
    rows = S // GRID_W
    row = jnp.repeat(jnp.arange(rows, dtype=f32), GRID_W)
    col = jnp.tile(jnp.arange(GRID_W, dtype=f32), rows)
    n_ax = HEAD_DIM // 4
    f_ax = ROPE_THETA ** (-jnp.arange(n_ax, dtype=f32) / n_ax)
    ang2 = jnp.concatenate([row[:, None] * f_ax, col[:, None] * f_ax], axis=-1)
    cos2, sin2 = jnp.cos(ang2), jnp.sin(ang2)
    ones = lambda n: jnp.ones((Lc, n), f32)
    zeros = lambda n: jnp.zeros((Lc, n), f32)
    cos1 = jnp.concatenate([ones(n_r), jnp.cos(ang1)])
    sin1 = jnp.concatenate([zeros(n_r), jnp.sin(ang1)])
    cosf = jnp.concatenate([ones(HEAD_DIM), jnp.concatenate([cos2, cos2], axis=1)])
    sinf = jnp.concatenate([zeros(HEAD_DIM), jnp.concatenate([-sin2, sin2], axis=1)])
    return cos1, sin1, cosf, sinf


def kernel(x, c, ctx, c_ctx, mod_w, mod_b, norm_pre, norm_post, ffn_gate, ffn_up, ffn_down, ev_w_in, ev_w_out, lru_conv_w, lru_conv_b, lru_wa, lru_ba, lru_wx, lru_bx, lru_lambda, ret_decay_logit, ret_gn, od_w_in, od_w_out, pool_w, pool_scale, q_norm, k_norm, loss_target, m_c_ctx, m_mod_w, m_mod_b, m_norm_pre, m_norm_post, m_ffn_gate, m_ffn_up, m_ffn_down, m_ev_w_in, m_ev_w_out, m_lru_conv_w, m_lru_conv_b, m_lru_wa, m_lru_ba, m_lru_wx, m_lru_bx, m_lru_lambda, m_ret_decay_logit, m_ret_gn, m_od_w_in, m_od_w_out, m_pool_w, m_pool_scale, m_q_norm, m_k_norm, v_c_ctx, v_mod_w, v_mod_b, v_norm_pre, v_norm_post, v_ffn_gate, v_ffn_up, v_ffn_down, v_ev_w_in, v_ev_w_out, v_lru_conv_w, v_lru_conv_b, v_lru_wa, v_lru_ba, v_lru_wx, v_lru_bx, v_lru_lambda, v_ret_decay_logit, v_ret_gn, v_od_w_in, v_od_w_out, v_pool_w, v_pool_scale, v_q_norm, v_k_norm):
    wts = dict(c_ctx=c_ctx, mod_w=mod_w, mod_b=mod_b, norm_pre=norm_pre, norm_post=norm_post, ffn_gate=ffn_gate, ffn_up=ffn_up,
               ffn_down=ffn_down, ev_w_in=ev_w_in, ev_w_out=ev_w_out, lru_conv_w=lru_conv_w, lru_conv_b=lru_conv_b,
               lru_wa=lru_wa, lru_ba=lru_ba, lru_wx=lru_wx, lru_bx=lru_bx, lru_lambda=lru_lambda,
               ret_decay_logit=ret_decay_logit, ret_gn=ret_gn, od_w_in=od_w_in, od_w_out=od_w_out, pool_w=pool_w,
               pool_scale=pool_scale, q_norm=q_norm, k_norm=k_norm)
    mom_m = dict(zip(WEIGHT_NAMES, (m_c_ctx, m_mod_w, m_mod_b, m_norm_pre, m_norm_post, m_ffn_gate, m_ffn_up, m_ffn_down,
                                    m_ev_w_in, m_ev_w_out, m_lru_conv_w, m_lru_conv_b, m_lru_wa, m_lru_ba, m_lru_wx, m_lru_bx,
                                    m_lru_lambda, m_ret_decay_logit, m_ret_gn, m_od_w_in, m_od_w_out, m_pool_w, m_pool_scale,
                                    m_q_norm, m_k_norm)))
    mom_v = dict(zip(WEIGHT_NAMES, (v_c_ctx, v_mod_w, v_mod_b, v_norm_pre, v_norm_post, v_ffn_gate, v_ffn_up, v_ffn_down,
                                    v_ev_w_in, v_ev_w_out, v_lru_conv_w, v_lru_conv_b, v_lru_wa, v_lru_ba, v_lru_wx, v_lru_bx,
                                    v_lru_lambda, v_ret_decay_logit, v_ret_gn, v_od_w_in, v_od_w_out, v_pool_w, v_pool_scale,
                                    v_q_norm, v_k_norm)))

    _, S, D = x.shape
    Lc = ctx.shape[1]
    T = Lc + S
    TR = 256 if (Lc % 256 == 0 and S % 256 == 0) else 128
    assert Lc % TR == 0 and S % TR == 0 and TR % RET_CHUNK == 0
    cfg = RowCfg(TR, T // TR, Lc // TR)
    W = lru_conv_b.shape[-1]
    H = ret_decay_logit.shape[-1]
    U = POOL_GROUP * len(POOL_WINDOWS)
    nq = (N_CHIPS * od_w_in.shape[-1]) // U - 3
    assert W % (H * RET_DV) == 0 and (2 * W) % RET_DK == 0
    nL = mod_w.shape[0]
    C4 = mod_w.shape[-1]
    assert nL == 2, "two layers: an even mixer then an odd one"

    xi, yi, ci = _coords()
    chip = 2 * xi + yi
    me = 4 * xi + 2 * yi + ci

    sc = jax.nn.silu(c)
    small_in, _ = _pack([sc] + [wts[n] for n in SMALL_SHARDED])
    g1 = all_gather_small(small_in, "gather_small_fwd")
    parts = _unpack(g1, [sc.shape] + [wts[n].shape for n in SMALL_SHARDED], lead=(N_DEV,))
    sc_all = parts[0][:, 0]
    full = {n: _unshard(parts[1 + i][0::2]) for i, n in enumerate(SMALL_SHARDED)}
    for n in SMALL_REPL + ("c_ctx",):
        full[n] = wts[n]

    scc = jax.nn.silu(c_ctx)[None]
    pad_rows = MOD_ROWS - N_DEV - 1
    s16 = jnp.concatenate([sc_all, scc, jnp.zeros((pad_rows, D), f32)])
    modp = mod_fwd(s16, mod_w, "mod_fwd")
    g2 = all_gather_small(modp.reshape(-1, LANES), "gather_mod")
    mod_all = g2.reshape(N_DEV, nL, MOD_ROWS, C4)[0::2]
    mod_all = jnp.moveaxis(mod_all, 0, 2).reshape(nL, MOD_ROWS, N_CHIPS * C4) + mod_b[:, None, :]
    mod_l = lax.dynamic_index_in_dim(mod_all, me, axis=1, keepdims=False).reshape(nL, 3, 3, D)
    mod_c = mod_all[:, N_DEV].reshape(nL, 3, 3, D)

    def mod_of(li, s, kind, ctx_live=True):
        cpart = mod_c[li, s, kind] if ctx_live else jnp.zeros((D,), f32)
        return jnp.stack([cpart, mod_l[li, s, kind]])[:, None, :]

    packed = {n: cast_bf16(wts[n], "cast_" + n) for n in BIG}
    ffn_units = [(0, 0), (0, 1), (1, 0), (1, 1)]

    def ffn_gather(u):
        return {r: Comm("gather", [(packed[n], u)]) for r, n in (("gate", "ffn_gate"), ("up", "ffn_up"), ("down", "ffn_down"))}

    first = exchange(Comm("gather", [(packed["ffn_gate"], (0, 0)), (packed["ffn_up"], (0, 0)), (packed["ffn_down"], (0, 0)),
                                     (packed["ev_w_in"], (0,)), (packed["ev_w_out"], (0,))]), "gather_first")
    wunit = {(0, 0): dict(wg=first[0], wu=first[1], wd=first[2])}
    w_ev = dict(w_in=first[3], w_out=first[4])

    cos1, sin1, cosf, sinf = _rope_tables(S, Lc)

    def sub_params(li, s, ctx_live=True, gate_ctx_live=True):
        return dict(g_pre=full["norm_pre"][li, s][None], g_post=full["norm_post"][li, s][None],
                    shift=mod_of(li, s, 0, ctx_live), scale=mod_of(li, s, 1, ctx_live),
                    gate=mod_of(li, s, 2, ctx_live and gate_ctx_live))

    def ffn_params(u, ctx_live=True):
        p = sub_params(u[0], 2 * u[1], ctx_live)
        p.update(wunit[u])
        return p

    def take(got):
        return dict(wg=got["gate"][0], wu=got["up"][0], wd=got["down"][0])

    X0 = jnp.concatenate([ctx[0], x[0]], axis=0)
    p00 = ffn_params((0, 0))
    X1, s00, got = ffn_fwd(cfg, X0, p00, "l0f0", host=ffn_gather((0, 1)))
    wunit[(0, 1)] = take(got)
    p01 = sub_params(0, 1)
    p01.update(W=W, H=H, conv_w=full["lru_conv_w"][0], conv_b=full["lru_conv_b"], wa=full["lru_wa"][0],
               ba=full["lru_ba"][0][:, None, :], wx=full["lru_wx"][0], bx=full["lru_bx"][0][:, None, :],
               lam=full["lru_lambda"][0][:, None, :], logit=full["ret_decay_logit"][0][:, :, None, None],
               gn=full["ret_gn"], cos1=cos1, sin1=sin1, **w_ev)
    X2, s01, got = even_fwd(cfg, X1, p01, "l0mix", host={"in": Comm("gather", [(packed["od_w_in"], (0,))]),
                                                          "out": Comm("gather", [(packed["od_w_out"], (0,))])})
    w_od = dict(w_in=got["in"][0], w_out=got["out"][0])
    p02 = ffn_params((0, 1))
    X3, s02, got = ffn_fwd(cfg, X2, p02, "l0f1", host=ffn_gather((1, 0)))
    wunit[(1, 0)] = take(got)
    p10 = ffn_params((1, 0))
    X4, s10, got = ffn_fwd(cfg, X3, p10, "l1f0", host=ffn_gather((1, 1)))
    wunit[(1, 1)] = take(got)
    p11 = sub_params(1, 1, gate_ctx_live=False)
    p11.update(nq=nq, pool_w=full["pool_w"][0], pool_scale=full["pool_scale"], qg=full["q_norm"], kg=full["k_norm"],
               cosf=cosf, sinf=sinf, **w_od)
    X5, s11 = odd_fwd(cfg, X4, p11, "l1mix")
    p12 = ffn_params((1, 1), ctx_live=False)
    X6, s12, _ = ffn_fwd(cfg, X5, p12, "l1f1")
    sq, dX = loss_fwd_bwd(cfg, X6, loss_target[0], "loss")
    loss = lax.psum(0.5 * jnp.sum(sq) / D, ("x", "y", "c"))

    def ffn_scatter(gw, extra=None):
        host = {"du": Comm("scatter", [(gw["wd"], ())]), "dh1": Comm("scatter", [(gw["wg"], ())]),
                "dh2": Comm("scatter", [(gw["wu"], ())])}
        if extra is not None:
            host["gwd"] = Comm("scatter", [(extra["w_in"], ())])
            host["gwg"] = Comm("scatter", [(extra["w_out"], ())])
        return host

    def took(got):
        return dict(wd=got["du"][0], wg=got["dh1"][0], wu=got["dh2"][0])

    recv_ffn = {}
    dX, g12, gw11, _ = ffn_bwd(cfg, dX, s12, p12, "l1f1")
    dX, g11, gw_od = odd_bwd(cfg, dX, s11, p11, "l1mix")
    dX, g10, gw10, got = ffn_bwd(cfg, dX, s10, p10, "l1f0", host=ffn_scatter(gw11))
    recv_ffn[(1, 1)] = took(got)
    dX, g02, gw01, got = ffn_bwd(cfg, dX, s02, p02, "l0f1", host=ffn_scatter(gw10, gw_od))
    recv_ffn[(1, 0)] = took(got)
    recv_od = dict(w_in=got["gwd"][0], w_out=got["gwg"][0])
    dX, g01, gw_ev = even_bwd(cfg, dX, s01, p01, "l0mix")
    dX, g00, gw00, got = ffn_bwd(cfg, dX, s00, p00, "l0f0", host=ffn_scatter(gw01, gw_ev))
    recv_ffn[(0, 1)] = took(got)
    recv_ev = dict(w_in=got["gwd"][0], w_out=got["gwg"][0])
    last = exchange(Comm("scatter", [(gw00["wd"], ()), (gw00["wg"], ()), (gw00["wu"], ())]), "scatter_last")
    recv_ffn[(0, 0)] = dict(wd=last[0], wg=last[1], wu=last[2])
    grad_x = dX[Lc:][None]

    subs = [[g00, g01, g02], [g10, g11, g12]]
    zero_d = jnp.zeros((D,), f32)

    def dmod(group, live):
        rows = []
        for li in range(nL):
            for s in range(3):
                for kind, key in enumerate(("shift", "scale", "gate")):
                    rows.append(subs[li][s][key][group, 0] if live(li, s, kind) else zero_d)
        return jnp.stack(rows).reshape(nL, 9 * D)

    dmod_l = dmod(1, lambda li, s, kind: True)
    dmod_c = dmod(0, lambda li, s, kind: not (li == 1 and (s == 2 or (s == 1 and kind == 2))))

    dm_in, _ = _pack([dmod_l, dmod_c])
    g3 = all_gather_small(dm_in, "gather_dmod")
    dl_all, dc_all = _unpack(g3, [dmod_l.shape, dmod_c.shape], lead=(N_DEV,))
    dm16 = jnp.moveaxis(jnp.concatenate([dl_all, dc_all], axis=0), 0, 1)
    dm16 = lax.dynamic_slice_in_dim(dm16, chip * C4, C4, axis=2)
    s16b = jnp.concatenate([sc_all, jnp.broadcast_to(scc, (N_DEV, D))])
    g_mod_w, dscc_part = mod_bwd(s16b, dm16, mod_w, "mod_bwd")

    norm_pre_g = jnp.stack([jnp.concatenate([subs[li][s]["g_pre"] for s in range(3)]) for li in range(nL)])
    norm_post_g = jnp.stack([jnp.concatenate([subs[li][s]["g_post"] for s in range(3)]) for li in range(nL)])
    small_g = dict(norm_pre=norm_pre_g, norm_post=norm_post_g, lru_conv_w=g01["conv_w"][None], lru_ba=g01["ba"][:, 0][None],
                   lru_bx=g01["bx"][:, 0][None], lru_lambda=g01["lam"][:, 0][None], pool_scale=g11["pool_scale"],
                   mod_b=dmod_l + dmod_c, lru_conv_b=g01["conv_b"], lru_wa=g01["wa"][None], lru_wx=g01["wx"][None],
                   ret_decay_logit=g01["logit"][:, :, 0, 0][None], ret_gn=g01["gn"], pool_w=g11["pool_w"][None],
                   q_norm=g11["qg"], k_norm=g11["kg"])
    names = SMALL_SHARDED + SMALL_REPL
    sg_in, _ = _pack([small_g[n] for n in names] + [dscc_part])
    g4 = all_gather_small(sg_in, "gather_small_grads")
    tot = sum_leading(g4, "sum_small_grads")
    tot_parts = _unpack(tot, [small_g[n].shape for n in names])
    dscc_all = _unpack(g4, [small_g[n].shape for n in names] + [dscc_part.shape], lead=(N_DEV,))[-1]
    dscc = dscc_all[0, 0] + dscc_all[2, 0] + dscc_all[4, 0] + dscc_all[6, 0]
    _, silu_vjp = jax.vjp(jax.nn.silu, c_ctx)
    grads = {"c_ctx": silu_vjp(dscc)[0]}
    for n, g in zip(names, tot_parts):
        if n in SMALL_SHARDED:
            k = wts[n].shape[-1]
            g = lax.dynamic_slice_in_dim(g, chip * k, k, axis=g.ndim - 1)
        grads[n] = g.reshape(wts[n].shape)

    partial = {}
    for n, key in (("ffn_gate", "wg"), ("ffn_up", "wu"), ("ffn_down", "wd")):
        acc = None
        for u in ffn_units:
            acc = sum_leading(recv_ffn[u][key], "sum_%s_%d%d" % (n, u[0], u[1]), into=acc, full_shape=wts[n].shape, widx=u)
        partial[n] = acc
    for n, r in (("ev_w_in", recv_ev["w_in"]), ("ev_w_out", recv_ev["w_out"]), ("od_w_in", recv_od["w_in"]),
                 ("od_w_out", recv_od["w_out"])):
        partial[n] = sum_leading(r, "sum_" + n).reshape(wts[n].shape)
    partial = [partial[n] for n in BIG]
    other = swap_with_sibling(partial, "swap_partials")

    delta, new_m, new_v = {}, {}, {}
    for n, pa, pb in zip(BIG, partial, other):
        grads[n], delta[n], new_m[n], new_v[n] = adamw(wts[n], mom_m[n], mom_v[n], [pa, pb], "adamw_" + n)
    grads["mod_w"], delta["mod_w"], new_m["mod_w"], new_v["mod_w"] = adamw(mod_w, m_mod_w, v_mod_w, [g_mod_w], "adamw_mod_w")
    snames = [n for n in WEIGHT_NAMES if n not in BIG and n != "mod_w"]
    pk = lambda d: _pack([d[n] for n in snames])[0]
    sres = adamw(pk(wts), pk(mom_m), pk(mom_v), [pk(grads)], "adamw_small")
    for res, dst in zip(sres[1:], (delta, new_m, new_v)):
        for n, a in zip(snames, _unpack(res, [wts[n].shape for n in snames])):
            dst[n] = a

    return (loss, grad_x, *[grads[n] for n in WEIGHT_NAMES], *[delta[n] for n in WEIGHT_NAMES],
            *[new_m[n] for n in WEIGHT_NAMES], *[new_v[n] for n in WEIGHT_NAMES])
```

```python
import functools
import math

import jax
import jax.numpy as jnp
from jax import lax
from jax.experimental import pallas as pl
from jax.experimental.pallas import tpu as pltpu

f32 = jnp.float32
bf16 = jnp.bfloat16
MESH = pl.DeviceIdType.MESH

EPS = 1e-6
FFN_STEP = 0.5
LRU_C = 8.0
CONV_W = 4
CONV_LEFT = 2
RET_DK = 256
RET_DV = 256
RET_CHUNK = 128
RET_THETA = 10000.0
POOL_WINDOWS = (2, 4, 8, 16)
POOL_GROUP = 128
HEAD_DIM = 128
ROPE_THETA = 10000.0
GRID_W = 64
ADAM_LR = 0.001
ADAM_B1 = 0.9
ADAM_B2 = 0.999
ADAM_EPS = 1e-08
ADAM_WD = 0.01
ADAM_STEP = 10

N_CHIPS = 4
N_DEV = 8
HALO = 8
VMEM_LIMIT = 56 * 1024 * 1024


def _params(sem=None):
    return pltpu.CompilerParams(dimension_semantics=sem, vmem_limit_bytes=VMEM_LIMIT)


def _pick(n, prefs):
    for p in prefs:
        if n % p == 0:
            return p
    return n


def _sds(shape, dtype):
    return jax.ShapeDtypeStruct(tuple(shape), dtype)


_MM_KINDS = {
    "v1": ((1, 0), "out[:, g] = A @ W[g]"),
    "v2": ((1, 0), "out = sum_g A[:, g] @ W[g]"),
    "v3": ((1, 1), "out = sum_g A[:, g] @ W[g]^T"),
    "v4": ((1, 1), "out[:, g] = A @ W[g]^T"),
    "v5": ((0, 0), "out[g] = A^T @ C[:, g]"),
    "v6": ((0, 0), "out[g] = A[:, g]^T @ C"),
}


def matmul(kind, a, b, *, widx=(), out_dtype=f32, init=None, gshape=None, comm=None, name):
    nw = len(widx)
    cdims = _MM_KINDS[kind][0]
    if kind in ("v1", "v2", "v3", "v4"):
        G = b.shape[0]
        d1, d2 = b.shape[-2:]
        M = a.shape[0]
    else:
        G, d1, d2 = gshape
        M = a.shape[0]
    tm_p, tn_p, tk_p = (768, 512, 256, 128), (1408, 1536, 1024, 768, 512, 256, 128), (2048, 1408, 1536, 1024, 768, 512, 256, 128)
    wnone = (None,) * (1 + nw)

    if kind == "v1":
        K, Ns = d1, d2
        tm, tn, tk = _pick(M, tm_p), _pick(Ns, tn_p), _pick(K, tk_p)
        nI, nJ, nR = M // tm, Ns // tn, K // tk
        grid = (G, nI, nJ, nR)
        a_spec = pl.BlockSpec((tm, tk), lambda g, i, j, r: (i, r))
        b_spec = pl.BlockSpec(wnone + (tk, tn), lambda g, i, j, r: (g,) + widx + (r, j))
        o_spec = pl.BlockSpec((tm, tn), lambda g, i, j, r: (i, g * nJ + j))
        out_shape = _sds((M, G * Ns), out_dtype)
        acc_shape = (tm, tn)
    elif kind == "v2":
        Ks, N = d1, d2
        tm, tn, tk = _pick(M, tm_p), _pick(N, tn_p), _pick(Ks, tk_p)
        nI, nJ, nRk = M // tm, N // tn, Ks // tk
        nR = G * nRk
        grid = (1, nI, nJ, nR)
        a_spec = pl.BlockSpec((tm, tk), lambda g, i, j, r: (i, r))
        b_spec = pl.BlockSpec(wnone + (tk, tn), lambda g, i, j, r: (r // nRk,) + widx + (r % nRk, j))
        o_spec = pl.BlockSpec((tm, tn), lambda g, i, j, r: (i, j))
        out_shape = _sds((M, N), out_dtype)
        acc_shape = (tm, tn)
    elif kind == "v3":
        K, Ns = d1, d2
        tm, tn, tk = _pick(M, tm_p), _pick(K, tn_p), _pick(Ns, tk_p)
        nI, nJ, nRk = M // tm, K // tn, Ns // tk
        nR = G * nRk
        grid = (1, nI, nJ, nR)
        a_spec = pl.BlockSpec((tm, tk), lambda g, i, j, r: (i, r))
        b_spec = pl.BlockSpec(wnone + (tn, tk), lambda g, i, j, r: (r // nRk,) + widx + (j, r % nRk))
        o_spec = pl.BlockSpec((tm, tn), lambda g, i, j, r: (i, j))
        out_shape = _sds((M, K), out_dtype)
        acc_shape = (tm, tn)
    elif kind == "v4":
        Ks, N = d1, d2
        tm, tn, tk = _pick(M, tm_p), _pick(Ks, tn_p), _pick(N, tk_p)
        nI, nJ, nR = M // tm, Ks // tn, N // tk
        grid = (G, nI, nJ, nR)
        a_spec = pl.BlockSpec((tm, tk), lambda g, i, j, r: (i, r))
        b_spec = pl.BlockSpec(wnone + (tn, tk), lambda g, i, j, r: (g,) + widx + (j, r))
        o_spec = pl.BlockSpec((tm, tn), lambda g, i, j, r: (i, g * nJ + j))
        out_shape = _sds((M, G * Ks), out_dtype)
        acc_shape = (tm, tn)
    elif kind == "v5":
        K, Ns = d1, d2
        tm, tn, tk = _pick(K, (2048,) + tm_p), _pick(Ns, tn_p), _pick(M, (768, 512, 256, 128))
        nI, nJ, nR = K // tm, Ns // tn, M // tk
        grid = (G, nI, nJ, nR)
        a_spec = pl.BlockSpec((tk, tm), lambda g, i, j, r: (r, i))
        b_spec = pl.BlockSpec((tk, tn), lambda g, i, j, r: (r, g * nJ + j))
        o_spec = pl.BlockSpec((None, tm, tn), lambda g, i, j, r: (g, i, j))
        out_shape = _sds(gshape, out_dtype)
        acc_shape = (tm, tn)
    else:
        Ks, N = d1, d2
        tm, tn, tk = _pick(Ks, (1408,) + tm_p), _pick(N, (2048,) + tn_p), _pick(M, (768, 512, 256, 128))
        nI, nJ, nR = Ks // tm, N // tn, M // tk
        grid = (G, nI, nJ, nR)
        a_spec = pl.BlockSpec((tk, tm), lambda g, i, j, r: (r, g * nI + i))
        b_spec = pl.BlockSpec((tk, tn), lambda g, i, j, r: (r, j))
        o_spec = pl.BlockSpec((None, tm, tn), lambda g, i, j, r: (g, i, j))
        out_shape = _sds(gshape, out_dtype)
        acc_shape = (tm, tn)

    has_init = init is not None
    ncomm = len(comm.srcs) if comm is not None else 0

    def body(*refs):
        a_ref, b_ref = refs[0], refs[1]
        pos = 2
        init_ref = None
        if has_init:
            init_ref = refs[pos]
            pos += 1
        cin = refs[pos:pos + ncomm]
        pos += ncomm
        o_ref = refs[pos]
        cout = refs[pos + 1:pos + 1 + ncomm]
        acc_ref = refs[pos + 1 + ncomm]
        sems = refs[pos + 2 + ncomm:]
        r = pl.program_id(3)
        first, last = _grid_ends(grid)

        if ncomm:
            @pl.when(first)
            def _():
                _comm_start(comm, cin, cout, *sems)

        @pl.when(r == 0)
        def _():
            if has_init:
                acc_ref[...] = init_ref[...]
            else:
                acc_ref[...] = jnp.zeros(acc_shape, f32)

        acc_ref[...] += lax.dot_general(a_ref[...], b_ref[...], ((cdims[:1], cdims[1:]), ((), ())),
                                        preferred_element_type=f32)

        @pl.when(r == nR - 1)
        def _():
            o_ref[...] = acc_ref[...].astype(o_ref.dtype)

        if ncomm:
            @pl.when(last)
            def _():
                _comm_wait(comm, cin, cout, *sems)

    in_specs = [a_spec, b_spec]
    args = [a, b]
    if has_init:
        in_specs.append(pl.BlockSpec((tm, tn), lambda g, i, j, r: (i, j)))
        args.append(init)
    out_specs, out_shapes, scratch = [o_spec], [out_shape], [pltpu.VMEM(acc_shape, f32)]
    if ncomm:
        hbm = pl.BlockSpec(memory_space=pl.ANY)
        in_specs += [hbm] * ncomm
        args += [src for src, _ in comm.srcs]
        out_specs += [hbm] * ncomm
        out_shapes += comm.out_shapes()
        scratch += _comm_sems(ncomm)
    res = pl.pallas_call(
        body, name=name, grid=grid, in_specs=in_specs, out_specs=out_specs, out_shape=out_shapes,
        scratch_shapes=scratch, compiler_params=_params(("arbitrary", "arbitrary", "arbitrary", "arbitrary")),
    )(*args)
    return (res[0], list(res[1:])) if ncomm else res[0]


class Comm:
    def __init__(self, mode, srcs):
        self.mode, self.srcs = mode, srcs

    def piece(self, n):
        arr, idx = self.srcs[n]
        shp = arr.shape[len(idx):]
        return shp if self.mode == "gather" else shp[1:]

    def out_shapes(self):
        return [_sds((N_CHIPS,) + tuple(self.piece(n)), self.srcs[n][0].dtype) for n in range(len(self.srcs))]


def _comm_sems(n):
    nsem = n * (N_CHIPS - 1)
    return [pltpu.SemaphoreType.DMA((nsem,)), pltpu.SemaphoreType.DMA((nsem,)), pltpu.SemaphoreType.DMA((n,))]


def _coords():
    return lax.axis_index("x"), lax.axis_index("y"), lax.axis_index("c")


def _flip(v, bit):
    return 1 - v if bit else v


def _chip_peers(x, y, c):
    out = []
    for k in range(1, N_CHIPS):
        kx, ky = (k >> 1) & 1, k & 1
        px, py = _flip(x, kx), _flip(y, ky)
        out.append((k, (px, py, c), 2 * px + py))
    return out


def _comm_copies(comm, in_refs, out_refs, send_sems, recv_sems, local_sems, with_recvs):
    x, y, c = _coords()
    s = 2 * x + y
    local, sends, recvs = [], [], []
    for w, (_, idx) in enumerate(comm.srcs):
        src = in_refs[w].at[idx] if idx else in_refs[w]
        out = out_refs[w]
        if comm.mode == "gather":
            local.append(pltpu.make_async_copy(src, out.at[s], local_sems.at[w]))
        else:
            local.append(pltpu.make_async_copy(src.at[s], out.at[N_CHIPS - 1], local_sems.at[w]))
        for k, peer, pidx in _chip_peers(x, y, c):
            j = w * (N_CHIPS - 1) + k - 1
            if comm.mode == "gather":
                out_src, out_dst, in_dst = src, out.at[s], out.at[pidx]
            else:
                out_src, out_dst, in_dst = src.at[pidx], out.at[k - 1], out.at[k - 1]
            sends.append(pltpu.make_async_remote_copy(src_ref=out_src, dst_ref=out_dst, send_sem=send_sems.at[j],
                                                      recv_sem=recv_sems.at[j], device_id=peer, device_id_type=MESH))
            if with_recvs:
                recvs.append(pltpu.make_async_remote_copy(src_ref=out_src, dst_ref=in_dst, send_sem=send_sems.at[j],
                                                          recv_sem=recv_sems.at[j], device_id=peer, device_id_type=MESH))
    return local, sends, recvs


def _comm_start(comm, in_refs, out_refs, send_sems, recv_sems, local_sems):
    local, sends, _ = _comm_copies(comm, in_refs, out_refs, send_sems, recv_sems, local_sems, False)
    for cp in local + sends:
        cp.start()


def _comm_wait(comm, in_refs, out_refs, send_sems, recv_sems, local_sems):
    local, sends, recvs = _comm_copies(comm, in_refs, out_refs, send_sems, recv_sems, local_sems, True)
    for cp in recvs:
        cp.wait_recv()
    for cp in sends:
        cp.wait_send()
    for cp in local:
        cp.wait()


def exchange(comm, name):
    n = len(comm.srcs)

    def body(*refs):
        in_refs, out_refs, sems = refs[:n], refs[n:2 * n], refs[2 * n:]
        _comm_start(comm, in_refs, out_refs, *sems)
        _comm_wait(comm, in_refs, out_refs, *sems)

    hbm = pl.BlockSpec(memory_space=pl.ANY)
    return pl.pallas_call(body, name=name, out_shape=comm.out_shapes(), in_specs=[hbm] * n, out_specs=[hbm] * n,
                          scratch_shapes=_comm_sems(n))(*[src for src, _ in comm.srcs])


class RowCfg:
    def __init__(self, TR, nT, cT):
        self.TR, self.nT, self.cT = TR, nT, cT


def _row_spec(cfg, spec, off):
    kind = spec[0]
    TR = cfg.TR
    hb = TR // HALO
    nH = cfg.nT * hb
    if kind == "row":
        _, arr, w, cb = spec
        return pl.BlockSpec((TR, w), lambda i: (i + off, cb))
    if kind == "prev":
        _, arr, w, cb = spec
        return pl.BlockSpec((HALO, w), lambda i: (jnp.maximum((i + off) * hb - 1, 0), cb))
    if kind == "next":
        _, arr, w, cb = spec
        return pl.BlockSpec((HALO, w), lambda i: (jnp.minimum((i + off + 1) * hb, nH - 1), cb))
    if kind == "full":
        arr = spec[1]
        nd = arr.ndim
        return pl.BlockSpec(arr.shape, lambda i: (0,) * nd)
    if kind == "grp":
        arr = spec[1]
        cT = cfg.cT
        return pl.BlockSpec((None, 1, arr.shape[-1]), lambda i: (((i + off) >= cT).astype(jnp.int32), 0, 0))
    if kind == "drow":
        _, arr, w, cb = spec
        return pl.BlockSpec((arr.shape[0], TR, w), lambda i: (0, i + off, cb))
    raise ValueError(kind)


def rowcall(cfg, fn, name, ins, outs, *, off=0, n=None, scratch=()):
    n = cfg.nT - off if n is None else n
    in_specs = [_row_spec(cfg, s, off) for s in ins]
    out_specs = [_row_spec(cfg, (s[0], s[1]) + tuple(s[2:]), off) for s in outs]
    out_shape = [s[1] for s in outs]

    def body(*refs):
        fn(pl.program_id(0) + off, *refs)

    res = pl.pallas_call(
        body, name=name, grid=(n,), in_specs=in_specs, out_specs=out_specs, out_shape=out_shape,
        scratch_shapes=list(scratch), compiler_params=_params(("arbitrary",)),
    )(*[s[1] for s in ins])
    return res


def _acc(ref, val, first):
    @pl.when(first)
    def _():
        ref[...] = val

    @pl.when(jnp.logical_not(first))
    def _():
        ref[...] += val


def _rms(x):
    return x * lax.rsqrt(jnp.mean(x * x, axis=-1, keepdims=True) + EPS)


def _pre_fn(x, g, shift, scale):
    return (_rms(x) * g) * (1.0 + scale) + shift


def pre_fwd(cfg, x, g, shift, scale, name):
    D = x.shape[1]

    def fn(i, x_ref, g_ref, sh_ref, sc_ref, h_ref):
        h_ref[...] = _pre_fn(x_ref[...], g_ref[...], sh_ref[...], sc_ref[...]).astype(bf16)

    return rowcall(cfg, fn, name, [("row", x, D, 0), ("full", g), ("grp", shift), ("grp", scale)],
                   [("row", _sds(x.shape, bf16), D, 0)])[0]


def pre_bwd(cfg, x, g, shift, scale, dh, dx_in, name):
    D = x.shape[1]
    cT = cfg.cT

    def fn(i, x_ref, g_ref, sh_ref, sc_ref, dh_ref, dxin_ref, dx_ref, dg_ref, dsh_ref, dsc_ref):
        _, vjp = jax.vjp(_pre_fn, x_ref[...], g_ref[...], sh_ref[...], sc_ref[...])
        dx, dg, dsh, dsc = vjp(dh_ref[...])
        dx_ref[...] = dxin_ref[...] + dx
        _acc(dg_ref, dg, i == 0)
        first = jnp.logical_or(i == 0, i == cT)
        _acc(dsh_ref, dsh, first)
        _acc(dsc_ref, dsc, first)

    return rowcall(cfg, fn, name,
                   [("row", x, D, 0), ("full", g), ("grp", shift), ("grp", scale), ("row", dh, D, 0), ("row", dx_in, D, 0)],
                   [("row", _sds(x.shape, f32), D, 0), ("full", _sds((1, D), f32)),
                    ("grp", _sds((2, 1, D), f32)), ("grp", _sds((2, 1, D), f32))])


def _post_fn(w, y, g, gate):
    return (w * gate) * (_rms(y) * g)


def post_fwd(cfg, x, y, g, gate, w, name):
    D = x.shape[1]

    def fn(i, x_ref, y_ref, g_ref, gt_ref, o_ref):
        o_ref[...] = x_ref[...] + _post_fn(w, y_ref[...], g_ref[...], gt_ref[...])

    return rowcall(cfg, fn, name, [("row", x, D, 0), ("row", y, D, 0), ("full", g), ("grp", gate)],
                   [("row", _sds(x.shape, f32), D, 0)])[0]


def post_bwd(cfg, dx, y, g, gate, w, name):
    D = dx.shape[1]
    cT = cfg.cT

    def fn(i, dx_ref, y_ref, g_ref, gt_ref, dy_ref, dg_ref, dgt_ref):
        _, vjp = jax.vjp(functools.partial(_post_fn, w), y_ref[...], g_ref[...], gt_ref[...])
        dy, dg, dgt = vjp(dx_ref[...])
        dy_ref[...] = dy.astype(bf16)
        _acc(dg_ref, dg, i == 0)
        _acc(dgt_ref, dgt, jnp.logical_or(i == 0, i == cT))

    return rowcall(cfg, fn, name, [("row", dx, D, 0), ("row", y, D, 0), ("full", g), ("grp", gate)],
                   [("row", _sds(dx.shape, bf16), D, 0), ("full", _sds((1, D), f32)), ("grp", _sds((2, 1, D), f32))])


def _swiglu_fn(a, b):
    return jax.nn.silu(a) * b


def swiglu_fwd(cfg, a, b, name):
    F = a.shape[1]
    tf = _pick(F, (1408, 1024, 512, 256, 128))
    TR = cfg.TR

    def body(a_ref, b_ref, u_ref):
        u_ref[...] = _swiglu_fn(a_ref[...], b_ref[...]).astype(bf16)

    spec = pl.BlockSpec((TR, tf), lambda i, j: (i, j))
    return pl.pallas_call(body, name=name, grid=(cfg.nT, F // tf), in_specs=[spec, spec], out_specs=spec,
                          out_shape=_sds(a.shape, bf16), compiler_params=_params(("arbitrary", "arbitrary")))(a, b)


def swiglu_bwd(cfg, a, b, du, name):
    F = a.shape[1]
    tf = _pick(F, (1408, 1024, 512, 256, 128))
    TR = cfg.TR

    def body(a_ref, b_ref, du_ref, da_ref, db_ref):
        _, vjp = jax.vjp(_swiglu_fn, a_ref[...], b_ref[...])
        da, db = vjp(du_ref[...])
        da_ref[...] = da.astype(bf16)
        db_ref[...] = db.astype(bf16)

    spec = pl.BlockSpec((TR, tf), lambda i, j: (i, j))
    return pl.pallas_call(body, name=name, grid=(cfg.nT, F // tf), in_specs=[spec, spec, spec], out_specs=[spec, spec],
                          out_shape=[_sds(a.shape, bf16), _sds(a.shape, bf16)],
                          compiler_params=_params(("arbitrary", "arbitrary")))(a, b, du)


def _hosted(host, role, got, fn):
    comm = host.get(role) if host else None
    if comm is None:
        return fn(None)
    out, res = fn(comm)
    got[role] = res
    return out


def _grid_ends(grid):
    ids = [pl.program_id(n) for n in range(len(grid))]
    first = functools.reduce(jnp.logical_and, [i == 0 for i in ids])
    last = functools.reduce(jnp.logical_and, [i == n - 1 for i, n in zip(ids, grid)])
    return first, last


def _comm_plumbing(comm):
    if comm is None:
        return [], [], [], [], []
    n = len(comm.srcs)
    hbm = pl.BlockSpec(memory_space=pl.ANY)
    return [hbm] * n, [src for src, _ in comm.srcs], [hbm] * n, comm.out_shapes(), _comm_sems(n)


FFN_ROWS = 384


def ffn_gateup(cfg, h, wg, wu, name, comm=None):
    M, K = h.shape
    G, _, F = wg.shape
    tm = _pick(M, (FFN_ROWS, 256, 128))
    grid = (G, M // tm)
    ncomm = len(comm.srcs) if comm is not None else 0

    def body(h_ref, wg_ref, wu_ref, *rest):
        cin, rest = rest[:ncomm], rest[ncomm:]
        a_ref, b_ref, u_ref = rest[:3]
        cout, sems = rest[3:3 + ncomm], rest[3 + ncomm:]
        first, last = _grid_ends(grid)
        if ncomm:
            @pl.when(first)
            def _():
                _comm_start(comm, cin, cout, *sems)

        hv = h_ref[...]
        a = jnp.dot(hv, wg_ref[...], preferred_element_type=f32)
        b = jnp.dot(hv, wu_ref[...], preferred_element_type=f32)
        a_ref[...] = a.astype(bf16)
        b_ref[...] = b.astype(bf16)
        u_ref[...] = _swiglu_fn(a, b).astype(bf16)
        if ncomm:
            @pl.when(last)
            def _():
                _comm_wait(comm, cin, cout, *sems)

    ci, ca, co, cs, csem = _comm_plumbing(comm)
    w_spec = pl.BlockSpec((None, K, F), lambda g, i: (g, 0, 0))
    o_spec = pl.BlockSpec((tm, F), lambda g, i: (i, g))
    res = pl.pallas_call(
        body, name=name, grid=grid, in_specs=[pl.BlockSpec((tm, K), lambda g, i: (i, 0)), w_spec, w_spec] + ci,
        out_specs=[o_spec] * 3 + co, out_shape=[_sds((M, G * F), bf16)] * 3 + cs, scratch_shapes=csem,
        compiler_params=_params(("arbitrary", "arbitrary")))(h, wg, wu, *ca)
    return (res[0], res[1], res[2]), list(res[3:])


def ffn_du_act(cfg, dy, wd, a, b, name, comm=None):
    M, N = dy.shape
    G, F, _ = wd.shape
    tm = _pick(M, (FFN_ROWS, 256, 128))
    grid = (G, M // tm)
    ncomm = len(comm.srcs) if comm is not None else 0

    def body(dy_ref, wd_ref, a_ref, b_ref, *rest):
        cin, rest = rest[:ncomm], rest[ncomm:]
        da_ref, db_ref = rest[:2]
        cout, sems = rest[2:2 + ncomm], rest[2 + ncomm:]
        first, last = _grid_ends(grid)
        if ncomm:
            @pl.when(first)
            def _():
                _comm_start(comm, cin, cout, *sems)

        du = _dotf(dy_ref[...], wd_ref[...], 1, 1)
        _, vjp = jax.vjp(_swiglu_fn, a_ref[...].astype(f32), b_ref[...].astype(f32))
        da, db = vjp(du)
        da_ref[...] = da.astype(bf16)
        db_ref[...] = db.astype(bf16)
        if ncomm:
            @pl.when(last)
            def _():
                _comm_wait(comm, cin, cout, *sems)

    ci, ca, co, cs, csem = _comm_plumbing(comm)
    t_spec = pl.BlockSpec((tm, F), lambda g, i: (i, g))
    res = pl.pallas_call(
        body, name=name, grid=grid,
        in_specs=[pl.BlockSpec((tm, N), lambda g, i: (i, 0)), pl.BlockSpec((None, F, N), lambda g, i: (g, 0, 0)), t_spec,
                  t_spec] + ci,
        out_specs=[t_spec, t_spec] + co, out_shape=[_sds((M, G * F), bf16)] * 2 + cs, scratch_shapes=csem,
        compiler_params=_params(("arbitrary", "arbitrary")))(dy, wd, a, b, *ca)
    return (res[0], res[1]), list(res[2:])


def ffn_fwd(cfg, x, p, tag, host=None):
    got = {}
    host = host or {}
    h = pre_fwd(cfg, x, p["g_pre"], p["shift"], p["scale"], tag + "_pre")
    (a, b, u), res = ffn_gateup(cfg, h, p["wg"], p["wu"], tag + "_gateup", comm=host.get("gateup"))
    if res:
        got["gateup"] = res
    y = _hosted(host, "down", got, lambda cm: matmul("v2", u, p["wd"], comm=cm, name=tag + "_down"))
    xo = post_fwd(cfg, x, y, p["g_post"], p["gate"], FFN_STEP, tag + "_post")
    return xo, (x, h, a, b, u, y), got


def ffn_bwd(cfg, dX, saved, p, tag):
    x, h, a, b, u, y = saved
    dy, dg_post, dgate = post_bwd(cfg, dX, y, p["g_post"], p["gate"], FFN_STEP, tag + "_postb")
    (da, db), _ = ffn_du_act(cfg, dy, p["wd"], a, b, tag + "_duact")
    gwd = matmul("v6", u, dy, gshape=p["wd"].shape, out_dtype=bf16, name=tag + "_gwd")
    gwg = matmul("v5", h, da, gshape=p["wg"].shape, out_dtype=bf16, name=tag + "_gwg")
    gwu, r_wd = matmul("v5", h, db, gshape=p["wu"].shape, out_dtype=bf16, comm=Comm("scatter", [(gwd, ())]),
                       name=tag + "_gwu")
    dh, r_wg = matmul("v3", da, p["wg"], comm=Comm("scatter", [(gwg, ())]), name=tag + "_dh1")
    dh, r_wu = matmul("v3", db, p["wu"], init=dh, comm=Comm("scatter", [(gwu, ())]), name=tag + "_dh2")
    dX, dg_pre, dshift, dscale = pre_bwd(cfg, x, p["g_pre"], p["shift"], p["scale"], dh, dX, tag + "_preb")
    small = dict(g_pre=dg_pre, g_post=dg_post, shift=dshift, scale=dscale, gate=dgate)
    return dX, small, dict(wg=r_wg[0], wu=r_wu[0], wd=r_wd[0])


def _seg_flags(cfg, i):
    start = jnp.logical_or(i == 0, i == cfg.cT)
    end = jnp.logical_or(i == cfg.cT - 1, i == cfg.nT - 1)
    return start, end


def _fill_halo(buf, cur, prev, nxt, start, end, TR):
    buf[pl.ds(0, HALO), :] = jnp.where(start, 0.0, prev)
    buf[pl.ds(HALO, TR), :] = cur
    buf[pl.ds(HALO + TR, HALO), :] = jnp.where(end, 0.0, nxt)


def conv_fwd(cfg, z, cw, cb, W, name):
    TR = cfg.TR

    def fn(i, r_ref, rp_ref, rn_ref, cw_ref, cb_ref, u_ref, buf):
        start, end = _seg_flags(cfg, i)
        _fill_halo(buf, r_ref[...], rp_ref[...], rn_ref[...], start, end, TR)
        u = jnp.broadcast_to(cb_ref[...], (TR, W))
        for k in range(CONV_W):
            u = u + buf[pl.ds(HALO + k - CONV_LEFT, TR), :] * cw_ref[pl.ds(k, 1), :]
        u_ref[...] = u

    return rowcall(cfg, fn, name, [("row", z, W, 1), ("prev", z, W, 1), ("next", z, W, 1), ("full", cw), ("full", cb)],
                   [("row", _sds((z.shape[0], W), f32), W, 0)], scratch=[pltpu.VMEM((TR + 2 * HALO, W), f32)])[0]


def conv_bwd(cfg, z, du, cw, W, name):
    TR = cfg.TR

    def fn(i, r_ref, rp_ref, rn_ref, du_ref, dup_ref, dun_ref, cw_ref, dr_ref, dcw_ref, dcb_ref, rbuf, dbuf):
        start, end = _seg_flags(cfg, i)
        _fill_halo(rbuf, r_ref[...], rp_ref[...], rn_ref[...], start, end, TR)
        _fill_halo(dbuf, du_ref[...], dup_ref[...], dun_ref[...], start, end, TR)
        du = du_ref[...]

        @pl.when(i == 0)
        def _():
            dcw_ref[...] = jnp.zeros(dcw_ref.shape, f32)
            dcb_ref[...] = jnp.zeros(dcb_ref.shape, f32)

        dr = jnp.zeros((TR, W), f32)
        for k in range(CONV_W):
            dr = dr + dbuf[pl.ds(HALO - (k - CONV_LEFT), TR), :] * cw_ref[pl.ds(k, 1), :]
            dcw_ref[pl.ds(k, 1), :] += jnp.sum(du * rbuf[pl.ds(HALO + k - CONV_LEFT, TR), :], axis=0, keepdims=True)
        dcb_ref[...] += jnp.sum(du, axis=0, keepdims=True)
        dr_ref[...] = dr

    T = z.shape[0]
    return rowcall(cfg, fn, name,
                   [("row", z, W, 1), ("prev", z, W, 1), ("next", z, W, 1), ("row", du, W, 0), ("prev", du, W, 0),
                    ("next", du, W, 0), ("full", cw)],
                   [("row", _sds((T, W), f32), W, 0), ("full", _sds((CONV_W, W), f32)), ("full", _sds((1, W), f32))],
                   scratch=[pltpu.VMEM((TR + 2 * HALO, W), f32), pltpu.VMEM((TR + 2 * HALO, W), f32)])


def _softplus(x):
    return jnp.maximum(x, 0.0) + jnp.log1p(jnp.exp(-jnp.abs(x)))


def _neg_expm1(x):
    series = -x * (1.0 + x * (0.5 + x * (1.0 / 6.0 + x * (1.0 / 24.0 + x * (1.0 / 120.0 + x * (1.0 / 720.0))))))
    return jnp.where(x > -0.1, series, 1.0 - jnp.exp(x))


def _lru_coef(u, pa, px, lam):
    r = jax.nn.sigmoid(pa)
    i = jax.nn.sigmoid(px)
    log_a = -LRU_C * r * _softplus(-lam)
    a = jnp.exp(log_a)
    b = jnp.sqrt(_neg_expm1(2.0 * log_a)) * (i * u)
    return a, b


def _blockdiag(u_bf, w_ref, d, nblk, blk):
    return jnp.concatenate(
        [jnp.dot(u_bf[:, n * blk:(n + 1) * blk], w_ref[d, n].astype(bf16), preferred_element_type=f32)
         for n in range(nblk)], axis=1)


def lru_coef_fwd(cfg, u, wa, ba, wx, bx, lam, name):
    T, W = u.shape
    nblk, blk = wa.shape[1], wa.shape[2]

    def fn(i, u_ref, wa_ref, ba_ref, wx_ref, bx_ref, lam_ref, a_ref, b_ref):
        uv = u_ref[...]
        u_bf = uv.astype(bf16)
        for d in range(2):
            pa = _blockdiag(u_bf, wa_ref, d, nblk, blk) + ba_ref[d]
            px = _blockdiag(u_bf, wx_ref, d, nblk, blk) + bx_ref[d]
            a, b = _lru_coef(uv, pa, px, lam_ref[d])
            a_ref[d] = a
            b_ref[d] = b

    return rowcall(cfg, fn, name, [("row", u, W, 0), ("full", wa), ("full", ba), ("full", wx), ("full", bx), ("full", lam)],
                   [("drow", _sds((2, T, W), f32), W, 0), ("drow", _sds((2, T, W), f32), W, 0)])


def lru_coef_bwd(cfg, u, da, db, wa, ba, wx, bx, lam, name):
    T, W = u.shape
    nblk, blk = wa.shape[1], wa.shape[2]

    def fn(i, u_ref, da_ref, db_ref, wa_ref, ba_ref, wx_ref, bx_ref, lam_ref,
           du_ref, dwa_ref, dba_ref, dwx_ref, dbx_ref, dlam_ref):
        @pl.when(i == 0)
        def _():
            for r in (dwa_ref, dba_ref, dwx_ref, dbx_ref, dlam_ref):
                r[...] = jnp.zeros(r.shape, f32)

        uv = u_ref[...]
        u_bf = uv.astype(bf16)
        du = jnp.zeros(uv.shape, f32)
        for d in range(2):
            pa = _blockdiag(u_bf, wa_ref, d, nblk, blk) + ba_ref[d]
            px = _blockdiag(u_bf, wx_ref, d, nblk, blk) + bx_ref[d]
            _, vjp = jax.vjp(_lru_coef, uv, pa, px, lam_ref[d])
            du_e, dpa, dpx, dlam = vjp((da_ref[d], db_ref[d]))
            du = du + du_e
            dba_ref[d] += jnp.sum(dpa, axis=0, keepdims=True)
            dbx_ref[d] += jnp.sum(dpx, axis=0, keepdims=True)
            dlam_ref[d] += dlam
            parts = []
            for n in range(nblk):
                sl = slice(n * blk, (n + 1) * blk)
                ga, gx = dpa[:, sl].astype(bf16), dpx[:, sl].astype(bf16)
                ub = u_bf[:, sl]
                dwa_ref[d, n] += lax.dot_general(ub, ga, (((0,), (0,)), ((), ())), preferred_element_type=f32)
                dwx_ref[d, n] += lax.dot_general(ub, gx, (((0,), (0,)), ((), ())), preferred_element_type=f32)
                parts.append(
                    lax.dot_general(ga, wa_ref[d, n].astype(bf16), (((1,), (1,)), ((), ())), preferred_element_type=f32)
                    + lax.dot_general(gx, wx_ref[d, n].astype(bf16), (((1,), (1,)), ((), ())), preferred_element_type=f32))
            du = du + jnp.concatenate(parts, axis=1)
        du_ref[...] = du

    return rowcall(cfg, fn, name,
                   [("row", u, W, 0), ("drow", da, W, 0), ("drow", db, W, 0), ("full", wa), ("full", ba), ("full", wx),
                    ("full", bx), ("full", lam)],
                   [("row", _sds((T, W), f32), W, 0), ("full", _sds(wa.shape, f32)), ("full", _sds(ba.shape, f32)),
                    ("full", _sds(wx.shape, f32)), ("full", _sds(bx.shape, f32)), ("full", _sds(lam.shape, f32))])


def _dir_tile(cfg, d, j):
    rev = jnp.where(j < cfg.cT, cfg.cT - 1 - j, cfg.nT - 1 - (j - cfg.cT))
    return jnp.where(d == 0, j, rev)


def lru_scan(cfg, a, b, name):
    _, T, W = a.shape
    TR, nT = cfg.TR, cfg.nT

    def body(a_ref, b_ref, h_ref, hp_ref, st):
        d, j = pl.program_id(0), pl.program_id(1)

        @pl.when(j == 0)
        def _():
            st[...] = jnp.zeros(st.shape, f32)

        def step(t, h):
            idx = t + d * (TR - 1 - 2 * t)
            hn = a_ref[pl.ds(idx, 1), :] * h + b_ref[pl.ds(idx, 1), :]
            hp_ref[pl.ds(idx, 1), :] = h
            h_ref[pl.ds(idx, 1), :] = hn
            return hn

        st[...] = lax.fori_loop(0, TR, step, st[...])

    spec = pl.BlockSpec((None, TR, W), lambda d, j: (d, _dir_tile(cfg, d, j), 0))
    return pl.pallas_call(body, name=name, grid=(2, nT), in_specs=[spec, spec], out_specs=[spec, spec],
                          out_shape=[_sds(a.shape, f32), _sds(a.shape, f32)], scratch_shapes=[pltpu.VMEM((1, W), f32)],
                          compiler_params=_params(("arbitrary", "arbitrary")))(a, b)


def lru_scan_bwd(cfg, a, hp, dh, name):
    _, T, W = a.shape
    TR, nT = cfg.TR, cfg.nT

    def body(a_ref, hp_ref, dh_ref, da_ref, db_ref, st):
        d, j = pl.program_id(0), pl.program_id(1)

        @pl.when(j == 0)
        def _():
            st[...] = jnp.zeros(st.shape, f32)

        def step(t, c):
            p = TR - 1 - t
            idx = p + d * (TR - 1 - 2 * p)
            g = dh_ref[pl.ds(idx, 1), :] + c
            db_ref[pl.ds(idx, 1), :] = g
            da_ref[pl.ds(idx, 1), :] = g * hp_ref[pl.ds(idx, 1), :]
            return a_ref[pl.ds(idx, 1), :] * g

        st[...] = lax.fori_loop(0, TR, step, st[...])

    spec = pl.BlockSpec((None, TR, W), lambda d, j: (d, _dir_tile(cfg, d, nT - 1 - j), 0))
    dspec = pl.BlockSpec((TR, W), lambda d, j: (_dir_tile(cfg, d, nT - 1 - j), 0))
    return pl.pallas_call(body, name=name, grid=(2, nT), in_specs=[spec, spec, dspec], out_specs=[spec, spec],
                          out_shape=[_sds(a.shape, f32), _sds(a.shape, f32)], scratch_shapes=[pltpu.VMEM((1, W), f32)],
                          compiler_params=_params(("arbitrary", "arbitrary")))(a, hp, dh)


def _lru_out_fn(gl, h0, h1):
    return jax.nn.gelu(gl) * (h0 + h1)


def lru_out_fwd(cfg, z, h, W, name):
    def fn(i, g_ref, h_ref, o_ref):
        o_ref[...] = _lru_out_fn(g_ref[...], h_ref[0], h_ref[1]).astype(bf16)

    return rowcall(cfg, fn, name, [("row", z, W, 0), ("drow", h, W, 0)], [("row", _sds((z.shape[0], W), bf16), W, 0)])[0]


def lru_out_bwd(cfg, z, h, dmix, W, name):
    def fn(i, g_ref, h_ref, d_ref, dg_ref, dh_ref):
        _, vjp = jax.vjp(_lru_out_fn, g_ref[...], h_ref[0], h_ref[1])
        dg, dh0, _ = vjp(d_ref[...])
        dg_ref[...] = dg
        dh_ref[...] = dh0

    T = z.shape[0]
    return rowcall(cfg, fn, name, [("row", z, W, 0), ("drow", h, W, 0), ("row", dmix, W, 0)],
                   [("row", _sds((T, W), f32), W, 0), ("row", _sds((T, W), f32), W, 0)])


def _rot_half(x, cos, sin):
    n = x.shape[1] // 2
    x1, x2 = x[:, :n], x[:, n:]
    return jnp.concatenate([x1 * cos - x2 * sin, x1 * sin + x2 * cos], axis=1)


def _dotf(a, b, ca, cb):
    return lax.dot_general(a, b, (((ca,), (cb,)), ((), ())), preferred_element_type=f32)


@functools.partial(jax.custom_vjp, nondiff_argnums=(2, 3))
def _dotb(a, b, ca, cb):
    return _dotf(a.astype(bf16), b.astype(bf16), ca, cb)


def _dotb_fwd(a, b, ca, cb):
    return _dotb(a, b, ca, cb), (a, b)


def _dotb_bwd(ca, cb, res, ct):
    a, b = res
    a16, b16, ct16 = a.astype(bf16), b.astype(bf16), ct.astype(bf16)
    da = _dotf(ct16, b16, 1, 1 - cb) if ca == 1 else _dotf(b16, ct16, 1 - cb, 1)
    db = _dotf(a16, ct16, 1 - ca, 0) if cb == 0 else _dotf(ct16, a16, 0, 1 - ca)
    return da, db


_dotb.defvjp(_dotb_fwd, _dotb_bwd)


def _ret_chunk(d, q, k, v, s, logit, cos, sin):
    C = q.shape[0]
    lg = -_softplus(-logit)
    qr = _rot_half(q, cos, sin)
    kr = _rot_half(k, cos, sin) * (RET_DK ** -0.5)
    ii = lax.broadcasted_iota(jnp.int32, (C, C), 0)
    jj = lax.broadcasted_iota(jnp.int32, (C, C), 1)
    diff = ((ii - jj) if d == 0 else (jj - ii)).astype(f32)
    intra = jnp.where(diff >= 0, jnp.exp(lg * jnp.maximum(diff, 0.0)), 0.0)
    pos = lax.broadcasted_iota(jnp.int32, (C, 1), 0).astype(f32)
    if d == 0:
        q_dec, k_dec = jnp.exp(lg * (pos + 1.0)), jnp.exp(lg * (C - 1.0 - pos))
    else:
        q_dec, k_dec = jnp.exp(lg * (C - pos)), jnp.exp(lg * pos)
    s_dec = jnp.exp(lg * C)
    scores = _dotb(qr, kr, 1, 1) * intra
    o = _dotb(scores, v, 1, 0) + _dotb(qr * q_dec, s, 1, 0)
    s_new = s * s_dec + _dotb(kr * k_dec, v, 0, 0)
    return o, s_new


def _chunk_cfg(cfg):
    f = cfg.TR // RET_CHUNK
    return RowCfg(RET_CHUNK, cfg.nT * f, cfg.cT * f)


def ret_fwd(cfg, z, logit, cos, sin, H, qcol, name):
    T = z.shape[0]
    cc = _chunk_cfg(cfg)
    C, nC = RET_CHUNK, cc.nT

    def body(q_ref, k_ref, v_ref, lg_ref, cos_ref, sin_ref, o_ref, s_ref, st):
        d, j = pl.program_id(0), pl.program_id(2)

        @pl.when(j == 0)
        def _():
            st[...] = jnp.zeros(st.shape, f32)

        s_ref[...] = st[...]
        for dd in range(2):
            @pl.when(d == dd)
            def _():
                o, s_new = _ret_chunk(dd, q_ref[...], k_ref[...], v_ref[...], st[...], lg_ref[...], cos_ref[...], sin_ref[...])
                o_ref[...] = o
                st[...] = s_new

    tile = lambda d, j: _dir_tile(cc, d, j)
    zq = pl.BlockSpec((C, RET_DK), lambda d, h, j: (tile(d, j), qcol + h))
    zk = pl.BlockSpec((C, RET_DK), lambda d, h, j: (tile(d, j), qcol + H + h))
    zv = pl.BlockSpec((C, RET_DV), lambda d, h, j: (tile(d, j), qcol + 2 * H + h))
    lgs = pl.BlockSpec((None, None, 1, 1), lambda d, h, j: (d, h, 0, 0))
    cs = pl.BlockSpec((C, RET_DK // 2), lambda d, h, j: (tile(d, j), 0))
    o_spec = pl.BlockSpec((None, C, RET_DV), lambda d, h, j: (d, tile(d, j), h))
    s_spec = pl.BlockSpec((None, None, None, RET_DK, RET_DV), lambda d, h, j: (d, h, tile(d, j), 0, 0))
    return pl.pallas_call(
        body, name=name, grid=(2, H, nC), in_specs=[zq, zk, zv, lgs, cs, cs], out_specs=[o_spec, s_spec],
        out_shape=[_sds((2, T, H * RET_DV), f32), _sds((2, H, nC, RET_DK, RET_DV), f32)],
        scratch_shapes=[pltpu.VMEM((RET_DK, RET_DV), f32)],
        compiler_params=_params(("arbitrary", "arbitrary", "arbitrary")))(z, z, z, logit, cos, sin)


def ret_bwd(cfg, z, states, do, logit, cos, sin, H, qcol, name):
    T = z.shape[0]
    cc = _chunk_cfg(cfg)
    C, nC = RET_CHUNK, cc.nT

    def body(q_ref, k_ref, v_ref, s_ref, do_ref, lg_ref, cos_ref, sin_ref, dq_ref, dk_ref, dv_ref, dlg_ref, st):
        d, j = pl.program_id(0), pl.program_id(2)

        @pl.when(j == 0)
        def _():
            st[...] = jnp.zeros(st.shape, f32)
            dlg_ref[...] = jnp.zeros(dlg_ref.shape, f32)

        for dd in range(2):
            @pl.when(d == dd)
            def _():
                fn = lambda q, k, v, s, lg: _ret_chunk(dd, q, k, v, s, lg, cos_ref[...], sin_ref[...])
                _, vjp = jax.vjp(fn, q_ref[...], k_ref[...], v_ref[...], s_ref[...], lg_ref[...])
                dq, dk, dv, ds, dlg = vjp((do_ref[...], st[...]))
                dq_ref[...] = dq
                dk_ref[...] = dk
                dv_ref[...] = dv
                st[...] = ds
                dlg_ref[...] += dlg

    tile = lambda d, j: _dir_tile(cc, d, nC - 1 - j)
    zq = pl.BlockSpec((C, RET_DK), lambda d, h, j: (tile(d, j), qcol + h))
    zk = pl.BlockSpec((C, RET_DK), lambda d, h, j: (tile(d, j), qcol + H + h))
    zv = pl.BlockSpec((C, RET_DV), lambda d, h, j: (tile(d, j), qcol + 2 * H + h))
    s_spec = pl.BlockSpec((None, None, None, RET_DK, RET_DV), lambda d, h, j: (d, h, tile(d, j), 0, 0))
    do_spec = pl.BlockSpec((C, RET_DV), lambda d, h, j: (tile(d, j), h))
    lgs = pl.BlockSpec((None, None, 1, 1), lambda d, h, j: (d, h, 0, 0))
    cs = pl.BlockSpec((C, RET_DK // 2), lambda d, h, j: (tile(d, j), 0))
    g_spec = pl.BlockSpec((None, C, RET_DK), lambda d, h, j: (d, tile(d, j), h))
    gshape = _sds((2, T, H * RET_DK), f32)
    return pl.pallas_call(
        body, name=name, grid=(2, H, nC), in_specs=[zq, zk, zv, s_spec, do_spec, lgs, cs, cs],
        out_specs=[g_spec, g_spec, g_spec, lgs], out_shape=[gshape, gshape, gshape, _sds((2, H, 1, 1), f32)],
        scratch_shapes=[pltpu.VMEM((RET_DK, RET_DV), f32)],
        compiler_params=_params(("arbitrary", "arbitrary", "arbitrary")))(z, z, z, states, do, logit, cos, sin)


def _ret_norm_fn(H, o0, o1, ol, gn):
    o = o0 + o1
    parts = []
    for h in range(H):
        x = o[:, h * RET_DV:(h + 1) * RET_DV]
        mu = jnp.mean(x, axis=-1, keepdims=True)
        var = jnp.mean(jnp.square(x - mu), axis=-1, keepdims=True)
        parts.append((x - mu) * lax.rsqrt(var + EPS))
    return (jnp.concatenate(parts, axis=1) * gn) * jax.nn.silu(ol)


def ret_norm_fwd(cfg, o, z, gn, H, olcol, name):
    RV = H * RET_DV

    def fn(i, o_ref, ol_ref, gn_ref, r_ref):
        r_ref[...] = _ret_norm_fn(H, o_ref[0], o_ref[1], ol_ref[...], gn_ref[...]).astype(bf16)

    return rowcall(cfg, fn, name, [("drow", o, RV, 0), ("row", z, RV, olcol), ("full", gn)],
                   [("row", _sds((z.shape[0], RV), bf16), RV, 0)])[0]


def ret_norm_bwd(cfg, o, z, gn, dmix, H, olcol, dcol, name):
    RV = H * RET_DV
    T = z.shape[0]

    def fn(i, o_ref, ol_ref, gn_ref, d_ref, do_ref, dol_ref, dgn_ref):
        _, vjp = jax.vjp(functools.partial(_ret_norm_fn, H), o_ref[0], o_ref[1], ol_ref[...], gn_ref[...])
        do, _, dol, dgn = vjp(d_ref[...])
        do_ref[...] = do
        dol_ref[...] = dol
        _acc(dgn_ref, dgn, i == 0)

    return rowcall(cfg, fn, name, [("drow", o, RV, 0), ("row", z, RV, olcol), ("full", gn), ("row", dmix, RV, dcol)],
                   [("row", _sds((T, RV), f32), RV, 0), ("row", _sds((T, RV), f32), RV, 0), ("full", _sds((1, RV), f32))])


def _pool_geom(cfg, i, w, L):
    t = (i - cfg.cT) * cfg.TR + lax.broadcasted_iota(jnp.int32, (cfg.TR, 1), 0)
    lo = jnp.clip(t - w // 2, 0, L)
    hi = jnp.clip(t + w // 2, 0, L)
    return (hi - lo).astype(f32)


def _pool_centred(cfg, i, buf, gi, w, L):
    TR, G = cfg.TR, POOL_GROUP
    cols = pl.ds(gi * G, G)
    tot = buf[pl.ds(HALO - w // 2, TR), cols]
    for s in range(-w // 2 + 1, w // 2):
        tot = tot + buf[pl.ds(HALO + s, TR), cols]
    cnt = _pool_geom(cfg, i, w, L)
    return tot / cnt - buf[pl.ds(HALO, TR), cols], cnt


def pool_fwd(cfg, z, pw, ps, name):
    T = z.shape[0]
    TR, cT = cfg.TR, cfg.cT
    P = POOL_GROUP * len(POOL_WINDOWS)
    L = T - cT * TR

    def fn(i, x_ref, xp_ref, xn_ref, pw_ref, ps_ref, o_ref, buf):
        @pl.when(i < cT)
        def _():
            o_ref[...] = jnp.zeros(o_ref.shape, bf16)

        @pl.when(i >= cT)
        def _():
            start, end = _seg_flags(cfg, i)
            _fill_halo(buf, x_ref[...], xp_ref[...], xn_ref[...], start, end, TR)
            outs = []
            for gi, w in enumerate(POOL_WINDOWS):
                m, _ = _pool_centred(cfg, i, buf, gi, w, L)
                outs.append(jnp.dot(m.astype(bf16), pw_ref[gi].astype(bf16), preferred_element_type=f32))
            o_ref[...] = (jnp.concatenate(outs, axis=1) * ps_ref[...]).astype(bf16)

    return rowcall(cfg, fn, name, [("row", z, P, 0), ("prev", z, P, 0), ("next", z, P, 0), ("full", pw), ("full", ps)],
                   [("row", _sds((T, P), bf16), P, 0)], scratch=[pltpu.VMEM((TR + 2 * HALO, P), f32)])[0]


def pool_bwd_a(cfg, z, dmix, pw, ps, name):
    T = z.shape[0]
    TR, cT = cfg.TR, cfg.cT
    G = POOL_GROUP
    P = G * len(POOL_WINDOWS)
    L = T - cT * TR

    def fn(i, x_ref, xp_ref, xn_ref, d_ref, pw_ref, ps_ref, dm_ref, dmn_ref, dpw_ref, dps_ref, buf):
        @pl.when(i == 0)
        def _():
            dpw_ref[...] = jnp.zeros(dpw_ref.shape, f32)
            dps_ref[...] = jnp.zeros(dps_ref.shape, f32)

        @pl.when(i < cT)
        def _():
            dm_ref[...] = jnp.zeros(dm_ref.shape, f32)
            dmn_ref[...] = jnp.zeros(dmn_ref.shape, f32)

        @pl.when(i >= cT)
        def _():
            start, end = _seg_flags(cfg, i)
            _fill_halo(buf, x_ref[...], xp_ref[...], xn_ref[...], start, end, TR)
            dout = d_ref[...]
            dpre = dout * ps_ref[...]
            pres, dms, dmns = [], [], []
            for gi, w in enumerate(POOL_WINDOWS):
                m, cnt = _pool_centred(cfg, i, buf, gi, w, L)
                m_bf = m.astype(bf16)
                w_bf = pw_ref[gi].astype(bf16)
                pres.append(jnp.dot(m_bf, w_bf, preferred_element_type=f32))
                g_bf = dpre[:, gi * G:(gi + 1) * G].astype(bf16)
                dpw_ref[gi] += _dotf(m_bf, g_bf, 0, 0)
                dm = _dotf(g_bf, w_bf, 1, 1)
                dms.append(dm)
                dmns.append(dm / cnt)
            dps_ref[...] += jnp.sum(dout * jnp.concatenate(pres, axis=1), axis=0, keepdims=True)
            dm_ref[...] = jnp.concatenate(dms, axis=1)
            dmn_ref[...] = jnp.concatenate(dmns, axis=1)

    return rowcall(cfg, fn, name,
                   [("row", z, P, 0), ("prev", z, P, 0), ("next", z, P, 0), ("row", dmix, P, 0), ("full", pw), ("full", ps)],
                   [("row", _sds((T, P), f32), P, 0), ("row", _sds((T, P), f32), P, 0), ("full", _sds(pw.shape, f32)),
                    ("full", _sds((1, P), f32))], scratch=[pltpu.VMEM((TR + 2 * HALO, P), f32)])


def pool_bwd_b(cfg, dm, dmn, name):
    T, P = dm.shape
    TR, cT = cfg.TR, cfg.cT
    G = POOL_GROUP

    def fn(i, dm_ref, c_ref, p_ref, n_ref, dx_ref, buf):
        start, end = _seg_flags(cfg, i)
        _fill_halo(buf, c_ref[...], p_ref[...], n_ref[...], start, end, TR)
        outs = []
        for gi, w in enumerate(POOL_WINDOWS):
            cols = pl.ds(gi * G, G)
            tot = buf[pl.ds(HALO + w // 2, TR), cols]
            for s in range(-w // 2 + 1, w // 2):
                tot = tot + buf[pl.ds(HALO + s, TR), cols]
            outs.append(tot)
        dx_ref[...] = jnp.concatenate(outs, axis=1) - dm_ref[...]

    return rowcall(cfg, fn, name, [("row", dm, P, 0), ("row", dmn, P, 0), ("prev", dmn, P, 0), ("next", dmn, P, 0)],
                   [("row", _sds((T, P), f32), P, 0)], scratch=[pltpu.VMEM((TR + 2 * HALO, P), f32)])[0]


def _swap_halves(x):
    return pltpu.roll(x, HEAD_DIM // 2, 1)


def _headnorm(x, g):
    return _rms(x) * g


def att_prep(cfg, z, qg, kg, cosf, sinf, nq, name):
    T = z.shape[0]
    U = z.shape[1] // (nq + 3)
    nh = U // HEAD_DIM

    def fn(i, *refs):
        q_refs = refs[:nq]
        k_ref, v_ref, qg_ref, kg_ref, cos_ref, sin_ref, qn_ref, kn_ref, vb_ref = refs[nq:]
        cosv, sinv = cos_ref[...], sin_ref[...]

        def heads(x, g):
            outs = []
            for h in range(nh):
                y = _headnorm(x[:, h * HEAD_DIM:(h + 1) * HEAD_DIM], g)
                outs.append(y * cosv + _swap_halves(y) * sinv)
            return jnp.concatenate(outs, axis=1)

        qn_ref[...] = jnp.concatenate([heads(r[...], qg_ref[...]) for r in q_refs], axis=1).astype(bf16)
        kn_ref[...] = heads(k_ref[...], kg_ref[...]).astype(bf16)
        vb_ref[...] = v_ref[...].astype(bf16)

    ins = [("row", z, U, 1 + n) for n in range(nq)] + [("row", z, U, nq + 1), ("row", z, U, nq + 2), ("full", qg),
                                                       ("full", kg), ("row", cosf, HEAD_DIM, 0), ("row", sinf, HEAD_DIM, 0)]
    return rowcall(cfg, fn, name, ins, [("row", _sds((T, nq * U), bf16), nq * U, 0), ("row", _sds((T, U), bf16), U, 0),
                                        ("row", _sds((T, U), bf16), U, 0)])


def att_prep_bwd(cfg, z, qg, kg, cosf, sinf, dqn, dkn, dvb, dxpool, nq, name):
    T = z.shape[0]
    U = z.shape[1] // (nq + 3)
    nh = U // HEAD_DIM
    cT = cfg.cT

    def fn(i, *refs):
        q_refs = refs[:nq]
        (k_ref, qg_ref, kg_ref, cos_ref, sin_ref, dqn_ref, dkn_ref, dvb_ref, dxp_ref, dz_ref, dqg_ref, dkg_ref) = refs[nq:]
        cosv, sinv = cos_ref[...], sin_ref[...]

        @pl.when(i == 0)
        def _():
            dqg_ref[...] = jnp.zeros(dqg_ref.shape, f32)
            dkg_ref[...] = jnp.zeros(dkg_ref.shape, f32)

        def heads_bwd(x, g, dy, dg_ref):
            outs = []
            for h in range(nh):
                sl = slice(h * HEAD_DIM, (h + 1) * HEAD_DIM)
                d = dy[:, sl]
                dn = d * cosv + _swap_halves(d * sinv)
                _, vjp = jax.vjp(_headnorm, x[:, sl], g)
                dx, dg = vjp(dn)
                dg_ref[...] += dg
                outs.append(dx)
            return jnp.concatenate(outs, axis=1)

        dk = heads_bwd(k_ref[...], kg_ref[...], dkn_ref[...], dkg_ref)
        tail = [dk.astype(bf16), dvb_ref[...].astype(bf16)]

        @pl.when(i < cT)
        def _():
            zeros = jnp.zeros((cfg.TR, (nq + 1) * U), bf16)
            dz_ref[...] = jnp.concatenate([zeros] + tail, axis=1)

        @pl.when(i >= cT)
        def _():
            dq = [heads_bwd(r[...], qg_ref[...], dqn_ref[:, n * U:(n + 1) * U], dqg_ref) for n, r in enumerate(q_refs)]
            dz_ref[...] = jnp.concatenate([dxp_ref[...].astype(bf16)] + [t.astype(bf16) for t in dq] + tail, axis=1)

    ins = ([("row", z, U, 1 + n) for n in range(nq)] +
           [("row", z, U, nq + 1), ("full", qg), ("full", kg), ("row", cosf, HEAD_DIM, 0), ("row", sinf, HEAD_DIM, 0),
            ("row", dqn, nq * U, 0), ("row", dkn, U, 0), ("row", dvb, U, 0), ("row", dxpool, U, 0)])
    W = (nq + 3) * U
    return rowcall(cfg, fn, name, ins, [("row", _sds((T, W), bf16), W, 0), ("full", _sds((1, HEAD_DIM), f32)),
                                        ("full", _sds((1, HEAD_DIM), f32))])


def _stack_heads(x, n):
    return jnp.concatenate([x[:, h * HEAD_DIM:(h + 1) * HEAD_DIM] for h in range(n)], axis=0)


def _unstack_heads(x, n):
    rows = x.shape[0] // n
    return jnp.concatenate([x[h * rows:(h + 1) * rows] for h in range(n)], axis=1)


def _att_tiles(cfg, T):
    tq = cfg.TR
    tk = _pick(T, (768, 512, 256, 128))
    return tq, tk, (T - cfg.cT * cfg.TR) // tq, T // tk


LOG2E = 1.4426950408889634


def att_fwd(cfg, qn, kn, vb, nq, name):
    T, U = kn.shape
    KV = U // HEAD_DIM
    tq, tk, nQ, nK = _att_tiles(cfg, T)
    scale = HEAD_DIM ** -0.5
    c2 = scale * LOG2E
    R = nq * tq

    def body(q_ref, k_ref, v_ref, o_ref, lse_ref, m_sc, l_sc, acc):
        ik = pl.program_id(2)

        @pl.when(ik == 0)
        def _():
            m_sc[...] = jnp.full(m_sc.shape, -jnp.inf, f32)
            l_sc[...] = jnp.zeros(l_sc.shape, f32)
            acc[...] = jnp.zeros(acc.shape, f32)

        k, v = k_ref[...], v_ref[...]
        for h in range(nq):
            rows = pl.ds(h * tq, tq)
            s = _dotf(q_ref[:, h * HEAD_DIM:(h + 1) * HEAD_DIM], k, 1, 1)
            m_old = m_sc[rows, :]
            m_new = jnp.maximum(m_old, jnp.max(s, axis=-1, keepdims=True))
            alpha = jnp.exp2((m_old - m_new) * c2)
            p = jnp.exp2((s - m_new) * c2)
            l_sc[rows, :] = alpha * l_sc[rows, :] + jnp.sum(p, axis=-1, keepdims=True)
            acc[rows, :] = alpha * acc[rows, :] + jnp.dot(p.astype(bf16), v, preferred_element_type=f32)
            m_sc[rows, :] = m_new

        @pl.when(ik == nK - 1)
        def _():
            o_ref[...] = _unstack_heads(acc[...] / l_sc[...], nq)
            lse_ref[...] = m_sc[...] * scale + jnp.log(l_sc[...])

    W = nq * HEAD_DIM
    q_spec = pl.BlockSpec((tq, W), lambda h, i, k: (i + cfg.cT, h))
    kv_spec = pl.BlockSpec((tk, HEAD_DIM), lambda h, i, k: (k, h))
    lse_spec = pl.BlockSpec((None, None, R, 1), lambda h, i, k: (h, i, 0, 0))
    return pl.pallas_call(
        body, name=name, grid=(KV, nQ, nK), in_specs=[q_spec, kv_spec, kv_spec], out_specs=[q_spec, lse_spec],
        out_shape=[_sds((T, nq * U), f32), _sds((KV, nQ, R, 1), f32)],
        scratch_shapes=[pltpu.VMEM((R, 1), f32), pltpu.VMEM((R, 1), f32), pltpu.VMEM((R, HEAD_DIM), f32)],
        compiler_params=_params(("arbitrary", "arbitrary", "arbitrary")))(qn, kn, vb)


def att_bwd_dq(cfg, qn, kn, vb, o, lse, do, nq, name):
    T, U = kn.shape
    KV = U // HEAD_DIM
    tq, tk, nQ, nK = _att_tiles(cfg, T)
    scale = HEAD_DIM ** -0.5
    c2 = scale * LOG2E
    R = nq * tq
    W = nq * HEAD_DIM

    def body(q_ref, k_ref, v_ref, o_ref, lse_ref, do_ref, dq_ref, acc, dl):
        ik = pl.program_id(2)

        @pl.when(ik == 0)
        def _():
            acc[...] = jnp.zeros(acc.shape, f32)
            dl[...] = jnp.sum(_stack_heads(do_ref[...] * o_ref[...], nq), axis=-1, keepdims=True)

        k, v = k_ref[...], v_ref[...]
        for h in range(nq):
            rows = pl.ds(h * tq, tq)
            cols = slice(h * HEAD_DIM, (h + 1) * HEAD_DIM)
            s = _dotf(q_ref[:, cols], k, 1, 1)
            p = jnp.exp2(s * c2 - lse_ref[rows, :] * LOG2E)
            dp = _dotf(do_ref[:, cols].astype(bf16), v, 1, 1)
            ds = (p * (dp - dl[rows, :]) * scale).astype(bf16)
            acc[rows, :] += jnp.dot(ds, k, preferred_element_type=f32)

        @pl.when(ik == nK - 1)
        def _():
            dq_ref[...] = _unstack_heads(acc[...], nq)

    q_spec = pl.BlockSpec((tq, W), lambda h, i, k: (i + cfg.cT, h))
    kv_spec = pl.BlockSpec((tk, HEAD_DIM), lambda h, i, k: (k, h))
    lse_spec = pl.BlockSpec((None, None, R, 1), lambda h, i, k: (h, i, 0, 0))
    return pl.pallas_call(
        body, name=name, grid=(KV, nQ, nK), in_specs=[q_spec, kv_spec, kv_spec, q_spec, lse_spec, q_spec], out_specs=q_spec,
        out_shape=_sds((T, nq * U), f32), scratch_shapes=[pltpu.VMEM((R, HEAD_DIM), f32), pltpu.VMEM((R, 1), f32)],
        compiler_params=_params(("arbitrary", "arbitrary", "arbitrary")))(qn, kn, vb, o, lse, do)


def att_bwd_dkv(cfg, qn, kn, vb, o, lse, do, nq, name):
    T, U = kn.shape
    KV = U // HEAD_DIM
    tq, tk, nQ, nK = _att_tiles(cfg, T)
    scale = HEAD_DIM ** -0.5
    c2 = scale * LOG2E
    R = nq * tq
    W = nq * HEAD_DIM

    def body(q_ref, k_ref, v_ref, o_ref, lse_ref, do_ref, dk_ref, dv_ref, dk_acc, dv_acc):
        iq = pl.program_id(2)

        @pl.when(iq == 0)
        def _():
            dk_acc[...] = jnp.zeros(dk_acc.shape, f32)
            dv_acc[...] = jnp.zeros(dv_acc.shape, f32)

        k, v = k_ref[...], v_ref[...]
        for h in range(nq):
            rows = pl.ds(h * tq, tq)
            cols = slice(h * HEAD_DIM, (h + 1) * HEAD_DIM)
            qh = q_ref[:, cols]
            doh = do_ref[:, cols]
            dl = jnp.sum(doh * o_ref[:, cols], axis=-1, keepdims=True)
            p = jnp.exp2(_dotf(qh, k, 1, 1) * c2 - lse_ref[rows, :] * LOG2E)
            do_bf = doh.astype(bf16)
            dv_acc[...] += _dotf(p.astype(bf16), do_bf, 0, 0)
            dp = _dotf(do_bf, v, 1, 1)
            ds = (p * (dp - dl) * scale).astype(bf16)
            dk_acc[...] += _dotf(ds, qh, 0, 0)

        @pl.when(iq == nQ - 1)
        def _():
            dk_ref[...] = dk_acc[...]
            dv_ref[...] = dv_acc[...]

    q_spec = pl.BlockSpec((tq, W), lambda h, k, i: (i + cfg.cT, h))
    kv_spec = pl.BlockSpec((tk, HEAD_DIM), lambda h, k, i: (k, h))
    lse_spec = pl.BlockSpec((None, None, R, 1), lambda h, k, i: (h, i, 0, 0))
    return pl.pallas_call(
        body, name=name, grid=(KV, nK, nQ), in_specs=[q_spec, kv_spec, kv_spec, q_spec, lse_spec, q_spec],
        out_specs=[kv_spec, kv_spec], out_shape=[_sds((T, U), f32), _sds((T, U), f32)],
        scratch_shapes=[pltpu.VMEM((tk, HEAD_DIM), f32), pltpu.VMEM((tk, HEAD_DIM), f32)],
        compiler_params=_params(("arbitrary", "arbitrary", "arbitrary")))(qn, kn, vb, o, lse, do)


def od_mix(cfg, pooled, o, name):
    T, P = pooled.shape
    QW = o.shape[1]
    cT = cfg.cT

    def fn(i, p_ref, o_ref, m_ref):
        @pl.when(i < cT)
        def _():
            m_ref[...] = jnp.zeros(m_ref.shape, bf16)

        @pl.when(i >= cT)
        def _():
            m_ref[...] = jnp.concatenate([p_ref[...], o_ref[...].astype(bf16)], axis=1)

    return rowcall(cfg, fn, name, [("row", pooled, P, 0), ("row", o, QW, 0)], [("row", _sds((T, P + QW), bf16), P + QW, 0)])[0]


def ev_mix(cfg, lru, ret, name):
    T, W = lru.shape
    RV = ret.shape[1]

    def fn(i, a_ref, b_ref, m_ref):
        m_ref[...] = jnp.concatenate([a_ref[...], b_ref[...]], axis=1)

    return rowcall(cfg, fn, name, [("row", lru, W, 0), ("row", ret, RV, 0)], [("row", _sds((T, W + RV), bf16), W + RV, 0)])[0]


def ev_dz_pack(cfg, dgl, dr, dq, dk, dv, dol, name):
    T, W = dgl.shape
    RV = dol.shape[1]
    width = 2 * W + 4 * RV

    def fn(i, g_ref, r_ref, q_ref, k_ref, v_ref, o_ref, dz_ref):
        parts = [g_ref[...], r_ref[...], q_ref[0] + q_ref[1], k_ref[0] + k_ref[1], v_ref[0] + v_ref[1], o_ref[...]]
        dz_ref[...] = jnp.concatenate([p.astype(bf16) for p in parts], axis=1)

    return rowcall(cfg, fn, name, [("row", dgl, W, 0), ("row", dr, W, 0), ("drow", dq, RV, 0), ("drow", dk, RV, 0),
                                   ("drow", dv, RV, 0), ("row", dol, RV, 0)], [("row", _sds((T, width), bf16), width, 0)])[0]


def loss_fwd_bwd(cfg, xf, target, name):
    T, D = xf.shape
    TR, cT = cfg.TR, cfg.cT

    def body(x_ref, t_ref, sq_ref, dx_ref):
        i = pl.program_id(0)

        @pl.when(i == 0)
        def _():
            sq_ref[...] = jnp.zeros(sq_ref.shape, f32)

        @pl.when(i < cT)
        def _():
            dx_ref[...] = jnp.zeros(dx_ref.shape, f32)

        @pl.when(i >= cT)
        def _():
            diff = x_ref[...] - t_ref[...]
            sq_ref[...] += jnp.sum(diff * diff, axis=0, keepdims=True)
            dx_ref[...] = diff / D

    row = pl.BlockSpec((TR, D), lambda i: (i, 0))
    trow = pl.BlockSpec((TR, D), lambda i: (jnp.maximum(i - cT, 0), 0))
    return pl.pallas_call(body, name=name, grid=(cfg.nT,), in_specs=[row, trow],
                          out_specs=[pl.BlockSpec((1, D), lambda i: (0, 0)), row],
                          out_shape=[_sds((1, D), f32), _sds((T, D), f32)], compiler_params=_params(("arbitrary",)))(xf, target)


MOD_ROWS = 16


def mod_fwd(s16, mod_w, name):
    nL, D, C4 = mod_w.shape
    tc = _pick(C4, (512, 256, 128))

    def body(s_ref, w_ref, o_ref):
        o_ref[...] = jnp.dot(s_ref[...], w_ref[...], precision=lax.Precision.HIGHEST, preferred_element_type=f32)

    return pl.pallas_call(
        body, name=name, grid=(nL, C4 // tc),
        in_specs=[pl.BlockSpec((MOD_ROWS, D), lambda l, j: (0, 0)), pl.BlockSpec((None, D, tc), lambda l, j: (l, 0, j))],
        out_specs=pl.BlockSpec((None, MOD_ROWS, tc), lambda l, j: (l, 0, j)), out_shape=_sds((nL, MOD_ROWS, C4), f32),
        compiler_params=_params(("arbitrary", "arbitrary")))(s16, mod_w)


def mod_bwd(s16, dm16, mod_w, name):
    nL, D, C4 = mod_w.shape
    tc = _pick(C4, (512, 256, 128))
    half = MOD_ROWS // 2

    def body(s_ref, d_ref, w_ref, g_ref, dc_ref):
        first = jnp.logical_and(pl.program_id(0) == 0, pl.program_id(1) == 0)
        g_ref[...] = lax.dot_general(s_ref[...], d_ref[...], (((0,), (0,)), ((), ())), precision=lax.Precision.HIGHEST,
                                     preferred_element_type=f32)
        part = lax.dot_general(d_ref[...], w_ref[...], (((1,), (1,)), ((), ())), precision=lax.Precision.HIGHEST,
                               preferred_element_type=f32)
        _acc(dc_ref, jnp.sum(part[half:], axis=0, keepdims=True), first)

    return pl.pallas_call(
        body, name=name, grid=(nL, C4 // tc),
        in_specs=[pl.BlockSpec((MOD_ROWS, D), lambda l, j: (0, 0)), pl.BlockSpec((None, MOD_ROWS, tc), lambda l, j: (l, 0, j)),
                  pl.BlockSpec((None, D, tc), lambda l, j: (l, 0, j))],
        out_specs=[pl.BlockSpec((None, D, tc), lambda l, j: (l, 0, j)), pl.BlockSpec((1, D), lambda l, j: (0, 0))],
        out_shape=[_sds((nL, D, C4), f32), _sds((1, D), f32)],
        compiler_params=_params(("arbitrary", "arbitrary")))(s16, dm16, mod_w)


def _as2d(a):
    return a.reshape(-1, a.shape[-1])


ELEMENTWISE_VMEM = 24 * 1024 * 1024


def _tiles2d(shape, n_arrays):
    R, C = shape
    tc = _pick(C, (1536, 1408, 1024, 768, 512, 256, 128))
    fits = [t for t in (512, 256, 128, 64, 32, 16, 8) if R % t == 0 and t * tc * 4 * 2 * n_arrays <= ELEMENTWISE_VMEM]
    return (fits[0] if fits else R), tc


def cast_bf16(a, name):
    a2 = _as2d(a)
    tr, tc = _tiles2d(a2.shape, 2)

    def body(a_ref, o_ref):
        o_ref[...] = a_ref[...].astype(bf16)

    spec = pl.BlockSpec((tr, tc), lambda i, j: (i, j))
    out = pl.pallas_call(body, name=name, grid=(a2.shape[0] // tr, a2.shape[1] // tc), in_specs=[spec], out_specs=spec,
                         out_shape=_sds(a2.shape, bf16), compiler_params=_params(("arbitrary", "arbitrary")))(a2)
    return out.reshape(a.shape)


def sum_leading(a, name, *, into=None, full_shape=None, widx=()):
    n = a.shape[0]
    a3 = a.reshape(n, -1, a.shape[-1])
    tr, tc = _tiles2d(a3.shape[1:], n + 1)

    def body(a_ref, *rest):
        o_ref = rest[-1]
        tot = a_ref[0].astype(f32)
        for k in range(1, n):
            tot = tot + a_ref[k].astype(f32)
        o_ref[...] = tot

    grid = (a3.shape[1] // tr, a3.shape[2] // tc)
    in_specs = [pl.BlockSpec((n, tr, tc), lambda i, j: (0, i, j))]
    args = [a3]
    if not widx:
        out = pl.pallas_call(body, name=name, grid=grid, in_specs=in_specs,
                             out_specs=pl.BlockSpec((tr, tc), lambda i, j: (i, j)), out_shape=_sds(a3.shape[1:], f32),
                             compiler_params=_params(("arbitrary", "arbitrary")))(*args)
        return out.reshape(a.shape[1:])
    lead = tuple(full_shape[:len(widx)])
    flat = lead + tuple(a3.shape[1:])
    aliases = {}
    if into is not None:
        in_specs.append(pl.BlockSpec(memory_space=pl.ANY))
        args.append(into.reshape(flat))
        aliases = {1: 0}
    out = pl.pallas_call(body, name=name, grid=grid, in_specs=in_specs,
                         out_specs=pl.BlockSpec((None,) * len(widx) + (tr, tc), lambda i, j: tuple(widx) + (i, j)),
                         out_shape=_sds(flat, f32), input_output_aliases=aliases,
                         compiler_params=_params(("arbitrary", "arbitrary")))(*args)
    return out.reshape(full_shape)


def adamw(w, m, v, g_parts, name):
    w2, m2, v2 = _as2d(w), _as2d(m), _as2d(v)
    parts = [_as2d(p) for p in g_parts]
    tr, tc = _tiles2d(w2.shape, 7 + len(parts))
    npart = len(parts)

    def body(*refs):
        w_ref, m_ref, v_ref = refs[:3]
        p_refs = refs[3:3 + npart]
        g_ref, d_ref, nm_ref, nv_ref = refs[3 + npart:]
        g = p_refs[0][...]
        for p in p_refs[1:]:
            g = g + p[...]
        mn = ADAM_B1 * m_ref[...] + (1.0 - ADAM_B1) * g
        vn = ADAM_B2 * v_ref[...] + (1.0 - ADAM_B2) * jnp.square(g)
        m_hat = mn / (1.0 - ADAM_B1 ** ADAM_STEP)
        v_hat = vn / (1.0 - ADAM_B2 ** ADAM_STEP)
        g_ref[...] = g
        d_ref[...] = -ADAM_LR * (m_hat / (jnp.sqrt(v_hat) + ADAM_EPS) + ADAM_WD * w_ref[...])
        nm_ref[...] = mn
        nv_ref[...] = vn

    spec = pl.BlockSpec((tr, tc), lambda i, j: (i, j))
    outs = pl.pallas_call(body, name=name, grid=(w2.shape[0] // tr, w2.shape[1] // tc), in_specs=[spec] * (3 + npart),
                          out_specs=[spec] * 4, out_shape=[_sds(w2.shape, f32)] * 4,
                          compiler_params=_params(("arbitrary", "arbitrary")))(w2, m2, v2, *parts)
    return [o.reshape(w.shape) for o in outs]


def all_gather_small(a, name):
    R, C = a.shape

    def body(a_ref, out_ref, send_sems, recv_sems, local_sem):
        x, y, c = _coords()
        me = 4 * x + 2 * y + c
        mine = pltpu.make_async_copy(a_ref, out_ref.at[me], local_sem)
        mine.start()
        copies = []
        for k in range(1, N_DEV):
            kx, ky, kc = (k >> 2) & 1, (k >> 1) & 1, k & 1
            peer = (_flip(x, kx), _flip(y, ky), _flip(c, kc))
            cp = pltpu.make_async_remote_copy(src_ref=a_ref, dst_ref=out_ref.at[me], send_sem=send_sems.at[k - 1],
                                              recv_sem=recv_sems.at[k - 1], device_id=peer, device_id_type=MESH)
            cp.start()
            copies.append((cp, 4 * peer[0] + 2 * peer[1] + peer[2], peer))
        for k, (cp, pidx, peer) in enumerate(copies):
            pltpu.make_async_remote_copy(src_ref=a_ref, dst_ref=out_ref.at[pidx], send_sem=send_sems.at[k],
                                         recv_sem=recv_sems.at[k], device_id=peer, device_id_type=MESH).wait_recv()
        for cp, _, _ in copies:
            cp.wait_send()
        mine.wait()

    return pl.pallas_call(
        body, name=name, out_shape=_sds((N_DEV, R, C), f32),
        in_specs=[pl.BlockSpec(memory_space=pltpu.VMEM)], out_specs=pl.BlockSpec(memory_space=pltpu.VMEM),
        scratch_shapes=[pltpu.SemaphoreType.DMA((N_DEV - 1,)), pltpu.SemaphoreType.DMA((N_DEV - 1,)), pltpu.SemaphoreType.DMA],
        compiler_params=pltpu.CompilerParams(vmem_limit_bytes=VMEM_LIMIT))(a)


def swap_with_sibling(parts, name):
    n = len(parts)

    def body(*refs):
        in_refs, out_refs = refs[:n], refs[n:2 * n]
        send_sems, recv_sems = refs[2 * n:]
        x, y, c = _coords()
        sends = []
        for w in range(n):
            cp = pltpu.make_async_remote_copy(src_ref=in_refs[w], dst_ref=out_refs[w], send_sem=send_sems.at[w],
                                              recv_sem=recv_sems.at[w], device_id=(x, y, 1 - c), device_id_type=MESH)
            cp.start()
            sends.append(cp)
        for cp in sends:
            cp.wait_recv()
        for cp in sends:
            cp.wait_send()

    hbm = pl.BlockSpec(memory_space=pl.ANY)
    return pl.pallas_call(
        body, name=name, out_shape=[_sds(a.shape, a.dtype) for a in parts], in_specs=[hbm] * n, out_specs=[hbm] * n,
        scratch_shapes=[pltpu.SemaphoreType.DMA((n,)), pltpu.SemaphoreType.DMA((n,))],
        )(*parts)


def even_fwd(cfg, x, p, tag, host=None):
    W, H = p["W"], p["H"]
    got = {}
    h = pre_fwd(cfg, x, p["g_pre"], p["shift"], p["scale"], tag + "_pre")
    z = _hosted(host, "in", got, lambda cm: matmul("v1", h, p["w_in"], comm=cm, name=tag + "_in"))
    u = conv_fwd(cfg, z, p["conv_w"], p["conv_b"], W, tag + "_conv")
    a, b = lru_coef_fwd(cfg, u, p["wa"], p["ba"], p["wx"], p["bx"], p["lam"], tag + "_coef")
    hh, hp = lru_scan(cfg, a, b, tag + "_scan")
    lru = lru_out_fwd(cfg, z, hh, W, tag + "_lruout")
    qcol = 2 * W // RET_DK
    o, st = ret_fwd(cfg, z, p["logit"], p["cos1"], p["sin1"], H, qcol, tag + "_ret")
    olcol = (2 * W + 3 * H * RET_DK) // (H * RET_DV)
    ret = ret_norm_fwd(cfg, o, z, p["gn"], H, olcol, tag + "_retnorm")
    mix = ev_mix(cfg, lru, ret, tag + "_mix")
    y = _hosted(host, "out", got, lambda cm: matmul("v2", mix, p["w_out"], comm=cm, name=tag + "_out"))
    xo = post_fwd(cfg, x, y, p["g_post"], p["gate"], 1.0, tag + "_post")
    return xo, (x, h, z, u, a, hh, hp, o, st, mix, y, olcol, qcol), got


def even_bwd(cfg, dX, saved, p, tag):
    x, h, z, u, a, hh, hp, o, st, mix, y, olcol, qcol = saved
    W, H = p["W"], p["H"]
    dy, dg_post, dgate = post_bwd(cfg, dX, y, p["g_post"], p["gate"], 1.0, tag + "_postb")
    dmix = matmul("v4", dy, p["w_out"], name=tag + "_dmix")
    g_out = matmul("v6", mix, dy, gshape=p["w_out"].shape, out_dtype=bf16, name=tag + "_gwout")
    dgl, dhs = lru_out_bwd(cfg, z, hh, dmix, W, tag + "_lruoutb")
    da, db = lru_scan_bwd(cfg, a, hp, dhs, tag + "_scanb")
    du, dwa, dba, dwx, dbx, dlam = lru_coef_bwd(cfg, u, da, db, p["wa"], p["ba"], p["wx"], p["bx"], p["lam"], tag + "_coefb")
    dr, dcw, dcb = conv_bwd(cfg, z, du, p["conv_w"], W, tag + "_convb")
    do, dol, dgn = ret_norm_bwd(cfg, o, z, p["gn"], dmix, H, olcol, W // (H * RET_DV), tag + "_retnormb")
    dq, dk, dv, dlg = ret_bwd(cfg, z, st, do, p["logit"], p["cos1"], p["sin1"], H, qcol, tag + "_retb")
    dz = ev_dz_pack(cfg, dgl, dr, dq, dk, dv, dol, tag + "_dz")
    g_in, r_out = matmul("v5", h, dz, gshape=p["w_in"].shape, out_dtype=bf16, comm=Comm("scatter", [(g_out, ())]),
                         name=tag + "_gwin")
    dh, r_in = matmul("v3", dz, p["w_in"], comm=Comm("scatter", [(g_in, ())]), name=tag + "_dh")
    dX, dg_pre, dshift, dscale = pre_bwd(cfg, x, p["g_pre"], p["shift"], p["scale"], dh, dX, tag + "_preb")
    pg = dict(g_pre=dg_pre, g_post=dg_post, shift=dshift, scale=dscale, gate=dgate, conv_w=dcw, conv_b=dcb, wa=dwa,
              ba=dba, wx=dwx, bx=dbx, lam=dlam, logit=dlg, gn=dgn)
    return dX, pg, dict(w_in=r_in[0], w_out=r_out[0])


def odd_fwd(cfg, x, p, tag, host=None):
    nq = p["nq"]
    got = {}
    h = pre_fwd(cfg, x, p["g_pre"], p["shift"], p["scale"], tag + "_pre")
    z = _hosted(host, "in", got, lambda cm: matmul("v1", h, p["w_in"], comm=cm, name=tag + "_in"))
    pooled = pool_fwd(cfg, z, p["pool_w"], p["pool_scale"], tag + "_pool")
    qn, kn, vb = att_prep(cfg, z, p["qg"], p["kg"], p["cosf"], p["sinf"], nq, tag + "_prep")
    o, lse = att_fwd(cfg, qn, kn, vb, nq, tag + "_att")
    mix = od_mix(cfg, pooled, o, tag + "_mix")
    y = matmul("v2", mix, p["w_out"], name=tag + "_out")
    xo = post_fwd(cfg, x, y, p["g_post"], p["gate"], 1.0, tag + "_post")
    return xo, (x, h, z, qn, kn, vb, o, lse, mix, y), got


def odd_bwd(cfg, dX, saved, p, tag):
    x, h, z, qn, kn, vb, o, lse, mix, y = saved
    nq = p["nq"]
    U = kn.shape[1]
    dy, dg_post, dgate = post_bwd(cfg, dX, y, p["g_post"], p["gate"], 1.0, tag + "_postb")
    dmix = matmul("v4", dy, p["w_out"], name=tag + "_dmix")
    g_out = matmul("v6", mix, dy, gshape=p["w_out"].shape, out_dtype=bf16, name=tag + "_gwout")
    dm, dmn, dpw, dps = pool_bwd_a(cfg, z, dmix, p["pool_w"], p["pool_scale"], tag + "_poolb")
    dxp = pool_bwd_b(cfg, dm, dmn, tag + "_poolb2")
    do = dmix[:, U:]
    dqn = att_bwd_dq(cfg, qn, kn, vb, o, lse, do, nq, tag + "_attdq")
    dkn, dvb = att_bwd_dkv(cfg, qn, kn, vb, o, lse, do, nq, tag + "_attdkv")
    dz, dqg, dkg = att_prep_bwd(cfg, z, p["qg"], p["kg"], p["cosf"], p["sinf"], dqn, dkn, dvb, dxp, nq, tag + "_prepb")
    g_in, r_out = matmul("v5", h, dz, gshape=p["w_in"].shape, out_dtype=bf16, comm=Comm("scatter", [(g_out, ())]),
                         name=tag + "_gwin")
    dh, r_in = matmul("v3", dz, p["w_in"], comm=Comm("scatter", [(g_in, ())]), name=tag + "_dh")
    dX, dg_pre, dshift, dscale = pre_bwd(cfg, x, p["g_pre"], p["shift"], p["scale"], dh, dX, tag + "_preb")
    pg = dict(g_pre=dg_pre, g_post=dg_post, shift=dshift, scale=dscale, gate=dgate, pool_w=dpw, pool_scale=dps, qg=dqg, kg=dkg)
    return dX, pg, dict(w_in=r_in[0], w_out=r_out[0])


WEIGHT_NAMES = ("c_ctx", "mod_w", "mod_b", "norm_pre", "norm_post", "ffn_gate", "ffn_up", "ffn_down", "ev_w_in", "ev_w_out",
                "lru_conv_w", "lru_conv_b", "lru_wa", "lru_ba", "lru_wx", "lru_bx", "lru_lambda", "ret_decay_logit", "ret_gn",
                "od_w_in", "od_w_out", "pool_w", "pool_scale", "q_norm", "k_norm")
BIG = ("ffn_gate", "ffn_up", "ffn_down", "ev_w_in", "ev_w_out", "od_w_in", "od_w_out")
SMALL_SHARDED = ("norm_pre", "norm_post", "lru_conv_w", "lru_ba", "lru_bx", "lru_lambda", "pool_scale")
SMALL_REPL = ("mod_b", "lru_conv_b", "lru_wa", "lru_wx", "ret_decay_logit", "ret_gn", "pool_w", "q_norm", "k_norm")
LANES = 128


PACK_ROWS = 512


def _rows_of(n):
    return -(-n // (8 * LANES)) * 8


def _pack(arrs):
    rows = []
    for a in arrs:
        flat = a.reshape(-1)
        rows.append(jnp.pad(flat, (0, _rows_of(flat.shape[0]) * LANES - flat.shape[0])).reshape(-1, LANES))
    total = sum(r.shape[0] for r in rows)
    rows.append(jnp.zeros(((-total) % PACK_ROWS, LANES), f32))
    return jnp.concatenate(rows), None


def _unpack(packed, shapes, lead=()):
    out, pos = [], 0
    for shp in shapes:
        n = math.prod(shp)
        r = _rows_of(n)
        piece = packed[..., pos:pos + r, :].reshape(lead + (r * LANES,))
        out.append(piece[..., :n].reshape(lead + tuple(shp)))
        pos += r
    return out


def _unshard(g):
    return jnp.moveaxis(g, 0, -2).reshape(g.shape[1:-1] + (g.shape[0] * g.shape[-1],))


def _rope_tables(S, Lc):
    n_r = RET_DK // 2
    f_r = RET_THETA ** (-jnp.arange(n_r, dtype=f32) / n_r)
    ang1 = jnp.arange(S, dtype=f32)[:, None] * f_r
    rows = S // GRID_W
    row = jnp.repeat(jnp.arange(rows, dtype=f32), GRID_W)
    col = jnp.tile(jnp.arange(GRID_W, dtype=f32), rows)
    n_ax = HEAD_DIM // 4
    f_ax = ROPE_THETA ** (-jnp.arange(n_ax, dtype=f32) / n_ax)
    ang2 = jnp.concatenate([row[:, None] * f_ax, col[:, None] * f_ax], axis=-1)
    cos2, sin2 = jnp.cos(ang2), jnp.sin(ang2)
    ones = lambda n: jnp.ones((Lc, n), f32)
    zeros = lambda n: jnp.zeros((Lc, n), f32)
    cos1 = jnp.concatenate([ones(n_r), jnp.cos(ang1)])
    sin1 = jnp.concatenate([zeros(n_r), jnp.sin(ang1)])
    cosf = jnp.concatenate([ones(HEAD_DIM), jnp.concatenate([cos2, cos2], axis=1)])
    sinf = jnp.concatenate([zeros(HEAD_DIM), jnp.concatenate([-sin2, sin2], axis=1)])
    return cos1, sin1, cosf, sinf


def kernel(x, c, ctx, c_ctx, mod_w, mod_b, norm_pre, norm_post, ffn_gate, ffn_up, ffn_down, ev_w_in, ev_w_out, lru_conv_w, lru_conv_b, lru_wa, lru_ba, lru_wx, lru_bx, lru_lambda, ret_decay_logit, ret_gn, od_w_in, od_w_out, pool_w, pool_scale, q_norm, k_norm, loss_target, m_c_ctx, m_mod_w, m_mod_b, m_norm_pre, m_norm_post, m_ffn_gate, m_ffn_up, m_ffn_down, m_ev_w_in, m_ev_w_out, m_lru_conv_w, m_lru_conv_b, m_lru_wa, m_lru_ba, m_lru_wx, m_lru_bx, m_lru_lambda, m_ret_decay_logit, m_ret_gn, m_od_w_in, m_od_w_out, m_pool_w, m_pool_scale, m_q_norm, m_k_norm, v_c_ctx, v_mod_w, v_mod_b, v_norm_pre, v_norm_post, v_ffn_gate, v_ffn_up, v_ffn_down, v_ev_w_in, v_ev_w_out, v_lru_conv_w, v_lru_conv_b, v_lru_wa, v_lru_ba, v_lru_wx, v_lru_bx, v_lru_lambda, v_ret_decay_logit, v_ret_gn, v_od_w_in, v_od_w_out, v_pool_w, v_pool_scale, v_q_norm, v_k_norm):
    wts = dict(c_ctx=c_ctx, mod_w=mod_w, mod_b=mod_b, norm_pre=norm_pre, norm_post=norm_post, ffn_gate=ffn_gate, ffn_up=ffn_up,
               ffn_down=ffn_down, ev_w_in=ev_w_in, ev_w_out=ev_w_out, lru_conv_w=lru_conv_w, lru_conv_b=lru_conv_b,
               lru_wa=lru_wa, lru_ba=lru_ba, lru_wx=lru_wx, lru_bx=lru_bx, lru_lambda=lru_lambda,
               ret_decay_logit=ret_decay_logit, ret_gn=ret_gn, od_w_in=od_w_in, od_w_out=od_w_out, pool_w=pool_w,
               pool_scale=pool_scale, q_norm=q_norm, k_norm=k_norm)
    mom_m = dict(zip(WEIGHT_NAMES, (m_c_ctx, m_mod_w, m_mod_b, m_norm_pre, m_norm_post, m_ffn_gate, m_ffn_up, m_ffn_down,
                                    m_ev_w_in, m_ev_w_out, m_lru_conv_w, m_lru_conv_b, m_lru_wa, m_lru_ba, m_lru_wx, m_lru_bx,
                                    m_lru_lambda, m_ret_decay_logit, m_ret_gn, m_od_w_in, m_od_w_out, m_pool_w, m_pool_scale,
                                    m_q_norm, m_k_norm)))
    mom_v = dict(zip(WEIGHT_NAMES, (v_c_ctx, v_mod_w, v_mod_b, v_norm_pre, v_norm_post, v_ffn_gate, v_ffn_up, v_ffn_down,
                                    v_ev_w_in, v_ev_w_out, v_lru_conv_w, v_lru_conv_b, v_lru_wa, v_lru_ba, v_lru_wx, v_lru_bx,
                                    v_lru_lambda, v_ret_decay_logit, v_ret_gn, v_od_w_in, v_od_w_out, v_pool_w, v_pool_scale,
                                    v_q_norm, v_k_norm)))

    _, S, D = x.shape
    Lc = ctx.shape[1]
    T = Lc + S
    TR = 256 if (Lc % 256 == 0 and S % 256 == 0) else 128
    assert Lc % TR == 0 and S % TR == 0 and TR % RET_CHUNK == 0
    cfg = RowCfg(TR, T // TR, Lc // TR)
    W = lru_conv_b.shape[-1]
    H = ret_decay_logit.shape[-1]
    U = POOL_GROUP * len(POOL_WINDOWS)
    nq = (N_CHIPS * od_w_in.shape[-1]) // U - 3
    assert W % (H * RET_DV) == 0 and (2 * W) % RET_DK == 0
    nL = mod_w.shape[0]
    C4 = mod_w.shape[-1]
    assert nL == 2, "two layers: an even mixer then an odd one"

    xi, yi, ci = _coords()
    chip = 2 * xi + yi
    me = 4 * xi + 2 * yi + ci

    sc = jax.nn.silu(c)
    small_in, _ = _pack([sc] + [wts[n] for n in SMALL_SHARDED])
    g1 = all_gather_small(small_in, "gather_small_fwd")
    parts = _unpack(g1, [sc.shape] + [wts[n].shape for n in SMALL_SHARDED], lead=(N_DEV,))
    sc_all = parts[0][:, 0]
    full = {n: _unshard(parts[1 + i][0::2]) for i, n in enumerate(SMALL_SHARDED)}
    for n in SMALL_REPL + ("c_ctx",):
        full[n] = wts[n]

    scc = jax.nn.silu(c_ctx)[None]
    pad_rows = MOD_ROWS - N_DEV - 1
    s16 = jnp.concatenate([sc_all, scc, jnp.zeros((pad_rows, D), f32)])
    modp = mod_fwd(s16, mod_w, "mod_fwd")
    g2 = all_gather_small(modp.reshape(-1, LANES), "gather_mod")
    mod_all = g2.reshape(N_DEV, nL, MOD_ROWS, C4)[0::2]
    mod_all = jnp.moveaxis(mod_all, 0, 2).reshape(nL, MOD_ROWS, N_CHIPS * C4) + mod_b[:, None, :]
    mod_l = lax.dynamic_index_in_dim(mod_all, me, axis=1, keepdims=False).reshape(nL, 3, 3, D)
    mod_c = mod_all[:, N_DEV].reshape(nL, 3, 3, D)

    def mod_of(li, s, kind, ctx_live=True):
        cpart = mod_c[li, s, kind] if ctx_live else jnp.zeros((D,), f32)
        return jnp.stack([cpart, mod_l[li, s, kind]])[:, None, :]

    packed = {n: cast_bf16(wts[n], "cast_" + n) for n in BIG}
    ffn_units = [(0, 0), (0, 1), (1, 0), (1, 1)]

    def G(*pieces):
        return Comm("gather", [(packed[n], idx) for n, idx in pieces])

    cos1, sin1, cosf, sinf = _rope_tables(S, Lc)

    def sub_params(li, s, ctx_live=True, gate_ctx_live=True):
        return dict(g_pre=full["norm_pre"][li, s][None], g_post=full["norm_post"][li, s][None],
                    shift=mod_of(li, s, 0, ctx_live), scale=mod_of(li, s, 1, ctx_live),
                    gate=mod_of(li, s, 2, ctx_live and gate_ctx_live))

    X0 = jnp.concatenate([ctx[0], x[0]], axis=0)
    wg00, wu00 = exchange(G(("ffn_gate", (0, 0)), ("ffn_up", (0, 0))), "gather_first")
    p00 = sub_params(0, 0)
    p00.update(wg=wg00, wu=wu00)
    h00 = {"gateup": G(("ffn_down", (0, 0)), ("ev_w_in", (0,))), "down": G(("ev_w_out", (0,)), ("ffn_gate", (0, 1)))}
    h = pre_fwd(cfg, X0, p00["g_pre"], p00["shift"], p00["scale"], "l0f0_pre")
    (a00, b00, u00), (wd00, ev_in) = ffn_gateup(cfg, h, wg00, wu00, "l0f0_gateup", comm=h00["gateup"])
    p00.update(wd=wd00)
    y00, (ev_out, wg01) = matmul("v2", u00, wd00, comm=h00["down"], name="l0f0_down")
    X1 = post_fwd(cfg, X0, y00, p00["g_post"], p00["gate"], FFN_STEP, "l0f0_post")
    s00 = (X0, h, a00, b00, u00, y00)

    p01 = sub_params(0, 1)
    p01.update(W=W, H=H, conv_w=full["lru_conv_w"][0], conv_b=full["lru_conv_b"], wa=full["lru_wa"][0],
               ba=full["lru_ba"][0][:, None, :], wx=full["lru_wx"][0], bx=full["lru_bx"][0][:, None, :],
               lam=full["lru_lambda"][0][:, None, :], logit=full["ret_decay_logit"][0][:, :, None, None],
               gn=full["ret_gn"], cos1=cos1, sin1=sin1, w_in=ev_in, w_out=ev_out)
    X2, s01, got = even_fwd(cfg, X1, p01, "l0mix", host={"in": G(("ffn_up", (0, 1))), "out": G(("ffn_down", (0, 1)))})
    p02 = sub_params(0, 2)
    p02.update(wg=wg01, wu=got["in"][0], wd=got["out"][0])
    X3, s02, got = ffn_fwd(cfg, X2, p02, "l0f1", host={"gateup": G(("ffn_gate", (1, 0)), ("ffn_up", (1, 0))),
                                                       "down": G(("ffn_down", (1, 0)))})
    p10 = sub_params(1, 0)
    p10.update(wg=got["gateup"][0], wu=got["gateup"][1], wd=got["down"][0])
    X4, s10, got = ffn_fwd(cfg, X3, p10, "l1f0", host={"gateup": G(("od_w_in", (0,)), ("od_w_out", (0,)), ("ffn_gate", (1, 1))),
                                                       "down": G(("ffn_up", (1, 1)))})
    p11 = sub_params(1, 1, gate_ctx_live=False)
    p11.update(nq=nq, pool_w=full["pool_w"][0], pool_scale=full["pool_scale"], qg=full["q_norm"], kg=full["k_norm"],
               cosf=cosf, sinf=sinf, w_in=got["gateup"][0], w_out=got["gateup"][1])
    p12 = sub_params(1, 2, ctx_live=False)
    p12.update(wg=got["gateup"][2], wu=got["down"][0])
    X5, s11, got = odd_fwd(cfg, X4, p11, "l1mix", host={"in": G(("ffn_down", (1, 1)))})
    p12.update(wd=got["in"][0])
    X6, s12, _ = ffn_fwd(cfg, X5, p12, "l1f1")
    sq, dX = loss_fwd_bwd(cfg, X6, loss_target[0], "loss")
    loss = lax.psum(0.5 * jnp.sum(sq) / D, ("x", "y", "c"))

    recv_ffn = {}
    dX, g12, recv_ffn[(1, 1)] = ffn_bwd(cfg, dX, s12, p12, "l1f1")
    dX, g11, recv_od = odd_bwd(cfg, dX, s11, p11, "l1mix")
    dX, g10, recv_ffn[(1, 0)] = ffn_bwd(cfg, dX, s10, p10, "l1f0")
    dX, g02, recv_ffn[(0, 1)] = ffn_bwd(cfg, dX, s02, p02, "l0f1")
    dX, g01, recv_ev = even_bwd(cfg, dX, s01, p01, "l0mix")
    dX, g00, recv_ffn[(0, 0)] = ffn_bwd(cfg, dX, s00, p00, "l0f0")
    grad_x = dX[Lc:][None]

    subs = [[g00, g01, g02], [g10, g11, g12]]
    zero_d = jnp.zeros((D,), f32)

    def dmod(group, live):
        rows = []
        for li in range(nL):
            for s in range(3):
                for kind, key in enumerate(("shift", "scale", "gate")):
                    rows.append(subs[li][s][key][group, 0] if live(li, s, kind) else zero_d)
        return jnp.stack(rows).reshape(nL, 9 * D)

    dmod_l = dmod(1, lambda li, s, kind: True)
    dmod_c = dmod(0, lambda li, s, kind: not (li == 1 and (s == 2 or (s == 1 and kind == 2))))

    dm_in, _ = _pack([dmod_l, dmod_c])
    g3 = all_gather_small(dm_in, "gather_dmod")
    dl_all, dc_all = _unpack(g3, [dmod_l.shape, dmod_c.shape], lead=(N_DEV,))
    dm16 = jnp.moveaxis(jnp.concatenate([dl_all, dc_all], axis=0), 0, 1)
    dm16 = lax.dynamic_slice_in_dim(dm16, chip * C4, C4, axis=2)
    s16b = jnp.concatenate([sc_all, jnp.broadcast_to(scc, (N_DEV, D))])
    g_mod_w, dscc_part = mod_bwd(s16b, dm16, mod_w, "mod_bwd")

    norm_pre_g = jnp.stack([jnp.concatenate([subs[li][s]["g_pre"] for s in range(3)]) for li in range(nL)])
    norm_post_g = jnp.stack([jnp.concatenate([subs[li][s]["g_post"] for s in range(3)]) for li in range(nL)])
    small_g = dict(norm_pre=norm_pre_g, norm_post=norm_post_g, lru_conv_w=g01["conv_w"][None], lru_ba=g01["ba"][:, 0][None],
                   lru_bx=g01["bx"][:, 0][None], lru_lambda=g01["lam"][:, 0][None], pool_scale=g11["pool_scale"],
                   mod_b=dmod_l + dmod_c, lru_conv_b=g01["conv_b"], lru_wa=g01["wa"][None], lru_wx=g01["wx"][None],
                   ret_decay_logit=g01["logit"][:, :, 0, 0][None], ret_gn=g01["gn"], pool_w=g11["pool_w"][None],
                   q_norm=g11["qg"], k_norm=g11["kg"])
    names = SMALL_SHARDED + SMALL_REPL
    sg_in, _ = _pack([small_g[n] for n in names] + [dscc_part])
    g4 = all_gather_small(sg_in, "gather_small_grads")
    tot = sum_leading(g4, "sum_small_grads")
    tot_parts = _unpack(tot, [small_g[n].shape for n in names])
    dscc_all = _unpack(g4, [small_g[n].shape for n in names] + [dscc_part.shape], lead=(N_DEV,))[-1]
    dscc = dscc_all[0, 0] + dscc_all[2, 0] + dscc_all[4, 0] + dscc_all[6, 0]
    _, silu_vjp = jax.vjp(jax.nn.silu, c_ctx)
    grads = {"c_ctx": silu_vjp(dscc)[0]}
    for n, g in zip(names, tot_parts):
        if n in SMALL_SHARDED:
            k = wts[n].shape[-1]
            g = lax.dynamic_slice_in_dim(g, chip * k, k, axis=g.ndim - 1)
        grads[n] = g.reshape(wts[n].shape)

    partial = {}
    for n, key in (("ffn_gate", "wg"), ("ffn_up", "wu"), ("ffn_down", "wd")):
        acc = None
        for u in ffn_units:
            acc = sum_leading(recv_ffn[u][key], "sum_%s_%d%d" % (n, u[0], u[1]), into=acc, full_shape=wts[n].shape, widx=u)
        partial[n] = acc
    for n, r in (("ev_w_in", recv_ev["w_in"]), ("ev_w_out", recv_ev["w_out"]), ("od_w_in", recv_od["w_in"]),
                 ("od_w_out", recv_od["w_out"])):
        partial[n] = sum_leading(r, "sum_" + n).reshape(wts[n].shape)
    partial = [partial[n] for n in BIG]
    other = swap_with_sibling(partial, "swap_partials")

    delta, new_m, new_v = {}, {}, {}
    for n, pa, pb in zip(BIG, partial, other):
        grads[n], delta[n], new_m[n], new_v[n] = adamw(wts[n], mom_m[n], mom_v[n], [pa, pb], "adamw_" + n)
    grads["mod_w"], delta["mod_w"], new_m["mod_w"], new_v["mod_w"] = adamw(mod_w, m_mod_w, v_mod_w, [g_mod_w], "adamw_mod_w")
    snames = [n for n in WEIGHT_NAMES if n not in BIG and n != "mod_w"]
    pk = lambda d: _pack([d[n] for n in snames])[0]
    sres = adamw(pk(wts), pk(mom_m), pk(mom_v), [pk(grads)], "adamw_small")
    for res, dst in zip(sres[1:], (delta, new_m, new_v)):
        for n, a in zip(snames, _unpack(res, [wts[n].shape for n in snames])):
            dst[n] = a

    return (loss, grad_x, *[grads[n] for n in WEIGHT_NAMES], *[delta[n] for n in WEIGHT_NAMES],
            *[new_m[n] for n in WEIGHT_NAMES], *[new_v[n] for n in WEIGHT_NAMES])
```

```python
import functools
import math

import jax
import jax.numpy as jnp
from jax import lax
from jax.experimental import pallas as pl
from jax.experimental.pallas import tpu as pltpu

f32 = jnp.float32
bf16 = jnp.bfloat16
MESH = pl.DeviceIdType.MESH

EPS = 1e-6
FFN_STEP = 0.5
LRU_C = 8.0
CONV_W = 4
CONV_LEFT = 2
RET_DK = 256
RET_DV = 256
RET_CHUNK = 128
RET_THETA = 10000.0
POOL_WINDOWS = (2, 4, 8, 16)
POOL_GROUP = 128
HEAD_DIM = 128
ROPE_THETA = 10000.0
GRID_W = 64
ADAM_LR = 0.001
ADAM_B1 = 0.9
ADAM_B2 = 0.999
ADAM_EPS = 1e-08
ADAM_WD = 0.01
ADAM_STEP = 10

N_CHIPS = 4
N_DEV = 8
HALO = 8
VMEM_LIMIT = 56 * 1024 * 1024


def _params(sem=None):
    return pltpu.CompilerParams(dimension_semantics=sem, vmem_limit_bytes=VMEM_LIMIT)


def _pick(n, prefs):
    for p in prefs:
        if n % p == 0:
            return p
    return n


def _sds(shape, dtype):
    return jax.ShapeDtypeStruct(tuple(shape), dtype)


_MM_KINDS = {
    "v1": ((1, 0), "out[:, g] = A @ W[g]"),
    "v2": ((1, 0), "out = sum_g A[:, g] @ W[g]"),
    "v3": ((1, 1), "out = sum_g A[:, g] @ W[g]^T"),
    "v4": ((1, 1), "out[:, g] = A @ W[g]^T"),
    "v5": ((0, 0), "out[g] = A^T @ C[:, g]"),
    "v6": ((0, 0), "out[g] = A[:, g]^T @ C"),
}


def matmul(kind, a, b, *, widx=(), out_dtype=f32, init=None, gshape=None, comm=None, name):
    nw = len(widx)
    cdims = _MM_KINDS[kind][0]
    if kind in ("v1", "v2", "v3", "v4"):
        G = b.shape[0]
        d1, d2 = b.shape[-2:]
        M = a.shape[0]
    else:
        G, d1, d2 = gshape
        M = a.shape[0]
    tm_p, tn_p, tk_p = (768, 512, 256, 128), (1408, 1536, 1024, 768, 512, 256, 128), (2048, 1408, 1536, 1024, 768, 512, 256, 128)
    wnone = (None,) * (1 + nw)

    if kind == "v1":
        K, Ns = d1, d2
        tm, tn, tk = _pick(M, tm_p), _pick(Ns, tn_p), _pick(K, tk_p)
        nI, nJ, nR = M // tm, Ns // tn, K // tk
        grid = (G, nI, nJ, nR)
        a_spec = pl.BlockSpec((tm, tk), lambda g, i, j, r: (i, r))
        b_spec = pl.BlockSpec(wnone + (tk, tn), lambda g, i, j, r: (g,) + widx + (r, j))
        o_spec = pl.BlockSpec((tm, tn), lambda g, i, j, r: (i, g * nJ + j))
        out_shape = _sds((M, G * Ns), out_dtype)
        acc_shape = (tm, tn)
    elif kind == "v2":
        Ks, N = d1, d2
        tm, tn, tk = _pick(M, tm_p), _pick(N, tn_p), _pick(Ks, tk_p)
        nI, nJ, nRk = M // tm, N // tn, Ks // tk
        nR = G * nRk
        grid = (1, nI, nJ, nR)
        a_spec = pl.BlockSpec((tm, tk), lambda g, i, j, r: (i, r))
        b_spec = pl.BlockSpec(wnone + (tk, tn), lambda g, i, j, r: (r // nRk,) + widx + (r % nRk, j))
        o_spec = pl.BlockSpec((tm, tn), lambda g, i, j, r: (i, j))
        out_shape = _sds((M, N), out_dtype)
        acc_shape = (tm, tn)
    elif kind == "v3":
        K, Ns = d1, d2
        tm, tn, tk = _pick(M, tm_p), _pick(K, tn_p), _pick(Ns, tk_p)
        nI, nJ, nRk = M // tm, K // tn, Ns // tk
        nR = G * nRk
        grid = (1, nI, nJ, nR)
        a_spec = pl.BlockSpec((tm, tk), lambda g, i, j, r: (i, r))
        b_spec = pl.BlockSpec(wnone + (tn, tk), lambda g, i, j, r: (r // nRk,) + widx + (j, r % nRk))
        o_spec = pl.BlockSpec((tm, tn), lambda g, i, j, r: (i, j))
        out_shape = _sds((M, K), out_dtype)
        acc_shape = (tm, tn)
    elif kind == "v4":
        Ks, N = d1, d2
        tm, tn, tk = _pick(M, tm_p), _pick(Ks, tn_p), _pick(N, tk_p)
        nI, nJ, nR = M // tm, Ks // tn, N // tk
        grid = (G, nI, nJ, nR)
        a_spec = pl.BlockSpec((tm, tk), lambda g, i, j, r: (i, r))
        b_spec = pl.BlockSpec(wnone + (tn, tk), lambda g, i, j, r: (g,) + widx + (j, r))
        o_spec = pl.BlockSpec((tm, tn), lambda g, i, j, r: (i, g * nJ + j))
        out_shape = _sds((M, G * Ks), out_dtype)
        acc_shape = (tm, tn)
    elif kind == "v5":
        K, Ns = d1, d2
        tm, tn, tk = _pick(K, (2048,) + tm_p), _pick(Ns, tn_p), _pick(M, (768, 512, 256, 128))
        nI, nJ, nR = K // tm, Ns // tn, M // tk
        grid = (G, nI, nJ, nR)
        a_spec = pl.BlockSpec((tk, tm), lambda g, i, j, r: (r, i))
        b_spec = pl.BlockSpec((tk, tn), lambda g, i, j, r: (r, g * nJ + j))
        o_spec = pl.BlockSpec((None, tm, tn), lambda g, i, j, r: (g, i, j))
        out_shape = _sds(gshape, out_dtype)
        acc_shape = (tm, tn)
    else:
        Ks, N = d1, d2
        tm, tn, tk = _pick(Ks, (1408,) + tm_p), _pick(N, (2048,) + tn_p), _pick(M, (768, 512, 256, 128))
        nI, nJ, nR = Ks // tm, N // tn, M // tk
        grid = (G, nI, nJ, nR)
        a_spec = pl.BlockSpec((tk, tm), lambda g, i, j, r: (r, g * nI + i))
        b_spec = pl.BlockSpec((tk, tn), lambda g, i, j, r: (r, j))
        o_spec = pl.BlockSpec((None, tm, tn), lambda g, i, j, r: (g, i, j))
        out_shape = _sds(gshape, out_dtype)
        acc_shape = (tm, tn)

    has_init = init is not None
    ncomm = len(comm.srcs) if comm is not None else 0

    def body(*refs):
        a_ref, b_ref = refs[0], refs[1]
        pos = 2
        init_ref = None
        if has_init:
            init_ref = refs[pos]
            pos += 1
        cin = refs[pos:pos + ncomm]
        pos += ncomm
        o_ref = refs[pos]
        cout = refs[pos + 1:pos + 1 + ncomm]
        acc_ref = refs[pos + 1 + ncomm]
        sems = refs[pos + 2 + ncomm:]
        r = pl.program_id(3)
        first, last = _grid_ends(grid)

        if ncomm:
            @pl.when(first)
            def _():
                _comm_start(comm, cin, cout, *sems)

        @pl.when(r == 0)
        def _():
            if has_init:
                acc_ref[...] = init_ref[...]
            else:
                acc_ref[...] = jnp.zeros(acc_shape, f32)

        acc_ref[...] += lax.dot_general(a_ref[...], b_ref[...], ((cdims[:1], cdims[1:]), ((), ())),
                                        preferred_element_type=f32)

        @pl.when(r == nR - 1)
        def _():
            o_ref[...] = acc_ref[...].astype(o_ref.dtype)

        if ncomm:
            @pl.when(last)
            def _():
                _comm_wait(comm, cin, cout, *sems)

    in_specs = [a_spec, b_spec]
    args = [a, b]
    if has_init:
        in_specs.append(pl.BlockSpec((tm, tn), lambda g, i, j, r: (i, j)))
        args.append(init)
    out_specs, out_shapes, scratch = [o_spec], [out_shape], [pltpu.VMEM(acc_shape, f32)]
    if ncomm:
        hbm = pl.BlockSpec(memory_space=pl.ANY)
        in_specs += [hbm] * ncomm
        args += [src for src, _ in comm.srcs]
        out_specs += [hbm] * ncomm
        out_shapes += comm.out_shapes()
        scratch += _comm_sems(ncomm)
    res = pl.pallas_call(
        body, name=name, grid=grid, in_specs=in_specs, out_specs=out_specs, out_shape=out_shapes,
        scratch_shapes=scratch, compiler_params=_params(("arbitrary", "arbitrary", "arbitrary", "arbitrary")),
    )(*args)
    return (res[0], list(res[1:])) if ncomm else res[0]


class Comm:
    def __init__(self, mode, srcs):
        self.mode, self.srcs = mode, srcs

    def piece(self, n):
        arr, idx = self.srcs[n]
        shp = arr.shape[len(idx):]
        return shp if self.mode == "gather" else shp[1:]

    def out_shapes(self):
        return [_sds((N_CHIPS,) + tuple(self.piece(n)), self.srcs[n][0].dtype) for n in range(len(self.srcs))]


def _comm_sems(n):
    nsem = n * (N_CHIPS - 1)
    return [pltpu.SemaphoreType.DMA((nsem,)), pltpu.SemaphoreType.DMA((nsem,)), pltpu.SemaphoreType.DMA((n,))]


def _coords():
    return lax.axis_index("x"), lax.axis_index("y"), lax.axis_index("c")


def _flip(v, bit):
    return 1 - v if bit else v


def _chip_peers(x, y, c):
    out = []
    for k in range(1, N_CHIPS):
        kx, ky = (k >> 1) & 1, k & 1
        px, py = _flip(x, kx), _flip(y, ky)
        out.append((k, (px, py, c), 2 * px + py))
    return out


def _comm_copies(comm, in_refs, out_refs, send_sems, recv_sems, local_sems, with_recvs):
    x, y, c = _coords()
    s = 2 * x + y
    local, sends, recvs = [], [], []
    for w, (_, idx) in enumerate(comm.srcs):
        src = in_refs[w].at[idx] if idx else in_refs[w]
        out = out_refs[w]
        if comm.mode == "gather":
            local.append(pltpu.make_async_copy(src, out.at[s], local_sems.at[w]))
        else:
            local.append(pltpu.make_async_copy(src.at[s], out.at[N_CHIPS - 1], local_sems.at[w]))
        for k, peer, pidx in _chip_peers(x, y, c):
            j = w * (N_CHIPS - 1) + k - 1
            if comm.mode == "gather":
                out_src, out_dst, in_dst = src, out.at[s], out.at[pidx]
            else:
                out_src, out_dst, in_dst = src.at[pidx], out.at[k - 1], out.at[k - 1]
            sends.append(pltpu.make_async_remote_copy(src_ref=out_src, dst_ref=out_dst, send_sem=send_sems.at[j],
                                                      recv_sem=recv_sems.at[j], device_id=peer, device_id_type=MESH))
            if with_recvs:
                recvs.append(pltpu.make_async_remote_copy(src_ref=out_src, dst_ref=in_dst, send_sem=send_sems.at[j],
                                                          recv_sem=recv_sems.at[j], device_id=peer, device_id_type=MESH))
    return local, sends, recvs


def _comm_start(comm, in_refs, out_refs, send_sems, recv_sems, local_sems):
    local, sends, _ = _comm_copies(comm, in_refs, out_refs, send_sems, recv_sems, local_sems, False)
    for cp in local + sends:
        cp.start()


def _comm_wait(comm, in_refs, out_refs, send_sems, recv_sems, local_sems):
    local, sends, recvs = _comm_copies(comm, in_refs, out_refs, send_sems, recv_sems, local_sems, True)
    for cp in recvs:
        cp.wait_recv()
    for cp in sends:
        cp.wait_send()
    for cp in local:
        cp.wait()


def exchange(comm, name):
    n = len(comm.srcs)

    def body(*refs):
        in_refs, out_refs, sems = refs[:n], refs[n:2 * n], refs[2 * n:]
        _comm_start(comm, in_refs, out_refs, *sems)
        _comm_wait(comm, in_refs, out_refs, *sems)

    hbm = pl.BlockSpec(memory_space=pl.ANY)
    return pl.pallas_call(body, name=name, out_shape=comm.out_shapes(), in_specs=[hbm] * n, out_specs=[hbm] * n,
                          scratch_shapes=_comm_sems(n))(*[src for src, _ in comm.srcs])


class RowCfg:
    def __init__(self, TR, nT, cT):
        self.TR, self.nT, self.cT = TR, nT, cT


def _row_spec(cfg, spec, off):
    kind = spec[0]
    TR = cfg.TR
    hb = TR // HALO
    nH = cfg.nT * hb
    if kind == "row":
        _, arr, w, cb = spec
        return pl.BlockSpec((TR, w), lambda i: (i + off, cb))
    if kind == "prev":
        _, arr, w, cb = spec
        return pl.BlockSpec((HALO, w), lambda i: (jnp.maximum((i + off) * hb - 1, 0), cb))
    if kind == "next":
        _, arr, w, cb = spec
        return pl.BlockSpec((HALO, w), lambda i: (jnp.minimum((i + off + 1) * hb, nH - 1), cb))
    if kind == "full":
        arr = spec[1]
        nd = arr.ndim
        return pl.BlockSpec(arr.shape, lambda i: (0,) * nd)
    if kind == "grp":
        arr = spec[1]
        cT = cfg.cT
        return pl.BlockSpec((None, 1, arr.shape[-1]), lambda i: (((i + off) >= cT).astype(jnp.int32), 0, 0))
    if kind == "drow":
        _, arr, w, cb = spec
        return pl.BlockSpec((arr.shape[0], TR, w), lambda i: (0, i + off, cb))
    raise ValueError(kind)


def rowcall(cfg, fn, name, ins, outs, *, off=0, n=None, scratch=()):
    n = cfg.nT - off if n is None else n
    in_specs = [_row_spec(cfg, s, off) for s in ins]
    out_specs = [_row_spec(cfg, (s[0], s[1]) + tuple(s[2:]), off) for s in outs]
    out_shape = [s[1] for s in outs]

    def body(*refs):
        fn(pl.program_id(0) + off, *refs)

    res = pl.pallas_call(
        body, name=name, grid=(n,), in_specs=in_specs, out_specs=out_specs, out_shape=out_shape,
        scratch_shapes=list(scratch), compiler_params=_params(("arbitrary",)),
    )(*[s[1] for s in ins])
    return res


def _acc(ref, val, first):
    @pl.when(first)
    def _():
        ref[...] = val

    @pl.when(jnp.logical_not(first))
    def _():
        ref[...] += val


def _rms(x):
    return x * lax.rsqrt(jnp.mean(x * x, axis=-1, keepdims=True) + EPS)


def _pre_fn(x, g, shift, scale):
    return (_rms(x) * g) * (1.0 + scale) + shift


def pre_fwd(cfg, x, g, shift, scale, name):
    D = x.shape[1]

    def fn(i, x_ref, g_ref, sh_ref, sc_ref, h_ref):
        h_ref[...] = _pre_fn(x_ref[...], g_ref[...], sh_ref[...], sc_ref[...]).astype(bf16)

    return rowcall(cfg, fn, name, [("row", x, D, 0), ("full", g), ("grp", shift), ("grp", scale)],
                   [("row", _sds(x.shape, bf16), D, 0)])[0]


def pre_bwd(cfg, x, g, shift, scale, dh, dx_in, name):
    D = x.shape[1]
    cT = cfg.cT

    def fn(i, x_ref, g_ref, sh_ref, sc_ref, dh_ref, dxin_ref, dx_ref, dg_ref, dsh_ref, dsc_ref):
        _, vjp = jax.vjp(_pre_fn, x_ref[...], g_ref[...], sh_ref[...], sc_ref[...])
        dx, dg, dsh, dsc = vjp(dh_ref[...])
        dx_ref[...] = dxin_ref[...] + dx
        _acc(dg_ref, dg, i == 0)
        first = jnp.logical_or(i == 0, i == cT)
        _acc(dsh_ref, dsh, first)
        _acc(dsc_ref, dsc, first)

    return rowcall(cfg, fn, name,
                   [("row", x, D, 0), ("full", g), ("grp", shift), ("grp", scale), ("row", dh, D, 0), ("row", dx_in, D, 0)],
                   [("row", _sds(x.shape, f32), D, 0), ("full", _sds((1, D), f32)),
                    ("grp", _sds((2, 1, D), f32)), ("grp", _sds((2, 1, D), f32))])


def _post_fn(w, y, g, gate):
    return (w * gate) * (_rms(y) * g)


def post_fwd(cfg, x, y, g, gate, w, name):
    D = x.shape[1]

    def fn(i, x_ref, y_ref, g_ref, gt_ref, o_ref):
        o_ref[...] = x_ref[...] + _post_fn(w, y_ref[...], g_ref[...], gt_ref[...])

    return rowcall(cfg, fn, name, [("row", x, D, 0), ("row", y, D, 0), ("full", g), ("grp", gate)],
                   [("row", _sds(x.shape, f32), D, 0)])[0]


def post_bwd(cfg, dx, y, g, gate, w, name):
    D = dx.shape[1]
    cT = cfg.cT

    def fn(i, dx_ref, y_ref, g_ref, gt_ref, dy_ref, dg_ref, dgt_ref):
        _, vjp = jax.vjp(functools.partial(_post_fn, w), y_ref[...], g_ref[...], gt_ref[...])
        dy, dg, dgt = vjp(dx_ref[...])
        dy_ref[...] = dy.astype(bf16)
        _acc(dg_ref, dg, i == 0)
        _acc(dgt_ref, dgt, jnp.logical_or(i == 0, i == cT))

    return rowcall(cfg, fn, name, [("row", dx, D, 0), ("row", y, D, 0), ("full", g), ("grp", gate)],
                   [("row", _sds(dx.shape, bf16), D, 0), ("full", _sds((1, D), f32)), ("grp", _sds((2, 1, D), f32))])


def _swiglu_fn(a, b):
    return jax.nn.silu(a) * b


def swiglu_fwd(cfg, a, b, name):
    F = a.shape[1]
    tf = _pick(F, (1408, 1024, 512, 256, 128))
    TR = cfg.TR

    def body(a_ref, b_ref, u_ref):
        u_ref[...] = _swiglu_fn(a_ref[...], b_ref[...]).astype(bf16)

    spec = pl.BlockSpec((TR, tf), lambda i, j: (i, j))
    return pl.pallas_call(body, name=name, grid=(cfg.nT, F // tf), in_specs=[spec, spec], out_specs=spec,
                          out_shape=_sds(a.shape, bf16), compiler_params=_params(("arbitrary", "arbitrary")))(a, b)


def swiglu_bwd(cfg, a, b, du, name):
    F = a.shape[1]
    tf = _pick(F, (1408, 1024, 512, 256, 128))
    TR = cfg.TR

    def body(a_ref, b_ref, du_ref, da_ref, db_ref):
        _, vjp = jax.vjp(_swiglu_fn, a_ref[...], b_ref[...])
        da, db = vjp(du_ref[...])
        da_ref[...] = da.astype(bf16)
        db_ref[...] = db.astype(bf16)

    spec = pl.BlockSpec((TR, tf), lambda i, j: (i, j))
    return pl.pallas_call(body, name=name, grid=(cfg.nT, F // tf), in_specs=[spec, spec, spec], out_specs=[spec, spec],
                          out_shape=[_sds(a.shape, bf16), _sds(a.shape, bf16)],
                          compiler_params=_params(("arbitrary", "arbitrary")))(a, b, du)


def _hosted(host, role, got, fn):
    comm = host.get(role) if host else None
    if comm is None:
        return fn(None)
    out, res = fn(comm)
    got[role] = res
    return out


def _grid_ends(grid):
    ids = [pl.program_id(n) for n in range(len(grid))]
    first = functools.reduce(jnp.logical_and, [i == 0 for i in ids])
    last = functools.reduce(jnp.logical_and, [i == n - 1 for i, n in zip(ids, grid)])
    return first, last


def _comm_plumbing(comm):
    if comm is None:
        return [], [], [], [], []
    n = len(comm.srcs)
    hbm = pl.BlockSpec(memory_space=pl.ANY)
    return [hbm] * n, [src for src, _ in comm.srcs], [hbm] * n, comm.out_shapes(), _comm_sems(n)


FFN_ROWS = 384


def ffn_gateup(cfg, h, wg, wu, name, comm=None):
    M, K = h.shape
    G, _, F = wg.shape
    tm = _pick(M, (FFN_ROWS, 256, 128))
    grid = (G, M // tm)
    ncomm = len(comm.srcs) if comm is not None else 0

    def body(h_ref, wg_ref, wu_ref, *rest):
        cin, rest = rest[:ncomm], rest[ncomm:]
        a_ref, b_ref, u_ref = rest[:3]
        cout, sems = rest[3:3 + ncomm], rest[3 + ncomm:]
        first, last = _grid_ends(grid)
        if ncomm:
            @pl.when(first)
            def _():
                _comm_start(comm, cin, cout, *sems)

        hv = h_ref[...]
        a = jnp.dot(hv, wg_ref[...], preferred_element_type=f32)
        b = jnp.dot(hv, wu_ref[...], preferred_element_type=f32)
        a_ref[...] = a.astype(bf16)
        b_ref[...] = b.astype(bf16)
        u_ref[...] = _swiglu_fn(a, b).astype(bf16)
        if ncomm:
            @pl.when(last)
            def _():
                _comm_wait(comm, cin, cout, *sems)

    ci, ca, co, cs, csem = _comm_plumbing(comm)
    w_spec = pl.BlockSpec((None, K, F), lambda g, i: (g, 0, 0))
    o_spec = pl.BlockSpec((tm, F), lambda g, i: (i, g))
    res = pl.pallas_call(
        body, name=name, grid=grid, in_specs=[pl.BlockSpec((tm, K), lambda g, i: (i, 0)), w_spec, w_spec] + ci,
        out_specs=[o_spec] * 3 + co, out_shape=[_sds((M, G * F), bf16)] * 3 + cs, scratch_shapes=csem,
        compiler_params=_params(("arbitrary", "arbitrary")))(h, wg, wu, *ca)
    return (res[0], res[1], res[2]), list(res[3:])


def ffn_du_act(cfg, dy, wd, a, b, name, comm=None):
    M, N = dy.shape
    G, F, _ = wd.shape
    tm = _pick(M, (FFN_ROWS, 256, 128))
    grid = (G, M // tm)
    ncomm = len(comm.srcs) if comm is not None else 0

    def body(dy_ref, wd_ref, a_ref, b_ref, *rest):
        cin, rest = rest[:ncomm], rest[ncomm:]
        da_ref, db_ref = rest[:2]
        cout, sems = rest[2:2 + ncomm], rest[2 + ncomm:]
        first, last = _grid_ends(grid)
        if ncomm:
            @pl.when(first)
            def _():
                _comm_start(comm, cin, cout, *sems)

        du = _dotf(dy_ref[...], wd_ref[...], 1, 1)
        _, vjp = jax.vjp(_swiglu_fn, a_ref[...].astype(f32), b_ref[...].astype(f32))
        da, db = vjp(du)
        da_ref[...] = da.astype(bf16)
        db_ref[...] = db.astype(bf16)
        if ncomm:
            @pl.when(last)
            def _():
                _comm_wait(comm, cin, cout, *sems)

    ci, ca, co, cs, csem = _comm_plumbing(comm)
    t_spec = pl.BlockSpec((tm, F), lambda g, i: (i, g))
    res = pl.pallas_call(
        body, name=name, grid=grid,
        in_specs=[pl.BlockSpec((tm, N), lambda g, i: (i, 0)), pl.BlockSpec((None, F, N), lambda g, i: (g, 0, 0)), t_spec,
                  t_spec] + ci,
        out_specs=[t_spec, t_spec] + co, out_shape=[_sds((M, G * F), bf16)] * 2 + cs, scratch_shapes=csem,
        compiler_params=_params(("arbitrary", "arbitrary")))(dy, wd, a, b, *ca)
    return (res[0], res[1]), list(res[2:])


def ffn_fwd(cfg, x, p, tag, host=None):
    got = {}
    host = host or {}
    h = pre_fwd(cfg, x, p["g_pre"], p["shift"], p["scale"], tag + "_pre")
    (a, b, u), res = ffn_gateup(cfg, h, p["wg"], p["wu"], tag + "_gateup", comm=host.get("gateup"))
    if res:
        got["gateup"] = res
    y = _hosted(host, "down", got, lambda cm: matmul("v2", u, p["wd"], comm=cm, name=tag + "_down"))
    xo = post_fwd(cfg, x, y, p["g_post"], p["gate"], FFN_STEP, tag + "_post")
    return xo, (x, h, a, b, u, y), got


def ffn_bwd(cfg, dX, saved, p, tag):
    x, h, a, b, u, y = saved
    dy, dg_post, dgate = post_bwd(cfg, dX, y, p["g_post"], p["gate"], FFN_STEP, tag + "_postb")
    (da, db), _ = ffn_du_act(cfg, dy, p["wd"], a, b, tag + "_duact")
    gwd = matmul("v6", u, dy, gshape=p["wd"].shape, out_dtype=bf16, name=tag + "_gwd")
    gwg = matmul("v5", h, da, gshape=p["wg"].shape, out_dtype=bf16, name=tag + "_gwg")
    gwu, r_wd = matmul("v5", h, db, gshape=p["wu"].shape, out_dtype=bf16, comm=Comm("scatter", [(gwd, ())]),
                       name=tag + "_gwu")
    dh, r_wg = matmul("v3", da, p["wg"], comm=Comm("scatter", [(gwg, ())]), name=tag + "_dh1")
    dh, r_wu = matmul("v3", db, p["wu"], init=dh, comm=Comm("scatter", [(gwu, ())]), name=tag + "_dh2")
    dX, dg_pre, dshift, dscale = pre_bwd(cfg, x, p["g_pre"], p["shift"], p["scale"], dh, dX, tag + "_preb")
    small = dict(g_pre=dg_pre, g_post=dg_post, shift=dshift, scale=dscale, gate=dgate)
    return dX, small, dict(wg=r_wg[0], wu=r_wu[0], wd=r_wd[0])


def _seg_flags(cfg, i):
    start = jnp.logical_or(i == 0, i == cfg.cT)
    end = jnp.logical_or(i == cfg.cT - 1, i == cfg.nT - 1)
    return start, end


def _fill_halo(buf, cur, prev, nxt, start, end, TR):
    buf[pl.ds(0, HALO), :] = jnp.where(start, 0.0, prev)
    buf[pl.ds(HALO, TR), :] = cur
    buf[pl.ds(HALO + TR, HALO), :] = jnp.where(end, 0.0, nxt)


def conv_fwd(cfg, z, cw, cb, W, name):
    TR = cfg.TR

    def fn(i, r_ref, rp_ref, rn_ref, cw_ref, cb_ref, u_ref, buf):
        start, end = _seg_flags(cfg, i)
        _fill_halo(buf, r_ref[...], rp_ref[...], rn_ref[...], start, end, TR)
        u = jnp.broadcast_to(cb_ref[...], (TR, W))
        for k in range(CONV_W):
            u = u + buf[pl.ds(HALO + k - CONV_LEFT, TR), :] * cw_ref[pl.ds(k, 1), :]
        u_ref[...] = u

    return rowcall(cfg, fn, name, [("row", z, W, 1), ("prev", z, W, 1), ("next", z, W, 1), ("full", cw), ("full", cb)],
                   [("row", _sds((z.shape[0], W), f32), W, 0)], scratch=[pltpu.VMEM((TR + 2 * HALO, W), f32)])[0]


def conv_bwd(cfg, z, du, cw, W, name):
    TR = cfg.TR

    def fn(i, r_ref, rp_ref, rn_ref, du_ref, dup_ref, dun_ref, cw_ref, dr_ref, dcw_ref, dcb_ref, rbuf, dbuf):
        start, end = _seg_flags(cfg, i)
        _fill_halo(rbuf, r_ref[...], rp_ref[...], rn_ref[...], start, end, TR)
        _fill_halo(dbuf, du_ref[...], dup_ref[...], dun_ref[...], start, end, TR)
        du = du_ref[...]

        @pl.when(i == 0)
        def _():
            dcw_ref[...] = jnp.zeros(dcw_ref.shape, f32)
            dcb_ref[...] = jnp.zeros(dcb_ref.shape, f32)

        dr = jnp.zeros((TR, W), f32)
        for k in range(CONV_W):
            dr = dr + dbuf[pl.ds(HALO - (k - CONV_LEFT), TR), :] * cw_ref[pl.ds(k, 1), :]
            dcw_ref[pl.ds(k, 1), :] += jnp.sum(du * rbuf[pl.ds(HALO + k - CONV_LEFT, TR), :], axis=0, keepdims=True)
        dcb_ref[...] += jnp.sum(du, axis=0, keepdims=True)
        dr_ref[...] = dr

    T = z.shape[0]
    return rowcall(cfg, fn, name,
                   [("row", z, W, 1), ("prev", z, W, 1), ("next", z, W, 1), ("row", du, W, 0), ("prev", du, W, 0),
                    ("next", du, W, 0), ("full", cw)],
                   [("row", _sds((T, W), f32), W, 0), ("full", _sds((CONV_W, W), f32)), ("full", _sds((1, W), f32))],
                   scratch=[pltpu.VMEM((TR + 2 * HALO, W), f32), pltpu.VMEM((TR + 2 * HALO, W), f32)])


def _softplus(x):
    return jnp.maximum(x, 0.0) + jnp.log1p(jnp.exp(-jnp.abs(x)))


def _neg_expm1(x):
    series = -x * (1.0 + x * (0.5 + x * (1.0 / 6.0 + x * (1.0 / 24.0 + x * (1.0 / 120.0 + x * (1.0 / 720.0))))))
    return jnp.where(x > -0.1, series, 1.0 - jnp.exp(x))


def _lru_coef(u, pa, px, lam):
    r = jax.nn.sigmoid(pa)
    i = jax.nn.sigmoid(px)
    log_a = -LRU_C * r * _softplus(-lam)
    a = jnp.exp(log_a)
    b = jnp.sqrt(_neg_expm1(2.0 * log_a)) * (i * u)
    return a, b


def _blockdiag(u_bf, w_ref, d, nblk, blk):
    return jnp.concatenate(
        [jnp.dot(u_bf[:, n * blk:(n + 1) * blk], w_ref[d, n].astype(bf16), preferred_element_type=f32)
         for n in range(nblk)], axis=1)


def lru_coef_fwd(cfg, u, wa, ba, wx, bx, lam, name):
    T, W = u.shape
    nblk, blk = wa.shape[1], wa.shape[2]

    def fn(i, u_ref, wa_ref, ba_ref, wx_ref, bx_ref, lam_ref, a_ref, b_ref):
        uv = u_ref[...]
        u_bf = uv.astype(bf16)
        for d in range(2):
            pa = _blockdiag(u_bf, wa_ref, d, nblk, blk) + ba_ref[d]
            px = _blockdiag(u_bf, wx_ref, d, nblk, blk) + bx_ref[d]
            a, b = _lru_coef(uv, pa, px, lam_ref[d])
            a_ref[d] = a
            b_ref[d] = b

    return rowcall(cfg, fn, name, [("row", u, W, 0), ("full", wa), ("full", ba), ("full", wx), ("full", bx), ("full", lam)],
                   [("drow", _sds((2, T, W), f32), W, 0), ("drow", _sds((2, T, W), f32), W, 0)])


def lru_coef_bwd(cfg, u, da, db, wa, ba, wx, bx, lam, name):
    T, W = u.shape
    nblk, blk = wa.shape[1], wa.shape[2]

    def fn(i, u_ref, da_ref, db_ref, wa_ref, ba_ref, wx_ref, bx_ref, lam_ref,
           du_ref, dwa_ref, dba_ref, dwx_ref, dbx_ref, dlam_ref):
        @pl.when(i == 0)
        def _():
            for r in (dwa_ref, dba_ref, dwx_ref, dbx_ref, dlam_ref):
                r[...] = jnp.zeros(r.shape, f32)

        uv = u_ref[...]
        u_bf = uv.astype(bf16)
        du = jnp.zeros(uv.shape, f32)
        for d in range(2):
            pa = _blockdiag(u_bf, wa_ref, d, nblk, blk) + ba_ref[d]
            px = _blockdiag(u_bf, wx_ref, d, nblk, blk) + bx_ref[d]
            _, vjp = jax.vjp(_lru_coef, uv, pa, px, lam_ref[d])
            du_e, dpa, dpx, dlam = vjp((da_ref[d], db_ref[d]))
            du = du + du_e
            dba_ref[d] += jnp.sum(dpa, axis=0, keepdims=True)
            dbx_ref[d] += jnp.sum(dpx, axis=0, keepdims=True)
            dlam_ref[d] += dlam
            parts = []
            for n in range(nblk):
                sl = slice(n * blk, (n + 1) * blk)
                ga, gx = dpa[:, sl].astype(bf16), dpx[:, sl].astype(bf16)
                ub = u_bf[:, sl]
                dwa_ref[d, n] += lax.dot_general(ub, ga, (((0,), (0,)), ((), ())), preferred_element_type=f32)
                dwx_ref[d, n] += lax.dot_general(ub, gx, (((0,), (0,)), ((), ())), preferred_element_type=f32)
                parts.append(
                    lax.dot_general(ga, wa_ref[d, n].astype(bf16), (((1,), (1,)), ((), ())), preferred_element_type=f32)
                    + lax.dot_general(gx, wx_ref[d, n].astype(bf16), (((1,), (1,)), ((), ())), preferred_element_type=f32))
            du = du + jnp.concatenate(parts, axis=1)
        du_ref[...] = du

    return rowcall(cfg, fn, name,
                   [("row", u, W, 0), ("drow", da, W, 0), ("drow", db, W, 0), ("full", wa), ("full", ba), ("full", wx),
                    ("full", bx), ("full", lam)],
                   [("row", _sds((T, W), f32), W, 0), ("full", _sds(wa.shape, f32)), ("full", _sds(ba.shape, f32)),
                    ("full", _sds(wx.shape, f32)), ("full", _sds(bx.shape, f32)), ("full", _sds(lam.shape, f32))])


def _dir_tile(cfg, d, j):
    rev = jnp.where(j < cfg.cT, cfg.cT - 1 - j, cfg.nT - 1 - (j - cfg.cT))
    return jnp.where(d == 0, j, rev)


def lru_scan(cfg, a, b, name):
    _, T, W = a.shape
    TR, nT = cfg.TR, cfg.nT

    def body(a_ref, b_ref, h_ref, hp_ref, st):
        d, j = pl.program_id(0), pl.program_id(1)

        @pl.when(j == 0)
        def _():
            st[...] = jnp.zeros(st.shape, f32)

        def step(t, h):
            idx = t + d * (TR - 1 - 2 * t)
            hn = a_ref[pl.ds(idx, 1), :] * h + b_ref[pl.ds(idx, 1), :]
            hp_ref[pl.ds(idx, 1), :] = h
            h_ref[pl.ds(idx, 1), :] = hn
            return hn

        st[...] = lax.fori_loop(0, TR, step, st[...])

    spec = pl.BlockSpec((None, TR, W), lambda d, j: (d, _dir_tile(cfg, d, j), 0))
    return pl.pallas_call(body, name=name, grid=(2, nT), in_specs=[spec, spec], out_specs=[spec, spec],
                          out_shape=[_sds(a.shape, f32), _sds(a.shape, f32)], scratch_shapes=[pltpu.VMEM((1, W), f32)],
                          compiler_params=_params(("arbitrary", "arbitrary")))(a, b)


def lru_scan_bwd(cfg, a, hp, dh, name):
    _, T, W = a.shape
    TR, nT = cfg.TR, cfg.nT

    def body(a_ref, hp_ref, dh_ref, da_ref, db_ref, st):
        d, j = pl.program_id(0), pl.program_id(1)

        @pl.when(j == 0)
        def _():
            st[...] = jnp.zeros(st.shape, f32)

        def step(t, c):
            p = TR - 1 - t
            idx = p + d * (TR - 1 - 2 * p)
            g = dh_ref[pl.ds(idx, 1), :] + c
            db_ref[pl.ds(idx, 1), :] = g
            da_ref[pl.ds(idx, 1), :] = g * hp_ref[pl.ds(idx, 1), :]
            return a_ref[pl.ds(idx, 1), :] * g

        st[...] = lax.fori_loop(0, TR, step, st[...])

    spec = pl.BlockSpec((None, TR, W), lambda d, j: (d, _dir_tile(cfg, d, nT - 1 - j), 0))
    dspec = pl.BlockSpec((TR, W), lambda d, j: (_dir_tile(cfg, d, nT - 1 - j), 0))
    return pl.pallas_call(body, name=name, grid=(2, nT), in_specs=[spec, spec, dspec], out_specs=[spec, spec],
                          out_shape=[_sds(a.shape, f32), _sds(a.shape, f32)], scratch_shapes=[pltpu.VMEM((1, W), f32)],
                          compiler_params=_params(("arbitrary", "arbitrary")))(a, hp, dh)


def _lru_out_fn(gl, h0, h1):
    return jax.nn.gelu(gl) * (h0 + h1)


def lru_out_fwd(cfg, z, h, W, name):
    def fn(i, g_ref, h_ref, o_ref):
        o_ref[...] = _lru_out_fn(g_ref[...], h_ref[0], h_ref[1]).astype(bf16)

    return rowcall(cfg, fn, name, [("row", z, W, 0), ("drow", h, W, 0)], [("row", _sds((z.shape[0], W), bf16), W, 0)])[0]


def lru_out_bwd(cfg, z, h, dmix, W, name):
    def fn(i, g_ref, h_ref, d_ref, dg_ref, dh_ref):
        _, vjp = jax.vjp(_lru_out_fn, g_ref[...], h_ref[0], h_ref[1])
        dg, dh0, _ = vjp(d_ref[...])
        dg_ref[...] = dg
        dh_ref[...] = dh0

    T = z.shape[0]
    return rowcall(cfg, fn, name, [("row", z, W, 0), ("drow", h, W, 0), ("row", dmix, W, 0)],
                   [("row", _sds((T, W), f32), W, 0), ("row", _sds((T, W), f32), W, 0)])


def _rot_half(x, cos, sin):
    n = x.shape[1] // 2
    x1, x2 = x[:, :n], x[:, n:]
    return jnp.concatenate([x1 * cos - x2 * sin, x1 * sin + x2 * cos], axis=1)


def _dotf(a, b, ca, cb):
    return lax.dot_general(a, b, (((ca,), (cb,)), ((), ())), preferred_element_type=f32)


@functools.partial(jax.custom_vjp, nondiff_argnums=(2, 3))
def _dotb(a, b, ca, cb):
    return _dotf(a.astype(bf16), b.astype(bf16), ca, cb)


def _dotb_fwd(a, b, ca, cb):
    return _dotb(a, b, ca, cb), (a, b)


def _dotb_bwd(ca, cb, res, ct):
    a, b = res
    a16, b16, ct16 = a.astype(bf16), b.astype(bf16), ct.astype(bf16)
    da = _dotf(ct16, b16, 1, 1 - cb) if ca == 1 else _dotf(b16, ct16, 1 - cb, 1)
    db = _dotf(a16, ct16, 1 - ca, 0) if cb == 0 else _dotf(ct16, a16, 0, 1 - ca)
    return da, db


_dotb.defvjp(_dotb_fwd, _dotb_bwd)


def _ret_chunk(d, q, k, v, s, logit, cos, sin):
    C = q.shape[0]
    lg = -_softplus(-logit)
    qr = _rot_half(q, cos, sin)
    kr = _rot_half(k, cos, sin) * (RET_DK ** -0.5)
    ii = lax.broadcasted_iota(jnp.int32, (C, C), 0)
    jj = lax.broadcasted_iota(jnp.int32, (C, C), 1)
    diff = ((ii - jj) if d == 0 else (jj - ii)).astype(f32)
    intra = jnp.where(diff >= 0, jnp.exp(lg * jnp.maximum(diff, 0.0)), 0.0)
    pos = lax.broadcasted_iota(jnp.int32, (C, 1), 0).astype(f32)
    if d == 0:
        q_dec, k_dec = jnp.exp(lg * (pos + 1.0)), jnp.exp(lg * (C - 1.0 - pos))
    else:
        q_dec, k_dec = jnp.exp(lg * (C - pos)), jnp.exp(lg * pos)
    s_dec = jnp.exp(lg * C)
    scores = _dotb(qr, kr, 1, 1) * intra
    o = _dotb(scores, v, 1, 0) + _dotb(qr * q_dec, s, 1, 0)
    s_new = s * s_dec + _dotb(kr * k_dec, v, 0, 0)
    return o, s_new


def _chunk_cfg(cfg):
    f = cfg.TR // RET_CHUNK
    return RowCfg(RET_CHUNK, cfg.nT * f, cfg.cT * f)


def ret_fwd(cfg, z, logit, cos, sin, H, qcol, name):
    T = z.shape[0]
    cc = _chunk_cfg(cfg)
    C, nC = RET_CHUNK, cc.nT

    def body(q_ref, k_ref, v_ref, lg_ref, cos_ref, sin_ref, o_ref, s_ref, st):
        d, j = pl.program_id(0), pl.program_id(2)

        @pl.when(j == 0)
        def _():
            st[...] = jnp.zeros(st.shape, f32)

        s_ref[...] = st[...]
        for dd in range(2):
            @pl.when(d == dd)
            def _():
                o, s_new = _ret_chunk(dd, q_ref[...], k_ref[...], v_ref[...], st[...], lg_ref[...], cos_ref[...], sin_ref[...])
                o_ref[...] = o
                st[...] = s_new

    tile = lambda d, j: _dir_tile(cc, d, j)
    zq = pl.BlockSpec((C, RET_DK), lambda d, h, j: (tile(d, j), qcol + h))
    zk = pl.BlockSpec((C, RET_DK), lambda d, h, j: (tile(d, j), qcol + H + h))
    zv = pl.BlockSpec((C, RET_DV), lambda d, h, j: (tile(d, j), qcol + 2 * H + h))
    lgs = pl.BlockSpec((None, None, 1, 1), lambda d, h, j: (d, h, 0, 0))
    cs = pl.BlockSpec((C, RET_DK // 2), lambda d, h, j: (tile(d, j), 0))
    o_spec = pl.BlockSpec((None, C, RET_DV), lambda d, h, j: (d, tile(d, j), h))
    s_spec = pl.BlockSpec((None, None, None, RET_DK, RET_DV), lambda d, h, j: (d, h, tile(d, j), 0, 0))
    return pl.pallas_call(
        body, name=name, grid=(2, H, nC), in_specs=[zq, zk, zv, lgs, cs, cs], out_specs=[o_spec, s_spec],
        out_shape=[_sds((2, T, H * RET_DV), f32), _sds((2, H, nC, RET_DK, RET_DV), f32)],
        scratch_shapes=[pltpu.VMEM((RET_DK, RET_DV), f32)],
        compiler_params=_params(("arbitrary", "arbitrary", "arbitrary")))(z, z, z, logit, cos, sin)


def ret_bwd(cfg, z, states, do, logit, cos, sin, H, qcol, name):
    T = z.shape[0]
    cc = _chunk_cfg(cfg)
    C, nC = RET_CHUNK, cc.nT

    def body(q_ref, k_ref, v_ref, s_ref, do_ref, lg_ref, cos_ref, sin_ref, dq_ref, dk_ref, dv_ref, dlg_ref, st):
        d, j = pl.program_id(0), pl.program_id(2)

        @pl.when(j == 0)
        def _():
            st[...] = jnp.zeros(st.shape, f32)
            dlg_ref[...] = jnp.zeros(dlg_ref.shape, f32)

        for dd in range(2):
            @pl.when(d == dd)
            def _():
                fn = lambda q, k, v, s, lg: _ret_chunk(dd, q, k, v, s, lg, cos_ref[...], sin_ref[...])
                _, vjp = jax.vjp(fn, q_ref[...], k_ref[...], v_ref[...], s_ref[...], lg_ref[...])
                dq, dk, dv, ds, dlg = vjp((do_ref[...], st[...]))
                dq_ref[...] = dq
                dk_ref[...] = dk
                dv_ref[...] = dv
                st[...] = ds
                dlg_ref[...] += dlg

    tile = lambda d, j: _dir_tile(cc, d, nC - 1 - j)
    zq = pl.BlockSpec((C, RET_DK), lambda d, h, j: (tile(d, j), qcol + h))
    zk = pl.BlockSpec((C, RET_DK), lambda d, h, j: (tile(d, j), qcol + H + h))
    zv = pl.BlockSpec((C, RET_DV), lambda d, h, j: (tile(d, j), qcol + 2 * H + h))
    s_spec = pl.BlockSpec((None, None, None, RET_DK, RET_DV), lambda d, h, j: (d, h, tile(d, j), 0, 0))
    do_spec = pl.BlockSpec((C, RET_DV), lambda d, h, j: (tile(d, j), h))
    lgs = pl.BlockSpec((None, None, 1, 1), lambda d, h, j: (d, h, 0, 0))
    cs = pl.BlockSpec((C, RET_DK // 2), lambda d, h, j: (tile(d, j), 0))
    g_spec = pl.BlockSpec((None, C, RET_DK), lambda d, h, j: (d, tile(d, j), h))
    gshape = _sds((2, T, H * RET_DK), f32)
    return pl.pallas_call(
        body, name=name, grid=(2, H, nC), in_specs=[zq, zk, zv, s_spec, do_spec, lgs, cs, cs],
        out_specs=[g_spec, g_spec, g_spec, lgs], out_shape=[gshape, gshape, gshape, _sds((2, H, 1, 1), f32)],
        scratch_shapes=[pltpu.VMEM((RET_DK, RET_DV), f32)],
        compiler_params=_params(("arbitrary", "arbitrary", "arbitrary")))(z, z, z, states, do, logit, cos, sin)


def _ret_norm_fn(H, o0, o1, ol, gn):
    o = o0 + o1
    parts = []
    for h in range(H):
        x = o[:, h * RET_DV:(h + 1) * RET_DV]
        mu = jnp.mean(x, axis=-1, keepdims=True)
        var = jnp.mean(jnp.square(x - mu), axis=-1, keepdims=True)
        parts.append((x - mu) * lax.rsqrt(var + EPS))
    return (jnp.concatenate(parts, axis=1) * gn) * jax.nn.silu(ol)


def ret_norm_fwd(cfg, o, z, gn, H, olcol, name):
    RV = H * RET_DV

    def fn(i, o_ref, ol_ref, gn_ref, r_ref):
        r_ref[...] = _ret_norm_fn(H, o_ref[0], o_ref[1], ol_ref[...], gn_ref[...]).astype(bf16)

    return rowcall(cfg, fn, name, [("drow", o, RV, 0), ("row", z, RV, olcol), ("full", gn)],
                   [("row", _sds((z.shape[0], RV), bf16), RV, 0)])[0]


def ret_norm_bwd(cfg, o, z, gn, dmix, H, olcol, dcol, name):
    RV = H * RET_DV
    T = z.shape[0]

    def fn(i, o_ref, ol_ref, gn_ref, d_ref, do_ref, dol_ref, dgn_ref):
        _, vjp = jax.vjp(functools.partial(_ret_norm_fn, H), o_ref[0], o_ref[1], ol_ref[...], gn_ref[...])
        do, _, dol, dgn = vjp(d_ref[...])
        do_ref[...] = do
        dol_ref[...] = dol
        _acc(dgn_ref, dgn, i == 0)

    return rowcall(cfg, fn, name, [("drow", o, RV, 0), ("row", z, RV, olcol), ("full", gn), ("row", dmix, RV, dcol)],
                   [("row", _sds((T, RV), f32), RV, 0), ("row", _sds((T, RV), f32), RV, 0), ("full", _sds((1, RV), f32))])


def _pool_geom(cfg, i, w, L):
    t = (i - cfg.cT) * cfg.TR + lax.broadcasted_iota(jnp.int32, (cfg.TR, 1), 0)
    lo = jnp.clip(t - w // 2, 0, L)
    hi = jnp.clip(t + w // 2, 0, L)
    return (hi - lo).astype(f32)


def _pool_centred(cfg, i, buf, gi, w, L):
    TR, G = cfg.TR, POOL_GROUP
    cols = pl.ds(gi * G, G)
    tot = buf[pl.ds(HALO - w // 2, TR), cols]
    for s in range(-w // 2 + 1, w // 2):
        tot = tot + buf[pl.ds(HALO + s, TR), cols]
    cnt = _pool_geom(cfg, i, w, L)
    return tot / cnt - buf[pl.ds(HALO, TR), cols], cnt


def pool_fwd(cfg, z, pw, ps, name):
    T = z.shape[0]
    TR, cT = cfg.TR, cfg.cT
    P = POOL_GROUP * len(POOL_WINDOWS)
    L = T - cT * TR

    def fn(i, x_ref, xp_ref, xn_ref, pw_ref, ps_ref, o_ref, buf):
        @pl.when(i < cT)
        def _():
            o_ref[...] = jnp.zeros(o_ref.shape, bf16)

        @pl.when(i >= cT)
        def _():
            start, end = _seg_flags(cfg, i)
            _fill_halo(buf, x_ref[...], xp_ref[...], xn_ref[...], start, end, TR)
            outs = []
            for gi, w in enumerate(POOL_WINDOWS):
                m, _ = _pool_centred(cfg, i, buf, gi, w, L)
                outs.append(jnp.dot(m.astype(bf16), pw_ref[gi].astype(bf16), preferred_element_type=f32))
            o_ref[...] = (jnp.concatenate(outs, axis=1) * ps_ref[...]).astype(bf16)

    return rowcall(cfg, fn, name, [("row", z, P, 0), ("prev", z, P, 0), ("next", z, P, 0), ("full", pw), ("full", ps)],
                   [("row", _sds((T, P), bf16), P, 0)], scratch=[pltpu.VMEM((TR + 2 * HALO, P), f32)])[0]


def pool_bwd_a(cfg, z, dmix, pw, ps, name):
    T = z.shape[0]
    TR, cT = cfg.TR, cfg.cT
    G = POOL_GROUP
    P = G * len(POOL_WINDOWS)
    L = T - cT * TR

    def fn(i, x_ref, xp_ref, xn_ref, d_ref, pw_ref, ps_ref, dm_ref, dmn_ref, dpw_ref, dps_ref, buf):
        @pl.when(i == 0)
        def _():
            dpw_ref[...] = jnp.zeros(dpw_ref.shape, f32)
            dps_ref[...] = jnp.zeros(dps_ref.shape, f32)

        @pl.when(i < cT)
        def _():
            dm_ref[...] = jnp.zeros(dm_ref.shape, f32)
            dmn_ref[...] = jnp.zeros(dmn_ref.shape, f32)

        @pl.when(i >= cT)
        def _():
            start, end = _seg_flags(cfg, i)
            _fill_halo(buf, x_ref[...], xp_ref[...], xn_ref[...], start, end, TR)
            dout = d_ref[...]
            dpre = dout * ps_ref[...]
            pres, dms, dmns = [], [], []
            for gi, w in enumerate(POOL_WINDOWS):
                m, cnt = _pool_centred(cfg, i, buf, gi, w, L)
                m_bf = m.astype(bf16)
                w_bf = pw_ref[gi].astype(bf16)
                pres.append(jnp.dot(m_bf, w_bf, preferred_element_type=f32))
                g_bf = dpre[:, gi * G:(gi + 1) * G].astype(bf16)
                dpw_ref[gi] += _dotf(m_bf, g_bf, 0, 0)
                dm = _dotf(g_bf, w_bf, 1, 1)
                dms.append(dm)
                dmns.append(dm / cnt)
            dps_ref[...] += jnp.sum(dout * jnp.concatenate(pres, axis=1), axis=0, keepdims=True)
            dm_ref[...] = jnp.concatenate(dms, axis=1)
            dmn_ref[...] = jnp.concatenate(dmns, axis=1)

    return rowcall(cfg, fn, name,
                   [("row", z, P, 0), ("prev", z, P, 0), ("next", z, P, 0), ("row", dmix, P, 0), ("full", pw), ("full", ps)],
                   [("row", _sds((T, P), f32), P, 0), ("row", _sds((T, P), f32), P, 0), ("full", _sds(pw.shape, f32)),
                    ("full", _sds((1, P), f32))], scratch=[pltpu.VMEM((TR + 2 * HALO, P), f32)])


def pool_bwd_b(cfg, dm, dmn, name):
    T, P = dm.shape
    TR, cT = cfg.TR, cfg.cT
    G = POOL_GROUP

    def fn(i, dm_ref, c_ref, p_ref, n_ref, dx_ref, buf):
        start, end = _seg_flags(cfg, i)
        _fill_halo(buf, c_ref[...], p_ref[...], n_ref[...], start, end, TR)
        outs = []
        for gi, w in enumerate(POOL_WINDOWS):
            cols = pl.ds(gi * G, G)
            tot = buf[pl.ds(HALO + w // 2, TR), cols]
            for s in range(-w // 2 + 1, w // 2):
                tot = tot + buf[pl.ds(HALO + s, TR), cols]
            outs.append(tot)
        dx_ref[...] = jnp.concatenate(outs, axis=1) - dm_ref[...]

    return rowcall(cfg, fn, name, [("row", dm, P, 0), ("row", dmn, P, 0), ("prev", dmn, P, 0), ("next", dmn, P, 0)],
                   [("row", _sds((T, P), f32), P, 0)], scratch=[pltpu.VMEM((TR + 2 * HALO, P), f32)])[0]


def _swap_halves(x):
    return pltpu.roll(x, HEAD_DIM // 2, 1)


def _headnorm(x, g):
    return _rms(x) * g


def att_prep(cfg, z, qg, kg, cosf, sinf, nq, name):
    T = z.shape[0]
    U = z.shape[1] // (nq + 3)
    nh = U // HEAD_DIM

    def fn(i, *refs):
        q_refs = refs[:nq]
        k_ref, v_ref, qg_ref, kg_ref, cos_ref, sin_ref, qn_ref, kn_ref, vb_ref = refs[nq:]
        cosv, sinv = cos_ref[...], sin_ref[...]

        def heads(x, g):
            outs = []
            for h in range(nh):
                y = _headnorm(x[:, h * HEAD_DIM:(h + 1) * HEAD_DIM], g)
                outs.append(y * cosv + _swap_halves(y) * sinv)
            return jnp.concatenate(outs, axis=1)

        qn_ref[...] = jnp.concatenate([heads(r[...], qg_ref[...]) for r in q_refs], axis=1).astype(bf16)
        kn_ref[...] = heads(k_ref[...], kg_ref[...]).astype(bf16)
        vb_ref[...] = v_ref[...].astype(bf16)

    ins = [("row", z, U, 1 + n) for n in range(nq)] + [("row", z, U, nq + 1), ("row", z, U, nq + 2), ("full", qg),
                                                       ("full", kg), ("row", cosf, HEAD_DIM, 0), ("row", sinf, HEAD_DIM, 0)]
    return rowcall(cfg, fn, name, ins, [("row", _sds((T, nq * U), bf16), nq * U, 0), ("row", _sds((T, U), bf16), U, 0),
                                        ("row", _sds((T, U), bf16), U, 0)])


def att_prep_bwd(cfg, z, qg, kg, cosf, sinf, dqn, dkn, dvb, dxpool, nq, name):
    T = z.shape[0]
    U = z.shape[1] // (nq + 3)
    nh = U // HEAD_DIM
    cT = cfg.cT

    def fn(i, *refs):
        q_refs = refs[:nq]
        (k_ref, qg_ref, kg_ref, cos_ref, sin_ref, dqn_ref, dkn_ref, dvb_ref, dxp_ref, dz_ref, dqg_ref, dkg_ref) = refs[nq:]
        cosv, sinv = cos_ref[...], sin_ref[...]

        @pl.when(i == 0)
        def _():
            dqg_ref[...] = jnp.zeros(dqg_ref.shape, f32)
            dkg_ref[...] = jnp.zeros(dkg_ref.shape, f32)

        def heads_bwd(x, g, dy, dg_ref):
            outs = []
            for h in range(nh):
                sl = slice(h * HEAD_DIM, (h + 1) * HEAD_DIM)
                d = dy[:, sl]
                dn = d * cosv + _swap_halves(d * sinv)
                _, vjp = jax.vjp(_headnorm, x[:, sl], g)
                dx, dg = vjp(dn)
                dg_ref[...] += dg
                outs.append(dx)
            return jnp.concatenate(outs, axis=1)

        dk = heads_bwd(k_ref[...], kg_ref[...], dkn_ref[...], dkg_ref)
        tail = [dk.astype(bf16), dvb_ref[...].astype(bf16)]

        @pl.when(i < cT)
        def _():
            zeros = jnp.zeros((cfg.TR, (nq + 1) * U), bf16)
            dz_ref[...] = jnp.concatenate([zeros] + tail, axis=1)

        @pl.when(i >= cT)
        def _():
            dq = [heads_bwd(r[...], qg_ref[...], dqn_ref[:, n * U:(n + 1) * U], dqg_ref) for n, r in enumerate(q_refs)]
            dz_ref[...] = jnp.concatenate([dxp_ref[...].astype(bf16)] + [t.astype(bf16) for t in dq] + tail, axis=1)

    ins = ([("row", z, U, 1 + n) for n in range(nq)] +
           [("row", z, U, nq + 1), ("full", qg), ("full", kg), ("row", cosf, HEAD_DIM, 0), ("row", sinf, HEAD_DIM, 0),
            ("row", dqn, nq * U, 0), ("row", dkn, U, 0), ("row", dvb, U, 0), ("row", dxpool, U, 0)])
    W = (nq + 3) * U
    return rowcall(cfg, fn, name, ins, [("row", _sds((T, W), bf16), W, 0), ("full", _sds((1, HEAD_DIM), f32)),
                                        ("full", _sds((1, HEAD_DIM), f32))])


def _stack_heads(x, n):
    return jnp.concatenate([x[:, h * HEAD_DIM:(h + 1) * HEAD_DIM] for h in range(n)], axis=0)


def _unstack_heads(x, n):
    rows = x.shape[0] // n
    return jnp.concatenate([x[h * rows:(h + 1) * rows] for h in range(n)], axis=1)


def _att_tiles(cfg, T):
    tq = cfg.TR
    tk = _pick(T, (4224, 2816, 1408, 768, 512, 256, 128))
    return tq, tk, (T - cfg.cT * cfg.TR) // tq, T // tk


LOG2E = 1.4426950408889634


def att_fwd(cfg, qn, kn, vb, nq, name):
    T, U = kn.shape
    KV = U // HEAD_DIM
    tq, tk, nQ, nK = _att_tiles(cfg, T)
    scale = HEAD_DIM ** -0.5
    c2 = scale * LOG2E
    R = nq * tq

    def body(q_ref, k_ref, v_ref, o_ref, lse_ref, *scratch):
        ik = pl.program_id(2)
        m_sc, l_sc, acc = scratch[:nq], scratch[nq:2 * nq], scratch[2 * nq:]

        @pl.when(ik == 0)
        def _():
            for h in range(nq):
                m_sc[h][...] = jnp.full(m_sc[h].shape, -jnp.inf, f32)
                l_sc[h][...] = jnp.zeros(l_sc[h].shape, f32)
                acc[h][...] = jnp.zeros(acc[h].shape, f32)

        k, v = k_ref[...], v_ref[...]
        for h in range(nq):
            s = _dotf(q_ref[:, h * HEAD_DIM:(h + 1) * HEAD_DIM], k, 1, 1)
            m_old = m_sc[h][...]
            m_new = jnp.maximum(m_old, jnp.max(s, axis=-1, keepdims=True))
            alpha = jnp.exp2((m_old - m_new) * c2)
            p = jnp.exp2((s - m_new) * c2)
            l_sc[h][...] = alpha * l_sc[h][...] + jnp.sum(p, axis=-1, keepdims=True)
            acc[h][...] = alpha * acc[h][...] + jnp.dot(p.astype(bf16), v, preferred_element_type=f32)
            m_sc[h][...] = m_new

        @pl.when(ik == nK - 1)
        def _():
            o_ref[...] = jnp.concatenate([acc[h][...] / l_sc[h][...] for h in range(nq)], axis=1)
            lse_ref[...] = jnp.concatenate([m_sc[h][...] * scale + jnp.log(l_sc[h][...]) for h in range(nq)], axis=0)

    W = nq * HEAD_DIM
    q_spec = pl.BlockSpec((tq, W), lambda h, i, k: (i + cfg.cT, h))
    kv_spec = pl.BlockSpec((tk, HEAD_DIM), lambda h, i, k: (k, h))
    lse_spec = pl.BlockSpec((None, None, R, 1), lambda h, i, k: (h, i, 0, 0))
    col = [pltpu.VMEM((tq, 1), f32)] * nq
    return pl.pallas_call(
        body, name=name, grid=(KV, nQ, nK), in_specs=[q_spec, kv_spec, kv_spec], out_specs=[q_spec, lse_spec],
        out_shape=[_sds((T, nq * U), f32), _sds((KV, nQ, R, 1), f32)],
        scratch_shapes=col + col + [pltpu.VMEM((tq, HEAD_DIM), f32)] * nq,
        compiler_params=_params(("arbitrary", "arbitrary", "arbitrary")))(qn, kn, vb)


def att_bwd_dq(cfg, qn, kn, vb, o, lse, do, nq, name):
    T, U = kn.shape
    KV = U // HEAD_DIM
    tq, tk, nQ, nK = _att_tiles(cfg, T)
    scale = HEAD_DIM ** -0.5
    c2 = scale * LOG2E
    R = nq * tq
    W = nq * HEAD_DIM

    def body(q_ref, k_ref, v_ref, o_ref, lse_ref, do_ref, dq_ref, acc, dl):
        ik = pl.program_id(2)

        @pl.when(ik == 0)
        def _():
            acc[...] = jnp.zeros(acc.shape, f32)
            dl[...] = jnp.sum(_stack_heads(do_ref[...] * o_ref[...], nq), axis=-1, keepdims=True)

        k, v = k_ref[...], v_ref[...]
        for h in range(nq):
            rows = pl.ds(h * tq, tq)
            cols = slice(h * HEAD_DIM, (h + 1) * HEAD_DIM)
            s = _dotf(q_ref[:, cols], k, 1, 1)
            p = jnp.exp2(s * c2 - lse_ref[rows, :] * LOG2E)
            dp = _dotf(do_ref[:, cols].astype(bf16), v, 1, 1)
            ds = (p * (dp - dl[rows, :]) * scale).astype(bf16)
            acc[rows, :] += jnp.dot(ds, k, preferred_element_type=f32)

        @pl.when(ik == nK - 1)
        def _():
            dq_ref[...] = _unstack_heads(acc[...], nq)

    q_spec = pl.BlockSpec((tq, W), lambda h, i, k: (i + cfg.cT, h))
    kv_spec = pl.BlockSpec((tk, HEAD_DIM), lambda h, i, k: (k, h))
    lse_spec = pl.BlockSpec((None, None, R, 1), lambda h, i, k: (h, i, 0, 0))
    return pl.pallas_call(
        body, name=name, grid=(KV, nQ, nK), in_specs=[q_spec, kv_spec, kv_spec, q_spec, lse_spec, q_spec], out_specs=q_spec,
        out_shape=_sds((T, nq * U), f32), scratch_shapes=[pltpu.VMEM((R, HEAD_DIM), f32), pltpu.VMEM((R, 1), f32)],
        compiler_params=_params(("arbitrary", "arbitrary", "arbitrary")))(qn, kn, vb, o, lse, do)


def att_bwd_dkv(cfg, qn, kn, vb, o, lse, do, nq, name):
    T, U = kn.shape
    KV = U // HEAD_DIM
    tq, tk, nQ, nK = _att_tiles(cfg, T)
    scale = HEAD_DIM ** -0.5
    c2 = scale * LOG2E
    R = nq * tq
    W = nq * HEAD_DIM

    def body(q_ref, k_ref, v_ref, o_ref, lse_ref, do_ref, dk_ref, dv_ref, dk_acc, dv_acc):
        iq = pl.program_id(2)

        @pl.when(iq == 0)
        def _():
            dk_acc[...] = jnp.zeros(dk_acc.shape, f32)
            dv_acc[...] = jnp.zeros(dv_acc.shape, f32)

        k, v = k_ref[...], v_ref[...]
        for h in range(nq):
            rows = pl.ds(h * tq, tq)
            cols = slice(h * HEAD_DIM, (h + 1) * HEAD_DIM)
            qh = q_ref[:, cols]
            doh = do_ref[:, cols]
            dl = jnp.sum(doh * o_ref[:, cols], axis=-1, keepdims=True)
            p = jnp.exp2(_dotf(qh, k, 1, 1) * c2 - lse_ref[rows, :] * LOG2E)
            do_bf = doh.astype(bf16)
            dv_acc[...] += _dotf(p.astype(bf16), do_bf, 0, 0)
            dp = _dotf(do_bf, v, 1, 1)
            ds = (p * (dp - dl) * scale).astype(bf16)
            dk_acc[...] += _dotf(ds, qh, 0, 0)

        @pl.when(iq == nQ - 1)
        def _():
            dk_ref[...] = dk_acc[...]
            dv_ref[...] = dv_acc[...]

    q_spec = pl.BlockSpec((tq, W), lambda h, k, i: (i + cfg.cT, h))
    kv_spec = pl.BlockSpec((tk, HEAD_DIM), lambda h, k, i: (k, h))
    lse_spec = pl.BlockSpec((None, None, R, 1), lambda h, k, i: (h, i, 0, 0))
    return pl.pallas_call(
        body, name=name, grid=(KV, nK, nQ), in_specs=[q_spec, kv_spec, kv_spec, q_spec, lse_spec, q_spec],
        out_specs=[kv_spec, kv_spec], out_shape=[_sds((T, U), f32), _sds((T, U), f32)],
        scratch_shapes=[pltpu.VMEM((tk, HEAD_DIM), f32), pltpu.VMEM((tk, HEAD_DIM), f32)],
        compiler_params=_params(("arbitrary", "arbitrary", "arbitrary")))(qn, kn, vb, o, lse, do)


def od_mix(cfg, pooled, o, name):
    T, P = pooled.shape
    QW = o.shape[1]
    cT = cfg.cT

    def fn(i, p_ref, o_ref, m_ref):
        @pl.when(i < cT)
        def _():
            m_ref[...] = jnp.zeros(m_ref.shape, bf16)

        @pl.when(i >= cT)
        def _():
            m_ref[...] = jnp.concatenate([p_ref[...], o_ref[...].astype(bf16)], axis=1)

    return rowcall(cfg, fn, name, [("row", pooled, P, 0), ("row", o, QW, 0)], [("row", _sds((T, P + QW), bf16), P + QW, 0)])[0]


def ev_mix(cfg, lru, ret, name):
    T, W = lru.shape
    RV = ret.shape[1]

    def fn(i, a_ref, b_ref, m_ref):
        m_ref[...] = jnp.concatenate([a_ref[...], b_ref[...]], axis=1)

    return rowcall(cfg, fn, name, [("row", lru, W, 0), ("row", ret, RV, 0)], [("row", _sds((T, W + RV), bf16), W + RV, 0)])[0]


def ev_dz_pack(cfg, dgl, dr, dq, dk, dv, dol, name):
    T, W = dgl.shape
    RV = dol.shape[1]
    width = 2 * W + 4 * RV

    def fn(i, g_ref, r_ref, q_ref, k_ref, v_ref, o_ref, dz_ref):
        parts = [g_ref[...], r_ref[...], q_ref[0] + q_ref[1], k_ref[0] + k_ref[1], v_ref[0] + v_ref[1], o_ref[...]]
        dz_ref[...] = jnp.concatenate([p.astype(bf16) for p in parts], axis=1)

    return rowcall(cfg, fn, name, [("row", dgl, W, 0), ("row", dr, W, 0), ("drow", dq, RV, 0), ("drow", dk, RV, 0),
                                   ("drow", dv, RV, 0), ("row", dol, RV, 0)], [("row", _sds((T, width), bf16), width, 0)])[0]


def loss_fwd_bwd(cfg, xf, target, name):
    T, D = xf.shape
    TR, cT = cfg.TR, cfg.cT

    def body(x_ref, t_ref, sq_ref, dx_ref):
        i = pl.program_id(0)

        @pl.when(i == 0)
        def _():
            sq_ref[...] = jnp.zeros(sq_ref.shape, f32)

        @pl.when(i < cT)
        def _():
            dx_ref[...] = jnp.zeros(dx_ref.shape, f32)

        @pl.when(i >= cT)
        def _():
            diff = x_ref[...] - t_ref[...]
            sq_ref[...] += jnp.sum(diff * diff, axis=0, keepdims=True)
            dx_ref[...] = diff / D

    row = pl.BlockSpec((TR, D), lambda i: (i, 0))
    trow = pl.BlockSpec((TR, D), lambda i: (jnp.maximum(i - cT, 0), 0))
    return pl.pallas_call(body, name=name, grid=(cfg.nT,), in_specs=[row, trow],
                          out_specs=[pl.BlockSpec((1, D), lambda i: (0, 0)), row],
                          out_shape=[_sds((1, D), f32), _sds((T, D), f32)], compiler_params=_params(("arbitrary",)))(xf, target)


MOD_ROWS = 16


def mod_fwd(s16, mod_w, name):
    nL, D, C4 = mod_w.shape
    tc = _pick(C4, (512, 256, 128))

    def body(s_ref, w_ref, o_ref):
        o_ref[...] = jnp.dot(s_ref[...], w_ref[...], precision=lax.Precision.HIGHEST, preferred_element_type=f32)

    return pl.pallas_call(
        body, name=name, grid=(nL, C4 // tc),
        in_specs=[pl.BlockSpec((MOD_ROWS, D), lambda l, j: (0, 0)), pl.BlockSpec((None, D, tc), lambda l, j: (l, 0, j))],
        out_specs=pl.BlockSpec((None, MOD_ROWS, tc), lambda l, j: (l, 0, j)), out_shape=_sds((nL, MOD_ROWS, C4), f32),
        compiler_params=_params(("arbitrary", "arbitrary")))(s16, mod_w)


def mod_bwd(s16, dm16, mod_w, name):
    nL, D, C4 = mod_w.shape
    tc = _pick(C4, (512, 256, 128))
    half = MOD_ROWS // 2

    def body(s_ref, d_ref, w_ref, g_ref, dc_ref):
        first = jnp.logical_and(pl.program_id(0) == 0, pl.program_id(1) == 0)
        g_ref[...] = lax.dot_general(s_ref[...], d_ref[...], (((0,), (0,)), ((), ())), precision=lax.Precision.HIGHEST,
                                     preferred_element_type=f32)
        part = lax.dot_general(d_ref[...], w_ref[...], (((1,), (1,)), ((), ())), precision=lax.Precision.HIGHEST,
                               preferred_element_type=f32)
        _acc(dc_ref, jnp.sum(part[half:], axis=0, keepdims=True), first)

    return pl.pallas_call(
        body, name=name, grid=(nL, C4 // tc),
        in_specs=[pl.BlockSpec((MOD_ROWS, D), lambda l, j: (0, 0)), pl.BlockSpec((None, MOD_ROWS, tc), lambda l, j: (l, 0, j)),
                  pl.BlockSpec((None, D, tc), lambda l, j: (l, 0, j))],
        out_specs=[pl.BlockSpec((None, D, tc), lambda l, j: (l, 0, j)), pl.BlockSpec((1, D), lambda l, j: (0, 0))],
        out_shape=[_sds((nL, D, C4), f32), _sds((1, D), f32)],
        compiler_params=_params(("arbitrary", "arbitrary")))(s16, dm16, mod_w)


def _as2d(a):
    return a.reshape(-1, a.shape[-1])


ELEMENTWISE_VMEM = 24 * 1024 * 1024


def _tiles2d(shape, n_arrays):
    R, C = shape
    tc = _pick(C, (1536, 1408, 1024, 768, 512, 256, 128))
    fits = [t for t in (512, 256, 128, 64, 32, 16, 8) if R % t == 0 and t * tc * 4 * 2 * n_arrays <= ELEMENTWISE_VMEM]
    return (fits[0] if fits else R), tc


def cast_bf16(a, name):
    a2 = _as2d(a)
    tr, tc = _tiles2d(a2.shape, 2)

    def body(a_ref, o_ref):
        o_ref[...] = a_ref[...].astype(bf16)

    spec = pl.BlockSpec((tr, tc), lambda i, j: (i, j))
    out = pl.pallas_call(body, name=name, grid=(a2.shape[0] // tr, a2.shape[1] // tc), in_specs=[spec], out_specs=spec,
                         out_shape=_sds(a2.shape, bf16), compiler_params=_params(("arbitrary", "arbitrary")))(a2)
    return out.reshape(a.shape)


def sum_leading(a, name, *, into=None, full_shape=None, widx=()):
    n = a.shape[0]
    a3 = a.reshape(n, -1, a.shape[-1])
    tr, tc = _tiles2d(a3.shape[1:], n + 1)

    def body(a_ref, *rest):
        o_ref = rest[-1]
        tot = a_ref[0].astype(f32)
        for k in range(1, n):
            tot = tot + a_ref[k].astype(f32)
        o_ref[...] = tot

    grid = (a3.shape[1] // tr, a3.shape[2] // tc)
    in_specs = [pl.BlockSpec((n, tr, tc), lambda i, j: (0, i, j))]
    args = [a3]
    if not widx:
        out = pl.pallas_call(body, name=name, grid=grid, in_specs=in_specs,
                             out_specs=pl.BlockSpec((tr, tc), lambda i, j: (i, j)), out_shape=_sds(a3.shape[1:], f32),
                             compiler_params=_params(("arbitrary", "arbitrary")))(*args)
        return out.reshape(a.shape[1:])
    lead = tuple(full_shape[:len(widx)])
    flat = lead + tuple(a3.shape[1:])
    aliases = {}
    if into is not None:
        in_specs.append(pl.BlockSpec(memory_space=pl.ANY))
        args.append(into.reshape(flat))
        aliases = {1: 0}
    out = pl.pallas_call(body, name=name, grid=grid, in_specs=in_specs,
                         out_specs=pl.BlockSpec((None,) * len(widx) + (tr, tc), lambda i, j: tuple(widx) + (i, j)),
                         out_shape=_sds(flat, f32), input_output_aliases=aliases,
                         compiler_params=_params(("arbitrary", "arbitrary")))(*args)
    return out.reshape(full_shape)


def adamw(w, m, v, g_parts, name):
    w2, m2, v2 = _as2d(w), _as2d(m), _as2d(v)
    parts = [_as2d(p) for p in g_parts]
    tr, tc = _tiles2d(w2.shape, 7 + len(parts))
    npart = len(parts)

    def body(*refs):
        w_ref, m_ref, v_ref = refs[:3]
        p_refs = refs[3:3 + npart]
        g_ref, d_ref, nm_ref, nv_ref = refs[3 + npart:]
        g = p_refs[0][...]
        for p in p_refs[1:]:
            g = g + p[...]
        mn = ADAM_B1 * m_ref[...] + (1.0 - ADAM_B1) * g
        vn = ADAM_B2 * v_ref[...] + (1.0 - ADAM_B2) * jnp.square(g)
        m_hat = mn / (1.0 - ADAM_B1 ** ADAM_STEP)
        v_hat = vn / (1.0 - ADAM_B2 ** ADAM_STEP)
        g_ref[...] = g
        d_ref[...] = -ADAM_LR * (m_hat / (jnp.sqrt(v_hat) + ADAM_EPS) + ADAM_WD * w_ref[...])
        nm_ref[...] = mn
        nv_ref[...] = vn

    spec = pl.BlockSpec((tr, tc), lambda i, j: (i, j))
    outs = pl.pallas_call(body, name=name, grid=(w2.shape[0] // tr, w2.shape[1] // tc), in_specs=[spec] * (3 + npart),
                          out_specs=[spec] * 4, out_shape=[_sds(w2.shape, f32)] * 4,
                          compiler_params=_params(("arbitrary", "arbitrary")))(w2, m2, v2, *parts)
    return [o.reshape(w.shape) for o in outs]


def all_gather_small(a, name):
    R, C = a.shape

    def body(a_ref, out_ref, send_sems, recv_sems, local_sem):
        x, y, c = _coords()
        me = 4 * x + 2 * y + c
        mine = pltpu.make_async_copy(a_ref, out_ref.at[me], local_sem)
        mine.start()
        copies = []
        for k in range(1, N_DEV):
            kx, ky, kc = (k >> 2) & 1, (k >> 1) & 1, k & 1
            peer = (_flip(x, kx), _flip(y, ky), _flip(c, kc))
            cp = pltpu.make_async_remote_copy(src_ref=a_ref, dst_ref=out_ref.at[me], send_sem=send_sems.at[k - 1],
                                              recv_sem=recv_sems.at[k - 1], device_id=peer, device_id_type=MESH)
            cp.start()
            copies.append((cp, 4 * peer[0] + 2 * peer[1] + peer[2], peer))
        for k, (cp, pidx, peer) in enumerate(copies):
            pltpu.make_async_remote_copy(src_ref=a_ref, dst_ref=out_ref.at[pidx], send_sem=send_sems.at[k],
                                         recv_sem=recv_sems.at[k], device_id=peer, device_id_type=MESH).wait_recv()
        for cp, _, _ in copies:
            cp.wait_send()
        mine.wait()

    return pl.pallas_call(
        body, name=name, out_shape=_sds((N_DEV, R, C), f32),
        in_specs=[pl.BlockSpec(memory_space=pltpu.VMEM)], out_specs=pl.BlockSpec(memory_space=pltpu.VMEM),
        scratch_shapes=[pltpu.SemaphoreType.DMA((N_DEV - 1,)), pltpu.SemaphoreType.DMA((N_DEV - 1,)), pltpu.SemaphoreType.DMA],
        compiler_params=pltpu.CompilerParams(vmem_limit_bytes=VMEM_LIMIT))(a)


def swap_with_sibling(parts, name):
    n = len(parts)

    def body(*refs):
        in_refs, out_refs = refs[:n], refs[n:2 * n]
        send_sems, recv_sems = refs[2 * n:]
        x, y, c = _coords()
        sends = []
        for w in range(n):
            cp = pltpu.make_async_remote_copy(src_ref=in_refs[w], dst_ref=out_refs[w], send_sem=send_sems.at[w],
                                              recv_sem=recv_sems.at[w], device_id=(x, y, 1 - c), device_id_type=MESH)
            cp.start()
            sends.append(cp)
        for cp in sends:
            cp.wait_recv()
        for cp in sends:
            cp.wait_send()

    hbm = pl.BlockSpec(memory_space=pl.ANY)
    return pl.pallas_call(
        body, name=name, out_shape=[_sds(a.shape, a.dtype) for a in parts], in_specs=[hbm] * n, out_specs=[hbm] * n,
        scratch_shapes=[pltpu.SemaphoreType.DMA((n,)), pltpu.SemaphoreType.DMA((n,))],
        )(*parts)


def even_fwd(cfg, x, p, tag, host=None):
    W, H = p["W"], p["H"]
    got = {}
    h = pre_fwd(cfg, x, p["g_pre"], p["shift"], p["scale"], tag + "_pre")
    z = _hosted(host, "in", got, lambda cm: matmul("v1", h, p["w_in"], comm=cm, name=tag + "_in"))
    u = conv_fwd(cfg, z, p["conv_w"], p["conv_b"], W, tag + "_conv")
    a, b = lru_coef_fwd(cfg, u, p["wa"], p["ba"], p["wx"], p["bx"], p["lam"], tag + "_coef")
    hh, hp = lru_scan(cfg, a, b, tag + "_scan")
    lru = lru_out_fwd(cfg, z, hh, W, tag + "_lruout")
    qcol = 2 * W // RET_DK
    o, st = ret_fwd(cfg, z, p["logit"], p["cos1"], p["sin1"], H, qcol, tag + "_ret")
    olcol = (2 * W + 3 * H * RET_DK) // (H * RET_DV)
    ret = ret_norm_fwd(cfg, o, z, p["gn"], H, olcol, tag + "_retnorm")
    mix = ev_mix(cfg, lru, ret, tag + "_mix")
    y = _hosted(host, "out", got, lambda cm: matmul("v2", mix, p["w_out"], comm=cm, name=tag + "_out"))
    xo = post_fwd(cfg, x, y, p["g_post"], p["gate"], 1.0, tag + "_post")
    return xo, (x, h, z, u, a, hh, hp, o, st, mix, y, olcol, qcol), got


def even_bwd(cfg, dX, saved, p, tag):
    x, h, z, u, a, hh, hp, o, st, mix, y, olcol, qcol = saved
    W, H = p["W"], p["H"]
    dy, dg_post, dgate = post_bwd(cfg, dX, y, p["g_post"], p["gate"], 1.0, tag + "_postb")
    dmix = matmul("v4", dy, p["w_out"], name=tag + "_dmix")
    g_out = matmul("v6", mix, dy, gshape=p["w_out"].shape, out_dtype=bf16, name=tag + "_gwout")
    dgl, dhs = lru_out_bwd(cfg, z, hh, dmix, W, tag + "_lruoutb")
    da, db = lru_scan_bwd(cfg, a, hp, dhs, tag + "_scanb")
    du, dwa, dba, dwx, dbx, dlam = lru_coef_bwd(cfg, u, da, db, p["wa"], p["ba"], p["wx"], p["bx"], p["lam"], tag + "_coefb")
    dr, dcw, dcb = conv_bwd(cfg, z, du, p["conv_w"], W, tag + "_convb")
    do, dol, dgn = ret_norm_bwd(cfg, o, z, p["gn"], dmix, H, olcol, W // (H * RET_DV), tag + "_retnormb")
    dq, dk, dv, dlg = ret_bwd(cfg, z, st, do, p["logit"], p["cos1"], p["sin1"], H, qcol, tag + "_retb")
    dz = ev_dz_pack(cfg, dgl, dr, dq, dk, dv, dol, tag + "_dz")
    g_in, r_out = matmul("v5", h, dz, gshape=p["w_in"].shape, out_dtype=bf16, comm=Comm("scatter", [(g_out, ())]),
                         name=tag + "_gwin")
    dh, r_in = matmul("v3", dz, p["w_in"], comm=Comm("scatter", [(g_in, ())]), name=tag + "_dh")
    dX, dg_pre, dshift, dscale = pre_bwd(cfg, x, p["g_pre"], p["shift"], p["scale"], dh, dX, tag + "_preb")
    pg = dict(g_pre=dg_pre, g_post=dg_post, shift=dshift, scale=dscale, gate=dgate, conv_w=dcw, conv_b=dcb, wa=dwa,
              ba=dba, wx=dwx, bx=dbx, lam=dlam, logit=dlg, gn=dgn)
    return dX, pg, dict(w_in=r_in[0], w_out=r_out[0])


def odd_fwd(cfg, x, p, tag, host=None):
    nq = p["nq"]
    got = {}
    h = pre_fwd(cfg, x, p["g_pre"], p["shift"], p["scale"], tag + "_pre")
    z = _hosted(host, "in", got, lambda cm: matmul("v1", h, p["w_in"], comm=cm, name=tag + "_in"))
    pooled = pool_fwd(cfg, z, p["pool_w"], p["pool_scale"], tag + "_pool")
    qn, kn, vb = att_prep(cfg, z, p["qg"], p["kg"], p["cosf"], p["sinf"], nq, tag + "_prep")
    o, lse = att_fwd(cfg, qn, kn, vb, nq, tag + "_att")
    mix = od_mix(cfg, pooled, o, tag + "_mix")
    y = matmul("v2", mix, p["w_out"], name=tag + "_out")
    xo = post_fwd(cfg, x, y, p["g_post"], p["gate"], 1.0, tag + "_post")
    return xo, (x, h, z, qn, kn, vb, o, lse, mix, y), got


def odd_bwd(cfg, dX, saved, p, tag):
    x, h, z, qn, kn, vb, o, lse, mix, y = saved
    nq = p["nq"]
    U = kn.shape[1]
    dy, dg_post, dgate = post_bwd(cfg, dX, y, p["g_post"], p["gate"], 1.0, tag + "_postb")
    dmix = matmul("v4", dy, p["w_out"], name=tag + "_dmix")
    g_out = matmul("v6", mix, dy, gshape=p["w_out"].shape, out_dtype=bf16, name=tag + "_gwout")
    dm, dmn, dpw, dps = pool_bwd_a(cfg, z, dmix, p["pool_w"], p["pool_scale"], tag + "_poolb")
    dxp = pool_bwd_b(cfg, dm, dmn, tag + "_poolb2")
    do = dmix[:, U:]
    dqn = att_bwd_dq(cfg, qn, kn, vb, o, lse, do, nq, tag + "_attdq")
    dkn, dvb = att_bwd_dkv(cfg, qn, kn, vb, o, lse, do, nq, tag + "_attdkv")
    dz, dqg, dkg = att_prep_bwd(cfg, z, p["qg"], p["kg"], p["cosf"], p["sinf"], dqn, dkn, dvb, dxp, nq, tag + "_prepb")
    g_in, r_out = matmul("v5", h, dz, gshape=p["w_in"].shape, out_dtype=bf16, comm=Comm("scatter", [(g_out, ())]),
                         name=tag + "_gwin")
    dh, r_in = matmul("v3", dz, p["w_in"], comm=Comm("scatter", [(g_in, ())]), name=tag + "_dh")
    dX, dg_pre, dshift, dscale = pre_bwd(cfg, x, p["g_pre"], p["shift"], p["scale"], dh, dX, tag + "_preb")
    pg = dict(g_pre=dg_pre, g_post=dg_post, shift=dshift, scale=dscale, gate=dgate, pool_w=dpw, pool_scale=dps, qg=dqg, kg=dkg)
    return dX, pg, dict(w_in=r_in[0], w_out=r_out[0])


WEIGHT_NAMES = ("c_ctx", "mod_w", "mod_b", "norm_pre", "norm_post", "ffn_gate", "ffn_up", "ffn_down", "ev_w_in", "ev_w_out",
                "lru_conv_w", "lru_conv_b", "lru_wa", "lru_ba", "lru_wx", "lru_bx", "lru_lambda", "ret_decay_logit", "ret_gn",
                "od_w_in", "od_w_out", "pool_w", "pool_scale", "q_norm", "k_norm")
BIG = ("ffn_gate", "ffn_up", "ffn_down", "ev_w_in", "ev_w_out", "od_w_in", "od_w_out")
SMALL_SHARDED = ("norm_pre", "norm_post", "lru_conv_w", "lru_ba", "lru_bx", "lru_lambda", "pool_scale")
SMALL_REPL = ("mod_b", "lru_conv_b", "lru_wa", "lru_wx", "ret_decay_logit", "ret_gn", "pool_w", "q_norm", "k_norm")
LANES = 128


PACK_ROWS = 512


def _rows_of(n):
    return -(-n // (8 * LANES)) * 8


def _pack(arrs):
    rows = []
    for a in arrs:
        flat = a.reshape(-1)
        rows.append(jnp.pad(flat, (0, _rows_of(flat.shape[0]) * LANES - flat.shape[0])).reshape(-1, LANES))
    total = sum(r.shape[0] for r in rows)
    rows.append(jnp.zeros(((-total) % PACK_ROWS, LANES), f32))
    return jnp.concatenate(rows), None


def _unpack(packed, shapes, lead=()):
    out, pos = [], 0
    for shp in shapes:
        n = math.prod(shp)
        r = _rows_of(n)
        piece = packed[..., pos:pos + r, :].reshape(lead + (r * LANES,))
        out.append(piece[..., :n].reshape(lead + tuple(shp)))
        pos += r
    return out


def _unshard(g):
    return jnp.moveaxis(g, 0, -2).reshape(g.shape[1:-1] + (g.shape[0] * g.shape[-1],))


def _rope_tables(S, Lc):
    n_r = RET_DK // 2
    f_r = RET_THETA ** (-jnp.arange(n_r, dtype=f32) / n_r)
    ang1 = jnp.arange(S, dtype=f32)[:, None] * f_r
    rows = S // GRID_W
    row = jnp.repeat(jnp.arange(rows, dtype=f32), GRID_W)
    col = jnp.tile(jnp.arange(GRID_W, dtype=f32), rows)
    n_ax = HEAD_DIM // 4
    f_ax = ROPE_THETA ** (-jnp.arange(n_ax, dtype=f32) / n_ax)
    ang2 = jnp.concatenate([row[:, None] * f_ax, col[:, None] * f_ax], axis=-1)
    cos2, sin2 = jnp.cos(ang2), jnp.sin(ang2)
    ones = lambda n: jnp.ones((Lc, n), f32)
    zeros = lambda n: jnp.zeros((Lc, n), f32)
    cos1 = jnp.concatenate([ones(n_r), jnp.cos(ang1)])
    sin1 = jnp.concatenate([zeros(n_r), jnp.sin(ang1)])
    cosf = jnp.concatenate([ones(HEAD_DIM), jnp.concatenate([cos2, cos2], axis=1)])
    sinf = jnp.concatenate([zeros(HEAD_DIM), jnp.concatenate([-sin2, sin2], axis=1)])
    return cos1, sin1, cosf, sinf


def kernel(x, c, ctx, c_ctx, mod_w, mod_b, norm_pre, norm_post, ffn_gate, ffn_up, ffn_down, ev_w_in, ev_w_out, lru_conv_w, lru_conv_b, lru_wa, lru_ba, lru_wx, lru_bx, lru_lambda, ret_decay_logit, ret_gn, od_w_in, od_w_out, pool_w, pool_scale, q_norm, k_norm, loss_target, m_c_ctx, m_mod_w, m_mod_b, m_norm_pre, m_norm_post, m_ffn_gate, m_ffn_up, m_ffn_down, m_ev_w_in, m_ev_w_out, m_lru_conv_w, m_lru_conv_b, m_lru_wa, m_lru_ba, m_lru_wx, m_lru_bx, m_lru_lambda, m_ret_decay_logit, m_ret_gn, m_od_w_in, m_od_w_out, m_pool_w, m_pool_scale, m_q_norm, m_k_norm, v_c_ctx, v_mod_w, v_mod_b, v_norm_pre, v_norm_post, v_ffn_gate, v_ffn_up, v_ffn_down, v_ev_w_in, v_ev_w_out, v_lru_conv_w, v_lru_conv_b, v_lru_wa, v_lru_ba, v_lru_wx, v_lru_bx, v_lru_lambda, v_ret_decay_logit, v_ret_gn, v_od_w_in, v_od_w_out, v_pool_w, v_pool_scale, v_q_norm, v_k_norm):
    wts = dict(c_ctx=c_ctx, mod_w=mod_w, mod_b=mod_b, norm_pre=norm_pre, norm_post=norm_post, ffn_gate=ffn_gate, ffn_up=ffn_up,
               ffn_down=ffn_down, ev_w_in=ev_w_in, ev_w_out=ev_w_out, lru_conv_w=lru_conv_w, lru_conv_b=lru_conv_b,
               lru_wa=lru_wa, lru_ba=lru_ba, lru_wx=lru_wx, lru_bx=lru_bx, lru_lambda=lru_lambda,
               ret_decay_logit=ret_decay_logit, ret_gn=ret_gn, od_w_in=od_w_in, od_w_out=od_w_out, pool_w=pool_w,
               pool_scale=pool_scale, q_norm=q_norm, k_norm=k_norm)
    mom_m = dict(zip(WEIGHT_NAMES, (m_c_ctx, m_mod_w, m_mod_b, m_norm_pre, m_norm_post, m_ffn_gate, m_ffn_up, m_ffn_down,
                                    m_ev_w_in, m_ev_w_out, m_lru_conv_w, m_lru_conv_b, m_lru_wa, m_lru_ba, m_lru_wx, m_lru_bx,
                                    m_lru_lambda, m_ret_decay_logit, m_ret_gn, m_od_w_in, m_od_w_out, m_pool_w, m_pool_scale,
                                    m_q_norm, m_k_norm)))
    mom_v = dict(zip(WEIGHT_NAMES, (v_c_ctx, v_mod_w, v_mod_b, v_norm_pre, v_norm_post, v_ffn_gate, v_ffn_up, v_ffn_down,
                                    v_ev_w_in, v_ev_w_out, v_lru_conv_w, v_lru_conv_b, v_lru_wa, v_lru_ba, v_lru_wx, v_lru_bx,
                                    v_lru_lambda, v_ret_decay_logit, v_ret_gn, v_od_w_in, v_od_w_out, v_pool_w, v_pool_scale,
                                    v_q_norm, v_k_norm)))

    _, S, D = x.shape
    Lc = ctx.shape[1]
    T = Lc + S
    TR = 256 if (Lc % 256 == 0 and S % 256 == 0) else 128
    assert Lc % TR == 0 and S % TR == 0 and TR % RET_CHUNK == 0
    cfg = RowCfg(TR, T // TR, Lc // TR)
    W = lru_conv_b.shape[-1]
    H = ret_decay_logit.shape[-1]
    U = POOL_GROUP * len(POOL_WINDOWS)
    nq = (N_CHIPS * od_w_in.shape[-1]) // U - 3
    assert W % (H * RET_DV) == 0 and (2 * W) % RET_DK == 0
    nL = mod_w.shape[0]
    C4 = mod_w.shape[-1]
    assert nL == 2, "two layers: an even mixer then an odd one"

    xi, yi, ci = _coords()
    chip = 2 * xi + yi
    me = 4 * xi + 2 * yi + ci

    sc = jax.nn.silu(c)
    small_in, _ = _pack([sc] + [wts[n] for n in SMALL_SHARDED])
    g1 = all_gather_small(small_in, "gather_small_fwd")
    parts = _unpack(g1, [sc.shape] + [wts[n].shape for n in SMALL_SHARDED], lead=(N_DEV,))
    sc_all = parts[0][:, 0]
    full = {n: _unshard(parts[1 + i][0::2]) for i, n in enumerate(SMALL_SHARDED)}
    for n in SMALL_REPL + ("c_ctx",):
        full[n] = wts[n]

    scc = jax.nn.silu(c_ctx)[None]
    pad_rows = MOD_ROWS - N_DEV - 1
    s16 = jnp.concatenate([sc_all, scc, jnp.zeros((pad_rows, D), f32)])
    modp = mod_fwd(s16, mod_w, "mod_fwd")
    g2 = all_gather_small(modp.reshape(-1, LANES), "gather_mod")
    mod_all = g2.reshape(N_DEV, nL, MOD_ROWS, C4)[0::2]
    mod_all = jnp.moveaxis(mod_all, 0, 2).reshape(nL, MOD_ROWS, N_CHIPS * C4) + mod_b[:, None, :]
    mod_l = lax.dynamic_index_in_dim(mod_all, me, axis=1, keepdims=False).reshape(nL, 3, 3, D)
    mod_c = mod_all[:, N_DEV].reshape(nL, 3, 3, D)

    def mod_of(li, s, kind, ctx_live=True):
        cpart = mod_c[li, s, kind] if ctx_live else jnp.zeros((D,), f32)
        return jnp.stack([cpart, mod_l[li, s, kind]])[:, None, :]

    packed = {n: cast_bf16(wts[n], "cast_" + n) for n in BIG}
    ffn_units = [(0, 0), (0, 1), (1, 0), (1, 1)]

    def G(*pieces):
        return Comm("gather", [(packed[n], idx) for n, idx in pieces])

    cos1, sin1, cosf, sinf = _rope_tables(S, Lc)

    def sub_params(li, s, ctx_live=True, gate_ctx_live=True):
        return dict(g_pre=full["norm_pre"][li, s][None], g_post=full["norm_post"][li, s][None],
                    shift=mod_of(li, s, 0, ctx_live), scale=mod_of(li, s, 1, ctx_live),
                    gate=mod_of(li, s, 2, ctx_live and gate_ctx_live))

    X0 = jnp.concatenate([ctx[0], x[0]], axis=0)
    wg00, wu00 = exchange(G(("ffn_gate", (0, 0)), ("ffn_up", (0, 0))), "gather_first")
    p00 = sub_params(0, 0)
    p00.update(wg=wg00, wu=wu00)
    h00 = {"gateup": G(("ffn_down", (0, 0)), ("ev_w_in", (0,))), "down": G(("ev_w_out", (0,)), ("ffn_gate", (0, 1)))}
    h = pre_fwd(cfg, X0, p00["g_pre"], p00["shift"], p00["scale"], "l0f0_pre")
    (a00, b00, u00), (wd00, ev_in) = ffn_gateup(cfg, h, wg00, wu00, "l0f0_gateup", comm=h00["gateup"])
    p00.update(wd=wd00)
    y00, (ev_out, wg01) = matmul("v2", u00, wd00, comm=h00["down"], name="l0f0_down")
    X1 = post_fwd(cfg, X0, y00, p00["g_post"], p00["gate"], FFN_STEP, "l0f0_post")
    s00 = (X0, h, a00, b00, u00, y00)

    p01 = sub_params(0, 1)
    p01.update(W=W, H=H, conv_w=full["lru_conv_w"][0], conv_b=full["lru_conv_b"], wa=full["lru_wa"][0],
               ba=full["lru_ba"][0][:, None, :], wx=full["lru_wx"][0], bx=full["lru_bx"][0][:, None, :],
               lam=full["lru_lambda"][0][:, None, :], logit=full["ret_decay_logit"][0][:, :, None, None],
               gn=full["ret_gn"], cos1=cos1, sin1=sin1, w_in=ev_in, w_out=ev_out)
    X2, s01, got = even_fwd(cfg, X1, p01, "l0mix", host={"in": G(("ffn_up", (0, 1))), "out": G(("ffn_down", (0, 1)))})
    p02 = sub_params(0, 2)
    p02.update(wg=wg01, wu=got["in"][0], wd=got["out"][0])
    X3, s02, got = ffn_fwd(cfg, X2, p02, "l0f1", host={"gateup": G(("ffn_gate", (1, 0)), ("ffn_up", (1, 0))),
                                                       "down": G(("ffn_down", (1, 0)))})
    p10 = sub_params(1, 0)
    p10.update(wg=got["gateup"][0], wu=got["gateup"][1], wd=got["down"][0])
    X4, s10, got = ffn_fwd(cfg, X3, p10, "l1f0", host={"gateup": G(("od_w_in", (0,)), ("od_w_out", (0,)), ("ffn_gate", (1, 1))),
                                                       "down": G(("ffn_up", (1, 1)))})
    p11 = sub_params(1, 1, gate_ctx_live=False)
    p11.update(nq=nq, pool_w=full["pool_w"][0], pool_scale=full["pool_scale"], qg=full["q_norm"], kg=full["k_norm"],
               cosf=cosf, sinf=sinf, w_in=got["gateup"][0], w_out=got["gateup"][1])
    p12 = sub_params(1, 2, ctx_live=False)
    p12.update(wg=got["gateup"][2], wu=got["down"][0])
    X5, s11, got = odd_fwd(cfg, X4, p11, "l1mix", host={"in": G(("ffn_down", (1, 1)))})
    p12.update(wd=got["in"][0])
    X6, s12, _ = ffn_fwd(cfg, X5, p12, "l1f1")
    sq, dX = loss_fwd_bwd(cfg, X6, loss_target[0], "loss")
    loss = lax.psum(0.5 * jnp.sum(sq) / D, ("x", "y", "c"))

    recv_ffn = {}
    dX, g12, recv_ffn[(1, 1)] = ffn_bwd(cfg, dX, s12, p12, "l1f1")
    dX, g11, recv_od = odd_bwd(cfg, dX, s11, p11, "l1mix")
    dX, g10, recv_ffn[(1, 0)] = ffn_bwd(cfg, dX, s10, p10, "l1f0")
    dX, g02, recv_ffn[(0, 1)] = ffn_bwd(cfg, dX, s02, p02, "l0f1")
    dX, g01, recv_ev = even_bwd(cfg, dX, s01, p01, "l0mix")
    dX, g00, recv_ffn[(0, 0)] = ffn_bwd(cfg, dX, s00, p00, "l0f0")
    grad_x = dX[Lc:][None]

    subs = [[g00, g01, g02], [g10, g11, g12]]
    zero_d = jnp.zeros((D,), f32)

    def dmod(group, live):
        rows = []
        for li in range(nL):
            for s in range(3):
                for kind, key in enumerate(("shift", "scale", "gate")):
                    rows.append(subs[li][s][key][group, 0] if live(li, s, kind) else zero_d)
        return jnp.stack(rows).reshape(nL, 9 * D)

    dmod_l = dmod(1, lambda li, s, kind: True)
    dmod_c = dmod(0, lambda li, s, kind: not (li == 1 and (s == 2 or (s == 1 and kind == 2))))

    dm_in, _ = _pack([dmod_l, dmod_c])
    g3 = all_gather_small(dm_in, "gather_dmod")
    dl_all, dc_all = _unpack(g3, [dmod_l.shape, dmod_c.shape], lead=(N_DEV,))
    dm16 = jnp.moveaxis(jnp.concatenate([dl_all, dc_all], axis=0), 0, 1)
    dm16 = lax.dynamic_slice_in_dim(dm16, chip * C4, C4, axis=2)
    s16b = jnp.concatenate([sc_all, jnp.broadcast_to(scc, (N_DEV, D))])
    g_mod_w, dscc_part = mod_bwd(s16b, dm16, mod_w, "mod_bwd")

    norm_pre_g = jnp.stack([jnp.concatenate([subs[li][s]["g_pre"] for s in range(3)]) for li in range(nL)])
    norm_post_g = jnp.stack([jnp.concatenate([subs[li][s]["g_post"] for s in range(3)]) for li in range(nL)])
    small_g = dict(norm_pre=norm_pre_g, norm_post=norm_post_g, lru_conv_w=g01["conv_w"][None], lru_ba=g01["ba"][:, 0][None],
                   lru_bx=g01["bx"][:, 0][None], lru_lambda=g01["lam"][:, 0][None], pool_scale=g11["pool_scale"],
                   mod_b=dmod_l + dmod_c, lru_conv_b=g01["conv_b"], lru_wa=g01["wa"][None], lru_wx=g01["wx"][None],
                   ret_decay_logit=g01["logit"][:, :, 0, 0][None], ret_gn=g01["gn"], pool_w=g11["pool_w"][None],
                   q_norm=g11["qg"], k_norm=g11["kg"])
    names = SMALL_SHARDED + SMALL_REPL
    sg_in, _ = _pack([small_g[n] for n in names] + [dscc_part])
    g4 = all_gather_small(sg_in, "gather_small_grads")
    tot = sum_leading(g4, "sum_small_grads")
    tot_parts = _unpack(tot, [small_g[n].shape for n in names])
    dscc_all = _unpack(g4, [small_g[n].shape for n in names] + [dscc_part.shape], lead=(N_DEV,))[-1]
    dscc = dscc_all[0, 0] + dscc_all[2, 0] + dscc_all[4, 0] + dscc_all[6, 0]
    _, silu_vjp = jax.vjp(jax.nn.silu, c_ctx)
    grads = {"c_ctx": silu_vjp(dscc)[0]}
    for n, g in zip(names, tot_parts):
        if n in SMALL_SHARDED:
            k = wts[n].shape[-1]
            g = lax.dynamic_slice_in_dim(g, chip * k, k, axis=g.ndim - 1)
        grads[n] = g.reshape(wts[n].shape)

    partial = {}
    for n, key in (("ffn_gate", "wg"), ("ffn_up", "wu"), ("ffn_down", "wd")):
        acc = None
        for u in ffn_units:
            acc = sum_leading(recv_ffn[u][key], "sum_%s_%d%d" % (n, u[0], u[1]), into=acc, full_shape=wts[n].shape, widx=u)
        partial[n] = acc
    for n, r in (("ev_w_in", recv_ev["w_in"]), ("ev_w_out", recv_ev["w_out"]), ("od_w_in", recv_od["w_in"]),
                 ("od_w_out", recv_od["w_out"])):
        partial[n] = sum_leading(r, "sum_" + n).reshape(wts[n].shape)
    partial = [partial[n] for n in BIG]
    other = swap_with_sibling(partial, "swap_partials")

    delta, new_m, new_v = {}, {}, {}
    for n, pa, pb in zip(BIG, partial, other):
        grads[n], delta[n], new_m[n], new_v[n] = adamw(wts[n], mom_m[n], mom_v[n], [pa, pb], "adamw_" + n)
    grads["mod_w"], delta["mod_w"], new_m["mod_w"], new_v["mod_w"] = adamw(mod_w, m_mod_w, v_mod_w, [g_mod_w], "adamw_mod_w")
    snames = [n for n in WEIGHT_NAMES if n not in BIG and n != "mod_w"]
    pk = lambda d: _pack([d[n] for n in snames])[0]
    sres = adamw(pk(wts), pk(mom_m), pk(mom_v), [pk(grads)], "adamw_small")
    for res, dst in zip(sres[1:], (delta, new_m, new_v)):
        for n, a in zip(snames, _unpack(res, [wts[n].shape for n in snames])):
            dst[n] = a

    return (loss, grad_x, *[grads[n] for n in WEIGHT_NAMES], *[delta[n] for n in WEIGHT_NAMES],
            *[new_m[n] for n in WEIGHT_NAMES], *[new_v[n] for n in WEIGHT_NAMES])
```

```python
import functools
import math

import jax
import jax.numpy as jnp
from jax import lax
from jax.experimental import pallas as pl
from jax.experimental.pallas import tpu as pltpu

f32 = jnp.float32
bf16 = jnp.bfloat16
MESH = pl.DeviceIdType.MESH

EPS = 1e-6
FFN_STEP = 0.5
LRU_C = 8.0
CONV_W = 4
CONV_LEFT = 2
RET_DK = 256
RET_DV = 256
RET_CHUNK = 128
RET_THETA = 10000.0
POOL_WINDOWS = (2, 4, 8, 16)
POOL_GROUP = 128
HEAD_DIM = 128
ROPE_THETA = 10000.0
GRID_W = 64
ADAM_LR = 0.001
ADAM_B1 = 0.9
ADAM_B2 = 0.999
ADAM_EPS = 1e-08
ADAM_WD = 0.01
ADAM_STEP = 10

N_CHIPS = 4
N_DEV = 8
HALO = 8
VMEM_LIMIT = 56 * 1024 * 1024


def _params(sem=None):
    return pltpu.CompilerParams(dimension_semantics=sem, vmem_limit_bytes=VMEM_LIMIT)


def _pick(n, prefs):
    for p in prefs:
        if n % p == 0:
            return p
    return n


def _sds(shape, dtype):
    return jax.ShapeDtypeStruct(tuple(shape), dtype)


MATMUL_VMEM = 46 * 1024 * 1024


def _fit_rows(M, tm, tn, out_dtype):
    fixed = 2 * tm * tn * 4 + 2 * tm * tn * jnp.dtype(out_dtype).itemsize
    for rows in (1408, 768, 512, 256, 128):
        if M % rows == 0 and fixed + 2 * rows * (tm + tn) * 2 <= MATMUL_VMEM:
            return rows
    return M


_MM_KINDS = {
    "v1": ((1, 0), "out[:, g] = A @ W[g]"),
    "v2": ((1, 0), "out = sum_g A[:, g] @ W[g]"),
    "v3": ((1, 1), "out = sum_g A[:, g] @ W[g]^T"),
    "v4": ((1, 1), "out[:, g] = A @ W[g]^T"),
    "v5": ((0, 0), "out[g] = A^T @ C[:, g]"),
    "v6": ((0, 0), "out[g] = A[:, g]^T @ C"),
}


def matmul(kind, a, b, *, widx=(), out_dtype=f32, init=None, gshape=None, comm=None, name):
    nw = len(widx)
    cdims = _MM_KINDS[kind][0]
    if kind in ("v1", "v2", "v3", "v4"):
        G = b.shape[0]
        d1, d2 = b.shape[-2:]
        M = a.shape[0]
    else:
        G, d1, d2 = gshape
        M = a.shape[0]
    tm_p, tn_p, tk_p = (768, 512, 256, 128), (1408, 1536, 1024, 768, 512, 256, 128), (2048, 1408, 1536, 1024, 768, 512, 256, 128)
    wnone = (None,) * (1 + nw)

    if kind == "v1":
        K, Ns = d1, d2
        tm, tn, tk = _pick(M, tm_p), _pick(Ns, tn_p), _pick(K, tk_p)
        nI, nJ, nR = M // tm, Ns // tn, K // tk
        grid = (G, nI, nJ, nR)
        a_spec = pl.BlockSpec((tm, tk), lambda g, i, j, r: (i, r))
        b_spec = pl.BlockSpec(wnone + (tk, tn), lambda g, i, j, r: (g,) + widx + (r, j))
        o_spec = pl.BlockSpec((tm, tn), lambda g, i, j, r: (i, g * nJ + j))
        out_shape = _sds((M, G * Ns), out_dtype)
        acc_shape = (tm, tn)
    elif kind == "v2":
        Ks, N = d1, d2
        tm, tn, tk = _pick(M, tm_p), _pick(N, (2048,) + tn_p), _pick(Ks, tk_p)
        nI, nJ, nRk = M // tm, N // tn, Ks // tk
        nR = G * nRk
        grid = (1, nI, nJ, nR)
        a_spec = pl.BlockSpec((tm, tk), lambda g, i, j, r: (i, r))
        b_spec = pl.BlockSpec(wnone + (tk, tn), lambda g, i, j, r: (r // nRk,) + widx + (r % nRk, j))
        o_spec = pl.BlockSpec((tm, tn), lambda g, i, j, r: (i, j))
        out_shape = _sds((M, N), out_dtype)
        acc_shape = (tm, tn)
    elif kind == "v3":
        K, Ns = d1, d2
        tm, tn, tk = _pick(M, tm_p), _pick(K, (2048,) + tn_p), _pick(Ns, tk_p)
        nI, nJ, nRk = M // tm, K // tn, Ns // tk
        nR = G * nRk
        grid = (1, nI, nJ, nR)
        a_spec = pl.BlockSpec((tm, tk), lambda g, i, j, r: (i, r))
        b_spec = pl.BlockSpec(wnone + (tn, tk), lambda g, i, j, r: (r // nRk,) + widx + (j, r % nRk))
        o_spec = pl.BlockSpec((tm, tn), lambda g, i, j, r: (i, j))
        out_shape = _sds((M, K), out_dtype)
        acc_shape = (tm, tn)
    elif kind == "v4":
        Ks, N = d1, d2
        tm, tn, tk = _pick(M, tm_p), _pick(Ks, tn_p), _pick(N, tk_p)
        nI, nJ, nR = M // tm, Ks // tn, N // tk
        grid = (G, nI, nJ, nR)
        a_spec = pl.BlockSpec((tm, tk), lambda g, i, j, r: (i, r))
        b_spec = pl.BlockSpec(wnone + (tn, tk), lambda g, i, j, r: (g,) + widx + (j, r))
        o_spec = pl.BlockSpec((tm, tn), lambda g, i, j, r: (i, g * nJ + j))
        out_shape = _sds((M, G * Ks), out_dtype)
        acc_shape = (tm, tn)
    elif kind == "v5":
        K, Ns = d1, d2
        tm, tn, tk = _pick(K, (2048,) + tm_p), _pick(Ns, tn_p), 0
        tk = _fit_rows(M, tm, tn, out_dtype)
        nI, nJ, nR = K // tm, Ns // tn, M // tk
        grid = (G, nI, nJ, nR)
        a_spec = pl.BlockSpec((tk, tm), lambda g, i, j, r: (r, i))
        b_spec = pl.BlockSpec((tk, tn), lambda g, i, j, r: (r, g * nJ + j))
        o_spec = pl.BlockSpec((None, tm, tn), lambda g, i, j, r: (g, i, j))
        out_shape = _sds(gshape, out_dtype)
        acc_shape = (tm, tn)
    else:
        Ks, N = d1, d2
        tm, tn, tk = _pick(Ks, (1408,) + tm_p), _pick(N, (2048,) + tn_p), 0
        tk = _fit_rows(M, tm, tn, out_dtype)
        nI, nJ, nR = Ks // tm, N // tn, M // tk
        grid = (G, nI, nJ, nR)
        a_spec = pl.BlockSpec((tk, tm), lambda g, i, j, r: (r, g * nI + i))
        b_spec = pl.BlockSpec((tk, tn), lambda g, i, j, r: (r, j))
        o_spec = pl.BlockSpec((None, tm, tn), lambda g, i, j, r: (g, i, j))
        out_shape = _sds(gshape, out_dtype)
        acc_shape = (tm, tn)

    has_init = init is not None
    ncomm = len(comm.srcs) if comm is not None else 0

    def body(*refs):
        a_ref, b_ref = refs[0], refs[1]
        pos = 2
        init_ref = None
        if has_init:
            init_ref = refs[pos]
            pos += 1
        cin = refs[pos:pos + ncomm]
        pos += ncomm
        o_ref = refs[pos]
        cout = refs[pos + 1:pos + 1 + ncomm]
        acc_ref = refs[pos + 1 + ncomm]
        sems = refs[pos + 2 + ncomm:]
        r = pl.program_id(3)
        first, last = _grid_ends(grid)

        if ncomm:
            @pl.when(first)
            def _():
                _comm_start(comm, cin, cout, *sems)

        def prod():
            return lax.dot_general(a_ref[...], b_ref[...], ((cdims[:1], cdims[1:]), ((), ())), preferred_element_type=f32)

        def start():
            return init_ref[...] + prod() if has_init else prod()

        if nR == 1:
            o_ref[...] = start().astype(o_ref.dtype)
        else:
            @pl.when(r == 0)
            def _():
                acc_ref[...] = start()

            @pl.when(jnp.logical_and(r > 0, r < nR - 1))
            def _():
                acc_ref[...] += prod()

            @pl.when(r == nR - 1)
            def _():
                o_ref[...] = (acc_ref[...] + prod()).astype(o_ref.dtype)

        if ncomm:
            @pl.when(last)
            def _():
                _comm_wait(comm, cin, cout, *sems)

    in_specs = [a_spec, b_spec]
    args = [a, b]
    if has_init:
        in_specs.append(pl.BlockSpec((tm, tn), lambda g, i, j, r: (i, j)))
        args.append(init)
    out_specs, out_shapes, scratch = [o_spec], [out_shape], [pltpu.VMEM(acc_shape, f32)]
    if ncomm:
        hbm = pl.BlockSpec(memory_space=pl.ANY)
        in_specs += [hbm] * ncomm
        args += [src for src, _ in comm.srcs]
        out_specs += [hbm] * ncomm
        out_shapes += comm.out_shapes()
        scratch += _comm_sems(ncomm)
    res = pl.pallas_call(
        body, name=name, grid=grid, in_specs=in_specs, out_specs=out_specs, out_shape=out_shapes,
        scratch_shapes=scratch, compiler_params=_params(("arbitrary", "arbitrary", "arbitrary", "arbitrary")),
    )(*args)
    return (res[0], list(res[1:])) if ncomm else res[0]


class Comm:
    def __init__(self, mode, srcs):
        self.mode, self.srcs = mode, srcs

    def piece(self, n):
        arr, idx = self.srcs[n]
        shp = arr.shape[len(idx):]
        return shp if self.mode == "gather" else shp[1:]

    def out_shapes(self):
        return [_sds((N_CHIPS,) + tuple(self.piece(n)), self.srcs[n][0].dtype) for n in range(len(self.srcs))]


def _comm_sems(n):
    nsem = n * (N_CHIPS - 1)
    return [pltpu.SemaphoreType.DMA((nsem,)), pltpu.SemaphoreType.DMA((nsem,)), pltpu.SemaphoreType.DMA((n,))]


def _coords():
    return lax.axis_index("x"), lax.axis_index("y"), lax.axis_index("c")


def _flip(v, bit):
    return 1 - v if bit else v


def _chip_peers(x, y, c):
    out = []
    for k in range(1, N_CHIPS):
        kx, ky = (k >> 1) & 1, k & 1
        px, py = _flip(x, kx), _flip(y, ky)
        out.append((k, (px, py, c), 2 * px + py))
    return out


def _comm_copies(comm, in_refs, out_refs, send_sems, recv_sems, local_sems, with_recvs):
    x, y, c = _coords()
    s = 2 * x + y
    local, sends, recvs = [], [], []
    for w, (_, idx) in enumerate(comm.srcs):
        src = in_refs[w].at[idx] if idx else in_refs[w]
        out = out_refs[w]
        if comm.mode == "gather":
            local.append(pltpu.make_async_copy(src, out.at[s], local_sems.at[w]))
        else:
            local.append(pltpu.make_async_copy(src.at[s], out.at[N_CHIPS - 1], local_sems.at[w]))
        for k, peer, pidx in _chip_peers(x, y, c):
            j = w * (N_CHIPS - 1) + k - 1
            if comm.mode == "gather":
                out_src, out_dst, in_dst = src, out.at[s], out.at[pidx]
            else:
                out_src, out_dst, in_dst = src.at[pidx], out.at[k - 1], out.at[k - 1]
            sends.append(pltpu.make_async_remote_copy(src_ref=out_src, dst_ref=out_dst, send_sem=send_sems.at[j],
                                                      recv_sem=recv_sems.at[j], device_id=peer, device_id_type=MESH))
            if with_recvs:
                recvs.append(pltpu.make_async_remote_copy(src_ref=out_src, dst_ref=in_dst, send_sem=send_sems.at[j],
                                                          recv_sem=recv_sems.at[j], device_id=peer, device_id_type=MESH))
    return local, sends, recvs


def _comm_start(comm, in_refs, out_refs, send_sems, recv_sems, local_sems):
    local, sends, _ = _comm_copies(comm, in_refs, out_refs, send_sems, recv_sems, local_sems, False)
    for cp in local + sends:
        cp.start()


def _comm_wait(comm, in_refs, out_refs, send_sems, recv_sems, local_sems):
    local, sends, recvs = _comm_copies(comm, in_refs, out_refs, send_sems, recv_sems, local_sems, True)
    for cp in recvs:
        cp.wait_recv()
    for cp in sends:
        cp.wait_send()
    for cp in local:
        cp.wait()


def exchange(comm, name):
    n = len(comm.srcs)

    def body(*refs):
        in_refs, out_refs, sems = refs[:n], refs[n:2 * n], refs[2 * n:]
        _comm_start(comm, in_refs, out_refs, *sems)
        _comm_wait(comm, in_refs, out_refs, *sems)

    hbm = pl.BlockSpec(memory_space=pl.ANY)
    return pl.pallas_call(body, name=name, out_shape=comm.out_shapes(), in_specs=[hbm] * n, out_specs=[hbm] * n,
                          scratch_shapes=_comm_sems(n))(*[src for src, _ in comm.srcs])


class RowCfg:
    def __init__(self, TR, nT, cT):
        self.TR, self.nT, self.cT = TR, nT, cT


def _row_spec(cfg, spec, off):
    kind = spec[0]
    TR = cfg.TR
    hb = TR // HALO
    nH = cfg.nT * hb
    if kind == "row":
        _, arr, w, cb = spec
        return pl.BlockSpec((TR, w), lambda i: (i + off, cb))
    if kind == "prev":
        _, arr, w, cb = spec
        return pl.BlockSpec((HALO, w), lambda i: (jnp.maximum((i + off) * hb - 1, 0), cb))
    if kind == "next":
        _, arr, w, cb = spec
        return pl.BlockSpec((HALO, w), lambda i: (jnp.minimum((i + off + 1) * hb, nH - 1), cb))
    if kind == "full":
        arr = spec[1]
        nd = arr.ndim
        return pl.BlockSpec(arr.shape, lambda i: (0,) * nd)
    if kind == "grp":
        arr = spec[1]
        cT = cfg.cT
        return pl.BlockSpec((None, 1, arr.shape[-1]), lambda i: (((i + off) >= cT).astype(jnp.int32), 0, 0))
    if kind == "drow":
        _, arr, w, cb = spec
        return pl.BlockSpec((arr.shape[0], TR, w), lambda i: (0, i + off, cb))
    raise ValueError(kind)


def rowcall(cfg, fn, name, ins, outs, *, off=0, n=None, scratch=()):
    n = cfg.nT - off if n is None else n
    in_specs = [_row_spec(cfg, s, off) for s in ins]
    out_specs = [_row_spec(cfg, (s[0], s[1]) + tuple(s[2:]), off) for s in outs]
    out_shape = [s[1] for s in outs]

    def body(*refs):
        fn(pl.program_id(0) + off, *refs)

    res = pl.pallas_call(
        body, name=name, grid=(n,), in_specs=in_specs, out_specs=out_specs, out_shape=out_shape,
        scratch_shapes=list(scratch), compiler_params=_params(("arbitrary",)),
    )(*[s[1] for s in ins])
    return res


def _acc(ref, val, first):
    @pl.when(first)
    def _():
        ref[...] = val

    @pl.when(jnp.logical_not(first))
    def _():
        ref[...] += val


def _rms(x):
    return x * lax.rsqrt(jnp.mean(x * x, axis=-1, keepdims=True) + EPS)


def _pre_fn(x, g, shift, scale):
    return (_rms(x) * g) * (1.0 + scale) + shift


def _strips(TR, rows_per_strip, body, init):
    n = TR // rows_per_strip
    unroll = STRIP_UNROLL if n % STRIP_UNROLL == 0 else 1

    def step(r, carry):
        for u in range(unroll):
            start = pl.multiple_of((r * unroll + u) * rows_per_strip, rows_per_strip)
            carry = body(pl.ds(start, rows_per_strip), carry)
        return carry

    return lax.fori_loop(0, n // unroll, step, init)


STRIP_UNROLL = 8
F32_STRIP = 8
BF16_STRIP = 16


def _inv_rms(x):
    return lax.rsqrt(jnp.mean(x * x, axis=-1, keepdims=True) + EPS)


def pre_fwd(cfg, x, g, shift, scale, name):
    D = x.shape[1]

    def fn(i, x_ref, g_ref, sh_ref, sc_ref, h_ref):
        c = g_ref[...] * (1.0 + sc_ref[...])
        sh = sh_ref[...]

        def strip(rows, carry):
            xv = x_ref[rows, :]
            h_ref[rows, :] = (xv * _inv_rms(xv) * c + sh).astype(bf16)
            return carry

        _strips(cfg.TR, BF16_STRIP, strip, 0)

    return rowcall(cfg, fn, name, [("row", x, D, 0), ("full", g), ("grp", shift), ("grp", scale)],
                   [("row", _sds(x.shape, bf16), D, 0)])[0]


def pre_bwd(cfg, x, g, shift, scale, dh, dx_in, name):
    D = x.shape[1]
    cT = cfg.cT

    def fn(i, x_ref, g_ref, sh_ref, sc_ref, dh_ref, dxin_ref, dx_ref, dg_ref, dsh_ref, dsc_ref):
        gv, scv = g_ref[...], sc_ref[...]
        c = gv * (1.0 + scv)

        def strip(rows, carry):
            s0, s1 = carry
            xv, dhv = x_ref[rows, :], dh_ref[rows, :]
            r = _inv_rms(xv)
            xn = xv * r
            dxn = dhv * c
            m = jnp.mean(dxn * xn, axis=-1, keepdims=True)
            dx_ref[rows, :] = dxin_ref[rows, :] + r * (dxn - xn * m)
            return s0 + dhv, s1 + dhv * xn

        zero = jnp.zeros((F32_STRIP, D), f32)
        s0, s1 = _strips(cfg.TR, F32_STRIP, strip, (zero, zero))
        s0 = jnp.sum(s0, axis=0, keepdims=True)
        s1 = jnp.sum(s1, axis=0, keepdims=True)
        _acc(dg_ref, (1.0 + scv) * s1, i == 0)
        first = jnp.logical_or(i == 0, i == cT)
        _acc(dsh_ref, s0, first)
        _acc(dsc_ref, gv * s1, first)

    return rowcall(cfg, fn, name,
                   [("row", x, D, 0), ("full", g), ("grp", shift), ("grp", scale), ("row", dh, D, 0), ("row", dx_in, D, 0)],
                   [("row", _sds(x.shape, f32), D, 0), ("full", _sds((1, D), f32)),
                    ("grp", _sds((2, 1, D), f32)), ("grp", _sds((2, 1, D), f32))])


def _post_fn(w, y, g, gate):
    return (w * gate) * (_rms(y) * g)


def post_fwd(cfg, x, y, g, gate, w, name):
    D = x.shape[1]

    def fn(i, x_ref, y_ref, g_ref, gt_ref, o_ref):
        c = (w * gt_ref[...]) * g_ref[...]

        def strip(rows, carry):
            yv = y_ref[rows, :]
            o_ref[rows, :] = x_ref[rows, :] + c * (yv * _inv_rms(yv))
            return carry

        _strips(cfg.TR, F32_STRIP, strip, 0)

    return rowcall(cfg, fn, name, [("row", x, D, 0), ("row", y, D, 0), ("full", g), ("grp", gate)],
                   [("row", _sds(x.shape, f32), D, 0)])[0]


def post_bwd(cfg, dx, y, g, gate, w, name):
    D = dx.shape[1]
    cT = cfg.cT
    half = BF16_STRIP // 2

    def fn(i, dx_ref, y_ref, g_ref, gt_ref, dy_ref, dg_ref, dgt_ref):
        gv, gtv = g_ref[...], gt_ref[...]
        c = (w * gtv) * gv

        def strip(rows, s1):
            yv, dv = y_ref[rows, :], dx_ref[rows, :]
            r = _inv_rms(yv)
            yn = yv * r
            dyn = dv * c
            m = jnp.mean(dyn * yn, axis=-1, keepdims=True)
            dy_ref[rows, :] = (r * (dyn - yn * m)).astype(bf16)
            t = dv * yn
            return s1 + t[:half] + t[half:]

        s1 = _strips(cfg.TR, BF16_STRIP, strip, jnp.zeros((half, D), f32))
        s1 = jnp.sum(s1, axis=0, keepdims=True)
        _acc(dg_ref, (w * gtv) * s1, i == 0)
        _acc(dgt_ref, (w * gv) * s1, jnp.logical_or(i == 0, i == cT))

    return rowcall(cfg, fn, name, [("row", dx, D, 0), ("row", y, D, 0), ("full", g), ("grp", gate)],
                   [("row", _sds(dx.shape, bf16), D, 0), ("full", _sds((1, D), f32)), ("grp", _sds((2, 1, D), f32))])


def _swiglu_fn(a, b):
    return jax.nn.silu(a) * b


def swiglu_fwd(cfg, a, b, name):
    F = a.shape[1]
    tf = _pick(F, (1408, 1024, 512, 256, 128))
    TR = cfg.TR

    def body(a_ref, b_ref, u_ref):
        u_ref[...] = _swiglu_fn(a_ref[...], b_ref[...]).astype(bf16)

    spec = pl.BlockSpec((TR, tf), lambda i, j: (i, j))
    return pl.pallas_call(body, name=name, grid=(cfg.nT, F // tf), in_specs=[spec, spec], out_specs=spec,
                          out_shape=_sds(a.shape, bf16), compiler_params=_params(("arbitrary", "arbitrary")))(a, b)


def swiglu_bwd(cfg, a, b, du, name):
    F = a.shape[1]
    tf = _pick(F, (1408, 1024, 512, 256, 128))
    TR = cfg.TR

    def body(a_ref, b_ref, du_ref, da_ref, db_ref):
        _, vjp = jax.vjp(_swiglu_fn, a_ref[...], b_ref[...])
        da, db = vjp(du_ref[...])
        da_ref[...] = da.astype(bf16)
        db_ref[...] = db.astype(bf16)

    spec = pl.BlockSpec((TR, tf), lambda i, j: (i, j))
    return pl.pallas_call(body, name=name, grid=(cfg.nT, F // tf), in_specs=[spec, spec, spec], out_specs=[spec, spec],
                          out_shape=[_sds(a.shape, bf16), _sds(a.shape, bf16)],
                          compiler_params=_params(("arbitrary", "arbitrary")))(a, b, du)


def _hosted(host, role, got, fn):
    comm = host.get(role) if host else None
    if comm is None:
        return fn(None)
    out, res = fn(comm)
    got[role] = res
    return out


def _grid_ends(grid):
    ids = [pl.program_id(n) for n in range(len(grid))]
    first = functools.reduce(jnp.logical_and, [i == 0 for i in ids])
    last = functools.reduce(jnp.logical_and, [i == n - 1 for i, n in zip(ids, grid)])
    return first, last


def _comm_plumbing(comm):
    if comm is None:
        return [], [], [], [], []
    n = len(comm.srcs)
    hbm = pl.BlockSpec(memory_space=pl.ANY)
    return [hbm] * n, [src for src, _ in comm.srcs], [hbm] * n, comm.out_shapes(), _comm_sems(n)


FFN_ROWS = 384


def ffn_gateup(cfg, h, wg, wu, name, comm=None):
    M, K = h.shape
    G, _, F = wg.shape
    tm = _pick(M, (FFN_ROWS, 256, 128))
    grid = (G, M // tm)
    ncomm = len(comm.srcs) if comm is not None else 0

    def body(h_ref, wg_ref, wu_ref, *rest):
        cin, rest = rest[:ncomm], rest[ncomm:]
        a_ref, b_ref, u_ref = rest[:3]
        cout, sems = rest[3:3 + ncomm], rest[3 + ncomm:]
        first, last = _grid_ends(grid)
        if ncomm:
            @pl.when(first)
            def _():
                _comm_start(comm, cin, cout, *sems)

        hv = h_ref[...]
        a = jnp.dot(hv, wg_ref[...], preferred_element_type=f32)
        b = jnp.dot(hv, wu_ref[...], preferred_element_type=f32)
        a_ref[...] = a.astype(bf16)
        b_ref[...] = b.astype(bf16)
        u_ref[...] = _swiglu_fn(a, b).astype(bf16)
        if ncomm:
            @pl.when(last)
            def _():
                _comm_wait(comm, cin, cout, *sems)

    ci, ca, co, cs, csem = _comm_plumbing(comm)
    w_spec = pl.BlockSpec((None, K, F), lambda g, i: (g, 0, 0))
    o_spec = pl.BlockSpec((tm, F), lambda g, i: (i, g))
    res = pl.pallas_call(
        body, name=name, grid=grid, in_specs=[pl.BlockSpec((tm, K), lambda g, i: (i, 0)), w_spec, w_spec] + ci,
        out_specs=[o_spec] * 3 + co, out_shape=[_sds((M, G * F), bf16)] * 3 + cs, scratch_shapes=csem,
        compiler_params=_params(("arbitrary", "arbitrary")))(h, wg, wu, *ca)
    return (res[0], res[1], res[2]), list(res[3:])


def ffn_du_act(cfg, dy, wd, a, b, name, comm=None):
    M, N = dy.shape
    G, F, _ = wd.shape
    tm = _pick(M, (FFN_ROWS, 256, 128))
    grid = (G, M // tm)
    ncomm = len(comm.srcs) if comm is not None else 0

    def body(dy_ref, wd_ref, a_ref, b_ref, *rest):
        cin, rest = rest[:ncomm], rest[ncomm:]
        da_ref, db_ref = rest[:2]
        cout, sems = rest[2:2 + ncomm], rest[2 + ncomm:]
        first, last = _grid_ends(grid)
        if ncomm:
            @pl.when(first)
            def _():
                _comm_start(comm, cin, cout, *sems)

        du = _dotf(dy_ref[...], wd_ref[...], 1, 1)
        _, vjp = jax.vjp(_swiglu_fn, a_ref[...].astype(f32), b_ref[...].astype(f32))
        da, db = vjp(du)
        da_ref[...] = da.astype(bf16)
        db_ref[...] = db.astype(bf16)
        if ncomm:
            @pl.when(last)
            def _():
                _comm_wait(comm, cin, cout, *sems)

    ci, ca, co, cs, csem = _comm_plumbing(comm)
    t_spec = pl.BlockSpec((tm, F), lambda g, i: (i, g))
    res = pl.pallas_call(
        body, name=name, grid=grid,
        in_specs=[pl.BlockSpec((tm, N), lambda g, i: (i, 0)), pl.BlockSpec((None, F, N), lambda g, i: (g, 0, 0)), t_spec,
                  t_spec] + ci,
        out_specs=[t_spec, t_spec] + co, out_shape=[_sds((M, G * F), bf16)] * 2 + cs, scratch_shapes=csem,
        compiler_params=_params(("arbitrary", "arbitrary")))(dy, wd, a, b, *ca)
    return (res[0], res[1]), list(res[2:])


def ffn_fwd(cfg, x, p, tag, host=None):
    got = {}
    host = host or {}
    h = pre_fwd(cfg, x, p["g_pre"], p["shift"], p["scale"], tag + "_pre")
    (a, b, u), res = ffn_gateup(cfg, h, p["wg"], p["wu"], tag + "_gateup", comm=host.get("gateup"))
    if res:
        got["gateup"] = res
    y = _hosted(host, "down", got, lambda cm: matmul("v2", u, p["wd"], comm=cm, name=tag + "_down"))
    xo = post_fwd(cfg, x, y, p["g_post"], p["gate"], FFN_STEP, tag + "_post")
    return xo, (x, h, a, b, u, y), got


def ffn_bwd(cfg, dX, saved, p, tag):
    x, h, a, b, u, y = saved
    dy, dg_post, dgate = post_bwd(cfg, dX, y, p["g_post"], p["gate"], FFN_STEP, tag + "_postb")
    (da, db), _ = ffn_du_act(cfg, dy, p["wd"], a, b, tag + "_duact")
    gwd = matmul("v6", u, dy, gshape=p["wd"].shape, out_dtype=bf16, name=tag + "_gwd")
    gwg = matmul("v5", h, da, gshape=p["wg"].shape, out_dtype=bf16, name=tag + "_gwg")
    gwu, r_wd = matmul("v5", h, db, gshape=p["wu"].shape, out_dtype=bf16, comm=Comm("scatter", [(gwd, ())]),
                       name=tag + "_gwu")
    dh, r_wg = matmul("v3", da, p["wg"], comm=Comm("scatter", [(gwg, ())]), name=tag + "_dh1")
    dh, r_wu = matmul("v3", db, p["wu"], init=dh, comm=Comm("scatter", [(gwu, ())]), name=tag + "_dh2")
    dX, dg_pre, dshift, dscale = pre_bwd(cfg, x, p["g_pre"], p["shift"], p["scale"], dh, dX, tag + "_preb")
    small = dict(g_pre=dg_pre, g_post=dg_post, shift=dshift, scale=dscale, gate=dgate)
    return dX, small, dict(wg=r_wg[0], wu=r_wu[0], wd=r_wd[0])


def _seg_flags(cfg, i):
    start = jnp.logical_or(i == 0, i == cfg.cT)
    end = jnp.logical_or(i == cfg.cT - 1, i == cfg.nT - 1)
    return start, end


def _fill_halo(buf, cur, prev, nxt, start, end, TR):
    buf[pl.ds(0, HALO), :] = jnp.where(start, 0.0, prev)
    buf[pl.ds(HALO, TR), :] = cur
    buf[pl.ds(HALO + TR, HALO), :] = jnp.where(end, 0.0, nxt)


def conv_fwd(cfg, z, cw, cb, W, name):
    TR = cfg.TR

    def fn(i, r_ref, rp_ref, rn_ref, cw_ref, cb_ref, u_ref, buf):
        start, end = _seg_flags(cfg, i)
        _fill_halo(buf, r_ref[...], rp_ref[...], rn_ref[...], start, end, TR)
        u = jnp.broadcast_to(cb_ref[...], (TR, W))
        for k in range(CONV_W):
            u = u + buf[pl.ds(HALO + k - CONV_LEFT, TR), :] * cw_ref[pl.ds(k, 1), :]
        u_ref[...] = u

    return rowcall(cfg, fn, name, [("row", z, W, 1), ("prev", z, W, 1), ("next", z, W, 1), ("full", cw), ("full", cb)],
                   [("row", _sds((z.shape[0], W), f32), W, 0)], scratch=[pltpu.VMEM((TR + 2 * HALO, W), f32)])[0]


def conv_bwd(cfg, z, du, cw, W, name):
    TR = cfg.TR

    def fn(i, r_ref, rp_ref, rn_ref, du_ref, dup_ref, dun_ref, cw_ref, dr_ref, dcw_ref, dcb_ref, rbuf, dbuf):
        start, end = _seg_flags(cfg, i)
        _fill_halo(rbuf, r_ref[...], rp_ref[...], rn_ref[...], start, end, TR)
        _fill_halo(dbuf, du_ref[...], dup_ref[...], dun_ref[...], start, end, TR)
        du = du_ref[...]

        @pl.when(i == 0)
        def _():
            dcw_ref[...] = jnp.zeros(dcw_ref.shape, f32)
            dcb_ref[...] = jnp.zeros(dcb_ref.shape, f32)

        dr = jnp.zeros((TR, W), f32)
        for k in range(CONV_W):
            dr = dr + dbuf[pl.ds(HALO - (k - CONV_LEFT), TR), :] * cw_ref[pl.ds(k, 1), :]
            dcw_ref[pl.ds(k, 1), :] += jnp.sum(du * rbuf[pl.ds(HALO + k - CONV_LEFT, TR), :], axis=0, keepdims=True)
        dcb_ref[...] += jnp.sum(du, axis=0, keepdims=True)
        dr_ref[...] = dr

    T = z.shape[0]
    return rowcall(cfg, fn, name,
                   [("row", z, W, 1), ("prev", z, W, 1), ("next", z, W, 1), ("row", du, W, 0), ("prev", du, W, 0),
                    ("next", du, W, 0), ("full", cw)],
                   [("row", _sds((T, W), f32), W, 0), ("full", _sds((CONV_W, W), f32)), ("full", _sds((1, W), f32))],
                   scratch=[pltpu.VMEM((TR + 2 * HALO, W), f32), pltpu.VMEM((TR + 2 * HALO, W), f32)])


def _softplus(x):
    return jnp.maximum(x, 0.0) + jnp.log1p(jnp.exp(-jnp.abs(x)))


def _neg_expm1(x):
    series = -x * (1.0 + x * (0.5 + x * (1.0 / 6.0 + x * (1.0 / 24.0 + x * (1.0 / 120.0 + x * (1.0 / 720.0))))))
    return jnp.where(x > -0.1, series, 1.0 - jnp.exp(x))


def _lru_coef(u, pa, px, lam):
    r = jax.nn.sigmoid(pa)
    i = jax.nn.sigmoid(px)
    log_a = -LRU_C * r * _softplus(-lam)
    a = jnp.exp(log_a)
    b = jnp.sqrt(_neg_expm1(2.0 * log_a)) * (i * u)
    return a, b


def _blockdiag(u_bf, w_ref, d, nblk, blk):
    return jnp.concatenate(
        [jnp.dot(u_bf[:, n * blk:(n + 1) * blk], w_ref[d, n].astype(bf16), preferred_element_type=f32)
         for n in range(nblk)], axis=1)


def lru_coef_fwd(cfg, u, wa, ba, wx, bx, lam, name):
    T, W = u.shape
    nblk, blk = wa.shape[1], wa.shape[2]

    def fn(i, u_ref, wa_ref, ba_ref, wx_ref, bx_ref, lam_ref, a_ref, b_ref):
        uv = u_ref[...]
        u_bf = uv.astype(bf16)
        for d in range(2):
            pa = _blockdiag(u_bf, wa_ref, d, nblk, blk) + ba_ref[d]
            px = _blockdiag(u_bf, wx_ref, d, nblk, blk) + bx_ref[d]
            a, b = _lru_coef(uv, pa, px, lam_ref[d])
            a_ref[d] = a
            b_ref[d] = b

    return rowcall(cfg, fn, name, [("row", u, W, 0), ("full", wa), ("full", ba), ("full", wx), ("full", bx), ("full", lam)],
                   [("drow", _sds((2, T, W), f32), W, 0), ("drow", _sds((2, T, W), f32), W, 0)])


def lru_coef_bwd(cfg, u, da, db, wa, ba, wx, bx, lam, name):
    T, W = u.shape
    nblk, blk = wa.shape[1], wa.shape[2]

    def fn(i, u_ref, da_ref, db_ref, wa_ref, ba_ref, wx_ref, bx_ref, lam_ref,
           du_ref, dwa_ref, dba_ref, dwx_ref, dbx_ref, dlam_ref):
        @pl.when(i == 0)
        def _():
            for r in (dwa_ref, dba_ref, dwx_ref, dbx_ref, dlam_ref):
                r[...] = jnp.zeros(r.shape, f32)

        uv = u_ref[...]
        u_bf = uv.astype(bf16)
        du = jnp.zeros(uv.shape, f32)
        for d in range(2):
            pa = _blockdiag(u_bf, wa_ref, d, nblk, blk) + ba_ref[d]
            px = _blockdiag(u_bf, wx_ref, d, nblk, blk) + bx_ref[d]
            _, vjp = jax.vjp(_lru_coef, uv, pa, px, lam_ref[d])
            du_e, dpa, dpx, dlam = vjp((da_ref[d], db_ref[d]))
            du = du + du_e
            dba_ref[d] += jnp.sum(dpa, axis=0, keepdims=True)
            dbx_ref[d] += jnp.sum(dpx, axis=0, keepdims=True)
            dlam_ref[d] += dlam
            parts = []
            for n in range(nblk):
                sl = slice(n * blk, (n + 1) * blk)
                ga, gx = dpa[:, sl].astype(bf16), dpx[:, sl].astype(bf16)
                ub = u_bf[:, sl]
                dwa_ref[d, n] += lax.dot_general(ub, ga, (((0,), (0,)), ((), ())), preferred_element_type=f32)
                dwx_ref[d, n] += lax.dot_general(ub, gx, (((0,), (0,)), ((), ())), preferred_element_type=f32)
                parts.append(
                    lax.dot_general(ga, wa_ref[d, n].astype(bf16), (((1,), (1,)), ((), ())), preferred_element_type=f32)
                    + lax.dot_general(gx, wx_ref[d, n].astype(bf16), (((1,), (1,)), ((), ())), preferred_element_type=f32))
            du = du + jnp.concatenate(parts, axis=1)
        du_ref[...] = du

    return rowcall(cfg, fn, name,
                   [("row", u, W, 0), ("drow", da, W, 0), ("drow", db, W, 0), ("full", wa), ("full", ba), ("full", wx),
                    ("full", bx), ("full", lam)],
                   [("row", _sds((T, W), f32), W, 0), ("full", _sds(wa.shape, f32)), ("full", _sds(ba.shape, f32)),
                    ("full", _sds(wx.shape, f32)), ("full", _sds(bx.shape, f32)), ("full", _sds(lam.shape, f32))])


def _dir_tile(cfg, d, j):
    rev = jnp.where(j < cfg.cT, cfg.cT - 1 - j, cfg.nT - 1 - (j - cfg.cT))
    return jnp.where(d == 0, j, rev)


def lru_scan(cfg, a, b, name):
    _, T, W = a.shape
    TR, nT = cfg.TR, cfg.nT

    def body(a_ref, b_ref, h_ref, hp_ref, st):
        d, j = pl.program_id(0), pl.program_id(1)

        @pl.when(j == 0)
        def _():
            st[...] = jnp.zeros(st.shape, f32)

        def step(t, h):
            idx = t + d * (TR - 1 - 2 * t)
            hn = a_ref[pl.ds(idx, 1), :] * h + b_ref[pl.ds(idx, 1), :]
            hp_ref[pl.ds(idx, 1), :] = h
            h_ref[pl.ds(idx, 1), :] = hn
            return hn

        st[...] = lax.fori_loop(0, TR, step, st[...])

    spec = pl.BlockSpec((None, TR, W), lambda d, j: (d, _dir_tile(cfg, d, j), 0))
    return pl.pallas_call(body, name=name, grid=(2, nT), in_specs=[spec, spec], out_specs=[spec, spec],
                          out_shape=[_sds(a.shape, f32), _sds(a.shape, f32)], scratch_shapes=[pltpu.VMEM((1, W), f32)],
                          compiler_params=_params(("arbitrary", "arbitrary")))(a, b)


def lru_scan_bwd(cfg, a, hp, dh, name):
    _, T, W = a.shape
    TR, nT = cfg.TR, cfg.nT

    def body(a_ref, hp_ref, dh_ref, da_ref, db_ref, st):
        d, j = pl.program_id(0), pl.program_id(1)

        @pl.when(j == 0)
        def _():
            st[...] = jnp.zeros(st.shape, f32)

        def step(t, c):
            p = TR - 1 - t
            idx = p + d * (TR - 1 - 2 * p)
            g = dh_ref[pl.ds(idx, 1), :] + c
            db_ref[pl.ds(idx, 1), :] = g
            da_ref[pl.ds(idx, 1), :] = g * hp_ref[pl.ds(idx, 1), :]
            return a_ref[pl.ds(idx, 1), :] * g

        st[...] = lax.fori_loop(0, TR, step, st[...])

    spec = pl.BlockSpec((None, TR, W), lambda d, j: (d, _dir_tile(cfg, d, nT - 1 - j), 0))
    dspec = pl.BlockSpec((TR, W), lambda d, j: (_dir_tile(cfg, d, nT - 1 - j), 0))
    return pl.pallas_call(body, name=name, grid=(2, nT), in_specs=[spec, spec, dspec], out_specs=[spec, spec],
                          out_shape=[_sds(a.shape, f32), _sds(a.shape, f32)], scratch_shapes=[pltpu.VMEM((1, W), f32)],
                          compiler_params=_params(("arbitrary", "arbitrary")))(a, hp, dh)


def _lru_out_fn(gl, h0, h1):
    return jax.nn.gelu(gl) * (h0 + h1)


def lru_out_fwd(cfg, z, h, W, name):
    def fn(i, g_ref, h_ref, o_ref):
        o_ref[...] = _lru_out_fn(g_ref[...], h_ref[0], h_ref[1]).astype(bf16)

    return rowcall(cfg, fn, name, [("row", z, W, 0), ("drow", h, W, 0)], [("row", _sds((z.shape[0], W), bf16), W, 0)])[0]


def lru_out_bwd(cfg, z, h, dmix, W, name):
    def fn(i, g_ref, h_ref, d_ref, dg_ref, dh_ref):
        _, vjp = jax.vjp(_lru_out_fn, g_ref[...], h_ref[0], h_ref[1])
        dg, dh0, _ = vjp(d_ref[...])
        dg_ref[...] = dg
        dh_ref[...] = dh0

    T = z.shape[0]
    return rowcall(cfg, fn, name, [("row", z, W, 0), ("drow", h, W, 0), ("row", dmix, W, 0)],
                   [("row", _sds((T, W), f32), W, 0), ("row", _sds((T, W), f32), W, 0)])


def _rot_half(x, cos, sin):
    n = x.shape[1] // 2
    x1, x2 = x[:, :n], x[:, n:]
    return jnp.concatenate([x1 * cos - x2 * sin, x1 * sin + x2 * cos], axis=1)


def _dotf(a, b, ca, cb):
    return lax.dot_general(a, b, (((ca,), (cb,)), ((), ())), preferred_element_type=f32)


@functools.partial(jax.custom_vjp, nondiff_argnums=(2, 3))
def _dotb(a, b, ca, cb):
    return _dotf(a.astype(bf16), b.astype(bf16), ca, cb)


def _dotb_fwd(a, b, ca, cb):
    return _dotb(a, b, ca, cb), (a, b)


def _dotb_bwd(ca, cb, res, ct):
    a, b = res
    a16, b16, ct16 = a.astype(bf16), b.astype(bf16), ct.astype(bf16)
    da = _dotf(ct16, b16, 1, 1 - cb) if ca == 1 else _dotf(b16, ct16, 1 - cb, 1)
    db = _dotf(a16, ct16, 1 - ca, 0) if cb == 0 else _dotf(ct16, a16, 0, 1 - ca)
    return da, db


_dotb.defvjp(_dotb_fwd, _dotb_bwd)


def _ret_chunk(d, q, k, v, s, logit, cos, sin):
    C = q.shape[0]
    lg = -_softplus(-logit)
    qr = _rot_half(q, cos, sin)
    kr = _rot_half(k, cos, sin) * (RET_DK ** -0.5)
    ii = lax.broadcasted_iota(jnp.int32, (C, C), 0)
    jj = lax.broadcasted_iota(jnp.int32, (C, C), 1)
    diff = ((ii - jj) if d == 0 else (jj - ii)).astype(f32)
    intra = jnp.where(diff >= 0, jnp.exp(lg * jnp.maximum(diff, 0.0)), 0.0)
    pos = lax.broadcasted_iota(jnp.int32, (C, 1), 0).astype(f32)
    if d == 0:
        q_dec, k_dec = jnp.exp(lg * (pos + 1.0)), jnp.exp(lg * (C - 1.0 - pos))
    else:
        q_dec, k_dec = jnp.exp(lg * (C - pos)), jnp.exp(lg * pos)
    s_dec = jnp.exp(lg * C)
    scores = _dotb(qr, kr, 1, 1) * intra
    o = _dotb(scores, v, 1, 0) + _dotb(qr * q_dec, s, 1, 0)
    s_new = s * s_dec + _dotb(kr * k_dec, v, 0, 0)
    return o, s_new


def _chunk_cfg(cfg):
    f = cfg.TR // RET_CHUNK
    return RowCfg(RET_CHUNK, cfg.nT * f, cfg.cT * f)


def ret_fwd(cfg, z, logit, cos, sin, H, qcol, name):
    T = z.shape[0]
    cc = _chunk_cfg(cfg)
    C, nC = RET_CHUNK, cc.nT

    def body(q_ref, k_ref, v_ref, lg_ref, cos_ref, sin_ref, o_ref, s_ref, st):
        d, j = pl.program_id(0), pl.program_id(2)

        @pl.when(j == 0)
        def _():
            st[...] = jnp.zeros(st.shape, f32)

        s_ref[...] = st[...]
        for dd in range(2):
            @pl.when(d == dd)
            def _():
                o, s_new = _ret_chunk(dd, q_ref[...], k_ref[...], v_ref[...], st[...], lg_ref[...], cos_ref[...], sin_ref[...])
                o_ref[...] = o
                st[...] = s_new

    tile = lambda d, j: _dir_tile(cc, d, j)
    zq = pl.BlockSpec((C, RET_DK), lambda d, h, j: (tile(d, j), qcol + h))
    zk = pl.BlockSpec((C, RET_DK), lambda d, h, j: (tile(d, j), qcol + H + h))
    zv = pl.BlockSpec((C, RET_DV), lambda d, h, j: (tile(d, j), qcol + 2 * H + h))
    lgs = pl.BlockSpec((None, None, 1, 1), lambda d, h, j: (d, h, 0, 0))
    cs = pl.BlockSpec((C, RET_DK // 2), lambda d, h, j: (tile(d, j), 0))
    o_spec = pl.BlockSpec((None, C, RET_DV), lambda d, h, j: (d, tile(d, j), h))
    s_spec = pl.BlockSpec((None, None, None, RET_DK, RET_DV), lambda d, h, j: (d, h, tile(d, j), 0, 0))
    return pl.pallas_call(
        body, name=name, grid=(2, H, nC), in_specs=[zq, zk, zv, lgs, cs, cs], out_specs=[o_spec, s_spec],
        out_shape=[_sds((2, T, H * RET_DV), f32), _sds((2, H, nC, RET_DK, RET_DV), f32)],
        scratch_shapes=[pltpu.VMEM((RET_DK, RET_DV), f32)],
        compiler_params=_params(("arbitrary", "arbitrary", "arbitrary")))(z, z, z, logit, cos, sin)


def ret_bwd(cfg, z, states, do, logit, cos, sin, H, qcol, name):
    T = z.shape[0]
    cc = _chunk_cfg(cfg)
    C, nC = RET_CHUNK, cc.nT

    def body(q_ref, k_ref, v_ref, s_ref, do_ref, lg_ref, cos_ref, sin_ref, dq_ref, dk_ref, dv_ref, dlg_ref, st):
        d, j = pl.program_id(0), pl.program_id(2)

        @pl.when(j == 0)
        def _():
            st[...] = jnp.zeros(st.shape, f32)
            dlg_ref[...] = jnp.zeros(dlg_ref.shape, f32)

        for dd in range(2):
            @pl.when(d == dd)
            def _():
                fn = lambda q, k, v, s, lg: _ret_chunk(dd, q, k, v, s, lg, cos_ref[...], sin_ref[...])
                _, vjp = jax.vjp(fn, q_ref[...], k_ref[...], v_ref[...], s_ref[...], lg_ref[...])
                dq, dk, dv, ds, dlg = vjp((do_ref[...], st[...]))
                dq_ref[...] = dq
                dk_ref[...] = dk
                dv_ref[...] = dv
                st[...] = ds
                dlg_ref[...] += dlg

    tile = lambda d, j: _dir_tile(cc, d, nC - 1 - j)
    zq = pl.BlockSpec((C, RET_DK), lambda d, h, j: (tile(d, j), qcol + h))
    zk = pl.BlockSpec((C, RET_DK), lambda d, h, j: (tile(d, j), qcol + H + h))
    zv = pl.BlockSpec((C, RET_DV), lambda d, h, j: (tile(d, j), qcol + 2 * H + h))
    s_spec = pl.BlockSpec((None, None, None, RET_DK, RET_DV), lambda d, h, j: (d, h, tile(d, j), 0, 0))
    do_spec = pl.BlockSpec((C, RET_DV), lambda d, h, j: (tile(d, j), h))
    lgs = pl.BlockSpec((None, None, 1, 1), lambda d, h, j: (d, h, 0, 0))
    cs = pl.BlockSpec((C, RET_DK // 2), lambda d, h, j: (tile(d, j), 0))
    g_spec = pl.BlockSpec((None, C, RET_DK), lambda d, h, j: (d, tile(d, j), h))
    gshape = _sds((2, T, H * RET_DK), f32)
    return pl.pallas_call(
        body, name=name, grid=(2, H, nC), in_specs=[zq, zk, zv, s_spec, do_spec, lgs, cs, cs],
        out_specs=[g_spec, g_spec, g_spec, lgs], out_shape=[gshape, gshape, gshape, _sds((2, H, 1, 1), f32)],
        scratch_shapes=[pltpu.VMEM((RET_DK, RET_DV), f32)],
        compiler_params=_params(("arbitrary", "arbitrary", "arbitrary")))(z, z, z, states, do, logit, cos, sin)


def _ret_norm_fn(H, o0, o1, ol, gn):
    o = o0 + o1
    parts = []
    for h in range(H):
        x = o[:, h * RET_DV:(h + 1) * RET_DV]
        mu = jnp.mean(x, axis=-1, keepdims=True)
        var = jnp.mean(jnp.square(x - mu), axis=-1, keepdims=True)
        parts.append((x - mu) * lax.rsqrt(var + EPS))
    return (jnp.concatenate(parts, axis=1) * gn) * jax.nn.silu(ol)


def ret_norm_fwd(cfg, o, z, gn, H, olcol, name):
    RV = H * RET_DV

    def fn(i, o_ref, ol_ref, gn_ref, r_ref):
        r_ref[...] = _ret_norm_fn(H, o_ref[0], o_ref[1], ol_ref[...], gn_ref[...]).astype(bf16)

    return rowcall(cfg, fn, name, [("drow", o, RV, 0), ("row", z, RV, olcol), ("full", gn)],
                   [("row", _sds((z.shape[0], RV), bf16), RV, 0)])[0]


def ret_norm_bwd(cfg, o, z, gn, dmix, H, olcol, dcol, name):
    RV = H * RET_DV
    T = z.shape[0]

    def fn(i, o_ref, ol_ref, gn_ref, d_ref, do_ref, dol_ref, dgn_ref):
        _, vjp = jax.vjp(functools.partial(_ret_norm_fn, H), o_ref[0], o_ref[1], ol_ref[...], gn_ref[...])
        do, _, dol, dgn = vjp(d_ref[...])
        do_ref[...] = do
        dol_ref[...] = dol
        _acc(dgn_ref, dgn, i == 0)

    return rowcall(cfg, fn, name, [("drow", o, RV, 0), ("row", z, RV, olcol), ("full", gn), ("row", dmix, RV, dcol)],
                   [("row", _sds((T, RV), f32), RV, 0), ("row", _sds((T, RV), f32), RV, 0), ("full", _sds((1, RV), f32))])


def _pool_geom(cfg, i, w, L):
    t = (i - cfg.cT) * cfg.TR + lax.broadcasted_iota(jnp.int32, (cfg.TR, 1), 0)
    lo = jnp.clip(t - w // 2, 0, L)
    hi = jnp.clip(t + w // 2, 0, L)
    return (hi - lo).astype(f32)


def _pool_centred(cfg, i, buf, gi, w, L):
    TR, G = cfg.TR, POOL_GROUP
    cols = pl.ds(gi * G, G)
    tot = buf[pl.ds(HALO - w // 2, TR), cols]
    for s in range(-w // 2 + 1, w // 2):
        tot = tot + buf[pl.ds(HALO + s, TR), cols]
    cnt = _pool_geom(cfg, i, w, L)
    return tot / cnt - buf[pl.ds(HALO, TR), cols], cnt


def pool_fwd(cfg, z, pw, ps, name):
    T = z.shape[0]
    TR, cT = cfg.TR, cfg.cT
    P = POOL_GROUP * len(POOL_WINDOWS)
    L = T - cT * TR

    def fn(i, x_ref, xp_ref, xn_ref, pw_ref, ps_ref, o_ref, buf):
        @pl.when(i < cT)
        def _():
            o_ref[...] = jnp.zeros(o_ref.shape, bf16)

        @pl.when(i >= cT)
        def _():
            start, end = _seg_flags(cfg, i)
            _fill_halo(buf, x_ref[...], xp_ref[...], xn_ref[...], start, end, TR)
            outs = []
            for gi, w in enumerate(POOL_WINDOWS):
                m, _ = _pool_centred(cfg, i, buf, gi, w, L)
                outs.append(jnp.dot(m.astype(bf16), pw_ref[gi].astype(bf16), preferred_element_type=f32))
            o_ref[...] = (jnp.concatenate(outs, axis=1) * ps_ref[...]).astype(bf16)

    return rowcall(cfg, fn, name, [("row", z, P, 0), ("prev", z, P, 0), ("next", z, P, 0), ("full", pw), ("full", ps)],
                   [("row", _sds((T, P), bf16), P, 0)], scratch=[pltpu.VMEM((TR + 2 * HALO, P), f32)])[0]


def pool_bwd_a(cfg, z, dmix, pw, ps, name):
    T = z.shape[0]
    TR, cT = cfg.TR, cfg.cT
    G = POOL_GROUP
    P = G * len(POOL_WINDOWS)
    L = T - cT * TR

    def fn(i, x_ref, xp_ref, xn_ref, d_ref, pw_ref, ps_ref, dm_ref, dmn_ref, dpw_ref, dps_ref, buf):
        @pl.when(i == 0)
        def _():
            dpw_ref[...] = jnp.zeros(dpw_ref.shape, f32)
            dps_ref[...] = jnp.zeros(dps_ref.shape, f32)

        @pl.when(i < cT)
        def _():
            dm_ref[...] = jnp.zeros(dm_ref.shape, f32)
            dmn_ref[...] = jnp.zeros(dmn_ref.shape, f32)

        @pl.when(i >= cT)
        def _():
            start, end = _seg_flags(cfg, i)
            _fill_halo(buf, x_ref[...], xp_ref[...], xn_ref[...], start, end, TR)
            dout = d_ref[...]
            dpre = dout * ps_ref[...]
            pres, dms, dmns = [], [], []
            for gi, w in enumerate(POOL_WINDOWS):
                m, cnt = _pool_centred(cfg, i, buf, gi, w, L)
                m_bf = m.astype(bf16)
                w_bf = pw_ref[gi].astype(bf16)
                pres.append(jnp.dot(m_bf, w_bf, preferred_element_type=f32))
                g_bf = dpre[:, gi * G:(gi + 1) * G].astype(bf16)
                dpw_ref[gi] += _dotf(m_bf, g_bf, 0, 0)
                dm = _dotf(g_bf, w_bf, 1, 1)
                dms.append(dm)
                dmns.append(dm / cnt)
            dps_ref[...] += jnp.sum(dout * jnp.concatenate(pres, axis=1), axis=0, keepdims=True)
            dm_ref[...] = jnp.concatenate(dms, axis=1)
            dmn_ref[...] = jnp.concatenate(dmns, axis=1)

    return rowcall(cfg, fn, name,
                   [("row", z, P, 0), ("prev", z, P, 0), ("next", z, P, 0), ("row", dmix, P, 0), ("full", pw), ("full", ps)],
                   [("row", _sds((T, P), f32), P, 0), ("row", _sds((T, P), f32), P, 0), ("full", _sds(pw.shape, f32)),
                    ("full", _sds((1, P), f32))], scratch=[pltpu.VMEM((TR + 2 * HALO, P), f32)])


def pool_bwd_b(cfg, dm, dmn, name):
    T, P = dm.shape
    TR, cT = cfg.TR, cfg.cT
    G = POOL_GROUP

    def fn(i, dm_ref, c_ref, p_ref, n_ref, dx_ref, buf):
        start, end = _seg_flags(cfg, i)
        _fill_halo(buf, c_ref[...], p_ref[...], n_ref[...], start, end, TR)
        outs = []
        for gi, w in enumerate(POOL_WINDOWS):
            cols = pl.ds(gi * G, G)
            tot = buf[pl.ds(HALO + w // 2, TR), cols]
            for s in range(-w // 2 + 1, w // 2):
                tot = tot + buf[pl.ds(HALO + s, TR), cols]
            outs.append(tot)
        dx_ref[...] = jnp.concatenate(outs, axis=1) - dm_ref[...]

    return rowcall(cfg, fn, name, [("row", dm, P, 0), ("row", dmn, P, 0), ("prev", dmn, P, 0), ("next", dmn, P, 0)],
                   [("row", _sds((T, P), f32), P, 0)], scratch=[pltpu.VMEM((TR + 2 * HALO, P), f32)])[0]


def _swap_halves(x):
    return pltpu.roll(x, HEAD_DIM // 2, 1)


def _headnorm(x, g):
    return _rms(x) * g


def att_prep(cfg, z, qg, kg, cosf, sinf, nq, name):
    T = z.shape[0]
    U = z.shape[1] // (nq + 3)
    nh = U // HEAD_DIM

    def fn(i, *refs):
        q_refs = refs[:nq]
        k_ref, v_ref, qg_ref, kg_ref, cos_ref, sin_ref, qn_ref, kn_ref, vb_ref = refs[nq:]
        cosv, sinv = cos_ref[...], sin_ref[...]

        def heads(x, g):
            outs = []
            for h in range(nh):
                y = _headnorm(x[:, h * HEAD_DIM:(h + 1) * HEAD_DIM], g)
                outs.append(y * cosv + _swap_halves(y) * sinv)
            return jnp.concatenate(outs, axis=1)

        qn_ref[...] = jnp.concatenate([heads(r[...], qg_ref[...]) for r in q_refs], axis=1).astype(bf16)
        kn_ref[...] = heads(k_ref[...], kg_ref[...]).astype(bf16)
        vb_ref[...] = v_ref[...].astype(bf16)

    ins = [("row", z, U, 1 + n) for n in range(nq)] + [("row", z, U, nq + 1), ("row", z, U, nq + 2), ("full", qg),
                                                       ("full", kg), ("row", cosf, HEAD_DIM, 0), ("row", sinf, HEAD_DIM, 0)]
    return rowcall(cfg, fn, name, ins, [("row", _sds((T, nq * U), bf16), nq * U, 0), ("row", _sds((T, U), bf16), U, 0),
                                        ("row", _sds((T, U), bf16), U, 0)])


def att_prep_bwd(cfg, z, qg, kg, cosf, sinf, dqn, dkn, dvb, dxpool, nq, name):
    T = z.shape[0]
    U = z.shape[1] // (nq + 3)
    nh = U // HEAD_DIM
    cT = cfg.cT

    def fn(i, *refs):
        q_refs = refs[:nq]
        (k_ref, qg_ref, kg_ref, cos_ref, sin_ref, dqn_ref, dkn_ref, dvb_ref, dxp_ref, dz_ref, dqg_ref, dkg_ref) = refs[nq:]
        cosv, sinv = cos_ref[...], sin_ref[...]

        @pl.when(i == 0)
        def _():
            dqg_ref[...] = jnp.zeros(dqg_ref.shape, f32)
            dkg_ref[...] = jnp.zeros(dkg_ref.shape, f32)

        def heads_bwd(x, g, dy, dg_ref):
            outs = []
            for h in range(nh):
                sl = slice(h * HEAD_DIM, (h + 1) * HEAD_DIM)
                d = dy[:, sl]
                dn = d * cosv + _swap_halves(d * sinv)
                _, vjp = jax.vjp(_headnorm, x[:, sl], g)
                dx, dg = vjp(dn)
                dg_ref[...] += dg
                outs.append(dx)
            return jnp.concatenate(outs, axis=1)

        dk = heads_bwd(k_ref[...], kg_ref[...], dkn_ref[...], dkg_ref)
        tail = [dk.astype(bf16), dvb_ref[...].astype(bf16)]

        @pl.when(i < cT)
        def _():
            zeros = jnp.zeros((cfg.TR, (nq + 1) * U), bf16)
            dz_ref[...] = jnp.concatenate([zeros] + tail, axis=1)

        @pl.when(i >= cT)
        def _():
            dq = [heads_bwd(r[...], qg_ref[...], dqn_ref[:, n * U:(n + 1) * U], dqg_ref) for n, r in enumerate(q_refs)]
            dz_ref[...] = jnp.concatenate([dxp_ref[...].astype(bf16)] + [t.astype(bf16) for t in dq] + tail, axis=1)

    ins = ([("row", z, U, 1 + n) for n in range(nq)] +
           [("row", z, U, nq + 1), ("full", qg), ("full", kg), ("row", cosf, HEAD_DIM, 0), ("row", sinf, HEAD_DIM, 0),
            ("row", dqn, nq * U, 0), ("row", dkn, U, 0), ("row", dvb, U, 0), ("row", dxpool, U, 0)])
    W = (nq + 3) * U
    return rowcall(cfg, fn, name, ins, [("row", _sds((T, W), bf16), W, 0), ("full", _sds((1, HEAD_DIM), f32)),
                                        ("full", _sds((1, HEAD_DIM), f32))])


def _stack_heads(x, n):
    return jnp.concatenate([x[:, h * HEAD_DIM:(h + 1) * HEAD_DIM] for h in range(n)], axis=0)


def _unstack_heads(x, n):
    rows = x.shape[0] // n
    return jnp.concatenate([x[h * rows:(h + 1) * rows] for h in range(n)], axis=1)


def _att_tiles(cfg, T):
    tq = cfg.TR
    tk = _pick(T, (4224, 2816, 1408, 768, 512, 256, 128))
    return tq, tk, (T - cfg.cT * cfg.TR) // tq, T // tk


LOG2E = 1.4426950408889634


def att_fwd(cfg, qn, kn, vb, nq, name):
    T, U = kn.shape
    KV = U // HEAD_DIM
    tq, tk, nQ, nK = _att_tiles(cfg, T)
    scale = HEAD_DIM ** -0.5
    c2 = scale * LOG2E
    R = nq * tq

    def body(q_ref, k_ref, v_ref, o_ref, lse_ref, *scratch):
        ik = pl.program_id(2)
        m_sc, l_sc, acc = scratch[:nq], scratch[nq:2 * nq], scratch[2 * nq:]

        @pl.when(ik == 0)
        def _():
            for h in range(nq):
                m_sc[h][...] = jnp.full(m_sc[h].shape, -jnp.inf, f32)
                l_sc[h][...] = jnp.zeros(l_sc[h].shape, f32)
                acc[h][...] = jnp.zeros(acc[h].shape, f32)

        k, v = k_ref[...], v_ref[...]
        for h in range(nq):
            s = _dotf(q_ref[:, h * HEAD_DIM:(h + 1) * HEAD_DIM], k, 1, 1)
            m_old = m_sc[h][...]
            m_new = jnp.maximum(m_old, jnp.max(s, axis=-1, keepdims=True))
            alpha = jnp.exp2((m_old - m_new) * c2)
            p = jnp.exp2((s - m_new) * c2)
            l_sc[h][...] = alpha * l_sc[h][...] + jnp.sum(p, axis=-1, keepdims=True)
            acc[h][...] = alpha * acc[h][...] + jnp.dot(p.astype(bf16), v, preferred_element_type=f32)
            m_sc[h][...] = m_new

        @pl.when(ik == nK - 1)
        def _():
            o_ref[...] = jnp.concatenate([acc[h][...] / l_sc[h][...] for h in range(nq)], axis=1)
            lse_ref[...] = jnp.concatenate([m_sc[h][...] * scale + jnp.log(l_sc[h][...]) for h in range(nq)], axis=0)

    W = nq * HEAD_DIM
    q_spec = pl.BlockSpec((tq, W), lambda h, i, k: (i + cfg.cT, h))
    kv_spec = pl.BlockSpec((tk, HEAD_DIM), lambda h, i, k: (k, h))
    lse_spec = pl.BlockSpec((None, None, R, 1), lambda h, i, k: (h, i, 0, 0))
    col = [pltpu.VMEM((tq, 1), f32)] * nq
    return pl.pallas_call(
        body, name=name, grid=(KV, nQ, nK), in_specs=[q_spec, kv_spec, kv_spec], out_specs=[q_spec, lse_spec],
        out_shape=[_sds((T, nq * U), f32), _sds((KV, nQ, R, 1), f32)],
        scratch_shapes=col + col + [pltpu.VMEM((tq, HEAD_DIM), f32)] * nq,
        compiler_params=_params(("arbitrary", "arbitrary", "arbitrary")))(qn, kn, vb)


def att_bwd_dq(cfg, qn, kn, vb, o, lse, do, nq, name):
    T, U = kn.shape
    KV = U // HEAD_DIM
    tq, tk, nQ, nK = _att_tiles(cfg, T)
    scale = HEAD_DIM ** -0.5
    c2 = scale * LOG2E
    R = nq * tq
    W = nq * HEAD_DIM

    def body(q_ref, k_ref, v_ref, o_ref, lse_ref, do_ref, dq_ref, acc, dl):
        ik = pl.program_id(2)

        @pl.when(ik == 0)
        def _():
            acc[...] = jnp.zeros(acc.shape, f32)
            dl[...] = jnp.sum(_stack_heads(do_ref[...] * o_ref[...], nq), axis=-1, keepdims=True)

        k, v = k_ref[...], v_ref[...]
        for h in range(nq):
            rows = pl.ds(h * tq, tq)
            cols = slice(h * HEAD_DIM, (h + 1) * HEAD_DIM)
            s = _dotf(q_ref[:, cols], k, 1, 1)
            p = jnp.exp2(s * c2 - lse_ref[rows, :] * LOG2E)
            dp = _dotf(do_ref[:, cols].astype(bf16), v, 1, 1)
            ds = (p * (dp - dl[rows, :]) * scale).astype(bf16)
            acc[rows, :] += jnp.dot(ds, k, preferred_element_type=f32)

        @pl.when(ik == nK - 1)
        def _():
            dq_ref[...] = _unstack_heads(acc[...], nq)

    q_spec = pl.BlockSpec((tq, W), lambda h, i, k: (i + cfg.cT, h))
    kv_spec = pl.BlockSpec((tk, HEAD_DIM), lambda h, i, k: (k, h))
    lse_spec = pl.BlockSpec((None, None, R, 1), lambda h, i, k: (h, i, 0, 0))
    return pl.pallas_call(
        body, name=name, grid=(KV, nQ, nK), in_specs=[q_spec, kv_spec, kv_spec, q_spec, lse_spec, q_spec], out_specs=q_spec,
        out_shape=_sds((T, nq * U), f32), scratch_shapes=[pltpu.VMEM((R, HEAD_DIM), f32), pltpu.VMEM((R, 1), f32)],
        compiler_params=_params(("arbitrary", "arbitrary", "arbitrary")))(qn, kn, vb, o, lse, do)


def att_bwd_dkv(cfg, qn, kn, vb, o, lse, do, nq, name):
    T, U = kn.shape
    KV = U // HEAD_DIM
    tq, tk, nQ, nK = _att_tiles(cfg, T)
    scale = HEAD_DIM ** -0.5
    c2 = scale * LOG2E
    R = nq * tq
    W = nq * HEAD_DIM

    def body(q_ref, k_ref, v_ref, o_ref, lse_ref, do_ref, dk_ref, dv_ref, dk_acc, dv_acc):
        iq = pl.program_id(2)

        @pl.when(iq == 0)
        def _():
            dk_acc[...] = jnp.zeros(dk_acc.shape, f32)
            dv_acc[...] = jnp.zeros(dv_acc.shape, f32)

        k, v = k_ref[...], v_ref[...]
        for h in range(nq):
            rows = pl.ds(h * tq, tq)
            cols = slice(h * HEAD_DIM, (h + 1) * HEAD_DIM)
            qh = q_ref[:, cols]
            doh = do_ref[:, cols]
            dl = jnp.sum(doh * o_ref[:, cols], axis=-1, keepdims=True)
            p = jnp.exp2(_dotf(qh, k, 1, 1) * c2 - lse_ref[rows, :] * LOG2E)
            do_bf = doh.astype(bf16)
            dv_acc[...] += _dotf(p.astype(bf16), do_bf, 0, 0)
            dp = _dotf(do_bf, v, 1, 1)
            ds = (p * (dp - dl) * scale).astype(bf16)
            dk_acc[...] += _dotf(ds, qh, 0, 0)

        @pl.when(iq == nQ - 1)
        def _():
            dk_ref[...] = dk_acc[...]
            dv_ref[...] = dv_acc[...]

    q_spec = pl.BlockSpec((tq, W), lambda h, k, i: (i + cfg.cT, h))
    kv_spec = pl.BlockSpec((tk, HEAD_DIM), lambda h, k, i: (k, h))
    lse_spec = pl.BlockSpec((None, None, R, 1), lambda h, k, i: (h, i, 0, 0))
    return pl.pallas_call(
        body, name=name, grid=(KV, nK, nQ), in_specs=[q_spec, kv_spec, kv_spec, q_spec, lse_spec, q_spec],
        out_specs=[kv_spec, kv_spec], out_shape=[_sds((T, U), f32), _sds((T, U), f32)],
        scratch_shapes=[pltpu.VMEM((tk, HEAD_DIM), f32), pltpu.VMEM((tk, HEAD_DIM), f32)],
        compiler_params=_params(("arbitrary", "arbitrary", "arbitrary")))(qn, kn, vb, o, lse, do)


def od_mix(cfg, pooled, o, name):
    T, P = pooled.shape
    QW = o.shape[1]
    cT = cfg.cT

    def fn(i, p_ref, o_ref, m_ref):
        @pl.when(i < cT)
        def _():
            m_ref[...] = jnp.zeros(m_ref.shape, bf16)

        @pl.when(i >= cT)
        def _():
            m_ref[...] = jnp.concatenate([p_ref[...], o_ref[...].astype(bf16)], axis=1)

    return rowcall(cfg, fn, name, [("row", pooled, P, 0), ("row", o, QW, 0)], [("row", _sds((T, P + QW), bf16), P + QW, 0)])[0]


def ev_mix(cfg, lru, ret, name):
    T, W = lru.shape
    RV = ret.shape[1]

    def fn(i, a_ref, b_ref, m_ref):
        m_ref[...] = jnp.concatenate([a_ref[...], b_ref[...]], axis=1)

    return rowcall(cfg, fn, name, [("row", lru, W, 0), ("row", ret, RV, 0)], [("row", _sds((T, W + RV), bf16), W + RV, 0)])[0]


def ev_dz_pack(cfg, dgl, dr, dq, dk, dv, dol, name):
    T, W = dgl.shape
    RV = dol.shape[1]
    width = 2 * W + 4 * RV

    def fn(i, g_ref, r_ref, q_ref, k_ref, v_ref, o_ref, dz_ref):
        parts = [g_ref[...], r_ref[...], q_ref[0] + q_ref[1], k_ref[0] + k_ref[1], v_ref[0] + v_ref[1], o_ref[...]]
        dz_ref[...] = jnp.concatenate([p.astype(bf16) for p in parts], axis=1)

    return rowcall(cfg, fn, name, [("row", dgl, W, 0), ("row", dr, W, 0), ("drow", dq, RV, 0), ("drow", dk, RV, 0),
                                   ("drow", dv, RV, 0), ("row", dol, RV, 0)], [("row", _sds((T, width), bf16), width, 0)])[0]


def loss_fwd_bwd(cfg, xf, target, name):
    T, D = xf.shape
    TR, cT = cfg.TR, cfg.cT

    def body(x_ref, t_ref, sq_ref, dx_ref):
        i = pl.program_id(0)

        @pl.when(i == 0)
        def _():
            sq_ref[...] = jnp.zeros(sq_ref.shape, f32)

        @pl.when(i < cT)
        def _():
            dx_ref[...] = jnp.zeros(dx_ref.shape, f32)

        @pl.when(i >= cT)
        def _():
            diff = x_ref[...] - t_ref[...]
            sq_ref[...] += jnp.sum(diff * diff, axis=0, keepdims=True)
            dx_ref[...] = diff / D

    row = pl.BlockSpec((TR, D), lambda i: (i, 0))
    trow = pl.BlockSpec((TR, D), lambda i: (jnp.maximum(i - cT, 0), 0))
    return pl.pallas_call(body, name=name, grid=(cfg.nT,), in_specs=[row, trow],
                          out_specs=[pl.BlockSpec((1, D), lambda i: (0, 0)), row],
                          out_shape=[_sds((1, D), f32), _sds((T, D), f32)], compiler_params=_params(("arbitrary",)))(xf, target)


MOD_ROWS = 16


def mod_fwd(s16, mod_w, name):
    nL, D, C4 = mod_w.shape
    tc = _pick(C4, (512, 256, 128))

    def body(s_ref, w_ref, o_ref):
        o_ref[...] = jnp.dot(s_ref[...], w_ref[...], precision=lax.Precision.HIGHEST, preferred_element_type=f32)

    return pl.pallas_call(
        body, name=name, grid=(nL, C4 // tc),
        in_specs=[pl.BlockSpec((MOD_ROWS, D), lambda l, j: (0, 0)), pl.BlockSpec((None, D, tc), lambda l, j: (l, 0, j))],
        out_specs=pl.BlockSpec((None, MOD_ROWS, tc), lambda l, j: (l, 0, j)), out_shape=_sds((nL, MOD_ROWS, C4), f32),
        compiler_params=_params(("arbitrary", "arbitrary")))(s16, mod_w)


def mod_bwd(s16, dm16, mod_w, name):
    nL, D, C4 = mod_w.shape
    tc = _pick(C4, (512, 256, 128))
    half = MOD_ROWS // 2

    def body(s_ref, d_ref, w_ref, g_ref, dc_ref):
        first = jnp.logical_and(pl.program_id(0) == 0, pl.program_id(1) == 0)
        g_ref[...] = lax.dot_general(s_ref[...], d_ref[...], (((0,), (0,)), ((), ())), precision=lax.Precision.HIGHEST,
                                     preferred_element_type=f32)
        part = lax.dot_general(d_ref[...], w_ref[...], (((1,), (1,)), ((), ())), precision=lax.Precision.HIGHEST,
                               preferred_element_type=f32)
        _acc(dc_ref, jnp.sum(part[half:], axis=0, keepdims=True), first)

    return pl.pallas_call(
        body, name=name, grid=(nL, C4 // tc),
        in_specs=[pl.BlockSpec((MOD_ROWS, D), lambda l, j: (0, 0)), pl.BlockSpec((None, MOD_ROWS, tc), lambda l, j: (l, 0, j)),
                  pl.BlockSpec((None, D, tc), lambda l, j: (l, 0, j))],
        out_specs=[pl.BlockSpec((None, D, tc), lambda l, j: (l, 0, j)), pl.BlockSpec((1, D), lambda l, j: (0, 0))],
        out_shape=[_sds((nL, D, C4), f32), _sds((1, D), f32)],
        compiler_params=_params(("arbitrary", "arbitrary")))(s16, dm16, mod_w)


def _as2d(a):
    return a.reshape(-1, a.shape[-1])


ELEMENTWISE_VMEM = 24 * 1024 * 1024


def _tiles2d(shape, n_arrays):
    R, C = shape
    tc = _pick(C, (1536, 1408, 1024, 768, 512, 256, 128))
    fits = [t for t in (512, 256, 128, 64, 32, 16, 8) if R % t == 0 and t * tc * 4 * 2 * n_arrays <= ELEMENTWISE_VMEM]
    return (fits[0] if fits else R), tc


def cast_bf16(a, name):
    a2 = _as2d(a)
    tr, tc = _tiles2d(a2.shape, 2)

    def body(a_ref, o_ref):
        o_ref[...] = a_ref[...].astype(bf16)

    spec = pl.BlockSpec((tr, tc), lambda i, j: (i, j))
    out = pl.pallas_call(body, name=name, grid=(a2.shape[0] // tr, a2.shape[1] // tc), in_specs=[spec], out_specs=spec,
                         out_shape=_sds(a2.shape, bf16), compiler_params=_params(("arbitrary", "arbitrary")))(a2)
    return out.reshape(a.shape)


def sum_leading(a, name, *, into=None, full_shape=None, widx=()):
    n = a.shape[0]
    a3 = a.reshape(n, -1, a.shape[-1])
    tr, tc = _tiles2d(a3.shape[1:], n + 1)

    def body(a_ref, *rest):
        o_ref = rest[-1]
        tot = a_ref[0].astype(f32)
        for k in range(1, n):
            tot = tot + a_ref[k].astype(f32)
        o_ref[...] = tot

    grid = (a3.shape[1] // tr, a3.shape[2] // tc)
    in_specs = [pl.BlockSpec((n, tr, tc), lambda i, j: (0, i, j))]
    args = [a3]
    if not widx:
        out = pl.pallas_call(body, name=name, grid=grid, in_specs=in_specs,
                             out_specs=pl.BlockSpec((tr, tc), lambda i, j: (i, j)), out_shape=_sds(a3.shape[1:], f32),
                             compiler_params=_params(("arbitrary", "arbitrary")))(*args)
        return out.reshape(a.shape[1:])
    lead = tuple(full_shape[:len(widx)])
    flat = lead + tuple(a3.shape[1:])
    aliases = {}
    if into is not None:
        in_specs.append(pl.BlockSpec(memory_space=pl.ANY))
        args.append(into.reshape(flat))
        aliases = {1: 0}
    out = pl.pallas_call(body, name=name, grid=grid, in_specs=in_specs,
                         out_specs=pl.BlockSpec((None,) * len(widx) + (tr, tc), lambda i, j: tuple(widx) + (i, j)),
                         out_shape=_sds(flat, f32), input_output_aliases=aliases,
                         compiler_params=_params(("arbitrary", "arbitrary")))(*args)
    return out.reshape(full_shape)


def adamw(w, m, v, g_parts, name):
    w2, m2, v2 = _as2d(w), _as2d(m), _as2d(v)
    parts = [_as2d(p) for p in g_parts]
    tr, tc = _tiles2d(w2.shape, 7 + len(parts))
    npart = len(parts)

    def body(*refs):
        w_ref, m_ref, v_ref = refs[:3]
        p_refs = refs[3:3 + npart]
        g_ref, d_ref, nm_ref, nv_ref = refs[3 + npart:]
        g = p_refs[0][...]
        for p in p_refs[1:]:
            g = g + p[...]
        mn = ADAM_B1 * m_ref[...] + (1.0 - ADAM_B1) * g
        vn = ADAM_B2 * v_ref[...] + (1.0 - ADAM_B2) * jnp.square(g)
        m_hat = mn / (1.0 - ADAM_B1 ** ADAM_STEP)
        v_hat = vn / (1.0 - ADAM_B2 ** ADAM_STEP)
        g_ref[...] = g
        d_ref[...] = -ADAM_LR * (m_hat / (jnp.sqrt(v_hat) + ADAM_EPS) + ADAM_WD * w_ref[...])
        nm_ref[...] = mn
        nv_ref[...] = vn

    spec = pl.BlockSpec((tr, tc), lambda i, j: (i, j))
    outs = pl.pallas_call(body, name=name, grid=(w2.shape[0] // tr, w2.shape[1] // tc), in_specs=[spec] * (3 + npart),
                          out_specs=[spec] * 4, out_shape=[_sds(w2.shape, f32)] * 4,
                          compiler_params=_params(("arbitrary", "arbitrary")))(w2, m2, v2, *parts)
    return [o.reshape(w.shape) for o in outs]


def all_gather_small(a, name):
    R, C = a.shape

    def body(a_ref, out_ref, send_sems, recv_sems, local_sem):
        x, y, c = _coords()
        me = 4 * x + 2 * y + c
        mine = pltpu.make_async_copy(a_ref, out_ref.at[me], local_sem)
        mine.start()
        copies = []
        for k in range(1, N_DEV):
            kx, ky, kc = (k >> 2) & 1, (k >> 1) & 1, k & 1
            peer = (_flip(x, kx), _flip(y, ky), _flip(c, kc))
            cp = pltpu.make_async_remote_copy(src_ref=a_ref, dst_ref=out_ref.at[me], send_sem=send_sems.at[k - 1],
                                              recv_sem=recv_sems.at[k - 1], device_id=peer, device_id_type=MESH)
            cp.start()
            copies.append((cp, 4 * peer[0] + 2 * peer[1] + peer[2], peer))
        for k, (cp, pidx, peer) in enumerate(copies):
            pltpu.make_async_remote_copy(src_ref=a_ref, dst_ref=out_ref.at[pidx], send_sem=send_sems.at[k],
                                         recv_sem=recv_sems.at[k], device_id=peer, device_id_type=MESH).wait_recv()
        for cp, _, _ in copies:
            cp.wait_send()
        mine.wait()

    return pl.pallas_call(
        body, name=name, out_shape=_sds((N_DEV, R, C), f32),
        in_specs=[pl.BlockSpec(memory_space=pltpu.VMEM)], out_specs=pl.BlockSpec(memory_space=pltpu.VMEM),
        scratch_shapes=[pltpu.SemaphoreType.DMA((N_DEV - 1,)), pltpu.SemaphoreType.DMA((N_DEV - 1,)), pltpu.SemaphoreType.DMA],
        compiler_params=pltpu.CompilerParams(vmem_limit_bytes=VMEM_LIMIT))(a)


def swap_with_sibling(parts, name):
    n = len(parts)

    def body(*refs):
        in_refs, out_refs = refs[:n], refs[n:2 * n]
        send_sems, recv_sems = refs[2 * n:]
        x, y, c = _coords()
        sends = []
        for w in range(n):
            cp = pltpu.make_async_remote_copy(src_ref=in_refs[w], dst_ref=out_refs[w], send_sem=send_sems.at[w],
                                              recv_sem=recv_sems.at[w], device_id=(x, y, 1 - c), device_id_type=MESH)
            cp.start()
            sends.append(cp)
        for cp in sends:
            cp.wait_recv()
        for cp in sends:
            cp.wait_send()

    hbm = pl.BlockSpec(memory_space=pl.ANY)
    return pl.pallas_call(
        body, name=name, out_shape=[_sds(a.shape, a.dtype) for a in parts], in_specs=[hbm] * n, out_specs=[hbm] * n,
        scratch_shapes=[pltpu.SemaphoreType.DMA((n,)), pltpu.SemaphoreType.DMA((n,))],
        )(*parts)


def even_fwd(cfg, x, p, tag, host=None):
    W, H = p["W"], p["H"]
    got = {}
    h = pre_fwd(cfg, x, p["g_pre"], p["shift"], p["scale"], tag + "_pre")
    z = _hosted(host, "in", got, lambda cm: matmul("v1", h, p["w_in"], comm=cm, name=tag + "_in"))
    u = conv_fwd(cfg, z, p["conv_w"], p["conv_b"], W, tag + "_conv")
    a, b = lru_coef_fwd(cfg, u, p["wa"], p["ba"], p["wx"], p["bx"], p["lam"], tag + "_coef")
    hh, hp = lru_scan(cfg, a, b, tag + "_scan")
    lru = lru_out_fwd(cfg, z, hh, W, tag + "_lruout")
    qcol = 2 * W // RET_DK
    o, st = ret_fwd(cfg, z, p["logit"], p["cos1"], p["sin1"], H, qcol, tag + "_ret")
    olcol = (2 * W + 3 * H * RET_DK) // (H * RET_DV)
    ret = ret_norm_fwd(cfg, o, z, p["gn"], H, olcol, tag + "_retnorm")
    mix = ev_mix(cfg, lru, ret, tag + "_mix")
    y = _hosted(host, "out", got, lambda cm: matmul("v2", mix, p["w_out"], comm=cm, name=tag + "_out"))
    xo = post_fwd(cfg, x, y, p["g_post"], p["gate"], 1.0, tag + "_post")
    return xo, (x, h, z, u, a, hh, hp, o, st, mix, y, olcol, qcol), got


def even_bwd(cfg, dX, saved, p, tag):
    x, h, z, u, a, hh, hp, o, st, mix, y, olcol, qcol = saved
    W, H = p["W"], p["H"]
    dy, dg_post, dgate = post_bwd(cfg, dX, y, p["g_post"], p["gate"], 1.0, tag + "_postb")
    dmix = matmul("v4", dy, p["w_out"], name=tag + "_dmix")
    g_out = matmul("v6", mix, dy, gshape=p["w_out"].shape, out_dtype=bf16, name=tag + "_gwout")
    dgl, dhs = lru_out_bwd(cfg, z, hh, dmix, W, tag + "_lruoutb")
    da, db = lru_scan_bwd(cfg, a, hp, dhs, tag + "_scanb")
    du, dwa, dba, dwx, dbx, dlam = lru_coef_bwd(cfg, u, da, db, p["wa"], p["ba"], p["wx"], p["bx"], p["lam"], tag + "_coefb")
    dr, dcw, dcb = conv_bwd(cfg, z, du, p["conv_w"], W, tag + "_convb")
    do, dol, dgn = ret_norm_bwd(cfg, o, z, p["gn"], dmix, H, olcol, W // (H * RET_DV), tag + "_retnormb")
    dq, dk, dv, dlg = ret_bwd(cfg, z, st, do, p["logit"], p["cos1"], p["sin1"], H, qcol, tag + "_retb")
    dz = ev_dz_pack(cfg, dgl, dr, dq, dk, dv, dol, tag + "_dz")
    g_in, r_out = matmul("v5", h, dz, gshape=p["w_in"].shape, out_dtype=bf16, comm=Comm("scatter", [(g_out, ())]),
                         name=tag + "_gwin")
    dh, r_in = matmul("v3", dz, p["w_in"], comm=Comm("scatter", [(g_in, ())]), name=tag + "_dh")
    dX, dg_pre, dshift, dscale = pre_bwd(cfg, x, p["g_pre"], p["shift"], p["scale"], dh, dX, tag + "_preb")
    pg = dict(g_pre=dg_pre, g_post=dg_post, shift=dshift, scale=dscale, gate=dgate, conv_w=dcw, conv_b=dcb, wa=dwa,
              ba=dba, wx=dwx, bx=dbx, lam=dlam, logit=dlg, gn=dgn)
    return dX, pg, dict(w_in=r_in[0], w_out=r_out[0])


def odd_fwd(cfg, x, p, tag, host=None):
    nq = p["nq"]
    got = {}
    h = pre_fwd(cfg, x, p["g_pre"], p["shift"], p["scale"], tag + "_pre")
    z = _hosted(host, "in", got, lambda cm: matmul("v1", h, p["w_in"], comm=cm, name=tag + "_in"))
    pooled = pool_fwd(cfg, z, p["pool_w"], p["pool_scale"], tag + "_pool")
    qn, kn, vb = att_prep(cfg, z, p["qg"], p["kg"], p["cosf"], p["sinf"], nq, tag + "_prep")
    o, lse = att_fwd(cfg, qn, kn, vb, nq, tag + "_att")
    mix = od_mix(cfg, pooled, o, tag + "_mix")
    y = matmul("v2", mix, p["w_out"], name=tag + "_out")
    xo = post_fwd(cfg, x, y, p["g_post"], p["gate"], 1.0, tag + "_post")
    return xo, (x, h, z, qn, kn, vb, o, lse, mix, y), got


def odd_bwd(cfg, dX, saved, p, tag):
    x, h, z, qn, kn, vb, o, lse, mix, y = saved
    nq = p["nq"]
    U = kn.shape[1]
    dy, dg_post, dgate = post_bwd(cfg, dX, y, p["g_post"], p["gate"], 1.0, tag + "_postb")
    dmix = matmul("v4", dy, p["w_out"], name=tag + "_dmix")
    g_out = matmul("v6", mix, dy, gshape=p["w_out"].shape, out_dtype=bf16, name=tag + "_gwout")
    dm, dmn, dpw, dps = pool_bwd_a(cfg, z, dmix, p["pool_w"], p["pool_scale"], tag + "_poolb")
    dxp = pool_bwd_b(cfg, dm, dmn, tag + "_poolb2")
    do = dmix[:, U:]
    dqn = att_bwd_dq(cfg, qn, kn, vb, o, lse, do, nq, tag + "_attdq")
    dkn, dvb = att_bwd_dkv(cfg, qn, kn, vb, o, lse, do, nq, tag + "_attdkv")
    dz, dqg, dkg = att_prep_bwd(cfg, z, p["qg"], p["kg"], p["cosf"], p["sinf"], dqn, dkn, dvb, dxp, nq, tag + "_prepb")
    g_in, r_out = matmul("v5", h, dz, gshape=p["w_in"].shape, out_dtype=bf16, comm=Comm("scatter", [(g_out, ())]),
                         name=tag + "_gwin")
    dh, r_in = matmul("v3", dz, p["w_in"], comm=Comm("scatter", [(g_in, ())]), name=tag + "_dh")
    dX, dg_pre, dshift, dscale = pre_bwd(cfg, x, p["g_pre"], p["shift"], p["scale"], dh, dX, tag + "_preb")
    pg = dict(g_pre=dg_pre, g_post=dg_post, shift=dshift, scale=dscale, gate=dgate, pool_w=dpw, pool_scale=dps, qg=dqg, kg=dkg)
    return dX, pg, dict(w_in=r_in[0], w_out=r_out[0])


WEIGHT_NAMES = ("c_ctx", "mod_w", "mod_b", "norm_pre", "norm_post", "ffn_gate", "ffn_up", "ffn_down", "ev_w_in", "ev_w_out",
                "lru_conv_w", "lru_conv_b", "lru_wa", "lru_ba", "lru_wx", "lru_bx", "lru_lambda", "ret_decay_logit", "ret_gn",
                "od_w_in", "od_w_out", "pool_w", "pool_scale", "q_norm", "k_norm")
BIG = ("ffn_gate", "ffn_up", "ffn_down", "ev_w_in", "ev_w_out", "od_w_in", "od_w_out")
SMALL_SHARDED = ("norm_pre", "norm_post", "lru_conv_w", "lru_ba", "lru_bx", "lru_lambda", "pool_scale")
SMALL_REPL = ("mod_b", "lru_conv_b", "lru_wa", "lru_wx", "ret_decay_logit", "ret_gn", "pool_w", "q_norm", "k_norm")
LANES = 128


PACK_ROWS = 512


def _rows_of(n):
    return -(-n // (8 * LANES)) * 8


def _pack(arrs):
    rows = []
    for a in arrs:
        flat = a.reshape(-1)
        rows.append(jnp.pad(flat, (0, _rows_of(flat.shape[0]) * LANES - flat.shape[0])).reshape(-1, LANES))
    total = sum(r.shape[0] for r in rows)
    rows.append(jnp.zeros(((-total) % PACK_ROWS, LANES), f32))
    return jnp.concatenate(rows), None


def _unpack(packed, shapes, lead=()):
    out, pos = [], 0
    for shp in shapes:
        n = math.prod(shp)
        r = _rows_of(n)
        piece = packed[..., pos:pos + r, :].reshape(lead + (r * LANES,))
        out.append(piece[..., :n].reshape(lead + tuple(shp)))
        pos += r
    return out


def _unshard(g):
    return jnp.moveaxis(g, 0, -2).reshape(g.shape[1:-1] + (g.shape[0] * g.shape[-1],))


def _rope_tables(S, Lc):
    n_r = RET_DK // 2
    f_r = RET_THETA ** (-jnp.arange(n_r, dtype=f32) / n_r)
    ang1 = jnp.arange(S, dtype=f32)[:, None] * f_r
    rows = S // GRID_W
    row = jnp.repeat(jnp.arange(rows, dtype=f32), GRID_W)
    col = jnp.tile(jnp.arange(GRID_W, dtype=f32), rows)
    n_ax = HEAD_DIM // 4
    f_ax = ROPE_THETA ** (-jnp.arange(n_ax, dtype=f32) / n_ax)
    ang2 = jnp.concatenate([row[:, None] * f_ax, col[:, None] * f_ax], axis=-1)
    cos2, sin2 = jnp.cos(ang2), jnp.sin(ang2)
    ones = lambda n: jnp.ones((Lc, n), f32)
    zeros = lambda n: jnp.zeros((Lc, n), f32)
    cos1 = jnp.concatenate([ones(n_r), jnp.cos(ang1)])
    sin1 = jnp.concatenate([zeros(n_r), jnp.sin(ang1)])
    cosf = jnp.concatenate([ones(HEAD_DIM), jnp.concatenate([cos2, cos2], axis=1)])
    sinf = jnp.concatenate([zeros(HEAD_DIM), jnp.concatenate([-sin2, sin2], axis=1)])
    return cos1, sin1, cosf, sinf


def kernel(x, c, ctx, c_ctx, mod_w, mod_b, norm_pre, norm_post, ffn_gate, ffn_up, ffn_down, ev_w_in, ev_w_out, lru_conv_w, lru_conv_b, lru_wa, lru_ba, lru_wx, lru_bx, lru_lambda, ret_decay_logit, ret_gn, od_w_in, od_w_out, pool_w, pool_scale, q_norm, k_norm, loss_target, m_c_ctx, m_mod_w, m_mod_b, m_norm_pre, m_norm_post, m_ffn_gate, m_ffn_up, m_ffn_down, m_ev_w_in, m_ev_w_out, m_lru_conv_w, m_lru_conv_b, m_lru_wa, m_lru_ba, m_lru_wx, m_lru_bx, m_lru_lambda, m_ret_decay_logit, m_ret_gn, m_od_w_in, m_od_w_out, m_pool_w, m_pool_scale, m_q_norm, m_k_norm, v_c_ctx, v_mod_w, v_mod_b, v_norm_pre, v_norm_post, v_ffn_gate, v_ffn_up, v_ffn_down, v_ev_w_in, v_ev_w_out, v_lru_conv_w, v_lru_conv_b, v_lru_wa, v_lru_ba, v_lru_wx, v_lru_bx, v_lru_lambda, v_ret_decay_logit, v_ret_gn, v_od_w_in, v_od_w_out, v_pool_w, v_pool_scale, v_q_norm, v_k_norm):
    wts = dict(c_ctx=c_ctx, mod_w=mod_w, mod_b=mod_b, norm_pre=norm_pre, norm_post=norm_post, ffn_gate=ffn_gate, ffn_up=ffn_up,
               ffn_down=ffn_down, ev_w_in=ev_w_in, ev_w_out=ev_w_out, lru_conv_w=lru_conv_w, lru_conv_b=lru_conv_b,
               lru_wa=lru_wa, lru_ba=lru_ba, lru_wx=lru_wx, lru_bx=lru_bx, lru_lambda=lru_lambda,
               ret_decay_logit=ret_decay_logit, ret_gn=ret_gn, od_w_in=od_w_in, od_w_out=od_w_out, pool_w=pool_w,
               pool_scale=pool_scale, q_norm=q_norm, k_norm=k_norm)
    mom_m = dict(zip(WEIGHT_NAMES, (m_c_ctx, m_mod_w, m_mod_b, m_norm_pre, m_norm_post, m_ffn_gate, m_ffn_up, m_ffn_down,
                                    m_ev_w_in, m_ev_w_out, m_lru_conv_w, m_lru_conv_b, m_lru_wa, m_lru_ba, m_lru_wx, m_lru_bx,
                                    m_lru_lambda, m_ret_decay_logit, m_ret_gn, m_od_w_in, m_od_w_out, m_pool_w, m_pool_scale,
                                    m_q_norm, m_k_norm)))
    mom_v = dict(zip(WEIGHT_NAMES, (v_c_ctx, v_mod_w, v_mod_b, v_norm_pre, v_norm_post, v_ffn_gate, v_ffn_up, v_ffn_down,
                                    v_ev_w_in, v_ev_w_out, v_lru_conv_w, v_lru_conv_b, v_lru_wa, v_lru_ba, v_lru_wx, v_lru_bx,
                                    v_lru_lambda, v_ret_decay_logit, v_ret_gn, v_od_w_in, v_od_w_out, v_pool_w, v_pool_scale,
                                    v_q_norm, v_k_norm)))

    _, S, D = x.shape
    Lc = ctx.shape[1]
    T = Lc + S
    TR = 256 if (Lc % 256 == 0 and S % 256 == 0) else 128
    assert Lc % TR == 0 and S % TR == 0 and TR % RET_CHUNK == 0
    cfg = RowCfg(TR, T // TR, Lc // TR)
    W = lru_conv_b.shape[-1]
    H = ret_decay_logit.shape[-1]
    U = POOL_GROUP * len(POOL_WINDOWS)
    nq = (N_CHIPS * od_w_in.shape[-1]) // U - 3
    assert W % (H * RET_DV) == 0 and (2 * W) % RET_DK == 0
    nL = mod_w.shape[0]
    C4 = mod_w.shape[-1]
    assert nL == 2, "two layers: an even mixer then an odd one"

    xi, yi, ci = _coords()
    chip = 2 * xi + yi
    me = 4 * xi + 2 * yi + ci

    sc = jax.nn.silu(c)
    small_in, _ = _pack([sc] + [wts[n] for n in SMALL_SHARDED])
    g1 = all_gather_small(small_in, "gather_small_fwd")
    parts = _unpack(g1, [sc.shape] + [wts[n].shape for n in SMALL_SHARDED], lead=(N_DEV,))
    sc_all = parts[0][:, 0]
    full = {n: _unshard(parts[1 + i][0::2]) for i, n in enumerate(SMALL_SHARDED)}
    for n in SMALL_REPL + ("c_ctx",):
        full[n] = wts[n]

    scc = jax.nn.silu(c_ctx)[None]
    pad_rows = MOD_ROWS - N_DEV - 1
    s16 = jnp.concatenate([sc_all, scc, jnp.zeros((pad_rows, D), f32)])
    modp = mod_fwd(s16, mod_w, "mod_fwd")
    g2 = all_gather_small(modp.reshape(-1, LANES), "gather_mod")
    mod_all = g2.reshape(N_DEV, nL, MOD_ROWS, C4)[0::2]
    mod_all = jnp.moveaxis(mod_all, 0, 2).reshape(nL, MOD_ROWS, N_CHIPS * C4) + mod_b[:, None, :]
    mod_l = lax.dynamic_index_in_dim(mod_all, me, axis=1, keepdims=False).reshape(nL, 3, 3, D)
    mod_c = mod_all[:, N_DEV].reshape(nL, 3, 3, D)

    def mod_of(li, s, kind, ctx_live=True):
        cpart = mod_c[li, s, kind] if ctx_live else jnp.zeros((D,), f32)
        return jnp.stack([cpart, mod_l[li, s, kind]])[:, None, :]

    packed = {n: cast_bf16(wts[n], "cast_" + n) for n in BIG}
    ffn_units = [(0, 0), (0, 1), (1, 0), (1, 1)]

    def G(*pieces):
        return Comm("gather", [(packed[n], idx) for n, idx in pieces])

    cos1, sin1, cosf, sinf = _rope_tables(S, Lc)

    def sub_params(li, s, ctx_live=True, gate_ctx_live=True):
        return dict(g_pre=full["norm_pre"][li, s][None], g_post=full["norm_post"][li, s][None],
                    shift=mod_of(li, s, 0, ctx_live), scale=mod_of(li, s, 1, ctx_live),
                    gate=mod_of(li, s, 2, ctx_live and gate_ctx_live))

    X0 = jnp.concatenate([ctx[0], x[0]], axis=0)
    wg00, wu00 = exchange(G(("ffn_gate", (0, 0)), ("ffn_up", (0, 0))), "gather_first")
    p00 = sub_params(0, 0)
    p00.update(wg=wg00, wu=wu00)
    h00 = {"gateup": G(("ffn_down", (0, 0)), ("ev_w_in", (0,))), "down": G(("ev_w_out", (0,)), ("ffn_gate", (0, 1)))}
    h = pre_fwd(cfg, X0, p00["g_pre"], p00["shift"], p00["scale"], "l0f0_pre")
    (a00, b00, u00), (wd00, ev_in) = ffn_gateup(cfg, h, wg00, wu00, "l0f0_gateup", comm=h00["gateup"])
    p00.update(wd=wd00)
    y00, (ev_out, wg01) = matmul("v2", u00, wd00, comm=h00["down"], name="l0f0_down")
    X1 = post_fwd(cfg, X0, y00, p00["g_post"], p00["gate"], FFN_STEP, "l0f0_post")
    s00 = (X0, h, a00, b00, u00, y00)

    p01 = sub_params(0, 1)
    p01.update(W=W, H=H, conv_w=full["lru_conv_w"][0], conv_b=full["lru_conv_b"], wa=full["lru_wa"][0],
               ba=full["lru_ba"][0][:, None, :], wx=full["lru_wx"][0], bx=full["lru_bx"][0][:, None, :],
               lam=full["lru_lambda"][0][:, None, :], logit=full["ret_decay_logit"][0][:, :, None, None],
               gn=full["ret_gn"], cos1=cos1, sin1=sin1, w_in=ev_in, w_out=ev_out)
    X2, s01, got = even_fwd(cfg, X1, p01, "l0mix", host={"in": G(("ffn_up", (0, 1))), "out": G(("ffn_down", (0, 1)))})
    p02 = sub_params(0, 2)
    p02.update(wg=wg01, wu=got["in"][0], wd=got["out"][0])
    X3, s02, got = ffn_fwd(cfg, X2, p02, "l0f1", host={"gateup": G(("ffn_gate", (1, 0)), ("ffn_up", (1, 0))),
                                                       "down": G(("ffn_down", (1, 0)))})
    p10 = sub_params(1, 0)
    p10.update(wg=got["gateup"][0], wu=got["gateup"][1], wd=got["down"][0])
    X4, s10, got = ffn_fwd(cfg, X3, p10, "l1f0", host={"gateup": G(("od_w_in", (0,)), ("od_w_out", (0,)), ("ffn_gate", (1, 1))),
                                                       "down": G(("ffn_up", (1, 1)))})
    p11 = sub_params(1, 1, gate_ctx_live=False)
    p11.update(nq=nq, pool_w=full["pool_w"][0], pool_scale=full["pool_scale"], qg=full["q_norm"], kg=full["k_norm"],
               cosf=cosf, sinf=sinf, w_in=got["gateup"][0], w_out=got["gateup"][1])
    p12 = sub_params(1, 2, ctx_live=False)
    p12.update(wg=got["gateup"][2], wu=got["down"][0])
    X5, s11, got = odd_fwd(cfg, X4, p11, "l1mix", host={"in": G(("ffn_down", (1, 1)))})
    p12.update(wd=got["in"][0])
    X6, s12, _ = ffn_fwd(cfg, X5, p12, "l1f1")
    sq, dX = loss_fwd_bwd(cfg, X6, loss_target[0], "loss")
    loss = lax.psum(0.5 * jnp.sum(sq) / D, ("x", "y", "c"))

    recv_ffn = {}
    dX, g12, recv_ffn[(1, 1)] = ffn_bwd(cfg, dX, s12, p12, "l1f1")
    dX, g11, recv_od = odd_bwd(cfg, dX, s11, p11, "l1mix")
    dX, g10, recv_ffn[(1, 0)] = ffn_bwd(cfg, dX, s10, p10, "l1f0")
    dX, g02, recv_ffn[(0, 1)] = ffn_bwd(cfg, dX, s02, p02, "l0f1")
    dX, g01, recv_ev = even_bwd(cfg, dX, s01, p01, "l0mix")
    dX, g00, recv_ffn[(0, 0)] = ffn_bwd(cfg, dX, s00, p00, "l0f0")
    grad_x = dX[Lc:][None]

    subs = [[g00, g01, g02], [g10, g11, g12]]
    zero_d = jnp.zeros((D,), f32)

    def dmod(group, live):
        rows = []
        for li in range(nL):
            for s in range(3):
                for kind, key in enumerate(("shift", "scale", "gate")):
                    rows.append(subs[li][s][key][group, 0] if live(li, s, kind) else zero_d)
        return jnp.stack(rows).reshape(nL, 9 * D)

    dmod_l = dmod(1, lambda li, s, kind: True)
    dmod_c = dmod(0, lambda li, s, kind: not (li == 1 and (s == 2 or (s == 1 and kind == 2))))

    dm_in, _ = _pack([dmod_l, dmod_c])
    g3 = all_gather_small(dm_in, "gather_dmod")
    dl_all, dc_all = _unpack(g3, [dmod_l.shape, dmod_c.shape], lead=(N_DEV,))
    dm16 = jnp.moveaxis(jnp.concatenate([dl_all, dc_all], axis=0), 0, 1)
    dm16 = lax.dynamic_slice_in_dim(dm16, chip * C4, C4, axis=2)
    s16b = jnp.concatenate([sc_all, jnp.broadcast_to(scc, (N_DEV, D))])
    g_mod_w, dscc_part = mod_bwd(s16b, dm16, mod_w, "mod_bwd")

    norm_pre_g = jnp.stack([jnp.concatenate([subs[li][s]["g_pre"] for s in range(3)]) for li in range(nL)])
    norm_post_g = jnp.stack([jnp.concatenate([subs[li][s]["g_post"] for s in range(3)]) for li in range(nL)])
    small_g = dict(norm_pre=norm_pre_g, norm_post=norm_post_g, lru_conv_w=g01["conv_w"][None], lru_ba=g01["ba"][:, 0][None],
                   lru_bx=g01["bx"][:, 0][None], lru_lambda=g01["lam"][:, 0][None], pool_scale=g11["pool_scale"],
                   mod_b=dmod_l + dmod_c, lru_conv_b=g01["conv_b"], lru_wa=g01["wa"][None], lru_wx=g01["wx"][None],
                   ret_decay_logit=g01["logit"][:, :, 0, 0][None], ret_gn=g01["gn"], pool_w=g11["pool_w"][None],
                   q_norm=g11["qg"], k_norm=g11["kg"])
    names = SMALL_SHARDED + SMALL_REPL
    sg_in, _ = _pack([small_g[n] for n in names] + [dscc_part])
    g4 = all_gather_small(sg_in, "gather_small_grads")
    tot = sum_leading(g4, "sum_small_grads")
    tot_parts = _unpack(tot, [small_g[n].shape for n in names])
    dscc_all = _unpack(g4, [small_g[n].shape for n in names] + [dscc_part.shape], lead=(N_DEV,))[-1]
    dscc = dscc_all[0, 0] + dscc_all[2, 0] + dscc_all[4, 0] + dscc_all[6, 0]
    _, silu_vjp = jax.vjp(jax.nn.silu, c_ctx)
    grads = {"c_ctx": silu_vjp(dscc)[0]}
    for n, g in zip(names, tot_parts):
        if n in SMALL_SHARDED:
            k = wts[n].shape[-1]
            g = lax.dynamic_slice_in_dim(g, chip * k, k, axis=g.ndim - 1)
        grads[n] = g.reshape(wts[n].shape)

    partial = {}
    for n, key in (("ffn_gate", "wg"), ("ffn_up", "wu"), ("ffn_down", "wd")):
        acc = None
        for u in ffn_units:
            acc = sum_leading(recv_ffn[u][key], "sum_%s_%d%d" % (n, u[0], u[1]), into=acc, full_shape=wts[n].shape, widx=u)
        partial[n] = acc
    for n, r in (("ev_w_in", recv_ev["w_in"]), ("ev_w_out", recv_ev["w_out"]), ("od_w_in", recv_od["w_in"]),
                 ("od_w_out", recv_od["w_out"])):
        partial[n] = sum_leading(r, "sum_" + n).reshape(wts[n].shape)
    partial = [partial[n] for n in BIG]
    other = swap_with_sibling(partial, "swap_partials")

    delta, new_m, new_v = {}, {}, {}
    for n, pa, pb in zip(BIG, partial, other):
        grads[n], delta[n], new_m[n], new_v[n] = adamw(wts[n], mom_m[n], mom_v[n], [pa, pb], "adamw_" + n)
    grads["mod_w"], delta["mod_w"], new_m["mod_w"], new_v["mod_w"] = adamw(mod_w, m_mod_w, v_mod_w, [g_mod_w], "adamw_mod_w")
    snames = [n for n in WEIGHT_NAMES if n not in BIG and n != "mod_w"]
    pk = lambda d: _pack([d[n] for n in snames])[0]
    sres = adamw(pk(wts), pk(mom_m), pk(mom_v), [pk(grads)], "adamw_small")
    for res, dst in zip(sres[1:], (delta, new_m, new_v)):
        for n, a in zip(snames, _unpack(res, [wts[n].shape for n in snames])):
            dst[n] = a

    return (loss, grad_x, *[grads[n] for n in WEIGHT_NAMES], *[delta[n] for n in WEIGHT_NAMES],
            *[new_m[n] for n in WEIGHT_NAMES], *[new_v[n] for n in WEIGHT_NAMES])
```

```python
import functools
import math

import jax
import jax.numpy as jnp
from jax import lax
from jax.experimental import pallas as pl
from jax.experimental.pallas import tpu as pltpu

f32 = jnp.float32
bf16 = jnp.bfloat16
MESH = pl.DeviceIdType.MESH

EPS = 1e-6
FFN_STEP = 0.5
LRU_C = 8.0
CONV_W = 4
CONV_LEFT = 2
RET_DK = 256
RET_DV = 256
RET_CHUNK = 128
RET_THETA = 10000.0
POOL_WINDOWS = (2, 4, 8, 16)
POOL_GROUP = 128
HEAD_DIM = 128
ROPE_THETA = 10000.0
GRID_W = 64
ADAM_LR = 0.001
ADAM_B1 = 0.9
ADAM_B2 = 0.999
ADAM_EPS = 1e-08
ADAM_WD = 0.01
ADAM_STEP = 10

N_CHIPS = 4
N_DEV = 8
HALO = 8
VMEM_LIMIT = 56 * 1024 * 1024


def _params(sem=None):
    return pltpu.CompilerParams(dimension_semantics=sem, vmem_limit_bytes=VMEM_LIMIT)


def _pick(n, prefs):
    for p in prefs:
        if n % p == 0:
            return p
    return n


def _sds(shape, dtype):
    return jax.ShapeDtypeStruct(tuple(shape), dtype)


MATMUL_VMEM = 46 * 1024 * 1024


def _fit_rows(M, tm, tn, out_dtype):
    fixed = 2 * tm * tn * 4 + 2 * tm * tn * jnp.dtype(out_dtype).itemsize
    for rows in (1408, 768, 512, 256, 128):
        if M % rows == 0 and fixed + 2 * rows * (tm + tn) * 2 <= MATMUL_VMEM:
            return rows
    return M


_MM_KINDS = {
    "v1": ((1, 0), "out[:, g] = A @ W[g]"),
    "v2": ((1, 0), "out = sum_g A[:, g] @ W[g]"),
    "v3": ((1, 1), "out = sum_g A[:, g] @ W[g]^T"),
    "v4": ((1, 1), "out[:, g] = A @ W[g]^T"),
    "v5": ((0, 0), "out[g] = A^T @ C[:, g]"),
    "v6": ((0, 0), "out[g] = A[:, g]^T @ C"),
}


def matmul(kind, a, b, *, widx=(), out_dtype=f32, init=None, gshape=None, comm=None, name):
    nw = len(widx)
    cdims = _MM_KINDS[kind][0]
    if kind in ("v1", "v2", "v3", "v4"):
        G = b.shape[0]
        d1, d2 = b.shape[-2:]
        M = a.shape[0]
    else:
        G, d1, d2 = gshape
        M = a.shape[0]
    tm_p, tn_p, tk_p = (768, 512, 256, 128), (1408, 1536, 1024, 768, 512, 256, 128), (2048, 1408, 1536, 1024, 768, 512, 256, 128)
    wnone = (None,) * (1 + nw)

    if kind == "v1":
        K, Ns = d1, d2
        tm, tn, tk = _pick(M, tm_p), _pick(Ns, tn_p), _pick(K, tk_p)
        nI, nJ, nR = M // tm, Ns // tn, K // tk
        grid = (G, nI, nJ, nR)
        a_spec = pl.BlockSpec((tm, tk), lambda g, i, j, r: (i, r))
        b_spec = pl.BlockSpec(wnone + (tk, tn), lambda g, i, j, r: (g,) + widx + (r, j))
        o_spec = pl.BlockSpec((tm, tn), lambda g, i, j, r: (i, g * nJ + j))
        out_shape = _sds((M, G * Ns), out_dtype)
        acc_shape = (tm, tn)
    elif kind == "v2":
        Ks, N = d1, d2
        tm, tn, tk = _pick(M, tm_p), _pick(N, (2048,) + tn_p), _pick(Ks, tk_p)
        nI, nJ, nRk = M // tm, N // tn, Ks // tk
        nR = G * nRk
        grid = (1, nI, nJ, nR)
        a_spec = pl.BlockSpec((tm, tk), lambda g, i, j, r: (i, r))
        b_spec = pl.BlockSpec(wnone + (tk, tn), lambda g, i, j, r: (r // nRk,) + widx + (r % nRk, j))
        o_spec = pl.BlockSpec((tm, tn), lambda g, i, j, r: (i, j))
        out_shape = _sds((M, N), out_dtype)
        acc_shape = (tm, tn)
    elif kind == "v3":
        K, Ns = d1, d2
        tm, tn, tk = _pick(M, tm_p), _pick(K, (2048,) + tn_p), _pick(Ns, tk_p)
        nI, nJ, nRk = M // tm, K // tn, Ns // tk
        nR = G * nRk
        grid = (1, nI, nJ, nR)
        a_spec = pl.BlockSpec((tm, tk), lambda g, i, j, r: (i, r))
        b_spec = pl.BlockSpec(wnone + (tn, tk), lambda g, i, j, r: (r // nRk,) + widx + (j, r % nRk))
        o_spec = pl.BlockSpec((tm, tn), lambda g, i, j, r: (i, j))
        out_shape = _sds((M, K), out_dtype)
        acc_shape = (tm, tn)
    elif kind == "v4":
        Ks, N = d1, d2
        tm, tn, tk = _pick(M, tm_p), _pick(Ks, tn_p), _pick(N, tk_p)
        nI, nJ, nR = M // tm, Ks // tn, N // tk
        grid = (G, nI, nJ, nR)
        a_spec = pl.BlockSpec((tm, tk), lambda g, i, j, r: (i, r))
        b_spec = pl.BlockSpec(wnone + (tn, tk), lambda g, i, j, r: (g,) + widx + (j, r))
        o_spec = pl.BlockSpec((tm, tn), lambda g, i, j, r: (i, g * nJ + j))
        out_shape = _sds((M, G * Ks), out_dtype)
        acc_shape = (tm, tn)
    elif kind == "v5":
        K, Ns = d1, d2
        tm, tn, tk = _pick(K, (2048,) + tm_p), _pick(Ns, tn_p), 0
        tk = _fit_rows(M, tm, tn, out_dtype)
        nI, nJ, nR = K // tm, Ns // tn, M // tk
        grid = (G, nI, nJ, nR)
        a_spec = pl.BlockSpec((tk, tm), lambda g, i, j, r: (r, i))
        b_spec = pl.BlockSpec((tk, tn), lambda g, i, j, r: (r, g * nJ + j))
        o_spec = pl.BlockSpec((None, tm, tn), lambda g, i, j, r: (g, i, j))
        out_shape = _sds(gshape, out_dtype)
        acc_shape = (tm, tn)
    else:
        Ks, N = d1, d2
        tm, tn, tk = _pick(Ks, (1408,) + tm_p), _pick(N, (2048,) + tn_p), 0
        tk = _fit_rows(M, tm, tn, out_dtype)
        nI, nJ, nR = Ks // tm, N // tn, M // tk
        grid = (G, nI, nJ, nR)
        a_spec = pl.BlockSpec((tk, tm), lambda g, i, j, r: (r, g * nI + i))
        b_spec = pl.BlockSpec((tk, tn), lambda g, i, j, r: (r, j))
        o_spec = pl.BlockSpec((None, tm, tn), lambda g, i, j, r: (g, i, j))
        out_shape = _sds(gshape, out_dtype)
        acc_shape = (tm, tn)

    has_init = init is not None
    ncomm = len(comm.srcs) if comm is not None else 0

    def body(*refs):
        a_ref, b_ref = refs[0], refs[1]
        pos = 2
        init_ref = None
        if has_init:
            init_ref = refs[pos]
            pos += 1
        cin = refs[pos:pos + ncomm]
        pos += ncomm
        o_ref = refs[pos]
        cout = refs[pos + 1:pos + 1 + ncomm]
        acc_ref = refs[pos + 1 + ncomm]
        sems = refs[pos + 2 + ncomm:]
        r = pl.program_id(3)
        first, last = _grid_ends(grid)

        if ncomm:
            @pl.when(first)
            def _():
                _comm_start(comm, cin, cout, *sems)

        def prod():
            return lax.dot_general(a_ref[...], b_ref[...], ((cdims[:1], cdims[1:]), ((), ())), preferred_element_type=f32)

        def start():
            return init_ref[...] + prod() if has_init else prod()

        if nR == 1:
            o_ref[...] = start().astype(o_ref.dtype)
        else:
            @pl.when(r == 0)
            def _():
                acc_ref[...] = start()

            @pl.when(jnp.logical_and(r > 0, r < nR - 1))
            def _():
                acc_ref[...] += prod()

            @pl.when(r == nR - 1)
            def _():
                o_ref[...] = (acc_ref[...] + prod()).astype(o_ref.dtype)

        if ncomm:
            @pl.when(last)
            def _():
                _comm_wait(comm, cin, cout, *sems)

    in_specs = [a_spec, b_spec]
    args = [a, b]
    if has_init:
        in_specs.append(pl.BlockSpec((tm, tn), lambda g, i, j, r: (i, j)))
        args.append(init)
    out_specs, out_shapes, scratch = [o_spec], [out_shape], [pltpu.VMEM(acc_shape, f32)]
    if ncomm:
        hbm = pl.BlockSpec(memory_space=pl.ANY)
        in_specs += [hbm] * ncomm
        args += [src for src, _ in comm.srcs]
        out_specs += [hbm] * ncomm
        out_shapes += comm.out_shapes()
        scratch += _comm_sems(ncomm)
    res = pl.pallas_call(
        body, name=name, grid=grid, in_specs=in_specs, out_specs=out_specs, out_shape=out_shapes,
        scratch_shapes=scratch, compiler_params=_params(("arbitrary", "arbitrary", "arbitrary", "arbitrary")),
    )(*args)
    return (res[0], list(res[1:])) if ncomm else res[0]


class Comm:
    def __init__(self, mode, srcs):
        self.mode, self.srcs = mode, srcs

    def piece(self, n):
        arr, idx = self.srcs[n]
        shp = arr.shape[len(idx):]
        return shp if self.mode == "gather" else shp[1:]

    def out_shapes(self):
        return [_sds((N_CHIPS,) + tuple(self.piece(n)), self.srcs[n][0].dtype) for n in range(len(self.srcs))]


def _comm_sems(n):
    nsem = n * (N_CHIPS - 1)
    return [pltpu.SemaphoreType.DMA((nsem,)), pltpu.SemaphoreType.DMA((nsem,)), pltpu.SemaphoreType.DMA((n,))]


def _coords():
    return lax.axis_index("x"), lax.axis_index("y"), lax.axis_index("c")


def _flip(v, bit):
    return 1 - v if bit else v


def _chip_peers(x, y, c):
    out = []
    for k in range(1, N_CHIPS):
        kx, ky = (k >> 1) & 1, k & 1
        px, py = _flip(x, kx), _flip(y, ky)
        out.append((k, (px, py, c), 2 * px + py))
    return out


def _comm_copies(comm, in_refs, out_refs, send_sems, recv_sems, local_sems, with_recvs):
    x, y, c = _coords()
    s = 2 * x + y
    local, sends, recvs = [], [], []
    for w, (_, idx) in enumerate(comm.srcs):
        src = in_refs[w].at[idx] if idx else in_refs[w]
        out = out_refs[w]
        if comm.mode == "gather":
            local.append(pltpu.make_async_copy(src, out.at[s], local_sems.at[w]))
        else:
            local.append(pltpu.make_async_copy(src.at[s], out.at[N_CHIPS - 1], local_sems.at[w]))
        for k, peer, pidx in _chip_peers(x, y, c):
            j = w * (N_CHIPS - 1) + k - 1
            if comm.mode == "gather":
                out_src, out_dst, in_dst = src, out.at[s], out.at[pidx]
            else:
                out_src, out_dst, in_dst = src.at[pidx], out.at[k - 1], out.at[k - 1]
            sends.append(pltpu.make_async_remote_copy(src_ref=out_src, dst_ref=out_dst, send_sem=send_sems.at[j],
                                                      recv_sem=recv_sems.at[j], device_id=peer, device_id_type=MESH))
            if with_recvs:
                recvs.append(pltpu.make_async_remote_copy(src_ref=out_src, dst_ref=in_dst, send_sem=send_sems.at[j],
                                                          recv_sem=recv_sems.at[j], device_id=peer, device_id_type=MESH))
    return local, sends, recvs


def _comm_start(comm, in_refs, out_refs, send_sems, recv_sems, local_sems):
    local, sends, _ = _comm_copies(comm, in_refs, out_refs, send_sems, recv_sems, local_sems, False)
    for cp in local + sends:
        cp.start()


def _comm_wait(comm, in_refs, out_refs, send_sems, recv_sems, local_sems):
    local, sends, recvs = _comm_copies(comm, in_refs, out_refs, send_sems, recv_sems, local_sems, True)
    for cp in recvs:
        cp.wait_recv()
    for cp in sends:
        cp.wait_send()
    for cp in local:
        cp.wait()


def exchange(comm, name):
    n = len(comm.srcs)

    def body(*refs):
        in_refs, out_refs, sems = refs[:n], refs[n:2 * n], refs[2 * n:]
        _comm_start(comm, in_refs, out_refs, *sems)
        _comm_wait(comm, in_refs, out_refs, *sems)

    hbm = pl.BlockSpec(memory_space=pl.ANY)
    return pl.pallas_call(body, name=name, out_shape=comm.out_shapes(), in_specs=[hbm] * n, out_specs=[hbm] * n,
                          scratch_shapes=_comm_sems(n))(*[src for src, _ in comm.srcs])


class RowCfg:
    def __init__(self, TR, nT, cT):
        self.TR, self.nT, self.cT = TR, nT, cT


def _row_spec(cfg, spec, off):
    kind = spec[0]
    TR = cfg.TR
    hb = TR // HALO
    nH = cfg.nT * hb
    if kind == "row":
        _, arr, w, cb = spec
        return pl.BlockSpec((TR, w), lambda i: (i + off, cb))
    if kind == "prev":
        _, arr, w, cb = spec
        return pl.BlockSpec((HALO, w), lambda i: (jnp.maximum((i + off) * hb - 1, 0), cb))
    if kind == "next":
        _, arr, w, cb = spec
        return pl.BlockSpec((HALO, w), lambda i: (jnp.minimum((i + off + 1) * hb, nH - 1), cb))
    if kind == "full":
        arr = spec[1]
        nd = arr.ndim
        return pl.BlockSpec(arr.shape, lambda i: (0,) * nd)
    if kind == "grp":
        arr = spec[1]
        cT = cfg.cT
        return pl.BlockSpec((None, 1, arr.shape[-1]), lambda i: (((i + off) >= cT).astype(jnp.int32), 0, 0))
    if kind == "drow":
        _, arr, w, cb = spec
        return pl.BlockSpec((arr.shape[0], TR, w), lambda i: (0, i + off, cb))
    raise ValueError(kind)


def rowcall(cfg, fn, name, ins, outs, *, off=0, n=None, scratch=()):
    n = cfg.nT - off if n is None else n
    in_specs = [_row_spec(cfg, s, off) for s in ins]
    out_specs = [_row_spec(cfg, (s[0], s[1]) + tuple(s[2:]), off) for s in outs]
    out_shape = [s[1] for s in outs]

    def body(*refs):
        fn(pl.program_id(0) + off, *refs)

    res = pl.pallas_call(
        body, name=name, grid=(n,), in_specs=in_specs, out_specs=out_specs, out_shape=out_shape,
        scratch_shapes=list(scratch), compiler_params=_params(("arbitrary",)),
    )(*[s[1] for s in ins])
    return res


def _acc(ref, val, first):
    @pl.when(first)
    def _():
        ref[...] = val

    @pl.when(jnp.logical_not(first))
    def _():
        ref[...] += val


def _rms(x):
    return x * lax.rsqrt(jnp.mean(x * x, axis=-1, keepdims=True) + EPS)


def _pre_fn(x, g, shift, scale):
    return (_rms(x) * g) * (1.0 + scale) + shift


def _strips(TR, rows_per_strip, body, init):
    n = TR // rows_per_strip
    unroll = STRIP_UNROLL if n % STRIP_UNROLL == 0 else 1

    def step(r, carry):
        for u in range(unroll):
            start = pl.multiple_of((r * unroll + u) * rows_per_strip, rows_per_strip)
            carry = body(pl.ds(start, rows_per_strip), carry)
        return carry

    return lax.fori_loop(0, n // unroll, step, init)


STRIP_UNROLL = 8
F32_STRIP = 8
BF16_STRIP = 16


def _inv_rms(x):
    return lax.rsqrt(jnp.mean(x * x, axis=-1, keepdims=True) + EPS)


def pre_fwd(cfg, x, g, shift, scale, name):
    D = x.shape[1]

    def fn(i, x_ref, g_ref, sh_ref, sc_ref, h_ref):
        c = g_ref[...] * (1.0 + sc_ref[...])
        sh = sh_ref[...]

        def strip(rows, carry):
            xv = x_ref[rows, :]
            h_ref[rows, :] = (xv * _inv_rms(xv) * c + sh).astype(bf16)
            return carry

        _strips(cfg.TR, BF16_STRIP, strip, 0)

    return rowcall(cfg, fn, name, [("row", x, D, 0), ("full", g), ("grp", shift), ("grp", scale)],
                   [("row", _sds(x.shape, bf16), D, 0)])[0]


def pre_bwd(cfg, x, g, shift, scale, dh, dx_in, name):
    D = x.shape[1]
    cT = cfg.cT

    def fn(i, x_ref, g_ref, sh_ref, sc_ref, dh_ref, dxin_ref, dx_ref, dg_ref, dsh_ref, dsc_ref):
        gv, scv = g_ref[...], sc_ref[...]
        c = gv * (1.0 + scv)

        def strip(rows, carry):
            s0, s1 = carry
            xv, dhv = x_ref[rows, :], dh_ref[rows, :]
            r = _inv_rms(xv)
            xn = xv * r
            dxn = dhv * c
            m = jnp.mean(dxn * xn, axis=-1, keepdims=True)
            dx_ref[rows, :] = dxin_ref[rows, :] + r * (dxn - xn * m)
            return s0 + dhv, s1 + dhv * xn

        zero = jnp.zeros((F32_STRIP, D), f32)
        s0, s1 = _strips(cfg.TR, F32_STRIP, strip, (zero, zero))
        s0 = jnp.sum(s0, axis=0, keepdims=True)
        s1 = jnp.sum(s1, axis=0, keepdims=True)
        _acc(dg_ref, (1.0 + scv) * s1, i == 0)
        first = jnp.logical_or(i == 0, i == cT)
        _acc(dsh_ref, s0, first)
        _acc(dsc_ref, gv * s1, first)

    return rowcall(cfg, fn, name,
                   [("row", x, D, 0), ("full", g), ("grp", shift), ("grp", scale), ("row", dh, D, 0), ("row", dx_in, D, 0)],
                   [("row", _sds(x.shape, f32), D, 0), ("full", _sds((1, D), f32)),
                    ("grp", _sds((2, 1, D), f32)), ("grp", _sds((2, 1, D), f32))])


def _post_fn(w, y, g, gate):
    return (w * gate) * (_rms(y) * g)


def post_fwd(cfg, x, y, g, gate, w, name):
    D = x.shape[1]

    def fn(i, x_ref, y_ref, g_ref, gt_ref, o_ref):
        c = (w * gt_ref[...]) * g_ref[...]

        def strip(rows, carry):
            yv = y_ref[rows, :]
            o_ref[rows, :] = x_ref[rows, :] + c * (yv * _inv_rms(yv))
            return carry

        _strips(cfg.TR, F32_STRIP, strip, 0)

    return rowcall(cfg, fn, name, [("row", x, D, 0), ("row", y, D, 0), ("full", g), ("grp", gate)],
                   [("row", _sds(x.shape, f32), D, 0)])[0]


def post_bwd(cfg, dx, y, g, gate, w, name):
    D = dx.shape[1]
    cT = cfg.cT
    half = BF16_STRIP // 2

    def fn(i, dx_ref, y_ref, g_ref, gt_ref, dy_ref, dg_ref, dgt_ref):
        gv, gtv = g_ref[...], gt_ref[...]
        c = (w * gtv) * gv

        def strip(rows, s1):
            yv, dv = y_ref[rows, :], dx_ref[rows, :]
            r = _inv_rms(yv)
            yn = yv * r
            dyn = dv * c
            m = jnp.mean(dyn * yn, axis=-1, keepdims=True)
            dy_ref[rows, :] = (r * (dyn - yn * m)).astype(bf16)
            t = dv * yn
            return s1 + t[:half] + t[half:]

        s1 = _strips(cfg.TR, BF16_STRIP, strip, jnp.zeros((half, D), f32))
        s1 = jnp.sum(s1, axis=0, keepdims=True)
        _acc(dg_ref, (w * gtv) * s1, i == 0)
        _acc(dgt_ref, (w * gv) * s1, jnp.logical_or(i == 0, i == cT))

    return rowcall(cfg, fn, name, [("row", dx, D, 0), ("row", y, D, 0), ("full", g), ("grp", gate)],
                   [("row", _sds(dx.shape, bf16), D, 0), ("full", _sds((1, D), f32)), ("grp", _sds((2, 1, D), f32))])


def _swiglu_fn(a, b):
    return jax.nn.silu(a) * b


def swiglu_fwd(cfg, a, b, name):
    F = a.shape[1]
    tf = _pick(F, (1408, 1024, 512, 256, 128))
    TR = cfg.TR

    def body(a_ref, b_ref, u_ref):
        u_ref[...] = _swiglu_fn(a_ref[...], b_ref[...]).astype(bf16)

    spec = pl.BlockSpec((TR, tf), lambda i, j: (i, j))
    return pl.pallas_call(body, name=name, grid=(cfg.nT, F // tf), in_specs=[spec, spec], out_specs=spec,
                          out_shape=_sds(a.shape, bf16), compiler_params=_params(("arbitrary", "arbitrary")))(a, b)


def swiglu_bwd(cfg, a, b, du, name):
    F = a.shape[1]
    tf = _pick(F, (1408, 1024, 512, 256, 128))
    TR = cfg.TR

    def body(a_ref, b_ref, du_ref, da_ref, db_ref):
        _, vjp = jax.vjp(_swiglu_fn, a_ref[...], b_ref[...])
        da, db = vjp(du_ref[...])
        da_ref[...] = da.astype(bf16)
        db_ref[...] = db.astype(bf16)

    spec = pl.BlockSpec((TR, tf), lambda i, j: (i, j))
    return pl.pallas_call(body, name=name, grid=(cfg.nT, F // tf), in_specs=[spec, spec, spec], out_specs=[spec, spec],
                          out_shape=[_sds(a.shape, bf16), _sds(a.shape, bf16)],
                          compiler_params=_params(("arbitrary", "arbitrary")))(a, b, du)


def _hosted(host, role, got, fn):
    comm = host.get(role) if host else None
    if comm is None:
        return fn(None)
    out, res = fn(comm)
    got[role] = res
    return out


def _grid_ends(grid):
    ids = [pl.program_id(n) for n in range(len(grid))]
    first = functools.reduce(jnp.logical_and, [i == 0 for i in ids])
    last = functools.reduce(jnp.logical_and, [i == n - 1 for i, n in zip(ids, grid)])
    return first, last


def _comm_plumbing(comm):
    if comm is None:
        return [], [], [], [], []
    n = len(comm.srcs)
    hbm = pl.BlockSpec(memory_space=pl.ANY)
    return [hbm] * n, [src for src, _ in comm.srcs], [hbm] * n, comm.out_shapes(), _comm_sems(n)


FFN_ROWS = 384


def ffn_gateup(cfg, h, wg, wu, name, comm=None):
    M, K = h.shape
    G, _, F = wg.shape
    tm = _pick(M, (FFN_ROWS, 256, 128))
    grid = (G, M // tm)
    ncomm = len(comm.srcs) if comm is not None else 0

    def body(h_ref, wg_ref, wu_ref, *rest):
        cin, rest = rest[:ncomm], rest[ncomm:]
        s_ref, t_ref, u_ref = rest[:3]
        cout, sems = rest[3:3 + ncomm], rest[3 + ncomm:]
        first, last = _grid_ends(grid)
        if ncomm:
            @pl.when(first)
            def _():
                _comm_start(comm, cin, cout, *sems)

        hv = h_ref[...]
        a = jnp.dot(hv, wg_ref[...], preferred_element_type=f32)
        b = jnp.dot(hv, wu_ref[...], preferred_element_type=f32)
        sig = jax.nn.sigmoid(a)
        sa = a * sig
        s_ref[...] = sa.astype(bf16)
        t_ref[...] = (b * (sig + sa * (1.0 - sig))).astype(bf16)
        u_ref[...] = (sa * b).astype(bf16)
        if ncomm:
            @pl.when(last)
            def _():
                _comm_wait(comm, cin, cout, *sems)

    ci, ca, co, cs, csem = _comm_plumbing(comm)
    w_spec = pl.BlockSpec((None, K, F), lambda g, i: (g, 0, 0))
    o_spec = pl.BlockSpec((tm, F), lambda g, i: (i, g))
    res = pl.pallas_call(
        body, name=name, grid=grid, in_specs=[pl.BlockSpec((tm, K), lambda g, i: (i, 0)), w_spec, w_spec] + ci,
        out_specs=[o_spec] * 3 + co, out_shape=[_sds((M, G * F), bf16)] * 3 + cs, scratch_shapes=csem,
        compiler_params=_params(("arbitrary", "arbitrary")))(h, wg, wu, *ca)
    return (res[0], res[1], res[2]), list(res[3:])


def ffn_du_act(cfg, dy, wd, s, t, name, comm=None):
    M, N = dy.shape
    G, F, _ = wd.shape
    tm = _pick(M, (FFN_ROWS, 256, 128))
    grid = (G, M // tm)
    ncomm = len(comm.srcs) if comm is not None else 0

    def body(dy_ref, wd_ref, s_ref, t_ref, *rest):
        cin, rest = rest[:ncomm], rest[ncomm:]
        da_ref, db_ref = rest[:2]
        cout, sems = rest[2:2 + ncomm], rest[2 + ncomm:]
        first, last = _grid_ends(grid)
        if ncomm:
            @pl.when(first)
            def _():
                _comm_start(comm, cin, cout, *sems)

        du = _dotf(dy_ref[...], wd_ref[...], 1, 1)
        da_ref[...] = (du * t_ref[...].astype(f32)).astype(bf16)
        db_ref[...] = (du * s_ref[...].astype(f32)).astype(bf16)
        if ncomm:
            @pl.when(last)
            def _():
                _comm_wait(comm, cin, cout, *sems)

    ci, ca, co, cs, csem = _comm_plumbing(comm)
    t_spec = pl.BlockSpec((tm, F), lambda g, i: (i, g))
    res = pl.pallas_call(
        body, name=name, grid=grid,
        in_specs=[pl.BlockSpec((tm, N), lambda g, i: (i, 0)), pl.BlockSpec((None, F, N), lambda g, i: (g, 0, 0)), t_spec,
                  t_spec] + ci,
        out_specs=[t_spec, t_spec] + co, out_shape=[_sds((M, G * F), bf16)] * 2 + cs, scratch_shapes=csem,
        compiler_params=_params(("arbitrary", "arbitrary")))(dy, wd, s, t, *ca)
    return (res[0], res[1]), list(res[2:])


def ffn_dh(cfg, da, db, wg, wu, name, comm=None):
    M = da.shape[0]
    G, K, F = wg.shape
    tm = _pick(M, (768, 512, 256, 128))
    nR = 2 * G
    grid = (M // tm, nR)
    ncomm = len(comm.srcs) if comm is not None else 0

    def body(da_ref, db_ref, wg_ref, wu_ref, *rest):
        cin, rest = rest[:ncomm], rest[ncomm:]
        o_ref = rest[0]
        cout, acc_ref, sems = rest[1:1 + ncomm], rest[1 + ncomm], rest[2 + ncomm:]
        r = pl.program_id(1)
        first, last = _grid_ends(grid)
        if ncomm:
            @pl.when(first)
            def _():
                _comm_start(comm, cin, cout, *sems)

        @pl.when(r == 0)
        def _():
            acc_ref[...] = _dotf(da_ref[...], wg_ref[...], 1, 1)

        @pl.when(jnp.logical_and(r > 0, r < G))
        def _():
            acc_ref[...] += _dotf(da_ref[...], wg_ref[...], 1, 1)

        @pl.when(jnp.logical_and(r >= G, r < nR - 1))
        def _():
            acc_ref[...] += _dotf(db_ref[...], wu_ref[...], 1, 1)

        @pl.when(r == nR - 1)
        def _():
            o_ref[...] = acc_ref[...] + _dotf(db_ref[...], wu_ref[...], 1, 1)

        if ncomm:
            @pl.when(last)
            def _():
                _comm_wait(comm, cin, cout, *sems)

    ga = lambda r: jnp.minimum(r, G - 1)
    gb = lambda r: jnp.maximum(r - G, 0)
    ci, ca, co, cs, csem = _comm_plumbing(comm)
    res = pl.pallas_call(
        body, name=name, grid=grid,
        in_specs=[pl.BlockSpec((tm, F), lambda i, r: (i, ga(r))), pl.BlockSpec((tm, F), lambda i, r: (i, gb(r))),
                  pl.BlockSpec((None, K, F), lambda i, r: (ga(r), 0, 0)),
                  pl.BlockSpec((None, K, F), lambda i, r: (gb(r), 0, 0))] + ci,
        out_specs=[pl.BlockSpec((tm, K), lambda i, r: (i, 0))] + co, out_shape=[_sds((M, K), f32)] + cs,
        scratch_shapes=[pltpu.VMEM((tm, K), f32)] + csem,
        compiler_params=_params(("arbitrary", "arbitrary")))(da, db, wg, wu, *ca)
    return res[0], list(res[1:])


def ffn_fwd(cfg, x, p, tag, host=None):
    got = {}
    host = host or {}
    h = pre_fwd(cfg, x, p["g_pre"], p["shift"], p["scale"], tag + "_pre")
    (a, b, u), res = ffn_gateup(cfg, h, p["wg"], p["wu"], tag + "_gateup", comm=host.get("gateup"))
    if res:
        got["gateup"] = res
    y = _hosted(host, "down", got, lambda cm: matmul("v2", u, p["wd"], comm=cm, name=tag + "_down"))
    xo = post_fwd(cfg, x, y, p["g_post"], p["gate"], FFN_STEP, tag + "_post")
    return xo, (x, h, a, b, u, y), got


def ffn_bwd(cfg, dX, saved, p, tag):
    x, h, a, b, u, y = saved
    dy, dg_post, dgate = post_bwd(cfg, dX, y, p["g_post"], p["gate"], FFN_STEP, tag + "_postb")
    (da, db), _ = ffn_du_act(cfg, dy, p["wd"], a, b, tag + "_duact")
    gwd = matmul("v6", u, dy, gshape=p["wd"].shape, out_dtype=bf16, name=tag + "_gwd")
    gwg = matmul("v5", h, da, gshape=p["wg"].shape, out_dtype=bf16, name=tag + "_gwg")
    gwu, r_wd = matmul("v5", h, db, gshape=p["wu"].shape, out_dtype=bf16, comm=Comm("scatter", [(gwd, ())]),
                       name=tag + "_gwu")
    dh, (r_wg, r_wu) = ffn_dh(cfg, da, db, p["wg"], p["wu"], tag + "_dh", comm=Comm("scatter", [(gwg, ()), (gwu, ())]))
    dX, dg_pre, dshift, dscale = pre_bwd(cfg, x, p["g_pre"], p["shift"], p["scale"], dh, dX, tag + "_preb")
    small = dict(g_pre=dg_pre, g_post=dg_post, shift=dshift, scale=dscale, gate=dgate)
    return dX, small, dict(wg=r_wg, wu=r_wu, wd=r_wd[0])


def _seg_flags(cfg, i):
    start = jnp.logical_or(i == 0, i == cfg.cT)
    end = jnp.logical_or(i == cfg.cT - 1, i == cfg.nT - 1)
    return start, end


def _fill_halo(buf, cur, prev, nxt, start, end, TR):
    buf[pl.ds(0, HALO), :] = jnp.where(start, 0.0, prev)
    buf[pl.ds(HALO, TR), :] = cur
    buf[pl.ds(HALO + TR, HALO), :] = jnp.where(end, 0.0, nxt)


def conv_fwd(cfg, z, cw, cb, W, name):
    TR = cfg.TR

    def fn(i, r_ref, rp_ref, rn_ref, cw_ref, cb_ref, u_ref, buf):
        start, end = _seg_flags(cfg, i)
        _fill_halo(buf, r_ref[...], rp_ref[...], rn_ref[...], start, end, TR)
        u = jnp.broadcast_to(cb_ref[...], (TR, W))
        for k in range(CONV_W):
            u = u + buf[pl.ds(HALO + k - CONV_LEFT, TR), :] * cw_ref[pl.ds(k, 1), :]
        u_ref[...] = u

    return rowcall(cfg, fn, name, [("row", z, W, 1), ("prev", z, W, 1), ("next", z, W, 1), ("full", cw), ("full", cb)],
                   [("row", _sds((z.shape[0], W), f32), W, 0)], scratch=[pltpu.VMEM((TR + 2 * HALO, W), f32)])[0]


def conv_bwd(cfg, z, du, cw, W, name):
    TR = cfg.TR

    def fn(i, r_ref, rp_ref, rn_ref, du_ref, dup_ref, dun_ref, cw_ref, dr_ref, dcw_ref, dcb_ref, rbuf, dbuf):
        start, end = _seg_flags(cfg, i)
        _fill_halo(rbuf, r_ref[...], rp_ref[...], rn_ref[...], start, end, TR)
        _fill_halo(dbuf, du_ref[...], dup_ref[...], dun_ref[...], start, end, TR)
        du = du_ref[...]

        @pl.when(i == 0)
        def _():
            dcw_ref[...] = jnp.zeros(dcw_ref.shape, f32)
            dcb_ref[...] = jnp.zeros(dcb_ref.shape, f32)

        dr = jnp.zeros((TR, W), f32)
        for k in range(CONV_W):
            dr = dr + dbuf[pl.ds(HALO - (k - CONV_LEFT), TR), :] * cw_ref[pl.ds(k, 1), :]
            dcw_ref[pl.ds(k, 1), :] += jnp.sum(du * rbuf[pl.ds(HALO + k - CONV_LEFT, TR), :], axis=0, keepdims=True)
        dcb_ref[...] += jnp.sum(du, axis=0, keepdims=True)
        dr_ref[...] = dr

    T = z.shape[0]
    return rowcall(cfg, fn, name,
                   [("row", z, W, 1), ("prev", z, W, 1), ("next", z, W, 1), ("row", du, W, 0), ("prev", du, W, 0),
                    ("next", du, W, 0), ("full", cw)],
                   [("row", _sds((T, W), f32), W, 0), ("full", _sds((CONV_W, W), f32)), ("full", _sds((1, W), f32))],
                   scratch=[pltpu.VMEM((TR + 2 * HALO, W), f32), pltpu.VMEM((TR + 2 * HALO, W), f32)])


def _softplus(x):
    return jnp.maximum(x, 0.0) + jnp.log1p(jnp.exp(-jnp.abs(x)))


def _neg_expm1(x):
    series = -x * (1.0 + x * (0.5 + x * (1.0 / 6.0 + x * (1.0 / 24.0 + x * (1.0 / 120.0 + x * (1.0 / 720.0))))))
    return jnp.where(x > -0.1, series, 1.0 - jnp.exp(x))


def _lru_coef(u, pa, px, lam):
    r = jax.nn.sigmoid(pa)
    i = jax.nn.sigmoid(px)
    log_a = -LRU_C * r * _softplus(-lam)
    a = jnp.exp(log_a)
    b = jnp.sqrt(_neg_expm1(2.0 * log_a)) * (i * u)
    return a, b


def _blockdiag(u_bf, w_ref, d, nblk, blk):
    return jnp.concatenate(
        [jnp.dot(u_bf[:, n * blk:(n + 1) * blk], w_ref[d, n].astype(bf16), preferred_element_type=f32)
         for n in range(nblk)], axis=1)


def lru_coef_fwd(cfg, u, wa, ba, wx, bx, lam, name):
    T, W = u.shape
    nblk, blk = wa.shape[1], wa.shape[2]

    def fn(i, u_ref, wa_ref, ba_ref, wx_ref, bx_ref, lam_ref, a_ref, b_ref):
        uv = u_ref[...]
        u_bf = uv.astype(bf16)
        for d in range(2):
            pa = _blockdiag(u_bf, wa_ref, d, nblk, blk) + ba_ref[d]
            px = _blockdiag(u_bf, wx_ref, d, nblk, blk) + bx_ref[d]
            a, b = _lru_coef(uv, pa, px, lam_ref[d])
            a_ref[d] = a
            b_ref[d] = b

    return rowcall(cfg, fn, name, [("row", u, W, 0), ("full", wa), ("full", ba), ("full", wx), ("full", bx), ("full", lam)],
                   [("drow", _sds((2, T, W), f32), W, 0), ("drow", _sds((2, T, W), f32), W, 0)])


def lru_coef_bwd(cfg, u, da, db, wa, ba, wx, bx, lam, name):
    T, W = u.shape
    nblk, blk = wa.shape[1], wa.shape[2]

    def fn(i, u_ref, da_ref, db_ref, wa_ref, ba_ref, wx_ref, bx_ref, lam_ref,
           du_ref, dwa_ref, dba_ref, dwx_ref, dbx_ref, dlam_ref):
        @pl.when(i == 0)
        def _():
            for r in (dwa_ref, dba_ref, dwx_ref, dbx_ref, dlam_ref):
                r[...] = jnp.zeros(r.shape, f32)

        uv = u_ref[...]
        u_bf = uv.astype(bf16)
        du = jnp.zeros(uv.shape, f32)
        for d in range(2):
            pa = _blockdiag(u_bf, wa_ref, d, nblk, blk) + ba_ref[d]
            px = _blockdiag(u_bf, wx_ref, d, nblk, blk) + bx_ref[d]
            _, vjp = jax.vjp(_lru_coef, uv, pa, px, lam_ref[d])
            du_e, dpa, dpx, dlam = vjp((da_ref[d], db_ref[d]))
            du = du + du_e
            dba_ref[d] += jnp.sum(dpa, axis=0, keepdims=True)
            dbx_ref[d] += jnp.sum(dpx, axis=0, keepdims=True)
            dlam_ref[d] += dlam
            parts = []
            for n in range(nblk):
                sl = slice(n * blk, (n + 1) * blk)
                ga, gx = dpa[:, sl].astype(bf16), dpx[:, sl].astype(bf16)
                ub = u_bf[:, sl]
                dwa_ref[d, n] += lax.dot_general(ub, ga, (((0,), (0,)), ((), ())), preferred_element_type=f32)
                dwx_ref[d, n] += lax.dot_general(ub, gx, (((0,), (0,)), ((), ())), preferred_element_type=f32)
                parts.append(
                    lax.dot_general(ga, wa_ref[d, n].astype(bf16), (((1,), (1,)), ((), ())), preferred_element_type=f32)
                    + lax.dot_general(gx, wx_ref[d, n].astype(bf16), (((1,), (1,)), ((), ())), preferred_element_type=f32))
            du = du + jnp.concatenate(parts, axis=1)
        du_ref[...] = du

    return rowcall(cfg, fn, name,
                   [("row", u, W, 0), ("drow", da, W, 0), ("drow", db, W, 0), ("full", wa), ("full", ba), ("full", wx),
                    ("full", bx), ("full", lam)],
                   [("row", _sds((T, W), f32), W, 0), ("full", _sds(wa.shape, f32)), ("full", _sds(ba.shape, f32)),
                    ("full", _sds(wx.shape, f32)), ("full", _sds(bx.shape, f32)), ("full", _sds(lam.shape, f32))])


def _dir_tile(cfg, d, j):
    rev = jnp.where(j < cfg.cT, cfg.cT - 1 - j, cfg.nT - 1 - (j - cfg.cT))
    return jnp.where(d == 0, j, rev)


def lru_scan(cfg, a, b, name):
    _, T, W = a.shape
    TR, nT = cfg.TR, cfg.nT

    def body(a_ref, b_ref, h_ref, hp_ref, st):
        d, j = pl.program_id(0), pl.program_id(1)

        @pl.when(j == 0)
        def _():
            st[...] = jnp.zeros(st.shape, f32)

        def step(t, h):
            idx = t + d * (TR - 1 - 2 * t)
            hn = a_ref[pl.ds(idx, 1), :] * h + b_ref[pl.ds(idx, 1), :]
            hp_ref[pl.ds(idx, 1), :] = h
            h_ref[pl.ds(idx, 1), :] = hn
            return hn

        st[...] = lax.fori_loop(0, TR, step, st[...])

    spec = pl.BlockSpec((None, TR, W), lambda d, j: (d, _dir_tile(cfg, d, j), 0))
    return pl.pallas_call(body, name=name, grid=(2, nT), in_specs=[spec, spec], out_specs=[spec, spec],
                          out_shape=[_sds(a.shape, f32), _sds(a.shape, f32)], scratch_shapes=[pltpu.VMEM((1, W), f32)],
                          compiler_params=_params(("arbitrary", "arbitrary")))(a, b)


def lru_scan_bwd(cfg, a, hp, dh, name):
    _, T, W = a.shape
    TR, nT = cfg.TR, cfg.nT

    def body(a_ref, hp_ref, dh_ref, da_ref, db_ref, st):
        d, j = pl.program_id(0), pl.program_id(1)

        @pl.when(j == 0)
        def _():
            st[...] = jnp.zeros(st.shape, f32)

        def step(t, c):
            p = TR - 1 - t
            idx = p + d * (TR - 1 - 2 * p)
            g = dh_ref[pl.ds(idx, 1), :] + c
            db_ref[pl.ds(idx, 1), :] = g
            da_ref[pl.ds(idx, 1), :] = g * hp_ref[pl.ds(idx, 1), :]
            return a_ref[pl.ds(idx, 1), :] * g

        st[...] = lax.fori_loop(0, TR, step, st[...])

    spec = pl.BlockSpec((None, TR, W), lambda d, j: (d, _dir_tile(cfg, d, nT - 1 - j), 0))
    dspec = pl.BlockSpec((TR, W), lambda d, j: (_dir_tile(cfg, d, nT - 1 - j), 0))
    return pl.pallas_call(body, name=name, grid=(2, nT), in_specs=[spec, spec, dspec], out_specs=[spec, spec],
                          out_shape=[_sds(a.shape, f32), _sds(a.shape, f32)], scratch_shapes=[pltpu.VMEM((1, W), f32)],
                          compiler_params=_params(("arbitrary", "arbitrary")))(a, hp, dh)


def _lru_out_fn(gl, h0, h1):
    return jax.nn.gelu(gl) * (h0 + h1)


def lru_out_fwd(cfg, z, h, W, name):
    def fn(i, g_ref, h_ref, o_ref):
        o_ref[...] = _lru_out_fn(g_ref[...], h_ref[0], h_ref[1]).astype(bf16)

    return rowcall(cfg, fn, name, [("row", z, W, 0), ("drow", h, W, 0)], [("row", _sds((z.shape[0], W), bf16), W, 0)])[0]


def lru_out_bwd(cfg, z, h, dmix, W, name):
    def fn(i, g_ref, h_ref, d_ref, dg_ref, dh_ref):
        _, vjp = jax.vjp(_lru_out_fn, g_ref[...], h_ref[0], h_ref[1])
        dg, dh0, _ = vjp(d_ref[...])
        dg_ref[...] = dg
        dh_ref[...] = dh0

    T = z.shape[0]
    return rowcall(cfg, fn, name, [("row", z, W, 0), ("drow", h, W, 0), ("row", dmix, W, 0)],
                   [("row", _sds((T, W), f32), W, 0), ("row", _sds((T, W), f32), W, 0)])


def _rot_half(x, cos, sin):
    n = x.shape[1] // 2
    x1, x2 = x[:, :n], x[:, n:]
    return jnp.concatenate([x1 * cos - x2 * sin, x1 * sin + x2 * cos], axis=1)


def _dotf(a, b, ca, cb):
    return lax.dot_general(a, b, (((ca,), (cb,)), ((), ())), preferred_element_type=f32)


@functools.partial(jax.custom_vjp, nondiff_argnums=(2, 3))
def _dotb(a, b, ca, cb):
    return _dotf(a.astype(bf16), b.astype(bf16), ca, cb)


def _dotb_fwd(a, b, ca, cb):
    return _dotb(a, b, ca, cb), (a, b)


def _dotb_bwd(ca, cb, res, ct):
    a, b = res
    a16, b16, ct16 = a.astype(bf16), b.astype(bf16), ct.astype(bf16)
    da = _dotf(ct16, b16, 1, 1 - cb) if ca == 1 else _dotf(b16, ct16, 1 - cb, 1)
    db = _dotf(a16, ct16, 1 - ca, 0) if cb == 0 else _dotf(ct16, a16, 0, 1 - ca)
    return da, db


_dotb.defvjp(_dotb_fwd, _dotb_bwd)


def _ret_chunk(d, q, k, v, s, logit, cos, sin):
    C = q.shape[0]
    lg = -_softplus(-logit)
    qr = _rot_half(q, cos, sin)
    kr = _rot_half(k, cos, sin) * (RET_DK ** -0.5)
    ii = lax.broadcasted_iota(jnp.int32, (C, C), 0)
    jj = lax.broadcasted_iota(jnp.int32, (C, C), 1)
    diff = ((ii - jj) if d == 0 else (jj - ii)).astype(f32)
    intra = jnp.where(diff >= 0, jnp.exp(lg * jnp.maximum(diff, 0.0)), 0.0)
    pos = lax.broadcasted_iota(jnp.int32, (C, 1), 0).astype(f32)
    if d == 0:
        q_dec, k_dec = jnp.exp(lg * (pos + 1.0)), jnp.exp(lg * (C - 1.0 - pos))
    else:
        q_dec, k_dec = jnp.exp(lg * (C - pos)), jnp.exp(lg * pos)
    s_dec = jnp.exp(lg * C)
    scores = _dotb(qr, kr, 1, 1) * intra
    o = _dotb(scores, v, 1, 0) + _dotb(qr * q_dec, s, 1, 0)
    s_new = s * s_dec + _dotb(kr * k_dec, v, 0, 0)
    return o, s_new


def _chunk_cfg(cfg):
    f = cfg.TR // RET_CHUNK
    return RowCfg(RET_CHUNK, cfg.nT * f, cfg.cT * f)


def ret_fwd(cfg, z, logit, cos, sin, H, qcol, name):
    T = z.shape[0]
    cc = _chunk_cfg(cfg)
    C, nC = RET_CHUNK, cc.nT
    RV = H * RET_DV
    qb = qcol * RET_DK // RV

    def body(q_ref, k_ref, v_ref, lg_ref, cos_ref, sin_ref, o_ref, s_ref, st):
        d, j = pl.program_id(0), pl.program_id(1)

        @pl.when(j == 0)
        def _():
            st[...] = jnp.zeros(st.shape, f32)

        s_ref[...] = st[...]
        for dd in range(2):
            @pl.when(d == dd)
            def _():
                outs = []
                for h in range(H):
                    cols = slice(h * RET_DK, (h + 1) * RET_DK)
                    o, s_new = _ret_chunk(dd, q_ref[:, cols], k_ref[:, cols], v_ref[:, cols], st[h], lg_ref[h],
                                          cos_ref[...], sin_ref[...])
                    outs.append(o)
                    st[h] = s_new
                o_ref[...] = jnp.concatenate(outs, axis=1)

    tile = lambda d, j: _dir_tile(cc, d, j)
    zq = pl.BlockSpec((C, RV), lambda d, j: (tile(d, j), qb))
    zk = pl.BlockSpec((C, RV), lambda d, j: (tile(d, j), qb + 1))
    zv = pl.BlockSpec((C, RV), lambda d, j: (tile(d, j), qb + 2))
    lgs = pl.BlockSpec((None, H, 1, 1), lambda d, j: (d, 0, 0, 0))
    cs = pl.BlockSpec((C, RET_DK // 2), lambda d, j: (tile(d, j), 0))
    o_spec = pl.BlockSpec((None, C, RV), lambda d, j: (d, tile(d, j), 0))
    s_spec = pl.BlockSpec((None, H, None, RET_DK, RET_DV), lambda d, j: (d, 0, tile(d, j), 0, 0))
    return pl.pallas_call(
        body, name=name, grid=(2, nC), in_specs=[zq, zk, zv, lgs, cs, cs], out_specs=[o_spec, s_spec],
        out_shape=[_sds((2, T, RV), f32), _sds((2, H, nC, RET_DK, RET_DV), f32)],
        scratch_shapes=[pltpu.VMEM((H, RET_DK, RET_DV), f32)],
        compiler_params=_params(("arbitrary", "arbitrary")))(z, z, z, logit, cos, sin)


def ret_bwd(cfg, z, states, do, logit, cos, sin, H, qcol, name):
    T = z.shape[0]
    cc = _chunk_cfg(cfg)
    C, nC = RET_CHUNK, cc.nT
    RV = H * RET_DV
    qb = qcol * RET_DK // RV

    def body(q_ref, k_ref, v_ref, s_ref, do_ref, lg_ref, cos_ref, sin_ref, dq_ref, dk_ref, dv_ref, dlg_ref, st):
        d, j = pl.program_id(0), pl.program_id(1)

        @pl.when(j == 0)
        def _():
            st[...] = jnp.zeros(st.shape, f32)
            dlg_ref[...] = jnp.zeros(dlg_ref.shape, f32)

        for dd in range(2):
            @pl.when(d == dd)
            def _():
                fn = lambda q, k, v, s, lg: _ret_chunk(dd, q, k, v, s, lg, cos_ref[...], sin_ref[...])
                dqs, dks, dvs = [], [], []
                for h in range(H):
                    cols = slice(h * RET_DK, (h + 1) * RET_DK)
                    _, vjp = jax.vjp(fn, q_ref[:, cols], k_ref[:, cols], v_ref[:, cols], s_ref[h], lg_ref[h])
                    dq, dk, dv, ds, dlg = vjp((do_ref[:, cols], st[h]))
                    dqs.append(dq)
                    dks.append(dk)
                    dvs.append(dv)
                    st[h] = ds
                    dlg_ref[h] += dlg
                dq_ref[...] = jnp.concatenate(dqs, axis=1)
                dk_ref[...] = jnp.concatenate(dks, axis=1)
                dv_ref[...] = jnp.concatenate(dvs, axis=1)

    tile = lambda d, j: _dir_tile(cc, d, nC - 1 - j)
    zq = pl.BlockSpec((C, RV), lambda d, j: (tile(d, j), qb))
    zk = pl.BlockSpec((C, RV), lambda d, j: (tile(d, j), qb + 1))
    zv = pl.BlockSpec((C, RV), lambda d, j: (tile(d, j), qb + 2))
    s_spec = pl.BlockSpec((None, H, None, RET_DK, RET_DV), lambda d, j: (d, 0, tile(d, j), 0, 0))
    do_spec = pl.BlockSpec((C, RV), lambda d, j: (tile(d, j), 0))
    lgs = pl.BlockSpec((None, H, 1, 1), lambda d, j: (d, 0, 0, 0))
    cs = pl.BlockSpec((C, RET_DK // 2), lambda d, j: (tile(d, j), 0))
    g_spec = pl.BlockSpec((None, C, RV), lambda d, j: (d, tile(d, j), 0))
    gshape = _sds((2, T, RV), f32)
    return pl.pallas_call(
        body, name=name, grid=(2, nC), in_specs=[zq, zk, zv, s_spec, do_spec, lgs, cs, cs],
        out_specs=[g_spec, g_spec, g_spec, lgs], out_shape=[gshape, gshape, gshape, _sds((2, H, 1, 1), f32)],
        scratch_shapes=[pltpu.VMEM((H, RET_DK, RET_DV), f32)],
        compiler_params=_params(("arbitrary", "arbitrary")))(z, z, z, states, do, logit, cos, sin)


def _ret_norm_fn(H, o0, o1, ol, gn):
    o = o0 + o1
    parts = []
    for h in range(H):
        x = o[:, h * RET_DV:(h + 1) * RET_DV]
        mu = jnp.mean(x, axis=-1, keepdims=True)
        var = jnp.mean(jnp.square(x - mu), axis=-1, keepdims=True)
        parts.append((x - mu) * lax.rsqrt(var + EPS))
    return (jnp.concatenate(parts, axis=1) * gn) * jax.nn.silu(ol)


def ret_norm_fwd(cfg, o, z, gn, H, olcol, name):
    RV = H * RET_DV

    def fn(i, o_ref, ol_ref, gn_ref, r_ref):
        r_ref[...] = _ret_norm_fn(H, o_ref[0], o_ref[1], ol_ref[...], gn_ref[...]).astype(bf16)

    return rowcall(cfg, fn, name, [("drow", o, RV, 0), ("row", z, RV, olcol), ("full", gn)],
                   [("row", _sds((z.shape[0], RV), bf16), RV, 0)])[0]


def ret_norm_bwd(cfg, o, z, gn, dmix, H, olcol, dcol, name):
    RV = H * RET_DV
    T = z.shape[0]

    def fn(i, o_ref, ol_ref, gn_ref, d_ref, do_ref, dol_ref, dgn_ref):
        _, vjp = jax.vjp(functools.partial(_ret_norm_fn, H), o_ref[0], o_ref[1], ol_ref[...], gn_ref[...])
        do, _, dol, dgn = vjp(d_ref[...])
        do_ref[...] = do
        dol_ref[...] = dol
        _acc(dgn_ref, dgn, i == 0)

    return rowcall(cfg, fn, name, [("drow", o, RV, 0), ("row", z, RV, olcol), ("full", gn), ("row", dmix, RV, dcol)],
                   [("row", _sds((T, RV), f32), RV, 0), ("row", _sds((T, RV), f32), RV, 0), ("full", _sds((1, RV), f32))])


def _pool_geom(cfg, i, w, L):
    t = (i - cfg.cT) * cfg.TR + lax.broadcasted_iota(jnp.int32, (cfg.TR, 1), 0)
    lo = jnp.clip(t - w // 2, 0, L)
    hi = jnp.clip(t + w // 2, 0, L)
    return (hi - lo).astype(f32)


def _pool_centred(cfg, i, buf, gi, w, L):
    TR, G = cfg.TR, POOL_GROUP
    cols = pl.ds(gi * G, G)
    tot = buf[pl.ds(HALO - w // 2, TR), cols]
    for s in range(-w // 2 + 1, w // 2):
        tot = tot + buf[pl.ds(HALO + s, TR), cols]
    cnt = _pool_geom(cfg, i, w, L)
    return tot / cnt - buf[pl.ds(HALO, TR), cols], cnt


def pool_fwd(cfg, z, pw, ps, name):
    T = z.shape[0]
    TR, cT = cfg.TR, cfg.cT
    P = POOL_GROUP * len(POOL_WINDOWS)
    L = T - cT * TR

    def fn(i, x_ref, xp_ref, xn_ref, pw_ref, ps_ref, o_ref, buf):
        @pl.when(i < cT)
        def _():
            o_ref[...] = jnp.zeros(o_ref.shape, bf16)

        @pl.when(i >= cT)
        def _():
            start, end = _seg_flags(cfg, i)
            _fill_halo(buf, x_ref[...], xp_ref[...], xn_ref[...], start, end, TR)
            outs = []
            for gi, w in enumerate(POOL_WINDOWS):
                m, _ = _pool_centred(cfg, i, buf, gi, w, L)
                outs.append(jnp.dot(m.astype(bf16), pw_ref[gi].astype(bf16), preferred_element_type=f32))
            o_ref[...] = (jnp.concatenate(outs, axis=1) * ps_ref[...]).astype(bf16)

    return rowcall(cfg, fn, name, [("row", z, P, 0), ("prev", z, P, 0), ("next", z, P, 0), ("full", pw), ("full", ps)],
                   [("row", _sds((T, P), bf16), P, 0)], scratch=[pltpu.VMEM((TR + 2 * HALO, P), f32)])[0]


def pool_bwd_a(cfg, z, dmix, pw, ps, name):
    T = z.shape[0]
    TR, cT = cfg.TR, cfg.cT
    G = POOL_GROUP
    P = G * len(POOL_WINDOWS)
    L = T - cT * TR

    def fn(i, x_ref, xp_ref, xn_ref, d_ref, pw_ref, ps_ref, dm_ref, dmn_ref, dpw_ref, dps_ref, buf):
        @pl.when(i == 0)
        def _():
            dpw_ref[...] = jnp.zeros(dpw_ref.shape, f32)
            dps_ref[...] = jnp.zeros(dps_ref.shape, f32)

        @pl.when(i < cT)
        def _():
            dm_ref[...] = jnp.zeros(dm_ref.shape, f32)
            dmn_ref[...] = jnp.zeros(dmn_ref.shape, f32)

        @pl.when(i >= cT)
        def _():
            start, end = _seg_flags(cfg, i)
            _fill_halo(buf, x_ref[...], xp_ref[...], xn_ref[...], start, end, TR)
            dout = d_ref[...]
            dpre = dout * ps_ref[...]
            pres, dms, dmns = [], [], []
            for gi, w in enumerate(POOL_WINDOWS):
                m, cnt = _pool_centred(cfg, i, buf, gi, w, L)
                m_bf = m.astype(bf16)
                w_bf = pw_ref[gi].astype(bf16)
                pres.append(jnp.dot(m_bf, w_bf, preferred_element_type=f32))
                g_bf = dpre[:, gi * G:(gi + 1) * G].astype(bf16)
                dpw_ref[gi] += _dotf(m_bf, g_bf, 0, 0)
                dm = _dotf(g_bf, w_bf, 1, 1)
                dms.append(dm)
                dmns.append(dm / cnt)
            dps_ref[...] += jnp.sum(dout * jnp.concatenate(pres, axis=1), axis=0, keepdims=True)
            dm_ref[...] = jnp.concatenate(dms, axis=1)
            dmn_ref[...] = jnp.concatenate(dmns, axis=1)

    return rowcall(cfg, fn, name,
                   [("row", z, P, 0), ("prev", z, P, 0), ("next", z, P, 0), ("row", dmix, P, 0), ("full", pw), ("full", ps)],
                   [("row", _sds((T, P), f32), P, 0), ("row", _sds((T, P), f32), P, 0), ("full", _sds(pw.shape, f32)),
                    ("full", _sds((1, P), f32))], scratch=[pltpu.VMEM((TR + 2 * HALO, P), f32)])


def pool_bwd_b(cfg, dm, dmn, name):
    T, P = dm.shape
    TR, cT = cfg.TR, cfg.cT
    G = POOL_GROUP

    def fn(i, dm_ref, c_ref, p_ref, n_ref, dx_ref, buf):
        start, end = _seg_flags(cfg, i)
        _fill_halo(buf, c_ref[...], p_ref[...], n_ref[...], start, end, TR)
        outs = []
        for gi, w in enumerate(POOL_WINDOWS):
            cols = pl.ds(gi * G, G)
            tot = buf[pl.ds(HALO + w // 2, TR), cols]
            for s in range(-w // 2 + 1, w // 2):
                tot = tot + buf[pl.ds(HALO + s, TR), cols]
            outs.append(tot)
        dx_ref[...] = jnp.concatenate(outs, axis=1) - dm_ref[...]

    return rowcall(cfg, fn, name, [("row", dm, P, 0), ("row", dmn, P, 0), ("prev", dmn, P, 0), ("next", dmn, P, 0)],
                   [("row", _sds((T, P), f32), P, 0)], scratch=[pltpu.VMEM((TR + 2 * HALO, P), f32)])[0]


def _swap_halves(x):
    return pltpu.roll(x, HEAD_DIM // 2, 1)


def _headnorm(x, g):
    return _rms(x) * g


def att_prep(cfg, z, qg, kg, cosf, sinf, nq, name):
    T = z.shape[0]
    U = z.shape[1] // (nq + 3)
    nh = U // HEAD_DIM

    def fn(i, *refs):
        q_refs = refs[:nq]
        k_ref, v_ref, qg_ref, kg_ref, cos_ref, sin_ref, qn_ref, kn_ref, vb_ref = refs[nq:]
        cosv, sinv = cos_ref[...], sin_ref[...]

        def heads(x, g):
            outs = []
            for h in range(nh):
                y = _headnorm(x[:, h * HEAD_DIM:(h + 1) * HEAD_DIM], g)
                outs.append(y * cosv + _swap_halves(y) * sinv)
            return jnp.concatenate(outs, axis=1)

        qn_ref[...] = jnp.concatenate([heads(r[...], qg_ref[...]) for r in q_refs], axis=1).astype(bf16)
        kn_ref[...] = heads(k_ref[...], kg_ref[...]).astype(bf16)
        vb_ref[...] = v_ref[...].astype(bf16)

    ins = [("row", z, U, 1 + n) for n in range(nq)] + [("row", z, U, nq + 1), ("row", z, U, nq + 2), ("full", qg),
                                                       ("full", kg), ("row", cosf, HEAD_DIM, 0), ("row", sinf, HEAD_DIM, 0)]
    return rowcall(cfg, fn, name, ins, [("row", _sds((T, nq * U), bf16), nq * U, 0), ("row", _sds((T, U), bf16), U, 0),
                                        ("row", _sds((T, U), bf16), U, 0)])


def att_prep_bwd(cfg, z, qg, kg, cosf, sinf, dqn, dkn, dvb, dxpool, nq, name):
    T = z.shape[0]
    U = z.shape[1] // (nq + 3)
    nh = U // HEAD_DIM
    cT = cfg.cT

    def fn(i, *refs):
        q_refs = refs[:nq]
        (k_ref, qg_ref, kg_ref, cos_ref, sin_ref, dqn_ref, dkn_ref, dvb_ref, dxp_ref, dz_ref, dqg_ref, dkg_ref) = refs[nq:]
        cosv, sinv = cos_ref[...], sin_ref[...]

        @pl.when(i == 0)
        def _():
            dqg_ref[...] = jnp.zeros(dqg_ref.shape, f32)
            dkg_ref[...] = jnp.zeros(dkg_ref.shape, f32)

        def heads_bwd(x, g, dy, dg_ref):
            outs = []
            for h in range(nh):
                sl = slice(h * HEAD_DIM, (h + 1) * HEAD_DIM)
                d = dy[:, sl]
                dn = d * cosv + _swap_halves(d * sinv)
                _, vjp = jax.vjp(_headnorm, x[:, sl], g)
                dx, dg = vjp(dn)
                dg_ref[...] += dg
                outs.append(dx)
            return jnp.concatenate(outs, axis=1)

        dk = heads_bwd(k_ref[...], kg_ref[...], dkn_ref[...], dkg_ref)
        tail = [dk.astype(bf16), dvb_ref[...].astype(bf16)]

        @pl.when(i < cT)
        def _():
            zeros = jnp.zeros((cfg.TR, (nq + 1) * U), bf16)
            dz_ref[...] = jnp.concatenate([zeros] + tail, axis=1)

        @pl.when(i >= cT)
        def _():
            dq = [heads_bwd(r[...], qg_ref[...], dqn_ref[:, n * U:(n + 1) * U], dqg_ref) for n, r in enumerate(q_refs)]
            dz_ref[...] = jnp.concatenate([dxp_ref[...].astype(bf16)] + [t.astype(bf16) for t in dq] + tail, axis=1)

    ins = ([("row", z, U, 1 + n) for n in range(nq)] +
           [("row", z, U, nq + 1), ("full", qg), ("full", kg), ("row", cosf, HEAD_DIM, 0), ("row", sinf, HEAD_DIM, 0),
            ("row", dqn, nq * U, 0), ("row", dkn, U, 0), ("row", dvb, U, 0), ("row", dxpool, U, 0)])
    W = (nq + 3) * U
    return rowcall(cfg, fn, name, ins, [("row", _sds((T, W), bf16), W, 0), ("full", _sds((1, HEAD_DIM), f32)),
                                        ("full", _sds((1, HEAD_DIM), f32))])


def _stack_heads(x, n):
    return jnp.concatenate([x[:, h * HEAD_DIM:(h + 1) * HEAD_DIM] for h in range(n)], axis=0)


def _unstack_heads(x, n):
    rows = x.shape[0] // n
    return jnp.concatenate([x[h * rows:(h + 1) * rows] for h in range(n)], axis=1)


def _att_tiles(cfg, T):
    tq = cfg.TR
    tk = _pick(T, (4224, 2816, 1408, 768, 512, 256, 128))
    return tq, tk, (T - cfg.cT * cfg.TR) // tq, T // tk


LOG2E = 1.4426950408889634


def att_fwd(cfg, qn, kn, vb, nq, name):
    T, U = kn.shape
    KV = U // HEAD_DIM
    tq, tk, nQ, nK = _att_tiles(cfg, T)
    scale = HEAD_DIM ** -0.5
    c2 = scale * LOG2E
    R = nq * tq

    def body(q_ref, k_ref, v_ref, o_ref, lse_ref, *scratch):
        ik = pl.program_id(2)
        m_sc, l_sc, acc = scratch[:nq], scratch[nq:2 * nq], scratch[2 * nq:]

        @pl.when(ik == 0)
        def _():
            for h in range(nq):
                m_sc[h][...] = jnp.full(m_sc[h].shape, -jnp.inf, f32)
                l_sc[h][...] = jnp.zeros(l_sc[h].shape, f32)
                acc[h][...] = jnp.zeros(acc[h].shape, f32)

        k, v = k_ref[...], v_ref[...]
        for h in range(nq):
            s = _dotf(q_ref[:, h * HEAD_DIM:(h + 1) * HEAD_DIM], k, 1, 1)
            m_old = m_sc[h][...]
            m_new = jnp.maximum(m_old, jnp.max(s, axis=-1, keepdims=True))
            alpha = jnp.exp2((m_old - m_new) * c2)
            p = jnp.exp2((s - m_new) * c2)
            l_sc[h][...] = alpha * l_sc[h][...] + jnp.sum(p, axis=-1, keepdims=True)
            acc[h][...] = alpha * acc[h][...] + jnp.dot(p.astype(bf16), v, preferred_element_type=f32)
            m_sc[h][...] = m_new

        @pl.when(ik == nK - 1)
        def _():
            o_ref[...] = jnp.concatenate([acc[h][...] / l_sc[h][...] for h in range(nq)], axis=1)
            lse_ref[...] = jnp.concatenate([m_sc[h][...] * scale + jnp.log(l_sc[h][...]) for h in range(nq)], axis=0)

    W = nq * HEAD_DIM
    q_spec = pl.BlockSpec((tq, W), lambda h, i, k: (i + cfg.cT, h))
    kv_spec = pl.BlockSpec((tk, HEAD_DIM), lambda h, i, k: (k, h))
    lse_spec = pl.BlockSpec((None, None, R, 1), lambda h, i, k: (h, i, 0, 0))
    col = [pltpu.VMEM((tq, 1), f32)] * nq
    return pl.pallas_call(
        body, name=name, grid=(KV, nQ, nK), in_specs=[q_spec, kv_spec, kv_spec], out_specs=[q_spec, lse_spec],
        out_shape=[_sds((T, nq * U), f32), _sds((KV, nQ, R, 1), f32)],
        scratch_shapes=col + col + [pltpu.VMEM((tq, HEAD_DIM), f32)] * nq,
        compiler_params=_params(("arbitrary", "arbitrary", "arbitrary")))(qn, kn, vb)


def att_bwd_dq(cfg, qn, kn, vb, o, lse, do, nq, name):
    T, U = kn.shape
    KV = U // HEAD_DIM
    tq, tk, nQ, nK = _att_tiles(cfg, T)
    scale = HEAD_DIM ** -0.5
    c2 = scale * LOG2E
    R = nq * tq
    W = nq * HEAD_DIM

    def body(q_ref, k_ref, v_ref, o_ref, lse_ref, do_ref, dq_ref, acc, dl):
        ik = pl.program_id(2)

        @pl.when(ik == 0)
        def _():
            acc[...] = jnp.zeros(acc.shape, f32)
            dl[...] = jnp.sum(_stack_heads(do_ref[...] * o_ref[...], nq), axis=-1, keepdims=True)

        k, v = k_ref[...], v_ref[...]
        for h in range(nq):
            rows = pl.ds(h * tq, tq)
            cols = slice(h * HEAD_DIM, (h + 1) * HEAD_DIM)
            s = _dotf(q_ref[:, cols], k, 1, 1)
            p = jnp.exp2(s * c2 - lse_ref[rows, :] * LOG2E)
            dp = _dotf(do_ref[:, cols].astype(bf16), v, 1, 1)
            ds = (p * (dp - dl[rows, :]) * scale).astype(bf16)
            acc[rows, :] += jnp.dot(ds, k, preferred_element_type=f32)

        @pl.when(ik == nK - 1)
        def _():
            dq_ref[...] = _unstack_heads(acc[...], nq)

    q_spec = pl.BlockSpec((tq, W), lambda h, i, k: (i + cfg.cT, h))
    kv_spec = pl.BlockSpec((tk, HEAD_DIM), lambda h, i, k: (k, h))
    lse_spec = pl.BlockSpec((None, None, R, 1), lambda h, i, k: (h, i, 0, 0))
    return pl.pallas_call(
        body, name=name, grid=(KV, nQ, nK), in_specs=[q_spec, kv_spec, kv_spec, q_spec, lse_spec, q_spec], out_specs=q_spec,
        out_shape=_sds((T, nq * U), f32), scratch_shapes=[pltpu.VMEM((R, HEAD_DIM), f32), pltpu.VMEM((R, 1), f32)],
        compiler_params=_params(("arbitrary", "arbitrary", "arbitrary")))(qn, kn, vb, o, lse, do)


def att_bwd_dkv(cfg, qn, kn, vb, o, lse, do, nq, name):
    T, U = kn.shape
    KV = U // HEAD_DIM
    tq, tk, nQ, nK = _att_tiles(cfg, T)
    scale = HEAD_DIM ** -0.5
    c2 = scale * LOG2E
    R = nq * tq
    W = nq * HEAD_DIM

    def body(q_ref, k_ref, v_ref, o_ref, lse_ref, do_ref, dk_ref, dv_ref, dk_acc, dv_acc):
        iq = pl.program_id(2)

        @pl.when(iq == 0)
        def _():
            dk_acc[...] = jnp.zeros(dk_acc.shape, f32)
            dv_acc[...] = jnp.zeros(dv_acc.shape, f32)

        k, v = k_ref[...], v_ref[...]
        for h in range(nq):
            rows = pl.ds(h * tq, tq)
            cols = slice(h * HEAD_DIM, (h + 1) * HEAD_DIM)
            qh = q_ref[:, cols]
            doh = do_ref[:, cols]
            dl = jnp.sum(doh * o_ref[:, cols], axis=-1, keepdims=True)
            p = jnp.exp2(_dotf(qh, k, 1, 1) * c2 - lse_ref[rows, :] * LOG2E)
            do_bf = doh.astype(bf16)
            dv_acc[...] += _dotf(p.astype(bf16), do_bf, 0, 0)
            dp = _dotf(do_bf, v, 1, 1)
            ds = (p * (dp - dl) * scale).astype(bf16)
            dk_acc[...] += _dotf(ds, qh, 0, 0)

        @pl.when(iq == nQ - 1)
        def _():
            dk_ref[...] = dk_acc[...]
            dv_ref[...] = dv_acc[...]

    q_spec = pl.BlockSpec((tq, W), lambda h, k, i: (i + cfg.cT, h))
    kv_spec = pl.BlockSpec((tk, HEAD_DIM), lambda h, k, i: (k, h))
    lse_spec = pl.BlockSpec((None, None, R, 1), lambda h, k, i: (h, i, 0, 0))
    return pl.pallas_call(
        body, name=name, grid=(KV, nK, nQ), in_specs=[q_spec, kv_spec, kv_spec, q_spec, lse_spec, q_spec],
        out_specs=[kv_spec, kv_spec], out_shape=[_sds((T, U), f32), _sds((T, U), f32)],
        scratch_shapes=[pltpu.VMEM((tk, HEAD_DIM), f32), pltpu.VMEM((tk, HEAD_DIM), f32)],
        compiler_params=_params(("arbitrary", "arbitrary", "arbitrary")))(qn, kn, vb, o, lse, do)


def od_mix(cfg, pooled, o, name):
    T, P = pooled.shape
    QW = o.shape[1]
    cT = cfg.cT

    def fn(i, p_ref, o_ref, m_ref):
        @pl.when(i < cT)
        def _():
            m_ref[...] = jnp.zeros(m_ref.shape, bf16)

        @pl.when(i >= cT)
        def _():
            m_ref[...] = jnp.concatenate([p_ref[...], o_ref[...].astype(bf16)], axis=1)

    return rowcall(cfg, fn, name, [("row", pooled, P, 0), ("row", o, QW, 0)], [("row", _sds((T, P + QW), bf16), P + QW, 0)])[0]


def ev_mix(cfg, lru, ret, name):
    T, W = lru.shape
    RV = ret.shape[1]

    def fn(i, a_ref, b_ref, m_ref):
        m_ref[...] = jnp.concatenate([a_ref[...], b_ref[...]], axis=1)

    return rowcall(cfg, fn, name, [("row", lru, W, 0), ("row", ret, RV, 0)], [("row", _sds((T, W + RV), bf16), W + RV, 0)])[0]


def ev_dz_pack(cfg, dgl, dr, dq, dk, dv, dol, name):
    T, W = dgl.shape
    RV = dol.shape[1]
    width = 2 * W + 4 * RV

    def fn(i, g_ref, r_ref, q_ref, k_ref, v_ref, o_ref, dz_ref):
        parts = [g_ref[...], r_ref[...], q_ref[0] + q_ref[1], k_ref[0] + k_ref[1], v_ref[0] + v_ref[1], o_ref[...]]
        dz_ref[...] = jnp.concatenate([p.astype(bf16) for p in parts], axis=1)

    return rowcall(cfg, fn, name, [("row", dgl, W, 0), ("row", dr, W, 0), ("drow", dq, RV, 0), ("drow", dk, RV, 0),
                                   ("drow", dv, RV, 0), ("row", dol, RV, 0)], [("row", _sds((T, width), bf16), width, 0)])[0]


def loss_fwd_bwd(cfg, xf, target, name):
    T, D = xf.shape
    TR, cT = cfg.TR, cfg.cT

    def body(x_ref, t_ref, sq_ref, dx_ref):
        i = pl.program_id(0)

        @pl.when(i == 0)
        def _():
            sq_ref[...] = jnp.zeros(sq_ref.shape, f32)

        @pl.when(i < cT)
        def _():
            dx_ref[...] = jnp.zeros(dx_ref.shape, f32)

        @pl.when(i >= cT)
        def _():
            diff = x_ref[...] - t_ref[...]
            sq_ref[...] += jnp.sum(diff * diff, axis=0, keepdims=True)
            dx_ref[...] = diff / D

    row = pl.BlockSpec((TR, D), lambda i: (i, 0))
    trow = pl.BlockSpec((TR, D), lambda i: (jnp.maximum(i - cT, 0), 0))
    return pl.pallas_call(body, name=name, grid=(cfg.nT,), in_specs=[row, trow],
                          out_specs=[pl.BlockSpec((1, D), lambda i: (0, 0)), row],
                          out_shape=[_sds((1, D), f32), _sds((T, D), f32)], compiler_params=_params(("arbitrary",)))(xf, target)


MOD_ROWS = 16


def mod_fwd(s16, mod_w, name):
    nL, D, C4 = mod_w.shape
    tc = _pick(C4, (512, 256, 128))

    def body(s_ref, w_ref, o_ref):
        o_ref[...] = jnp.dot(s_ref[...], w_ref[...], precision=lax.Precision.HIGHEST, preferred_element_type=f32)

    return pl.pallas_call(
        body, name=name, grid=(nL, C4 // tc),
        in_specs=[pl.BlockSpec((MOD_ROWS, D), lambda l, j: (0, 0)), pl.BlockSpec((None, D, tc), lambda l, j: (l, 0, j))],
        out_specs=pl.BlockSpec((None, MOD_ROWS, tc), lambda l, j: (l, 0, j)), out_shape=_sds((nL, MOD_ROWS, C4), f32),
        compiler_params=_params(("arbitrary", "arbitrary")))(s16, mod_w)


def mod_bwd(s16, dm16, mod_w, name):
    nL, D, C4 = mod_w.shape
    tc = _pick(C4, (512, 256, 128))
    half = MOD_ROWS // 2

    def body(s_ref, d_ref, w_ref, g_ref, dc_ref):
        first = jnp.logical_and(pl.program_id(0) == 0, pl.program_id(1) == 0)
        g_ref[...] = lax.dot_general(s_ref[...], d_ref[...], (((0,), (0,)), ((), ())), precision=lax.Precision.HIGHEST,
                                     preferred_element_type=f32)
        part = lax.dot_general(d_ref[...], w_ref[...], (((1,), (1,)), ((), ())), precision=lax.Precision.HIGHEST,
                               preferred_element_type=f32)
        _acc(dc_ref, jnp.sum(part[half:], axis=0, keepdims=True), first)

    return pl.pallas_call(
        body, name=name, grid=(nL, C4 // tc),
        in_specs=[pl.BlockSpec((MOD_ROWS, D), lambda l, j: (0, 0)), pl.BlockSpec((None, MOD_ROWS, tc), lambda l, j: (l, 0, j)),
                  pl.BlockSpec((None, D, tc), lambda l, j: (l, 0, j))],
        out_specs=[pl.BlockSpec((None, D, tc), lambda l, j: (l, 0, j)), pl.BlockSpec((1, D), lambda l, j: (0, 0))],
        out_shape=[_sds((nL, D, C4), f32), _sds((1, D), f32)],
        compiler_params=_params(("arbitrary", "arbitrary")))(s16, dm16, mod_w)


def _as2d(a):
    return a.reshape(-1, a.shape[-1])


ELEMENTWISE_VMEM = 24 * 1024 * 1024


def _tiles2d(shape, n_arrays):
    R, C = shape
    tc = _pick(C, (1536, 1408, 1024, 768, 512, 256, 128))
    fits = [t for t in (512, 256, 128, 64, 32, 16, 8) if R % t == 0 and t * tc * 4 * 2 * n_arrays <= ELEMENTWISE_VMEM]
    return (fits[0] if fits else R), tc


def cast_bf16(a, name):
    a2 = _as2d(a)
    tr, tc = _tiles2d(a2.shape, 2)

    def body(a_ref, o_ref):
        o_ref[...] = a_ref[...].astype(bf16)

    spec = pl.BlockSpec((tr, tc), lambda i, j: (i, j))
    out = pl.pallas_call(body, name=name, grid=(a2.shape[0] // tr, a2.shape[1] // tc), in_specs=[spec], out_specs=spec,
                         out_shape=_sds(a2.shape, bf16), compiler_params=_params(("arbitrary", "arbitrary")))(a2)
    return out.reshape(a.shape)


def sum_leading(a, name, *, into=None, full_shape=None, widx=()):
    n = a.shape[0]
    a3 = a.reshape(n, -1, a.shape[-1])
    tr, tc = _tiles2d(a3.shape[1:], n + 1)

    def body(a_ref, *rest):
        o_ref = rest[-1]
        tot = a_ref[0].astype(f32)
        for k in range(1, n):
            tot = tot + a_ref[k].astype(f32)
        o_ref[...] = tot

    grid = (a3.shape[1] // tr, a3.shape[2] // tc)
    in_specs = [pl.BlockSpec((n, tr, tc), lambda i, j: (0, i, j))]
    args = [a3]
    if not widx:
        out = pl.pallas_call(body, name=name, grid=grid, in_specs=in_specs,
                             out_specs=pl.BlockSpec((tr, tc), lambda i, j: (i, j)), out_shape=_sds(a3.shape[1:], f32),
                             compiler_params=_params(("arbitrary", "arbitrary")))(*args)
        return out.reshape(a.shape[1:])
    lead = tuple(full_shape[:len(widx)])
    flat = lead + tuple(a3.shape[1:])
    aliases = {}
    if into is not None:
        in_specs.append(pl.BlockSpec(memory_space=pl.ANY))
        args.append(into.reshape(flat))
        aliases = {1: 0}
    out = pl.pallas_call(body, name=name, grid=grid, in_specs=in_specs,
                         out_specs=pl.BlockSpec((None,) * len(widx) + (tr, tc), lambda i, j: tuple(widx) + (i, j)),
                         out_shape=_sds(flat, f32), input_output_aliases=aliases,
                         compiler_params=_params(("arbitrary", "arbitrary")))(*args)
    return out.reshape(full_shape)


def adamw(w, m, v, g_parts, name):
    w2, m2, v2 = _as2d(w), _as2d(m), _as2d(v)
    parts = [_as2d(p) for p in g_parts]
    tr, tc = _tiles2d(w2.shape, 7 + len(parts))
    npart = len(parts)

    def body(*refs):
        w_ref, m_ref, v_ref = refs[:3]
        p_refs = refs[3:3 + npart]
        g_ref, d_ref, nm_ref, nv_ref = refs[3 + npart:]
        g = p_refs[0][...]
        for p in p_refs[1:]:
            g = g + p[...]
        mn = ADAM_B1 * m_ref[...] + (1.0 - ADAM_B1) * g
        vn = ADAM_B2 * v_ref[...] + (1.0 - ADAM_B2) * jnp.square(g)
        m_hat = mn / (1.0 - ADAM_B1 ** ADAM_STEP)
        v_hat = vn / (1.0 - ADAM_B2 ** ADAM_STEP)
        g_ref[...] = g
        d_ref[...] = -ADAM_LR * (m_hat / (jnp.sqrt(v_hat) + ADAM_EPS) + ADAM_WD * w_ref[...])
        nm_ref[...] = mn
        nv_ref[...] = vn

    spec = pl.BlockSpec((tr, tc), lambda i, j: (i, j))
    outs = pl.pallas_call(body, name=name, grid=(w2.shape[0] // tr, w2.shape[1] // tc), in_specs=[spec] * (3 + npart),
                          out_specs=[spec] * 4, out_shape=[_sds(w2.shape, f32)] * 4,
                          compiler_params=_params(("arbitrary", "arbitrary")))(w2, m2, v2, *parts)
    return [o.reshape(w.shape) for o in outs]


def all_gather_small(a, name):
    R, C = a.shape

    def body(a_ref, out_ref, send_sems, recv_sems, local_sem):
        x, y, c = _coords()
        me = 4 * x + 2 * y + c
        mine = pltpu.make_async_copy(a_ref, out_ref.at[me], local_sem)
        mine.start()
        copies = []
        for k in range(1, N_DEV):
            kx, ky, kc = (k >> 2) & 1, (k >> 1) & 1, k & 1
            peer = (_flip(x, kx), _flip(y, ky), _flip(c, kc))
            cp = pltpu.make_async_remote_copy(src_ref=a_ref, dst_ref=out_ref.at[me], send_sem=send_sems.at[k - 1],
                                              recv_sem=recv_sems.at[k - 1], device_id=peer, device_id_type=MESH)
            cp.start()
            copies.append((cp, 4 * peer[0] + 2 * peer[1] + peer[2], peer))
        for k, (cp, pidx, peer) in enumerate(copies):
            pltpu.make_async_remote_copy(src_ref=a_ref, dst_ref=out_ref.at[pidx], send_sem=send_sems.at[k],
                                         recv_sem=recv_sems.at[k], device_id=peer, device_id_type=MESH).wait_recv()
        for cp, _, _ in copies:
            cp.wait_send()
        mine.wait()

    return pl.pallas_call(
        body, name=name, out_shape=_sds((N_DEV, R, C), f32),
        in_specs=[pl.BlockSpec(memory_space=pltpu.VMEM)], out_specs=pl.BlockSpec(memory_space=pltpu.VMEM),
        scratch_shapes=[pltpu.SemaphoreType.DMA((N_DEV - 1,)), pltpu.SemaphoreType.DMA((N_DEV - 1,)), pltpu.SemaphoreType.DMA],
        compiler_params=pltpu.CompilerParams(vmem_limit_bytes=VMEM_LIMIT))(a)


def swap_with_sibling(parts, name):
    n = len(parts)

    def body(*refs):
        in_refs, out_refs = refs[:n], refs[n:2 * n]
        send_sems, recv_sems = refs[2 * n:]
        x, y, c = _coords()
        sends = []
        for w in range(n):
            cp = pltpu.make_async_remote_copy(src_ref=in_refs[w], dst_ref=out_refs[w], send_sem=send_sems.at[w],
                                              recv_sem=recv_sems.at[w], device_id=(x, y, 1 - c), device_id_type=MESH)
            cp.start()
            sends.append(cp)
        for cp in sends:
            cp.wait_recv()
        for cp in sends:
            cp.wait_send()

    hbm = pl.BlockSpec(memory_space=pl.ANY)
    return pl.pallas_call(
        body, name=name, out_shape=[_sds(a.shape, a.dtype) for a in parts], in_specs=[hbm] * n, out_specs=[hbm] * n,
        scratch_shapes=[pltpu.SemaphoreType.DMA((n,)), pltpu.SemaphoreType.DMA((n,))],
        )(*parts)


def even_fwd(cfg, x, p, tag, host=None):
    W, H = p["W"], p["H"]
    got = {}
    h = pre_fwd(cfg, x, p["g_pre"], p["shift"], p["scale"], tag + "_pre")
    z = _hosted(host, "in", got, lambda cm: matmul("v1", h, p["w_in"], comm=cm, name=tag + "_in"))
    u = conv_fwd(cfg, z, p["conv_w"], p["conv_b"], W, tag + "_conv")
    a, b = lru_coef_fwd(cfg, u, p["wa"], p["ba"], p["wx"], p["bx"], p["lam"], tag + "_coef")
    hh, hp = lru_scan(cfg, a, b, tag + "_scan")
    lru = lru_out_fwd(cfg, z, hh, W, tag + "_lruout")
    qcol = 2 * W // RET_DK
    o, st = ret_fwd(cfg, z, p["logit"], p["cos1"], p["sin1"], H, qcol, tag + "_ret")
    olcol = (2 * W + 3 * H * RET_DK) // (H * RET_DV)
    ret = ret_norm_fwd(cfg, o, z, p["gn"], H, olcol, tag + "_retnorm")
    mix = ev_mix(cfg, lru, ret, tag + "_mix")
    y = _hosted(host, "out", got, lambda cm: matmul("v2", mix, p["w_out"], comm=cm, name=tag + "_out"))
    xo = post_fwd(cfg, x, y, p["g_post"], p["gate"], 1.0, tag + "_post")
    return xo, (x, h, z, u, a, hh, hp, o, st, mix, y, olcol, qcol), got


def even_bwd(cfg, dX, saved, p, tag):
    x, h, z, u, a, hh, hp, o, st, mix, y, olcol, qcol = saved
    W, H = p["W"], p["H"]
    dy, dg_post, dgate = post_bwd(cfg, dX, y, p["g_post"], p["gate"], 1.0, tag + "_postb")
    dmix = matmul("v4", dy, p["w_out"], name=tag + "_dmix")
    g_out = matmul("v6", mix, dy, gshape=p["w_out"].shape, out_dtype=bf16, name=tag + "_gwout")
    dgl, dhs = lru_out_bwd(cfg, z, hh, dmix, W, tag + "_lruoutb")
    da, db = lru_scan_bwd(cfg, a, hp, dhs, tag + "_scanb")
    du, dwa, dba, dwx, dbx, dlam = lru_coef_bwd(cfg, u, da, db, p["wa"], p["ba"], p["wx"], p["bx"], p["lam"], tag + "_coefb")
    dr, dcw, dcb = conv_bwd(cfg, z, du, p["conv_w"], W, tag + "_convb")
    do, dol, dgn = ret_norm_bwd(cfg, o, z, p["gn"], dmix, H, olcol, W // (H * RET_DV), tag + "_retnormb")
    dq, dk, dv, dlg = ret_bwd(cfg, z, st, do, p["logit"], p["cos1"], p["sin1"], H, qcol, tag + "_retb")
    dz = ev_dz_pack(cfg, dgl, dr, dq, dk, dv, dol, tag + "_dz")
    g_in, r_out = matmul("v5", h, dz, gshape=p["w_in"].shape, out_dtype=bf16, comm=Comm("scatter", [(g_out, ())]),
                         name=tag + "_gwin")
    dh, r_in = matmul("v3", dz, p["w_in"], comm=Comm("scatter", [(g_in, ())]), name=tag + "_dh")
    dX, dg_pre, dshift, dscale = pre_bwd(cfg, x, p["g_pre"], p["shift"], p["scale"], dh, dX, tag + "_preb")
    pg = dict(g_pre=dg_pre, g_post=dg_post, shift=dshift, scale=dscale, gate=dgate, conv_w=dcw, conv_b=dcb, wa=dwa,
              ba=dba, wx=dwx, bx=dbx, lam=dlam, logit=dlg, gn=dgn)
    return dX, pg, dict(w_in=r_in[0], w_out=r_out[0])


def odd_fwd(cfg, x, p, tag, host=None):
    nq = p["nq"]
    got = {}
    h = pre_fwd(cfg, x, p["g_pre"], p["shift"], p["scale"], tag + "_pre")
    z = _hosted(host, "in", got, lambda cm: matmul("v1", h, p["w_in"], comm=cm, name=tag + "_in"))
    pooled = pool_fwd(cfg, z, p["pool_w"], p["pool_scale"], tag + "_pool")
    qn, kn, vb = att_prep(cfg, z, p["qg"], p["kg"], p["cosf"], p["sinf"], nq, tag + "_prep")
    o, lse = att_fwd(cfg, qn, kn, vb, nq, tag + "_att")
    mix = od_mix(cfg, pooled, o, tag + "_mix")
    y = matmul("v2", mix, p["w_out"], name=tag + "_out")
    xo = post_fwd(cfg, x, y, p["g_post"], p["gate"], 1.0, tag + "_post")
    return xo, (x, h, z, qn, kn, vb, o, lse, mix, y), got


def odd_bwd(cfg, dX, saved, p, tag):
    x, h, z, qn, kn, vb, o, lse, mix, y = saved
    nq = p["nq"]
    U = kn.shape[1]
    dy, dg_post, dgate = post_bwd(cfg, dX, y, p["g_post"], p["gate"], 1.0, tag + "_postb")
    dmix = matmul("v4", dy, p["w_out"], name=tag + "_dmix")
    g_out = matmul("v6", mix, dy, gshape=p["w_out"].shape, out_dtype=bf16, name=tag + "_gwout")
    dm, dmn, dpw, dps = pool_bwd_a(cfg, z, dmix, p["pool_w"], p["pool_scale"], tag + "_poolb")
    dxp = pool_bwd_b(cfg, dm, dmn, tag + "_poolb2")
    do = dmix[:, U:]
    dqn = att_bwd_dq(cfg, qn, kn, vb, o, lse, do, nq, tag + "_attdq")
    dkn, dvb = att_bwd_dkv(cfg, qn, kn, vb, o, lse, do, nq, tag + "_attdkv")
    dz, dqg, dkg = att_prep_bwd(cfg, z, p["qg"], p["kg"], p["cosf"], p["sinf"], dqn, dkn, dvb, dxp, nq, tag + "_prepb")
    g_in, r_out = matmul("v5", h, dz, gshape=p["w_in"].shape, out_dtype=bf16, comm=Comm("scatter", [(g_out, ())]),
                         name=tag + "_gwin")
    dh, r_in = matmul("v3", dz, p["w_in"], comm=Comm("scatter", [(g_in, ())]), name=tag + "_dh")
    dX, dg_pre, dshift, dscale = pre_bwd(cfg, x, p["g_pre"], p["shift"], p["scale"], dh, dX, tag + "_preb")
    pg = dict(g_pre=dg_pre, g_post=dg_post, shift=dshift, scale=dscale, gate=dgate, pool_w=dpw, pool_scale=dps, qg=dqg, kg=dkg)
    return dX, pg, dict(w_in=r_in[0], w_out=r_out[0])


WEIGHT_NAMES = ("c_ctx", "mod_w", "mod_b", "norm_pre", "norm_post", "ffn_gate", "ffn_up", "ffn_down", "ev_w_in", "ev_w_out",
                "lru_conv_w", "lru_conv_b", "lru_wa", "lru_ba", "lru_wx", "lru_bx", "lru_lambda", "ret_decay_logit", "ret_gn",
                "od_w_in", "od_w_out", "pool_w", "pool_scale", "q_norm", "k_norm")
BIG = ("ffn_gate", "ffn_up", "ffn_down", "ev_w_in", "ev_w_out", "od_w_in", "od_w_out")
SMALL_SHARDED = ("norm_pre", "norm_post", "lru_conv_w", "lru_ba", "lru_bx", "lru_lambda", "pool_scale")
SMALL_REPL = ("mod_b", "lru_conv_b", "lru_wa", "lru_wx", "ret_decay_logit", "ret_gn", "pool_w", "q_norm", "k_norm")
LANES = 128


PACK_ROWS = 512


def _rows_of(n):
    return -(-n // (8 * LANES)) * 8


def _pack(arrs):
    rows = []
    for a in arrs:
        flat = a.reshape(-1)
        rows.append(jnp.pad(flat, (0, _rows_of(flat.shape[0]) * LANES - flat.shape[0])).reshape(-1, LANES))
    total = sum(r.shape[0] for r in rows)
    rows.append(jnp.zeros(((-total) % PACK_ROWS, LANES), f32))
    return jnp.concatenate(rows), None


def _unpack(packed, shapes, lead=()):
    out, pos = [], 0
    for shp in shapes:
        n = math.prod(shp)
        r = _rows_of(n)
        piece = packed[..., pos:pos + r, :].reshape(lead + (r * LANES,))
        out.append(piece[..., :n].reshape(lead + tuple(shp)))
        pos += r
    return out


def _unshard(g):
    return jnp.moveaxis(g, 0, -2).reshape(g.shape[1:-1] + (g.shape[0] * g.shape[-1],))


def _rope_tables(S, Lc):
    n_r = RET_DK // 2
    f_r = RET_THETA ** (-jnp.arange(n_r, dtype=f32) / n_r)
    ang1 = jnp.arange(S, dtype=f32)[:, None] * f_r
    rows = S // GRID_W
    row = jnp.repeat(jnp.arange(rows, dtype=f32), GRID_W)
    col = jnp.tile(jnp.arange(GRID_W, dtype=f32), rows)
    n_ax = HEAD_DIM // 4
    f_ax = ROPE_THETA ** (-jnp.arange(n_ax, dtype=f32) / n_ax)
    ang2 = jnp.concatenate([row[:, None] * f_ax, col[:, None] * f_ax], axis=-1)
    cos2, sin2 = jnp.cos(ang2), jnp.sin(ang2)
    ones = lambda n: jnp.ones((Lc, n), f32)
    zeros = lambda n: jnp.zeros((Lc, n), f32)
    cos1 = jnp.concatenate([ones(n_r), jnp.cos(ang1)])
    sin1 = jnp.concatenate([zeros(n_r), jnp.sin(ang1)])
    cosf = jnp.concatenate([ones(HEAD_DIM), jnp.concatenate([cos2, cos2], axis=1)])
    sinf = jnp.concatenate([zeros(HEAD_DIM), jnp.concatenate([-sin2, sin2], axis=1)])
    return cos1, sin1, cosf, sinf


def kernel(x, c, ctx, c_ctx, mod_w, mod_b, norm_pre, norm_post, ffn_gate, ffn_up, ffn_down, ev_w_in, ev_w_out, lru_conv_w, lru_conv_b, lru_wa, lru_ba, lru_wx, lru_bx, lru_lambda, ret_decay_logit, ret_gn, od_w_in, od_w_out, pool_w, pool_scale, q_norm, k_norm, loss_target, m_c_ctx, m_mod_w, m_mod_b, m_norm_pre, m_norm_post, m_ffn_gate, m_ffn_up, m_ffn_down, m_ev_w_in, m_ev_w_out, m_lru_conv_w, m_lru_conv_b, m_lru_wa, m_lru_ba, m_lru_wx, m_lru_bx, m_lru_lambda, m_ret_decay_logit, m_ret_gn, m_od_w_in, m_od_w_out, m_pool_w, m_pool_scale, m_q_norm, m_k_norm, v_c_ctx, v_mod_w, v_mod_b, v_norm_pre, v_norm_post, v_ffn_gate, v_ffn_up, v_ffn_down, v_ev_w_in, v_ev_w_out, v_lru_conv_w, v_lru_conv_b, v_lru_wa, v_lru_ba, v_lru_wx, v_lru_bx, v_lru_lambda, v_ret_decay_logit, v_ret_gn, v_od_w_in, v_od_w_out, v_pool_w, v_pool_scale, v_q_norm, v_k_norm):
    wts = dict(c_ctx=c_ctx, mod_w=mod_w, mod_b=mod_b, norm_pre=norm_pre, norm_post=norm_post, ffn_gate=ffn_gate, ffn_up=ffn_up,
               ffn_down=ffn_down, ev_w_in=ev_w_in, ev_w_out=ev_w_out, lru_conv_w=lru_conv_w, lru_conv_b=lru_conv_b,
               lru_wa=lru_wa, lru_ba=lru_ba, lru_wx=lru_wx, lru_bx=lru_bx, lru_lambda=lru_lambda,
               ret_decay_logit=ret_decay_logit, ret_gn=ret_gn, od_w_in=od_w_in, od_w_out=od_w_out, pool_w=pool_w,
               pool_scale=pool_scale, q_norm=q_norm, k_norm=k_norm)
    mom_m = dict(zip(WEIGHT_NAMES, (m_c_ctx, m_mod_w, m_mod_b, m_norm_pre, m_norm_post, m_ffn_gate, m_ffn_up, m_ffn_down,
                                    m_ev_w_in, m_ev_w_out, m_lru_conv_w, m_lru_conv_b, m_lru_wa, m_lru_ba, m_lru_wx, m_lru_bx,
                                    m_lru_lambda, m_ret_decay_logit, m_ret_gn, m_od_w_in, m_od_w_out, m_pool_w, m_pool_scale,
                                    m_q_norm, m_k_norm)))
    mom_v = dict(zip(WEIGHT_NAMES, (v_c_ctx, v_mod_w, v_mod_b, v_norm_pre, v_norm_post, v_ffn_gate, v_ffn_up, v_ffn_down,
                                    v_ev_w_in, v_ev_w_out, v_lru_conv_w, v_lru_conv_b, v_lru_wa, v_lru_ba, v_lru_wx, v_lru_bx,
                                    v_lru_lambda, v_ret_decay_logit, v_ret_gn, v_od_w_in, v_od_w_out, v_pool_w, v_pool_scale,
                                    v_q_norm, v_k_norm)))

    _, S, D = x.shape
    Lc = ctx.shape[1]
    T = Lc + S
    TR = 256 if (Lc % 256 == 0 and S % 256 == 0) else 128
    assert Lc % TR == 0 and S % TR == 0 and TR % RET_CHUNK == 0
    cfg = RowCfg(TR, T // TR, Lc // TR)
    W = lru_conv_b.shape[-1]
    H = ret_decay_logit.shape[-1]
    U = POOL_GROUP * len(POOL_WINDOWS)
    nq = (N_CHIPS * od_w_in.shape[-1]) // U - 3
    assert W % (H * RET_DV) == 0 and (2 * W) % (H * RET_DK) == 0
    nL = mod_w.shape[0]
    C4 = mod_w.shape[-1]
    assert nL == 2, "two layers: an even mixer then an odd one"

    xi, yi, ci = _coords()
    chip = 2 * xi + yi
    me = 4 * xi + 2 * yi + ci

    sc = jax.nn.silu(c)
    small_in, _ = _pack([sc] + [wts[n] for n in SMALL_SHARDED])
    g1 = all_gather_small(small_in, "gather_small_fwd")
    parts = _unpack(g1, [sc.shape] + [wts[n].shape for n in SMALL_SHARDED], lead=(N_DEV,))
    sc_all = parts[0][:, 0]
    full = {n: _unshard(parts[1 + i][0::2]) for i, n in enumerate(SMALL_SHARDED)}
    for n in SMALL_REPL + ("c_ctx",):
        full[n] = wts[n]

    scc = jax.nn.silu(c_ctx)[None]
    pad_rows = MOD_ROWS - N_DEV - 1
    s16 = jnp.concatenate([sc_all, scc, jnp.zeros((pad_rows, D), f32)])
    modp = mod_fwd(s16, mod_w, "mod_fwd")
    g2 = all_gather_small(modp.reshape(-1, LANES), "gather_mod")
    mod_all = g2.reshape(N_DEV, nL, MOD_ROWS, C4)[0::2]
    mod_all = jnp.moveaxis(mod_all, 0, 2).reshape(nL, MOD_ROWS, N_CHIPS * C4) + mod_b[:, None, :]
    mod_l = lax.dynamic_index_in_dim(mod_all, me, axis=1, keepdims=False).reshape(nL, 3, 3, D)
    mod_c = mod_all[:, N_DEV].reshape(nL, 3, 3, D)

    def mod_of(li, s, kind, ctx_live=True):
        cpart = mod_c[li, s, kind] if ctx_live else jnp.zeros((D,), f32)
        return jnp.stack([cpart, mod_l[li, s, kind]])[:, None, :]

    packed = {n: cast_bf16(wts[n], "cast_" + n) for n in BIG}
    ffn_units = [(0, 0), (0, 1), (1, 0), (1, 1)]

    def G(*pieces):
        return Comm("gather", [(packed[n], idx) for n, idx in pieces])

    cos1, sin1, cosf, sinf = _rope_tables(S, Lc)

    def sub_params(li, s, ctx_live=True, gate_ctx_live=True):
        return dict(g_pre=full["norm_pre"][li, s][None], g_post=full["norm_post"][li, s][None],
                    shift=mod_of(li, s, 0, ctx_live), scale=mod_of(li, s, 1, ctx_live),
                    gate=mod_of(li, s, 2, ctx_live and gate_ctx_live))

    X0 = jnp.concatenate([ctx[0], x[0]], axis=0)
    wg00, wu00 = exchange(G(("ffn_gate", (0, 0)), ("ffn_up", (0, 0))), "gather_first")
    p00 = sub_params(0, 0)
    p00.update(wg=wg00, wu=wu00)
    h00 = {"gateup": G(("ffn_down", (0, 0)), ("ev_w_in", (0,))), "down": G(("ev_w_out", (0,)), ("ffn_gate", (0, 1)))}
    h = pre_fwd(cfg, X0, p00["g_pre"], p00["shift"], p00["scale"], "l0f0_pre")
    (a00, b00, u00), (wd00, ev_in) = ffn_gateup(cfg, h, wg00, wu00, "l0f0_gateup", comm=h00["gateup"])
    p00.update(wd=wd00)
    y00, (ev_out, wg01) = matmul("v2", u00, wd00, comm=h00["down"], name="l0f0_down")
    X1 = post_fwd(cfg, X0, y00, p00["g_post"], p00["gate"], FFN_STEP, "l0f0_post")
    s00 = (X0, h, a00, b00, u00, y00)

    p01 = sub_params(0, 1)
    p01.update(W=W, H=H, conv_w=full["lru_conv_w"][0], conv_b=full["lru_conv_b"], wa=full["lru_wa"][0],
               ba=full["lru_ba"][0][:, None, :], wx=full["lru_wx"][0], bx=full["lru_bx"][0][:, None, :],
               lam=full["lru_lambda"][0][:, None, :], logit=full["ret_decay_logit"][0][:, :, None, None],
               gn=full["ret_gn"], cos1=cos1, sin1=sin1, w_in=ev_in, w_out=ev_out)
    X2, s01, got = even_fwd(cfg, X1, p01, "l0mix", host={"in": G(("ffn_up", (0, 1))), "out": G(("ffn_down", (0, 1)))})
    p02 = sub_params(0, 2)
    p02.update(wg=wg01, wu=got["in"][0], wd=got["out"][0])
    X3, s02, got = ffn_fwd(cfg, X2, p02, "l0f1", host={"gateup": G(("ffn_gate", (1, 0)), ("ffn_up", (1, 0))),
                                                       "down": G(("ffn_down", (1, 0)))})
    p10 = sub_params(1, 0)
    p10.update(wg=got["gateup"][0], wu=got["gateup"][1], wd=got["down"][0])
    X4, s10, got = ffn_fwd(cfg, X3, p10, "l1f0", host={"gateup": G(("od_w_in", (0,)), ("od_w_out", (0,)), ("ffn_gate", (1, 1))),
                                                       "down": G(("ffn_up", (1, 1)))})
    p11 = sub_params(1, 1, gate_ctx_live=False)
    p11.update(nq=nq, pool_w=full["pool_w"][0], pool_scale=full["pool_scale"], qg=full["q_norm"], kg=full["k_norm"],
               cosf=cosf, sinf=sinf, w_in=got["gateup"][0], w_out=got["gateup"][1])
    p12 = sub_params(1, 2, ctx_live=False)
    p12.update(wg=got["gateup"][2], wu=got["down"][0])
    X5, s11, got = odd_fwd(cfg, X4, p11, "l1mix", host={"in": G(("ffn_down", (1, 1)))})
    p12.update(wd=got["in"][0])
    X6, s12, _ = ffn_fwd(cfg, X5, p12, "l1f1")
    sq, dX = loss_fwd_bwd(cfg, X6, loss_target[0], "loss")
    loss = lax.psum(0.5 * jnp.sum(sq) / D, ("x", "y", "c"))

    recv_ffn = {}
    dX, g12, recv_ffn[(1, 1)] = ffn_bwd(cfg, dX, s12, p12, "l1f1")
    dX, g11, recv_od = odd_bwd(cfg, dX, s11, p11, "l1mix")
    dX, g10, recv_ffn[(1, 0)] = ffn_bwd(cfg, dX, s10, p10, "l1f0")
    dX, g02, recv_ffn[(0, 1)] = ffn_bwd(cfg, dX, s02, p02, "l0f1")
    dX, g01, recv_ev = even_bwd(cfg, dX, s01, p01, "l0mix")
    dX, g00, recv_ffn[(0, 0)] = ffn_bwd(cfg, dX, s00, p00, "l0f0")
    grad_x = dX[Lc:][None]

    subs = [[g00, g01, g02], [g10, g11, g12]]
    zero_d = jnp.zeros((D,), f32)

    def dmod(group, live):
        rows = []
        for li in range(nL):
            for s in range(3):
                for kind, key in enumerate(("shift", "scale", "gate")):
                    rows.append(subs[li][s][key][group, 0] if live(li, s, kind) else zero_d)
        return jnp.stack(rows).reshape(nL, 9 * D)

    dmod_l = dmod(1, lambda li, s, kind: True)
    dmod_c = dmod(0, lambda li, s, kind: not (li == 1 and (s == 2 or (s == 1 and kind == 2))))

    dm_in, _ = _pack([dmod_l, dmod_c])
    g3 = all_gather_small(dm_in, "gather_dmod")
    dl_all, dc_all = _unpack(g3, [dmod_l.shape, dmod_c.shape], lead=(N_DEV,))
    dm16 = jnp.moveaxis(jnp.concatenate([dl_all, dc_all], axis=0), 0, 1)
    dm16 = lax.dynamic_slice_in_dim(dm16, chip * C4, C4, axis=2)
    s16b = jnp.concatenate([sc_all, jnp.broadcast_to(scc, (N_DEV, D))])
    g_mod_w, dscc_part = mod_bwd(s16b, dm16, mod_w, "mod_bwd")

    norm_pre_g = jnp.stack([jnp.concatenate([subs[li][s]["g_pre"] for s in range(3)]) for li in range(nL)])
    norm_post_g = jnp.stack([jnp.concatenate([subs[li][s]["g_post"] for s in range(3)]) for li in range(nL)])
    small_g = dict(norm_pre=norm_pre_g, norm_post=norm_post_g, lru_conv_w=g01["conv_w"][None], lru_ba=g01["ba"][:, 0][None],
                   lru_bx=g01["bx"][:, 0][None], lru_lambda=g01["lam"][:, 0][None], pool_scale=g11["pool_scale"],
                   mod_b=dmod_l + dmod_c, lru_conv_b=g01["conv_b"], lru_wa=g01["wa"][None], lru_wx=g01["wx"][None],
                   ret_decay_logit=g01["logit"][:, :, 0, 0][None], ret_gn=g01["gn"], pool_w=g11["pool_w"][None],
                   q_norm=g11["qg"], k_norm=g11["kg"])
    names = SMALL_SHARDED + SMALL_REPL
    sg_in, _ = _pack([small_g[n] for n in names] + [dscc_part])
    g4 = all_gather_small(sg_in, "gather_small_grads")
    tot = sum_leading(g4, "sum_small_grads")
    tot_parts = _unpack(tot, [small_g[n].shape for n in names])
    dscc_all = _unpack(g4, [small_g[n].shape for n in names] + [dscc_part.shape], lead=(N_DEV,))[-1]
    dscc = dscc_all[0, 0] + dscc_all[2, 0] + dscc_all[4, 0] + dscc_all[6, 0]
    _, silu_vjp = jax.vjp(jax.nn.silu, c_ctx)
    grads = {"c_ctx": silu_vjp(dscc)[0]}
    for n, g in zip(names, tot_parts):
        if n in SMALL_SHARDED:
            k = wts[n].shape[-1]
            g = lax.dynamic_slice_in_dim(g, chip * k, k, axis=g.ndim - 1)
        grads[n] = g.reshape(wts[n].shape)

    partial = {}
    for n, key in (("ffn_gate", "wg"), ("ffn_up", "wu"), ("ffn_down", "wd")):
        acc = None
        for u in ffn_units:
            acc = sum_leading(recv_ffn[u][key], "sum_%s_%d%d" % (n, u[0], u[1]), into=acc, full_shape=wts[n].shape, widx=u)
        partial[n] = acc
    for n, r in (("ev_w_in", recv_ev["w_in"]), ("ev_w_out", recv_ev["w_out"]), ("od_w_in", recv_od["w_in"]),
                 ("od_w_out", recv_od["w_out"])):
        partial[n] = sum_leading(r, "sum_" + n).reshape(wts[n].shape)
    partial = [partial[n] for n in BIG]
    other = swap_with_sibling(partial, "swap_partials")

    delta, new_m, new_v = {}, {}, {}
    for n, pa, pb in zip(BIG, partial, other):
        grads[n], delta[n], new_m[n], new_v[n] = adamw(wts[n], mom_m[n], mom_v[n], [pa, pb], "adamw_" + n)
    grads["mod_w"], delta["mod_w"], new_m["mod_w"], new_v["mod_w"] = adamw(mod_w, m_mod_w, v_mod_w, [g_mod_w], "adamw_mod_w")
    snames = [n for n in WEIGHT_NAMES if n not in BIG and n != "mod_w"]
    pk = lambda d: _pack([d[n] for n in snames])[0]
    sres = adamw(pk(wts), pk(mom_m), pk(mom_v), [pk(grads)], "adamw_small")
    for res, dst in zip(sres[1:], (delta, new_m, new_v)):
        for n, a in zip(snames, _unpack(res, [wts[n].shape for n in snames])):
            dst[n] = a

    return (loss, grad_x, *[grads[n] for n in WEIGHT_NAMES], *[delta[n] for n in WEIGHT_NAMES],
            *[new_m[n] for n in WEIGHT_NAMES], *[new_v[n] for n in WEIGHT_NAMES])
```

```python
import functools
import math

import jax
import jax.numpy as jnp
from jax import lax
from jax.experimental import pallas as pl
from jax.experimental.pallas import tpu as pltpu

f32 = jnp.float32
bf16 = jnp.bfloat16
MESH = pl.DeviceIdType.MESH

EPS = 1e-6
FFN_STEP = 0.5
LRU_C = 8.0
CONV_W = 4
CONV_LEFT = 2
RET_DK = 256
RET_DV = 256
RET_CHUNK = 128
RET_THETA = 10000.0
POOL_WINDOWS = (2, 4, 8, 16)
POOL_GROUP = 128
HEAD_DIM = 128
ROPE_THETA = 10000.0
GRID_W = 64
ADAM_LR = 0.001
ADAM_B1 = 0.9
ADAM_B2 = 0.999
ADAM_EPS = 1e-08
ADAM_WD = 0.01
ADAM_STEP = 10

N_CHIPS = 4
N_DEV = 8
HALO = 8
VMEM_LIMIT = 56 * 1024 * 1024


def _params(sem=None):
    return pltpu.CompilerParams(dimension_semantics=sem, vmem_limit_bytes=VMEM_LIMIT)


def _pick(n, prefs):
    for p in prefs:
        if n % p == 0:
            return p
    return n


def _sds(shape, dtype):
    return jax.ShapeDtypeStruct(tuple(shape), dtype)


MATMUL_VMEM = 46 * 1024 * 1024


def _fit_rows(M, tm, tn, out_dtype):
    fixed = 2 * tm * tn * 4 + 2 * tm * tn * jnp.dtype(out_dtype).itemsize
    for rows in (1408, 768, 512, 256, 128):
        if M % rows == 0 and fixed + 2 * rows * (tm + tn) * 2 <= MATMUL_VMEM:
            return rows
    return M


_MM_KINDS = {
    "v1": ((1, 0), "out[:, g] = A @ W[g]"),
    "v2": ((1, 0), "out = sum_g A[:, g] @ W[g]"),
    "v3": ((1, 1), "out = sum_g A[:, g] @ W[g]^T"),
    "v4": ((1, 1), "out[:, g] = A @ W[g]^T"),
    "v5": ((0, 0), "out[g] = A^T @ C[:, g]"),
    "v6": ((0, 0), "out[g] = A[:, g]^T @ C"),
}


def matmul(kind, a, b, *, widx=(), out_dtype=f32, init=None, gshape=None, comm=None, name):
    nw = len(widx)
    cdims = _MM_KINDS[kind][0]
    if kind in ("v1", "v2", "v3", "v4"):
        G = b.shape[0]
        d1, d2 = b.shape[-2:]
        M = a.shape[0]
    else:
        G, d1, d2 = gshape
        M = a.shape[0]
    tm_p, tn_p, tk_p = (768, 512, 256, 128), (1408, 1536, 1024, 768, 512, 256, 128), (2048, 1408, 1536, 1024, 768, 512, 256, 128)
    wnone = (None,) * (1 + nw)

    if kind == "v1":
        K, Ns = d1, d2
        tm, tn, tk = _pick(M, tm_p), _pick(Ns, tn_p), _pick(K, tk_p)
        nI, nJ, nR = M // tm, Ns // tn, K // tk
        grid = (G, nI, nJ, nR)
        a_spec = pl.BlockSpec((tm, tk), lambda g, i, j, r: (i, r))
        b_spec = pl.BlockSpec(wnone + (tk, tn), lambda g, i, j, r: (g,) + widx + (r, j))
        o_spec = pl.BlockSpec((tm, tn), lambda g, i, j, r: (i, g * nJ + j))
        out_shape = _sds((M, G * Ns), out_dtype)
        acc_shape = (tm, tn)
    elif kind == "v2":
        Ks, N = d1, d2
        tm, tn, tk = _pick(M, tm_p), _pick(N, (2048,) + tn_p), _pick(Ks, tk_p)
        nI, nJ, nRk = M // tm, N // tn, Ks // tk
        nR = G * nRk
        grid = (1, nI, nJ, nR)
        a_spec = pl.BlockSpec((tm, tk), lambda g, i, j, r: (i, r))
        b_spec = pl.BlockSpec(wnone + (tk, tn), lambda g, i, j, r: (r // nRk,) + widx + (r % nRk, j))
        o_spec = pl.BlockSpec((tm, tn), lambda g, i, j, r: (i, j))
        out_shape = _sds((M, N), out_dtype)
        acc_shape = (tm, tn)
    elif kind == "v3":
        K, Ns = d1, d2
        tm, tn, tk = _pick(M, tm_p), _pick(K, (2048,) + tn_p), _pick(Ns, tk_p)
        nI, nJ, nRk = M // tm, K // tn, Ns // tk
        nR = G * nRk
        grid = (1, nI, nJ, nR)
        a_spec = pl.BlockSpec((tm, tk), lambda g, i, j, r: (i, r))
        b_spec = pl.BlockSpec(wnone + (tn, tk), lambda g, i, j, r: (r // nRk,) + widx + (j, r % nRk))
        o_spec = pl.BlockSpec((tm, tn), lambda g, i, j, r: (i, j))
        out_shape = _sds((M, K), out_dtype)
        acc_shape = (tm, tn)
    elif kind == "v4":
        Ks, N = d1, d2
        tm, tn, tk = _pick(M, tm_p), _pick(Ks, tn_p), _pick(N, tk_p)
        nI, nJ, nR = M // tm, Ks // tn, N // tk
        grid = (G, nI, nJ, nR)
        a_spec = pl.BlockSpec((tm, tk), lambda g, i, j, r: (i, r))
        b_spec = pl.BlockSpec(wnone + (tn, tk), lambda g, i, j, r: (g,) + widx + (j, r))
        o_spec = pl.BlockSpec((tm, tn), lambda g, i, j, r: (i, g * nJ + j))
        out_shape = _sds((M, G * Ks), out_dtype)
        acc_shape = (tm, tn)
    elif kind == "v5":
        K, Ns = d1, d2
        tm, tn, tk = _pick(K, (2048,) + tm_p), _pick(Ns, tn_p), 0
        tk = _fit_rows(M, tm, tn, out_dtype)
        nI, nJ, nR = K // tm, Ns // tn, M // tk
        grid = (G, nI, nJ, nR)
        a_spec = pl.BlockSpec((tk, tm), lambda g, i, j, r: (r, i))
        b_spec = pl.BlockSpec((tk, tn), lambda g, i, j, r: (r, g * nJ + j))
        o_spec = pl.BlockSpec((None, tm, tn), lambda g, i, j, r: (g, i, j))
        out_shape = _sds(gshape, out_dtype)
        acc_shape = (tm, tn)
    else:
        Ks, N = d1, d2
        tm, tn, tk = _pick(Ks, (1408,) + tm_p), _pick(N, (2048,) + tn_p), 0
        tk = _fit_rows(M, tm, tn, out_dtype)
        nI, nJ, nR = Ks // tm, N // tn, M // tk
        grid = (G, nI, nJ, nR)
        a_spec = pl.BlockSpec((tk, tm), lambda g, i, j, r: (r, g * nI + i))
        b_spec = pl.BlockSpec((tk, tn), lambda g, i, j, r: (r, j))
        o_spec = pl.BlockSpec((None, tm, tn), lambda g, i, j, r: (g, i, j))
        out_shape = _sds(gshape, out_dtype)
        acc_shape = (tm, tn)

    has_init = init is not None
    ncomm = len(comm.srcs) if comm is not None else 0

    def body(*refs):
        a_ref, b_ref = refs[0], refs[1]
        pos = 2
        init_ref = None
        if has_init:
            init_ref = refs[pos]
            pos += 1
        cin = refs[pos:pos + ncomm]
        pos += ncomm
        o_ref = refs[pos]
        cout = refs[pos + 1:pos + 1 + ncomm]
        acc_ref = refs[pos + 1 + ncomm]
        sems = refs[pos + 2 + ncomm:]
        r = pl.program_id(3)
        first, last = _grid_ends(grid)

        if ncomm:
            @pl.when(first)
            def _():
                _comm_start(comm, cin, cout, *sems)

        def prod():
            return lax.dot_general(a_ref[...], b_ref[...], ((cdims[:1], cdims[1:]), ((), ())), preferred_element_type=f32)

        def start():
            return init_ref[...] + prod() if has_init else prod()

        if nR == 1:
            o_ref[...] = start().astype(o_ref.dtype)
        else:
            @pl.when(r == 0)
            def _():
                acc_ref[...] = start()

            @pl.when(jnp.logical_and(r > 0, r < nR - 1))
            def _():
                acc_ref[...] += prod()

            @pl.when(r == nR - 1)
            def _():
                o_ref[...] = (acc_ref[...] + prod()).astype(o_ref.dtype)

        if ncomm:
            @pl.when(last)
            def _():
                _comm_wait(comm, cin, cout, *sems)

    in_specs = [a_spec, b_spec]
    args = [a, b]
    if has_init:
        in_specs.append(pl.BlockSpec((tm, tn), lambda g, i, j, r: (i, j)))
        args.append(init)
    out_specs, out_shapes, scratch = [o_spec], [out_shape], [pltpu.VMEM(acc_shape, f32)]
    if ncomm:
        hbm = pl.BlockSpec(memory_space=pl.ANY)
        in_specs += [hbm] * ncomm
        args += [src for src, _ in comm.srcs]
        out_specs += [hbm] * ncomm
        out_shapes += comm.out_shapes()
        scratch += _comm_sems(ncomm)
    res = pl.pallas_call(
        body, name=name, grid=grid, in_specs=in_specs, out_specs=out_specs, out_shape=out_shapes,
        scratch_shapes=scratch, compiler_params=_params(("arbitrary", "arbitrary", "arbitrary", "arbitrary")),
    )(*args)
    return (res[0], list(res[1:])) if ncomm else res[0]


class Comm:
    def __init__(self, mode, srcs):
        self.mode, self.srcs = mode, srcs

    def piece(self, n):
        arr, idx = self.srcs[n]
        shp = arr.shape[len(idx):]
        return shp if self.mode == "gather" else shp[1:]

    def out_shapes(self):
        return [_sds((N_CHIPS,) + tuple(self.piece(n)), self.srcs[n][0].dtype) for n in range(len(self.srcs))]


def _comm_sems(n):
    nsem = n * (N_CHIPS - 1)
    return [pltpu.SemaphoreType.DMA((nsem,)), pltpu.SemaphoreType.DMA((nsem,)), pltpu.SemaphoreType.DMA((n,))]


def _coords():
    return lax.axis_index("x"), lax.axis_index("y"), lax.axis_index("c")


def _flip(v, bit):
    return 1 - v if bit else v


def _chip_peers(x, y, c):
    out = []
    for k in range(1, N_CHIPS):
        kx, ky = (k >> 1) & 1, k & 1
        px, py = _flip(x, kx), _flip(y, ky)
        out.append((k, (px, py, c), 2 * px + py))
    return out


def _comm_copies(comm, in_refs, out_refs, send_sems, recv_sems, local_sems, with_recvs):
    x, y, c = _coords()
    s = 2 * x + y
    local, sends, recvs = [], [], []
    for w, (_, idx) in enumerate(comm.srcs):
        src = in_refs[w].at[idx] if idx else in_refs[w]
        out = out_refs[w]
        if comm.mode == "gather":
            local.append(pltpu.make_async_copy(src, out.at[s], local_sems.at[w]))
        else:
            local.append(pltpu.make_async_copy(src.at[s], out.at[N_CHIPS - 1], local_sems.at[w]))
        for k, peer, pidx in _chip_peers(x, y, c):
            j = w * (N_CHIPS - 1) + k - 1
            if comm.mode == "gather":
                out_src, out_dst, in_dst = src, out.at[s], out.at[pidx]
            else:
                out_src, out_dst, in_dst = src.at[pidx], out.at[k - 1], out.at[k - 1]
            sends.append(pltpu.make_async_remote_copy(src_ref=out_src, dst_ref=out_dst, send_sem=send_sems.at[j],
                                                      recv_sem=recv_sems.at[j], device_id=peer, device_id_type=MESH))
            if with_recvs:
                recvs.append(pltpu.make_async_remote_copy(src_ref=out_src, dst_ref=in_dst, send_sem=send_sems.at[j],
                                                          recv_sem=recv_sems.at[j], device_id=peer, device_id_type=MESH))
    return local, sends, recvs


def _comm_start(comm, in_refs, out_refs, send_sems, recv_sems, local_sems):
    local, sends, _ = _comm_copies(comm, in_refs, out_refs, send_sems, recv_sems, local_sems, False)
    for cp in local + sends:
        cp.start()


def _comm_wait(comm, in_refs, out_refs, send_sems, recv_sems, local_sems):
    local, sends, recvs = _comm_copies(comm, in_refs, out_refs, send_sems, recv_sems, local_sems, True)
    for cp in recvs:
        cp.wait_recv()
    for cp in sends:
        cp.wait_send()
    for cp in local:
        cp.wait()


def exchange(comm, name):
    n = len(comm.srcs)

    def body(*refs):
        in_refs, out_refs, sems = refs[:n], refs[n:2 * n], refs[2 * n:]
        _comm_start(comm, in_refs, out_refs, *sems)
        _comm_wait(comm, in_refs, out_refs, *sems)

    hbm = pl.BlockSpec(memory_space=pl.ANY)
    return pl.pallas_call(body, name=name, out_shape=comm.out_shapes(), in_specs=[hbm] * n, out_specs=[hbm] * n,
                          scratch_shapes=_comm_sems(n))(*[src for src, _ in comm.srcs])


class RowCfg:
    def __init__(self, TR, nT, cT):
        self.TR, self.nT, self.cT = TR, nT, cT


def _row_spec(cfg, spec, off):
    kind = spec[0]
    TR = cfg.TR
    hb = TR // HALO
    nH = cfg.nT * hb
    if kind == "row":
        _, arr, w, cb = spec
        return pl.BlockSpec((TR, w), lambda i: (i + off, cb))
    if kind == "prev":
        _, arr, w, cb = spec
        return pl.BlockSpec((HALO, w), lambda i: (jnp.maximum((i + off) * hb - 1, 0), cb))
    if kind == "next":
        _, arr, w, cb = spec
        return pl.BlockSpec((HALO, w), lambda i: (jnp.minimum((i + off + 1) * hb, nH - 1), cb))
    if kind == "full":
        arr = spec[1]
        nd = arr.ndim
        return pl.BlockSpec(arr.shape, lambda i: (0,) * nd)
    if kind == "grp":
        arr = spec[1]
        cT = cfg.cT
        return pl.BlockSpec((None, 1, arr.shape[-1]), lambda i: (((i + off) >= cT).astype(jnp.int32), 0, 0))
    if kind == "drow":
        _, arr, w, cb = spec
        return pl.BlockSpec((arr.shape[0], TR, w), lambda i: (0, i + off, cb))
    raise ValueError(kind)


def rowcall(cfg, fn, name, ins, outs, *, off=0, n=None, scratch=()):
    n = cfg.nT - off if n is None else n
    in_specs = [_row_spec(cfg, s, off) for s in ins]
    out_specs = [_row_spec(cfg, (s[0], s[1]) + tuple(s[2:]), off) for s in outs]
    out_shape = [s[1] for s in outs]

    def body(*refs):
        fn(pl.program_id(0) + off, *refs)

    res = pl.pallas_call(
        body, name=name, grid=(n,), in_specs=in_specs, out_specs=out_specs, out_shape=out_shape,
        scratch_shapes=list(scratch), compiler_params=_params(("arbitrary",)),
    )(*[s[1] for s in ins])
    return res


def _acc(ref, val, first):
    @pl.when(first)
    def _():
        ref[...] = val

    @pl.when(jnp.logical_not(first))
    def _():
        ref[...] += val


def _rms(x):
    return x * lax.rsqrt(jnp.mean(x * x, axis=-1, keepdims=True) + EPS)


def _pre_fn(x, g, shift, scale):
    return (_rms(x) * g) * (1.0 + scale) + shift


def _strips(TR, rows_per_strip, body, init):
    n = TR // rows_per_strip
    unroll = STRIP_UNROLL if n % STRIP_UNROLL == 0 else 1

    def step(r, carry):
        for u in range(unroll):
            start = pl.multiple_of((r * unroll + u) * rows_per_strip, rows_per_strip)
            carry = body(pl.ds(start, rows_per_strip), carry)
        return carry

    return lax.fori_loop(0, n // unroll, step, init)


STRIP_UNROLL = 8
F32_STRIP = 8
BF16_STRIP = 16


def _inv_rms(x):
    return lax.rsqrt(jnp.mean(x * x, axis=-1, keepdims=True) + EPS)


def pre_fwd(cfg, x, g, shift, scale, name):
    D = x.shape[1]

    def fn(i, x_ref, g_ref, sh_ref, sc_ref, h_ref):
        c = g_ref[...] * (1.0 + sc_ref[...])
        sh = sh_ref[...]

        def strip(rows, carry):
            xv = x_ref[rows, :]
            h_ref[rows, :] = (xv * _inv_rms(xv) * c + sh).astype(bf16)
            return carry

        _strips(cfg.TR, BF16_STRIP, strip, 0)

    return rowcall(cfg, fn, name, [("row", x, D, 0), ("full", g), ("grp", shift), ("grp", scale)],
                   [("row", _sds(x.shape, bf16), D, 0)])[0]


def pre_bwd(cfg, x, g, shift, scale, dh, dx_in, name):
    D = x.shape[1]
    cT = cfg.cT

    def fn(i, x_ref, g_ref, sh_ref, sc_ref, dh_ref, dxin_ref, dx_ref, dg_ref, dsh_ref, dsc_ref):
        gv, scv = g_ref[...], sc_ref[...]
        c = gv * (1.0 + scv)

        def strip(rows, carry):
            s0, s1 = carry
            xv, dhv = x_ref[rows, :], dh_ref[rows, :]
            r = _inv_rms(xv)
            xn = xv * r
            dxn = dhv * c
            m = jnp.mean(dxn * xn, axis=-1, keepdims=True)
            dx_ref[rows, :] = dxin_ref[rows, :] + r * (dxn - xn * m)
            return s0 + dhv, s1 + dhv * xn

        zero = jnp.zeros((F32_STRIP, D), f32)
        s0, s1 = _strips(cfg.TR, F32_STRIP, strip, (zero, zero))
        s0 = jnp.sum(s0, axis=0, keepdims=True)
        s1 = jnp.sum(s1, axis=0, keepdims=True)
        _acc(dg_ref, (1.0 + scv) * s1, i == 0)
        first = jnp.logical_or(i == 0, i == cT)
        _acc(dsh_ref, s0, first)
        _acc(dsc_ref, gv * s1, first)

    return rowcall(cfg, fn, name,
                   [("row", x, D, 0), ("full", g), ("grp", shift), ("grp", scale), ("row", dh, D, 0), ("row", dx_in, D, 0)],
                   [("row", _sds(x.shape, f32), D, 0), ("full", _sds((1, D), f32)),
                    ("grp", _sds((2, 1, D), f32)), ("grp", _sds((2, 1, D), f32))])


def _post_fn(w, y, g, gate):
    return (w * gate) * (_rms(y) * g)


def post_fwd(cfg, x, y, g, gate, w, name):
    D = x.shape[1]

    def fn(i, x_ref, y_ref, g_ref, gt_ref, o_ref):
        c = (w * gt_ref[...]) * g_ref[...]

        def strip(rows, carry):
            yv = y_ref[rows, :]
            o_ref[rows, :] = x_ref[rows, :] + c * (yv * _inv_rms(yv))
            return carry

        _strips(cfg.TR, F32_STRIP, strip, 0)

    return rowcall(cfg, fn, name, [("row", x, D, 0), ("row", y, D, 0), ("full", g), ("grp", gate)],
                   [("row", _sds(x.shape, f32), D, 0)])[0]


def post_bwd(cfg, dx, y, g, gate, w, name):
    D = dx.shape[1]
    cT = cfg.cT
    half = BF16_STRIP // 2

    def fn(i, dx_ref, y_ref, g_ref, gt_ref, dy_ref, dg_ref, dgt_ref):
        gv, gtv = g_ref[...], gt_ref[...]
        c = (w * gtv) * gv

        def strip(rows, s1):
            yv, dv = y_ref[rows, :], dx_ref[rows, :]
            r = _inv_rms(yv)
            yn = yv * r
            dyn = dv * c
            m = jnp.mean(dyn * yn, axis=-1, keepdims=True)
            dy_ref[rows, :] = (r * (dyn - yn * m)).astype(bf16)
            t = dv * yn
            return s1 + t[:half] + t[half:]

        s1 = _strips(cfg.TR, BF16_STRIP, strip, jnp.zeros((half, D), f32))
        s1 = jnp.sum(s1, axis=0, keepdims=True)
        _acc(dg_ref, (w * gtv) * s1, i == 0)
        _acc(dgt_ref, (w * gv) * s1, jnp.logical_or(i == 0, i == cT))

    return rowcall(cfg, fn, name, [("row", dx, D, 0), ("row", y, D, 0), ("full", g), ("grp", gate)],
                   [("row", _sds(dx.shape, bf16), D, 0), ("full", _sds((1, D), f32)), ("grp", _sds((2, 1, D), f32))])


def _swiglu_fn(a, b):
    return jax.nn.silu(a) * b


def swiglu_fwd(cfg, a, b, name):
    F = a.shape[1]
    tf = _pick(F, (1408, 1024, 512, 256, 128))
    TR = cfg.TR

    def body(a_ref, b_ref, u_ref):
        u_ref[...] = _swiglu_fn(a_ref[...], b_ref[...]).astype(bf16)

    spec = pl.BlockSpec((TR, tf), lambda i, j: (i, j))
    return pl.pallas_call(body, name=name, grid=(cfg.nT, F // tf), in_specs=[spec, spec], out_specs=spec,
                          out_shape=_sds(a.shape, bf16), compiler_params=_params(("arbitrary", "arbitrary")))(a, b)


def swiglu_bwd(cfg, a, b, du, name):
    F = a.shape[1]
    tf = _pick(F, (1408, 1024, 512, 256, 128))
    TR = cfg.TR

    def body(a_ref, b_ref, du_ref, da_ref, db_ref):
        _, vjp = jax.vjp(_swiglu_fn, a_ref[...], b_ref[...])
        da, db = vjp(du_ref[...])
        da_ref[...] = da.astype(bf16)
        db_ref[...] = db.astype(bf16)

    spec = pl.BlockSpec((TR, tf), lambda i, j: (i, j))
    return pl.pallas_call(body, name=name, grid=(cfg.nT, F // tf), in_specs=[spec, spec, spec], out_specs=[spec, spec],
                          out_shape=[_sds(a.shape, bf16), _sds(a.shape, bf16)],
                          compiler_params=_params(("arbitrary", "arbitrary")))(a, b, du)


def _hosted(host, role, got, fn):
    comm = host.get(role) if host else None
    if comm is None:
        return fn(None)
    out, res = fn(comm)
    got[role] = res
    return out


def _grid_ends(grid):
    ids = [pl.program_id(n) for n in range(len(grid))]
    first = functools.reduce(jnp.logical_and, [i == 0 for i in ids])
    last = functools.reduce(jnp.logical_and, [i == n - 1 for i, n in zip(ids, grid)])
    return first, last


def _comm_plumbing(comm):
    if comm is None:
        return [], [], [], [], []
    n = len(comm.srcs)
    hbm = pl.BlockSpec(memory_space=pl.ANY)
    return [hbm] * n, [src for src, _ in comm.srcs], [hbm] * n, comm.out_shapes(), _comm_sems(n)


FFN_ROWS = 384


def ffn_gateup(cfg, h, wg, wu, name, comm=None):
    M, K = h.shape
    G, _, F = wg.shape
    tm = _pick(M, (FFN_ROWS, 256, 128))
    grid = (G, M // tm)
    ncomm = len(comm.srcs) if comm is not None else 0

    def body(h_ref, wg_ref, wu_ref, *rest):
        cin, rest = rest[:ncomm], rest[ncomm:]
        s_ref, t_ref, u_ref = rest[:3]
        cout, sems = rest[3:3 + ncomm], rest[3 + ncomm:]
        first, last = _grid_ends(grid)
        if ncomm:
            @pl.when(first)
            def _():
                _comm_start(comm, cin, cout, *sems)

        hv = h_ref[...]
        a = jnp.dot(hv, wg_ref[...], preferred_element_type=f32)
        b = jnp.dot(hv, wu_ref[...], preferred_element_type=f32)
        sig = jax.nn.sigmoid(a)
        sa = a * sig
        s_ref[...] = sa.astype(bf16)
        t_ref[...] = (b * (sig + sa * (1.0 - sig))).astype(bf16)
        u_ref[...] = (sa * b).astype(bf16)
        if ncomm:
            @pl.when(last)
            def _():
                _comm_wait(comm, cin, cout, *sems)

    ci, ca, co, cs, csem = _comm_plumbing(comm)
    w_spec = pl.BlockSpec((None, K, F), lambda g, i: (g, 0, 0))
    o_spec = pl.BlockSpec((tm, F), lambda g, i: (i, g))
    res = pl.pallas_call(
        body, name=name, grid=grid, in_specs=[pl.BlockSpec((tm, K), lambda g, i: (i, 0)), w_spec, w_spec] + ci,
        out_specs=[o_spec] * 3 + co, out_shape=[_sds((M, G * F), bf16)] * 3 + cs, scratch_shapes=csem,
        compiler_params=_params(("arbitrary", "arbitrary")))(h, wg, wu, *ca)
    return (res[0], res[1], res[2]), list(res[3:])


def ffn_du_act(cfg, dy, wd, s, t, name, comm=None):
    M, N = dy.shape
    G, F, _ = wd.shape
    tm = _pick(M, (FFN_ROWS, 256, 128))
    grid = (G, M // tm)
    ncomm = len(comm.srcs) if comm is not None else 0

    def body(dy_ref, wd_ref, s_ref, t_ref, *rest):
        cin, rest = rest[:ncomm], rest[ncomm:]
        da_ref, db_ref = rest[:2]
        cout, sems = rest[2:2 + ncomm], rest[2 + ncomm:]
        first, last = _grid_ends(grid)
        if ncomm:
            @pl.when(first)
            def _():
                _comm_start(comm, cin, cout, *sems)

        du = _dotf(dy_ref[...], wd_ref[...], 1, 1)
        da_ref[...] = (du * t_ref[...].astype(f32)).astype(bf16)
        db_ref[...] = (du * s_ref[...].astype(f32)).astype(bf16)
        if ncomm:
            @pl.when(last)
            def _():
                _comm_wait(comm, cin, cout, *sems)

    ci, ca, co, cs, csem = _comm_plumbing(comm)
    t_spec = pl.BlockSpec((tm, F), lambda g, i: (i, g))
    res = pl.pallas_call(
        body, name=name, grid=grid,
        in_specs=[pl.BlockSpec((tm, N), lambda g, i: (i, 0)), pl.BlockSpec((None, F, N), lambda g, i: (g, 0, 0)), t_spec,
                  t_spec] + ci,
        out_specs=[t_spec, t_spec] + co, out_shape=[_sds((M, G * F), bf16)] * 2 + cs, scratch_shapes=csem,
        compiler_params=_params(("arbitrary", "arbitrary")))(dy, wd, s, t, *ca)
    return (res[0], res[1]), list(res[2:])


def ffn_dh(cfg, da, db, wg, wu, name, comm=None):
    M = da.shape[0]
    G, K, F = wg.shape
    tm = _pick(M, (768, 512, 256, 128))
    nR = 2 * G
    grid = (M // tm, nR)
    ncomm = len(comm.srcs) if comm is not None else 0

    def body(da_ref, db_ref, wg_ref, wu_ref, *rest):
        cin, rest = rest[:ncomm], rest[ncomm:]
        o_ref = rest[0]
        cout, acc_ref, sems = rest[1:1 + ncomm], rest[1 + ncomm], rest[2 + ncomm:]
        r = pl.program_id(1)
        first, last = _grid_ends(grid)
        if ncomm:
            @pl.when(first)
            def _():
                _comm_start(comm, cin, cout, *sems)

        @pl.when(r == 0)
        def _():
            acc_ref[...] = _dotf(da_ref[...], wg_ref[...], 1, 1)

        @pl.when(jnp.logical_and(r > 0, r < G))
        def _():
            acc_ref[...] += _dotf(da_ref[...], wg_ref[...], 1, 1)

        @pl.when(jnp.logical_and(r >= G, r < nR - 1))
        def _():
            acc_ref[...] += _dotf(db_ref[...], wu_ref[...], 1, 1)

        @pl.when(r == nR - 1)
        def _():
            o_ref[...] = acc_ref[...] + _dotf(db_ref[...], wu_ref[...], 1, 1)

        if ncomm:
            @pl.when(last)
            def _():
                _comm_wait(comm, cin, cout, *sems)

    ga = lambda r: jnp.minimum(r, G - 1)
    gb = lambda r: jnp.maximum(r - G, 0)
    ci, ca, co, cs, csem = _comm_plumbing(comm)
    res = pl.pallas_call(
        body, name=name, grid=grid,
        in_specs=[pl.BlockSpec((tm, F), lambda i, r: (i, ga(r))), pl.BlockSpec((tm, F), lambda i, r: (i, gb(r))),
                  pl.BlockSpec((None, K, F), lambda i, r: (ga(r), 0, 0)),
                  pl.BlockSpec((None, K, F), lambda i, r: (gb(r), 0, 0))] + ci,
        out_specs=[pl.BlockSpec((tm, K), lambda i, r: (i, 0))] + co, out_shape=[_sds((M, K), f32)] + cs,
        scratch_shapes=[pltpu.VMEM((tm, K), f32)] + csem,
        compiler_params=_params(("arbitrary", "arbitrary")))(da, db, wg, wu, *ca)
    return res[0], list(res[1:])


def ffn_fwd(cfg, x, p, tag, host=None):
    got = {}
    host = host or {}
    h = pre_fwd(cfg, x, p["g_pre"], p["shift"], p["scale"], tag + "_pre")
    (a, b, u), res = ffn_gateup(cfg, h, p["wg"], p["wu"], tag + "_gateup", comm=host.get("gateup"))
    if res:
        got["gateup"] = res
    y = _hosted(host, "down", got, lambda cm: matmul("v2", u, p["wd"], comm=cm, name=tag + "_down"))
    xo = post_fwd(cfg, x, y, p["g_post"], p["gate"], FFN_STEP, tag + "_post")
    return xo, (x, h, a, b, u, y), got


def ffn_bwd(cfg, dX, saved, p, tag):
    x, h, a, b, u, y = saved
    dy, dg_post, dgate = post_bwd(cfg, dX, y, p["g_post"], p["gate"], FFN_STEP, tag + "_postb")
    (da, db), _ = ffn_du_act(cfg, dy, p["wd"], a, b, tag + "_duact")
    gwd = matmul("v6", u, dy, gshape=p["wd"].shape, out_dtype=bf16, name=tag + "_gwd")
    gwg = matmul("v5", h, da, gshape=p["wg"].shape, out_dtype=bf16, name=tag + "_gwg")
    gwu, r_wd = matmul("v5", h, db, gshape=p["wu"].shape, out_dtype=bf16, comm=Comm("scatter", [(gwd, ())]),
                       name=tag + "_gwu")
    dh, (r_wg, r_wu) = ffn_dh(cfg, da, db, p["wg"], p["wu"], tag + "_dh", comm=Comm("scatter", [(gwg, ()), (gwu, ())]))
    dX, dg_pre, dshift, dscale = pre_bwd(cfg, x, p["g_pre"], p["shift"], p["scale"], dh, dX, tag + "_preb")
    small = dict(g_pre=dg_pre, g_post=dg_post, shift=dshift, scale=dscale, gate=dgate)
    return dX, small, dict(wg=r_wg, wu=r_wu, wd=r_wd[0])


def _seg_flags(cfg, i):
    start = jnp.logical_or(i == 0, i == cfg.cT)
    end = jnp.logical_or(i == cfg.cT - 1, i == cfg.nT - 1)
    return start, end


def _fill_halo(buf, cur, prev, nxt, start, end, TR):
    buf[pl.ds(0, HALO), :] = jnp.where(start, 0.0, prev)
    buf[pl.ds(HALO, TR), :] = cur
    buf[pl.ds(HALO + TR, HALO), :] = jnp.where(end, 0.0, nxt)


def conv_fwd(cfg, z, cw, cb, W, name):
    TR = cfg.TR

    def fn(i, r_ref, rp_ref, rn_ref, cw_ref, cb_ref, u_ref, buf):
        start, end = _seg_flags(cfg, i)
        _fill_halo(buf, r_ref[...], rp_ref[...], rn_ref[...], start, end, TR)
        u = jnp.broadcast_to(cb_ref[...], (TR, W))
        for k in range(CONV_W):
            u = u + buf[pl.ds(HALO + k - CONV_LEFT, TR), :] * cw_ref[pl.ds(k, 1), :]
        u_ref[...] = u

    return rowcall(cfg, fn, name, [("row", z, W, 1), ("prev", z, W, 1), ("next", z, W, 1), ("full", cw), ("full", cb)],
                   [("row", _sds((z.shape[0], W), f32), W, 0)], scratch=[pltpu.VMEM((TR + 2 * HALO, W), f32)])[0]


def conv_bwd(cfg, z, du, cw, W, name):
    TR = cfg.TR

    def fn(i, r_ref, rp_ref, rn_ref, du_ref, dup_ref, dun_ref, cw_ref, dr_ref, dcw_ref, dcb_ref, rbuf, dbuf):
        start, end = _seg_flags(cfg, i)
        _fill_halo(rbuf, r_ref[...], rp_ref[...], rn_ref[...], start, end, TR)
        _fill_halo(dbuf, du_ref[...], dup_ref[...], dun_ref[...], start, end, TR)
        du = du_ref[...]

        @pl.when(i == 0)
        def _():
            dcw_ref[...] = jnp.zeros(dcw_ref.shape, f32)
            dcb_ref[...] = jnp.zeros(dcb_ref.shape, f32)

        dr = jnp.zeros((TR, W), f32)
        for k in range(CONV_W):
            dr = dr + dbuf[pl.ds(HALO - (k - CONV_LEFT), TR), :] * cw_ref[pl.ds(k, 1), :]
            dcw_ref[pl.ds(k, 1), :] += jnp.sum(du * rbuf[pl.ds(HALO + k - CONV_LEFT, TR), :], axis=0, keepdims=True)
        dcb_ref[...] += jnp.sum(du, axis=0, keepdims=True)
        dr_ref[...] = dr

    T = z.shape[0]
    return rowcall(cfg, fn, name,
                   [("row", z, W, 1), ("prev", z, W, 1), ("next", z, W, 1), ("row", du, W, 0), ("prev", du, W, 0),
                    ("next", du, W, 0), ("full", cw)],
                   [("row", _sds((T, W), f32), W, 0), ("full", _sds((CONV_W, W), f32)), ("full", _sds((1, W), f32))],
                   scratch=[pltpu.VMEM((TR + 2 * HALO, W), f32), pltpu.VMEM((TR + 2 * HALO, W), f32)])


def _softplus(x):
    return jnp.maximum(x, 0.0) + jnp.log1p(jnp.exp(-jnp.abs(x)))


def _neg_expm1(x):
    series = -x * (1.0 + x * (0.5 + x * (1.0 / 6.0 + x * (1.0 / 24.0 + x * (1.0 / 120.0 + x * (1.0 / 720.0))))))
    return jnp.where(x > -0.1, series, 1.0 - jnp.exp(x))


def _lru_coef(u, pa, px, lam):
    r = jax.nn.sigmoid(pa)
    i = jax.nn.sigmoid(px)
    log_a = -LRU_C * r * _softplus(-lam)
    a = jnp.exp(log_a)
    b = jnp.sqrt(_neg_expm1(2.0 * log_a)) * (i * u)
    return a, b


def _blockdiag(u_bf, w_ref, d, nblk, blk):
    return jnp.concatenate(
        [jnp.dot(u_bf[:, n * blk:(n + 1) * blk], w_ref[d, n].astype(bf16), preferred_element_type=f32)
         for n in range(nblk)], axis=1)


def lru_coef_fwd(cfg, u, wa, ba, wx, bx, lam, name):
    T, W = u.shape
    nblk, blk = wa.shape[1], wa.shape[2]

    def fn(i, u_ref, wa_ref, ba_ref, wx_ref, bx_ref, lam_ref, a_ref, b_ref):
        uv = u_ref[...]
        u_bf = uv.astype(bf16)
        for d in range(2):
            pa = _blockdiag(u_bf, wa_ref, d, nblk, blk) + ba_ref[d]
            px = _blockdiag(u_bf, wx_ref, d, nblk, blk) + bx_ref[d]
            a, b = _lru_coef(uv, pa, px, lam_ref[d])
            a_ref[d] = a
            b_ref[d] = b

    return rowcall(cfg, fn, name, [("row", u, W, 0), ("full", wa), ("full", ba), ("full", wx), ("full", bx), ("full", lam)],
                   [("drow", _sds((2, T, W), f32), W, 0), ("drow", _sds((2, T, W), f32), W, 0)])


def lru_coef_bwd(cfg, u, da, db, wa, ba, wx, bx, lam, name):
    T, W = u.shape
    nblk, blk = wa.shape[1], wa.shape[2]

    def fn(i, u_ref, da_ref, db_ref, wa_ref, ba_ref, wx_ref, bx_ref, lam_ref,
           du_ref, dwa_ref, dba_ref, dwx_ref, dbx_ref, dlam_ref):
        @pl.when(i == 0)
        def _():
            for r in (dwa_ref, dba_ref, dwx_ref, dbx_ref, dlam_ref):
                r[...] = jnp.zeros(r.shape, f32)

        uv = u_ref[...]
        u_bf = uv.astype(bf16)
        du = jnp.zeros(uv.shape, f32)
        for d in range(2):
            pa = _blockdiag(u_bf, wa_ref, d, nblk, blk) + ba_ref[d]
            px = _blockdiag(u_bf, wx_ref, d, nblk, blk) + bx_ref[d]
            _, vjp = jax.vjp(_lru_coef, uv, pa, px, lam_ref[d])
            du_e, dpa, dpx, dlam = vjp((da_ref[d], db_ref[d]))
            du = du + du_e
            dba_ref[d] += jnp.sum(dpa, axis=0, keepdims=True)
            dbx_ref[d] += jnp.sum(dpx, axis=0, keepdims=True)
            dlam_ref[d] += dlam
            parts = []
            for n in range(nblk):
                sl = slice(n * blk, (n + 1) * blk)
                ga, gx = dpa[:, sl].astype(bf16), dpx[:, sl].astype(bf16)
                ub = u_bf[:, sl]
                dwa_ref[d, n] += lax.dot_general(ub, ga, (((0,), (0,)), ((), ())), preferred_element_type=f32)
                dwx_ref[d, n] += lax.dot_general(ub, gx, (((0,), (0,)), ((), ())), preferred_element_type=f32)
                parts.append(
                    lax.dot_general(ga, wa_ref[d, n].astype(bf16), (((1,), (1,)), ((), ())), preferred_element_type=f32)
                    + lax.dot_general(gx, wx_ref[d, n].astype(bf16), (((1,), (1,)), ((), ())), preferred_element_type=f32))
            du = du + jnp.concatenate(parts, axis=1)
        du_ref[...] = du

    return rowcall(cfg, fn, name,
                   [("row", u, W, 0), ("drow", da, W, 0), ("drow", db, W, 0), ("full", wa), ("full", ba), ("full", wx),
                    ("full", bx), ("full", lam)],
                   [("row", _sds((T, W), f32), W, 0), ("full", _sds(wa.shape, f32)), ("full", _sds(ba.shape, f32)),
                    ("full", _sds(wx.shape, f32)), ("full", _sds(bx.shape, f32)), ("full", _sds(lam.shape, f32))])


def _dir_tile(cfg, d, j):
    rev = jnp.where(j < cfg.cT, cfg.cT - 1 - j, cfg.nT - 1 - (j - cfg.cT))
    return jnp.where(d == 0, j, rev)


def lru_scan(cfg, a, b, name):
    _, T, W = a.shape
    TR, nT = cfg.TR, cfg.nT

    def body(a_ref, b_ref, h_ref, hp_ref, st):
        d, j = pl.program_id(0), pl.program_id(1)

        @pl.when(j == 0)
        def _():
            st[...] = jnp.zeros(st.shape, f32)

        def step(t, h):
            idx = t + d * (TR - 1 - 2 * t)
            hn = a_ref[pl.ds(idx, 1), :] * h + b_ref[pl.ds(idx, 1), :]
            hp_ref[pl.ds(idx, 1), :] = h
            h_ref[pl.ds(idx, 1), :] = hn
            return hn

        st[...] = lax.fori_loop(0, TR, step, st[...])

    spec = pl.BlockSpec((None, TR, W), lambda d, j: (d, _dir_tile(cfg, d, j), 0))
    return pl.pallas_call(body, name=name, grid=(2, nT), in_specs=[spec, spec], out_specs=[spec, spec],
                          out_shape=[_sds(a.shape, f32), _sds(a.shape, f32)], scratch_shapes=[pltpu.VMEM((1, W), f32)],
                          compiler_params=_params(("arbitrary", "arbitrary")))(a, b)


def lru_scan_bwd(cfg, a, hp, dh, name):
    _, T, W = a.shape
    TR, nT = cfg.TR, cfg.nT

    def body(a_ref, hp_ref, dh_ref, da_ref, db_ref, st):
        d, j = pl.program_id(0), pl.program_id(1)

        @pl.when(j == 0)
        def _():
            st[...] = jnp.zeros(st.shape, f32)

        def step(t, c):
            p = TR - 1 - t
            idx = p + d * (TR - 1 - 2 * p)
            g = dh_ref[pl.ds(idx, 1), :] + c
            db_ref[pl.ds(idx, 1), :] = g
            da_ref[pl.ds(idx, 1), :] = g * hp_ref[pl.ds(idx, 1), :]
            return a_ref[pl.ds(idx, 1), :] * g

        st[...] = lax.fori_loop(0, TR, step, st[...])

    spec = pl.BlockSpec((None, TR, W), lambda d, j: (d, _dir_tile(cfg, d, nT - 1 - j), 0))
    dspec = pl.BlockSpec((TR, W), lambda d, j: (_dir_tile(cfg, d, nT - 1 - j), 0))
    return pl.pallas_call(body, name=name, grid=(2, nT), in_specs=[spec, spec, dspec], out_specs=[spec, spec],
                          out_shape=[_sds(a.shape, f32), _sds(a.shape, f32)], scratch_shapes=[pltpu.VMEM((1, W), f32)],
                          compiler_params=_params(("arbitrary", "arbitrary")))(a, hp, dh)


def _lru_out_fn(gl, h0, h1):
    return jax.nn.gelu(gl) * (h0 + h1)


def lru_out_fwd(cfg, z, h, W, name):
    def fn(i, g_ref, h_ref, o_ref):
        o_ref[...] = _lru_out_fn(g_ref[...], h_ref[0], h_ref[1]).astype(bf16)

    return rowcall(cfg, fn, name, [("row", z, W, 0), ("drow", h, W, 0)], [("row", _sds((z.shape[0], W), bf16), W, 0)])[0]


def lru_out_bwd(cfg, z, h, dmix, W, name):
    def fn(i, g_ref, h_ref, d_ref, dg_ref, dh_ref):
        _, vjp = jax.vjp(_lru_out_fn, g_ref[...], h_ref[0], h_ref[1])
        dg, dh0, _ = vjp(d_ref[...])
        dg_ref[...] = dg
        dh_ref[...] = dh0

    T = z.shape[0]
    return rowcall(cfg, fn, name, [("row", z, W, 0), ("drow", h, W, 0), ("row", dmix, W, 0)],
                   [("row", _sds((T, W), f32), W, 0), ("row", _sds((T, W), f32), W, 0)])


def _rot_half(x, cos, sin):
    n = x.shape[1] // 2
    x1, x2 = x[:, :n], x[:, n:]
    return jnp.concatenate([x1 * cos - x2 * sin, x1 * sin + x2 * cos], axis=1)


def _dotf(a, b, ca, cb):
    return lax.dot_general(a, b, (((ca,), (cb,)), ((), ())), preferred_element_type=f32)


@functools.partial(jax.custom_vjp, nondiff_argnums=(2, 3))
def _dotb(a, b, ca, cb):
    return _dotf(a.astype(bf16), b.astype(bf16), ca, cb)


def _dotb_fwd(a, b, ca, cb):
    return _dotb(a, b, ca, cb), (a, b)


def _dotb_bwd(ca, cb, res, ct):
    a, b = res
    a16, b16, ct16 = a.astype(bf16), b.astype(bf16), ct.astype(bf16)
    da = _dotf(ct16, b16, 1, 1 - cb) if ca == 1 else _dotf(b16, ct16, 1 - cb, 1)
    db = _dotf(a16, ct16, 1 - ca, 0) if cb == 0 else _dotf(ct16, a16, 0, 1 - ca)
    return da, db


_dotb.defvjp(_dotb_fwd, _dotb_bwd)


def _ret_chunk(d, q, k, v, s, logit, cos, sin):
    C = q.shape[0]
    lg = -_softplus(-logit)
    qr = _rot_half(q, cos, sin)
    kr = _rot_half(k, cos, sin) * (RET_DK ** -0.5)
    ii = lax.broadcasted_iota(jnp.int32, (C, C), 0)
    jj = lax.broadcasted_iota(jnp.int32, (C, C), 1)
    diff = ((ii - jj) if d == 0 else (jj - ii)).astype(f32)
    intra = jnp.where(diff >= 0, jnp.exp(lg * jnp.maximum(diff, 0.0)), 0.0)
    pos = lax.broadcasted_iota(jnp.int32, (C, 1), 0).astype(f32)
    if d == 0:
        q_dec, k_dec = jnp.exp(lg * (pos + 1.0)), jnp.exp(lg * (C - 1.0 - pos))
    else:
        q_dec, k_dec = jnp.exp(lg * (C - pos)), jnp.exp(lg * pos)
    s_dec = jnp.exp(lg * C)
    scores = _dotb(qr, kr, 1, 1) * intra
    o = _dotb(scores, v, 1, 0) + _dotb(qr * q_dec, s, 1, 0)
    s_new = s * s_dec + _dotb(kr * k_dec, v, 0, 0)
    return o, s_new


def _chunk_cfg(cfg):
    f = cfg.TR // RET_CHUNK
    return RowCfg(RET_CHUNK, cfg.nT * f, cfg.cT * f)


def ret_fwd(cfg, z, logit, cos, sin, H, qcol, name):
    T = z.shape[0]
    cc = _chunk_cfg(cfg)
    C, nC = RET_CHUNK, cc.nT
    RV = H * RET_DV
    qb = qcol * RET_DK // RV

    def body(q_ref, k_ref, v_ref, lg_ref, cos_ref, sin_ref, o_ref, s_ref, st):
        d, j = pl.program_id(0), pl.program_id(1)

        @pl.when(j == 0)
        def _():
            st[...] = jnp.zeros(st.shape, f32)

        s_ref[...] = st[...]
        for dd in range(2):
            @pl.when(d == dd)
            def _():
                outs = []
                for h in range(H):
                    cols = slice(h * RET_DK, (h + 1) * RET_DK)
                    o, s_new = _ret_chunk(dd, q_ref[:, cols], k_ref[:, cols], v_ref[:, cols], st[h], lg_ref[h],
                                          cos_ref[...], sin_ref[...])
                    outs.append(o)
                    st[h] = s_new
                o_ref[...] = jnp.concatenate(outs, axis=1)

    tile = lambda d, j: _dir_tile(cc, d, j)
    zq = pl.BlockSpec((C, RV), lambda d, j: (tile(d, j), qb))
    zk = pl.BlockSpec((C, RV), lambda d, j: (tile(d, j), qb + 1))
    zv = pl.BlockSpec((C, RV), lambda d, j: (tile(d, j), qb + 2))
    lgs = pl.BlockSpec((None, H, 1, 1), lambda d, j: (d, 0, 0, 0))
    cs = pl.BlockSpec((C, RET_DK // 2), lambda d, j: (tile(d, j), 0))
    o_spec = pl.BlockSpec((None, C, RV), lambda d, j: (d, tile(d, j), 0))
    s_spec = pl.BlockSpec((None, H, None, RET_DK, RET_DV), lambda d, j: (d, 0, tile(d, j), 0, 0))
    return pl.pallas_call(
        body, name=name, grid=(2, nC), in_specs=[zq, zk, zv, lgs, cs, cs], out_specs=[o_spec, s_spec],
        out_shape=[_sds((2, T, RV), f32), _sds((2, H, nC, RET_DK, RET_DV), f32)],
        scratch_shapes=[pltpu.VMEM((H, RET_DK, RET_DV), f32)],
        compiler_params=_params(("arbitrary", "arbitrary")))(z, z, z, logit, cos, sin)


def ret_bwd(cfg, z, states, do, logit, cos, sin, H, qcol, name):
    T = z.shape[0]
    cc = _chunk_cfg(cfg)
    C, nC = RET_CHUNK, cc.nT
    RV = H * RET_DV
    qb = qcol * RET_DK // RV

    def body(q_ref, k_ref, v_ref, s_ref, do_ref, lg_ref, cos_ref, sin_ref, dq_ref, dk_ref, dv_ref, dlg_ref, st):
        d, j = pl.program_id(0), pl.program_id(1)

        @pl.when(j == 0)
        def _():
            st[...] = jnp.zeros(st.shape, f32)
            dlg_ref[...] = jnp.zeros(dlg_ref.shape, f32)

        for dd in range(2):
            @pl.when(d == dd)
            def _():
                fn = lambda q, k, v, s, lg: _ret_chunk(dd, q, k, v, s, lg, cos_ref[...], sin_ref[...])
                dqs, dks, dvs = [], [], []
                for h in range(H):
                    cols = slice(h * RET_DK, (h + 1) * RET_DK)
                    _, vjp = jax.vjp(fn, q_ref[:, cols], k_ref[:, cols], v_ref[:, cols], s_ref[h], lg_ref[h])
                    dq, dk, dv, ds, dlg = vjp((do_ref[:, cols], st[h]))
                    dqs.append(dq)
                    dks.append(dk)
                    dvs.append(dv)
                    st[h] = ds
                    dlg_ref[h] += dlg
                dq_ref[...] = jnp.concatenate(dqs, axis=1)
                dk_ref[...] = jnp.concatenate(dks, axis=1)
                dv_ref[...] = jnp.concatenate(dvs, axis=1)

    tile = lambda d, j: _dir_tile(cc, d, nC - 1 - j)
    zq = pl.BlockSpec((C, RV), lambda d, j: (tile(d, j), qb))
    zk = pl.BlockSpec((C, RV), lambda d, j: (tile(d, j), qb + 1))
    zv = pl.BlockSpec((C, RV), lambda d, j: (tile(d, j), qb + 2))
    s_spec = pl.BlockSpec((None, H, None, RET_DK, RET_DV), lambda d, j: (d, 0, tile(d, j), 0, 0))
    do_spec = pl.BlockSpec((C, RV), lambda d, j: (tile(d, j), 0))
    lgs = pl.BlockSpec((None, H, 1, 1), lambda d, j: (d, 0, 0, 0))
    cs = pl.BlockSpec((C, RET_DK // 2), lambda d, j: (tile(d, j), 0))
    g_spec = pl.BlockSpec((None, C, RV), lambda d, j: (d, tile(d, j), 0))
    gshape = _sds((2, T, RV), f32)
    return pl.pallas_call(
        body, name=name, grid=(2, nC), in_specs=[zq, zk, zv, s_spec, do_spec, lgs, cs, cs],
        out_specs=[g_spec, g_spec, g_spec, lgs], out_shape=[gshape, gshape, gshape, _sds((2, H, 1, 1), f32)],
        scratch_shapes=[pltpu.VMEM((H, RET_DK, RET_DV), f32)],
        compiler_params=_params(("arbitrary", "arbitrary")))(z, z, z, states, do, logit, cos, sin)


def _ret_norm_fn(H, o0, o1, ol, gn):
    o = o0 + o1
    parts = []
    for h in range(H):
        x = o[:, h * RET_DV:(h + 1) * RET_DV]
        mu = jnp.mean(x, axis=-1, keepdims=True)
        var = jnp.mean(jnp.square(x - mu), axis=-1, keepdims=True)
        parts.append((x - mu) * lax.rsqrt(var + EPS))
    return (jnp.concatenate(parts, axis=1) * gn) * jax.nn.silu(ol)


def ret_norm_fwd(cfg, o, z, gn, H, olcol, name):
    RV = H * RET_DV

    def fn(i, o_ref, ol_ref, gn_ref, r_ref):
        r_ref[...] = _ret_norm_fn(H, o_ref[0], o_ref[1], ol_ref[...], gn_ref[...]).astype(bf16)

    return rowcall(cfg, fn, name, [("drow", o, RV, 0), ("row", z, RV, olcol), ("full", gn)],
                   [("row", _sds((z.shape[0], RV), bf16), RV, 0)])[0]


def ret_norm_bwd(cfg, o, z, gn, dmix, H, olcol, dcol, name):
    RV = H * RET_DV
    T = z.shape[0]

    def fn(i, o_ref, ol_ref, gn_ref, d_ref, do_ref, dol_ref, dgn_ref):
        _, vjp = jax.vjp(functools.partial(_ret_norm_fn, H), o_ref[0], o_ref[1], ol_ref[...], gn_ref[...])
        do, _, dol, dgn = vjp(d_ref[...])
        do_ref[...] = do
        dol_ref[...] = dol
        _acc(dgn_ref, dgn, i == 0)

    return rowcall(cfg, fn, name, [("drow", o, RV, 0), ("row", z, RV, olcol), ("full", gn), ("row", dmix, RV, dcol)],
                   [("row", _sds((T, RV), f32), RV, 0), ("row", _sds((T, RV), f32), RV, 0), ("full", _sds((1, RV), f32))])


def _pool_geom(cfg, i, w, L):
    t = (i - cfg.cT) * cfg.TR + lax.broadcasted_iota(jnp.int32, (cfg.TR, 1), 0)
    lo = jnp.clip(t - w // 2, 0, L)
    hi = jnp.clip(t + w // 2, 0, L)
    return (hi - lo).astype(f32)


def _pool_centred(cfg, i, buf, gi, w, L):
    TR, G = cfg.TR, POOL_GROUP
    cols = pl.ds(gi * G, G)
    tot = buf[pl.ds(HALO - w // 2, TR), cols]
    for s in range(-w // 2 + 1, w // 2):
        tot = tot + buf[pl.ds(HALO + s, TR), cols]
    cnt = _pool_geom(cfg, i, w, L)
    return tot / cnt - buf[pl.ds(HALO, TR), cols], cnt


def pool_fwd(cfg, z, pw, ps, name):
    T = z.shape[0]
    TR, cT = cfg.TR, cfg.cT
    P = POOL_GROUP * len(POOL_WINDOWS)
    L = T - cT * TR

    def fn(i, x_ref, xp_ref, xn_ref, pw_ref, ps_ref, o_ref, buf):
        @pl.when(i < cT)
        def _():
            o_ref[...] = jnp.zeros(o_ref.shape, bf16)

        @pl.when(i >= cT)
        def _():
            start, end = _seg_flags(cfg, i)
            _fill_halo(buf, x_ref[...], xp_ref[...], xn_ref[...], start, end, TR)
            outs = []
            for gi, w in enumerate(POOL_WINDOWS):
                m, _ = _pool_centred(cfg, i, buf, gi, w, L)
                outs.append(jnp.dot(m.astype(bf16), pw_ref[gi].astype(bf16), preferred_element_type=f32))
            o_ref[...] = (jnp.concatenate(outs, axis=1) * ps_ref[...]).astype(bf16)

    return rowcall(cfg, fn, name, [("row", z, P, 0), ("prev", z, P, 0), ("next", z, P, 0), ("full", pw), ("full", ps)],
                   [("row", _sds((T, P), bf16), P, 0)], scratch=[pltpu.VMEM((TR + 2 * HALO, P), f32)])[0]


def pool_bwd_a(cfg, z, dmix, pw, ps, name):
    T = z.shape[0]
    TR, cT = cfg.TR, cfg.cT
    G = POOL_GROUP
    P = G * len(POOL_WINDOWS)
    L = T - cT * TR

    def fn(i, x_ref, xp_ref, xn_ref, d_ref, pw_ref, ps_ref, dm_ref, dmn_ref, dpw_ref, dps_ref, buf):
        @pl.when(i == 0)
        def _():
            dpw_ref[...] = jnp.zeros(dpw_ref.shape, f32)
            dps_ref[...] = jnp.zeros(dps_ref.shape, f32)

        @pl.when(i < cT)
        def _():
            dm_ref[...] = jnp.zeros(dm_ref.shape, f32)
            dmn_ref[...] = jnp.zeros(dmn_ref.shape, f32)

        @pl.when(i >= cT)
        def _():
            start, end = _seg_flags(cfg, i)
            _fill_halo(buf, x_ref[...], xp_ref[...], xn_ref[...], start, end, TR)
            dout = d_ref[...]
            dpre = dout * ps_ref[...]
            pres, dms, dmns = [], [], []
            for gi, w in enumerate(POOL_WINDOWS):
                m, cnt = _pool_centred(cfg, i, buf, gi, w, L)
                m_bf = m.astype(bf16)
                w_bf = pw_ref[gi].astype(bf16)
                pres.append(jnp.dot(m_bf, w_bf, preferred_element_type=f32))
                g_bf = dpre[:, gi * G:(gi + 1) * G].astype(bf16)
                dpw_ref[gi] += _dotf(m_bf, g_bf, 0, 0)
                dm = _dotf(g_bf, w_bf, 1, 1)
                dms.append(dm)
                dmns.append(dm / cnt)
            dps_ref[...] += jnp.sum(dout * jnp.concatenate(pres, axis=1), axis=0, keepdims=True)
            dm_ref[...] = jnp.concatenate(dms, axis=1)
            dmn_ref[...] = jnp.concatenate(dmns, axis=1)

    return rowcall(cfg, fn, name,
                   [("row", z, P, 0), ("prev", z, P, 0), ("next", z, P, 0), ("row", dmix, P, 0), ("full", pw), ("full", ps)],
                   [("row", _sds((T, P), f32), P, 0), ("row", _sds((T, P), f32), P, 0), ("full", _sds(pw.shape, f32)),
                    ("full", _sds((1, P), f32))], scratch=[pltpu.VMEM((TR + 2 * HALO, P), f32)])


def pool_bwd_b(cfg, dm, dmn, name):
    T, P = dm.shape
    TR, cT = cfg.TR, cfg.cT
    G = POOL_GROUP

    def fn(i, dm_ref, c_ref, p_ref, n_ref, dx_ref, buf):
        start, end = _seg_flags(cfg, i)
        _fill_halo(buf, c_ref[...], p_ref[...], n_ref[...], start, end, TR)
        outs = []
        for gi, w in enumerate(POOL_WINDOWS):
            cols = pl.ds(gi * G, G)
            tot = buf[pl.ds(HALO + w // 2, TR), cols]
            for s in range(-w // 2 + 1, w // 2):
                tot = tot + buf[pl.ds(HALO + s, TR), cols]
            outs.append(tot)
        dx_ref[...] = jnp.concatenate(outs, axis=1) - dm_ref[...]

    return rowcall(cfg, fn, name, [("row", dm, P, 0), ("row", dmn, P, 0), ("prev", dmn, P, 0), ("next", dmn, P, 0)],
                   [("row", _sds((T, P), f32), P, 0)], scratch=[pltpu.VMEM((TR + 2 * HALO, P), f32)])[0]


def _swap_halves(x):
    return pltpu.roll(x, HEAD_DIM // 2, 1)


def _headnorm(x, g):
    return _rms(x) * g


def att_prep(cfg, z, qg, kg, cosf, sinf, nq, name):
    T = z.shape[0]
    U = z.shape[1] // (nq + 3)
    nh = U // HEAD_DIM

    def fn(i, *refs):
        q_refs = refs[:nq]
        k_ref, v_ref, qg_ref, kg_ref, cos_ref, sin_ref, qn_ref, kn_ref, vb_ref = refs[nq:]
        cosv, sinv = cos_ref[...], sin_ref[...]

        def heads(x, g):
            outs = []
            for h in range(nh):
                y = _headnorm(x[:, h * HEAD_DIM:(h + 1) * HEAD_DIM], g)
                outs.append(y * cosv + _swap_halves(y) * sinv)
            return jnp.concatenate(outs, axis=1)

        qn_ref[...] = jnp.concatenate([heads(r[...], qg_ref[...]) for r in q_refs], axis=1).astype(bf16)
        kn_ref[...] = heads(k_ref[...], kg_ref[...]).astype(bf16)
        vb_ref[...] = v_ref[...].astype(bf16)

    ins = [("row", z, U, 1 + n) for n in range(nq)] + [("row", z, U, nq + 1), ("row", z, U, nq + 2), ("full", qg),
                                                       ("full", kg), ("row", cosf, HEAD_DIM, 0), ("row", sinf, HEAD_DIM, 0)]
    return rowcall(cfg, fn, name, ins, [("row", _sds((T, nq * U), bf16), nq * U, 0), ("row", _sds((T, U), bf16), U, 0),
                                        ("row", _sds((T, U), bf16), U, 0)])


def att_prep_bwd(cfg, z, qg, kg, cosf, sinf, dqn, dkn, dvb, dxpool, nq, name):
    T = z.shape[0]
    U = z.shape[1] // (nq + 3)
    nh = U // HEAD_DIM
    cT = cfg.cT

    def fn(i, *refs):
        q_refs = refs[:nq]
        (k_ref, qg_ref, kg_ref, cos_ref, sin_ref, dqn_ref, dkn_ref, dvb_ref, dxp_ref, dz_ref, dqg_ref, dkg_ref) = refs[nq:]
        cosv, sinv = cos_ref[...], sin_ref[...]

        @pl.when(i == 0)
        def _():
            dqg_ref[...] = jnp.zeros(dqg_ref.shape, f32)
            dkg_ref[...] = jnp.zeros(dkg_ref.shape, f32)

        def heads_bwd(x, g, dy, dg_ref):
            outs = []
            for h in range(nh):
                sl = slice(h * HEAD_DIM, (h + 1) * HEAD_DIM)
                d = dy[:, sl]
                dn = d * cosv + _swap_halves(d * sinv)
                _, vjp = jax.vjp(_headnorm, x[:, sl], g)
                dx, dg = vjp(dn)
                dg_ref[...] += dg
                outs.append(dx)
            return jnp.concatenate(outs, axis=1)

        dk = heads_bwd(k_ref[...], kg_ref[...], dkn_ref[...], dkg_ref)
        tail = [dk.astype(bf16), dvb_ref[...].astype(bf16)]

        @pl.when(i < cT)
        def _():
            zeros = jnp.zeros((cfg.TR, (nq + 1) * U), bf16)
            dz_ref[...] = jnp.concatenate([zeros] + tail, axis=1)

        @pl.when(i >= cT)
        def _():
            dq = [heads_bwd(r[...], qg_ref[...], dqn_ref[:, n * U:(n + 1) * U], dqg_ref) for n, r in enumerate(q_refs)]
            dz_ref[...] = jnp.concatenate([dxp_ref[...].astype(bf16)] + [t.astype(bf16) for t in dq] + tail, axis=1)

    ins = ([("row", z, U, 1 + n) for n in range(nq)] +
           [("row", z, U, nq + 1), ("full", qg), ("full", kg), ("row", cosf, HEAD_DIM, 0), ("row", sinf, HEAD_DIM, 0),
            ("row", dqn, nq * U, 0), ("row", dkn, U, 0), ("row", dvb, U, 0), ("row", dxpool, U, 0)])
    W = (nq + 3) * U
    return rowcall(cfg, fn, name, ins, [("row", _sds((T, W), bf16), W, 0), ("full", _sds((1, HEAD_DIM), f32)),
                                        ("full", _sds((1, HEAD_DIM), f32))])


def _stack_heads(x, n):
    return jnp.concatenate([x[:, h * HEAD_DIM:(h + 1) * HEAD_DIM] for h in range(n)], axis=0)


def _unstack_heads(x, n):
    rows = x.shape[0] // n
    return jnp.concatenate([x[h * rows:(h + 1) * rows] for h in range(n)], axis=1)


def _att_tiles(cfg, T):
    tq = cfg.TR
    tk = _pick(T, (4224, 2816, 1408, 768, 512, 256, 128))
    return tq, tk, (T - cfg.cT * cfg.TR) // tq, T // tk


LOG2E = 1.4426950408889634


def att_fwd(cfg, qn, kn, vb, nq, name):
    T, U = kn.shape
    KV = U // HEAD_DIM
    tq, tk, nQ, nK = _att_tiles(cfg, T)
    scale = HEAD_DIM ** -0.5
    c2 = scale * LOG2E
    R = nq * tq

    def body(q_ref, k_ref, v_ref, o_ref, lse_ref, *scratch):
        ik = pl.program_id(2)
        m_sc, l_sc, acc = scratch[:nq], scratch[nq:2 * nq], scratch[2 * nq:]

        @pl.when(ik == 0)
        def _():
            for h in range(nq):
                m_sc[h][...] = jnp.full(m_sc[h].shape, -jnp.inf, f32)
                l_sc[h][...] = jnp.zeros(l_sc[h].shape, f32)
                acc[h][...] = jnp.zeros(acc[h].shape, f32)

        k, v = k_ref[...], v_ref[...]
        for h in range(nq):
            s = _dotf(q_ref[:, h * HEAD_DIM:(h + 1) * HEAD_DIM], k, 1, 1)
            m_old = m_sc[h][...]
            m_new = jnp.maximum(m_old, jnp.max(s, axis=-1, keepdims=True))
            alpha = jnp.exp2((m_old - m_new) * c2)
            p = jnp.exp2((s - m_new) * c2)
            l_sc[h][...] = alpha * l_sc[h][...] + jnp.sum(p, axis=-1, keepdims=True)
            acc[h][...] = alpha * acc[h][...] + jnp.dot(p.astype(bf16), v, preferred_element_type=f32)
            m_sc[h][...] = m_new

        @pl.when(ik == nK - 1)
        def _():
            o_ref[...] = jnp.concatenate([acc[h][...] / l_sc[h][...] for h in range(nq)], axis=1)
            lse_ref[...] = jnp.concatenate([m_sc[h][...] * scale + jnp.log(l_sc[h][...]) for h in range(nq)], axis=0)

    W = nq * HEAD_DIM
    q_spec = pl.BlockSpec((tq, W), lambda h, i, k: (i + cfg.cT, h))
    kv_spec = pl.BlockSpec((tk, HEAD_DIM), lambda h, i, k: (k, h))
    lse_spec = pl.BlockSpec((None, None, R, 1), lambda h, i, k: (h, i, 0, 0))
    col = [pltpu.VMEM((tq, 1), f32)] * nq
    return pl.pallas_call(
        body, name=name, grid=(KV, nQ, nK), in_specs=[q_spec, kv_spec, kv_spec], out_specs=[q_spec, lse_spec],
        out_shape=[_sds((T, nq * U), f32), _sds((KV, nQ, R, 1), f32)],
        scratch_shapes=col + col + [pltpu.VMEM((tq, HEAD_DIM), f32)] * nq,
        compiler_params=_params(("arbitrary", "arbitrary", "arbitrary")))(qn, kn, vb)


def att_bwd_dq(cfg, qn, kn, vb, o, lse, do, nq, name):
    T, U = kn.shape
    KV = U // HEAD_DIM
    tq, tk, nQ, nK = _att_tiles(cfg, T)
    scale = HEAD_DIM ** -0.5
    c2 = scale * LOG2E
    R = nq * tq
    W = nq * HEAD_DIM

    def body(q_ref, k_ref, v_ref, o_ref, lse_ref, do_ref, dq_ref, acc, dl):
        ik = pl.program_id(2)

        @pl.when(ik == 0)
        def _():
            acc[...] = jnp.zeros(acc.shape, f32)
            dl[...] = jnp.sum(_stack_heads(do_ref[...] * o_ref[...], nq), axis=-1, keepdims=True)

        k, v = k_ref[...], v_ref[...]
        for h in range(nq):
            rows = pl.ds(h * tq, tq)
            cols = slice(h * HEAD_DIM, (h + 1) * HEAD_DIM)
            s = _dotf(q_ref[:, cols], k, 1, 1)
            p = jnp.exp2(s * c2 - lse_ref[rows, :] * LOG2E)
            dp = _dotf(do_ref[:, cols].astype(bf16), v, 1, 1)
            ds = (p * (dp - dl[rows, :]) * scale).astype(bf16)
            acc[rows, :] += jnp.dot(ds, k, preferred_element_type=f32)

        @pl.when(ik == nK - 1)
        def _():
            dq_ref[...] = _unstack_heads(acc[...], nq)

    q_spec = pl.BlockSpec((tq, W), lambda h, i, k: (i + cfg.cT, h))
    kv_spec = pl.BlockSpec((tk, HEAD_DIM), lambda h, i, k: (k, h))
    lse_spec = pl.BlockSpec((None, None, R, 1), lambda h, i, k: (h, i, 0, 0))
    return pl.pallas_call(
        body, name=name, grid=(KV, nQ, nK), in_specs=[q_spec, kv_spec, kv_spec, q_spec, lse_spec, q_spec], out_specs=q_spec,
        out_shape=_sds((T, nq * U), f32), scratch_shapes=[pltpu.VMEM((R, HEAD_DIM), f32), pltpu.VMEM((R, 1), f32)],
        compiler_params=_params(("arbitrary", "arbitrary", "arbitrary")))(qn, kn, vb, o, lse, do)


def att_bwd_dkv(cfg, qn, kn, vb, o, lse, do, nq, name):
    T, U = kn.shape
    KV = U // HEAD_DIM
    tq, tk, nQ, nK = _att_tiles(cfg, T)
    scale = HEAD_DIM ** -0.5
    c2 = scale * LOG2E
    R = nq * tq
    W = nq * HEAD_DIM

    def body(q_ref, k_ref, v_ref, o_ref, lse_ref, do_ref, dk_ref, dv_ref, dk_acc, dv_acc):
        iq = pl.program_id(2)

        @pl.when(iq == 0)
        def _():
            dk_acc[...] = jnp.zeros(dk_acc.shape, f32)
            dv_acc[...] = jnp.zeros(dv_acc.shape, f32)

        k, v = k_ref[...], v_ref[...]
        for h in range(nq):
            rows = pl.ds(h * tq, tq)
            cols = slice(h * HEAD_DIM, (h + 1) * HEAD_DIM)
            qh = q_ref[:, cols]
            doh = do_ref[:, cols]
            dl = jnp.sum(doh * o_ref[:, cols], axis=-1, keepdims=True)
            p = jnp.exp2(_dotf(qh, k, 1, 1) * c2 - lse_ref[rows, :] * LOG2E)
            do_bf = doh.astype(bf16)
            dv_acc[...] += _dotf(p.astype(bf16), do_bf, 0, 0)
            dp = _dotf(do_bf, v, 1, 1)
            ds = (p * (dp - dl) * scale).astype(bf16)
            dk_acc[...] += _dotf(ds, qh, 0, 0)

        @pl.when(iq == nQ - 1)
        def _():
            dk_ref[...] = dk_acc[...]
            dv_ref[...] = dv_acc[...]

    q_spec = pl.BlockSpec((tq, W), lambda h, k, i: (i + cfg.cT, h))
    kv_spec = pl.BlockSpec((tk, HEAD_DIM), lambda h, k, i: (k, h))
    lse_spec = pl.BlockSpec((None, None, R, 1), lambda h, k, i: (h, i, 0, 0))
    return pl.pallas_call(
        body, name=name, grid=(KV, nK, nQ), in_specs=[q_spec, kv_spec, kv_spec, q_spec, lse_spec, q_spec],
        out_specs=[kv_spec, kv_spec], out_shape=[_sds((T, U), f32), _sds((T, U), f32)],
        scratch_shapes=[pltpu.VMEM((tk, HEAD_DIM), f32), pltpu.VMEM((tk, HEAD_DIM), f32)],
        compiler_params=_params(("arbitrary", "arbitrary", "arbitrary")))(qn, kn, vb, o, lse, do)


def att_bwd(cfg, qn, kn, vb, o, lse, do, nq, name, tk_prefs=(4224, 2816, 1408, 768, 512, 256, 128)):
    T, U = kn.shape
    KV = U // HEAD_DIM
    tq = cfg.TR
    tk = _pick(T, tk_prefs)
    nQ, nK = (T - cfg.cT * cfg.TR) // tq, T // tk
    scale = HEAD_DIM ** -0.5
    c2 = scale * LOG2E
    R = nq * tq
    W = nq * HEAD_DIM

    def body(q_ref, k_ref, v_ref, o_ref, lse_ref, do_ref, dq_ref, dk_ref, dv_ref, acc, dl):
        iq, ik = pl.program_id(1), pl.program_id(2)

        @pl.when(jnp.logical_and(iq == 0, ik == 0))
        def _():
            dk_ref[...] = jnp.zeros(dk_ref.shape, f32)
            dv_ref[...] = jnp.zeros(dv_ref.shape, f32)

        @pl.when(ik == 0)
        def _():
            acc[...] = jnp.zeros(acc.shape, f32)
            dl[...] = jnp.sum(_stack_heads(do_ref[...] * o_ref[...], nq), axis=-1, keepdims=True)

        k, v = k_ref[...], v_ref[...]
        krows = pl.ds(pl.multiple_of(ik * tk, tk), tk)
        for h in range(nq):
            rows = pl.ds(h * tq, tq)
            cols = slice(h * HEAD_DIM, (h + 1) * HEAD_DIM)
            qh = q_ref[:, cols]
            p = jnp.exp2(_dotf(qh, k, 1, 1) * c2 - lse_ref[rows, :] * LOG2E)
            do_bf = do_ref[:, cols].astype(bf16)
            dp = _dotf(do_bf, v, 1, 1)
            ds = (p * (dp - dl[rows, :]) * scale).astype(bf16)
            acc[rows, :] += jnp.dot(ds, k, preferred_element_type=f32)
            dv_ref[krows, :] += _dotf(p.astype(bf16), do_bf, 0, 0)
            dk_ref[krows, :] += _dotf(ds, qh, 0, 0)

        @pl.when(ik == nK - 1)
        def _():
            dq_ref[...] = _unstack_heads(acc[...], nq)

    q_spec = pl.BlockSpec((tq, W), lambda h, i, k: (i + cfg.cT, h))
    kv_spec = pl.BlockSpec((tk, HEAD_DIM), lambda h, i, k: (k, h))
    lse_spec = pl.BlockSpec((None, None, R, 1), lambda h, i, k: (h, i, 0, 0))
    head_spec = pl.BlockSpec((T, HEAD_DIM), lambda h, i, k: (0, h))
    return pl.pallas_call(
        body, name=name, grid=(KV, nQ, nK), in_specs=[q_spec, kv_spec, kv_spec, q_spec, lse_spec, q_spec],
        out_specs=[q_spec, head_spec, head_spec], out_shape=[_sds((T, nq * U), f32), _sds((T, U), f32), _sds((T, U), f32)],
        scratch_shapes=[pltpu.VMEM((R, HEAD_DIM), f32), pltpu.VMEM((R, 1), f32)],
        compiler_params=_params(("arbitrary", "arbitrary", "arbitrary")))(qn, kn, vb, o, lse, do)


def od_mix(cfg, pooled, o, name):
    T, P = pooled.shape
    QW = o.shape[1]
    cT = cfg.cT

    def fn(i, p_ref, o_ref, m_ref):
        @pl.when(i < cT)
        def _():
            m_ref[...] = jnp.zeros(m_ref.shape, bf16)

        @pl.when(i >= cT)
        def _():
            m_ref[...] = jnp.concatenate([p_ref[...], o_ref[...].astype(bf16)], axis=1)

    return rowcall(cfg, fn, name, [("row", pooled, P, 0), ("row", o, QW, 0)], [("row", _sds((T, P + QW), bf16), P + QW, 0)])[0]


def ev_mix(cfg, lru, ret, name):
    T, W = lru.shape
    RV = ret.shape[1]

    def fn(i, a_ref, b_ref, m_ref):
        m_ref[...] = jnp.concatenate([a_ref[...], b_ref[...]], axis=1)

    return rowcall(cfg, fn, name, [("row", lru, W, 0), ("row", ret, RV, 0)], [("row", _sds((T, W + RV), bf16), W + RV, 0)])[0]


def ev_dz_pack(cfg, dgl, dr, dq, dk, dv, dol, name):
    T, W = dgl.shape
    RV = dol.shape[1]
    width = 2 * W + 4 * RV

    def fn(i, g_ref, r_ref, q_ref, k_ref, v_ref, o_ref, dz_ref):
        parts = [g_ref[...], r_ref[...], q_ref[0] + q_ref[1], k_ref[0] + k_ref[1], v_ref[0] + v_ref[1], o_ref[...]]
        dz_ref[...] = jnp.concatenate([p.astype(bf16) for p in parts], axis=1)

    return rowcall(cfg, fn, name, [("row", dgl, W, 0), ("row", dr, W, 0), ("drow", dq, RV, 0), ("drow", dk, RV, 0),
                                   ("drow", dv, RV, 0), ("row", dol, RV, 0)], [("row", _sds((T, width), bf16), width, 0)])[0]


def loss_fwd_bwd(cfg, xf, target, name):
    T, D = xf.shape
    TR, cT = cfg.TR, cfg.cT

    def body(x_ref, t_ref, sq_ref, dx_ref):
        i = pl.program_id(0)

        @pl.when(i == 0)
        def _():
            sq_ref[...] = jnp.zeros(sq_ref.shape, f32)

        @pl.when(i < cT)
        def _():
            dx_ref[...] = jnp.zeros(dx_ref.shape, f32)

        @pl.when(i >= cT)
        def _():
            diff = x_ref[...] - t_ref[...]
            sq_ref[...] += jnp.sum(diff * diff, axis=0, keepdims=True)
            dx_ref[...] = diff / D

    row = pl.BlockSpec((TR, D), lambda i: (i, 0))
    trow = pl.BlockSpec((TR, D), lambda i: (jnp.maximum(i - cT, 0), 0))
    return pl.pallas_call(body, name=name, grid=(cfg.nT,), in_specs=[row, trow],
                          out_specs=[pl.BlockSpec((1, D), lambda i: (0, 0)), row],
                          out_shape=[_sds((1, D), f32), _sds((T, D), f32)], compiler_params=_params(("arbitrary",)))(xf, target)


MOD_ROWS = 16


def mod_fwd(s16, mod_w, name):
    nL, D, C4 = mod_w.shape
    tc = _pick(C4, (512, 256, 128))

    def body(s_ref, w_ref, o_ref):
        o_ref[...] = jnp.dot(s_ref[...], w_ref[...], precision=lax.Precision.HIGHEST, preferred_element_type=f32)

    return pl.pallas_call(
        body, name=name, grid=(nL, C4 // tc),
        in_specs=[pl.BlockSpec((MOD_ROWS, D), lambda l, j: (0, 0)), pl.BlockSpec((None, D, tc), lambda l, j: (l, 0, j))],
        out_specs=pl.BlockSpec((None, MOD_ROWS, tc), lambda l, j: (l, 0, j)), out_shape=_sds((nL, MOD_ROWS, C4), f32),
        compiler_params=_params(("arbitrary", "arbitrary")))(s16, mod_w)


def mod_bwd(s16, dm16, mod_w, name):
    nL, D, C4 = mod_w.shape
    tc = _pick(C4, (512, 256, 128))
    half = MOD_ROWS // 2

    def body(s_ref, d_ref, w_ref, g_ref, dc_ref):
        first = jnp.logical_and(pl.program_id(0) == 0, pl.program_id(1) == 0)
        g_ref[...] = lax.dot_general(s_ref[...], d_ref[...], (((0,), (0,)), ((), ())), precision=lax.Precision.HIGHEST,
                                     preferred_element_type=f32)
        part = lax.dot_general(d_ref[...], w_ref[...], (((1,), (1,)), ((), ())), precision=lax.Precision.HIGHEST,
                               preferred_element_type=f32)
        _acc(dc_ref, jnp.sum(part[half:], axis=0, keepdims=True), first)

    return pl.pallas_call(
        body, name=name, grid=(nL, C4 // tc),
        in_specs=[pl.BlockSpec((MOD_ROWS, D), lambda l, j: (0, 0)), pl.BlockSpec((None, MOD_ROWS, tc), lambda l, j: (l, 0, j)),
                  pl.BlockSpec((None, D, tc), lambda l, j: (l, 0, j))],
        out_specs=[pl.BlockSpec((None, D, tc), lambda l, j: (l, 0, j)), pl.BlockSpec((1, D), lambda l, j: (0, 0))],
        out_shape=[_sds((nL, D, C4), f32), _sds((1, D), f32)],
        compiler_params=_params(("arbitrary", "arbitrary")))(s16, dm16, mod_w)


def _as2d(a):
    return a.reshape(-1, a.shape[-1])


ELEMENTWISE_VMEM = 24 * 1024 * 1024


def _tiles2d(shape, n_arrays):
    R, C = shape
    tc = _pick(C, (1536, 1408, 1024, 768, 512, 256, 128))
    fits = [t for t in (512, 256, 128, 64, 32, 16, 8) if R % t == 0 and t * tc * 4 * 2 * n_arrays <= ELEMENTWISE_VMEM]
    return (fits[0] if fits else R), tc


def cast_bf16(a, name):
    a2 = _as2d(a)
    tr, tc = _tiles2d(a2.shape, 2)

    def body(a_ref, o_ref):
        o_ref[...] = a_ref[...].astype(bf16)

    spec = pl.BlockSpec((tr, tc), lambda i, j: (i, j))
    out = pl.pallas_call(body, name=name, grid=(a2.shape[0] // tr, a2.shape[1] // tc), in_specs=[spec], out_specs=spec,
                         out_shape=_sds(a2.shape, bf16), compiler_params=_params(("arbitrary", "arbitrary")))(a2)
    return out.reshape(a.shape)


def sum_leading(a, name, *, into=None, full_shape=None, widx=()):
    n = a.shape[0]
    a3 = a.reshape(n, -1, a.shape[-1])
    tr, tc = _tiles2d(a3.shape[1:], n + 1)

    def body(a_ref, *rest):
        o_ref = rest[-1]
        tot = a_ref[0].astype(f32)
        for k in range(1, n):
            tot = tot + a_ref[k].astype(f32)
        o_ref[...] = tot

    grid = (a3.shape[1] // tr, a3.shape[2] // tc)
    in_specs = [pl.BlockSpec((n, tr, tc), lambda i, j: (0, i, j))]
    args = [a3]
    if not widx:
        out = pl.pallas_call(body, name=name, grid=grid, in_specs=in_specs,
                             out_specs=pl.BlockSpec((tr, tc), lambda i, j: (i, j)), out_shape=_sds(a3.shape[1:], f32),
                             compiler_params=_params(("arbitrary", "arbitrary")))(*args)
        return out.reshape(a.shape[1:])
    lead = tuple(full_shape[:len(widx)])
    flat = lead + tuple(a3.shape[1:])
    aliases = {}
    if into is not None:
        in_specs.append(pl.BlockSpec(memory_space=pl.ANY))
        args.append(into.reshape(flat))
        aliases = {1: 0}
    out = pl.pallas_call(body, name=name, grid=grid, in_specs=in_specs,
                         out_specs=pl.BlockSpec((None,) * len(widx) + (tr, tc), lambda i, j: tuple(widx) + (i, j)),
                         out_shape=_sds(flat, f32), input_output_aliases=aliases,
                         compiler_params=_params(("arbitrary", "arbitrary")))(*args)
    return out.reshape(full_shape)


def adamw(w, m, v, g_parts, name):
    w2, m2, v2 = _as2d(w), _as2d(m), _as2d(v)
    parts = [_as2d(p) for p in g_parts]
    tr, tc = _tiles2d(w2.shape, 7 + len(parts))
    npart = len(parts)

    def body(*refs):
        w_ref, m_ref, v_ref = refs[:3]
        p_refs = refs[3:3 + npart]
        g_ref, d_ref, nm_ref, nv_ref = refs[3 + npart:]
        g = p_refs[0][...]
        for p in p_refs[1:]:
            g = g + p[...]
        mn = ADAM_B1 * m_ref[...] + (1.0 - ADAM_B1) * g
        vn = ADAM_B2 * v_ref[...] + (1.0 - ADAM_B2) * jnp.square(g)
        m_hat = mn / (1.0 - ADAM_B1 ** ADAM_STEP)
        v_hat = vn / (1.0 - ADAM_B2 ** ADAM_STEP)
        g_ref[...] = g
        d_ref[...] = -ADAM_LR * (m_hat / (jnp.sqrt(v_hat) + ADAM_EPS) + ADAM_WD * w_ref[...])
        nm_ref[...] = mn
        nv_ref[...] = vn

    spec = pl.BlockSpec((tr, tc), lambda i, j: (i, j))
    outs = pl.pallas_call(body, name=name, grid=(w2.shape[0] // tr, w2.shape[1] // tc), in_specs=[spec] * (3 + npart),
                          out_specs=[spec] * 4, out_shape=[_sds(w2.shape, f32)] * 4,
                          compiler_params=_params(("arbitrary", "arbitrary")))(w2, m2, v2, *parts)
    return [o.reshape(w.shape) for o in outs]


def all_gather_small(a, name):
    R, C = a.shape

    def body(a_ref, out_ref, send_sems, recv_sems, local_sem):
        x, y, c = _coords()
        me = 4 * x + 2 * y + c
        mine = pltpu.make_async_copy(a_ref, out_ref.at[me], local_sem)
        mine.start()
        copies = []
        for k in range(1, N_DEV):
            kx, ky, kc = (k >> 2) & 1, (k >> 1) & 1, k & 1
            peer = (_flip(x, kx), _flip(y, ky), _flip(c, kc))
            cp = pltpu.make_async_remote_copy(src_ref=a_ref, dst_ref=out_ref.at[me], send_sem=send_sems.at[k - 1],
                                              recv_sem=recv_sems.at[k - 1], device_id=peer, device_id_type=MESH)
            cp.start()
            copies.append((cp, 4 * peer[0] + 2 * peer[1] + peer[2], peer))
        for k, (cp, pidx, peer) in enumerate(copies):
            pltpu.make_async_remote_copy(src_ref=a_ref, dst_ref=out_ref.at[pidx], send_sem=send_sems.at[k],
                                         recv_sem=recv_sems.at[k], device_id=peer, device_id_type=MESH).wait_recv()
        for cp, _, _ in copies:
            cp.wait_send()
        mine.wait()

    return pl.pallas_call(
        body, name=name, out_shape=_sds((N_DEV, R, C), f32),
        in_specs=[pl.BlockSpec(memory_space=pltpu.VMEM)], out_specs=pl.BlockSpec(memory_space=pltpu.VMEM),
        scratch_shapes=[pltpu.SemaphoreType.DMA((N_DEV - 1,)), pltpu.SemaphoreType.DMA((N_DEV - 1,)), pltpu.SemaphoreType.DMA],
        compiler_params=pltpu.CompilerParams(vmem_limit_bytes=VMEM_LIMIT))(a)


def swap_with_sibling(parts, name):
    n = len(parts)

    def body(*refs):
        in_refs, out_refs = refs[:n], refs[n:2 * n]
        send_sems, recv_sems = refs[2 * n:]
        x, y, c = _coords()
        sends = []
        for w in range(n):
            cp = pltpu.make_async_remote_copy(src_ref=in_refs[w], dst_ref=out_refs[w], send_sem=send_sems.at[w],
                                              recv_sem=recv_sems.at[w], device_id=(x, y, 1 - c), device_id_type=MESH)
            cp.start()
            sends.append(cp)
        for cp in sends:
            cp.wait_recv()
        for cp in sends:
            cp.wait_send()

    hbm = pl.BlockSpec(memory_space=pl.ANY)
    return pl.pallas_call(
        body, name=name, out_shape=[_sds(a.shape, a.dtype) for a in parts], in_specs=[hbm] * n, out_specs=[hbm] * n,
        scratch_shapes=[pltpu.SemaphoreType.DMA((n,)), pltpu.SemaphoreType.DMA((n,))],
        )(*parts)


def even_fwd(cfg, x, p, tag, host=None):
    W, H = p["W"], p["H"]
    got = {}
    h = pre_fwd(cfg, x, p["g_pre"], p["shift"], p["scale"], tag + "_pre")
    z = _hosted(host, "in", got, lambda cm: matmul("v1", h, p["w_in"], comm=cm, name=tag + "_in"))
    u = conv_fwd(cfg, z, p["conv_w"], p["conv_b"], W, tag + "_conv")
    a, b = lru_coef_fwd(cfg, u, p["wa"], p["ba"], p["wx"], p["bx"], p["lam"], tag + "_coef")
    hh, hp = lru_scan(cfg, a, b, tag + "_scan")
    lru = lru_out_fwd(cfg, z, hh, W, tag + "_lruout")
    qcol = 2 * W // RET_DK
    o, st = ret_fwd(cfg, z, p["logit"], p["cos1"], p["sin1"], H, qcol, tag + "_ret")
    olcol = (2 * W + 3 * H * RET_DK) // (H * RET_DV)
    ret = ret_norm_fwd(cfg, o, z, p["gn"], H, olcol, tag + "_retnorm")
    mix = ev_mix(cfg, lru, ret, tag + "_mix")
    y = _hosted(host, "out", got, lambda cm: matmul("v2", mix, p["w_out"], comm=cm, name=tag + "_out"))
    xo = post_fwd(cfg, x, y, p["g_post"], p["gate"], 1.0, tag + "_post")
    return xo, (x, h, z, u, a, hh, hp, o, st, mix, y, olcol, qcol), got


def even_bwd(cfg, dX, saved, p, tag):
    x, h, z, u, a, hh, hp, o, st, mix, y, olcol, qcol = saved
    W, H = p["W"], p["H"]
    dy, dg_post, dgate = post_bwd(cfg, dX, y, p["g_post"], p["gate"], 1.0, tag + "_postb")
    dmix = matmul("v4", dy, p["w_out"], name=tag + "_dmix")
    g_out = matmul("v6", mix, dy, gshape=p["w_out"].shape, out_dtype=bf16, name=tag + "_gwout")
    dgl, dhs = lru_out_bwd(cfg, z, hh, dmix, W, tag + "_lruoutb")
    da, db = lru_scan_bwd(cfg, a, hp, dhs, tag + "_scanb")
    du, dwa, dba, dwx, dbx, dlam = lru_coef_bwd(cfg, u, da, db, p["wa"], p["ba"], p["wx"], p["bx"], p["lam"], tag + "_coefb")
    dr, dcw, dcb = conv_bwd(cfg, z, du, p["conv_w"], W, tag + "_convb")
    do, dol, dgn = ret_norm_bwd(cfg, o, z, p["gn"], dmix, H, olcol, W // (H * RET_DV), tag + "_retnormb")
    dq, dk, dv, dlg = ret_bwd(cfg, z, st, do, p["logit"], p["cos1"], p["sin1"], H, qcol, tag + "_retb")
    dz = ev_dz_pack(cfg, dgl, dr, dq, dk, dv, dol, tag + "_dz")
    g_in, r_out = matmul("v5", h, dz, gshape=p["w_in"].shape, out_dtype=bf16, comm=Comm("scatter", [(g_out, ())]),
                         name=tag + "_gwin")
    dh, r_in = matmul("v3", dz, p["w_in"], comm=Comm("scatter", [(g_in, ())]), name=tag + "_dh")
    dX, dg_pre, dshift, dscale = pre_bwd(cfg, x, p["g_pre"], p["shift"], p["scale"], dh, dX, tag + "_preb")
    pg = dict(g_pre=dg_pre, g_post=dg_post, shift=dshift, scale=dscale, gate=dgate, conv_w=dcw, conv_b=dcb, wa=dwa,
              ba=dba, wx=dwx, bx=dbx, lam=dlam, logit=dlg, gn=dgn)
    return dX, pg, dict(w_in=r_in[0], w_out=r_out[0])


def odd_fwd(cfg, x, p, tag, host=None):
    nq = p["nq"]
    got = {}
    h = pre_fwd(cfg, x, p["g_pre"], p["shift"], p["scale"], tag + "_pre")
    z = _hosted(host, "in", got, lambda cm: matmul("v1", h, p["w_in"], comm=cm, name=tag + "_in"))
    pooled = pool_fwd(cfg, z, p["pool_w"], p["pool_scale"], tag + "_pool")
    qn, kn, vb = att_prep(cfg, z, p["qg"], p["kg"], p["cosf"], p["sinf"], nq, tag + "_prep")
    o, lse = att_fwd(cfg, qn, kn, vb, nq, tag + "_att")
    mix = od_mix(cfg, pooled, o, tag + "_mix")
    y = matmul("v2", mix, p["w_out"], name=tag + "_out")
    xo = post_fwd(cfg, x, y, p["g_post"], p["gate"], 1.0, tag + "_post")
    return xo, (x, h, z, qn, kn, vb, o, lse, mix, y), got


def odd_bwd(cfg, dX, saved, p, tag):
    x, h, z, qn, kn, vb, o, lse, mix, y = saved
    nq = p["nq"]
    U = kn.shape[1]
    dy, dg_post, dgate = post_bwd(cfg, dX, y, p["g_post"], p["gate"], 1.0, tag + "_postb")
    dmix = matmul("v4", dy, p["w_out"], name=tag + "_dmix")
    g_out = matmul("v6", mix, dy, gshape=p["w_out"].shape, out_dtype=bf16, name=tag + "_gwout")
    dm, dmn, dpw, dps = pool_bwd_a(cfg, z, dmix, p["pool_w"], p["pool_scale"], tag + "_poolb")
    dxp = pool_bwd_b(cfg, dm, dmn, tag + "_poolb2")
    do = dmix[:, U:]
    dqn, dkn, dvb = att_bwd(cfg, qn, kn, vb, o, lse, do, nq, tag + "_attb")
    dz, dqg, dkg = att_prep_bwd(cfg, z, p["qg"], p["kg"], p["cosf"], p["sinf"], dqn, dkn, dvb, dxp, nq, tag + "_prepb")
    g_in, r_out = matmul("v5", h, dz, gshape=p["w_in"].shape, out_dtype=bf16, comm=Comm("scatter", [(g_out, ())]),
                         name=tag + "_gwin")
    dh, r_in = matmul("v3", dz, p["w_in"], comm=Comm("scatter", [(g_in, ())]), name=tag + "_dh")
    dX, dg_pre, dshift, dscale = pre_bwd(cfg, x, p["g_pre"], p["shift"], p["scale"], dh, dX, tag + "_preb")
    pg = dict(g_pre=dg_pre, g_post=dg_post, shift=dshift, scale=dscale, gate=dgate, pool_w=dpw, pool_scale=dps, qg=dqg, kg=dkg)
    return dX, pg, dict(w_in=r_in[0], w_out=r_out[0])


WEIGHT_NAMES = ("c_ctx", "mod_w", "mod_b", "norm_pre", "norm_post", "ffn_gate", "ffn_up", "ffn_down", "ev_w_in", "ev_w_out",
                "lru_conv_w", "lru_conv_b", "lru_wa", "lru_ba", "lru_wx", "lru_bx", "lru_lambda", "ret_decay_logit", "ret_gn",
                "od_w_in", "od_w_out", "pool_w", "pool_scale", "q_norm", "k_norm")
BIG = ("ffn_gate", "ffn_up", "ffn_down", "ev_w_in", "ev_w_out", "od_w_in", "od_w_out")
SMALL_SHARDED = ("norm_pre", "norm_post", "lru_conv_w", "lru_ba", "lru_bx", "lru_lambda", "pool_scale")
SMALL_REPL = ("mod_b", "lru_conv_b", "lru_wa", "lru_wx", "ret_decay_logit", "ret_gn", "pool_w", "q_norm", "k_norm")
LANES = 128


PACK_ROWS = 512


def _rows_of(n):
    return -(-n // (8 * LANES)) * 8


def _pack(arrs):
    rows = []
    for a in arrs:
        flat = a.reshape(-1)
        rows.append(jnp.pad(flat, (0, _rows_of(flat.shape[0]) * LANES - flat.shape[0])).reshape(-1, LANES))
    total = sum(r.shape[0] for r in rows)
    rows.append(jnp.zeros(((-total) % PACK_ROWS, LANES), f32))
    return jnp.concatenate(rows), None


def _unpack(packed, shapes, lead=()):
    out, pos = [], 0
    for shp in shapes:
        n = math.prod(shp)
        r = _rows_of(n)
        piece = packed[..., pos:pos + r, :].reshape(lead + (r * LANES,))
        out.append(piece[..., :n].reshape(lead + tuple(shp)))
        pos += r
    return out


def _unshard(g):
    return jnp.moveaxis(g, 0, -2).reshape(g.shape[1:-1] + (g.shape[0] * g.shape[-1],))


def _rope_tables(S, Lc):
    n_r = RET_DK // 2
    f_r = RET_THETA ** (-jnp.arange(n_r, dtype=f32) / n_r)
    ang1 = jnp.arange(S, dtype=f32)[:, None] * f_r
    rows = S // GRID_W
    row = jnp.repeat(jnp.arange(rows, dtype=f32), GRID_W)
    col = jnp.tile(jnp.arange(GRID_W, dtype=f32), rows)
    n_ax = HEAD_DIM // 4
    f_ax = ROPE_THETA ** (-jnp.arange(n_ax, dtype=f32) / n_ax)
    ang2 = jnp.concatenate([row[:, None] * f_ax, col[:, None] * f_ax], axis=-1)
    cos2, sin2 = jnp.cos(ang2), jnp.sin(ang2)
    ones = lambda n: jnp.ones((Lc, n), f32)
    zeros = lambda n: jnp.zeros((Lc, n), f32)
    cos1 = jnp.concatenate([ones(n_r), jnp.cos(ang1)])
    sin1 = jnp.concatenate([zeros(n_r), jnp.sin(ang1)])
    cosf = jnp.concatenate([ones(HEAD_DIM), jnp.concatenate([cos2, cos2], axis=1)])
    sinf = jnp.concatenate([zeros(HEAD_DIM), jnp.concatenate([-sin2, sin2], axis=1)])
    return cos1, sin1, cosf, sinf


def kernel(x, c, ctx, c_ctx, mod_w, mod_b, norm_pre, norm_post, ffn_gate, ffn_up, ffn_down, ev_w_in, ev_w_out, lru_conv_w, lru_conv_b, lru_wa, lru_ba, lru_wx, lru_bx, lru_lambda, ret_decay_logit, ret_gn, od_w_in, od_w_out, pool_w, pool_scale, q_norm, k_norm, loss_target, m_c_ctx, m_mod_w, m_mod_b, m_norm_pre, m_norm_post, m_ffn_gate, m_ffn_up, m_ffn_down, m_ev_w_in, m_ev_w_out, m_lru_conv_w, m_lru_conv_b, m_lru_wa, m_lru_ba, m_lru_wx, m_lru_bx, m_lru_lambda, m_ret_decay_logit, m_ret_gn, m_od_w_in, m_od_w_out, m_pool_w, m_pool_scale, m_q_norm, m_k_norm, v_c_ctx, v_mod_w, v_mod_b, v_norm_pre, v_norm_post, v_ffn_gate, v_ffn_up, v_ffn_down, v_ev_w_in, v_ev_w_out, v_lru_conv_w, v_lru_conv_b, v_lru_wa, v_lru_ba, v_lru_wx, v_lru_bx, v_lru_lambda, v_ret_decay_logit, v_ret_gn, v_od_w_in, v_od_w_out, v_pool_w, v_pool_scale, v_q_norm, v_k_norm):
    wts = dict(c_ctx=c_ctx, mod_w=mod_w, mod_b=mod_b, norm_pre=norm_pre, norm_post=norm_post, ffn_gate=ffn_gate, ffn_up=ffn_up,
               ffn_down=ffn_down, ev_w_in=ev_w_in, ev_w_out=ev_w_out, lru_conv_w=lru_conv_w, lru_conv_b=lru_conv_b,
               lru_wa=lru_wa, lru_ba=lru_ba, lru_wx=lru_wx, lru_bx=lru_bx, lru_lambda=lru_lambda,
               ret_decay_logit=ret_decay_logit, ret_gn=ret_gn, od_w_in=od_w_in, od_w_out=od_w_out, pool_w=pool_w,
               pool_scale=pool_scale, q_norm=q_norm, k_norm=k_norm)
    mom_m = dict(zip(WEIGHT_NAMES, (m_c_ctx, m_mod_w, m_mod_b, m_norm_pre, m_norm_post, m_ffn_gate, m_ffn_up, m_ffn_down,
                                    m_ev_w_in, m_ev_w_out, m_lru_conv_w, m_lru_conv_b, m_lru_wa, m_lru_ba, m_lru_wx, m_lru_bx,
                                    m_lru_lambda, m_ret_decay_logit, m_ret_gn, m_od_w_in, m_od_w_out, m_pool_w, m_pool_scale,
                                    m_q_norm, m_k_norm)))
    mom_v = dict(zip(WEIGHT_NAMES, (v_c_ctx, v_mod_w, v_mod_b, v_norm_pre, v_norm_post, v_ffn_gate, v_ffn_up, v_ffn_down,
                                    v_ev_w_in, v_ev_w_out, v_lru_conv_w, v_lru_conv_b, v_lru_wa, v_lru_ba, v_lru_wx, v_lru_bx,
                                    v_lru_lambda, v_ret_decay_logit, v_ret_gn, v_od_w_in, v_od_w_out, v_pool_w, v_pool_scale,
                                    v_q_norm, v_k_norm)))

    _, S, D = x.shape
    Lc = ctx.shape[1]
    T = Lc + S
    TR = 256 if (Lc % 256 == 0 and S % 256 == 0) else 128
    assert Lc % TR == 0 and S % TR == 0 and TR % RET_CHUNK == 0
    cfg = RowCfg(TR, T // TR, Lc // TR)
    W = lru_conv_b.shape[-1]
    H = ret_decay_logit.shape[-1]
    U = POOL_GROUP * len(POOL_WINDOWS)
    nq = (N_CHIPS * od_w_in.shape[-1]) // U - 3
    assert W % (H * RET_DV) == 0 and (2 * W) % (H * RET_DK) == 0
    nL = mod_w.shape[0]
    C4 = mod_w.shape[-1]
    assert nL == 2, "two layers: an even mixer then an odd one"

    xi, yi, ci = _coords()
    chip = 2 * xi + yi
    me = 4 * xi + 2 * yi + ci

    sc = jax.nn.silu(c)
    small_in, _ = _pack([sc] + [wts[n] for n in SMALL_SHARDED])
    g1 = all_gather_small(small_in, "gather_small_fwd")
    parts = _unpack(g1, [sc.shape] + [wts[n].shape for n in SMALL_SHARDED], lead=(N_DEV,))
    sc_all = parts[0][:, 0]
    full = {n: _unshard(parts[1 + i][0::2]) for i, n in enumerate(SMALL_SHARDED)}
    for n in SMALL_REPL + ("c_ctx",):
        full[n] = wts[n]

    scc = jax.nn.silu(c_ctx)[None]
    pad_rows = MOD_ROWS - N_DEV - 1
    s16 = jnp.concatenate([sc_all, scc, jnp.zeros((pad_rows, D), f32)])
    modp = mod_fwd(s16, mod_w, "mod_fwd")
    g2 = all_gather_small(modp.reshape(-1, LANES), "gather_mod")
    mod_all = g2.reshape(N_DEV, nL, MOD_ROWS, C4)[0::2]
    mod_all = jnp.moveaxis(mod_all, 0, 2).reshape(nL, MOD_ROWS, N_CHIPS * C4) + mod_b[:, None, :]
    mod_l = lax.dynamic_index_in_dim(mod_all, me, axis=1, keepdims=False).reshape(nL, 3, 3, D)
    mod_c = mod_all[:, N_DEV].reshape(nL, 3, 3, D)

    def mod_of(li, s, kind, ctx_live=True):
        cpart = mod_c[li, s, kind] if ctx_live else jnp.zeros((D,), f32)
        return jnp.stack([cpart, mod_l[li, s, kind]])[:, None, :]

    packed = {n: cast_bf16(wts[n], "cast_" + n) for n in BIG}
    ffn_units = [(0, 0), (0, 1), (1, 0), (1, 1)]

    def G(*pieces):
        return Comm("gather", [(packed[n], idx) for n, idx in pieces])

    cos1, sin1, cosf, sinf = _rope_tables(S, Lc)

    def sub_params(li, s, ctx_live=True, gate_ctx_live=True):
        return dict(g_pre=full["norm_pre"][li, s][None], g_post=full["norm_post"][li, s][None],
                    shift=mod_of(li, s, 0, ctx_live), scale=mod_of(li, s, 1, ctx_live),
                    gate=mod_of(li, s, 2, ctx_live and gate_ctx_live))

    X0 = jnp.concatenate([ctx[0], x[0]], axis=0)
    wg00, wu00 = exchange(G(("ffn_gate", (0, 0)), ("ffn_up", (0, 0))), "gather_first")
    p00 = sub_params(0, 0)
    p00.update(wg=wg00, wu=wu00)
    h00 = {"gateup": G(("ffn_down", (0, 0)), ("ev_w_in", (0,))), "down": G(("ev_w_out", (0,)), ("ffn_gate", (0, 1)))}
    h = pre_fwd(cfg, X0, p00["g_pre"], p00["shift"], p00["scale"], "l0f0_pre")
    (a00, b00, u00), (wd00, ev_in) = ffn_gateup(cfg, h, wg00, wu00, "l0f0_gateup", comm=h00["gateup"])
    p00.update(wd=wd00)
    y00, (ev_out, wg01) = matmul("v2", u00, wd00, comm=h00["down"], name="l0f0_down")
    X1 = post_fwd(cfg, X0, y00, p00["g_post"], p00["gate"], FFN_STEP, "l0f0_post")
    s00 = (X0, h, a00, b00, u00, y00)

    p01 = sub_params(0, 1)
    p01.update(W=W, H=H, conv_w=full["lru_conv_w"][0], conv_b=full["lru_conv_b"], wa=full["lru_wa"][0],
               ba=full["lru_ba"][0][:, None, :], wx=full["lru_wx"][0], bx=full["lru_bx"][0][:, None, :],
               lam=full["lru_lambda"][0][:, None, :], logit=full["ret_decay_logit"][0][:, :, None, None],
               gn=full["ret_gn"], cos1=cos1, sin1=sin1, w_in=ev_in, w_out=ev_out)
    X2, s01, got = even_fwd(cfg, X1, p01, "l0mix", host={"in": G(("ffn_up", (0, 1))), "out": G(("ffn_down", (0, 1)))})
    p02 = sub_params(0, 2)
    p02.update(wg=wg01, wu=got["in"][0], wd=got["out"][0])
    X3, s02, got = ffn_fwd(cfg, X2, p02, "l0f1", host={"gateup": G(("ffn_gate", (1, 0)), ("ffn_up", (1, 0))),
                                                       "down": G(("ffn_down", (1, 0)))})
    p10 = sub_params(1, 0)
    p10.update(wg=got["gateup"][0], wu=got["gateup"][1], wd=got["down"][0])
    X4, s10, got = ffn_fwd(cfg, X3, p10, "l1f0", host={"gateup": G(("od_w_in", (0,)), ("od_w_out", (0,)), ("ffn_gate", (1, 1))),
                                                       "down": G(("ffn_up", (1, 1)))})
    p11 = sub_params(1, 1, gate_ctx_live=False)
    p11.update(nq=nq, pool_w=full["pool_w"][0], pool_scale=full["pool_scale"], qg=full["q_norm"], kg=full["k_norm"],
               cosf=cosf, sinf=sinf, w_in=got["gateup"][0], w_out=got["gateup"][1])
    p12 = sub_params(1, 2, ctx_live=False)
    p12.update(wg=got["gateup"][2], wu=got["down"][0])
    X5, s11, got = odd_fwd(cfg, X4, p11, "l1mix", host={"in": G(("ffn_down", (1, 1)))})
    p12.update(wd=got["in"][0])
    X6, s12, _ = ffn_fwd(cfg, X5, p12, "l1f1")
    sq, dX = loss_fwd_bwd(cfg, X6, loss_target[0], "loss")
    loss = lax.psum(0.5 * jnp.sum(sq) / D, ("x", "y", "c"))

    recv_ffn = {}
    dX, g12, recv_ffn[(1, 1)] = ffn_bwd(cfg, dX, s12, p12, "l1f1")
    dX, g11, recv_od = odd_bwd(cfg, dX, s11, p11, "l1mix")
    dX, g10, recv_ffn[(1, 0)] = ffn_bwd(cfg, dX, s10, p10, "l1f0")
    dX, g02, recv_ffn[(0, 1)] = ffn_bwd(cfg, dX, s02, p02, "l0f1")
    dX, g01, recv_ev = even_bwd(cfg, dX, s01, p01, "l0mix")
    dX, g00, recv_ffn[(0, 0)] = ffn_bwd(cfg, dX, s00, p00, "l0f0")
    grad_x = dX[Lc:][None]

    subs = [[g00, g01, g02], [g10, g11, g12]]
    zero_d = jnp.zeros((D,), f32)

    def dmod(group, live):
        rows = []
        for li in range(nL):
            for s in range(3):
                for kind, key in enumerate(("shift", "scale", "gate")):
                    rows.append(subs[li][s][key][group, 0] if live(li, s, kind) else zero_d)
        return jnp.stack(rows).reshape(nL, 9 * D)

    dmod_l = dmod(1, lambda li, s, kind: True)
    dmod_c = dmod(0, lambda li, s, kind: not (li == 1 and (s == 2 or (s == 1 and kind == 2))))

    dm_in, _ = _pack([dmod_l, dmod_c])
    g3 = all_gather_small(dm_in, "gather_dmod")
    dl_all, dc_all = _unpack(g3, [dmod_l.shape, dmod_c.shape], lead=(N_DEV,))
    dm16 = jnp.moveaxis(jnp.concatenate([dl_all, dc_all], axis=0), 0, 1)
    dm16 = lax.dynamic_slice_in_dim(dm16, chip * C4, C4, axis=2)
    s16b = jnp.concatenate([sc_all, jnp.broadcast_to(scc, (N_DEV, D))])
    g_mod_w, dscc_part = mod_bwd(s16b, dm16, mod_w, "mod_bwd")

    norm_pre_g = jnp.stack([jnp.concatenate([subs[li][s]["g_pre"] for s in range(3)]) for li in range(nL)])
    norm_post_g = jnp.stack([jnp.concatenate([subs[li][s]["g_post"] for s in range(3)]) for li in range(nL)])
    small_g = dict(norm_pre=norm_pre_g, norm_post=norm_post_g, lru_conv_w=g01["conv_w"][None], lru_ba=g01["ba"][:, 0][None],
                   lru_bx=g01["bx"][:, 0][None], lru_lambda=g01["lam"][:, 0][None], pool_scale=g11["pool_scale"],
                   mod_b=dmod_l + dmod_c, lru_conv_b=g01["conv_b"], lru_wa=g01["wa"][None], lru_wx=g01["wx"][None],
                   ret_decay_logit=g01["logit"][:, :, 0, 0][None], ret_gn=g01["gn"], pool_w=g11["pool_w"][None],
                   q_norm=g11["qg"], k_norm=g11["kg"])
    names = SMALL_SHARDED + SMALL_REPL
    sg_in, _ = _pack([small_g[n] for n in names] + [dscc_part])
    g4 = all_gather_small(sg_in, "gather_small_grads")
    tot = sum_leading(g4, "sum_small_grads")
    tot_parts = _unpack(tot, [small_g[n].shape for n in names])
    dscc_all = _unpack(g4, [small_g[n].shape for n in names] + [dscc_part.shape], lead=(N_DEV,))[-1]
    dscc = dscc_all[0, 0] + dscc_all[2, 0] + dscc_all[4, 0] + dscc_all[6, 0]
    _, silu_vjp = jax.vjp(jax.nn.silu, c_ctx)
    grads = {"c_ctx": silu_vjp(dscc)[0]}
    for n, g in zip(names, tot_parts):
        if n in SMALL_SHARDED:
            k = wts[n].shape[-1]
            g = lax.dynamic_slice_in_dim(g, chip * k, k, axis=g.ndim - 1)
        grads[n] = g.reshape(wts[n].shape)

    partial = {}
    for n, key in (("ffn_gate", "wg"), ("ffn_up", "wu"), ("ffn_down", "wd")):
        acc = None
        for u in ffn_units:
            acc = sum_leading(recv_ffn[u][key], "sum_%s_%d%d" % (n, u[0], u[1]), into=acc, full_shape=wts[n].shape, widx=u)
        partial[n] = acc
    for n, r in (("ev_w_in", recv_ev["w_in"]), ("ev_w_out", recv_ev["w_out"]), ("od_w_in", recv_od["w_in"]),
                 ("od_w_out", recv_od["w_out"])):
        partial[n] = sum_leading(r, "sum_" + n).reshape(wts[n].shape)
    partial = [partial[n] for n in BIG]
    other = swap_with_sibling(partial, "swap_partials")

    delta, new_m, new_v = {}, {}, {}
    for n, pa, pb in zip(BIG, partial, other):
        grads[n], delta[n], new_m[n], new_v[n] = adamw(wts[n], mom_m[n], mom_v[n], [pa, pb], "adamw_" + n)
    grads["mod_w"], delta["mod_w"], new_m["mod_w"], new_v["mod_w"] = adamw(mod_w, m_mod_w, v_mod_w, [g_mod_w], "adamw_mod_w")
    snames = [n for n in WEIGHT_NAMES if n not in BIG and n != "mod_w"]
    pk = lambda d: _pack([d[n] for n in snames])[0]
    sres = adamw(pk(wts), pk(mom_m), pk(mom_v), [pk(grads)], "adamw_small")
    for res, dst in zip(sres[1:], (delta, new_m, new_v)):
        for n, a in zip(snames, _unpack(res, [wts[n].shape for n in snames])):
            dst[n] = a

    return (loss, grad_x, *[grads[n] for n in WEIGHT_NAMES], *[delta[n] for n in WEIGHT_NAMES],
            *[new_m[n] for n in WEIGHT_NAMES], *[new_v[n] for n in WEIGHT_NAMES])
```

```python
import functools
import math

import jax
import jax.numpy as jnp
from jax import lax
from jax.experimental import pallas as pl
from jax.experimental.pallas import tpu as pltpu

f32 = jnp.float32
bf16 = jnp.bfloat16
MESH = pl.DeviceIdType.MESH

EPS = 1e-6
FFN_STEP = 0.5
LRU_C = 8.0
CONV_W = 4
CONV_LEFT = 2
RET_DK = 256
RET_DV = 256
RET_CHUNK = 128
RET_THETA = 10000.0
POOL_WINDOWS = (2, 4, 8, 16)
POOL_GROUP = 128
HEAD_DIM = 128
ROPE_THETA = 10000.0
GRID_W = 64
ADAM_LR = 0.001
ADAM_B1 = 0.9
ADAM_B2 = 0.999
ADAM_EPS = 1e-08
ADAM_WD = 0.01
ADAM_STEP = 10

N_CHIPS = 4
N_DEV = 8
HALO = 8
VMEM_LIMIT = 56 * 1024 * 1024


def _params(sem=None):
    return pltpu.CompilerParams(dimension_semantics=sem, vmem_limit_bytes=VMEM_LIMIT)


def _pick(n, prefs):
    for p in prefs:
        if n % p == 0:
            return p
    return n


def _sds(shape, dtype):
    return jax.ShapeDtypeStruct(tuple(shape), dtype)


MATMUL_VMEM = 46 * 1024 * 1024


def _fit_rows(M, tm, tn, out_dtype):
    fixed = 2 * tm * tn * 4 + 2 * tm * tn * jnp.dtype(out_dtype).itemsize
    for rows in (1408, 768, 512, 256, 128):
        if M % rows == 0 and fixed + 2 * rows * (tm + tn) * 2 <= MATMUL_VMEM:
            return rows
    return M


_MM_KINDS = {
    "v1": ((1, 0), "out[:, g] = A @ W[g]"),
    "v2": ((1, 0), "out = sum_g A[:, g] @ W[g]"),
    "v3": ((1, 1), "out = sum_g A[:, g] @ W[g]^T"),
    "v4": ((1, 1), "out[:, g] = A @ W[g]^T"),
    "v5": ((0, 0), "out[g] = A^T @ C[:, g]"),
    "v6": ((0, 0), "out[g] = A[:, g]^T @ C"),
}


def matmul(kind, a, b, *, widx=(), out_dtype=f32, init=None, gshape=None, comm=None, name):
    nw = len(widx)
    cdims = _MM_KINDS[kind][0]
    if kind in ("v1", "v2", "v3", "v4"):
        G = b.shape[0]
        d1, d2 = b.shape[-2:]
        M = a.shape[0]
    else:
        G, d1, d2 = gshape
        M = a.shape[0]
    tm_p, tn_p, tk_p = (768, 512, 256, 128), (1408, 1536, 1024, 768, 512, 256, 128), (2048, 1408, 1536, 1024, 768, 512, 256, 128)
    wnone = (None,) * (1 + nw)

    if kind == "v1":
        K, Ns = d1, d2
        tm, tn, tk = _pick(M, tm_p), _pick(Ns, tn_p), _pick(K, tk_p)
        nI, nJ, nR = M // tm, Ns // tn, K // tk
        grid = (G, nI, nJ, nR)
        a_spec = pl.BlockSpec((tm, tk), lambda g, i, j, r: (i, r))
        b_spec = pl.BlockSpec(wnone + (tk, tn), lambda g, i, j, r: (g,) + widx + (r, j))
        o_spec = pl.BlockSpec((tm, tn), lambda g, i, j, r: (i, g * nJ + j))
        out_shape = _sds((M, G * Ns), out_dtype)
        acc_shape = (tm, tn)
    elif kind == "v2":
        Ks, N = d1, d2
        tm, tn, tk = _pick(M, tm_p), _pick(N, (2048,) + tn_p), _pick(Ks, tk_p)
        nI, nJ, nRk = M // tm, N // tn, Ks // tk
        nR = G * nRk
        grid = (1, nI, nJ, nR)
        a_spec = pl.BlockSpec((tm, tk), lambda g, i, j, r: (i, r))
        b_spec = pl.BlockSpec(wnone + (tk, tn), lambda g, i, j, r: (r // nRk,) + widx + (r % nRk, j))
        o_spec = pl.BlockSpec((tm, tn), lambda g, i, j, r: (i, j))
        out_shape = _sds((M, N), out_dtype)
        acc_shape = (tm, tn)
    elif kind == "v3":
        K, Ns = d1, d2
        tm, tn, tk = _pick(M, tm_p), _pick(K, (2048,) + tn_p), _pick(Ns, tk_p)
        nI, nJ, nRk = M // tm, K // tn, Ns // tk
        nR = G * nRk
        grid = (1, nI, nJ, nR)
        a_spec = pl.BlockSpec((tm, tk), lambda g, i, j, r: (i, r))
        b_spec = pl.BlockSpec(wnone + (tn, tk), lambda g, i, j, r: (r // nRk,) + widx + (j, r % nRk))
        o_spec = pl.BlockSpec((tm, tn), lambda g, i, j, r: (i, j))
        out_shape = _sds((M, K), out_dtype)
        acc_shape = (tm, tn)
    elif kind == "v4":
        Ks, N = d1, d2
        tm, tn, tk = _pick(M, tm_p), _pick(Ks, tn_p), _pick(N, tk_p)
        nI, nJ, nR = M // tm, Ks // tn, N // tk
        grid = (G, nI, nJ, nR)
        a_spec = pl.BlockSpec((tm, tk), lambda g, i, j, r: (i, r))
        b_spec = pl.BlockSpec(wnone + (tn, tk), lambda g, i, j, r: (g,) + widx + (j, r))
        o_spec = pl.BlockSpec((tm, tn), lambda g, i, j, r: (i, g * nJ + j))
        out_shape = _sds((M, G * Ks), out_dtype)
        acc_shape = (tm, tn)
    elif kind == "v5":
        K, Ns = d1, d2
        tm, tn, tk = _pick(K, (2048,) + tm_p), _pick(Ns, tn_p), 0
        tk = _fit_rows(M, tm, tn, out_dtype)
        nI, nJ, nR = K // tm, Ns // tn, M // tk
        grid = (G, nI, nJ, nR)
        a_spec = pl.BlockSpec((tk, tm), lambda g, i, j, r: (r, i))
        b_spec = pl.BlockSpec((tk, tn), lambda g, i, j, r: (r, g * nJ + j))
        o_spec = pl.BlockSpec((None, tm, tn), lambda g, i, j, r: (g, i, j))
        out_shape = _sds(gshape, out_dtype)
        acc_shape = (tm, tn)
    else:
        Ks, N = d1, d2
        tm, tn, tk = _pick(Ks, (1408,) + tm_p), _pick(N, (2048,) + tn_p), 0
        tk = _fit_rows(M, tm, tn, out_dtype)
        nI, nJ, nR = Ks // tm, N // tn, M // tk
        grid = (G, nI, nJ, nR)
        a_spec = pl.BlockSpec((tk, tm), lambda g, i, j, r: (r, g * nI + i))
        b_spec = pl.BlockSpec((tk, tn), lambda g, i, j, r: (r, j))
        o_spec = pl.BlockSpec((None, tm, tn), lambda g, i, j, r: (g, i, j))
        out_shape = _sds(gshape, out_dtype)
        acc_shape = (tm, tn)

    has_init = init is not None
    ncomm = len(comm.srcs) if comm is not None else 0

    def body(*refs):
        a_ref, b_ref = refs[0], refs[1]
        pos = 2
        init_ref = None
        if has_init:
            init_ref = refs[pos]
            pos += 1
        cin = refs[pos:pos + ncomm]
        pos += ncomm
        o_ref = refs[pos]
        cout = refs[pos + 1:pos + 1 + ncomm]
        acc_ref = refs[pos + 1 + ncomm]
        sems = refs[pos + 2 + ncomm:]
        r = pl.program_id(3)
        first, last = _grid_ends(grid)

        if ncomm:
            @pl.when(first)
            def _():
                _comm_start(comm, cin, cout, *sems)

        def prod():
            return lax.dot_general(a_ref[...], b_ref[...], ((cdims[:1], cdims[1:]), ((), ())), preferred_element_type=f32)

        def start():
            return init_ref[...] + prod() if has_init else prod()

        if nR == 1:
            o_ref[...] = start().astype(o_ref.dtype)
        else:
            @pl.when(r == 0)
            def _():
                acc_ref[...] = start()

            @pl.when(jnp.logical_and(r > 0, r < nR - 1))
            def _():
                acc_ref[...] += prod()

            @pl.when(r == nR - 1)
            def _():
                o_ref[...] = (acc_ref[...] + prod()).astype(o_ref.dtype)

        if ncomm:
            @pl.when(last)
            def _():
                _comm_wait(comm, cin, cout, *sems)

    in_specs = [a_spec, b_spec]
    args = [a, b]
    if has_init:
        in_specs.append(pl.BlockSpec((tm, tn), lambda g, i, j, r: (i, j)))
        args.append(init)
    out_specs, out_shapes, scratch = [o_spec], [out_shape], [pltpu.VMEM(acc_shape, f32)]
    if ncomm:
        hbm = pl.BlockSpec(memory_space=pl.ANY)
        in_specs += [hbm] * ncomm
        args += [src for src, _ in comm.srcs]
        out_specs += [hbm] * ncomm
        out_shapes += comm.out_shapes()
        scratch += _comm_sems(ncomm)
    res = pl.pallas_call(
        body, name=name, grid=grid, in_specs=in_specs, out_specs=out_specs, out_shape=out_shapes,
        scratch_shapes=scratch, compiler_params=_params(("arbitrary", "arbitrary", "arbitrary", "arbitrary")),
    )(*args)
    return (res[0], list(res[1:])) if ncomm else res[0]


class Comm:
    def __init__(self, mode, srcs):
        self.mode, self.srcs = mode, srcs

    def piece(self, n):
        arr, idx = self.srcs[n]
        shp = arr.shape[len(idx):]
        return shp if self.mode == "gather" else shp[1:]

    def out_shapes(self):
        return [_sds((N_CHIPS,) + tuple(self.piece(n)), self.srcs[n][0].dtype) for n in range(len(self.srcs))]


def _comm_sems(n):
    nsem = n * (N_CHIPS - 1)
    return [pltpu.SemaphoreType.DMA((nsem,)), pltpu.SemaphoreType.DMA((nsem,)), pltpu.SemaphoreType.DMA((n,))]


def _coords():
    return lax.axis_index("x"), lax.axis_index("y"), lax.axis_index("c")


def _flip(v, bit):
    return 1 - v if bit else v


def _chip_peers(x, y, c):
    out = []
    for k in range(1, N_CHIPS):
        kx, ky = (k >> 1) & 1, k & 1
        px, py = _flip(x, kx), _flip(y, ky)
        out.append((k, (px, py, c), 2 * px + py))
    return out


def _comm_copies(comm, in_refs, out_refs, send_sems, recv_sems, local_sems, with_recvs):
    x, y, c = _coords()
    s = 2 * x + y
    local, sends, recvs = [], [], []
    for w, (_, idx) in enumerate(comm.srcs):
        src = in_refs[w].at[idx] if idx else in_refs[w]
        out = out_refs[w]
        if comm.mode == "gather":
            local.append(pltpu.make_async_copy(src, out.at[s], local_sems.at[w]))
        else:
            local.append(pltpu.make_async_copy(src.at[s], out.at[N_CHIPS - 1], local_sems.at[w]))
        for k, peer, pidx in _chip_peers(x, y, c):
            j = w * (N_CHIPS - 1) + k - 1
            if comm.mode == "gather":
                out_src, out_dst, in_dst = src, out.at[s], out.at[pidx]
            else:
                out_src, out_dst, in_dst = src.at[pidx], out.at[k - 1], out.at[k - 1]
            sends.append(pltpu.make_async_remote_copy(src_ref=out_src, dst_ref=out_dst, send_sem=send_sems.at[j],
                                                      recv_sem=recv_sems.at[j], device_id=peer, device_id_type=MESH))
            if with_recvs:
                recvs.append(pltpu.make_async_remote_copy(src_ref=out_src, dst_ref=in_dst, send_sem=send_sems.at[j],
                                                          recv_sem=recv_sems.at[j], device_id=peer, device_id_type=MESH))
    return local, sends, recvs


def _comm_start(comm, in_refs, out_refs, send_sems, recv_sems, local_sems):
    local, sends, _ = _comm_copies(comm, in_refs, out_refs, send_sems, recv_sems, local_sems, False)
    for cp in local + sends:
        cp.start()


def _comm_wait(comm, in_refs, out_refs, send_sems, recv_sems, local_sems):
    local, sends, recvs = _comm_copies(comm, in_refs, out_refs, send_sems, recv_sems, local_sems, True)
    for cp in recvs:
        cp.wait_recv()
    for cp in sends:
        cp.wait_send()
    for cp in local:
        cp.wait()


def exchange(comm, name):
    n = len(comm.srcs)

    def body(*refs):
        in_refs, out_refs, sems = refs[:n], refs[n:2 * n], refs[2 * n:]
        _comm_start(comm, in_refs, out_refs, *sems)
        _comm_wait(comm, in_refs, out_refs, *sems)

    hbm = pl.BlockSpec(memory_space=pl.ANY)
    return pl.pallas_call(body, name=name, out_shape=comm.out_shapes(), in_specs=[hbm] * n, out_specs=[hbm] * n,
                          scratch_shapes=_comm_sems(n))(*[src for src, _ in comm.srcs])


class RowCfg:
    def __init__(self, TR, nT, cT):
        self.TR, self.nT, self.cT = TR, nT, cT


def _row_spec(cfg, spec, off):
    kind = spec[0]
    TR = cfg.TR
    hb = TR // HALO
    nH = cfg.nT * hb
    if kind == "row":
        _, arr, w, cb = spec
        return pl.BlockSpec((TR, w), lambda i: (i + off, cb))
    if kind == "prev":
        _, arr, w, cb = spec
        return pl.BlockSpec((HALO, w), lambda i: (jnp.maximum((i + off) * hb - 1, 0), cb))
    if kind == "next":
        _, arr, w, cb = spec
        return pl.BlockSpec((HALO, w), lambda i: (jnp.minimum((i + off + 1) * hb, nH - 1), cb))
    if kind == "full":
        arr = spec[1]
        nd = arr.ndim
        return pl.BlockSpec(arr.shape, lambda i: (0,) * nd)
    if kind == "grp":
        arr = spec[1]
        cT = cfg.cT
        return pl.BlockSpec((None, 1, arr.shape[-1]), lambda i: (((i + off) >= cT).astype(jnp.int32), 0, 0))
    if kind == "drow":
        _, arr, w, cb = spec
        return pl.BlockSpec((arr.shape[0], TR, w), lambda i: (0, i + off, cb))
    raise ValueError(kind)


def rowcall(cfg, fn, name, ins, outs, *, off=0, n=None, scratch=()):
    n = cfg.nT - off if n is None else n
    in_specs = [_row_spec(cfg, s, off) for s in ins]
    out_specs = [_row_spec(cfg, (s[0], s[1]) + tuple(s[2:]), off) for s in outs]
    out_shape = [s[1] for s in outs]

    def body(*refs):
        fn(pl.program_id(0) + off, *refs)

    res = pl.pallas_call(
        body, name=name, grid=(n,), in_specs=in_specs, out_specs=out_specs, out_shape=out_shape,
        scratch_shapes=list(scratch), compiler_params=_params(("arbitrary",)),
    )(*[s[1] for s in ins])
    return res


def _acc(ref, val, first):
    @pl.when(first)
    def _():
        ref[...] = val

    @pl.when(jnp.logical_not(first))
    def _():
        ref[...] += val


def _rms(x):
    return x * lax.rsqrt(jnp.mean(x * x, axis=-1, keepdims=True) + EPS)


def _pre_fn(x, g, shift, scale):
    return (_rms(x) * g) * (1.0 + scale) + shift


def _strips(TR, rows_per_strip, body, init):
    n = TR // rows_per_strip
    unroll = STRIP_UNROLL if n % STRIP_UNROLL == 0 else 1

    def step(r, carry):
        for u in range(unroll):
            start = pl.multiple_of((r * unroll + u) * rows_per_strip, rows_per_strip)
            carry = body(pl.ds(start, rows_per_strip), carry)
        return carry

    return lax.fori_loop(0, n // unroll, step, init)


STRIP_UNROLL = 8
F32_STRIP = 8
BF16_STRIP = 16


def _inv_rms(x):
    return lax.rsqrt(jnp.mean(x * x, axis=-1, keepdims=True) + EPS)


def pre_fwd(cfg, x, g, shift, scale, name):
    D = x.shape[1]

    def fn(i, x_ref, g_ref, sh_ref, sc_ref, h_ref):
        c = g_ref[...] * (1.0 + sc_ref[...])
        sh = sh_ref[...]

        def strip(rows, carry):
            xv = x_ref[rows, :]
            h_ref[rows, :] = (xv * _inv_rms(xv) * c + sh).astype(bf16)
            return carry

        _strips(cfg.TR, BF16_STRIP, strip, 0)

    return rowcall(cfg, fn, name, [("row", x, D, 0), ("full", g), ("grp", shift), ("grp", scale)],
                   [("row", _sds(x.shape, bf16), D, 0)])[0]


def pre_bwd(cfg, x, g, shift, scale, dh, dx_in, name):
    D = x.shape[1]
    cT = cfg.cT

    def fn(i, x_ref, g_ref, sh_ref, sc_ref, dh_ref, dxin_ref, dx_ref, dg_ref, dsh_ref, dsc_ref):
        gv, scv = g_ref[...], sc_ref[...]
        c = gv * (1.0 + scv)

        def strip(rows, carry):
            s0, s1 = carry
            xv, dhv = x_ref[rows, :], dh_ref[rows, :]
            r = _inv_rms(xv)
            xn = xv * r
            dxn = dhv * c
            m = jnp.mean(dxn * xn, axis=-1, keepdims=True)
            dx_ref[rows, :] = dxin_ref[rows, :] + r * (dxn - xn * m)
            return s0 + dhv, s1 + dhv * xn

        zero = jnp.zeros((F32_STRIP, D), f32)
        s0, s1 = _strips(cfg.TR, F32_STRIP, strip, (zero, zero))
        s0 = jnp.sum(s0, axis=0, keepdims=True)
        s1 = jnp.sum(s1, axis=0, keepdims=True)
        _acc(dg_ref, (1.0 + scv) * s1, i == 0)
        first = jnp.logical_or(i == 0, i == cT)
        _acc(dsh_ref, s0, first)
        _acc(dsc_ref, gv * s1, first)

    return rowcall(cfg, fn, name,
                   [("row", x, D, 0), ("full", g), ("grp", shift), ("grp", scale), ("row", dh, D, 0), ("row", dx_in, D, 0)],
                   [("row", _sds(x.shape, f32), D, 0), ("full", _sds((1, D), f32)),
                    ("grp", _sds((2, 1, D), f32)), ("grp", _sds((2, 1, D), f32))])


def _post_fn(w, y, g, gate):
    return (w * gate) * (_rms(y) * g)


def post_fwd(cfg, x, y, g, gate, w, name):
    D = x.shape[1]

    def fn(i, x_ref, y_ref, g_ref, gt_ref, o_ref):
        c = (w * gt_ref[...]) * g_ref[...]

        def strip(rows, carry):
            yv = y_ref[rows, :]
            o_ref[rows, :] = x_ref[rows, :] + c * (yv * _inv_rms(yv))
            return carry

        _strips(cfg.TR, F32_STRIP, strip, 0)

    return rowcall(cfg, fn, name, [("row", x, D, 0), ("row", y, D, 0), ("full", g), ("grp", gate)],
                   [("row", _sds(x.shape, f32), D, 0)])[0]


def post_bwd(cfg, dx, y, g, gate, w, name):
    D = dx.shape[1]
    cT = cfg.cT
    half = BF16_STRIP // 2

    def fn(i, dx_ref, y_ref, g_ref, gt_ref, dy_ref, dg_ref, dgt_ref):
        gv, gtv = g_ref[...], gt_ref[...]
        c = (w * gtv) * gv

        def strip(rows, s1):
            yv, dv = y_ref[rows, :], dx_ref[rows, :]
            r = _inv_rms(yv)
            yn = yv * r
            dyn = dv * c
            m = jnp.mean(dyn * yn, axis=-1, keepdims=True)
            dy_ref[rows, :] = (r * (dyn - yn * m)).astype(bf16)
            t = dv * yn
            return s1 + t[:half] + t[half:]

        s1 = _strips(cfg.TR, BF16_STRIP, strip, jnp.zeros((half, D), f32))
        s1 = jnp.sum(s1, axis=0, keepdims=True)
        _acc(dg_ref, (w * gtv) * s1, i == 0)
        _acc(dgt_ref, (w * gv) * s1, jnp.logical_or(i == 0, i == cT))

    return rowcall(cfg, fn, name, [("row", dx, D, 0), ("row", y, D, 0), ("full", g), ("grp", gate)],
                   [("row", _sds(dx.shape, bf16), D, 0), ("full", _sds((1, D), f32)), ("grp", _sds((2, 1, D), f32))])


def _swiglu_fn(a, b):
    return jax.nn.silu(a) * b


def swiglu_fwd(cfg, a, b, name):
    F = a.shape[1]
    tf = _pick(F, (1408, 1024, 512, 256, 128))
    TR = cfg.TR

    def body(a_ref, b_ref, u_ref):
        u_ref[...] = _swiglu_fn(a_ref[...], b_ref[...]).astype(bf16)

    spec = pl.BlockSpec((TR, tf), lambda i, j: (i, j))
    return pl.pallas_call(body, name=name, grid=(cfg.nT, F // tf), in_specs=[spec, spec], out_specs=spec,
                          out_shape=_sds(a.shape, bf16), compiler_params=_params(("arbitrary", "arbitrary")))(a, b)


def swiglu_bwd(cfg, a, b, du, name):
    F = a.shape[1]
    tf = _pick(F, (1408, 1024, 512, 256, 128))
    TR = cfg.TR

    def body(a_ref, b_ref, du_ref, da_ref, db_ref):
        _, vjp = jax.vjp(_swiglu_fn, a_ref[...], b_ref[...])
        da, db = vjp(du_ref[...])
        da_ref[...] = da.astype(bf16)
        db_ref[...] = db.astype(bf16)

    spec = pl.BlockSpec((TR, tf), lambda i, j: (i, j))
    return pl.pallas_call(body, name=name, grid=(cfg.nT, F // tf), in_specs=[spec, spec, spec], out_specs=[spec, spec],
                          out_shape=[_sds(a.shape, bf16), _sds(a.shape, bf16)],
                          compiler_params=_params(("arbitrary", "arbitrary")))(a, b, du)


def _hosted(host, role, got, fn):
    comm = host.get(role) if host else None
    if comm is None:
        return fn(None)
    out, res = fn(comm)
    got[role] = res
    return out


def _grid_ends(grid):
    ids = [pl.program_id(n) for n in range(len(grid))]
    first = functools.reduce(jnp.logical_and, [i == 0 for i in ids])
    last = functools.reduce(jnp.logical_and, [i == n - 1 for i, n in zip(ids, grid)])
    return first, last


def _comm_plumbing(comm):
    if comm is None:
        return [], [], [], [], []
    n = len(comm.srcs)
    hbm = pl.BlockSpec(memory_space=pl.ANY)
    return [hbm] * n, [src for src, _ in comm.srcs], [hbm] * n, comm.out_shapes(), _comm_sems(n)


FFN_ROWS = 384


def ffn_gateup(cfg, h, wg, wu, name, comm=None):
    M, K = h.shape
    G, _, F = wg.shape
    tm = _pick(M, (FFN_ROWS, 256, 128))
    grid = (G, M // tm)
    ncomm = len(comm.srcs) if comm is not None else 0

    def body(h_ref, wg_ref, wu_ref, *rest):
        cin, rest = rest[:ncomm], rest[ncomm:]
        s_ref, t_ref, u_ref = rest[:3]
        cout, sems = rest[3:3 + ncomm], rest[3 + ncomm:]
        first, last = _grid_ends(grid)
        if ncomm:
            @pl.when(first)
            def _():
                _comm_start(comm, cin, cout, *sems)

        hv = h_ref[...]
        a = jnp.dot(hv, wg_ref[...], preferred_element_type=f32)
        b = jnp.dot(hv, wu_ref[...], preferred_element_type=f32)
        sig = jax.nn.sigmoid(a)
        sa = a * sig
        s_ref[...] = sa.astype(bf16)
        t_ref[...] = (b * (sig + sa * (1.0 - sig))).astype(bf16)
        u_ref[...] = (sa * b).astype(bf16)
        if ncomm:
            @pl.when(last)
            def _():
                _comm_wait(comm, cin, cout, *sems)

    ci, ca, co, cs, csem = _comm_plumbing(comm)
    w_spec = pl.BlockSpec((None, K, F), lambda g, i: (g, 0, 0))
    o_spec = pl.BlockSpec((tm, F), lambda g, i: (i, g))
    res = pl.pallas_call(
        body, name=name, grid=grid, in_specs=[pl.BlockSpec((tm, K), lambda g, i: (i, 0)), w_spec, w_spec] + ci,
        out_specs=[o_spec] * 3 + co, out_shape=[_sds((M, G * F), bf16)] * 3 + cs, scratch_shapes=csem,
        compiler_params=_params(("arbitrary", "arbitrary")))(h, wg, wu, *ca)
    return (res[0], res[1], res[2]), list(res[3:])


def ffn_du_act(cfg, dy, wd, s, t, name, comm=None):
    M, N = dy.shape
    G, F, _ = wd.shape
    tm = _pick(M, (FFN_ROWS, 256, 128))
    grid = (G, M // tm)
    ncomm = len(comm.srcs) if comm is not None else 0

    def body(dy_ref, wd_ref, s_ref, t_ref, *rest):
        cin, rest = rest[:ncomm], rest[ncomm:]
        da_ref, db_ref = rest[:2]
        cout, sems = rest[2:2 + ncomm], rest[2 + ncomm:]
        first, last = _grid_ends(grid)
        if ncomm:
            @pl.when(first)
            def _():
                _comm_start(comm, cin, cout, *sems)

        du = _dotf(dy_ref[...], wd_ref[...], 1, 1)
        da_ref[...] = (du * t_ref[...].astype(f32)).astype(bf16)
        db_ref[...] = (du * s_ref[...].astype(f32)).astype(bf16)
        if ncomm:
            @pl.when(last)
            def _():
                _comm_wait(comm, cin, cout, *sems)

    ci, ca, co, cs, csem = _comm_plumbing(comm)
    t_spec = pl.BlockSpec((tm, F), lambda g, i: (i, g))
    res = pl.pallas_call(
        body, name=name, grid=grid,
        in_specs=[pl.BlockSpec((tm, N), lambda g, i: (i, 0)), pl.BlockSpec((None, F, N), lambda g, i: (g, 0, 0)), t_spec,
                  t_spec] + ci,
        out_specs=[t_spec, t_spec] + co, out_shape=[_sds((M, G * F), bf16)] * 2 + cs, scratch_shapes=csem,
        compiler_params=_params(("arbitrary", "arbitrary")))(dy, wd, s, t, *ca)
    return (res[0], res[1]), list(res[2:])


def ffn_dh(cfg, da, db, wg, wu, name, comm=None):
    M = da.shape[0]
    G, K, F = wg.shape
    tm = _pick(M, (768, 512, 256, 128))
    nR = 2 * G
    grid = (M // tm, nR)
    ncomm = len(comm.srcs) if comm is not None else 0

    def body(da_ref, db_ref, wg_ref, wu_ref, *rest):
        cin, rest = rest[:ncomm], rest[ncomm:]
        o_ref = rest[0]
        cout, acc_ref, sems = rest[1:1 + ncomm], rest[1 + ncomm], rest[2 + ncomm:]
        r = pl.program_id(1)
        first, last = _grid_ends(grid)
        if ncomm:
            @pl.when(first)
            def _():
                _comm_start(comm, cin, cout, *sems)

        @pl.when(r == 0)
        def _():
            acc_ref[...] = _dotf(da_ref[...], wg_ref[...], 1, 1)

        @pl.when(jnp.logical_and(r > 0, r < G))
        def _():
            acc_ref[...] += _dotf(da_ref[...], wg_ref[...], 1, 1)

        @pl.when(jnp.logical_and(r >= G, r < nR - 1))
        def _():
            acc_ref[...] += _dotf(db_ref[...], wu_ref[...], 1, 1)

        @pl.when(r == nR - 1)
        def _():
            o_ref[...] = acc_ref[...] + _dotf(db_ref[...], wu_ref[...], 1, 1)

        if ncomm:
            @pl.when(last)
            def _():
                _comm_wait(comm, cin, cout, *sems)

    ga = lambda r: jnp.minimum(r, G - 1)
    gb = lambda r: jnp.maximum(r - G, 0)
    ci, ca, co, cs, csem = _comm_plumbing(comm)
    res = pl.pallas_call(
        body, name=name, grid=grid,
        in_specs=[pl.BlockSpec((tm, F), lambda i, r: (i, ga(r))), pl.BlockSpec((tm, F), lambda i, r: (i, gb(r))),
                  pl.BlockSpec((None, K, F), lambda i, r: (ga(r), 0, 0)),
                  pl.BlockSpec((None, K, F), lambda i, r: (gb(r), 0, 0))] + ci,
        out_specs=[pl.BlockSpec((tm, K), lambda i, r: (i, 0))] + co, out_shape=[_sds((M, K), f32)] + cs,
        scratch_shapes=[pltpu.VMEM((tm, K), f32)] + csem,
        compiler_params=_params(("arbitrary", "arbitrary")))(da, db, wg, wu, *ca)
    return res[0], list(res[1:])


def ffn_fwd(cfg, x, p, tag, host=None):
    got = {}
    host = host or {}
    h = pre_fwd(cfg, x, p["g_pre"], p["shift"], p["scale"], tag + "_pre")
    (a, b, u), res = ffn_gateup(cfg, h, p["wg"], p["wu"], tag + "_gateup", comm=host.get("gateup"))
    if res:
        got["gateup"] = res
    y = _hosted(host, "down", got, lambda cm: matmul("v2", u, p["wd"], comm=cm, name=tag + "_down"))
    xo = post_fwd(cfg, x, y, p["g_post"], p["gate"], FFN_STEP, tag + "_post")
    return xo, (x, h, a, b, u, y), got


def ffn_bwd(cfg, dX, saved, p, tag):
    x, h, a, b, u, y = saved
    dy, dg_post, dgate = post_bwd(cfg, dX, y, p["g_post"], p["gate"], FFN_STEP, tag + "_postb")
    (da, db), _ = ffn_du_act(cfg, dy, p["wd"], a, b, tag + "_duact")
    gwd = matmul("v6", u, dy, gshape=p["wd"].shape, out_dtype=bf16, name=tag + "_gwd")
    gwg = matmul("v5", h, da, gshape=p["wg"].shape, out_dtype=bf16, name=tag + "_gwg")
    gwu, r_wd = matmul("v5", h, db, gshape=p["wu"].shape, out_dtype=bf16, comm=Comm("scatter", [(gwd, ())]),
                       name=tag + "_gwu")
    dh, (r_wg, r_wu) = ffn_dh(cfg, da, db, p["wg"], p["wu"], tag + "_dh", comm=Comm("scatter", [(gwg, ()), (gwu, ())]))
    dX, dg_pre, dshift, dscale = pre_bwd(cfg, x, p["g_pre"], p["shift"], p["scale"], dh, dX, tag + "_preb")
    small = dict(g_pre=dg_pre, g_post=dg_post, shift=dshift, scale=dscale, gate=dgate)
    return dX, small, dict(wg=r_wg, wu=r_wu, wd=r_wd[0])


def _seg_flags(cfg, i):
    start = jnp.logical_or(i == 0, i == cfg.cT)
    end = jnp.logical_or(i == cfg.cT - 1, i == cfg.nT - 1)
    return start, end


def _fill_halo(buf, cur, prev, nxt, start, end, TR):
    buf[pl.ds(0, HALO), :] = jnp.where(start, 0.0, prev)
    buf[pl.ds(HALO, TR), :] = cur
    buf[pl.ds(HALO + TR, HALO), :] = jnp.where(end, 0.0, nxt)


def conv_fwd(cfg, z, cw, cb, W, name):
    TR = cfg.TR

    def fn(i, r_ref, rp_ref, rn_ref, cw_ref, cb_ref, u_ref, buf):
        start, end = _seg_flags(cfg, i)
        _fill_halo(buf, r_ref[...], rp_ref[...], rn_ref[...], start, end, TR)
        u = jnp.broadcast_to(cb_ref[...], (TR, W))
        for k in range(CONV_W):
            u = u + buf[pl.ds(HALO + k - CONV_LEFT, TR), :] * cw_ref[pl.ds(k, 1), :]
        u_ref[...] = u

    return rowcall(cfg, fn, name, [("row", z, W, 1), ("prev", z, W, 1), ("next", z, W, 1), ("full", cw), ("full", cb)],
                   [("row", _sds((z.shape[0], W), f32), W, 0)], scratch=[pltpu.VMEM((TR + 2 * HALO, W), f32)])[0]


def conv_bwd(cfg, z, du, cw, W, name):
    TR = cfg.TR

    def fn(i, r_ref, rp_ref, rn_ref, du_ref, dup_ref, dun_ref, cw_ref, dr_ref, dcw_ref, dcb_ref, rbuf, dbuf):
        start, end = _seg_flags(cfg, i)
        _fill_halo(rbuf, r_ref[...], rp_ref[...], rn_ref[...], start, end, TR)
        _fill_halo(dbuf, du_ref[...], dup_ref[...], dun_ref[...], start, end, TR)
        du = du_ref[...]

        @pl.when(i == 0)
        def _():
            dcw_ref[...] = jnp.zeros(dcw_ref.shape, f32)
            dcb_ref[...] = jnp.zeros(dcb_ref.shape, f32)

        dr = jnp.zeros((TR, W), f32)
        for k in range(CONV_W):
            dr = dr + dbuf[pl.ds(HALO - (k - CONV_LEFT), TR), :] * cw_ref[pl.ds(k, 1), :]
            dcw_ref[pl.ds(k, 1), :] += jnp.sum(du * rbuf[pl.ds(HALO + k - CONV_LEFT, TR), :], axis=0, keepdims=True)
        dcb_ref[...] += jnp.sum(du, axis=0, keepdims=True)
        dr_ref[...] = dr

    T = z.shape[0]
    return rowcall(cfg, fn, name,
                   [("row", z, W, 1), ("prev", z, W, 1), ("next", z, W, 1), ("row", du, W, 0), ("prev", du, W, 0),
                    ("next", du, W, 0), ("full", cw)],
                   [("row", _sds((T, W), f32), W, 0), ("full", _sds((CONV_W, W), f32)), ("full", _sds((1, W), f32))],
                   scratch=[pltpu.VMEM((TR + 2 * HALO, W), f32), pltpu.VMEM((TR + 2 * HALO, W), f32)])


def _softplus(x):
    return jnp.maximum(x, 0.0) + jnp.log1p(jnp.exp(-jnp.abs(x)))


def _neg_expm1(x):
    series = -x * (1.0 + x * (0.5 + x * (1.0 / 6.0 + x * (1.0 / 24.0 + x * (1.0 / 120.0 + x * (1.0 / 720.0))))))
    return jnp.where(x > -0.1, series, 1.0 - jnp.exp(x))


def _lru_coef(u, pa, px, lam):
    r = jax.nn.sigmoid(pa)
    i = jax.nn.sigmoid(px)
    log_a = -LRU_C * r * _softplus(-lam)
    a = jnp.exp(log_a)
    b = jnp.sqrt(_neg_expm1(2.0 * log_a)) * (i * u)
    return a, b


def _blockdiag(u_bf, w_ref, d, nblk, blk):
    return jnp.concatenate(
        [jnp.dot(u_bf[:, n * blk:(n + 1) * blk], w_ref[d, n].astype(bf16), preferred_element_type=f32)
         for n in range(nblk)], axis=1)


def lru_coef_fwd(cfg, u, wa, ba, wx, bx, lam, name):
    T, W = u.shape
    nblk, blk = wa.shape[1], wa.shape[2]

    def fn(i, u_ref, wa_ref, ba_ref, wx_ref, bx_ref, lam_ref, a_ref, b_ref):
        uv = u_ref[...]
        u_bf = uv.astype(bf16)
        for d in range(2):
            pa = _blockdiag(u_bf, wa_ref, d, nblk, blk) + ba_ref[d]
            px = _blockdiag(u_bf, wx_ref, d, nblk, blk) + bx_ref[d]
            a, b = _lru_coef(uv, pa, px, lam_ref[d])
            a_ref[d] = a
            b_ref[d] = b

    return rowcall(cfg, fn, name, [("row", u, W, 0), ("full", wa), ("full", ba), ("full", wx), ("full", bx), ("full", lam)],
                   [("drow", _sds((2, T, W), f32), W, 0), ("drow", _sds((2, T, W), f32), W, 0)])


def lru_coef_bwd(cfg, u, da, db, wa, ba, wx, bx, lam, name):
    T, W = u.shape
    nblk, blk = wa.shape[1], wa.shape[2]

    def fn(i, u_ref, da_ref, db_ref, wa_ref, ba_ref, wx_ref, bx_ref, lam_ref,
           du_ref, dwa_ref, dba_ref, dwx_ref, dbx_ref, dlam_ref):
        @pl.when(i == 0)
        def _():
            for r in (dwa_ref, dba_ref, dwx_ref, dbx_ref, dlam_ref):
                r[...] = jnp.zeros(r.shape, f32)

        uv = u_ref[...]
        u_bf = uv.astype(bf16)
        du = jnp.zeros(uv.shape, f32)
        for d in range(2):
            pa = _blockdiag(u_bf, wa_ref, d, nblk, blk) + ba_ref[d]
            px = _blockdiag(u_bf, wx_ref, d, nblk, blk) + bx_ref[d]
            _, vjp = jax.vjp(_lru_coef, uv, pa, px, lam_ref[d])
            du_e, dpa, dpx, dlam = vjp((da_ref[d], db_ref[d]))
            du = du + du_e
            dba_ref[d] += jnp.sum(dpa, axis=0, keepdims=True)
            dbx_ref[d] += jnp.sum(dpx, axis=0, keepdims=True)
            dlam_ref[d] += dlam
            parts = []
            for n in range(nblk):
                sl = slice(n * blk, (n + 1) * blk)
                ga, gx = dpa[:, sl].astype(bf16), dpx[:, sl].astype(bf16)
                ub = u_bf[:, sl]
                dwa_ref[d, n] += lax.dot_general(ub, ga, (((0,), (0,)), ((), ())), preferred_element_type=f32)
                dwx_ref[d, n] += lax.dot_general(ub, gx, (((0,), (0,)), ((), ())), preferred_element_type=f32)
                parts.append(
                    lax.dot_general(ga, wa_ref[d, n].astype(bf16), (((1,), (1,)), ((), ())), preferred_element_type=f32)
                    + lax.dot_general(gx, wx_ref[d, n].astype(bf16), (((1,), (1,)), ((), ())), preferred_element_type=f32))
            du = du + jnp.concatenate(parts, axis=1)
        du_ref[...] = du

    return rowcall(cfg, fn, name,
                   [("row", u, W, 0), ("drow", da, W, 0), ("drow", db, W, 0), ("full", wa), ("full", ba), ("full", wx),
                    ("full", bx), ("full", lam)],
                   [("row", _sds((T, W), f32), W, 0), ("full", _sds(wa.shape, f32)), ("full", _sds(ba.shape, f32)),
                    ("full", _sds(wx.shape, f32)), ("full", _sds(bx.shape, f32)), ("full", _sds(lam.shape, f32))])


def _dir_tile(cfg, d, j):
    rev = jnp.where(j < cfg.cT, cfg.cT - 1 - j, cfg.nT - 1 - (j - cfg.cT))
    return jnp.where(d == 0, j, rev)


def lru_scan(cfg, a, b, name):
    _, T, W = a.shape
    TR, nT = cfg.TR, cfg.nT

    def body(a_ref, b_ref, h_ref, hp_ref, st):
        d, j = pl.program_id(0), pl.program_id(1)

        @pl.when(j == 0)
        def _():
            st[...] = jnp.zeros(st.shape, f32)

        def step(t, h):
            idx = t + d * (TR - 1 - 2 * t)
            hn = a_ref[pl.ds(idx, 1), :] * h + b_ref[pl.ds(idx, 1), :]
            hp_ref[pl.ds(idx, 1), :] = h
            h_ref[pl.ds(idx, 1), :] = hn
            return hn

        st[...] = lax.fori_loop(0, TR, step, st[...])

    spec = pl.BlockSpec((None, TR, W), lambda d, j: (d, _dir_tile(cfg, d, j), 0))
    return pl.pallas_call(body, name=name, grid=(2, nT), in_specs=[spec, spec], out_specs=[spec, spec],
                          out_shape=[_sds(a.shape, f32), _sds(a.shape, f32)], scratch_shapes=[pltpu.VMEM((1, W), f32)],
                          compiler_params=_params(("arbitrary", "arbitrary")))(a, b)


def lru_scan_bwd(cfg, a, hp, dh, name):
    _, T, W = a.shape
    TR, nT = cfg.TR, cfg.nT

    def body(a_ref, hp_ref, dh_ref, da_ref, db_ref, st):
        d, j = pl.program_id(0), pl.program_id(1)

        @pl.when(j == 0)
        def _():
            st[...] = jnp.zeros(st.shape, f32)

        def step(t, c):
            p = TR - 1 - t
            idx = p + d * (TR - 1 - 2 * p)
            g = dh_ref[pl.ds(idx, 1), :] + c
            db_ref[pl.ds(idx, 1), :] = g
            da_ref[pl.ds(idx, 1), :] = g * hp_ref[pl.ds(idx, 1), :]
            return a_ref[pl.ds(idx, 1), :] * g

        st[...] = lax.fori_loop(0, TR, step, st[...])

    spec = pl.BlockSpec((None, TR, W), lambda d, j: (d, _dir_tile(cfg, d, nT - 1 - j), 0))
    dspec = pl.BlockSpec((TR, W), lambda d, j: (_dir_tile(cfg, d, nT - 1 - j), 0))
    return pl.pallas_call(body, name=name, grid=(2, nT), in_specs=[spec, spec, dspec], out_specs=[spec, spec],
                          out_shape=[_sds(a.shape, f32), _sds(a.shape, f32)], scratch_shapes=[pltpu.VMEM((1, W), f32)],
                          compiler_params=_params(("arbitrary", "arbitrary")))(a, hp, dh)


def _lru_out_fn(gl, h0, h1):
    return jax.nn.gelu(gl) * (h0 + h1)


def lru_out_fwd(cfg, z, h, W, name):
    def fn(i, g_ref, h_ref, o_ref):
        o_ref[...] = _lru_out_fn(g_ref[...], h_ref[0], h_ref[1]).astype(bf16)

    return rowcall(cfg, fn, name, [("row", z, W, 0), ("drow", h, W, 0)], [("row", _sds((z.shape[0], W), bf16), W, 0)])[0]


def lru_out_bwd(cfg, z, h, dmix, W, name):
    def fn(i, g_ref, h_ref, d_ref, dg_ref, dh_ref):
        _, vjp = jax.vjp(_lru_out_fn, g_ref[...], h_ref[0], h_ref[1])
        dg, dh0, _ = vjp(d_ref[...])
        dg_ref[...] = dg
        dh_ref[...] = dh0

    T = z.shape[0]
    return rowcall(cfg, fn, name, [("row", z, W, 0), ("drow", h, W, 0), ("row", dmix, W, 0)],
                   [("row", _sds((T, W), f32), W, 0), ("row", _sds((T, W), f32), W, 0)])


def _rot_half(x, cos, sin):
    n = x.shape[1] // 2
    x1, x2 = x[:, :n], x[:, n:]
    return jnp.concatenate([x1 * cos - x2 * sin, x1 * sin + x2 * cos], axis=1)


def _dotf(a, b, ca, cb):
    return lax.dot_general(a, b, (((ca,), (cb,)), ((), ())), preferred_element_type=f32)


@functools.partial(jax.custom_vjp, nondiff_argnums=(2, 3))
def _dotb(a, b, ca, cb):
    return _dotf(a.astype(bf16), b.astype(bf16), ca, cb)


def _dotb_fwd(a, b, ca, cb):
    return _dotb(a, b, ca, cb), (a, b)


def _dotb_bwd(ca, cb, res, ct):
    a, b = res
    a16, b16, ct16 = a.astype(bf16), b.astype(bf16), ct.astype(bf16)
    da = _dotf(ct16, b16, 1, 1 - cb) if ca == 1 else _dotf(b16, ct16, 1 - cb, 1)
    db = _dotf(a16, ct16, 1 - ca, 0) if cb == 0 else _dotf(ct16, a16, 0, 1 - ca)
    return da, db


_dotb.defvjp(_dotb_fwd, _dotb_bwd)


def _ret_chunk(d, q, k, v, s, logit, cos, sin):
    C = q.shape[0]
    lg = -_softplus(-logit)
    qr = _rot_half(q, cos, sin)
    kr = _rot_half(k, cos, sin) * (RET_DK ** -0.5)
    ii = lax.broadcasted_iota(jnp.int32, (C, C), 0)
    jj = lax.broadcasted_iota(jnp.int32, (C, C), 1)
    diff = ((ii - jj) if d == 0 else (jj - ii)).astype(f32)
    intra = jnp.where(diff >= 0, jnp.exp(lg * jnp.maximum(diff, 0.0)), 0.0)
    pos = lax.broadcasted_iota(jnp.int32, (C, 1), 0).astype(f32)
    if d == 0:
        q_dec, k_dec = jnp.exp(lg * (pos + 1.0)), jnp.exp(lg * (C - 1.0 - pos))
    else:
        q_dec, k_dec = jnp.exp(lg * (C - pos)), jnp.exp(lg * pos)
    s_dec = jnp.exp(lg * C)
    scores = _dotb(qr, kr, 1, 1) * intra
    o = _dotb(scores, v, 1, 0) + _dotb(qr * q_dec, s, 1, 0)
    s_new = s * s_dec + _dotb(kr * k_dec, v, 0, 0)
    return o, s_new


def _chunk_cfg(cfg):
    f = cfg.TR // RET_CHUNK
    return RowCfg(RET_CHUNK, cfg.nT * f, cfg.cT * f)


def ret_fwd(cfg, z, logit, cos, sin, H, qcol, name):
    T = z.shape[0]
    cc = _chunk_cfg(cfg)
    C, nC = RET_CHUNK, cc.nT
    RV = H * RET_DV
    qb = qcol * RET_DK // RV

    def body(q_ref, k_ref, v_ref, lg_ref, cos_ref, sin_ref, o_ref, s_ref, st):
        d, j = pl.program_id(0), pl.program_id(1)

        @pl.when(j == 0)
        def _():
            st[...] = jnp.zeros(st.shape, f32)

        s_ref[...] = st[...]
        for dd in range(2):
            @pl.when(d == dd)
            def _():
                outs = []
                for h in range(H):
                    cols = slice(h * RET_DK, (h + 1) * RET_DK)
                    o, s_new = _ret_chunk(dd, q_ref[:, cols], k_ref[:, cols], v_ref[:, cols], st[h], lg_ref[h],
                                          cos_ref[...], sin_ref[...])
                    outs.append(o)
                    st[h] = s_new
                o_ref[...] = jnp.concatenate(outs, axis=1)

    tile = lambda d, j: _dir_tile(cc, d, j)
    zq = pl.BlockSpec((C, RV), lambda d, j: (tile(d, j), qb))
    zk = pl.BlockSpec((C, RV), lambda d, j: (tile(d, j), qb + 1))
    zv = pl.BlockSpec((C, RV), lambda d, j: (tile(d, j), qb + 2))
    lgs = pl.BlockSpec((None, H, 1, 1), lambda d, j: (d, 0, 0, 0))
    cs = pl.BlockSpec((C, RET_DK // 2), lambda d, j: (tile(d, j), 0))
    o_spec = pl.BlockSpec((None, C, RV), lambda d, j: (d, tile(d, j), 0))
    s_spec = pl.BlockSpec((None, H, None, RET_DK, RET_DV), lambda d, j: (d, 0, tile(d, j), 0, 0))
    return pl.pallas_call(
        body, name=name, grid=(2, nC), in_specs=[zq, zk, zv, lgs, cs, cs], out_specs=[o_spec, s_spec],
        out_shape=[_sds((2, T, RV), f32), _sds((2, H, nC, RET_DK, RET_DV), f32)],
        scratch_shapes=[pltpu.VMEM((H, RET_DK, RET_DV), f32)],
        compiler_params=_params(("arbitrary", "arbitrary")))(z, z, z, logit, cos, sin)


def ret_bwd(cfg, z, states, do, logit, cos, sin, H, qcol, name):
    T = z.shape[0]
    cc = _chunk_cfg(cfg)
    C, nC = RET_CHUNK, cc.nT
    RV = H * RET_DV
    qb = qcol * RET_DK // RV

    def body(q_ref, k_ref, v_ref, s_ref, do_ref, lg_ref, cos_ref, sin_ref, dq_ref, dk_ref, dv_ref, dlg_ref, st):
        d, j = pl.program_id(0), pl.program_id(1)

        @pl.when(j == 0)
        def _():
            st[...] = jnp.zeros(st.shape, f32)
            dlg_ref[...] = jnp.zeros(dlg_ref.shape, f32)

        for dd in range(2):
            @pl.when(d == dd)
            def _():
                fn = lambda q, k, v, s, lg: _ret_chunk(dd, q, k, v, s, lg, cos_ref[...], sin_ref[...])
                dqs, dks, dvs = [], [], []
                for h in range(H):
                    cols = slice(h * RET_DK, (h + 1) * RET_DK)
                    _, vjp = jax.vjp(fn, q_ref[:, cols], k_ref[:, cols], v_ref[:, cols], s_ref[h], lg_ref[h])
                    dq, dk, dv, ds, dlg = vjp((do_ref[:, cols], st[h]))
                    dqs.append(dq)
                    dks.append(dk)
                    dvs.append(dv)
                    st[h] = ds
                    dlg_ref[h] += dlg
                dq_ref[...] = jnp.concatenate(dqs, axis=1)
                dk_ref[...] = jnp.concatenate(dks, axis=1)
                dv_ref[...] = jnp.concatenate(dvs, axis=1)

    tile = lambda d, j: _dir_tile(cc, d, nC - 1 - j)
    zq = pl.BlockSpec((C, RV), lambda d, j: (tile(d, j), qb))
    zk = pl.BlockSpec((C, RV), lambda d, j: (tile(d, j), qb + 1))
    zv = pl.BlockSpec((C, RV), lambda d, j: (tile(d, j), qb + 2))
    s_spec = pl.BlockSpec((None, H, None, RET_DK, RET_DV), lambda d, j: (d, 0, tile(d, j), 0, 0))
    do_spec = pl.BlockSpec((C, RV), lambda d, j: (tile(d, j), 0))
    lgs = pl.BlockSpec((None, H, 1, 1), lambda d, j: (d, 0, 0, 0))
    cs = pl.BlockSpec((C, RET_DK // 2), lambda d, j: (tile(d, j), 0))
    g_spec = pl.BlockSpec((None, C, RV), lambda d, j: (d, tile(d, j), 0))
    gshape = _sds((2, T, RV), f32)
    return pl.pallas_call(
        body, name=name, grid=(2, nC), in_specs=[zq, zk, zv, s_spec, do_spec, lgs, cs, cs],
        out_specs=[g_spec, g_spec, g_spec, lgs], out_shape=[gshape, gshape, gshape, _sds((2, H, 1, 1), f32)],
        scratch_shapes=[pltpu.VMEM((H, RET_DK, RET_DV), f32)],
        compiler_params=_params(("arbitrary", "arbitrary")))(z, z, z, states, do, logit, cos, sin)


def _ret_norm_fn(H, o0, o1, ol, gn):
    o = o0 + o1
    parts = []
    for h in range(H):
        x = o[:, h * RET_DV:(h + 1) * RET_DV]
        mu = jnp.mean(x, axis=-1, keepdims=True)
        var = jnp.mean(jnp.square(x - mu), axis=-1, keepdims=True)
        parts.append((x - mu) * lax.rsqrt(var + EPS))
    return (jnp.concatenate(parts, axis=1) * gn) * jax.nn.silu(ol)


def ret_norm_fwd(cfg, o, z, gn, H, olcol, name):
    RV = H * RET_DV

    def fn(i, o_ref, ol_ref, gn_ref, r_ref):
        r_ref[...] = _ret_norm_fn(H, o_ref[0], o_ref[1], ol_ref[...], gn_ref[...]).astype(bf16)

    return rowcall(cfg, fn, name, [("drow", o, RV, 0), ("row", z, RV, olcol), ("full", gn)],
                   [("row", _sds((z.shape[0], RV), bf16), RV, 0)])[0]


def ret_norm_bwd(cfg, o, z, gn, dmix, H, olcol, dcol, name):
    RV = H * RET_DV
    T = z.shape[0]

    def fn(i, o_ref, ol_ref, gn_ref, d_ref, do_ref, dol_ref, dgn_ref):
        _, vjp = jax.vjp(functools.partial(_ret_norm_fn, H), o_ref[0], o_ref[1], ol_ref[...], gn_ref[...])
        do, _, dol, dgn = vjp(d_ref[...])
        do_ref[...] = do
        dol_ref[...] = dol
        _acc(dgn_ref, dgn, i == 0)

    return rowcall(cfg, fn, name, [("drow", o, RV, 0), ("row", z, RV, olcol), ("full", gn), ("row", dmix, RV, dcol)],
                   [("row", _sds((T, RV), f32), RV, 0), ("row", _sds((T, RV), f32), RV, 0), ("full", _sds((1, RV), f32))])


def _pool_geom(cfg, i, w, L):
    t = (i - cfg.cT) * cfg.TR + lax.broadcasted_iota(jnp.int32, (cfg.TR, 1), 0)
    lo = jnp.clip(t - w // 2, 0, L)
    hi = jnp.clip(t + w // 2, 0, L)
    return (hi - lo).astype(f32)


def _pool_centred(cfg, i, buf, gi, w, L):
    TR, G = cfg.TR, POOL_GROUP
    cols = pl.ds(gi * G, G)
    tot = buf[pl.ds(HALO - w // 2, TR), cols]
    for s in range(-w // 2 + 1, w // 2):
        tot = tot + buf[pl.ds(HALO + s, TR), cols]
    cnt = _pool_geom(cfg, i, w, L)
    return tot / cnt - buf[pl.ds(HALO, TR), cols], cnt


def pool_fwd(cfg, z, pw, ps, name):
    T = z.shape[0]
    TR, cT = cfg.TR, cfg.cT
    P = POOL_GROUP * len(POOL_WINDOWS)
    L = T - cT * TR

    def fn(i, x_ref, xp_ref, xn_ref, pw_ref, ps_ref, o_ref, buf):
        @pl.when(i < cT)
        def _():
            o_ref[...] = jnp.zeros(o_ref.shape, bf16)

        @pl.when(i >= cT)
        def _():
            start, end = _seg_flags(cfg, i)
            _fill_halo(buf, x_ref[...], xp_ref[...], xn_ref[...], start, end, TR)
            outs = []
            for gi, w in enumerate(POOL_WINDOWS):
                m, _ = _pool_centred(cfg, i, buf, gi, w, L)
                outs.append(jnp.dot(m.astype(bf16), pw_ref[gi].astype(bf16), preferred_element_type=f32))
            o_ref[...] = (jnp.concatenate(outs, axis=1) * ps_ref[...]).astype(bf16)

    return rowcall(cfg, fn, name, [("row", z, P, 0), ("prev", z, P, 0), ("next", z, P, 0), ("full", pw), ("full", ps)],
                   [("row", _sds((T, P), bf16), P, 0)], scratch=[pltpu.VMEM((TR + 2 * HALO, P), f32)])[0]


def pool_bwd_a(cfg, z, dmix, pw, ps, name):
    T = z.shape[0]
    TR, cT = cfg.TR, cfg.cT
    G = POOL_GROUP
    P = G * len(POOL_WINDOWS)
    L = T - cT * TR

    def fn(i, x_ref, xp_ref, xn_ref, d_ref, pw_ref, ps_ref, dm_ref, dmn_ref, dpw_ref, dps_ref, buf):
        @pl.when(i == 0)
        def _():
            dpw_ref[...] = jnp.zeros(dpw_ref.shape, f32)
            dps_ref[...] = jnp.zeros(dps_ref.shape, f32)

        @pl.when(i < cT)
        def _():
            dm_ref[...] = jnp.zeros(dm_ref.shape, f32)
            dmn_ref[...] = jnp.zeros(dmn_ref.shape, f32)

        @pl.when(i >= cT)
        def _():
            start, end = _seg_flags(cfg, i)
            _fill_halo(buf, x_ref[...], xp_ref[...], xn_ref[...], start, end, TR)
            dout = d_ref[...]
            dpre = dout * ps_ref[...]
            pres, dms, dmns = [], [], []
            for gi, w in enumerate(POOL_WINDOWS):
                m, cnt = _pool_centred(cfg, i, buf, gi, w, L)
                m_bf = m.astype(bf16)
                w_bf = pw_ref[gi].astype(bf16)
                pres.append(jnp.dot(m_bf, w_bf, preferred_element_type=f32))
                g_bf = dpre[:, gi * G:(gi + 1) * G].astype(bf16)
                dpw_ref[gi] += _dotf(m_bf, g_bf, 0, 0)
                dm = _dotf(g_bf, w_bf, 1, 1)
                dms.append(dm)
                dmns.append(dm / cnt)
            dps_ref[...] += jnp.sum(dout * jnp.concatenate(pres, axis=1), axis=0, keepdims=True)
            dm_ref[...] = jnp.concatenate(dms, axis=1)
            dmn_ref[...] = jnp.concatenate(dmns, axis=1)

    return rowcall(cfg, fn, name,
                   [("row", z, P, 0), ("prev", z, P, 0), ("next", z, P, 0), ("row", dmix, P, 0), ("full", pw), ("full", ps)],
                   [("row", _sds((T, P), f32), P, 0), ("row", _sds((T, P), f32), P, 0), ("full", _sds(pw.shape, f32)),
                    ("full", _sds((1, P), f32))], scratch=[pltpu.VMEM((TR + 2 * HALO, P), f32)])


def pool_bwd_b(cfg, dm, dmn, name):
    T, P = dm.shape
    TR, cT = cfg.TR, cfg.cT
    G = POOL_GROUP

    def fn(i, dm_ref, c_ref, p_ref, n_ref, dx_ref, buf):
        start, end = _seg_flags(cfg, i)
        _fill_halo(buf, c_ref[...], p_ref[...], n_ref[...], start, end, TR)
        outs = []
        for gi, w in enumerate(POOL_WINDOWS):
            cols = pl.ds(gi * G, G)
            tot = buf[pl.ds(HALO + w // 2, TR), cols]
            for s in range(-w // 2 + 1, w // 2):
                tot = tot + buf[pl.ds(HALO + s, TR), cols]
            outs.append(tot)
        dx_ref[...] = jnp.concatenate(outs, axis=1) - dm_ref[...]

    return rowcall(cfg, fn, name, [("row", dm, P, 0), ("row", dmn, P, 0), ("prev", dmn, P, 0), ("next", dmn, P, 0)],
                   [("row", _sds((T, P), f32), P, 0)], scratch=[pltpu.VMEM((TR + 2 * HALO, P), f32)])[0]


def _swap_halves(x):
    return pltpu.roll(x, HEAD_DIM // 2, 1)


def _headnorm(x, g):
    return _rms(x) * g


def att_prep(cfg, z, qg, kg, cosf, sinf, nq, name):
    T = z.shape[0]
    U = z.shape[1] // (nq + 3)
    nh = U // HEAD_DIM

    def fn(i, *refs):
        q_refs = refs[:nq]
        k_ref, v_ref, qg_ref, kg_ref, cos_ref, sin_ref, qn_ref, kn_ref, vb_ref = refs[nq:]
        cosv, sinv = cos_ref[...], sin_ref[...]

        def heads(x, g):
            outs = []
            for h in range(nh):
                y = _headnorm(x[:, h * HEAD_DIM:(h + 1) * HEAD_DIM], g)
                outs.append(y * cosv + _swap_halves(y) * sinv)
            return jnp.concatenate(outs, axis=1)

        qn_ref[...] = jnp.concatenate([heads(r[...], qg_ref[...]) for r in q_refs], axis=1).astype(bf16)
        kn_ref[...] = heads(k_ref[...], kg_ref[...]).astype(bf16)
        vb_ref[...] = v_ref[...].astype(bf16)

    ins = [("row", z, U, 1 + n) for n in range(nq)] + [("row", z, U, nq + 1), ("row", z, U, nq + 2), ("full", qg),
                                                       ("full", kg), ("row", cosf, HEAD_DIM, 0), ("row", sinf, HEAD_DIM, 0)]
    return rowcall(cfg, fn, name, ins, [("row", _sds((T, nq * U), bf16), nq * U, 0), ("row", _sds((T, U), bf16), U, 0),
                                        ("row", _sds((T, U), bf16), U, 0)])


def att_prep_bwd(cfg, z, qg, kg, cosf, sinf, dqn, dkn, dvb, dxpool, nq, name):
    T = z.shape[0]
    U = z.shape[1] // (nq + 3)
    nh = U // HEAD_DIM
    cT = cfg.cT

    def fn(i, *refs):
        q_refs = refs[:nq]
        (k_ref, qg_ref, kg_ref, cos_ref, sin_ref, dqn_ref, dkn_ref, dvb_ref, dxp_ref, dz_ref, dqg_ref, dkg_ref) = refs[nq:]
        cosv, sinv = cos_ref[...], sin_ref[...]

        @pl.when(i == 0)
        def _():
            dqg_ref[...] = jnp.zeros(dqg_ref.shape, f32)
            dkg_ref[...] = jnp.zeros(dkg_ref.shape, f32)

        def heads_bwd(x, g, dy, dg_ref):
            outs = []
            for h in range(nh):
                sl = slice(h * HEAD_DIM, (h + 1) * HEAD_DIM)
                d = dy[:, sl]
                dn = d * cosv + _swap_halves(d * sinv)
                _, vjp = jax.vjp(_headnorm, x[:, sl], g)
                dx, dg = vjp(dn)
                dg_ref[...] += dg
                outs.append(dx)
            return jnp.concatenate(outs, axis=1)

        dk = heads_bwd(k_ref[...], kg_ref[...], dkn_ref[...], dkg_ref)
        tail = [dk.astype(bf16), dvb_ref[...].astype(bf16)]

        @pl.when(i < cT)
        def _():
            zeros = jnp.zeros((cfg.TR, (nq + 1) * U), bf16)
            dz_ref[...] = jnp.concatenate([zeros] + tail, axis=1)

        @pl.when(i >= cT)
        def _():
            dq = [heads_bwd(r[...], qg_ref[...], dqn_ref[:, n * U:(n + 1) * U], dqg_ref) for n, r in enumerate(q_refs)]
            dz_ref[...] = jnp.concatenate([dxp_ref[...].astype(bf16)] + [t.astype(bf16) for t in dq] + tail, axis=1)

    ins = ([("row", z, U, 1 + n) for n in range(nq)] +
           [("row", z, U, nq + 1), ("full", qg), ("full", kg), ("row", cosf, HEAD_DIM, 0), ("row", sinf, HEAD_DIM, 0),
            ("row", dqn, nq * U, 0), ("row", dkn, U, 0), ("row", dvb, U, 0), ("row", dxpool, U, 0)])
    W = (nq + 3) * U
    return rowcall(cfg, fn, name, ins, [("row", _sds((T, W), bf16), W, 0), ("full", _sds((1, HEAD_DIM), f32)),
                                        ("full", _sds((1, HEAD_DIM), f32))])


def _stack_heads(x, n):
    return jnp.concatenate([x[:, h * HEAD_DIM:(h + 1) * HEAD_DIM] for h in range(n)], axis=0)


def _unstack_heads(x, n):
    rows = x.shape[0] // n
    return jnp.concatenate([x[h * rows:(h + 1) * rows] for h in range(n)], axis=1)


def _att_tiles(cfg, T):
    tq = cfg.TR
    tk = _pick(T, (4224, 2816, 1408, 768, 512, 256, 128))
    return tq, tk, (T - cfg.cT * cfg.TR) // tq, T // tk


LOG2E = 1.4426950408889634


def att_fwd(cfg, qn, kn, vb, nq, name):
    T, U = kn.shape
    KV = U // HEAD_DIM
    tq, tk, nQ, nK = _att_tiles(cfg, T)
    scale = HEAD_DIM ** -0.5
    c2 = scale * LOG2E
    R = nq * tq

    def body(q_ref, k_ref, v_ref, o_ref, lse_ref, *scratch):
        ik = pl.program_id(2)
        m_sc, l_sc, acc = scratch[:nq], scratch[nq:2 * nq], scratch[2 * nq:]

        @pl.when(ik == 0)
        def _():
            for h in range(nq):
                m_sc[h][...] = jnp.full(m_sc[h].shape, -jnp.inf, f32)
                l_sc[h][...] = jnp.zeros(l_sc[h].shape, f32)
                acc[h][...] = jnp.zeros(acc[h].shape, f32)

        k, v = k_ref[...], v_ref[...]
        for h in range(nq):
            s = _dotf(q_ref[:, h * HEAD_DIM:(h + 1) * HEAD_DIM], k, 1, 1)
            m_old = m_sc[h][...]
            m_new = jnp.maximum(m_old, jnp.max(s, axis=-1, keepdims=True))
            alpha = jnp.exp2((m_old - m_new) * c2)
            p = jnp.exp2((s - m_new) * c2)
            l_sc[h][...] = alpha * l_sc[h][...] + jnp.sum(p, axis=-1, keepdims=True)
            acc[h][...] = alpha * acc[h][...] + jnp.dot(p.astype(bf16), v, preferred_element_type=f32)
            m_sc[h][...] = m_new

        @pl.when(ik == nK - 1)
        def _():
            o_ref[...] = jnp.concatenate([acc[h][...] / l_sc[h][...] for h in range(nq)], axis=1)
            lse_ref[...] = jnp.concatenate([m_sc[h][...] * scale + jnp.log(l_sc[h][...]) for h in range(nq)], axis=0)

    W = nq * HEAD_DIM
    q_spec = pl.BlockSpec((tq, W), lambda h, i, k: (i + cfg.cT, h))
    kv_spec = pl.BlockSpec((tk, HEAD_DIM), lambda h, i, k: (k, h))
    lse_spec = pl.BlockSpec((None, None, R, 1), lambda h, i, k: (h, i, 0, 0))
    col = [pltpu.VMEM((tq, 1), f32)] * nq
    return pl.pallas_call(
        body, name=name, grid=(KV, nQ, nK), in_specs=[q_spec, kv_spec, kv_spec], out_specs=[q_spec, lse_spec],
        out_shape=[_sds((T, nq * U), f32), _sds((KV, nQ, R, 1), f32)],
        scratch_shapes=col + col + [pltpu.VMEM((tq, HEAD_DIM), f32)] * nq,
        compiler_params=_params(("arbitrary", "arbitrary", "arbitrary")))(qn, kn, vb)


def att_bwd_dq(cfg, qn, kn, vb, o, lse, do, nq, name):
    T, U = kn.shape
    KV = U // HEAD_DIM
    tq, tk, nQ, nK = _att_tiles(cfg, T)
    scale = HEAD_DIM ** -0.5
    c2 = scale * LOG2E
    R = nq * tq
    W = nq * HEAD_DIM

    def body(q_ref, k_ref, v_ref, o_ref, lse_ref, do_ref, dq_ref, acc, dl):
        ik = pl.program_id(2)

        @pl.when(ik == 0)
        def _():
            acc[...] = jnp.zeros(acc.shape, f32)
            dl[...] = jnp.sum(_stack_heads(do_ref[...] * o_ref[...], nq), axis=-1, keepdims=True)

        k, v = k_ref[...], v_ref[...]
        for h in range(nq):
            rows = pl.ds(h * tq, tq)
            cols = slice(h * HEAD_DIM, (h + 1) * HEAD_DIM)
            s = _dotf(q_ref[:, cols], k, 1, 1)
            p = jnp.exp2(s * c2 - lse_ref[rows, :] * LOG2E)
            dp = _dotf(do_ref[:, cols].astype(bf16), v, 1, 1)
            ds = (p * (dp - dl[rows, :]) * scale).astype(bf16)
            acc[rows, :] += jnp.dot(ds, k, preferred_element_type=f32)

        @pl.when(ik == nK - 1)
        def _():
            dq_ref[...] = _unstack_heads(acc[...], nq)

    q_spec = pl.BlockSpec((tq, W), lambda h, i, k: (i + cfg.cT, h))
    kv_spec = pl.BlockSpec((tk, HEAD_DIM), lambda h, i, k: (k, h))
    lse_spec = pl.BlockSpec((None, None, R, 1), lambda h, i, k: (h, i, 0, 0))
    return pl.pallas_call(
        body, name=name, grid=(KV, nQ, nK), in_specs=[q_spec, kv_spec, kv_spec, q_spec, lse_spec, q_spec], out_specs=q_spec,
        out_shape=_sds((T, nq * U), f32), scratch_shapes=[pltpu.VMEM((R, HEAD_DIM), f32), pltpu.VMEM((R, 1), f32)],
        compiler_params=_params(("arbitrary", "arbitrary", "arbitrary")))(qn, kn, vb, o, lse, do)


def att_bwd_dkv(cfg, qn, kn, vb, o, lse, do, nq, name):
    T, U = kn.shape
    KV = U // HEAD_DIM
    tq, tk, nQ, nK = _att_tiles(cfg, T)
    scale = HEAD_DIM ** -0.5
    c2 = scale * LOG2E
    R = nq * tq
    W = nq * HEAD_DIM

    def body(q_ref, k_ref, v_ref, o_ref, lse_ref, do_ref, dk_ref, dv_ref, dk_acc, dv_acc):
        iq = pl.program_id(2)

        @pl.when(iq == 0)
        def _():
            dk_acc[...] = jnp.zeros(dk_acc.shape, f32)
            dv_acc[...] = jnp.zeros(dv_acc.shape, f32)

        k, v = k_ref[...], v_ref[...]
        for h in range(nq):
            rows = pl.ds(h * tq, tq)
            cols = slice(h * HEAD_DIM, (h + 1) * HEAD_DIM)
            qh = q_ref[:, cols]
            doh = do_ref[:, cols]
            dl = jnp.sum(doh * o_ref[:, cols], axis=-1, keepdims=True)
            p = jnp.exp2(_dotf(qh, k, 1, 1) * c2 - lse_ref[rows, :] * LOG2E)
            do_bf = doh.astype(bf16)
            dv_acc[...] += _dotf(p.astype(bf16), do_bf, 0, 0)
            dp = _dotf(do_bf, v, 1, 1)
            ds = (p * (dp - dl) * scale).astype(bf16)
            dk_acc[...] += _dotf(ds, qh, 0, 0)

        @pl.when(iq == nQ - 1)
        def _():
            dk_ref[...] = dk_acc[...]
            dv_ref[...] = dv_acc[...]

    q_spec = pl.BlockSpec((tq, W), lambda h, k, i: (i + cfg.cT, h))
    kv_spec = pl.BlockSpec((tk, HEAD_DIM), lambda h, k, i: (k, h))
    lse_spec = pl.BlockSpec((None, None, R, 1), lambda h, k, i: (h, i, 0, 0))
    return pl.pallas_call(
        body, name=name, grid=(KV, nK, nQ), in_specs=[q_spec, kv_spec, kv_spec, q_spec, lse_spec, q_spec],
        out_specs=[kv_spec, kv_spec], out_shape=[_sds((T, U), f32), _sds((T, U), f32)],
        scratch_shapes=[pltpu.VMEM((tk, HEAD_DIM), f32), pltpu.VMEM((tk, HEAD_DIM), f32)],
        compiler_params=_params(("arbitrary", "arbitrary", "arbitrary")))(qn, kn, vb, o, lse, do)


def att_bwd(cfg, qn, kn, vb, o, lse, do, nq, name, tk_prefs=(4224, 2816, 1408, 768, 512, 256, 128)):
    T, U = kn.shape
    KV = U // HEAD_DIM
    tq = cfg.TR
    tk = _pick(T, tk_prefs)
    nQ, nK = (T - cfg.cT * cfg.TR) // tq, T // tk
    scale = HEAD_DIM ** -0.5
    c2 = scale * LOG2E
    R = nq * tq
    W = nq * HEAD_DIM

    def body(q_ref, k_ref, v_ref, o_ref, lse_ref, do_ref, dq_ref, dk_ref, dv_ref, acc, dl):
        iq, ik = pl.program_id(1), pl.program_id(2)

        @pl.when(jnp.logical_and(iq == 0, ik == 0))
        def _():
            dk_ref[...] = jnp.zeros(dk_ref.shape, f32)
            dv_ref[...] = jnp.zeros(dv_ref.shape, f32)

        @pl.when(ik == 0)
        def _():
            acc[...] = jnp.zeros(acc.shape, f32)
            dl[...] = jnp.sum(_stack_heads(do_ref[...] * o_ref[...], nq), axis=-1, keepdims=True)

        k, v = k_ref[...], v_ref[...]
        krows = pl.ds(pl.multiple_of(ik * tk, tk), tk)
        for h in range(nq):
            rows = pl.ds(h * tq, tq)
            cols = slice(h * HEAD_DIM, (h + 1) * HEAD_DIM)
            qh = q_ref[:, cols]
            p = jnp.exp2(_dotf(qh, k, 1, 1) * c2 - lse_ref[rows, :] * LOG2E)
            do_bf = do_ref[:, cols].astype(bf16)
            dp = _dotf(do_bf, v, 1, 1)
            ds = (p * (dp - dl[rows, :]) * scale).astype(bf16)
            acc[rows, :] += jnp.dot(ds, k, preferred_element_type=f32)
            dv_ref[krows, :] += _dotf(p.astype(bf16), do_bf, 0, 0)
            dk_ref[krows, :] += _dotf(ds, qh, 0, 0)

        @pl.when(ik == nK - 1)
        def _():
            dq_ref[...] = _unstack_heads(acc[...], nq)

    q_spec = pl.BlockSpec((tq, W), lambda h, i, k: (i + cfg.cT, h))
    kv_spec = pl.BlockSpec((tk, HEAD_DIM), lambda h, i, k: (k, h))
    lse_spec = pl.BlockSpec((None, None, R, 1), lambda h, i, k: (h, i, 0, 0))
    head_spec = pl.BlockSpec((T, HEAD_DIM), lambda h, i, k: (0, h))
    return pl.pallas_call(
        body, name=name, grid=(KV, nQ, nK), in_specs=[q_spec, kv_spec, kv_spec, q_spec, lse_spec, q_spec],
        out_specs=[q_spec, head_spec, head_spec], out_shape=[_sds((T, nq * U), f32), _sds((T, U), f32), _sds((T, U), f32)],
        scratch_shapes=[pltpu.VMEM((R, HEAD_DIM), f32), pltpu.VMEM((R, 1), f32)],
        compiler_params=_params(("arbitrary", "arbitrary", "arbitrary")))(qn, kn, vb, o, lse, do)


def od_mix(cfg, pooled, o, name):
    T, P = pooled.shape
    QW = o.shape[1]
    cT = cfg.cT

    def fn(i, p_ref, o_ref, m_ref):
        @pl.when(i < cT)
        def _():
            m_ref[...] = jnp.zeros(m_ref.shape, bf16)

        @pl.when(i >= cT)
        def _():
            m_ref[...] = jnp.concatenate([p_ref[...], o_ref[...].astype(bf16)], axis=1)

    return rowcall(cfg, fn, name, [("row", pooled, P, 0), ("row", o, QW, 0)], [("row", _sds((T, P + QW), bf16), P + QW, 0)])[0]


def ev_mix(cfg, lru, ret, name):
    T, W = lru.shape
    RV = ret.shape[1]

    def fn(i, a_ref, b_ref, m_ref):
        m_ref[...] = jnp.concatenate([a_ref[...], b_ref[...]], axis=1)

    return rowcall(cfg, fn, name, [("row", lru, W, 0), ("row", ret, RV, 0)], [("row", _sds((T, W + RV), bf16), W + RV, 0)])[0]


def ev_dz_pack(cfg, dgl, dr, dq, dk, dv, dol, name):
    T, W = dgl.shape
    RV = dol.shape[1]
    width = 2 * W + 4 * RV

    def fn(i, g_ref, r_ref, q_ref, k_ref, v_ref, o_ref, dz_ref):
        parts = [g_ref[...], r_ref[...], q_ref[0] + q_ref[1], k_ref[0] + k_ref[1], v_ref[0] + v_ref[1], o_ref[...]]
        dz_ref[...] = jnp.concatenate([p.astype(bf16) for p in parts], axis=1)

    return rowcall(cfg, fn, name, [("row", dgl, W, 0), ("row", dr, W, 0), ("drow", dq, RV, 0), ("drow", dk, RV, 0),
                                   ("drow", dv, RV, 0), ("row", dol, RV, 0)], [("row", _sds((T, width), bf16), width, 0)])[0]


def loss_fwd_bwd(cfg, xf, target, name):
    T, D = xf.shape
    TR, cT = cfg.TR, cfg.cT

    def body(x_ref, t_ref, sq_ref, dx_ref):
        i = pl.program_id(0)

        @pl.when(i == 0)
        def _():
            sq_ref[...] = jnp.zeros(sq_ref.shape, f32)

        @pl.when(i < cT)
        def _():
            dx_ref[...] = jnp.zeros(dx_ref.shape, f32)

        @pl.when(i >= cT)
        def _():
            diff = x_ref[...] - t_ref[...]
            sq_ref[...] += jnp.sum(diff * diff, axis=0, keepdims=True)
            dx_ref[...] = diff / D

    row = pl.BlockSpec((TR, D), lambda i: (i, 0))
    trow = pl.BlockSpec((TR, D), lambda i: (jnp.maximum(i - cT, 0), 0))
    return pl.pallas_call(body, name=name, grid=(cfg.nT,), in_specs=[row, trow],
                          out_specs=[pl.BlockSpec((1, D), lambda i: (0, 0)), row],
                          out_shape=[_sds((1, D), f32), _sds((T, D), f32)], compiler_params=_params(("arbitrary",)))(xf, target)


MOD_ROWS = 16


def mod_fwd(s16, mod_w, name):
    nL, D, C4 = mod_w.shape
    tc = _pick(C4, (512, 256, 128))

    def body(s_ref, w_ref, o_ref):
        o_ref[...] = jnp.dot(s_ref[...], w_ref[...], precision=lax.Precision.HIGHEST, preferred_element_type=f32)

    return pl.pallas_call(
        body, name=name, grid=(nL, C4 // tc),
        in_specs=[pl.BlockSpec((MOD_ROWS, D), lambda l, j: (0, 0)), pl.BlockSpec((None, D, tc), lambda l, j: (l, 0, j))],
        out_specs=pl.BlockSpec((None, MOD_ROWS, tc), lambda l, j: (l, 0, j)), out_shape=_sds((nL, MOD_ROWS, C4), f32),
        compiler_params=_params(("arbitrary", "arbitrary")))(s16, mod_w)


def mod_bwd(s16, dm16, mod_w, name):
    nL, D, C4 = mod_w.shape
    tc = _pick(C4, (512, 256, 128))
    half = MOD_ROWS // 2

    def body(s_ref, d_ref, w_ref, g_ref, dc_ref):
        first = jnp.logical_and(pl.program_id(0) == 0, pl.program_id(1) == 0)
        g_ref[...] = lax.dot_general(s_ref[...], d_ref[...], (((0,), (0,)), ((), ())), precision=lax.Precision.HIGHEST,
                                     preferred_element_type=f32)
        part = lax.dot_general(d_ref[...], w_ref[...], (((1,), (1,)), ((), ())), precision=lax.Precision.HIGHEST,
                               preferred_element_type=f32)
        _acc(dc_ref, jnp.sum(part[half:], axis=0, keepdims=True), first)

    return pl.pallas_call(
        body, name=name, grid=(nL, C4 // tc),
        in_specs=[pl.BlockSpec((MOD_ROWS, D), lambda l, j: (0, 0)), pl.BlockSpec((None, MOD_ROWS, tc), lambda l, j: (l, 0, j)),
                  pl.BlockSpec((None, D, tc), lambda l, j: (l, 0, j))],
        out_specs=[pl.BlockSpec((None, D, tc), lambda l, j: (l, 0, j)), pl.BlockSpec((1, D), lambda l, j: (0, 0))],
        out_shape=[_sds((nL, D, C4), f32), _sds((1, D), f32)],
        compiler_params=_params(("arbitrary", "arbitrary")))(s16, dm16, mod_w)


def _as2d(a):
    return a.reshape(-1, a.shape[-1])


ELEMENTWISE_VMEM = 24 * 1024 * 1024


def _tiles2d(shape, n_arrays):
    R, C = shape
    tc = _pick(C, (1536, 1408, 1024, 768, 512, 256, 128))
    fits = [t for t in (512, 256, 128, 64, 32, 16, 8) if R % t == 0 and t * tc * 4 * 2 * n_arrays <= ELEMENTWISE_VMEM]
    return (fits[0] if fits else R), tc


def cast_bf16(a, name):
    a2 = _as2d(a)
    tr, tc = _tiles2d(a2.shape, 2)

    def body(a_ref, o_ref):
        o_ref[...] = a_ref[...].astype(bf16)

    spec = pl.BlockSpec((tr, tc), lambda i, j: (i, j))
    out = pl.pallas_call(body, name=name, grid=(a2.shape[0] // tr, a2.shape[1] // tc), in_specs=[spec], out_specs=spec,
                         out_shape=_sds(a2.shape, bf16), compiler_params=_params(("arbitrary", "arbitrary")))(a2)
    return out.reshape(a.shape)


def sum_leading(a, name, *, into=None, full_shape=None, widx=()):
    n = a.shape[0]
    a3 = a.reshape(n, -1, a.shape[-1])
    tr, tc = _tiles2d(a3.shape[1:], n + 1)

    def body(a_ref, *rest):
        o_ref = rest[-1]
        tot = a_ref[0].astype(f32)
        for k in range(1, n):
            tot = tot + a_ref[k].astype(f32)
        o_ref[...] = tot

    grid = (a3.shape[1] // tr, a3.shape[2] // tc)
    in_specs = [pl.BlockSpec((n, tr, tc), lambda i, j: (0, i, j))]
    args = [a3]
    if not widx:
        out = pl.pallas_call(body, name=name, grid=grid, in_specs=in_specs,
                             out_specs=pl.BlockSpec((tr, tc), lambda i, j: (i, j)), out_shape=_sds(a3.shape[1:], f32),
                             compiler_params=_params(("arbitrary", "arbitrary")))(*args)
        return out.reshape(a.shape[1:])
    lead = tuple(full_shape[:len(widx)])
    flat = lead + tuple(a3.shape[1:])
    aliases = {}
    if into is not None:
        in_specs.append(pl.BlockSpec(memory_space=pl.ANY))
        args.append(into.reshape(flat))
        aliases = {1: 0}
    out = pl.pallas_call(body, name=name, grid=grid, in_specs=in_specs,
                         out_specs=pl.BlockSpec((None,) * len(widx) + (tr, tc), lambda i, j: tuple(widx) + (i, j)),
                         out_shape=_sds(flat, f32), input_output_aliases=aliases,
                         compiler_params=_params(("arbitrary", "arbitrary")))(*args)
    return out.reshape(full_shape)


def adamw(w, m, v, g_parts, name):
    w2, m2, v2 = _as2d(w), _as2d(m), _as2d(v)
    parts = [_as2d(p) for p in g_parts]
    tr, tc = _tiles2d(w2.shape, 7 + len(parts))
    npart = len(parts)

    def body(*refs):
        w_ref, m_ref, v_ref = refs[:3]
        p_refs = refs[3:3 + npart]
        g_ref, d_ref, nm_ref, nv_ref = refs[3 + npart:]
        g = p_refs[0][...]
        for p in p_refs[1:]:
            g = g + p[...]
        mn = ADAM_B1 * m_ref[...] + (1.0 - ADAM_B1) * g
        vn = ADAM_B2 * v_ref[...] + (1.0 - ADAM_B2) * jnp.square(g)
        m_hat = mn / (1.0 - ADAM_B1 ** ADAM_STEP)
        v_hat = vn / (1.0 - ADAM_B2 ** ADAM_STEP)
        g_ref[...] = g
        d_ref[...] = -ADAM_LR * (m_hat / (jnp.sqrt(v_hat) + ADAM_EPS) + ADAM_WD * w_ref[...])
        nm_ref[...] = mn
        nv_ref[...] = vn

    spec = pl.BlockSpec((tr, tc), lambda i, j: (i, j))
    outs = pl.pallas_call(body, name=name, grid=(w2.shape[0] // tr, w2.shape[1] // tc), in_specs=[spec] * (3 + npart),
                          out_specs=[spec] * 4, out_shape=[_sds(w2.shape, f32)] * 4,
                          compiler_params=_params(("arbitrary", "arbitrary")))(w2, m2, v2, *parts)
    return [o.reshape(w.shape) for o in outs]


def all_gather_small(a, name):
    R, C = a.shape

    def body(a_ref, out_ref, send_sems, recv_sems, local_sem):
        x, y, c = _coords()
        me = 4 * x + 2 * y + c
        mine = pltpu.make_async_copy(a_ref, out_ref.at[me], local_sem)
        mine.start()
        copies = []
        for k in range(1, N_DEV):
            kx, ky, kc = (k >> 2) & 1, (k >> 1) & 1, k & 1
            peer = (_flip(x, kx), _flip(y, ky), _flip(c, kc))
            cp = pltpu.make_async_remote_copy(src_ref=a_ref, dst_ref=out_ref.at[me], send_sem=send_sems.at[k - 1],
                                              recv_sem=recv_sems.at[k - 1], device_id=peer, device_id_type=MESH)
            cp.start()
            copies.append((cp, 4 * peer[0] + 2 * peer[1] + peer[2], peer))
        for k, (cp, pidx, peer) in enumerate(copies):
            pltpu.make_async_remote_copy(src_ref=a_ref, dst_ref=out_ref.at[pidx], send_sem=send_sems.at[k],
                                         recv_sem=recv_sems.at[k], device_id=peer, device_id_type=MESH).wait_recv()
        for cp, _, _ in copies:
            cp.wait_send()
        mine.wait()

    return pl.pallas_call(
        body, name=name, out_shape=_sds((N_DEV, R, C), f32),
        in_specs=[pl.BlockSpec(memory_space=pltpu.VMEM)], out_specs=pl.BlockSpec(memory_space=pltpu.VMEM),
        scratch_shapes=[pltpu.SemaphoreType.DMA((N_DEV - 1,)), pltpu.SemaphoreType.DMA((N_DEV - 1,)), pltpu.SemaphoreType.DMA],
        compiler_params=pltpu.CompilerParams(vmem_limit_bytes=VMEM_LIMIT))(a)


def all_reduce_small(a, name):
    R, C = a.shape
    rs = R // N_DEV

    def rows(d):
        return pl.ds(pl.multiple_of(d * rs, 8), rs)

    def body(a_ref, out_ref, buf, s1, r1, s2, r2):
        x, y, c = _coords()
        me = 4 * x + 2 * y + c
        peers = []
        for k in range(1, N_DEV):
            peer = (_flip(x, (k >> 2) & 1), _flip(y, (k >> 1) & 1), _flip(c, k & 1))
            peers.append((k - 1, peer, 4 * peer[0] + 2 * peer[1] + peer[2]))

        def scatter(j, peer, pidx, dst_slot):
            return pltpu.make_async_remote_copy(src_ref=a_ref.at[rows(pidx)], dst_ref=buf.at[dst_slot], send_sem=s1.at[j],
                                                recv_sem=r1.at[j], device_id=peer, device_id_type=MESH)

        def spread(j, peer, slot):
            return pltpu.make_async_remote_copy(src_ref=out_ref.at[rows(me)], dst_ref=out_ref.at[rows(slot)],
                                                send_sem=s2.at[j], recv_sem=r2.at[j], device_id=peer, device_id_type=MESH)

        sends = [scatter(j, peer, pidx, me) for j, peer, pidx in peers]
        for cp in sends:
            cp.start()
        buf[me] = a_ref[rows(me), :]
        for j, peer, pidx in peers:
            scatter(j, peer, pidx, pidx).wait_recv()
        total = buf[0]
        for d in range(1, N_DEV):
            total = total + buf[d]
        out_ref[rows(me), :] = total
        outs = [spread(j, peer, me) for j, peer, pidx in peers]
        for cp in outs:
            cp.start()
        for j, peer, pidx in peers:
            spread(j, peer, pidx).wait_recv()
        for cp in sends + outs:
            cp.wait_send()

    sem = pltpu.SemaphoreType.DMA((N_DEV - 1,))
    return pl.pallas_call(
        body, name=name, out_shape=_sds((R, C), f32), in_specs=[pl.BlockSpec(memory_space=pltpu.VMEM)],
        out_specs=pl.BlockSpec(memory_space=pltpu.VMEM),
        scratch_shapes=[pltpu.VMEM((N_DEV, rs, C), f32), sem, sem, sem, sem],
        compiler_params=pltpu.CompilerParams(vmem_limit_bytes=VMEM_LIMIT))(a)


def swap_with_sibling(parts, name):
    n = len(parts)

    def body(*refs):
        in_refs, out_refs = refs[:n], refs[n:2 * n]
        send_sems, recv_sems = refs[2 * n:]
        x, y, c = _coords()
        sends = []
        for w in range(n):
            cp = pltpu.make_async_remote_copy(src_ref=in_refs[w], dst_ref=out_refs[w], send_sem=send_sems.at[w],
                                              recv_sem=recv_sems.at[w], device_id=(x, y, 1 - c), device_id_type=MESH)
            cp.start()
            sends.append(cp)
        for cp in sends:
            cp.wait_recv()
        for cp in sends:
            cp.wait_send()

    hbm = pl.BlockSpec(memory_space=pl.ANY)
    return pl.pallas_call(
        body, name=name, out_shape=[_sds(a.shape, a.dtype) for a in parts], in_specs=[hbm] * n, out_specs=[hbm] * n,
        scratch_shapes=[pltpu.SemaphoreType.DMA((n,)), pltpu.SemaphoreType.DMA((n,))],
        )(*parts)


def even_fwd(cfg, x, p, tag, host=None):
    W, H = p["W"], p["H"]
    got = {}
    h = pre_fwd(cfg, x, p["g_pre"], p["shift"], p["scale"], tag + "_pre")
    z = _hosted(host, "in", got, lambda cm: matmul("v1", h, p["w_in"], comm=cm, name=tag + "_in"))
    u = conv_fwd(cfg, z, p["conv_w"], p["conv_b"], W, tag + "_conv")
    a, b = lru_coef_fwd(cfg, u, p["wa"], p["ba"], p["wx"], p["bx"], p["lam"], tag + "_coef")
    hh, hp = lru_scan(cfg, a, b, tag + "_scan")
    lru = lru_out_fwd(cfg, z, hh, W, tag + "_lruout")
    qcol = 2 * W // RET_DK
    o, st = ret_fwd(cfg, z, p["logit"], p["cos1"], p["sin1"], H, qcol, tag + "_ret")
    olcol = (2 * W + 3 * H * RET_DK) // (H * RET_DV)
    ret = ret_norm_fwd(cfg, o, z, p["gn"], H, olcol, tag + "_retnorm")
    mix = ev_mix(cfg, lru, ret, tag + "_mix")
    y = _hosted(host, "out", got, lambda cm: matmul("v2", mix, p["w_out"], comm=cm, name=tag + "_out"))
    xo = post_fwd(cfg, x, y, p["g_post"], p["gate"], 1.0, tag + "_post")
    return xo, (x, h, z, u, a, hh, hp, o, st, mix, y, olcol, qcol), got


def even_bwd(cfg, dX, saved, p, tag):
    x, h, z, u, a, hh, hp, o, st, mix, y, olcol, qcol = saved
    W, H = p["W"], p["H"]
    dy, dg_post, dgate = post_bwd(cfg, dX, y, p["g_post"], p["gate"], 1.0, tag + "_postb")
    dmix = matmul("v4", dy, p["w_out"], name=tag + "_dmix")
    g_out = matmul("v6", mix, dy, gshape=p["w_out"].shape, out_dtype=bf16, name=tag + "_gwout")
    dgl, dhs = lru_out_bwd(cfg, z, hh, dmix, W, tag + "_lruoutb")
    da, db = lru_scan_bwd(cfg, a, hp, dhs, tag + "_scanb")
    du, dwa, dba, dwx, dbx, dlam = lru_coef_bwd(cfg, u, da, db, p["wa"], p["ba"], p["wx"], p["bx"], p["lam"], tag + "_coefb")
    dr, dcw, dcb = conv_bwd(cfg, z, du, p["conv_w"], W, tag + "_convb")
    do, dol, dgn = ret_norm_bwd(cfg, o, z, p["gn"], dmix, H, olcol, W // (H * RET_DV), tag + "_retnormb")
    dq, dk, dv, dlg = ret_bwd(cfg, z, st, do, p["logit"], p["cos1"], p["sin1"], H, qcol, tag + "_retb")
    dz = ev_dz_pack(cfg, dgl, dr, dq, dk, dv, dol, tag + "_dz")
    g_in, r_out = matmul("v5", h, dz, gshape=p["w_in"].shape, out_dtype=bf16, comm=Comm("scatter", [(g_out, ())]),
                         name=tag + "_gwin")
    dh, r_in = matmul("v3", dz, p["w_in"], comm=Comm("scatter", [(g_in, ())]), name=tag + "_dh")
    dX, dg_pre, dshift, dscale = pre_bwd(cfg, x, p["g_pre"], p["shift"], p["scale"], dh, dX, tag + "_preb")
    pg = dict(g_pre=dg_pre, g_post=dg_post, shift=dshift, scale=dscale, gate=dgate, conv_w=dcw, conv_b=dcb, wa=dwa,
              ba=dba, wx=dwx, bx=dbx, lam=dlam, logit=dlg, gn=dgn)
    return dX, pg, dict(w_in=r_in[0], w_out=r_out[0])


def odd_fwd(cfg, x, p, tag, host=None):
    nq = p["nq"]
    got = {}
    h = pre_fwd(cfg, x, p["g_pre"], p["shift"], p["scale"], tag + "_pre")
    z = _hosted(host, "in", got, lambda cm: matmul("v1", h, p["w_in"], comm=cm, name=tag + "_in"))
    pooled = pool_fwd(cfg, z, p["pool_w"], p["pool_scale"], tag + "_pool")
    qn, kn, vb = att_prep(cfg, z, p["qg"], p["kg"], p["cosf"], p["sinf"], nq, tag + "_prep")
    o, lse = att_fwd(cfg, qn, kn, vb, nq, tag + "_att")
    mix = od_mix(cfg, pooled, o, tag + "_mix")
    y = matmul("v2", mix, p["w_out"], name=tag + "_out")
    xo = post_fwd(cfg, x, y, p["g_post"], p["gate"], 1.0, tag + "_post")
    return xo, (x, h, z, qn, kn, vb, o, lse, mix, y), got


def odd_bwd(cfg, dX, saved, p, tag):
    x, h, z, qn, kn, vb, o, lse, mix, y = saved
    nq = p["nq"]
    U = kn.shape[1]
    dy, dg_post, dgate = post_bwd(cfg, dX, y, p["g_post"], p["gate"], 1.0, tag + "_postb")
    dmix = matmul("v4", dy, p["w_out"], name=tag + "_dmix")
    g_out = matmul("v6", mix, dy, gshape=p["w_out"].shape, out_dtype=bf16, name=tag + "_gwout")
    dm, dmn, dpw, dps = pool_bwd_a(cfg, z, dmix, p["pool_w"], p["pool_scale"], tag + "_poolb")
    dxp = pool_bwd_b(cfg, dm, dmn, tag + "_poolb2")
    do = dmix[:, U:]
    dqn, dkn, dvb = att_bwd(cfg, qn, kn, vb, o, lse, do, nq, tag + "_attb")
    dz, dqg, dkg = att_prep_bwd(cfg, z, p["qg"], p["kg"], p["cosf"], p["sinf"], dqn, dkn, dvb, dxp, nq, tag + "_prepb")
    g_in, r_out = matmul("v5", h, dz, gshape=p["w_in"].shape, out_dtype=bf16, comm=Comm("scatter", [(g_out, ())]),
                         name=tag + "_gwin")
    dh, r_in = matmul("v3", dz, p["w_in"], comm=Comm("scatter", [(g_in, ())]), name=tag + "_dh")
    dX, dg_pre, dshift, dscale = pre_bwd(cfg, x, p["g_pre"], p["shift"], p["scale"], dh, dX, tag + "_preb")
    pg = dict(g_pre=dg_pre, g_post=dg_post, shift=dshift, scale=dscale, gate=dgate, pool_w=dpw, pool_scale=dps, qg=dqg, kg=dkg)
    return dX, pg, dict(w_in=r_in[0], w_out=r_out[0])


WEIGHT_NAMES = ("c_ctx", "mod_w", "mod_b", "norm_pre", "norm_post", "ffn_gate", "ffn_up", "ffn_down", "ev_w_in", "ev_w_out",
                "lru_conv_w", "lru_conv_b", "lru_wa", "lru_ba", "lru_wx", "lru_bx", "lru_lambda", "ret_decay_logit", "ret_gn",
                "od_w_in", "od_w_out", "pool_w", "pool_scale", "q_norm", "k_norm")
BIG = ("ffn_gate", "ffn_up", "ffn_down", "ev_w_in", "ev_w_out", "od_w_in", "od_w_out")
SMALL_SHARDED = ("norm_pre", "norm_post", "lru_conv_w", "lru_ba", "lru_bx", "lru_lambda", "pool_scale")
SMALL_REPL = ("mod_b", "lru_conv_b", "lru_wa", "lru_wx", "ret_decay_logit", "ret_gn", "pool_w", "q_norm", "k_norm")
LANES = 128


PACK_ROWS = 512


def _rows_of(n):
    return -(-n // (8 * LANES)) * 8


def _pack(arrs):
    rows = []
    for a in arrs:
        flat = a.reshape(-1)
        rows.append(jnp.pad(flat, (0, _rows_of(flat.shape[0]) * LANES - flat.shape[0])).reshape(-1, LANES))
    total = sum(r.shape[0] for r in rows)
    rows.append(jnp.zeros(((-total) % PACK_ROWS, LANES), f32))
    return jnp.concatenate(rows), None


def _unpack(packed, shapes, lead=()):
    out, pos = [], 0
    for shp in shapes:
        n = math.prod(shp)
        r = _rows_of(n)
        piece = packed[..., pos:pos + r, :].reshape(lead + (r * LANES,))
        out.append(piece[..., :n].reshape(lead + tuple(shp)))
        pos += r
    return out


def _unshard(g):
    return jnp.moveaxis(g, 0, -2).reshape(g.shape[1:-1] + (g.shape[0] * g.shape[-1],))


def _rope_tables(S, Lc):
    n_r = RET_DK // 2
    f_r = RET_THETA ** (-jnp.arange(n_r, dtype=f32) / n_r)
    ang1 = jnp.arange(S, dtype=f32)[:, None] * f_r
    rows = S // GRID_W
    row = jnp.repeat(jnp.arange(rows, dtype=f32), GRID_W)
    col = jnp.tile(jnp.arange(GRID_W, dtype=f32), rows)
    n_ax = HEAD_DIM // 4
    f_ax = ROPE_THETA ** (-jnp.arange(n_ax, dtype=f32) / n_ax)
    ang2 = jnp.concatenate([row[:, None] * f_ax, col[:, None] * f_ax], axis=-1)
    cos2, sin2 = jnp.cos(ang2), jnp.sin(ang2)
    ones = lambda n: jnp.ones((Lc, n), f32)
    zeros = lambda n: jnp.zeros((Lc, n), f32)
    cos1 = jnp.concatenate([ones(n_r), jnp.cos(ang1)])
    sin1 = jnp.concatenate([zeros(n_r), jnp.sin(ang1)])
    cosf = jnp.concatenate([ones(HEAD_DIM), jnp.concatenate([cos2, cos2], axis=1)])
    sinf = jnp.concatenate([zeros(HEAD_DIM), jnp.concatenate([-sin2, sin2], axis=1)])
    return cos1, sin1, cosf, sinf


def kernel(x, c, ctx, c_ctx, mod_w, mod_b, norm_pre, norm_post, ffn_gate, ffn_up, ffn_down, ev_w_in, ev_w_out, lru_conv_w, lru_conv_b, lru_wa, lru_ba, lru_wx, lru_bx, lru_lambda, ret_decay_logit, ret_gn, od_w_in, od_w_out, pool_w, pool_scale, q_norm, k_norm, loss_target, m_c_ctx, m_mod_w, m_mod_b, m_norm_pre, m_norm_post, m_ffn_gate, m_ffn_up, m_ffn_down, m_ev_w_in, m_ev_w_out, m_lru_conv_w, m_lru_conv_b, m_lru_wa, m_lru_ba, m_lru_wx, m_lru_bx, m_lru_lambda, m_ret_decay_logit, m_ret_gn, m_od_w_in, m_od_w_out, m_pool_w, m_pool_scale, m_q_norm, m_k_norm, v_c_ctx, v_mod_w, v_mod_b, v_norm_pre, v_norm_post, v_ffn_gate, v_ffn_up, v_ffn_down, v_ev_w_in, v_ev_w_out, v_lru_conv_w, v_lru_conv_b, v_lru_wa, v_lru_ba, v_lru_wx, v_lru_bx, v_lru_lambda, v_ret_decay_logit, v_ret_gn, v_od_w_in, v_od_w_out, v_pool_w, v_pool_scale, v_q_norm, v_k_norm):
    wts = dict(c_ctx=c_ctx, mod_w=mod_w, mod_b=mod_b, norm_pre=norm_pre, norm_post=norm_post, ffn_gate=ffn_gate, ffn_up=ffn_up,
               ffn_down=ffn_down, ev_w_in=ev_w_in, ev_w_out=ev_w_out, lru_conv_w=lru_conv_w, lru_conv_b=lru_conv_b,
               lru_wa=lru_wa, lru_ba=lru_ba, lru_wx=lru_wx, lru_bx=lru_bx, lru_lambda=lru_lambda,
               ret_decay_logit=ret_decay_logit, ret_gn=ret_gn, od_w_in=od_w_in, od_w_out=od_w_out, pool_w=pool_w,
               pool_scale=pool_scale, q_norm=q_norm, k_norm=k_norm)
    mom_m = dict(zip(WEIGHT_NAMES, (m_c_ctx, m_mod_w, m_mod_b, m_norm_pre, m_norm_post, m_ffn_gate, m_ffn_up, m_ffn_down,
                                    m_ev_w_in, m_ev_w_out, m_lru_conv_w, m_lru_conv_b, m_lru_wa, m_lru_ba, m_lru_wx, m_lru_bx,
                                    m_lru_lambda, m_ret_decay_logit, m_ret_gn, m_od_w_in, m_od_w_out, m_pool_w, m_pool_scale,
                                    m_q_norm, m_k_norm)))
    mom_v = dict(zip(WEIGHT_NAMES, (v_c_ctx, v_mod_w, v_mod_b, v_norm_pre, v_norm_post, v_ffn_gate, v_ffn_up, v_ffn_down,
                                    v_ev_w_in, v_ev_w_out, v_lru_conv_w, v_lru_conv_b, v_lru_wa, v_lru_ba, v_lru_wx, v_lru_bx,
                                    v_lru_lambda, v_ret_decay_logit, v_ret_gn, v_od_w_in, v_od_w_out, v_pool_w, v_pool_scale,
                                    v_q_norm, v_k_norm)))

    _, S, D = x.shape
    Lc = ctx.shape[1]
    T = Lc + S
    TR = 256 if (Lc % 256 == 0 and S % 256 == 0) else 128
    assert Lc % TR == 0 and S % TR == 0 and TR % RET_CHUNK == 0
    cfg = RowCfg(TR, T // TR, Lc // TR)
    W = lru_conv_b.shape[-1]
    H = ret_decay_logit.shape[-1]
    U = POOL_GROUP * len(POOL_WINDOWS)
    nq = (N_CHIPS * od_w_in.shape[-1]) // U - 3
    assert W % (H * RET_DV) == 0 and (2 * W) % (H * RET_DK) == 0
    nL = mod_w.shape[0]
    C4 = mod_w.shape[-1]
    assert nL == 2, "two layers: an even mixer then an odd one"

    xi, yi, ci = _coords()
    chip = 2 * xi + yi
    me = 4 * xi + 2 * yi + ci

    sc = jax.nn.silu(c)
    small_in, _ = _pack([sc] + [wts[n] for n in SMALL_SHARDED])
    g1 = all_gather_small(small_in, "gather_small_fwd")
    parts = _unpack(g1, [sc.shape] + [wts[n].shape for n in SMALL_SHARDED], lead=(N_DEV,))
    sc_all = parts[0][:, 0]
    full = {n: _unshard(parts[1 + i][0::2]) for i, n in enumerate(SMALL_SHARDED)}
    for n in SMALL_REPL + ("c_ctx",):
        full[n] = wts[n]

    scc = jax.nn.silu(c_ctx)[None]
    pad_rows = MOD_ROWS - N_DEV - 1
    s16 = jnp.concatenate([sc_all, scc, jnp.zeros((pad_rows, D), f32)])
    modp = mod_fwd(s16, mod_w, "mod_fwd")
    g2 = all_gather_small(modp.reshape(-1, LANES), "gather_mod")
    mod_all = g2.reshape(N_DEV, nL, MOD_ROWS, C4)[0::2]
    mod_all = jnp.moveaxis(mod_all, 0, 2).reshape(nL, MOD_ROWS, N_CHIPS * C4) + mod_b[:, None, :]
    mod_l = lax.dynamic_index_in_dim(mod_all, me, axis=1, keepdims=False).reshape(nL, 3, 3, D)
    mod_c = mod_all[:, N_DEV].reshape(nL, 3, 3, D)

    def mod_of(li, s, kind, ctx_live=True):
        cpart = mod_c[li, s, kind] if ctx_live else jnp.zeros((D,), f32)
        return jnp.stack([cpart, mod_l[li, s, kind]])[:, None, :]

    packed = {n: cast_bf16(wts[n], "cast_" + n) for n in BIG}
    ffn_units = [(0, 0), (0, 1), (1, 0), (1, 1)]

    def G(*pieces):
        return Comm("gather", [(packed[n], idx) for n, idx in pieces])

    cos1, sin1, cosf, sinf = _rope_tables(S, Lc)

    def sub_params(li, s, ctx_live=True, gate_ctx_live=True):
        return dict(g_pre=full["norm_pre"][li, s][None], g_post=full["norm_post"][li, s][None],
                    shift=mod_of(li, s, 0, ctx_live), scale=mod_of(li, s, 1, ctx_live),
                    gate=mod_of(li, s, 2, ctx_live and gate_ctx_live))

    X0 = jnp.concatenate([ctx[0], x[0]], axis=0)
    wg00, wu00 = exchange(G(("ffn_gate", (0, 0)), ("ffn_up", (0, 0))), "gather_first")
    p00 = sub_params(0, 0)
    p00.update(wg=wg00, wu=wu00)
    h00 = {"gateup": G(("ffn_down", (0, 0)), ("ev_w_in", (0,))), "down": G(("ev_w_out", (0,)), ("ffn_gate", (0, 1)))}
    h = pre_fwd(cfg, X0, p00["g_pre"], p00["shift"], p00["scale"], "l0f0_pre")
    (a00, b00, u00), (wd00, ev_in) = ffn_gateup(cfg, h, wg00, wu00, "l0f0_gateup", comm=h00["gateup"])
    p00.update(wd=wd00)
    y00, (ev_out, wg01) = matmul("v2", u00, wd00, comm=h00["down"], name="l0f0_down")
    X1 = post_fwd(cfg, X0, y00, p00["g_post"], p00["gate"], FFN_STEP, "l0f0_post")
    s00 = (X0, h, a00, b00, u00, y00)

    p01 = sub_params(0, 1)
    p01.update(W=W, H=H, conv_w=full["lru_conv_w"][0], conv_b=full["lru_conv_b"], wa=full["lru_wa"][0],
               ba=full["lru_ba"][0][:, None, :], wx=full["lru_wx"][0], bx=full["lru_bx"][0][:, None, :],
               lam=full["lru_lambda"][0][:, None, :], logit=full["ret_decay_logit"][0][:, :, None, None],
               gn=full["ret_gn"], cos1=cos1, sin1=sin1, w_in=ev_in, w_out=ev_out)
    X2, s01, got = even_fwd(cfg, X1, p01, "l0mix", host={"in": G(("ffn_up", (0, 1))), "out": G(("ffn_down", (0, 1)))})
    p02 = sub_params(0, 2)
    p02.update(wg=wg01, wu=got["in"][0], wd=got["out"][0])
    X3, s02, got = ffn_fwd(cfg, X2, p02, "l0f1", host={"gateup": G(("ffn_gate", (1, 0)), ("ffn_up", (1, 0))),
                                                       "down": G(("ffn_down", (1, 0)))})
    p10 = sub_params(1, 0)
    p10.update(wg=got["gateup"][0], wu=got["gateup"][1], wd=got["down"][0])
    X4, s10, got = ffn_fwd(cfg, X3, p10, "l1f0", host={"gateup": G(("od_w_in", (0,)), ("od_w_out", (0,)), ("ffn_gate", (1, 1))),
                                                       "down": G(("ffn_up", (1, 1)))})
    p11 = sub_params(1, 1, gate_ctx_live=False)
    p11.update(nq=nq, pool_w=full["pool_w"][0], pool_scale=full["pool_scale"], qg=full["q_norm"], kg=full["k_norm"],
               cosf=cosf, sinf=sinf, w_in=got["gateup"][0], w_out=got["gateup"][1])
    p12 = sub_params(1, 2, ctx_live=False)
    p12.update(wg=got["gateup"][2], wu=got["down"][0])
    X5, s11, got = odd_fwd(cfg, X4, p11, "l1mix", host={"in": G(("ffn_down", (1, 1)))})
    p12.update(wd=got["in"][0])
    X6, s12, _ = ffn_fwd(cfg, X5, p12, "l1f1")
    sq, dX = loss_fwd_bwd(cfg, X6, loss_target[0], "loss")
    loss = lax.psum(0.5 * jnp.sum(sq) / D, ("x", "y", "c"))

    recv_ffn = {}
    dX, g12, recv_ffn[(1, 1)] = ffn_bwd(cfg, dX, s12, p12, "l1f1")
    dX, g11, recv_od = odd_bwd(cfg, dX, s11, p11, "l1mix")
    dX, g10, recv_ffn[(1, 0)] = ffn_bwd(cfg, dX, s10, p10, "l1f0")
    dX, g02, recv_ffn[(0, 1)] = ffn_bwd(cfg, dX, s02, p02, "l0f1")
    dX, g01, recv_ev = even_bwd(cfg, dX, s01, p01, "l0mix")
    dX, g00, recv_ffn[(0, 0)] = ffn_bwd(cfg, dX, s00, p00, "l0f0")
    grad_x = dX[Lc:][None]

    subs = [[g00, g01, g02], [g10, g11, g12]]
    zero_d = jnp.zeros((D,), f32)

    def dmod(group, live):
        rows = []
        for li in range(nL):
            for s in range(3):
                for kind, key in enumerate(("shift", "scale", "gate")):
                    rows.append(subs[li][s][key][group, 0] if live(li, s, kind) else zero_d)
        return jnp.stack(rows).reshape(nL, 9 * D)

    dmod_l = dmod(1, lambda li, s, kind: True)
    dmod_c = dmod(0, lambda li, s, kind: not (li == 1 and (s == 2 or (s == 1 and kind == 2))))

    dm_in, _ = _pack([dmod_l, dmod_c])
    g3 = all_gather_small(dm_in, "gather_dmod")
    dl_all, dc_all = _unpack(g3, [dmod_l.shape, dmod_c.shape], lead=(N_DEV,))
    dm16 = jnp.moveaxis(jnp.concatenate([dl_all, dc_all], axis=0), 0, 1)
    dm16 = lax.dynamic_slice_in_dim(dm16, chip * C4, C4, axis=2)
    s16b = jnp.concatenate([sc_all, jnp.broadcast_to(scc, (N_DEV, D))])
    g_mod_w, dscc_part = mod_bwd(s16b, dm16, mod_w, "mod_bwd")

    norm_pre_g = jnp.stack([jnp.concatenate([subs[li][s]["g_pre"] for s in range(3)]) for li in range(nL)])
    norm_post_g = jnp.stack([jnp.concatenate([subs[li][s]["g_post"] for s in range(3)]) for li in range(nL)])
    small_g = dict(norm_pre=norm_pre_g, norm_post=norm_post_g, lru_conv_w=g01["conv_w"][None], lru_ba=g01["ba"][:, 0][None],
                   lru_bx=g01["bx"][:, 0][None], lru_lambda=g01["lam"][:, 0][None], pool_scale=g11["pool_scale"],
                   mod_b=dmod_l + dmod_c, lru_conv_b=g01["conv_b"], lru_wa=g01["wa"][None], lru_wx=g01["wx"][None],
                   ret_decay_logit=g01["logit"][:, :, 0, 0][None], ret_gn=g01["gn"], pool_w=g11["pool_w"][None],
                   q_norm=g11["qg"], k_norm=g11["kg"])
    names = SMALL_SHARDED + SMALL_REPL
    sg_in, _ = _pack([small_g[n] for n in names] + [dscc_part])
    tot = all_reduce_small(sg_in, "gather_small_grads")
    tot_parts = _unpack(tot, [small_g[n].shape for n in names] + [dscc_part.shape])
    dscc = 0.5 * tot_parts[-1][0]
    tot_parts = tot_parts[:-1]
    _, silu_vjp = jax.vjp(jax.nn.silu, c_ctx)
    grads = {"c_ctx": silu_vjp(dscc)[0]}
    for n, g in zip(names, tot_parts):
        if n in SMALL_SHARDED:
            k = wts[n].shape[-1]
            g = lax.dynamic_slice_in_dim(g, chip * k, k, axis=g.ndim - 1)
        grads[n] = g.reshape(wts[n].shape)

    partial = {}
    for n, key in (("ffn_gate", "wg"), ("ffn_up", "wu"), ("ffn_down", "wd")):
        acc = None
        for u in ffn_units:
            acc = sum_leading(recv_ffn[u][key], "sum_%s_%d%d" % (n, u[0], u[1]), into=acc, full_shape=wts[n].shape, widx=u)
        partial[n] = acc
    for n, r in (("ev_w_in", recv_ev["w_in"]), ("ev_w_out", recv_ev["w_out"]), ("od_w_in", recv_od["w_in"]),
                 ("od_w_out", recv_od["w_out"])):
        partial[n] = sum_leading(r, "sum_" + n).reshape(wts[n].shape)
    partial = [partial[n] for n in BIG]
    other = swap_with_sibling(partial, "swap_partials")

    delta, new_m, new_v = {}, {}, {}
    for n, pa, pb in zip(BIG, partial, other):
        grads[n], delta[n], new_m[n], new_v[n] = adamw(wts[n], mom_m[n], mom_v[n], [pa, pb], "adamw_" + n)
    grads["mod_w"], delta["mod_w"], new_m["mod_w"], new_v["mod_w"] = adamw(mod_w, m_mod_w, v_mod_w, [g_mod_w], "adamw_mod_w")
    snames = [n for n in WEIGHT_NAMES if n not in BIG and n != "mod_w"]
    pk = lambda d: _pack([d[n] for n in snames])[0]
    sres = adamw(pk(wts), pk(mom_m), pk(mom_v), [pk(grads)], "adamw_small")
    for res, dst in zip(sres[1:], (delta, new_m, new_v)):
        for n, a in zip(snames, _unpack(res, [wts[n].shape for n in snames])):
            dst[n] = a

    return (loss, grad_x, *[grads[n] for n in WEIGHT_NAMES], *[delta[n] for n in WEIGHT_NAMES],
            *[new_m[n] for n in WEIGHT_NAMES], *[new_v[n] for n in WEIGHT_NAMES])
```

```python
import functools
import math

import jax
import jax.numpy as jnp
from jax import lax
from jax.experimental import pallas as pl
from jax.experimental.pallas import tpu as pltpu

f32 = jnp.float32
bf16 = jnp.bfloat16
MESH = pl.DeviceIdType.MESH

EPS = 1e-6
FFN_STEP = 0.5
LRU_C = 8.0
CONV_W = 4
CONV_LEFT = 2
RET_DK = 256
RET_DV = 256
RET_CHUNK = 128
RET_THETA = 10000.0
POOL_WINDOWS = (2, 4, 8, 16)
POOL_GROUP = 128
HEAD_DIM = 128
ROPE_THETA = 10000.0
GRID_W = 64
ADAM_LR = 0.001
ADAM_B1 = 0.9
ADAM_B2 = 0.999
ADAM_EPS = 1e-08
ADAM_WD = 0.01
ADAM_STEP = 10

N_CHIPS = 4
N_DEV = 8
HALO = 8
VMEM_LIMIT = 56 * 1024 * 1024


def _params(sem=None):
    return pltpu.CompilerParams(dimension_semantics=sem, vmem_limit_bytes=VMEM_LIMIT)


def _pick(n, prefs):
    for p in prefs:
        if n % p == 0:
            return p
    return n


def _sds(shape, dtype):
    return jax.ShapeDtypeStruct(tuple(shape), dtype)


MATMUL_VMEM = 46 * 1024 * 1024


def _fit_rows(M, tm, tn, out_dtype):
    fixed = 2 * tm * tn * 4 + 2 * tm * tn * jnp.dtype(out_dtype).itemsize
    for rows in (1408, 768, 512, 256, 128):
        if M % rows == 0 and fixed + 2 * rows * (tm + tn) * 2 <= MATMUL_VMEM:
            return rows
    return M


_MM_KINDS = {
    "v1": ((1, 0), "out[:, g] = A @ W[g]"),
    "v2": ((1, 0), "out = sum_g A[:, g] @ W[g]"),
    "v3": ((1, 1), "out = sum_g A[:, g] @ W[g]^T"),
    "v4": ((1, 1), "out[:, g] = A @ W[g]^T"),
    "v5": ((0, 0), "out[g] = A^T @ C[:, g]"),
    "v6": ((0, 0), "out[g] = A[:, g]^T @ C"),
}


def matmul(kind, a, b, *, widx=(), out_dtype=f32, init=None, gshape=None, comm=None, name):
    nw = len(widx)
    cdims = _MM_KINDS[kind][0]
    if kind in ("v1", "v2", "v3", "v4"):
        G = b.shape[0]
        d1, d2 = b.shape[-2:]
        M = a.shape[0]
    else:
        G, d1, d2 = gshape
        M = a.shape[0]
    tm_p, tn_p, tk_p = (768, 512, 256, 128), (1408, 1536, 1024, 768, 512, 256, 128), (2048, 1408, 1536, 1024, 768, 512, 256, 128)
    wnone = (None,) * (1 + nw)

    if kind == "v1":
        K, Ns = d1, d2
        tm, tn, tk = _pick(M, tm_p), _pick(Ns, tn_p), _pick(K, tk_p)
        nI, nJ, nR = M // tm, Ns // tn, K // tk
        grid = (G, nI, nJ, nR)
        a_spec = pl.BlockSpec((tm, tk), lambda g, i, j, r: (i, r))
        b_spec = pl.BlockSpec(wnone + (tk, tn), lambda g, i, j, r: (g,) + widx + (r, j))
        o_spec = pl.BlockSpec((tm, tn), lambda g, i, j, r: (i, g * nJ + j))
        out_shape = _sds((M, G * Ns), out_dtype)
        acc_shape = (tm, tn)
    elif kind == "v2":
        Ks, N = d1, d2
        tm, tn, tk = _pick(M, tm_p), _pick(N, (2048,) + tn_p), _pick(Ks, tk_p)
        nI, nJ, nRk = M // tm, N // tn, Ks // tk
        nR = G * nRk
        grid = (1, nI, nJ, nR)
        a_spec = pl.BlockSpec((tm, tk), lambda g, i, j, r: (i, r))
        b_spec = pl.BlockSpec(wnone + (tk, tn), lambda g, i, j, r: (r // nRk,) + widx + (r % nRk, j))
        o_spec = pl.BlockSpec((tm, tn), lambda g, i, j, r: (i, j))
        out_shape = _sds((M, N), out_dtype)
        acc_shape = (tm, tn)
    elif kind == "v3":
        K, Ns = d1, d2
        tm, tn, tk = _pick(M, tm_p), _pick(K, (2048,) + tn_p), _pick(Ns, tk_p)
        nI, nJ, nRk = M // tm, K // tn, Ns // tk
        nR = G * nRk
        grid = (1, nI, nJ, nR)
        a_spec = pl.BlockSpec((tm, tk), lambda g, i, j, r: (i, r))
        b_spec = pl.BlockSpec(wnone + (tn, tk), lambda g, i, j, r: (r // nRk,) + widx + (j, r % nRk))
        o_spec = pl.BlockSpec((tm, tn), lambda g, i, j, r: (i, j))
        out_shape = _sds((M, K), out_dtype)
        acc_shape = (tm, tn)
    elif kind == "v4":
        Ks, N = d1, d2
        tm, tn, tk = _pick(M, tm_p), _pick(Ks, tn_p), _pick(N, tk_p)
        nI, nJ, nR = M // tm, Ks // tn, N // tk
        grid = (G, nI, nJ, nR)
        a_spec = pl.BlockSpec((tm, tk), lambda g, i, j, r: (i, r))
        b_spec = pl.BlockSpec(wnone + (tn, tk), lambda g, i, j, r: (g,) + widx + (j, r))
        o_spec = pl.BlockSpec((tm, tn), lambda g, i, j, r: (i, g * nJ + j))
        out_shape = _sds((M, G * Ks), out_dtype)
        acc_shape = (tm, tn)
    elif kind == "v5":
        K, Ns = d1, d2
        tm, tn, tk = _pick(K, (2048,) + tm_p), _pick(Ns, tn_p), 0
        tk = _fit_rows(M, tm, tn, out_dtype)
        nI, nJ, nR = K // tm, Ns // tn, M // tk
        grid = (G, nI, nJ, nR)
        a_spec = pl.BlockSpec((tk, tm), lambda g, i, j, r: (r, i))
        b_spec = pl.BlockSpec((tk, tn), lambda g, i, j, r: (r, g * nJ + j))
        o_spec = pl.BlockSpec((None, tm, tn), lambda g, i, j, r: (g, i, j))
        out_shape = _sds(gshape, out_dtype)
        acc_shape = (tm, tn)
    else:
        Ks, N = d1, d2
        tm, tn, tk = _pick(Ks, (1408,) + tm_p), _pick(N, (2048,) + tn_p), 0
        tk = _fit_rows(M, tm, tn, out_dtype)
        nI, nJ, nR = Ks // tm, N // tn, M // tk
        grid = (G, nI, nJ, nR)
        a_spec = pl.BlockSpec((tk, tm), lambda g, i, j, r: (r, g * nI + i))
        b_spec = pl.BlockSpec((tk, tn), lambda g, i, j, r: (r, j))
        o_spec = pl.BlockSpec((None, tm, tn), lambda g, i, j, r: (g, i, j))
        out_shape = _sds(gshape, out_dtype)
        acc_shape = (tm, tn)

    has_init = init is not None
    ncomm = len(comm.srcs) if comm is not None else 0

    def body(*refs):
        a_ref, b_ref = refs[0], refs[1]
        pos = 2
        init_ref = None
        if has_init:
            init_ref = refs[pos]
            pos += 1
        cin = refs[pos:pos + ncomm]
        pos += ncomm
        o_ref = refs[pos]
        cout = refs[pos + 1:pos + 1 + ncomm]
        acc_ref = refs[pos + 1 + ncomm]
        sems = refs[pos + 2 + ncomm:]
        r = pl.program_id(3)
        first, last = _grid_ends(grid)

        if ncomm:
            @pl.when(first)
            def _():
                _comm_start(comm, cin, cout, *sems)

        def prod():
            return lax.dot_general(a_ref[...], b_ref[...], ((cdims[:1], cdims[1:]), ((), ())), preferred_element_type=f32)

        def start():
            return init_ref[...] + prod() if has_init else prod()

        if nR == 1:
            o_ref[...] = start().astype(o_ref.dtype)
        else:
            @pl.when(r == 0)
            def _():
                acc_ref[...] = start()

            @pl.when(jnp.logical_and(r > 0, r < nR - 1))
            def _():
                acc_ref[...] += prod()

            @pl.when(r == nR - 1)
            def _():
                o_ref[...] = (acc_ref[...] + prod()).astype(o_ref.dtype)

        if ncomm:
            @pl.when(last)
            def _():
                _comm_wait(comm, cin, cout, *sems)

    in_specs = [a_spec, b_spec]
    args = [a, b]
    if has_init:
        in_specs.append(pl.BlockSpec((tm, tn), lambda g, i, j, r: (i, j)))
        args.append(init)
    out_specs, out_shapes, scratch = [o_spec], [out_shape], [pltpu.VMEM(acc_shape, f32)]
    if ncomm:
        hbm = pl.BlockSpec(memory_space=pl.ANY)
        in_specs += [hbm] * ncomm
        args += [src for src, _ in comm.srcs]
        out_specs += [hbm] * ncomm
        out_shapes += comm.out_shapes()
        scratch += _comm_sems(ncomm)
    res = pl.pallas_call(
        body, name=name, grid=grid, in_specs=in_specs, out_specs=out_specs, out_shape=out_shapes,
        scratch_shapes=scratch, compiler_params=_params(("arbitrary", "arbitrary", "arbitrary", "arbitrary")),
    )(*args)
    return (res[0], list(res[1:])) if ncomm else res[0]


class Comm:
    def __init__(self, mode, srcs):
        self.mode, self.srcs = mode, srcs

    def piece(self, n):
        arr, idx = self.srcs[n]
        shp = arr.shape[len(idx):]
        return shp if self.mode == "gather" else shp[1:]

    def out_shapes(self):
        return [_sds((N_CHIPS,) + tuple(self.piece(n)), self.srcs[n][0].dtype) for n in range(len(self.srcs))]


def _comm_sems(n):
    nsem = n * (N_CHIPS - 1)
    return [pltpu.SemaphoreType.DMA((nsem,)), pltpu.SemaphoreType.DMA((nsem,)), pltpu.SemaphoreType.DMA((n,))]


def _coords():
    return lax.axis_index("x"), lax.axis_index("y"), lax.axis_index("c")


def _flip(v, bit):
    return 1 - v if bit else v


def _chip_peers(x, y, c):
    out = []
    for k in range(1, N_CHIPS):
        kx, ky = (k >> 1) & 1, k & 1
        px, py = _flip(x, kx), _flip(y, ky)
        out.append((k, (px, py, c), 2 * px + py))
    return out


def _comm_copies(comm, in_refs, out_refs, send_sems, recv_sems, local_sems, with_recvs):
    x, y, c = _coords()
    s = 2 * x + y
    local, sends, recvs = [], [], []
    for w, (_, idx) in enumerate(comm.srcs):
        src = in_refs[w].at[idx] if idx else in_refs[w]
        out = out_refs[w]
        if comm.mode == "gather":
            local.append(pltpu.make_async_copy(src, out.at[s], local_sems.at[w]))
        else:
            local.append(pltpu.make_async_copy(src.at[s], out.at[N_CHIPS - 1], local_sems.at[w]))
        for k, peer, pidx in _chip_peers(x, y, c):
            j = w * (N_CHIPS - 1) + k - 1
            if comm.mode == "gather":
                out_src, out_dst, in_dst = src, out.at[s], out.at[pidx]
            else:
                out_src, out_dst, in_dst = src.at[pidx], out.at[k - 1], out.at[k - 1]
            sends.append(pltpu.make_async_remote_copy(src_ref=out_src, dst_ref=out_dst, send_sem=send_sems.at[j],
                                                      recv_sem=recv_sems.at[j], device_id=peer, device_id_type=MESH))
            if with_recvs:
                recvs.append(pltpu.make_async_remote_copy(src_ref=out_src, dst_ref=in_dst, send_sem=send_sems.at[j],
                                                          recv_sem=recv_sems.at[j], device_id=peer, device_id_type=MESH))
    return local, sends, recvs


def _comm_start(comm, in_refs, out_refs, send_sems, recv_sems, local_sems):
    local, sends, _ = _comm_copies(comm, in_refs, out_refs, send_sems, recv_sems, local_sems, False)
    for cp in local + sends:
        cp.start()


def _comm_wait(comm, in_refs, out_refs, send_sems, recv_sems, local_sems):
    local, sends, recvs = _comm_copies(comm, in_refs, out_refs, send_sems, recv_sems, local_sems, True)
    for cp in recvs:
        cp.wait_recv()
    for cp in sends:
        cp.wait_send()
    for cp in local:
        cp.wait()


def gather_two_level(comm, name):
    n = len(comm.srcs)
    nrem = n * (N_CHIPS - 1)

    def body(*refs):
        in_refs, out_refs = refs[:n], refs[n:2 * n]
        ici_s, ici_r, d2d_s, d2d_r, local_sems = refs[2 * n:]
        x, y, c = _coords()
        s = 2 * x + y
        local, ici, fwd = [], [], []
        for w, (_, idx) in enumerate(comm.srcs):
            src = in_refs[w].at[idx] if idx else in_refs[w]
            out = out_refs[w]
            half = comm.piece(w)[0] // 2
            mine, theirs = pl.ds(c * half, half), pl.ds((1 - c) * half, half)
            cp = pltpu.make_async_copy(src, out.at[s], local_sems.at[w])
            cp.start()
            local.append(cp)
            for k, peer, pidx in _chip_peers(x, y, c):
                j = w * (N_CHIPS - 1) + k - 1
                send = pltpu.make_async_remote_copy(src_ref=src.at[mine], dst_ref=out.at[s].at[mine], send_sem=ici_s.at[j],
                                                    recv_sem=ici_r.at[j], device_id=peer, device_id_type=MESH)
                send.start()
                landed = pltpu.make_async_remote_copy(src_ref=src.at[mine], dst_ref=out.at[pidx].at[mine],
                                                      send_sem=ici_s.at[j], recv_sem=ici_r.at[j], device_id=peer,
                                                      device_id_type=MESH)
                passed = pltpu.make_async_remote_copy(src_ref=out.at[pidx].at[mine], dst_ref=out.at[pidx].at[mine],
                                                      send_sem=d2d_s.at[j], recv_sem=d2d_r.at[j], device_id=(x, y, 1 - c),
                                                      device_id_type=MESH)
                got = pltpu.make_async_remote_copy(src_ref=out.at[pidx].at[mine], dst_ref=out.at[pidx].at[theirs],
                                                   send_sem=d2d_s.at[j], recv_sem=d2d_r.at[j], device_id=(x, y, 1 - c),
                                                   device_id_type=MESH)
                ici.append((send, landed))
                fwd.append((passed, got))
        for (send, landed), (passed, got) in zip(ici, fwd):
            landed.wait_recv()
            passed.start()
        for passed, got in fwd:
            got.wait_recv()
        for (send, _), (passed, _) in zip(ici, fwd):
            send.wait_send()
            passed.wait_send()
        for cp in local:
            cp.wait()

    hbm = pl.BlockSpec(memory_space=pl.ANY)
    sem = pltpu.SemaphoreType.DMA((nrem,))
    return pl.pallas_call(body, name=name, out_shape=comm.out_shapes(), in_specs=[hbm] * n, out_specs=[hbm] * n,
                          scratch_shapes=[sem, sem, sem, sem, pltpu.SemaphoreType.DMA((n,))])(*[src for src, _ in comm.srcs])


def exchange(comm, name):
    n = len(comm.srcs)

    def body(*refs):
        in_refs, out_refs, sems = refs[:n], refs[n:2 * n], refs[2 * n:]
        _comm_start(comm, in_refs, out_refs, *sems)
        _comm_wait(comm, in_refs, out_refs, *sems)

    hbm = pl.BlockSpec(memory_space=pl.ANY)
    return pl.pallas_call(body, name=name, out_shape=comm.out_shapes(), in_specs=[hbm] * n, out_specs=[hbm] * n,
                          scratch_shapes=_comm_sems(n))(*[src for src, _ in comm.srcs])


class RowCfg:
    def __init__(self, TR, nT, cT):
        self.TR, self.nT, self.cT = TR, nT, cT


def _row_spec(cfg, spec, off):
    kind = spec[0]
    TR = cfg.TR
    hb = TR // HALO
    nH = cfg.nT * hb
    if kind == "row":
        _, arr, w, cb = spec
        return pl.BlockSpec((TR, w), lambda i: (i + off, cb))
    if kind == "prev":
        _, arr, w, cb = spec
        return pl.BlockSpec((HALO, w), lambda i: (jnp.maximum((i + off) * hb - 1, 0), cb))
    if kind == "next":
        _, arr, w, cb = spec
        return pl.BlockSpec((HALO, w), lambda i: (jnp.minimum((i + off + 1) * hb, nH - 1), cb))
    if kind == "full":
        arr = spec[1]
        nd = arr.ndim
        return pl.BlockSpec(arr.shape, lambda i: (0,) * nd)
    if kind == "grp":
        arr = spec[1]
        cT = cfg.cT
        return pl.BlockSpec((None, 1, arr.shape[-1]), lambda i: (((i + off) >= cT).astype(jnp.int32), 0, 0))
    if kind == "drow":
        _, arr, w, cb = spec
        return pl.BlockSpec((arr.shape[0], TR, w), lambda i: (0, i + off, cb))
    raise ValueError(kind)


def rowcall(cfg, fn, name, ins, outs, *, off=0, n=None, scratch=()):
    n = cfg.nT - off if n is None else n
    in_specs = [_row_spec(cfg, s, off) for s in ins]
    out_specs = [_row_spec(cfg, (s[0], s[1]) + tuple(s[2:]), off) for s in outs]
    out_shape = [s[1] for s in outs]

    def body(*refs):
        fn(pl.program_id(0) + off, *refs)

    res = pl.pallas_call(
        body, name=name, grid=(n,), in_specs=in_specs, out_specs=out_specs, out_shape=out_shape,
        scratch_shapes=list(scratch), compiler_params=_params(("arbitrary",)),
    )(*[s[1] for s in ins])
    return res


def _acc(ref, val, first):
    @pl.when(first)
    def _():
        ref[...] = val

    @pl.when(jnp.logical_not(first))
    def _():
        ref[...] += val


def _rms(x):
    return x * lax.rsqrt(jnp.mean(x * x, axis=-1, keepdims=True) + EPS)


def _pre_fn(x, g, shift, scale):
    return (_rms(x) * g) * (1.0 + scale) + shift


def _strips(TR, rows_per_strip, body, init):
    n = TR // rows_per_strip
    unroll = STRIP_UNROLL if n % STRIP_UNROLL == 0 else 1

    def step(r, carry):
        for u in range(unroll):
            start = pl.multiple_of((r * unroll + u) * rows_per_strip, rows_per_strip)
            carry = body(pl.ds(start, rows_per_strip), carry)
        return carry

    return lax.fori_loop(0, n // unroll, step, init)


STRIP_UNROLL = 8
F32_STRIP = 8
BF16_STRIP = 16


def _inv_rms(x):
    return lax.rsqrt(jnp.mean(x * x, axis=-1, keepdims=True) + EPS)


def pre_fwd(cfg, x, g, shift, scale, name):
    D = x.shape[1]

    def fn(i, x_ref, g_ref, sh_ref, sc_ref, h_ref):
        c = g_ref[...] * (1.0 + sc_ref[...])
        sh = sh_ref[...]

        def strip(rows, carry):
            xv = x_ref[rows, :]
            h_ref[rows, :] = (xv * _inv_rms(xv) * c + sh).astype(bf16)
            return carry

        _strips(cfg.TR, BF16_STRIP, strip, 0)

    return rowcall(cfg, fn, name, [("row", x, D, 0), ("full", g), ("grp", shift), ("grp", scale)],
                   [("row", _sds(x.shape, bf16), D, 0)])[0]


def pre_bwd(cfg, x, g, shift, scale, dh, dx_in, name):
    D = x.shape[1]
    cT = cfg.cT

    def fn(i, x_ref, g_ref, sh_ref, sc_ref, dh_ref, dxin_ref, dx_ref, dg_ref, dsh_ref, dsc_ref):
        gv, scv = g_ref[...], sc_ref[...]
        c = gv * (1.0 + scv)

        def strip(rows, carry):
            s0, s1 = carry
            xv, dhv = x_ref[rows, :], dh_ref[rows, :]
            r = _inv_rms(xv)
            xn = xv * r
            dxn = dhv * c
            m = jnp.mean(dxn * xn, axis=-1, keepdims=True)
            dx_ref[rows, :] = dxin_ref[rows, :] + r * (dxn - xn * m)
            return s0 + dhv, s1 + dhv * xn

        zero = jnp.zeros((F32_STRIP, D), f32)
        s0, s1 = _strips(cfg.TR, F32_STRIP, strip, (zero, zero))
        s0 = jnp.sum(s0, axis=0, keepdims=True)
        s1 = jnp.sum(s1, axis=0, keepdims=True)
        _acc(dg_ref, (1.0 + scv) * s1, i == 0)
        first = jnp.logical_or(i == 0, i == cT)
        _acc(dsh_ref, s0, first)
        _acc(dsc_ref, gv * s1, first)

    return rowcall(cfg, fn, name,
                   [("row", x, D, 0), ("full", g), ("grp", shift), ("grp", scale), ("row", dh, D, 0), ("row", dx_in, D, 0)],
                   [("row", _sds(x.shape, f32), D, 0), ("full", _sds((1, D), f32)),
                    ("grp", _sds((2, 1, D), f32)), ("grp", _sds((2, 1, D), f32))])


def _post_fn(w, y, g, gate):
    return (w * gate) * (_rms(y) * g)


def post_fwd(cfg, x, y, g, gate, w, name):
    D = x.shape[1]

    def fn(i, x_ref, y_ref, g_ref, gt_ref, o_ref):
        c = (w * gt_ref[...]) * g_ref[...]

        def strip(rows, carry):
            yv = y_ref[rows, :]
            o_ref[rows, :] = x_ref[rows, :] + c * (yv * _inv_rms(yv))
            return carry

        _strips(cfg.TR, F32_STRIP, strip, 0)

    return rowcall(cfg, fn, name, [("row", x, D, 0), ("row", y, D, 0), ("full", g), ("grp", gate)],
                   [("row", _sds(x.shape, f32), D, 0)])[0]


def post_bwd(cfg, dx, y, g, gate, w, name):
    D = dx.shape[1]
    cT = cfg.cT
    half = BF16_STRIP // 2

    def fn(i, dx_ref, y_ref, g_ref, gt_ref, dy_ref, dg_ref, dgt_ref):
        gv, gtv = g_ref[...], gt_ref[...]
        c = (w * gtv) * gv

        def strip(rows, s1):
            yv, dv = y_ref[rows, :], dx_ref[rows, :]
            r = _inv_rms(yv)
            yn = yv * r
            dyn = dv * c
            m = jnp.mean(dyn * yn, axis=-1, keepdims=True)
            dy_ref[rows, :] = (r * (dyn - yn * m)).astype(bf16)
            t = dv * yn
            return s1 + t[:half] + t[half:]

        s1 = _strips(cfg.TR, BF16_STRIP, strip, jnp.zeros((half, D), f32))
        s1 = jnp.sum(s1, axis=0, keepdims=True)
        _acc(dg_ref, (w * gtv) * s1, i == 0)
        _acc(dgt_ref, (w * gv) * s1, jnp.logical_or(i == 0, i == cT))

    return rowcall(cfg, fn, name, [("row", dx, D, 0), ("row", y, D, 0), ("full", g), ("grp", gate)],
                   [("row", _sds(dx.shape, bf16), D, 0), ("full", _sds((1, D), f32)), ("grp", _sds((2, 1, D), f32))])


def _swiglu_fn(a, b):
    return jax.nn.silu(a) * b


def swiglu_fwd(cfg, a, b, name):
    F = a.shape[1]
    tf = _pick(F, (1408, 1024, 512, 256, 128))
    TR = cfg.TR

    def body(a_ref, b_ref, u_ref):
        u_ref[...] = _swiglu_fn(a_ref[...], b_ref[...]).astype(bf16)

    spec = pl.BlockSpec((TR, tf), lambda i, j: (i, j))
    return pl.pallas_call(body, name=name, grid=(cfg.nT, F // tf), in_specs=[spec, spec], out_specs=spec,
                          out_shape=_sds(a.shape, bf16), compiler_params=_params(("arbitrary", "arbitrary")))(a, b)


def swiglu_bwd(cfg, a, b, du, name):
    F = a.shape[1]
    tf = _pick(F, (1408, 1024, 512, 256, 128))
    TR = cfg.TR

    def body(a_ref, b_ref, du_ref, da_ref, db_ref):
        _, vjp = jax.vjp(_swiglu_fn, a_ref[...], b_ref[...])
        da, db = vjp(du_ref[...])
        da_ref[...] = da.astype(bf16)
        db_ref[...] = db.astype(bf16)

    spec = pl.BlockSpec((TR, tf), lambda i, j: (i, j))
    return pl.pallas_call(body, name=name, grid=(cfg.nT, F // tf), in_specs=[spec, spec, spec], out_specs=[spec, spec],
                          out_shape=[_sds(a.shape, bf16), _sds(a.shape, bf16)],
                          compiler_params=_params(("arbitrary", "arbitrary")))(a, b, du)


def _hosted(host, role, got, fn):
    comm = host.get(role) if host else None
    if comm is None:
        return fn(None)
    out, res = fn(comm)
    got[role] = res
    return out


def _grid_ends(grid):
    ids = [pl.program_id(n) for n in range(len(grid))]
    first = functools.reduce(jnp.logical_and, [i == 0 for i in ids])
    last = functools.reduce(jnp.logical_and, [i == n - 1 for i, n in zip(ids, grid)])
    return first, last


def _comm_plumbing(comm):
    if comm is None:
        return [], [], [], [], []
    n = len(comm.srcs)
    hbm = pl.BlockSpec(memory_space=pl.ANY)
    return [hbm] * n, [src for src, _ in comm.srcs], [hbm] * n, comm.out_shapes(), _comm_sems(n)


FFN_ROWS = 384


def ffn_gateup(cfg, h, wg, wu, name, comm=None):
    M, K = h.shape
    G, _, F = wg.shape
    tm = _pick(M, (FFN_ROWS, 256, 128))
    grid = (G, M // tm)
    ncomm = len(comm.srcs) if comm is not None else 0

    def body(h_ref, wg_ref, wu_ref, *rest):
        cin, rest = rest[:ncomm], rest[ncomm:]
        s_ref, t_ref, u_ref = rest[:3]
        cout, sems = rest[3:3 + ncomm], rest[3 + ncomm:]
        first, last = _grid_ends(grid)
        if ncomm:
            @pl.when(first)
            def _():
                _comm_start(comm, cin, cout, *sems)

        hv = h_ref[...]
        a = jnp.dot(hv, wg_ref[...], preferred_element_type=f32)
        b = jnp.dot(hv, wu_ref[...], preferred_element_type=f32)
        sig = jax.nn.sigmoid(a)
        sa = a * sig
        s_ref[...] = sa.astype(bf16)
        t_ref[...] = (b * (sig + sa * (1.0 - sig))).astype(bf16)
        u_ref[...] = (sa * b).astype(bf16)
        if ncomm:
            @pl.when(last)
            def _():
                _comm_wait(comm, cin, cout, *sems)

    ci, ca, co, cs, csem = _comm_plumbing(comm)
    w_spec = pl.BlockSpec((None, K, F), lambda g, i: (g, 0, 0))
    o_spec = pl.BlockSpec((tm, F), lambda g, i: (i, g))
    res = pl.pallas_call(
        body, name=name, grid=grid, in_specs=[pl.BlockSpec((tm, K), lambda g, i: (i, 0)), w_spec, w_spec] + ci,
        out_specs=[o_spec] * 3 + co, out_shape=[_sds((M, G * F), bf16)] * 3 + cs, scratch_shapes=csem,
        compiler_params=_params(("arbitrary", "arbitrary")))(h, wg, wu, *ca)
    return (res[0], res[1], res[2]), list(res[3:])


def ffn_du_act(cfg, dy, wd, s, t, name, comm=None):
    M, N = dy.shape
    G, F, _ = wd.shape
    tm = _pick(M, (FFN_ROWS, 256, 128))
    grid = (G, M // tm)
    ncomm = len(comm.srcs) if comm is not None else 0

    def body(dy_ref, wd_ref, s_ref, t_ref, *rest):
        cin, rest = rest[:ncomm], rest[ncomm:]
        da_ref, db_ref = rest[:2]
        cout, sems = rest[2:2 + ncomm], rest[2 + ncomm:]
        first, last = _grid_ends(grid)
        if ncomm:
            @pl.when(first)
            def _():
                _comm_start(comm, cin, cout, *sems)

        du = _dotf(dy_ref[...], wd_ref[...], 1, 1)
        da_ref[...] = (du * t_ref[...].astype(f32)).astype(bf16)
        db_ref[...] = (du * s_ref[...].astype(f32)).astype(bf16)
        if ncomm:
            @pl.when(last)
            def _():
                _comm_wait(comm, cin, cout, *sems)

    ci, ca, co, cs, csem = _comm_plumbing(comm)
    t_spec = pl.BlockSpec((tm, F), lambda g, i: (i, g))
    res = pl.pallas_call(
        body, name=name, grid=grid,
        in_specs=[pl.BlockSpec((tm, N), lambda g, i: (i, 0)), pl.BlockSpec((None, F, N), lambda g, i: (g, 0, 0)), t_spec,
                  t_spec] + ci,
        out_specs=[t_spec, t_spec] + co, out_shape=[_sds((M, G * F), bf16)] * 2 + cs, scratch_shapes=csem,
        compiler_params=_params(("arbitrary", "arbitrary")))(dy, wd, s, t, *ca)
    return (res[0], res[1]), list(res[2:])


def ffn_dh(cfg, da, db, wg, wu, name, comm=None):
    M = da.shape[0]
    G, K, F = wg.shape
    tm = _pick(M, (768, 512, 256, 128))
    nR = 2 * G
    grid = (M // tm, nR)
    ncomm = len(comm.srcs) if comm is not None else 0

    def body(da_ref, db_ref, wg_ref, wu_ref, *rest):
        cin, rest = rest[:ncomm], rest[ncomm:]
        o_ref = rest[0]
        cout, acc_ref, sems = rest[1:1 + ncomm], rest[1 + ncomm], rest[2 + ncomm:]
        r = pl.program_id(1)
        first, last = _grid_ends(grid)
        if ncomm:
            @pl.when(first)
            def _():
                _comm_start(comm, cin, cout, *sems)

        @pl.when(r == 0)
        def _():
            acc_ref[...] = _dotf(da_ref[...], wg_ref[...], 1, 1)

        @pl.when(jnp.logical_and(r > 0, r < G))
        def _():
            acc_ref[...] += _dotf(da_ref[...], wg_ref[...], 1, 1)

        @pl.when(jnp.logical_and(r >= G, r < nR - 1))
        def _():
            acc_ref[...] += _dotf(db_ref[...], wu_ref[...], 1, 1)

        @pl.when(r == nR - 1)
        def _():
            o_ref[...] = acc_ref[...] + _dotf(db_ref[...], wu_ref[...], 1, 1)

        if ncomm:
            @pl.when(last)
            def _():
                _comm_wait(comm, cin, cout, *sems)

    ga = lambda r: jnp.minimum(r, G - 1)
    gb = lambda r: jnp.maximum(r - G, 0)
    ci, ca, co, cs, csem = _comm_plumbing(comm)
    res = pl.pallas_call(
        body, name=name, grid=grid,
        in_specs=[pl.BlockSpec((tm, F), lambda i, r: (i, ga(r))), pl.BlockSpec((tm, F), lambda i, r: (i, gb(r))),
                  pl.BlockSpec((None, K, F), lambda i, r: (ga(r), 0, 0)),
                  pl.BlockSpec((None, K, F), lambda i, r: (gb(r), 0, 0))] + ci,
        out_specs=[pl.BlockSpec((tm, K), lambda i, r: (i, 0))] + co, out_shape=[_sds((M, K), f32)] + cs,
        scratch_shapes=[pltpu.VMEM((tm, K), f32)] + csem,
        compiler_params=_params(("arbitrary", "arbitrary")))(da, db, wg, wu, *ca)
    return res[0], list(res[1:])


def ffn_fwd(cfg, x, p, tag, host=None):
    got = {}
    host = host or {}
    h = pre_fwd(cfg, x, p["g_pre"], p["shift"], p["scale"], tag + "_pre")
    (a, b, u), res = ffn_gateup(cfg, h, p["wg"], p["wu"], tag + "_gateup", comm=host.get("gateup"))
    if res:
        got["gateup"] = res
    y = _hosted(host, "down", got, lambda cm: matmul("v2", u, p["wd"], comm=cm, name=tag + "_down"))
    xo = post_fwd(cfg, x, y, p["g_post"], p["gate"], FFN_STEP, tag + "_post")
    return xo, (x, h, a, b, u, y), got


def ffn_bwd(cfg, dX, saved, p, tag):
    x, h, a, b, u, y = saved
    dy, dg_post, dgate = post_bwd(cfg, dX, y, p["g_post"], p["gate"], FFN_STEP, tag + "_postb")
    (da, db), _ = ffn_du_act(cfg, dy, p["wd"], a, b, tag + "_duact")
    gwd = matmul("v6", u, dy, gshape=p["wd"].shape, out_dtype=bf16, name=tag + "_gwd")
    gwg = matmul("v5", h, da, gshape=p["wg"].shape, out_dtype=bf16, name=tag + "_gwg")
    gwu, r_wd = matmul("v5", h, db, gshape=p["wu"].shape, out_dtype=bf16, comm=Comm("scatter", [(gwd, ())]),
                       name=tag + "_gwu")
    dh, (r_wg, r_wu) = ffn_dh(cfg, da, db, p["wg"], p["wu"], tag + "_dh", comm=Comm("scatter", [(gwg, ()), (gwu, ())]))
    dX, dg_pre, dshift, dscale = pre_bwd(cfg, x, p["g_pre"], p["shift"], p["scale"], dh, dX, tag + "_preb")
    small = dict(g_pre=dg_pre, g_post=dg_post, shift=dshift, scale=dscale, gate=dgate)
    return dX, small, dict(wg=r_wg, wu=r_wu, wd=r_wd[0])


def _seg_flags(cfg, i):
    start = jnp.logical_or(i == 0, i == cfg.cT)
    end = jnp.logical_or(i == cfg.cT - 1, i == cfg.nT - 1)
    return start, end


def _fill_halo(buf, cur, prev, nxt, start, end, TR):
    buf[pl.ds(0, HALO), :] = jnp.where(start, 0.0, prev)
    buf[pl.ds(HALO, TR), :] = cur
    buf[pl.ds(HALO + TR, HALO), :] = jnp.where(end, 0.0, nxt)


def conv_fwd(cfg, z, cw, cb, W, name):
    TR = cfg.TR

    def fn(i, r_ref, rp_ref, rn_ref, cw_ref, cb_ref, u_ref, buf):
        start, end = _seg_flags(cfg, i)
        _fill_halo(buf, r_ref[...], rp_ref[...], rn_ref[...], start, end, TR)
        u = jnp.broadcast_to(cb_ref[...], (TR, W))
        for k in range(CONV_W):
            u = u + buf[pl.ds(HALO + k - CONV_LEFT, TR), :] * cw_ref[pl.ds(k, 1), :]
        u_ref[...] = u

    return rowcall(cfg, fn, name, [("row", z, W, 1), ("prev", z, W, 1), ("next", z, W, 1), ("full", cw), ("full", cb)],
                   [("row", _sds((z.shape[0], W), f32), W, 0)], scratch=[pltpu.VMEM((TR + 2 * HALO, W), f32)])[0]


def conv_bwd(cfg, z, du, cw, W, name):
    TR = cfg.TR

    def fn(i, r_ref, rp_ref, rn_ref, du_ref, dup_ref, dun_ref, cw_ref, dr_ref, dcw_ref, dcb_ref, rbuf, dbuf):
        start, end = _seg_flags(cfg, i)
        _fill_halo(rbuf, r_ref[...], rp_ref[...], rn_ref[...], start, end, TR)
        _fill_halo(dbuf, du_ref[...], dup_ref[...], dun_ref[...], start, end, TR)
        du = du_ref[...]

        @pl.when(i == 0)
        def _():
            dcw_ref[...] = jnp.zeros(dcw_ref.shape, f32)
            dcb_ref[...] = jnp.zeros(dcb_ref.shape, f32)

        dr = jnp.zeros((TR, W), f32)
        for k in range(CONV_W):
            dr = dr + dbuf[pl.ds(HALO - (k - CONV_LEFT), TR), :] * cw_ref[pl.ds(k, 1), :]
            dcw_ref[pl.ds(k, 1), :] += jnp.sum(du * rbuf[pl.ds(HALO + k - CONV_LEFT, TR), :], axis=0, keepdims=True)
        dcb_ref[...] += jnp.sum(du, axis=0, keepdims=True)
        dr_ref[...] = dr

    T = z.shape[0]
    return rowcall(cfg, fn, name,
                   [("row", z, W, 1), ("prev", z, W, 1), ("next", z, W, 1), ("row", du, W, 0), ("prev", du, W, 0),
                    ("next", du, W, 0), ("full", cw)],
                   [("row", _sds((T, W), f32), W, 0), ("full", _sds((CONV_W, W), f32)), ("full", _sds((1, W), f32))],
                   scratch=[pltpu.VMEM((TR + 2 * HALO, W), f32), pltpu.VMEM((TR + 2 * HALO, W), f32)])


def _softplus(x):
    return jnp.maximum(x, 0.0) + jnp.log1p(jnp.exp(-jnp.abs(x)))


def _neg_expm1(x):
    series = -x * (1.0 + x * (0.5 + x * (1.0 / 6.0 + x * (1.0 / 24.0 + x * (1.0 / 120.0 + x * (1.0 / 720.0))))))
    return jnp.where(x > -0.1, series, 1.0 - jnp.exp(x))


def _lru_coef(u, pa, px, lam):
    r = jax.nn.sigmoid(pa)
    i = jax.nn.sigmoid(px)
    log_a = -LRU_C * r * _softplus(-lam)
    a = jnp.exp(log_a)
    b = jnp.sqrt(_neg_expm1(2.0 * log_a)) * (i * u)
    return a, b


def _blockdiag(u_bf, w_ref, d, nblk, blk):
    return jnp.concatenate(
        [jnp.dot(u_bf[:, n * blk:(n + 1) * blk], w_ref[d, n].astype(bf16), preferred_element_type=f32)
         for n in range(nblk)], axis=1)


def lru_coef_fwd(cfg, u, wa, ba, wx, bx, lam, name):
    T, W = u.shape
    nblk, blk = wa.shape[1], wa.shape[2]

    def fn(i, u_ref, wa_ref, ba_ref, wx_ref, bx_ref, lam_ref, a_ref, b_ref):
        uv = u_ref[...]
        u_bf = uv.astype(bf16)
        for d in range(2):
            pa = _blockdiag(u_bf, wa_ref, d, nblk, blk) + ba_ref[d]
            px = _blockdiag(u_bf, wx_ref, d, nblk, blk) + bx_ref[d]
            a, b = _lru_coef(uv, pa, px, lam_ref[d])
            a_ref[d] = a
            b_ref[d] = b

    return rowcall(cfg, fn, name, [("row", u, W, 0), ("full", wa), ("full", ba), ("full", wx), ("full", bx), ("full", lam)],
                   [("drow", _sds((2, T, W), f32), W, 0), ("drow", _sds((2, T, W), f32), W, 0)])


def lru_coef_bwd(cfg, u, da, db, wa, ba, wx, bx, lam, name):
    T, W = u.shape
    nblk, blk = wa.shape[1], wa.shape[2]

    def fn(i, u_ref, da_ref, db_ref, wa_ref, ba_ref, wx_ref, bx_ref, lam_ref,
           du_ref, dwa_ref, dba_ref, dwx_ref, dbx_ref, dlam_ref):
        @pl.when(i == 0)
        def _():
            for r in (dwa_ref, dba_ref, dwx_ref, dbx_ref, dlam_ref):
                r[...] = jnp.zeros(r.shape, f32)

        uv = u_ref[...]
        u_bf = uv.astype(bf16)
        du = jnp.zeros(uv.shape, f32)
        for d in range(2):
            pa = _blockdiag(u_bf, wa_ref, d, nblk, blk) + ba_ref[d]
            px = _blockdiag(u_bf, wx_ref, d, nblk, blk) + bx_ref[d]
            _, vjp = jax.vjp(_lru_coef, uv, pa, px, lam_ref[d])
            du_e, dpa, dpx, dlam = vjp((da_ref[d], db_ref[d]))
            du = du + du_e
            dba_ref[d] += jnp.sum(dpa, axis=0, keepdims=True)
            dbx_ref[d] += jnp.sum(dpx, axis=0, keepdims=True)
            dlam_ref[d] += dlam
            parts = []
            for n in range(nblk):
                sl = slice(n * blk, (n + 1) * blk)
                ga, gx = dpa[:, sl].astype(bf16), dpx[:, sl].astype(bf16)
                ub = u_bf[:, sl]
                dwa_ref[d, n] += lax.dot_general(ub, ga, (((0,), (0,)), ((), ())), preferred_element_type=f32)
                dwx_ref[d, n] += lax.dot_general(ub, gx, (((0,), (0,)), ((), ())), preferred_element_type=f32)
                parts.append(
                    lax.dot_general(ga, wa_ref[d, n].astype(bf16), (((1,), (1,)), ((), ())), preferred_element_type=f32)
                    + lax.dot_general(gx, wx_ref[d, n].astype(bf16), (((1,), (1,)), ((), ())), preferred_element_type=f32))
            du = du + jnp.concatenate(parts, axis=1)
        du_ref[...] = du

    return rowcall(cfg, fn, name,
                   [("row", u, W, 0), ("drow", da, W, 0), ("drow", db, W, 0), ("full", wa), ("full", ba), ("full", wx),
                    ("full", bx), ("full", lam)],
                   [("row", _sds((T, W), f32), W, 0), ("full", _sds(wa.shape, f32)), ("full", _sds(ba.shape, f32)),
                    ("full", _sds(wx.shape, f32)), ("full", _sds(bx.shape, f32)), ("full", _sds(lam.shape, f32))])


def _dir_tile(cfg, d, j):
    rev = jnp.where(j < cfg.cT, cfg.cT - 1 - j, cfg.nT - 1 - (j - cfg.cT))
    return jnp.where(d == 0, j, rev)


def lru_scan(cfg, a, b, name):
    _, T, W = a.shape
    TR, nT = cfg.TR, cfg.nT

    def body(a_ref, b_ref, h_ref, hp_ref, st):
        d, j = pl.program_id(0), pl.program_id(1)

        @pl.when(j == 0)
        def _():
            st[...] = jnp.zeros(st.shape, f32)

        def step(t, h):
            idx = t + d * (TR - 1 - 2 * t)
            hn = a_ref[pl.ds(idx, 1), :] * h + b_ref[pl.ds(idx, 1), :]
            hp_ref[pl.ds(idx, 1), :] = h
            h_ref[pl.ds(idx, 1), :] = hn
            return hn

        st[...] = lax.fori_loop(0, TR, step, st[...])

    spec = pl.BlockSpec((None, TR, W), lambda d, j: (d, _dir_tile(cfg, d, j), 0))
    return pl.pallas_call(body, name=name, grid=(2, nT), in_specs=[spec, spec], out_specs=[spec, spec],
                          out_shape=[_sds(a.shape, f32), _sds(a.shape, f32)], scratch_shapes=[pltpu.VMEM((1, W), f32)],
                          compiler_params=_params(("arbitrary", "arbitrary")))(a, b)


def lru_scan_bwd(cfg, a, hp, dh, name):
    _, T, W = a.shape
    TR, nT = cfg.TR, cfg.nT

    def body(a_ref, hp_ref, dh_ref, da_ref, db_ref, st):
        d, j = pl.program_id(0), pl.program_id(1)

        @pl.when(j == 0)
        def _():
            st[...] = jnp.zeros(st.shape, f32)

        def step(t, c):
            p = TR - 1 - t
            idx = p + d * (TR - 1 - 2 * p)
            g = dh_ref[pl.ds(idx, 1), :] + c
            db_ref[pl.ds(idx, 1), :] = g
            da_ref[pl.ds(idx, 1), :] = g * hp_ref[pl.ds(idx, 1), :]
            return a_ref[pl.ds(idx, 1), :] * g

        st[...] = lax.fori_loop(0, TR, step, st[...])

    spec = pl.BlockSpec((None, TR, W), lambda d, j: (d, _dir_tile(cfg, d, nT - 1 - j), 0))
    dspec = pl.BlockSpec((TR, W), lambda d, j: (_dir_tile(cfg, d, nT - 1 - j), 0))
    return pl.pallas_call(body, name=name, grid=(2, nT), in_specs=[spec, spec, dspec], out_specs=[spec, spec],
                          out_shape=[_sds(a.shape, f32), _sds(a.shape, f32)], scratch_shapes=[pltpu.VMEM((1, W), f32)],
                          compiler_params=_params(("arbitrary", "arbitrary")))(a, hp, dh)


def _lru_out_fn(gl, h0, h1):
    return jax.nn.gelu(gl) * (h0 + h1)


def lru_out_fwd(cfg, z, h, W, name):
    def fn(i, g_ref, h_ref, o_ref):
        o_ref[...] = _lru_out_fn(g_ref[...], h_ref[0], h_ref[1]).astype(bf16)

    return rowcall(cfg, fn, name, [("row", z, W, 0), ("drow", h, W, 0)], [("row", _sds((z.shape[0], W), bf16), W, 0)])[0]


def lru_out_bwd(cfg, z, h, dmix, W, name):
    def fn(i, g_ref, h_ref, d_ref, dg_ref, dh_ref):
        _, vjp = jax.vjp(_lru_out_fn, g_ref[...], h_ref[0], h_ref[1])
        dg, dh0, _ = vjp(d_ref[...])
        dg_ref[...] = dg
        dh_ref[...] = dh0

    T = z.shape[0]
    return rowcall(cfg, fn, name, [("row", z, W, 0), ("drow", h, W, 0), ("row", dmix, W, 0)],
                   [("row", _sds((T, W), f32), W, 0), ("row", _sds((T, W), f32), W, 0)])


def _rot_half(x, cos, sin):
    n = x.shape[1] // 2
    x1, x2 = x[:, :n], x[:, n:]
    return jnp.concatenate([x1 * cos - x2 * sin, x1 * sin + x2 * cos], axis=1)


def _dotf(a, b, ca, cb):
    return lax.dot_general(a, b, (((ca,), (cb,)), ((), ())), preferred_element_type=f32)


@functools.partial(jax.custom_vjp, nondiff_argnums=(2, 3))
def _dotb(a, b, ca, cb):
    return _dotf(a.astype(bf16), b.astype(bf16), ca, cb)


def _dotb_fwd(a, b, ca, cb):
    return _dotb(a, b, ca, cb), (a, b)


def _dotb_bwd(ca, cb, res, ct):
    a, b = res
    a16, b16, ct16 = a.astype(bf16), b.astype(bf16), ct.astype(bf16)
    da = _dotf(ct16, b16, 1, 1 - cb) if ca == 1 else _dotf(b16, ct16, 1 - cb, 1)
    db = _dotf(a16, ct16, 1 - ca, 0) if cb == 0 else _dotf(ct16, a16, 0, 1 - ca)
    return da, db


_dotb.defvjp(_dotb_fwd, _dotb_bwd)


def _ret_chunk(d, q, k, v, s, logit, cos, sin):
    C = q.shape[0]
    lg = -_softplus(-logit)
    qr = _rot_half(q, cos, sin)
    kr = _rot_half(k, cos, sin) * (RET_DK ** -0.5)
    ii = lax.broadcasted_iota(jnp.int32, (C, C), 0)
    jj = lax.broadcasted_iota(jnp.int32, (C, C), 1)
    diff = ((ii - jj) if d == 0 else (jj - ii)).astype(f32)
    intra = jnp.where(diff >= 0, jnp.exp(lg * jnp.maximum(diff, 0.0)), 0.0)
    pos = lax.broadcasted_iota(jnp.int32, (C, 1), 0).astype(f32)
    if d == 0:
        q_dec, k_dec = jnp.exp(lg * (pos + 1.0)), jnp.exp(lg * (C - 1.0 - pos))
    else:
        q_dec, k_dec = jnp.exp(lg * (C - pos)), jnp.exp(lg * pos)
    s_dec = jnp.exp(lg * C)
    scores = _dotb(qr, kr, 1, 1) * intra
    o = _dotb(scores, v, 1, 0) + _dotb(qr * q_dec, s, 1, 0)
    s_new = s * s_dec + _dotb(kr * k_dec, v, 0, 0)
    return o, s_new


def _chunk_cfg(cfg):
    f = cfg.TR // RET_CHUNK
    return RowCfg(RET_CHUNK, cfg.nT * f, cfg.cT * f)


def ret_fwd(cfg, z, logit, cos, sin, H, qcol, name):
    T = z.shape[0]
    cc = _chunk_cfg(cfg)
    C, nC = RET_CHUNK, cc.nT
    RV = H * RET_DV
    qb = qcol * RET_DK // RV

    def body(q_ref, k_ref, v_ref, lg_ref, cos_ref, sin_ref, o_ref, s_ref, st):
        d, j = pl.program_id(0), pl.program_id(1)

        @pl.when(j == 0)
        def _():
            st[...] = jnp.zeros(st.shape, f32)

        s_ref[...] = st[...]
        for dd in range(2):
            @pl.when(d == dd)
            def _():
                outs = []
                for h in range(H):
                    cols = slice(h * RET_DK, (h + 1) * RET_DK)
                    o, s_new = _ret_chunk(dd, q_ref[:, cols], k_ref[:, cols], v_ref[:, cols], st[h], lg_ref[h],
                                          cos_ref[...], sin_ref[...])
                    outs.append(o)
                    st[h] = s_new
                o_ref[...] = jnp.concatenate(outs, axis=1)

    tile = lambda d, j: _dir_tile(cc, d, j)
    zq = pl.BlockSpec((C, RV), lambda d, j: (tile(d, j), qb))
    zk = pl.BlockSpec((C, RV), lambda d, j: (tile(d, j), qb + 1))
    zv = pl.BlockSpec((C, RV), lambda d, j: (tile(d, j), qb + 2))
    lgs = pl.BlockSpec((None, H, 1, 1), lambda d, j: (d, 0, 0, 0))
    cs = pl.BlockSpec((C, RET_DK // 2), lambda d, j: (tile(d, j), 0))
    o_spec = pl.BlockSpec((None, C, RV), lambda d, j: (d, tile(d, j), 0))
    s_spec = pl.BlockSpec((None, H, None, RET_DK, RET_DV), lambda d, j: (d, 0, tile(d, j), 0, 0))
    return pl.pallas_call(
        body, name=name, grid=(2, nC), in_specs=[zq, zk, zv, lgs, cs, cs], out_specs=[o_spec, s_spec],
        out_shape=[_sds((2, T, RV), f32), _sds((2, H, nC, RET_DK, RET_DV), f32)],
        scratch_shapes=[pltpu.VMEM((H, RET_DK, RET_DV), f32)],
        compiler_params=_params(("arbitrary", "arbitrary")))(z, z, z, logit, cos, sin)


def ret_bwd(cfg, z, states, do, logit, cos, sin, H, qcol, name):
    T = z.shape[0]
    cc = _chunk_cfg(cfg)
    C, nC = RET_CHUNK, cc.nT
    RV = H * RET_DV
    qb = qcol * RET_DK // RV

    def body(q_ref, k_ref, v_ref, s_ref, do_ref, lg_ref, cos_ref, sin_ref, dq_ref, dk_ref, dv_ref, dlg_ref, st):
        d, j = pl.program_id(0), pl.program_id(1)

        @pl.when(j == 0)
        def _():
            st[...] = jnp.zeros(st.shape, f32)
            dlg_ref[...] = jnp.zeros(dlg_ref.shape, f32)

        for dd in range(2):
            @pl.when(d == dd)
            def _():
                fn = lambda q, k, v, s, lg: _ret_chunk(dd, q, k, v, s, lg, cos_ref[...], sin_ref[...])
                dqs, dks, dvs = [], [], []
                for h in range(H):
                    cols = slice(h * RET_DK, (h + 1) * RET_DK)
                    _, vjp = jax.vjp(fn, q_ref[:, cols], k_ref[:, cols], v_ref[:, cols], s_ref[h], lg_ref[h])
                    dq, dk, dv, ds, dlg = vjp((do_ref[:, cols], st[h]))
                    dqs.append(dq)
                    dks.append(dk)
                    dvs.append(dv)
                    st[h] = ds
                    dlg_ref[h] += dlg
                dq_ref[...] = jnp.concatenate(dqs, axis=1)
                dk_ref[...] = jnp.concatenate(dks, axis=1)
                dv_ref[...] = jnp.concatenate(dvs, axis=1)

    tile = lambda d, j: _dir_tile(cc, d, nC - 1 - j)
    zq = pl.BlockSpec((C, RV), lambda d, j: (tile(d, j), qb))
    zk = pl.BlockSpec((C, RV), lambda d, j: (tile(d, j), qb + 1))
    zv = pl.BlockSpec((C, RV), lambda d, j: (tile(d, j), qb + 2))
    s_spec = pl.BlockSpec((None, H, None, RET_DK, RET_DV), lambda d, j: (d, 0, tile(d, j), 0, 0))
    do_spec = pl.BlockSpec((C, RV), lambda d, j: (tile(d, j), 0))
    lgs = pl.BlockSpec((None, H, 1, 1), lambda d, j: (d, 0, 0, 0))
    cs = pl.BlockSpec((C, RET_DK // 2), lambda d, j: (tile(d, j), 0))
    g_spec = pl.BlockSpec((None, C, RV), lambda d, j: (d, tile(d, j), 0))
    gshape = _sds((2, T, RV), f32)
    return pl.pallas_call(
        body, name=name, grid=(2, nC), in_specs=[zq, zk, zv, s_spec, do_spec, lgs, cs, cs],
        out_specs=[g_spec, g_spec, g_spec, lgs], out_shape=[gshape, gshape, gshape, _sds((2, H, 1, 1), f32)],
        scratch_shapes=[pltpu.VMEM((H, RET_DK, RET_DV), f32)],
        compiler_params=_params(("arbitrary", "arbitrary")))(z, z, z, states, do, logit, cos, sin)


def _ret_norm_fn(H, o0, o1, ol, gn):
    o = o0 + o1
    parts = []
    for h in range(H):
        x = o[:, h * RET_DV:(h + 1) * RET_DV]
        mu = jnp.mean(x, axis=-1, keepdims=True)
        var = jnp.mean(jnp.square(x - mu), axis=-1, keepdims=True)
        parts.append((x - mu) * lax.rsqrt(var + EPS))
    return (jnp.concatenate(parts, axis=1) * gn) * jax.nn.silu(ol)


def ret_norm_fwd(cfg, o, z, gn, H, olcol, name):
    RV = H * RET_DV

    def fn(i, o_ref, ol_ref, gn_ref, r_ref):
        r_ref[...] = _ret_norm_fn(H, o_ref[0], o_ref[1], ol_ref[...], gn_ref[...]).astype(bf16)

    return rowcall(cfg, fn, name, [("drow", o, RV, 0), ("row", z, RV, olcol), ("full", gn)],
                   [("row", _sds((z.shape[0], RV), bf16), RV, 0)])[0]


def ret_norm_bwd(cfg, o, z, gn, dmix, H, olcol, dcol, name):
    RV = H * RET_DV
    T = z.shape[0]

    def fn(i, o_ref, ol_ref, gn_ref, d_ref, do_ref, dol_ref, dgn_ref):
        _, vjp = jax.vjp(functools.partial(_ret_norm_fn, H), o_ref[0], o_ref[1], ol_ref[...], gn_ref[...])
        do, _, dol, dgn = vjp(d_ref[...])
        do_ref[...] = do
        dol_ref[...] = dol
        _acc(dgn_ref, dgn, i == 0)

    return rowcall(cfg, fn, name, [("drow", o, RV, 0), ("row", z, RV, olcol), ("full", gn), ("row", dmix, RV, dcol)],
                   [("row", _sds((T, RV), f32), RV, 0), ("row", _sds((T, RV), f32), RV, 0), ("full", _sds((1, RV), f32))])


def _pool_geom(cfg, i, w, L):
    t = (i - cfg.cT) * cfg.TR + lax.broadcasted_iota(jnp.int32, (cfg.TR, 1), 0)
    lo = jnp.clip(t - w // 2, 0, L)
    hi = jnp.clip(t + w // 2, 0, L)
    return (hi - lo).astype(f32)


def _pool_centred(cfg, i, buf, gi, w, L):
    TR, G = cfg.TR, POOL_GROUP
    cols = pl.ds(gi * G, G)
    tot = buf[pl.ds(HALO - w // 2, TR), cols]
    for s in range(-w // 2 + 1, w // 2):
        tot = tot + buf[pl.ds(HALO + s, TR), cols]
    cnt = _pool_geom(cfg, i, w, L)
    return tot / cnt - buf[pl.ds(HALO, TR), cols], cnt


def pool_fwd(cfg, z, pw, ps, name):
    T = z.shape[0]
    TR, cT = cfg.TR, cfg.cT
    P = POOL_GROUP * len(POOL_WINDOWS)
    L = T - cT * TR

    def fn(i, x_ref, xp_ref, xn_ref, pw_ref, ps_ref, o_ref, buf):
        @pl.when(i < cT)
        def _():
            o_ref[...] = jnp.zeros(o_ref.shape, bf16)

        @pl.when(i >= cT)
        def _():
            start, end = _seg_flags(cfg, i)
            _fill_halo(buf, x_ref[...], xp_ref[...], xn_ref[...], start, end, TR)
            outs = []
            for gi, w in enumerate(POOL_WINDOWS):
                m, _ = _pool_centred(cfg, i, buf, gi, w, L)
                outs.append(jnp.dot(m.astype(bf16), pw_ref[gi].astype(bf16), preferred_element_type=f32))
            o_ref[...] = (jnp.concatenate(outs, axis=1) * ps_ref[...]).astype(bf16)

    return rowcall(cfg, fn, name, [("row", z, P, 0), ("prev", z, P, 0), ("next", z, P, 0), ("full", pw), ("full", ps)],
                   [("row", _sds((T, P), bf16), P, 0)], scratch=[pltpu.VMEM((TR + 2 * HALO, P), f32)])[0]


def pool_bwd_a(cfg, z, dmix, pw, ps, name):
    T = z.shape[0]
    TR, cT = cfg.TR, cfg.cT
    G = POOL_GROUP
    P = G * len(POOL_WINDOWS)
    L = T - cT * TR

    def fn(i, x_ref, xp_ref, xn_ref, d_ref, pw_ref, ps_ref, dm_ref, dmn_ref, dpw_ref, dps_ref, buf):
        @pl.when(i == 0)
        def _():
            dpw_ref[...] = jnp.zeros(dpw_ref.shape, f32)
            dps_ref[...] = jnp.zeros(dps_ref.shape, f32)

        @pl.when(i < cT)
        def _():
            dm_ref[...] = jnp.zeros(dm_ref.shape, f32)
            dmn_ref[...] = jnp.zeros(dmn_ref.shape, f32)

        @pl.when(i >= cT)
        def _():
            start, end = _seg_flags(cfg, i)
            _fill_halo(buf, x_ref[...], xp_ref[...], xn_ref[...], start, end, TR)
            dout = d_ref[...]
            dpre = dout * ps_ref[...]
            pres, dms, dmns = [], [], []
            for gi, w in enumerate(POOL_WINDOWS):
                m, cnt = _pool_centred(cfg, i, buf, gi, w, L)
                m_bf = m.astype(bf16)
                w_bf = pw_ref[gi].astype(bf16)
                pres.append(jnp.dot(m_bf, w_bf, preferred_element_type=f32))
                g_bf = dpre[:, gi * G:(gi + 1) * G].astype(bf16)
                dpw_ref[gi] += _dotf(m_bf, g_bf, 0, 0)
                dm = _dotf(g_bf, w_bf, 1, 1)
                dms.append(dm)
                dmns.append(dm / cnt)
            dps_ref[...] += jnp.sum(dout * jnp.concatenate(pres, axis=1), axis=0, keepdims=True)
            dm_ref[...] = jnp.concatenate(dms, axis=1)
            dmn_ref[...] = jnp.concatenate(dmns, axis=1)

    return rowcall(cfg, fn, name,
                   [("row", z, P, 0), ("prev", z, P, 0), ("next", z, P, 0), ("row", dmix, P, 0), ("full", pw), ("full", ps)],
                   [("row", _sds((T, P), f32), P, 0), ("row", _sds((T, P), f32), P, 0), ("full", _sds(pw.shape, f32)),
                    ("full", _sds((1, P), f32))], scratch=[pltpu.VMEM((TR + 2 * HALO, P), f32)])


def pool_bwd_b(cfg, dm, dmn, name):
    T, P = dm.shape
    TR, cT = cfg.TR, cfg.cT
    G = POOL_GROUP

    def fn(i, dm_ref, c_ref, p_ref, n_ref, dx_ref, buf):
        start, end = _seg_flags(cfg, i)
        _fill_halo(buf, c_ref[...], p_ref[...], n_ref[...], start, end, TR)
        outs = []
        for gi, w in enumerate(POOL_WINDOWS):
            cols = pl.ds(gi * G, G)
            tot = buf[pl.ds(HALO + w // 2, TR), cols]
            for s in range(-w // 2 + 1, w // 2):
                tot = tot + buf[pl.ds(HALO + s, TR), cols]
            outs.append(tot)
        dx_ref[...] = jnp.concatenate(outs, axis=1) - dm_ref[...]

    return rowcall(cfg, fn, name, [("row", dm, P, 0), ("row", dmn, P, 0), ("prev", dmn, P, 0), ("next", dmn, P, 0)],
                   [("row", _sds((T, P), f32), P, 0)], scratch=[pltpu.VMEM((TR + 2 * HALO, P), f32)])[0]


def _swap_halves(x):
    return pltpu.roll(x, HEAD_DIM // 2, 1)


def _headnorm(x, g):
    return _rms(x) * g


def att_prep(cfg, z, qg, kg, cosf, sinf, nq, name):
    T = z.shape[0]
    U = z.shape[1] // (nq + 3)
    nh = U // HEAD_DIM

    def fn(i, *refs):
        q_refs = refs[:nq]
        k_ref, v_ref, qg_ref, kg_ref, cos_ref, sin_ref, qn_ref, kn_ref, vb_ref = refs[nq:]
        cosv, sinv = cos_ref[...], sin_ref[...]

        def heads(x, g):
            outs = []
            for h in range(nh):
                y = _headnorm(x[:, h * HEAD_DIM:(h + 1) * HEAD_DIM], g)
                outs.append(y * cosv + _swap_halves(y) * sinv)
            return jnp.concatenate(outs, axis=1)

        qn_ref[...] = jnp.concatenate([heads(r[...], qg_ref[...]) for r in q_refs], axis=1).astype(bf16)
        kn_ref[...] = heads(k_ref[...], kg_ref[...]).astype(bf16)
        vb_ref[...] = v_ref[...].astype(bf16)

    ins = [("row", z, U, 1 + n) for n in range(nq)] + [("row", z, U, nq + 1), ("row", z, U, nq + 2), ("full", qg),
                                                       ("full", kg), ("row", cosf, HEAD_DIM, 0), ("row", sinf, HEAD_DIM, 0)]
    return rowcall(cfg, fn, name, ins, [("row", _sds((T, nq * U), bf16), nq * U, 0), ("row", _sds((T, U), bf16), U, 0),
                                        ("row", _sds((T, U), bf16), U, 0)])


def att_prep_bwd(cfg, z, qg, kg, cosf, sinf, dqn, dkn, dvb, dxpool, nq, name):
    T = z.shape[0]
    U = z.shape[1] // (nq + 3)
    nh = U // HEAD_DIM
    cT = cfg.cT

    def fn(i, *refs):
        q_refs = refs[:nq]
        (k_ref, qg_ref, kg_ref, cos_ref, sin_ref, dqn_ref, dkn_ref, dvb_ref, dxp_ref, dz_ref, dqg_ref, dkg_ref) = refs[nq:]
        cosv, sinv = cos_ref[...], sin_ref[...]

        @pl.when(i == 0)
        def _():
            dqg_ref[...] = jnp.zeros(dqg_ref.shape, f32)
            dkg_ref[...] = jnp.zeros(dkg_ref.shape, f32)

        def heads_bwd(x, g, dy, dg_ref):
            outs = []
            for h in range(nh):
                sl = slice(h * HEAD_DIM, (h + 1) * HEAD_DIM)
                d = dy[:, sl]
                dn = d * cosv + _swap_halves(d * sinv)
                _, vjp = jax.vjp(_headnorm, x[:, sl], g)
                dx, dg = vjp(dn)
                dg_ref[...] += dg
                outs.append(dx)
            return jnp.concatenate(outs, axis=1)

        dk = heads_bwd(k_ref[...], kg_ref[...], dkn_ref[...], dkg_ref)
        tail = [dk.astype(bf16), dvb_ref[...].astype(bf16)]

        @pl.when(i < cT)
        def _():
            zeros = jnp.zeros((cfg.TR, (nq + 1) * U), bf16)
            dz_ref[...] = jnp.concatenate([zeros] + tail, axis=1)

        @pl.when(i >= cT)
        def _():
            dq = [heads_bwd(r[...], qg_ref[...], dqn_ref[:, n * U:(n + 1) * U], dqg_ref) for n, r in enumerate(q_refs)]
            dz_ref[...] = jnp.concatenate([dxp_ref[...].astype(bf16)] + [t.astype(bf16) for t in dq] + tail, axis=1)

    ins = ([("row", z, U, 1 + n) for n in range(nq)] +
           [("row", z, U, nq + 1), ("full", qg), ("full", kg), ("row", cosf, HEAD_DIM, 0), ("row", sinf, HEAD_DIM, 0),
            ("row", dqn, nq * U, 0), ("row", dkn, U, 0), ("row", dvb, U, 0), ("row", dxpool, U, 0)])
    W = (nq + 3) * U
    return rowcall(cfg, fn, name, ins, [("row", _sds((T, W), bf16), W, 0), ("full", _sds((1, HEAD_DIM), f32)),
                                        ("full", _sds((1, HEAD_DIM), f32))])


def _stack_heads(x, n):
    return jnp.concatenate([x[:, h * HEAD_DIM:(h + 1) * HEAD_DIM] for h in range(n)], axis=0)


def _unstack_heads(x, n):
    rows = x.shape[0] // n
    return jnp.concatenate([x[h * rows:(h + 1) * rows] for h in range(n)], axis=1)


def _att_tiles(cfg, T):
    tq = cfg.TR
    tk = _pick(T, (4224, 2816, 1408, 768, 512, 256, 128))
    return tq, tk, (T - cfg.cT * cfg.TR) // tq, T // tk


LOG2E = 1.4426950408889634


def att_fwd(cfg, qn, kn, vb, nq, name):
    T, U = kn.shape
    KV = U // HEAD_DIM
    tq, tk, nQ, nK = _att_tiles(cfg, T)
    scale = HEAD_DIM ** -0.5
    c2 = scale * LOG2E
    R = nq * tq

    def body(q_ref, k_ref, v_ref, o_ref, lse_ref, *scratch):
        ik = pl.program_id(2)
        m_sc, l_sc, acc = scratch[:nq], scratch[nq:2 * nq], scratch[2 * nq:]

        @pl.when(ik == 0)
        def _():
            for h in range(nq):
                m_sc[h][...] = jnp.full(m_sc[h].shape, -jnp.inf, f32)
                l_sc[h][...] = jnp.zeros(l_sc[h].shape, f32)
                acc[h][...] = jnp.zeros(acc[h].shape, f32)

        k, v = k_ref[...], v_ref[...]
        for h in range(nq):
            s = _dotf(q_ref[:, h * HEAD_DIM:(h + 1) * HEAD_DIM], k, 1, 1)
            m_old = m_sc[h][...]
            m_new = jnp.maximum(m_old, jnp.max(s, axis=-1, keepdims=True))
            alpha = jnp.exp2((m_old - m_new) * c2)
            p = jnp.exp2((s - m_new) * c2)
            l_sc[h][...] = alpha * l_sc[h][...] + jnp.sum(p, axis=-1, keepdims=True)
            acc[h][...] = alpha * acc[h][...] + jnp.dot(p.astype(bf16), v, preferred_element_type=f32)
            m_sc[h][...] = m_new

        @pl.when(ik == nK - 1)
        def _():
            o_ref[...] = jnp.concatenate([acc[h][...] / l_sc[h][...] for h in range(nq)], axis=1)
            lse_ref[...] = jnp.concatenate([m_sc[h][...] * scale + jnp.log(l_sc[h][...]) for h in range(nq)], axis=0)

    W = nq * HEAD_DIM
    q_spec = pl.BlockSpec((tq, W), lambda h, i, k: (i + cfg.cT, h))
    kv_spec = pl.BlockSpec((tk, HEAD_DIM), lambda h, i, k: (k, h))
    lse_spec = pl.BlockSpec((None, None, R, 1), lambda h, i, k: (h, i, 0, 0))
    col = [pltpu.VMEM((tq, 1), f32)] * nq
    return pl.pallas_call(
        body, name=name, grid=(KV, nQ, nK), in_specs=[q_spec, kv_spec, kv_spec], out_specs=[q_spec, lse_spec],
        out_shape=[_sds((T, nq * U), f32), _sds((KV, nQ, R, 1), f32)],
        scratch_shapes=col + col + [pltpu.VMEM((tq, HEAD_DIM), f32)] * nq,
        compiler_params=_params(("arbitrary", "arbitrary", "arbitrary")))(qn, kn, vb)


def att_bwd_dq(cfg, qn, kn, vb, o, lse, do, nq, name):
    T, U = kn.shape
    KV = U // HEAD_DIM
    tq, tk, nQ, nK = _att_tiles(cfg, T)
    scale = HEAD_DIM ** -0.5
    c2 = scale * LOG2E
    R = nq * tq
    W = nq * HEAD_DIM

    def body(q_ref, k_ref, v_ref, o_ref, lse_ref, do_ref, dq_ref, acc, dl):
        ik = pl.program_id(2)

        @pl.when(ik == 0)
        def _():
            acc[...] = jnp.zeros(acc.shape, f32)
            dl[...] = jnp.sum(_stack_heads(do_ref[...] * o_ref[...], nq), axis=-1, keepdims=True)

        k, v = k_ref[...], v_ref[...]
        for h in range(nq):
            rows = pl.ds(h * tq, tq)
            cols = slice(h * HEAD_DIM, (h + 1) * HEAD_DIM)
            s = _dotf(q_ref[:, cols], k, 1, 1)
            p = jnp.exp2(s * c2 - lse_ref[rows, :] * LOG2E)
            dp = _dotf(do_ref[:, cols].astype(bf16), v, 1, 1)
            ds = (p * (dp - dl[rows, :]) * scale).astype(bf16)
            acc[rows, :] += jnp.dot(ds, k, preferred_element_type=f32)

        @pl.when(ik == nK - 1)
        def _():
            dq_ref[...] = _unstack_heads(acc[...], nq)

    q_spec = pl.BlockSpec((tq, W), lambda h, i, k: (i + cfg.cT, h))
    kv_spec = pl.BlockSpec((tk, HEAD_DIM), lambda h, i, k: (k, h))
    lse_spec = pl.BlockSpec((None, None, R, 1), lambda h, i, k: (h, i, 0, 0))
    return pl.pallas_call(
        body, name=name, grid=(KV, nQ, nK), in_specs=[q_spec, kv_spec, kv_spec, q_spec, lse_spec, q_spec], out_specs=q_spec,
        out_shape=_sds((T, nq * U), f32), scratch_shapes=[pltpu.VMEM((R, HEAD_DIM), f32), pltpu.VMEM((R, 1), f32)],
        compiler_params=_params(("arbitrary", "arbitrary", "arbitrary")))(qn, kn, vb, o, lse, do)


def att_bwd_dkv(cfg, qn, kn, vb, o, lse, do, nq, name):
    T, U = kn.shape
    KV = U // HEAD_DIM
    tq, tk, nQ, nK = _att_tiles(cfg, T)
    scale = HEAD_DIM ** -0.5
    c2 = scale * LOG2E
    R = nq * tq
    W = nq * HEAD_DIM

    def body(q_ref, k_ref, v_ref, o_ref, lse_ref, do_ref, dk_ref, dv_ref, dk_acc, dv_acc):
        iq = pl.program_id(2)

        @pl.when(iq == 0)
        def _():
            dk_acc[...] = jnp.zeros(dk_acc.shape, f32)
            dv_acc[...] = jnp.zeros(dv_acc.shape, f32)

        k, v = k_ref[...], v_ref[...]
        for h in range(nq):
            rows = pl.ds(h * tq, tq)
            cols = slice(h * HEAD_DIM, (h + 1) * HEAD_DIM)
            qh = q_ref[:, cols]
            doh = do_ref[:, cols]
            dl = jnp.sum(doh * o_ref[:, cols], axis=-1, keepdims=True)
            p = jnp.exp2(_dotf(qh, k, 1, 1) * c2 - lse_ref[rows, :] * LOG2E)
            do_bf = doh.astype(bf16)
            dv_acc[...] += _dotf(p.astype(bf16), do_bf, 0, 0)
            dp = _dotf(do_bf, v, 1, 1)
            ds = (p * (dp - dl) * scale).astype(bf16)
            dk_acc[...] += _dotf(ds, qh, 0, 0)

        @pl.when(iq == nQ - 1)
        def _():
            dk_ref[...] = dk_acc[...]
            dv_ref[...] = dv_acc[...]

    q_spec = pl.BlockSpec((tq, W), lambda h, k, i: (i + cfg.cT, h))
    kv_spec = pl.BlockSpec((tk, HEAD_DIM), lambda h, k, i: (k, h))
    lse_spec = pl.BlockSpec((None, None, R, 1), lambda h, k, i: (h, i, 0, 0))
    return pl.pallas_call(
        body, name=name, grid=(KV, nK, nQ), in_specs=[q_spec, kv_spec, kv_spec, q_spec, lse_spec, q_spec],
        out_specs=[kv_spec, kv_spec], out_shape=[_sds((T, U), f32), _sds((T, U), f32)],
        scratch_shapes=[pltpu.VMEM((tk, HEAD_DIM), f32), pltpu.VMEM((tk, HEAD_DIM), f32)],
        compiler_params=_params(("arbitrary", "arbitrary", "arbitrary")))(qn, kn, vb, o, lse, do)


def att_bwd(cfg, qn, kn, vb, o, lse, do, nq, name, tk_prefs=(4224, 2816, 1408, 768, 512, 256, 128)):
    T, U = kn.shape
    KV = U // HEAD_DIM
    tq = cfg.TR
    tk = _pick(T, tk_prefs)
    nQ, nK = (T - cfg.cT * cfg.TR) // tq, T // tk
    scale = HEAD_DIM ** -0.5
    c2 = scale * LOG2E
    R = nq * tq
    W = nq * HEAD_DIM

    def body(q_ref, k_ref, v_ref, o_ref, lse_ref, do_ref, dq_ref, dk_ref, dv_ref, acc, dl):
        iq, ik = pl.program_id(1), pl.program_id(2)

        @pl.when(jnp.logical_and(iq == 0, ik == 0))
        def _():
            dk_ref[...] = jnp.zeros(dk_ref.shape, f32)
            dv_ref[...] = jnp.zeros(dv_ref.shape, f32)

        @pl.when(ik == 0)
        def _():
            acc[...] = jnp.zeros(acc.shape, f32)
            dl[...] = jnp.sum(_stack_heads(do_ref[...] * o_ref[...], nq), axis=-1, keepdims=True)

        k, v = k_ref[...], v_ref[...]
        krows = pl.ds(pl.multiple_of(ik * tk, tk), tk)
        for h in range(nq):
            rows = pl.ds(h * tq, tq)
            cols = slice(h * HEAD_DIM, (h + 1) * HEAD_DIM)
            qh = q_ref[:, cols]
            p = jnp.exp2(_dotf(qh, k, 1, 1) * c2 - lse_ref[rows, :] * LOG2E)
            do_bf = do_ref[:, cols].astype(bf16)
            dp = _dotf(do_bf, v, 1, 1)
            ds = (p * (dp - dl[rows, :]) * scale).astype(bf16)
            acc[rows, :] += jnp.dot(ds, k, preferred_element_type=f32)
            dv_ref[krows, :] += _dotf(p.astype(bf16), do_bf, 0, 0)
            dk_ref[krows, :] += _dotf(ds, qh, 0, 0)

        @pl.when(ik == nK - 1)
        def _():
            dq_ref[...] = _unstack_heads(acc[...], nq)

    q_spec = pl.BlockSpec((tq, W), lambda h, i, k: (i + cfg.cT, h))
    kv_spec = pl.BlockSpec((tk, HEAD_DIM), lambda h, i, k: (k, h))
    lse_spec = pl.BlockSpec((None, None, R, 1), lambda h, i, k: (h, i, 0, 0))
    head_spec = pl.BlockSpec((T, HEAD_DIM), lambda h, i, k: (0, h))
    return pl.pallas_call(
        body, name=name, grid=(KV, nQ, nK), in_specs=[q_spec, kv_spec, kv_spec, q_spec, lse_spec, q_spec],
        out_specs=[q_spec, head_spec, head_spec], out_shape=[_sds((T, nq * U), f32), _sds((T, U), f32), _sds((T, U), f32)],
        scratch_shapes=[pltpu.VMEM((R, HEAD_DIM), f32), pltpu.VMEM((R, 1), f32)],
        compiler_params=_params(("arbitrary", "arbitrary", "arbitrary")))(qn, kn, vb, o, lse, do)


def od_mix(cfg, pooled, o, name):
    T, P = pooled.shape
    QW = o.shape[1]
    cT = cfg.cT

    def fn(i, p_ref, o_ref, m_ref):
        @pl.when(i < cT)
        def _():
            m_ref[...] = jnp.zeros(m_ref.shape, bf16)

        @pl.when(i >= cT)
        def _():
            m_ref[...] = jnp.concatenate([p_ref[...], o_ref[...].astype(bf16)], axis=1)

    return rowcall(cfg, fn, name, [("row", pooled, P, 0), ("row", o, QW, 0)], [("row", _sds((T, P + QW), bf16), P + QW, 0)])[0]


def ev_mix(cfg, lru, ret, name):
    T, W = lru.shape
    RV = ret.shape[1]

    def fn(i, a_ref, b_ref, m_ref):
        m_ref[...] = jnp.concatenate([a_ref[...], b_ref[...]], axis=1)

    return rowcall(cfg, fn, name, [("row", lru, W, 0), ("row", ret, RV, 0)], [("row", _sds((T, W + RV), bf16), W + RV, 0)])[0]


def ev_dz_pack(cfg, dgl, dr, dq, dk, dv, dol, name):
    T, W = dgl.shape
    RV = dol.shape[1]
    width = 2 * W + 4 * RV

    def fn(i, g_ref, r_ref, q_ref, k_ref, v_ref, o_ref, dz_ref):
        parts = [g_ref[...], r_ref[...], q_ref[0] + q_ref[1], k_ref[0] + k_ref[1], v_ref[0] + v_ref[1], o_ref[...]]
        dz_ref[...] = jnp.concatenate([p.astype(bf16) for p in parts], axis=1)

    return rowcall(cfg, fn, name, [("row", dgl, W, 0), ("row", dr, W, 0), ("drow", dq, RV, 0), ("drow", dk, RV, 0),
                                   ("drow", dv, RV, 0), ("row", dol, RV, 0)], [("row", _sds((T, width), bf16), width, 0)])[0]


def loss_fwd_bwd(cfg, xf, target, name):
    T, D = xf.shape
    TR, cT = cfg.TR, cfg.cT

    def body(x_ref, t_ref, sq_ref, dx_ref):
        i = pl.program_id(0)

        @pl.when(i == 0)
        def _():
            sq_ref[...] = jnp.zeros(sq_ref.shape, f32)

        @pl.when(i < cT)
        def _():
            dx_ref[...] = jnp.zeros(dx_ref.shape, f32)

        @pl.when(i >= cT)
        def _():
            diff = x_ref[...] - t_ref[...]
            sq_ref[...] += jnp.sum(diff * diff, axis=0, keepdims=True)
            dx_ref[...] = diff / D

    row = pl.BlockSpec((TR, D), lambda i: (i, 0))
    trow = pl.BlockSpec((TR, D), lambda i: (jnp.maximum(i - cT, 0), 0))
    return pl.pallas_call(body, name=name, grid=(cfg.nT,), in_specs=[row, trow],
                          out_specs=[pl.BlockSpec((1, D), lambda i: (0, 0)), row],
                          out_shape=[_sds((1, D), f32), _sds((T, D), f32)], compiler_params=_params(("arbitrary",)))(xf, target)


MOD_ROWS = 16


def mod_fwd(s16, mod_w, name):
    nL, D, C4 = mod_w.shape
    tc = _pick(C4, (512, 256, 128))

    def body(s_ref, w_ref, o_ref):
        o_ref[...] = jnp.dot(s_ref[...], w_ref[...], precision=lax.Precision.HIGHEST, preferred_element_type=f32)

    return pl.pallas_call(
        body, name=name, grid=(nL, C4 // tc),
        in_specs=[pl.BlockSpec((MOD_ROWS, D), lambda l, j: (0, 0)), pl.BlockSpec((None, D, tc), lambda l, j: (l, 0, j))],
        out_specs=pl.BlockSpec((None, MOD_ROWS, tc), lambda l, j: (l, 0, j)), out_shape=_sds((nL, MOD_ROWS, C4), f32),
        compiler_params=_params(("arbitrary", "arbitrary")))(s16, mod_w)


def mod_bwd(s16, dm16, mod_w, name):
    nL, D, C4 = mod_w.shape
    tc = _pick(C4, (512, 256, 128))
    half = MOD_ROWS // 2

    def body(s_ref, d_ref, w_ref, g_ref, dc_ref):
        first = jnp.logical_and(pl.program_id(0) == 0, pl.program_id(1) == 0)
        g_ref[...] = lax.dot_general(s_ref[...], d_ref[...], (((0,), (0,)), ((), ())), precision=lax.Precision.HIGHEST,
                                     preferred_element_type=f32)
        part = lax.dot_general(d_ref[...], w_ref[...], (((1,), (1,)), ((), ())), precision=lax.Precision.HIGHEST,
                               preferred_element_type=f32)
        _acc(dc_ref, jnp.sum(part[half:], axis=0, keepdims=True), first)

    return pl.pallas_call(
        body, name=name, grid=(nL, C4 // tc),
        in_specs=[pl.BlockSpec((MOD_ROWS, D), lambda l, j: (0, 0)), pl.BlockSpec((None, MOD_ROWS, tc), lambda l, j: (l, 0, j)),
                  pl.BlockSpec((None, D, tc), lambda l, j: (l, 0, j))],
        out_specs=[pl.BlockSpec((None, D, tc), lambda l, j: (l, 0, j)), pl.BlockSpec((1, D), lambda l, j: (0, 0))],
        out_shape=[_sds((nL, D, C4), f32), _sds((1, D), f32)],
        compiler_params=_params(("arbitrary", "arbitrary")))(s16, dm16, mod_w)


def _as2d(a):
    return a.reshape(-1, a.shape[-1])


ELEMENTWISE_VMEM = 24 * 1024 * 1024


def _tiles2d(shape, n_arrays):
    R, C = shape
    tc = _pick(C, (1536, 1408, 1024, 768, 512, 256, 128))
    fits = [t for t in (512, 256, 128, 64, 32, 16, 8) if R % t == 0 and t * tc * 4 * 2 * n_arrays <= ELEMENTWISE_VMEM]
    return (fits[0] if fits else R), tc


def cast_bf16(a, name):
    a2 = _as2d(a)
    tr, tc = _tiles2d(a2.shape, 2)

    def body(a_ref, o_ref):
        o_ref[...] = a_ref[...].astype(bf16)

    spec = pl.BlockSpec((tr, tc), lambda i, j: (i, j))
    out = pl.pallas_call(body, name=name, grid=(a2.shape[0] // tr, a2.shape[1] // tc), in_specs=[spec], out_specs=spec,
                         out_shape=_sds(a2.shape, bf16), compiler_params=_params(("arbitrary", "arbitrary")))(a2)
    return out.reshape(a.shape)


def sum_leading(a, name, *, into=None, full_shape=None, widx=()):
    n = a.shape[0]
    a3 = a.reshape(n, -1, a.shape[-1])
    tr, tc = _tiles2d(a3.shape[1:], n + 1)

    def body(a_ref, *rest):
        o_ref = rest[-1]
        tot = a_ref[0].astype(f32)
        for k in range(1, n):
            tot = tot + a_ref[k].astype(f32)
        o_ref[...] = tot

    grid = (a3.shape[1] // tr, a3.shape[2] // tc)
    in_specs = [pl.BlockSpec((n, tr, tc), lambda i, j: (0, i, j))]
    args = [a3]
    if not widx:
        out = pl.pallas_call(body, name=name, grid=grid, in_specs=in_specs,
                             out_specs=pl.BlockSpec((tr, tc), lambda i, j: (i, j)), out_shape=_sds(a3.shape[1:], f32),
                             compiler_params=_params(("arbitrary", "arbitrary")))(*args)
        return out.reshape(a.shape[1:])
    lead = tuple(full_shape[:len(widx)])
    flat = lead + tuple(a3.shape[1:])
    aliases = {}
    if into is not None:
        in_specs.append(pl.BlockSpec(memory_space=pl.ANY))
        args.append(into.reshape(flat))
        aliases = {1: 0}
    out = pl.pallas_call(body, name=name, grid=grid, in_specs=in_specs,
                         out_specs=pl.BlockSpec((None,) * len(widx) + (tr, tc), lambda i, j: tuple(widx) + (i, j)),
                         out_shape=_sds(flat, f32), input_output_aliases=aliases,
                         compiler_params=_params(("arbitrary", "arbitrary")))(*args)
    return out.reshape(full_shape)


def adamw(w, m, v, g_parts, name):
    w2, m2, v2 = _as2d(w), _as2d(m), _as2d(v)
    parts = [_as2d(p) for p in g_parts]
    tr, tc = _tiles2d(w2.shape, 7 + len(parts))
    npart = len(parts)

    def body(*refs):
        w_ref, m_ref, v_ref = refs[:3]
        p_refs = refs[3:3 + npart]
        g_ref, d_ref, nm_ref, nv_ref = refs[3 + npart:]
        g = p_refs[0][...]
        for p in p_refs[1:]:
            g = g + p[...]
        mn = ADAM_B1 * m_ref[...] + (1.0 - ADAM_B1) * g
        vn = ADAM_B2 * v_ref[...] + (1.0 - ADAM_B2) * jnp.square(g)
        m_hat = mn / (1.0 - ADAM_B1 ** ADAM_STEP)
        v_hat = vn / (1.0 - ADAM_B2 ** ADAM_STEP)
        g_ref[...] = g
        d_ref[...] = -ADAM_LR * (m_hat / (jnp.sqrt(v_hat) + ADAM_EPS) + ADAM_WD * w_ref[...])
        nm_ref[...] = mn
        nv_ref[...] = vn

    spec = pl.BlockSpec((tr, tc), lambda i, j: (i, j))
    outs = pl.pallas_call(body, name=name, grid=(w2.shape[0] // tr, w2.shape[1] // tc), in_specs=[spec] * (3 + npart),
                          out_specs=[spec] * 4, out_shape=[_sds(w2.shape, f32)] * 4,
                          compiler_params=_params(("arbitrary", "arbitrary")))(w2, m2, v2, *parts)
    return [o.reshape(w.shape) for o in outs]


def all_gather_small(a, name):
    R, C = a.shape

    def body(a_ref, out_ref, send_sems, recv_sems, local_sem):
        x, y, c = _coords()
        me = 4 * x + 2 * y + c
        mine = pltpu.make_async_copy(a_ref, out_ref.at[me], local_sem)
        mine.start()
        copies = []
        for k in range(1, N_DEV):
            kx, ky, kc = (k >> 2) & 1, (k >> 1) & 1, k & 1
            peer = (_flip(x, kx), _flip(y, ky), _flip(c, kc))
            cp = pltpu.make_async_remote_copy(src_ref=a_ref, dst_ref=out_ref.at[me], send_sem=send_sems.at[k - 1],
                                              recv_sem=recv_sems.at[k - 1], device_id=peer, device_id_type=MESH)
            cp.start()
            copies.append((cp, 4 * peer[0] + 2 * peer[1] + peer[2], peer))
        for k, (cp, pidx, peer) in enumerate(copies):
            pltpu.make_async_remote_copy(src_ref=a_ref, dst_ref=out_ref.at[pidx], send_sem=send_sems.at[k],
                                         recv_sem=recv_sems.at[k], device_id=peer, device_id_type=MESH).wait_recv()
        for cp, _, _ in copies:
            cp.wait_send()
        mine.wait()

    return pl.pallas_call(
        body, name=name, out_shape=_sds((N_DEV, R, C), f32),
        in_specs=[pl.BlockSpec(memory_space=pltpu.VMEM)], out_specs=pl.BlockSpec(memory_space=pltpu.VMEM),
        scratch_shapes=[pltpu.SemaphoreType.DMA((N_DEV - 1,)), pltpu.SemaphoreType.DMA((N_DEV - 1,)), pltpu.SemaphoreType.DMA],
        compiler_params=pltpu.CompilerParams(vmem_limit_bytes=VMEM_LIMIT))(a)


def all_reduce_small(a, name):
    R, C = a.shape
    rs = R // N_DEV

    def rows(d):
        return pl.ds(pl.multiple_of(d * rs, 8), rs)

    def body(a_ref, out_ref, buf, s1, r1, s2, r2):
        x, y, c = _coords()
        me = 4 * x + 2 * y + c
        peers = []
        for k in range(1, N_DEV):
            peer = (_flip(x, (k >> 2) & 1), _flip(y, (k >> 1) & 1), _flip(c, k & 1))
            peers.append((k - 1, peer, 4 * peer[0] + 2 * peer[1] + peer[2]))

        def scatter(j, peer, pidx, dst_slot):
            return pltpu.make_async_remote_copy(src_ref=a_ref.at[rows(pidx)], dst_ref=buf.at[dst_slot], send_sem=s1.at[j],
                                                recv_sem=r1.at[j], device_id=peer, device_id_type=MESH)

        def spread(j, peer, slot):
            return pltpu.make_async_remote_copy(src_ref=out_ref.at[rows(me)], dst_ref=out_ref.at[rows(slot)],
                                                send_sem=s2.at[j], recv_sem=r2.at[j], device_id=peer, device_id_type=MESH)

        sends = [scatter(j, peer, pidx, me) for j, peer, pidx in peers]
        for cp in sends:
            cp.start()
        buf[me] = a_ref[rows(me), :]
        for j, peer, pidx in peers:
            scatter(j, peer, pidx, pidx).wait_recv()
        total = buf[0]
        for d in range(1, N_DEV):
            total = total + buf[d]
        out_ref[rows(me), :] = total
        outs = [spread(j, peer, me) for j, peer, pidx in peers]
        for cp in outs:
            cp.start()
        for j, peer, pidx in peers:
            spread(j, peer, pidx).wait_recv()
        for cp in sends + outs:
            cp.wait_send()

    sem = pltpu.SemaphoreType.DMA((N_DEV - 1,))
    return pl.pallas_call(
        body, name=name, out_shape=_sds((R, C), f32), in_specs=[pl.BlockSpec(memory_space=pltpu.VMEM)],
        out_specs=pl.BlockSpec(memory_space=pltpu.VMEM),
        scratch_shapes=[pltpu.VMEM((N_DEV, rs, C), f32), sem, sem, sem, sem],
        compiler_params=pltpu.CompilerParams(vmem_limit_bytes=VMEM_LIMIT))(a)


def swap_with_sibling(parts, name):
    n = len(parts)

    def body(*refs):
        in_refs, out_refs = refs[:n], refs[n:2 * n]
        send_sems, recv_sems = refs[2 * n:]
        x, y, c = _coords()
        sends = []
        for w in range(n):
            cp = pltpu.make_async_remote_copy(src_ref=in_refs[w], dst_ref=out_refs[w], send_sem=send_sems.at[w],
                                              recv_sem=recv_sems.at[w], device_id=(x, y, 1 - c), device_id_type=MESH)
            cp.start()
            sends.append(cp)
        for cp in sends:
            cp.wait_recv()
        for cp in sends:
            cp.wait_send()

    hbm = pl.BlockSpec(memory_space=pl.ANY)
    return pl.pallas_call(
        body, name=name, out_shape=[_sds(a.shape, a.dtype) for a in parts], in_specs=[hbm] * n, out_specs=[hbm] * n,
        scratch_shapes=[pltpu.SemaphoreType.DMA((n,)), pltpu.SemaphoreType.DMA((n,))],
        )(*parts)


def even_fwd(cfg, x, p, tag, host=None):
    W, H = p["W"], p["H"]
    got = {}
    h = pre_fwd(cfg, x, p["g_pre"], p["shift"], p["scale"], tag + "_pre")
    z = _hosted(host, "in", got, lambda cm: matmul("v1", h, p["w_in"], comm=cm, name=tag + "_in"))
    u = conv_fwd(cfg, z, p["conv_w"], p["conv_b"], W, tag + "_conv")
    a, b = lru_coef_fwd(cfg, u, p["wa"], p["ba"], p["wx"], p["bx"], p["lam"], tag + "_coef")
    hh, hp = lru_scan(cfg, a, b, tag + "_scan")
    lru = lru_out_fwd(cfg, z, hh, W, tag + "_lruout")
    qcol = 2 * W // RET_DK
    o, st = ret_fwd(cfg, z, p["logit"], p["cos1"], p["sin1"], H, qcol, tag + "_ret")
    olcol = (2 * W + 3 * H * RET_DK) // (H * RET_DV)
    ret = ret_norm_fwd(cfg, o, z, p["gn"], H, olcol, tag + "_retnorm")
    mix = ev_mix(cfg, lru, ret, tag + "_mix")
    y = _hosted(host, "out", got, lambda cm: matmul("v2", mix, p["w_out"], comm=cm, name=tag + "_out"))
    xo = post_fwd(cfg, x, y, p["g_post"], p["gate"], 1.0, tag + "_post")
    return xo, (x, h, z, u, a, hh, hp, o, st, mix, y, olcol, qcol), got


def even_bwd(cfg, dX, saved, p, tag):
    x, h, z, u, a, hh, hp, o, st, mix, y, olcol, qcol = saved
    W, H = p["W"], p["H"]
    dy, dg_post, dgate = post_bwd(cfg, dX, y, p["g_post"], p["gate"], 1.0, tag + "_postb")
    dmix = matmul("v4", dy, p["w_out"], name=tag + "_dmix")
    g_out = matmul("v6", mix, dy, gshape=p["w_out"].shape, out_dtype=bf16, name=tag + "_gwout")
    dgl, dhs = lru_out_bwd(cfg, z, hh, dmix, W, tag + "_lruoutb")
    da, db = lru_scan_bwd(cfg, a, hp, dhs, tag + "_scanb")
    du, dwa, dba, dwx, dbx, dlam = lru_coef_bwd(cfg, u, da, db, p["wa"], p["ba"], p["wx"], p["bx"], p["lam"], tag + "_coefb")
    dr, dcw, dcb = conv_bwd(cfg, z, du, p["conv_w"], W, tag + "_convb")
    do, dol, dgn = ret_norm_bwd(cfg, o, z, p["gn"], dmix, H, olcol, W // (H * RET_DV), tag + "_retnormb")
    dq, dk, dv, dlg = ret_bwd(cfg, z, st, do, p["logit"], p["cos1"], p["sin1"], H, qcol, tag + "_retb")
    dz = ev_dz_pack(cfg, dgl, dr, dq, dk, dv, dol, tag + "_dz")
    g_in, r_out = matmul("v5", h, dz, gshape=p["w_in"].shape, out_dtype=bf16, comm=Comm("scatter", [(g_out, ())]),
                         name=tag + "_gwin")
    dh, r_in = matmul("v3", dz, p["w_in"], comm=Comm("scatter", [(g_in, ())]), name=tag + "_dh")
    dX, dg_pre, dshift, dscale = pre_bwd(cfg, x, p["g_pre"], p["shift"], p["scale"], dh, dX, tag + "_preb")
    pg = dict(g_pre=dg_pre, g_post=dg_post, shift=dshift, scale=dscale, gate=dgate, conv_w=dcw, conv_b=dcb, wa=dwa,
              ba=dba, wx=dwx, bx=dbx, lam=dlam, logit=dlg, gn=dgn)
    return dX, pg, dict(w_in=r_in[0], w_out=r_out[0])


def odd_fwd(cfg, x, p, tag, host=None):
    nq = p["nq"]
    got = {}
    h = pre_fwd(cfg, x, p["g_pre"], p["shift"], p["scale"], tag + "_pre")
    z = _hosted(host, "in", got, lambda cm: matmul("v1", h, p["w_in"], comm=cm, name=tag + "_in"))
    pooled = pool_fwd(cfg, z, p["pool_w"], p["pool_scale"], tag + "_pool")
    qn, kn, vb = att_prep(cfg, z, p["qg"], p["kg"], p["cosf"], p["sinf"], nq, tag + "_prep")
    o, lse = att_fwd(cfg, qn, kn, vb, nq, tag + "_att")
    mix = od_mix(cfg, pooled, o, tag + "_mix")
    y = matmul("v2", mix, p["w_out"], name=tag + "_out")
    xo = post_fwd(cfg, x, y, p["g_post"], p["gate"], 1.0, tag + "_post")
    return xo, (x, h, z, qn, kn, vb, o, lse, mix, y), got


def odd_bwd(cfg, dX, saved, p, tag):
    x, h, z, qn, kn, vb, o, lse, mix, y = saved
    nq = p["nq"]
    U = kn.shape[1]
    dy, dg_post, dgate = post_bwd(cfg, dX, y, p["g_post"], p["gate"], 1.0, tag + "_postb")
    dmix = matmul("v4", dy, p["w_out"], name=tag + "_dmix")
    g_out = matmul("v6", mix, dy, gshape=p["w_out"].shape, out_dtype=bf16, name=tag + "_gwout")
    dm, dmn, dpw, dps = pool_bwd_a(cfg, z, dmix, p["pool_w"], p["pool_scale"], tag + "_poolb")
    dxp = pool_bwd_b(cfg, dm, dmn, tag + "_poolb2")
    do = dmix[:, U:]
    dqn, dkn, dvb = att_bwd(cfg, qn, kn, vb, o, lse, do, nq, tag + "_attb")
    dz, dqg, dkg = att_prep_bwd(cfg, z, p["qg"], p["kg"], p["cosf"], p["sinf"], dqn, dkn, dvb, dxp, nq, tag + "_prepb")
    g_in, r_out = matmul("v5", h, dz, gshape=p["w_in"].shape, out_dtype=bf16, comm=Comm("scatter", [(g_out, ())]),
                         name=tag + "_gwin")
    dh, r_in = matmul("v3", dz, p["w_in"], comm=Comm("scatter", [(g_in, ())]), name=tag + "_dh")
    dX, dg_pre, dshift, dscale = pre_bwd(cfg, x, p["g_pre"], p["shift"], p["scale"], dh, dX, tag + "_preb")
    pg = dict(g_pre=dg_pre, g_post=dg_post, shift=dshift, scale=dscale, gate=dgate, pool_w=dpw, pool_scale=dps, qg=dqg, kg=dkg)
    return dX, pg, dict(w_in=r_in[0], w_out=r_out[0])


WEIGHT_NAMES = ("c_ctx", "mod_w", "mod_b", "norm_pre", "norm_post", "ffn_gate", "ffn_up", "ffn_down", "ev_w_in", "ev_w_out",
                "lru_conv_w", "lru_conv_b", "lru_wa", "lru_ba", "lru_wx", "lru_bx", "lru_lambda", "ret_decay_logit", "ret_gn",
                "od_w_in", "od_w_out", "pool_w", "pool_scale", "q_norm", "k_norm")
BIG = ("ffn_gate", "ffn_up", "ffn_down", "ev_w_in", "ev_w_out", "od_w_in", "od_w_out")
SMALL_SHARDED = ("norm_pre", "norm_post", "lru_conv_w", "lru_ba", "lru_bx", "lru_lambda", "pool_scale")
SMALL_REPL = ("mod_b", "lru_conv_b", "lru_wa", "lru_wx", "ret_decay_logit", "ret_gn", "pool_w", "q_norm", "k_norm")
LANES = 128


PACK_ROWS = 512


def _rows_of(n):
    return -(-n // (8 * LANES)) * 8


def _pack(arrs):
    rows = []
    for a in arrs:
        flat = a.reshape(-1)
        rows.append(jnp.pad(flat, (0, _rows_of(flat.shape[0]) * LANES - flat.shape[0])).reshape(-1, LANES))
    total = sum(r.shape[0] for r in rows)
    rows.append(jnp.zeros(((-total) % PACK_ROWS, LANES), f32))
    return jnp.concatenate(rows), None


def _unpack(packed, shapes, lead=()):
    out, pos = [], 0
    for shp in shapes:
        n = math.prod(shp)
        r = _rows_of(n)
        piece = packed[..., pos:pos + r, :].reshape(lead + (r * LANES,))
        out.append(piece[..., :n].reshape(lead + tuple(shp)))
        pos += r
    return out


def _unshard(g):
    return jnp.moveaxis(g, 0, -2).reshape(g.shape[1:-1] + (g.shape[0] * g.shape[-1],))


def _rope_tables(S, Lc):
    n_r = RET_DK // 2
    f_r = RET_THETA ** (-jnp.arange(n_r, dtype=f32) / n_r)
    ang1 = jnp.arange(S, dtype=f32)[:, None] * f_r
    rows = S // GRID_W
    row = jnp.repeat(jnp.arange(rows, dtype=f32), GRID_W)
    col = jnp.tile(jnp.arange(GRID_W, dtype=f32), rows)
    n_ax = HEAD_DIM // 4
    f_ax = ROPE_THETA ** (-jnp.arange(n_ax, dtype=f32) / n_ax)
    ang2 = jnp.concatenate([row[:, None] * f_ax, col[:, None] * f_ax], axis=-1)
    cos2, sin2 = jnp.cos(ang2), jnp.sin(ang2)
    ones = lambda n: jnp.ones((Lc, n), f32)
    zeros = lambda n: jnp.zeros((Lc, n), f32)
    cos1 = jnp.concatenate([ones(n_r), jnp.cos(ang1)])
    sin1 = jnp.concatenate([zeros(n_r), jnp.sin(ang1)])
    cosf = jnp.concatenate([ones(HEAD_DIM), jnp.concatenate([cos2, cos2], axis=1)])
    sinf = jnp.concatenate([zeros(HEAD_DIM), jnp.concatenate([-sin2, sin2], axis=1)])
    return cos1, sin1, cosf, sinf


def kernel(x, c, ctx, c_ctx, mod_w, mod_b, norm_pre, norm_post, ffn_gate, ffn_up, ffn_down, ev_w_in, ev_w_out, lru_conv_w, lru_conv_b, lru_wa, lru_ba, lru_wx, lru_bx, lru_lambda, ret_decay_logit, ret_gn, od_w_in, od_w_out, pool_w, pool_scale, q_norm, k_norm, loss_target, m_c_ctx, m_mod_w, m_mod_b, m_norm_pre, m_norm_post, m_ffn_gate, m_ffn_up, m_ffn_down, m_ev_w_in, m_ev_w_out, m_lru_conv_w, m_lru_conv_b, m_lru_wa, m_lru_ba, m_lru_wx, m_lru_bx, m_lru_lambda, m_ret_decay_logit, m_ret_gn, m_od_w_in, m_od_w_out, m_pool_w, m_pool_scale, m_q_norm, m_k_norm, v_c_ctx, v_mod_w, v_mod_b, v_norm_pre, v_norm_post, v_ffn_gate, v_ffn_up, v_ffn_down, v_ev_w_in, v_ev_w_out, v_lru_conv_w, v_lru_conv_b, v_lru_wa, v_lru_ba, v_lru_wx, v_lru_bx, v_lru_lambda, v_ret_decay_logit, v_ret_gn, v_od_w_in, v_od_w_out, v_pool_w, v_pool_scale, v_q_norm, v_k_norm):
    wts = dict(c_ctx=c_ctx, mod_w=mod_w, mod_b=mod_b, norm_pre=norm_pre, norm_post=norm_post, ffn_gate=ffn_gate, ffn_up=ffn_up,
               ffn_down=ffn_down, ev_w_in=ev_w_in, ev_w_out=ev_w_out, lru_conv_w=lru_conv_w, lru_conv_b=lru_conv_b,
               lru_wa=lru_wa, lru_ba=lru_ba, lru_wx=lru_wx, lru_bx=lru_bx, lru_lambda=lru_lambda,
               ret_decay_logit=ret_decay_logit, ret_gn=ret_gn, od_w_in=od_w_in, od_w_out=od_w_out, pool_w=pool_w,
               pool_scale=pool_scale, q_norm=q_norm, k_norm=k_norm)
    mom_m = dict(zip(WEIGHT_NAMES, (m_c_ctx, m_mod_w, m_mod_b, m_norm_pre, m_norm_post, m_ffn_gate, m_ffn_up, m_ffn_down,
                                    m_ev_w_in, m_ev_w_out, m_lru_conv_w, m_lru_conv_b, m_lru_wa, m_lru_ba, m_lru_wx, m_lru_bx,
                                    m_lru_lambda, m_ret_decay_logit, m_ret_gn, m_od_w_in, m_od_w_out, m_pool_w, m_pool_scale,
                                    m_q_norm, m_k_norm)))
    mom_v = dict(zip(WEIGHT_NAMES, (v_c_ctx, v_mod_w, v_mod_b, v_norm_pre, v_norm_post, v_ffn_gate, v_ffn_up, v_ffn_down,
                                    v_ev_w_in, v_ev_w_out, v_lru_conv_w, v_lru_conv_b, v_lru_wa, v_lru_ba, v_lru_wx, v_lru_bx,
                                    v_lru_lambda, v_ret_decay_logit, v_ret_gn, v_od_w_in, v_od_w_out, v_pool_w, v_pool_scale,
                                    v_q_norm, v_k_norm)))

    _, S, D = x.shape
    Lc = ctx.shape[1]
    T = Lc + S
    TR = 256 if (Lc % 256 == 0 and S % 256 == 0) else 128
    assert Lc % TR == 0 and S % TR == 0 and TR % RET_CHUNK == 0
    cfg = RowCfg(TR, T // TR, Lc // TR)
    W = lru_conv_b.shape[-1]
    H = ret_decay_logit.shape[-1]
    U = POOL_GROUP * len(POOL_WINDOWS)
    nq = (N_CHIPS * od_w_in.shape[-1]) // U - 3
    assert W % (H * RET_DV) == 0 and (2 * W) % (H * RET_DK) == 0
    nL = mod_w.shape[0]
    C4 = mod_w.shape[-1]
    assert nL == 2, "two layers: an even mixer then an odd one"

    xi, yi, ci = _coords()
    chip = 2 * xi + yi
    me = 4 * xi + 2 * yi + ci

    sc = jax.nn.silu(c)
    small_in, _ = _pack([sc] + [wts[n] for n in SMALL_SHARDED])
    g1 = all_gather_small(small_in, "gather_small_fwd")
    parts = _unpack(g1, [sc.shape] + [wts[n].shape for n in SMALL_SHARDED], lead=(N_DEV,))
    sc_all = parts[0][:, 0]
    full = {n: _unshard(parts[1 + i][0::2]) for i, n in enumerate(SMALL_SHARDED)}
    for n in SMALL_REPL + ("c_ctx",):
        full[n] = wts[n]

    scc = jax.nn.silu(c_ctx)[None]
    pad_rows = MOD_ROWS - N_DEV - 1
    s16 = jnp.concatenate([sc_all, scc, jnp.zeros((pad_rows, D), f32)])
    modp = mod_fwd(s16, mod_w, "mod_fwd")
    g2 = all_gather_small(modp.reshape(-1, LANES), "gather_mod")
    mod_all = g2.reshape(N_DEV, nL, MOD_ROWS, C4)[0::2]
    mod_all = jnp.moveaxis(mod_all, 0, 2).reshape(nL, MOD_ROWS, N_CHIPS * C4) + mod_b[:, None, :]
    mod_l = lax.dynamic_index_in_dim(mod_all, me, axis=1, keepdims=False).reshape(nL, 3, 3, D)
    mod_c = mod_all[:, N_DEV].reshape(nL, 3, 3, D)

    def mod_of(li, s, kind, ctx_live=True):
        cpart = mod_c[li, s, kind] if ctx_live else jnp.zeros((D,), f32)
        return jnp.stack([cpart, mod_l[li, s, kind]])[:, None, :]

    packed = {n: cast_bf16(wts[n], "cast_" + n) for n in BIG}
    ffn_units = [(0, 0), (0, 1), (1, 0), (1, 1)]

    def G(*pieces):
        return Comm("gather", [(packed[n], idx) for n, idx in pieces])

    cos1, sin1, cosf, sinf = _rope_tables(S, Lc)

    def sub_params(li, s, ctx_live=True, gate_ctx_live=True):
        return dict(g_pre=full["norm_pre"][li, s][None], g_post=full["norm_post"][li, s][None],
                    shift=mod_of(li, s, 0, ctx_live), scale=mod_of(li, s, 1, ctx_live),
                    gate=mod_of(li, s, 2, ctx_live and gate_ctx_live))

    X0 = jnp.concatenate([ctx[0], x[0]], axis=0)
    wg00, wu00 = gather_two_level(G(("ffn_gate", (0, 0)), ("ffn_up", (0, 0))), "gather_first")
    p00 = sub_params(0, 0)
    p00.update(wg=wg00, wu=wu00)
    h00 = {"gateup": G(("ffn_down", (0, 0)), ("ev_w_in", (0,))), "down": G(("ev_w_out", (0,)), ("ffn_gate", (0, 1)))}
    h = pre_fwd(cfg, X0, p00["g_pre"], p00["shift"], p00["scale"], "l0f0_pre")
    (a00, b00, u00), (wd00, ev_in) = ffn_gateup(cfg, h, wg00, wu00, "l0f0_gateup", comm=h00["gateup"])
    p00.update(wd=wd00)
    y00, (ev_out, wg01) = matmul("v2", u00, wd00, comm=h00["down"], name="l0f0_down")
    X1 = post_fwd(cfg, X0, y00, p00["g_post"], p00["gate"], FFN_STEP, "l0f0_post")
    s00 = (X0, h, a00, b00, u00, y00)

    p01 = sub_params(0, 1)
    p01.update(W=W, H=H, conv_w=full["lru_conv_w"][0], conv_b=full["lru_conv_b"], wa=full["lru_wa"][0],
               ba=full["lru_ba"][0][:, None, :], wx=full["lru_wx"][0], bx=full["lru_bx"][0][:, None, :],
               lam=full["lru_lambda"][0][:, None, :], logit=full["ret_decay_logit"][0][:, :, None, None],
               gn=full["ret_gn"], cos1=cos1, sin1=sin1, w_in=ev_in, w_out=ev_out)
    X2, s01, got = even_fwd(cfg, X1, p01, "l0mix", host={"in": G(("ffn_up", (0, 1))), "out": G(("ffn_down", (0, 1)))})
    p02 = sub_params(0, 2)
    p02.update(wg=wg01, wu=got["in"][0], wd=got["out"][0])
    X3, s02, got = ffn_fwd(cfg, X2, p02, "l0f1", host={"gateup": G(("ffn_gate", (1, 0)), ("ffn_up", (1, 0))),
                                                       "down": G(("ffn_down", (1, 0)))})
    p10 = sub_params(1, 0)
    p10.update(wg=got["gateup"][0], wu=got["gateup"][1], wd=got["down"][0])
    X4, s10, got = ffn_fwd(cfg, X3, p10, "l1f0", host={"gateup": G(("od_w_in", (0,)), ("od_w_out", (0,)), ("ffn_gate", (1, 1))),
                                                       "down": G(("ffn_up", (1, 1)))})
    p11 = sub_params(1, 1, gate_ctx_live=False)
    p11.update(nq=nq, pool_w=full["pool_w"][0], pool_scale=full["pool_scale"], qg=full["q_norm"], kg=full["k_norm"],
               cosf=cosf, sinf=sinf, w_in=got["gateup"][0], w_out=got["gateup"][1])
    p12 = sub_params(1, 2, ctx_live=False)
    p12.update(wg=got["gateup"][2], wu=got["down"][0])
    X5, s11, got = odd_fwd(cfg, X4, p11, "l1mix", host={"in": G(("ffn_down", (1, 1)))})
    p12.update(wd=got["in"][0])
    X6, s12, _ = ffn_fwd(cfg, X5, p12, "l1f1")
    sq, dX = loss_fwd_bwd(cfg, X6, loss_target[0], "loss")
    loss = lax.psum(0.5 * jnp.sum(sq) / D, ("x", "y", "c"))

    recv_ffn = {}
    dX, g12, recv_ffn[(1, 1)] = ffn_bwd(cfg, dX, s12, p12, "l1f1")
    dX, g11, recv_od = odd_bwd(cfg, dX, s11, p11, "l1mix")
    dX, g10, recv_ffn[(1, 0)] = ffn_bwd(cfg, dX, s10, p10, "l1f0")
    dX, g02, recv_ffn[(0, 1)] = ffn_bwd(cfg, dX, s02, p02, "l0f1")
    dX, g01, recv_ev = even_bwd(cfg, dX, s01, p01, "l0mix")
    dX, g00, recv_ffn[(0, 0)] = ffn_bwd(cfg, dX, s00, p00, "l0f0")
    grad_x = dX[Lc:][None]

    subs = [[g00, g01, g02], [g10, g11, g12]]
    zero_d = jnp.zeros((D,), f32)

    def dmod(group, live):
        rows = []
        for li in range(nL):
            for s in range(3):
                for kind, key in enumerate(("shift", "scale", "gate")):
                    rows.append(subs[li][s][key][group, 0] if live(li, s, kind) else zero_d)
        return jnp.stack(rows).reshape(nL, 9 * D)

    dmod_l = dmod(1, lambda li, s, kind: True)
    dmod_c = dmod(0, lambda li, s, kind: not (li == 1 and (s == 2 or (s == 1 and kind == 2))))

    dm_in, _ = _pack([dmod_l, dmod_c])
    g3 = all_gather_small(dm_in, "gather_dmod")
    dl_all, dc_all = _unpack(g3, [dmod_l.shape, dmod_c.shape], lead=(N_DEV,))
    dm16 = jnp.moveaxis(jnp.concatenate([dl_all, dc_all], axis=0), 0, 1)
    dm16 = lax.dynamic_slice_in_dim(dm16, chip * C4, C4, axis=2)
    s16b = jnp.concatenate([sc_all, jnp.broadcast_to(scc, (N_DEV, D))])
    g_mod_w, dscc_part = mod_bwd(s16b, dm16, mod_w, "mod_bwd")

    norm_pre_g = jnp.stack([jnp.concatenate([subs[li][s]["g_pre"] for s in range(3)]) for li in range(nL)])
    norm_post_g = jnp.stack([jnp.concatenate([subs[li][s]["g_post"] for s in range(3)]) for li in range(nL)])
    small_g = dict(norm_pre=norm_pre_g, norm_post=norm_post_g, lru_conv_w=g01["conv_w"][None], lru_ba=g01["ba"][:, 0][None],
                   lru_bx=g01["bx"][:, 0][None], lru_lambda=g01["lam"][:, 0][None], pool_scale=g11["pool_scale"],
                   mod_b=dmod_l + dmod_c, lru_conv_b=g01["conv_b"], lru_wa=g01["wa"][None], lru_wx=g01["wx"][None],
                   ret_decay_logit=g01["logit"][:, :, 0, 0][None], ret_gn=g01["gn"], pool_w=g11["pool_w"][None],
                   q_norm=g11["qg"], k_norm=g11["kg"])
    names = SMALL_SHARDED + SMALL_REPL
    sg_in, _ = _pack([small_g[n] for n in names] + [dscc_part])
    tot = all_reduce_small(sg_in, "gather_small_grads")
    tot_parts = _unpack(tot, [small_g[n].shape for n in names] + [dscc_part.shape])
    dscc = 0.5 * tot_parts[-1][0]
    tot_parts = tot_parts[:-1]
    _, silu_vjp = jax.vjp(jax.nn.silu, c_ctx)
    grads = {"c_ctx": silu_vjp(dscc)[0]}
    for n, g in zip(names, tot_parts):
        if n in SMALL_SHARDED:
            k = wts[n].shape[-1]
            g = lax.dynamic_slice_in_dim(g, chip * k, k, axis=g.ndim - 1)
        grads[n] = g.reshape(wts[n].shape)

    partial = {}
    for n, key in (("ffn_gate", "wg"), ("ffn_up", "wu"), ("ffn_down", "wd")):
        acc = None
        for u in ffn_units:
            acc = sum_leading(recv_ffn[u][key], "sum_%s_%d%d" % (n, u[0], u[1]), into=acc, full_shape=wts[n].shape, widx=u)
        partial[n] = acc
    for n, r in (("ev_w_in", recv_ev["w_in"]), ("ev_w_out", recv_ev["w_out"]), ("od_w_in", recv_od["w_in"]),
                 ("od_w_out", recv_od["w_out"])):
        partial[n] = sum_leading(r, "sum_" + n).reshape(wts[n].shape)
    partial = [partial[n] for n in BIG]
    other = swap_with_sibling(partial, "swap_partials")

    delta, new_m, new_v = {}, {}, {}
    for n, pa, pb in zip(BIG, partial, other):
        grads[n], delta[n], new_m[n], new_v[n] = adamw(wts[n], mom_m[n], mom_v[n], [pa, pb], "adamw_" + n)
    grads["mod_w"], delta["mod_w"], new_m["mod_w"], new_v["mod_w"] = adamw(mod_w, m_mod_w, v_mod_w, [g_mod_w], "adamw_mod_w")
    snames = [n for n in WEIGHT_NAMES if n not in BIG and n != "mod_w"]
    pk = lambda d: _pack([d[n] for n in snames])[0]
    sres = adamw(pk(wts), pk(mom_m), pk(mom_v), [pk(grads)], "adamw_small")
    for res, dst in zip(sres[1:], (delta, new_m, new_v)):
        for n, a in zip(snames, _unpack(res, [wts[n].shape for n in snames])):
            dst[n] = a

    return (loss, grad_x, *[grads[n] for n in WEIGHT_NAMES], *[delta[n] for n in WEIGHT_NAMES],
            *[new_m[n] for n in WEIGHT_NAMES], *[new_v[n] for n in WEIGHT_NAMES])
```
